```python
import jax, jax.numpy as jnp
from jax import lax
import numpy as np

D_MODEL = 4096
BATCH = 2
SEQ = 4096
DEPTH = 2

f32 = jnp.float32
D_MIX = D_MODEL
D_GRP = D_MIX // 4
HEAD_DIM = 128
NORM_EPS = 1e-6
NSA_HEADS = D_GRP // HEAD_DIM
NSA_KV_HEADS = 2
NSA_GROUP = NSA_HEADS // NSA_KV_HEADS
NSA_KV_W = NSA_KV_HEADS * HEAD_DIM
CMP_LEN = 32
CMP_STRIDE = 16
SEL_LEN = 64
N_SEL = 16
N_LOCAL_FORCED = 2
FORCED_SCORE = 1e4
WINDOW = 512
SEL_QBLK = 64
WIN_QBLK = 128
ROPE_THETA = 500000.0
ROPE_DIMS = HEAD_DIM // 4
RET_HEADS = D_GRP // HEAD_DIM
RET_CHUNK = 128
RET_THETA = 10000.0
DN_HEADS = D_GRP // HEAD_DIM
DN_CHUNK = 64
CONV_WIDTH = 4
RW_HEAD_DIM = 64
RW_HEADS = D_GRP // RW_HEAD_DIM
RW_DECAY_LORA = 64
RW_AAA_LORA = 64
RW_GATE_LORA = 160
RW_LN_EPS = 64e-5
N_GROUPS = 4
EXPERTS_PER_GROUP = 8
N_EXPERTS = N_GROUPS * EXPERTS_PER_GROUP
TOP_K = 2
D_EXPERT = 768
ROW_BLOCK = 256
PLE_DIM = 256
NSA_COLS = D_GRP + 6 * NSA_KV_W + 3 * NSA_HEADS
RET_COLS = 4 * D_GRP
DN_COLS = 4 * D_GRP + 2 * DN_HEADS
RW_COLS = 3 * D_GRP + RW_DECAY_LORA + RW_AAA_LORA + RW_GATE_LORA
P_TOTAL = NSA_COLS + RET_COLS + DN_COLS + RW_COLS

kernel_name = "hybrid_nsa_retnet_gdn_rwkv7_hmoe"


def split_cols(x, sizes):
    return jnp.split(x, np.cumsum(sizes)[:-1].tolist(), axis=-1)


def rms_norm(x, g, eps=NORM_EPS):
    xf = x.astype(f32)
    y = xf * lax.rsqrt(jnp.mean(xf * xf, -1, keepdims=True) + eps)
    return (y * g.astype(f32)).astype(x.dtype)


def l2norm(x, eps=1e-6):
    return x * lax.rsqrt(jnp.sum(x * x, -1, keepdims=True) + eps)


def head_layer_norm(x, eps):
    mu = jnp.mean(x, -1, keepdims=True)
    return (x - mu) * lax.rsqrt(jnp.var(x, -1, keepdims=True) + eps)


def masked_softmax(s, mask):
    s = jnp.where(mask, s.astype(f32), -jnp.inf)
    m = jnp.max(s, -1, keepdims=True)
    m = jnp.where(jnp.isfinite(m), m, 0.0)
    e = jnp.where(mask, jnp.exp(s - m), 0.0)
    return e / jnp.maximum(jnp.sum(e, -1, keepdims=True), jnp.finfo(f32).tiny)


def rotary(x, freqs, rot_dims):
    S = x.shape[1]
    half = rot_dims // 2
    ang = jnp.arange(S, dtype=f32)[:, None] * freqs[None, :]
    ang = ang.reshape((S,) + (1,) * (x.ndim - 3) + (half,))
    cos, sin = jnp.cos(ang), jnp.sin(ang)
    x1 = x[..., :half].astype(f32)
    x2 = x[..., half:rot_dims].astype(f32)
    out = jnp.concatenate([x1 * cos - x2 * sin, x2 * cos + x1 * sin, x[..., rot_dims:].astype(f32)], -1)
    return out.astype(x.dtype)


def cmp_sel_overlap(S):
    n_cmp = (S - CMP_LEN) // CMP_STRIDE + 1
    c0 = np.arange(n_cmp)[:, None] * CMP_STRIDE
    s0 = np.arange(S // SEL_LEN)[None, :] * SEL_LEN
    ov = np.clip(np.minimum(c0 + CMP_LEN, s0 + SEL_LEN) - np.maximum(c0, s0), 0, None) / CMP_LEN
    return jnp.asarray(ov, f32)


def nsa_mixer(q, kc, vc, ks, vs, kw, vw, gate_logit, ck_w1, ck_w2, ck_pe, cv_w1, cv_w2, cv_pe):
    B, S, _ = q.shape
    G, R, Dh = NSA_KV_HEADS, NSA_GROUP, HEAD_DIM
    scale = Dh ** -0.5
    half = ROPE_DIMS // 2
    freqs = ROPE_THETA ** (-jnp.arange(half, dtype=f32) / half)
    q = rotary(q.reshape(B, S, G, R, Dh), freqs, ROPE_DIMS)
    kc, vc, ks, vs, kw, vw = [t.reshape(B, S, G, Dh) for t in (kc, vc, ks, vs, kw, vw)]
    kc, ks, kw = [rotary(t, freqs, ROPE_DIMS) for t in (kc, ks, kw)]
    pos = jnp.arange(S)

    n_cmp = (S - CMP_LEN) // CMP_STRIDE + 1
    blk_tok = np.arange(n_cmp)[:, None] * CMP_STRIDE + np.arange(CMP_LEN)[None, :]

    def compress(t, w1, w2, pe):
        blk = t[:, blk_tok] + pe[:, None, :]
        blk = blk.transpose(0, 3, 1, 2, 4).reshape(B, G, n_cmp, CMP_LEN * Dh)
        return jax.nn.silu(blk @ w1) @ w2

    k_cmp = compress(kc, ck_w1, ck_w2, ck_pe)
    v_cmp = compress(vc, cv_w1, cv_w2, cv_pe)
    s_cmp = jnp.einsum('bsgrd,bgnd->bgrsn', q, k_cmp) * scale
    cmp_mask = jnp.asarray(blk_tok[:, -1])[None, :] <= pos[:, None]
    p_cmp = masked_softmax(s_cmp, cmp_mask)
    o_cmp = jnp.einsum('bgrsn,bgnd->bsgrd', p_cmp, v_cmp.astype(f32))

    n_sel_blk = S // SEL_LEN
    k_top = min(N_SEL, n_sel_blk)
    imp = jnp.einsum('bgrsn,nj->bgsj', p_cmp, cmp_sel_overlap(S))
    blk = jnp.arange(n_sel_blk)[None, :]
    cur = (pos // SEL_LEN)[:, None]
    visible = blk <= cur
    forced = (blk == 0) | (visible & (blk > cur - N_LOCAL_FORCED))
    imp = jnp.where(forced, FORCED_SCORE, jnp.where(visible, imp, -1.0))
    sel_idx = lax.top_k(imp, k_top)[1]

    ks_blk = ks.reshape(B, n_sel_blk, SEL_LEN, G, Dh).transpose(0, 3, 1, 2, 4)
    vs_blk = vs.reshape(B, n_sel_blk, SEL_LEN, G, Dh).transpose(0, 3, 1, 2, 4)
    n_qb = S // SEL_QBLK
    q_b = q.reshape(B, n_qb, SEL_QBLK, G, R, Dh).transpose(1, 0, 2, 3, 4, 5)
    idx_b = sel_idx.reshape(B, G, n_qb, SEL_QBLK, k_top).transpose(2, 0, 1, 3, 4)
    qpos_b = pos.reshape(n_qb, SEL_QBLK)
    b_ix = jnp.arange(B)[:, None, None, None]
    g_ix = jnp.arange(G)[None, :, None, None]

    def sel_block(args):
        qb, ib, qp = args
        kb = ks_blk[b_ix, g_ix, ib].reshape(B, G, SEL_QBLK, k_top * SEL_LEN, Dh)
        vb = vs_blk[b_ix, g_ix, ib].reshape(B, G, SEL_QBLK, k_top * SEL_LEN, Dh)
        tok = (ib[..., None] * SEL_LEN + jnp.arange(SEL_LEN)).reshape(B, G, SEL_QBLK, k_top * SEL_LEN)
        s = jnp.einsum('bqgrd,bgqkd->bgrqk', qb, kb) * scale
        mask = tok[:, :, None] <= qp[None, None, None, :, None]
        pr = masked_softmax(s, mask)
        return jnp.einsum('bgrqk,bgqkd->bqgrd', pr, vb.astype(f32))

    o_sel = lax.map(sel_block, (q_b, idx_b, qpos_b))
    o_sel = o_sel.transpose(1, 0, 2, 3, 4, 5).reshape(B, S, G, R, Dh)

    n_wb = S // WIN_QBLK
    span = WINDOW + WIN_QBLK
    key_idx = np.arange(n_wb)[:, None] * WIN_QBLK + np.arange(span)[None, :]
    pad = ((0, 0), (WINDOW, 0), (0, 0), (0, 0))
    kb = jnp.pad(kw, pad)[:, key_idx]
    vb = jnp.pad(vw, pad)[:, key_idx]
    qb = q.reshape(B, n_wb, WIN_QBLK, G, R, Dh)
    s_win = jnp.einsum('bnqgrd,bnkgd->bgrnqk', qb, kb) * scale
    kpos = jnp.asarray(key_idx - WINDOW)[:, None, :]
    qpos = pos.reshape(n_wb, WIN_QBLK)[:, :, None]
    win_mask = (kpos >= 0) & (kpos <= qpos) & (kpos > qpos - WINDOW)
    p_win = masked_softmax(s_win, win_mask)
    o_win = jnp.einsum('bgrnqk,bnkgd->bnqgrd', p_win, vb.astype(f32)).reshape(B, S, G, R, Dh)

    gate = jax.nn.sigmoid(gate_logit.astype(f32)).reshape(B, S, G, R, 3)
    o = gate[..., 0:1] * o_cmp + gate[..., 1:2] * o_sel + gate[..., 2:3] * o_win
    return o.reshape(B, S, D_GRP)


def retention_mixer(q, k, v, g):
    B, S, _ = q.shape
    H, Dh, C = RET_HEADS, HEAD_DIM, RET_CHUNK
    n_c = S // C
    freqs = RET_THETA ** (-jnp.linspace(0.0, 1.0, Dh // 2, dtype=f32))
    heads = lambda t: t.reshape(B, S, H, Dh)
    q = rotary(heads(q), freqs, Dh).astype(f32)
    k = rotary(heads(k), freqs, Dh).astype(f32) * Dh ** -0.5
    v = heads(v).astype(f32)
    log_gamma = jnp.log1p(-(2.0 ** (-5.0 - jnp.arange(H, dtype=f32))))
    n = jnp.arange(C, dtype=f32)
    diff = n[:, None] - n[None, :]
    decay_in = jnp.where(diff >= 0, jnp.exp(log_gamma[:, None, None] * jnp.maximum(diff, 0.0)), 0.0)
    xi = jnp.exp(log_gamma[:, None] * (n + 1.0))
    zeta = jnp.exp(log_gamma[:, None] * (C - 1.0 - n))
    gamma_c = jnp.exp(log_gamma * C)
    chunks = lambda t: t.reshape(B, n_c, C, H, Dh).transpose(1, 0, 3, 2, 4)

    def step(state, inp):
        qc, kc, vc = inp
        inner = jnp.einsum('bhnd,bhmd->bhnm', qc, kc) * decay_in
        o = jnp.einsum('bhnm,bhme->bhne', inner, vc) + jnp.einsum('bhnd,bhde->bhne', qc, state) * xi[..., None]
        state = state * gamma_c[:, None, None] + jnp.einsum('bhmd,bhme->bhde', kc * zeta[..., None], vc)
        return state, o

    _, o = lax.scan(step, jnp.zeros((B, H, Dh, Dh), f32), (chunks(q), chunks(k), chunks(v)))
    o = head_layer_norm(o.transpose(1, 0, 3, 2, 4).reshape(B, S, H, Dh), NORM_EPS)
    return jax.nn.silu(g.astype(f32)) * o.reshape(B, S, D_GRP)


def deltanet_mixer(q, k, v, a, b, z, conv_w, a_log, dt_bias, norm_g):
    B, S, _ = q.shape
    H, Dh, C = DN_HEADS, HEAD_DIM, DN_CHUNK
    n_c = S // C
    qkv = jnp.concatenate([q, k, v], -1)
    qkv = lax.conv_general_dilated(qkv, conv_w[:, None, :], window_strides=(1,), padding=[(CONV_WIDTH - 1, 0)],
                                   dimension_numbers=('NWC', 'WIO', 'NWC'), feature_group_count=3 * D_GRP)
    qkv = jax.nn.silu(qkv.astype(f32))
    q, k, v = [t.reshape(B, S, H, Dh) for t in jnp.split(qkv, 3, -1)]
    q = l2norm(q) * Dh ** -0.5
    k = l2norm(k)
    beta = jax.nn.sigmoid(b.astype(f32))
    g = -jnp.exp(a_log.astype(f32)) * jax.nn.softplus(a.astype(f32) + dt_bias.astype(f32))
    chunks = lambda t: jnp.moveaxis(t.reshape((B, n_c, C) + t.shape[2:]), 3, 1)
    qc, kc, vc, bc = chunks(q), chunks(k), chunks(v), chunks(beta)
    gc = jnp.cumsum(chunks(g), -1)
    kbeta = kc * bc[..., None]
    vbeta = vc * bc[..., None]
    idx = jnp.arange(C)
    tril = idx[:, None] >= idx[None, :]
    strict = idx[:, None] > idx[None, :]
    decay = jnp.exp(jnp.where(tril, gc[..., :, None] - gc[..., None, :], -jnp.inf))
    m = jnp.where(strict, jnp.einsum('bhncd,bhnkd->bhnck', kbeta, kc) * decay, 0.0)
    eye = jnp.eye(C, dtype=f32)
    t_inv = lax.linalg.triangular_solve(eye + m, jnp.broadcast_to(eye, m.shape), left_side=True,
                                        lower=True, unit_diagonal=True)
    u = t_inv @ vbeta
    w = t_inv @ (kbeta * jnp.exp(gc)[..., None])
    attn_in = jnp.where(tril, jnp.einsum('bhncd,bhnkd->bhnck', qc, kc) * decay, 0.0)
    q_dec = qc * jnp.exp(gc)[..., None]
    k_dec = kc * jnp.exp(gc[..., -1:] - gc)[..., None]
    g_last = jnp.exp(gc[..., -1])

    def step(state, inp):
        u_i, w_i, a_i, qd_i, kd_i, gl_i = inp
        v_new = u_i - w_i @ state
        o = qd_i @ state + a_i @ v_new
        state = state * gl_i[..., None, None] + jnp.swapaxes(kd_i, -1, -2) @ v_new
        return state, o

    xs = tuple(jnp.moveaxis(t, 2, 0) for t in (u, w, attn_in, q_dec, k_dec, g_last))
    _, o = lax.scan(step, jnp.zeros((B, H, Dh, Dh), f32), xs)
    o = o.transpose(1, 0, 3, 2, 4).reshape(B, S, H, Dh)
    o = rms_norm(o, norm_g) * jax.nn.silu(z.astype(f32)).reshape(B, S, H, Dh)
    return o.reshape(B, S, D_GRP)


def rwkv7_mixer(proj, mu, w0, w2, a0, a2, g2, k_k, k_a, r_k, ln_g, ln_b):
    B, S, _ = proj.shape
    H, Dh = RW_HEADS, RW_HEAD_DIM
    x = proj.astype(f32)
    x_prev = jnp.pad(x, ((0, 0), (1, 0), (0, 0)))[:, :-1]
    x = x + (x_prev - x) * mu
    r, k, v, w_lo, a_lo, g_lo = split_cols(x, (D_GRP, D_GRP, D_GRP, RW_DECAY_LORA, RW_AAA_LORA, RW_GATE_LORA))
    w = -jax.nn.softplus(-(w0 + jnp.tanh(w_lo) @ w2)) - 0.5
    decay = jnp.exp(-jnp.exp(w))
    a = jax.nn.sigmoid(a0 + a_lo @ a2)
    g = jax.nn.sigmoid(g_lo) @ g2
    heads = lambda t: t.reshape(B, S, H, Dh)
    kk = l2norm(heads(k * k_k))
    k = k * (1.0 + (a - 1.0) * k_a)
    r, k, v, a, decay = heads(r), heads(k), heads(v), heads(a), heads(decay)

    def step(state, inp):
        r_t, w_t, k_t, v_t, kk_t, a_t = inp
        sa = jnp.einsum('bhij,bhj->bhi', state, -kk_t)
        state = (state * w_t[:, :, None, :] + sa[..., None] * (kk_t * a_t)[:, :, None, :]
                 + v_t[..., None] * k_t[:, :, None, :])
        return state, jnp.einsum('bhij,bhj->bhi', state, r_t)

    xs = tuple(jnp.moveaxis(t, 1, 0) for t in (r, decay, k, v, kk, a))
    _, y = lax.scan(step, jnp.zeros((B, H, Dh, Dh), f32), xs)
    y = jnp.moveaxis(y, 0, 1)
    y = head_layer_norm(y, RW_LN_EPS) * ln_g.reshape(H, Dh) + ln_b.reshape(H, Dh)
    y = y + jnp.sum(r * k * r_k, -1, keepdims=True) * v
    return y.reshape(B, S, D_GRP) * g


def hybrid_mixer(hn, w_in, w_out, ck_w1, ck_w2, ck_pe, cv_w1, cv_w2, cv_pe, dn_conv, dn_a_log, dn_dt_bias,
                 dn_norm, rw_mu, rw_w0, rw_w2, rw_a0, rw_a2, rw_g2, rw_kk, rw_ka, rw_rk, rw_ln_g, rw_ln_b):
    proj = hn @ w_in
    p_nsa, p_ret, p_dn, p_rw = split_cols(proj, (NSA_COLS, RET_COLS, DN_COLS, RW_COLS))
    o_nsa = nsa_mixer(*split_cols(p_nsa, (D_GRP,) + (NSA_KV_W,) * 6 + (3 * NSA_HEADS,)),
                      ck_w1, ck_w2, ck_pe, cv_w1, cv_w2, cv_pe)
    o_ret = retention_mixer(*split_cols(p_ret, (D_GRP,) * 4))
    o_dn = deltanet_mixer(*split_cols(p_dn, (D_GRP,) * 3 + (DN_HEADS, DN_HEADS, D_GRP)),
                          dn_conv, dn_a_log, dn_dt_bias, dn_norm)
    o_rw = rwkv7_mixer(p_rw, rw_mu, rw_w0, rw_w2, rw_a0, rw_a2, rw_g2, rw_kk, rw_ka, rw_rk, rw_ln_g, rw_ln_b)
    o = jnp.concatenate([o_nsa, o_ret, o_dn, o_rw], -1).astype(hn.dtype)
    return o @ w_out


def hier_moe(xn, router_grp, router_grp_b, router_exp, router_exp_b, w_gate, w_up, w_down):
    B, S, D = xn.shape
    M = B * S
    xf = xn.reshape(M, D)
    grp_p = jax.nn.softmax((xf @ router_grp).astype(f32) + router_grp_b.astype(f32), -1)
    g_idx = jnp.argmax(grp_p, -1)
    g_w = jnp.take_along_axis(grp_p, g_idx[:, None], -1)
    exp_logits = ((xf @ router_exp).astype(f32) + router_exp_b.astype(f32)).reshape(M, N_GROUPS, EXPERTS_PER_GROUP)
    exp_logits = jnp.take_along_axis(exp_logits, g_idx[:, None, None], 1)[:, 0]
    top_p, top_i = lax.top_k(jax.nn.softmax(exp_logits, -1), TOP_K)
    weights = g_w * top_p / jnp.sum(top_p, -1, keepdims=True)
    expert_ids = g_idx[:, None] * EXPERTS_PER_GROUP + top_i
    A = M * TOP_K
    e_flat = expert_ids.reshape(A)
    tok_flat = jnp.repeat(jnp.arange(M), TOP_K)
    w_flat = weights.reshape(A)
    order = jnp.argsort(e_flat)
    e_s, tok_s, w_s = e_flat[order], tok_flat[order], w_flat[order]
    counts = jnp.bincount(e_flat, length=N_EXPERTS)
    padded = (counts + ROW_BLOCK - 1) // ROW_BLOCK * ROW_BLOCK
    start_unpad = jnp.cumsum(counts) - counts
    end_pad = jnp.cumsum(padded)
    start_pad = end_pad - padded
    dest = start_pad[e_s] + (jnp.arange(A) - start_unpad[e_s])
    n_blk = -(-A // ROW_BLOCK) + N_EXPERTS
    A_pad = n_blk * ROW_BLOCK
    x_pad = jnp.zeros((A_pad, D), xn.dtype).at[dest].set(xf[tok_s])
    tok_pad = jnp.zeros((A_pad,), jnp.int32).at[dest].set(tok_s.astype(jnp.int32))
    w_pad = jnp.zeros((A_pad,), f32).at[dest].set(w_s)
    blk_expert = jnp.minimum(jnp.searchsorted(end_pad, jnp.arange(n_blk) * ROW_BLOCK, side='right'), N_EXPERTS - 1)

    def expert_block(args):
        xb, e = args
        return (jax.nn.silu(xb @ w_gate[e]) * (xb @ w_up[e])) @ w_down[e]

    y_pad = lax.map(expert_block, (x_pad.reshape(n_blk, ROW_BLOCK, D), blk_expert)).reshape(A_pad, D)
    y = jnp.zeros((M, D), f32).at[tok_pad].add(w_pad[:, None] * y_pad.astype(f32))
    return y.astype(xn.dtype).reshape(B, S, D)


def setup_inputs(seed: int = 0) -> dict:
    key = jax.random.key(seed)
    ks = iter(jax.random.split(key, 48))
    L = DEPTH

    def nrm(shape, scale):
        return scale * jax.random.normal(next(ks), shape, f32)

    def gain(shape):
        return 1.0 + 0.05 * jax.random.normal(next(ks), shape, f32)

    return {
        "x": nrm((BATCH, SEQ, D_MODEL), 1.0),
        "p": nrm((DEPTH, BATCH, SEQ, PLE_DIM), 1.0),
        "norm_mix": gain((L, D_MODEL)),
        "w_in": nrm((L, D_MODEL, P_TOTAL), D_MODEL ** -0.5),
        "w_out": nrm((L, D_MIX, D_MODEL), D_MIX ** -0.5),
        "nsa_ck_w1": nrm((L, CMP_LEN * HEAD_DIM, HEAD_DIM), (CMP_LEN * HEAD_DIM) ** -0.5),
        "nsa_ck_w2": nrm((L, HEAD_DIM, HEAD_DIM), HEAD_DIM ** -0.5),
        "nsa_ck_pe": nrm((L, CMP_LEN, HEAD_DIM), 0.1),
        "nsa_cv_w1": nrm((L, CMP_LEN * HEAD_DIM, HEAD_DIM), (CMP_LEN * HEAD_DIM) ** -0.5),
        "nsa_cv_w2": nrm((L, HEAD_DIM, HEAD_DIM), HEAD_DIM ** -0.5),
        "nsa_cv_pe": nrm((L, CMP_LEN, HEAD_DIM), 0.1),
        "dn_conv": nrm((L, CONV_WIDTH, 3 * D_GRP), CONV_WIDTH ** -0.5),
        "dn_a_log": jnp.log(jax.random.uniform(next(ks), (L, DN_HEADS), f32, 1.0, 16.0)),
        "dn_dt_bias": jax.random.uniform(next(ks), (L, DN_HEADS), f32, -4.0, -2.0),
        "dn_norm": gain((L, HEAD_DIM)),
        "rw_mu": jax.random.uniform(next(ks), (L, RW_COLS), f32),
        "rw_w0": nrm((L, D_GRP), 0.5) - 0.5,
        "rw_w2": nrm((L, RW_DECAY_LORA, D_GRP), RW_DECAY_LORA ** -0.5),
        "rw_a0": nrm((L, D_GRP), 0.1),
        "rw_a2": nrm((L, RW_AAA_LORA, D_GRP), RW_AAA_LORA ** -0.5),
        "rw_g2": nrm((L, RW_GATE_LORA, D_GRP), RW_GATE_LORA ** -0.5),
        "rw_kk": 0.85 + nrm((L, D_GRP), 0.05),
        "rw_ka": gain((L, D_GRP)),
        "rw_rk": nrm((L, RW_HEADS, RW_HEAD_DIM), 0.1),
        "rw_ln_g": gain((L, D_GRP)),
        "rw_ln_b": nrm((L, D_GRP), 0.02),
        "norm_moe": gain((L, D_MODEL)),
        "moe_router_grp": nrm((L, D_MODEL, N_GROUPS), D_MODEL ** -0.5),
        "moe_router_grp_b": nrm((L, N_GROUPS), 0.01),
        "moe_router_exp": nrm((L, D_MODEL, N_EXPERTS), D_MODEL ** -0.5),
        "moe_router_exp_b": nrm((L, N_EXPERTS), 0.01),
        "moe_w_gate": nrm((L, N_EXPERTS, D_MODEL, D_EXPERT), D_MODEL ** -0.5),
        "moe_w_up": nrm((L, N_EXPERTS, D_MODEL, D_EXPERT), D_MODEL ** -0.5),
        "moe_w_down": nrm((L, N_EXPERTS, D_EXPERT, D_MODEL), D_EXPERT ** -0.5),
        "norm_ple": gain((L, D_MODEL)),
        "ple_w": nrm((L, PLE_DIM, D_MODEL), PLE_DIM ** -0.5),
        "ple_gate": nrm((L, D_MODEL, D_MODEL), D_MODEL ** -0.5),
        "norm_final": gain((D_MODEL,)),
    }


def reference(x, p, norm_mix, w_in, w_out, nsa_ck_w1, nsa_ck_w2, nsa_ck_pe, nsa_cv_w1, nsa_cv_w2, nsa_cv_pe,
              dn_conv, dn_a_log, dn_dt_bias, dn_norm, rw_mu, rw_w0, rw_w2, rw_a0, rw_a2, rw_g2, rw_kk, rw_ka,
              rw_rk, rw_ln_g, rw_ln_b, norm_moe, moe_router_grp, moe_router_grp_b, moe_router_exp,
              moe_router_exp_b, moe_w_gate, moe_w_up, moe_w_down, norm_ple, ple_w, ple_gate, norm_final):
    h = x
    for i in range(DEPTH):
        hn = rms_norm(h, norm_mix[i])
        h = h + hybrid_mixer(hn, w_in[i], w_out[i], nsa_ck_w1[i], nsa_ck_w2[i], nsa_ck_pe[i], nsa_cv_w1[i],
                             nsa_cv_w2[i], nsa_cv_pe[i], dn_conv[i], dn_a_log[i], dn_dt_bias[i], dn_norm[i],
                             rw_mu[i], rw_w0[i], rw_w2[i], rw_a0[i], rw_a2[i], rw_g2[i], rw_kk[i], rw_ka[i],
                             rw_rk[i], rw_ln_g[i], rw_ln_b[i])
        hn = rms_norm(h, norm_moe[i])
        h = h + hier_moe(hn, moe_router_grp[i], moe_router_grp_b[i], moe_router_exp[i], moe_router_exp_b[i],
                         moe_w_gate[i], moe_w_up[i], moe_w_down[i])
        hn = rms_norm(h, norm_ple[i])
        h = h + (p[i].astype(h.dtype) @ ple_w[i]) * jax.nn.sigmoid(hn @ ple_gate[i])
    return rms_norm(h, norm_final)
```

```python
import functools

import jax
import jax.numpy as jnp
import numpy as np
from jax import lax
from jax.experimental import pallas as pl
from jax.experimental.pallas import tpu as pltpu

D_MODEL = 4096
BATCH = 2
SEQ = 4096
DEPTH = 2
f32 = jnp.float32
bf16 = jnp.bfloat16
D_MIX = D_MODEL
D_GRP = D_MIX // 4
HEAD_DIM = 128
NORM_EPS = 1e-6
NSA_HEADS = D_GRP // HEAD_DIM
NSA_KV_HEADS = 2
NSA_GROUP = NSA_HEADS // NSA_KV_HEADS
NSA_KV_W = NSA_KV_HEADS * HEAD_DIM
CMP_LEN = 32
CMP_STRIDE = 16
SEL_LEN = 64
N_SEL = 16
N_LOCAL_FORCED = 2
FORCED_SCORE = 1e4
WINDOW = 512
SEL_QBLK = 64
WIN_QBLK = 128
ROPE_THETA = 500000.0
ROPE_DIMS = HEAD_DIM // 4
RET_HEADS = D_GRP // HEAD_DIM
RET_CHUNK = 128
RET_THETA = 10000.0
DN_HEADS = D_GRP // HEAD_DIM
DN_CHUNK = 64
CONV_WIDTH = 4
RW_HEAD_DIM = 64
RW_HEADS = D_GRP // RW_HEAD_DIM
RW_DECAY_LORA = 64
RW_AAA_LORA = 64
RW_GATE_LORA = 160
RW_LN_EPS = 64e-5
N_GROUPS = 4
EXPERTS_PER_GROUP = 8
N_EXPERTS = N_GROUPS * EXPERTS_PER_GROUP
TOP_K = 2
D_EXPERT = 768
ROW_BLOCK = 256
PLE_DIM = 256
NSA_COLS = D_GRP + 6 * NSA_KV_W + 3 * NSA_HEADS
RET_COLS = 4 * D_GRP
DN_COLS = 4 * D_GRP + 2 * DN_HEADS
RW_COLS = 3 * D_GRP + RW_DECAY_LORA + RW_AAA_LORA + RW_GATE_LORA
P_TOTAL = NSA_COLS + RET_COLS + DN_COLS + RW_COLS

VMEM_LIMIT = 56 * 1024 * 1024


def split_cols(x, sizes):
    return jnp.split(x, np.cumsum(sizes)[:-1].tolist(), axis=-1)


def rms_norm(x, g, eps=NORM_EPS):
    xf = x.astype(f32)
    y = xf * lax.rsqrt(jnp.mean(xf * xf, -1, keepdims=True) + eps)
    return (y * g.astype(f32)).astype(x.dtype)


def l2norm(x, eps=1e-6):
    return x * lax.rsqrt(jnp.sum(x * x, -1, keepdims=True) + eps)


def head_layer_norm(x, eps):
    mu = jnp.mean(x, -1, keepdims=True)
    return (x - mu) * lax.rsqrt(jnp.var(x, -1, keepdims=True) + eps)


def masked_softmax(s, mask):
    s = jnp.where(mask, s.astype(f32), -jnp.inf)
    m = jnp.max(s, -1, keepdims=True)
    m = jnp.where(jnp.isfinite(m), m, 0.0)
    e = jnp.where(mask, jnp.exp(s - m), 0.0)
    return e / jnp.maximum(jnp.sum(e, -1, keepdims=True), jnp.finfo(f32).tiny)


def rotary(x, freqs, rot_dims):
    S = x.shape[1]
    half = rot_dims // 2
    ang = jnp.arange(S, dtype=f32)[:, None] * freqs[None, :]
    ang = ang.reshape((S,) + (1,) * (x.ndim - 3) + (half,))
    cos, sin = jnp.cos(ang), jnp.sin(ang)
    x1 = x[..., :half].astype(f32)
    x2 = x[..., half:rot_dims].astype(f32)
    out = jnp.concatenate([x1 * cos - x2 * sin, x2 * cos + x1 * sin, x[..., rot_dims:].astype(f32)], -1)
    return out.astype(x.dtype)


def cmp_sel_overlap(S):
    n_cmp = (S - CMP_LEN) // CMP_STRIDE + 1
    c0 = np.arange(n_cmp)[:, None] * CMP_STRIDE
    s0 = np.arange(S // SEL_LEN)[None, :] * SEL_LEN
    ov = np.clip(np.minimum(c0 + CMP_LEN, s0 + SEL_LEN) - np.maximum(c0, s0), 0, None) / CMP_LEN
    return jnp.asarray(ov, f32)


def nsa_mixer(q, kc, vc, ks, vs, kw, vw, gate_logit, ck_w1, ck_w2, ck_pe, cv_w1, cv_w2, cv_pe):
    B, S, _ = q.shape
    G, R, Dh = NSA_KV_HEADS, NSA_GROUP, HEAD_DIM
    scale = Dh ** -0.5
    half = ROPE_DIMS // 2
    freqs = ROPE_THETA ** (-jnp.arange(half, dtype=f32) / half)
    q = rotary(q.reshape(B, S, G, R, Dh), freqs, ROPE_DIMS)
    kc, vc, ks, vs, kw, vw = [t.reshape(B, S, G, Dh) for t in (kc, vc, ks, vs, kw, vw)]
    kc, ks, kw = [rotary(t, freqs, ROPE_DIMS) for t in (kc, ks, kw)]
    pos = jnp.arange(S)
    n_cmp = (S - CMP_LEN) // CMP_STRIDE + 1
    blk_tok = np.arange(n_cmp)[:, None] * CMP_STRIDE + np.arange(CMP_LEN)[None, :]

    def compress(t, w1, w2, pe):
        blk = t[:, blk_tok] + pe[:, None, :]
        blk = blk.transpose(0, 3, 1, 2, 4).reshape(B, G, n_cmp, CMP_LEN * Dh)
        return jax.nn.silu(blk @ w1) @ w2

    k_cmp = compress(kc, ck_w1, ck_w2, ck_pe)
    v_cmp = compress(vc, cv_w1, cv_w2, cv_pe)
    s_cmp = jnp.einsum('bsgrd,bgnd->bgrsn', q, k_cmp) * scale
    cmp_mask = jnp.asarray(blk_tok[:, -1])[None, :] <= pos[:, None]
    p_cmp = masked_softmax(s_cmp, cmp_mask)
    o_cmp = jnp.einsum('bgrsn,bgnd->bsgrd', p_cmp, v_cmp.astype(f32))
    n_sel_blk = S // SEL_LEN
    k_top = min(N_SEL, n_sel_blk)
    imp = jnp.einsum('bgrsn,nj->bgsj', p_cmp, cmp_sel_overlap(S))
    blk = jnp.arange(n_sel_blk)[None, :]
    cur = (pos // SEL_LEN)[:, None]
    visible = blk <= cur
    forced = (blk == 0) | (visible & (blk > cur - N_LOCAL_FORCED))
    imp = jnp.where(forced, FORCED_SCORE, jnp.where(visible, imp, -1.0))
    sel_idx = lax.top_k(imp, k_top)[1]
    ks_blk = ks.reshape(B, n_sel_blk, SEL_LEN, G, Dh).transpose(0, 3, 1, 2, 4)
    vs_blk = vs.reshape(B, n_sel_blk, SEL_LEN, G, Dh).transpose(0, 3, 1, 2, 4)
    n_qb = S // SEL_QBLK
    q_b = q.reshape(B, n_qb, SEL_QBLK, G, R, Dh).transpose(1, 0, 2, 3, 4, 5)
    idx_b = sel_idx.reshape(B, G, n_qb, SEL_QBLK, k_top).transpose(2, 0, 1, 3, 4)
    qpos_b = pos.reshape(n_qb, SEL_QBLK)
    b_ix = jnp.arange(B)[:, None, None, None]
    g_ix = jnp.arange(G)[None, :, None, None]

    def sel_block(args):
        qb, ib, qp = args
        kb = ks_blk[b_ix, g_ix, ib].reshape(B, G, SEL_QBLK, k_top * SEL_LEN, Dh)
        vb = vs_blk[b_ix, g_ix, ib].reshape(B, G, SEL_QBLK, k_top * SEL_LEN, Dh)
        tok = (ib[..., None] * SEL_LEN + jnp.arange(SEL_LEN)).reshape(B, G, SEL_QBLK, k_top * SEL_LEN)
        s = jnp.einsum('bqgrd,bgqkd->bgrqk', qb, kb) * scale
        mask = tok[:, :, None] <= qp[None, None, None, :, None]
        pr = masked_softmax(s, mask)
        return jnp.einsum('bgrqk,bgqkd->bqgrd', pr, vb.astype(f32))

    o_sel = lax.map(sel_block, (q_b, idx_b, qpos_b))
    o_sel = o_sel.transpose(1, 0, 2, 3, 4, 5).reshape(B, S, G, R, Dh)
    n_wb = S // WIN_QBLK
    span = WINDOW + WIN_QBLK
    key_idx = np.arange(n_wb)[:, None] * WIN_QBLK + np.arange(span)[None, :]
    pad = ((0, 0), (WINDOW, 0), (0, 0), (0, 0))
    kb = jnp.pad(kw, pad)[:, key_idx]
    vb = jnp.pad(vw, pad)[:, key_idx]
    qb = q.reshape(B, n_wb, WIN_QBLK, G, R, Dh)
    s_win = jnp.einsum('bnqgrd,bnkgd->bgrnqk', qb, kb) * scale
    kpos = jnp.asarray(key_idx - WINDOW)[:, None, :]
    qpos = pos.reshape(n_wb, WIN_QBLK)[:, :, None]
    win_mask = (kpos >= 0) & (kpos <= qpos) & (kpos > qpos - WINDOW)
    p_win = masked_softmax(s_win, win_mask)
    o_win = jnp.einsum('bgrnqk,bnkgd->bnqgrd', p_win, vb.astype(f32)).reshape(B, S, G, R, Dh)
    gate = jax.nn.sigmoid(gate_logit.astype(f32)).reshape(B, S, G, R, 3)
    o = gate[..., 0:1] * o_cmp + gate[..., 1:2] * o_sel + gate[..., 2:3] * o_win
    return o.reshape(B, S, D_GRP)


def retention_mixer(q, k, v, g):
    B, S, _ = q.shape
    H, Dh, C = RET_HEADS, HEAD_DIM, RET_CHUNK
    n_c = S // C
    freqs = RET_THETA ** (-jnp.linspace(0.0, 1.0, Dh // 2, dtype=f32))
    heads = lambda t: t.reshape(B, S, H, Dh)
    q = rotary(heads(q), freqs, Dh).astype(f32)
    k = rotary(heads(k), freqs, Dh).astype(f32) * Dh ** -0.5
    v = heads(v).astype(f32)
    log_gamma = jnp.log1p(-(2.0 ** (-5.0 - jnp.arange(H, dtype=f32))))
    n = jnp.arange(C, dtype=f32)
    diff = n[:, None] - n[None, :]
    decay_in = jnp.where(diff >= 0, jnp.exp(log_gamma[:, None, None] * jnp.maximum(diff, 0.0)), 0.0)
    xi = jnp.exp(log_gamma[:, None] * (n + 1.0))
    zeta = jnp.exp(log_gamma[:, None] * (C - 1.0 - n))
    gamma_c = jnp.exp(log_gamma * C)
    chunks = lambda t: t.reshape(B, n_c, C, H, Dh).transpose(1, 0, 3, 2, 4)

    def step(state, inp):
        qc, kc, vc = inp
        inner = jnp.einsum('bhnd,bhmd->bhnm', qc, kc) * decay_in
        o = jnp.einsum('bhnm,bhme->bhne', inner, vc) + jnp.einsum('bhnd,bhde->bhne', qc, state) * xi[..., None]
        state = state * gamma_c[:, None, None] + jnp.einsum('bhmd,bhme->bhde', kc * zeta[..., None], vc)
        return state, o

    _, o = lax.scan(step, jnp.zeros((B, H, Dh, Dh), f32), (chunks(q), chunks(k), chunks(v)))
    o = head_layer_norm(o.transpose(1, 0, 3, 2, 4).reshape(B, S, H, Dh), NORM_EPS)
    return jax.nn.silu(g.astype(f32)) * o.reshape(B, S, D_GRP)


def deltanet_mixer(q, k, v, a, b, z, conv_w, a_log, dt_bias, norm_g):
    B, S, _ = q.shape
    H, Dh, C = DN_HEADS, HEAD_DIM, DN_CHUNK
    n_c = S // C
    qkv = jnp.concatenate([q, k, v], -1)
    qkv = lax.conv_general_dilated(qkv, conv_w[:, None, :], window_strides=(1,), padding=[(CONV_WIDTH - 1, 0)],
                                   dimension_numbers=('NWC', 'WIO', 'NWC'), feature_group_count=3 * D_GRP)
    qkv = jax.nn.silu(qkv.astype(f32))
    q, k, v = [t.reshape(B, S, H, Dh) for t in jnp.split(qkv, 3, -1)]
    q = l2norm(q) * Dh ** -0.5
    k = l2norm(k)
    beta = jax.nn.sigmoid(b.astype(f32))
    g = -jnp.exp(a_log.astype(f32)) * jax.nn.softplus(a.astype(f32) + dt_bias.astype(f32))
    chunks = lambda t: jnp.moveaxis(t.reshape((B, n_c, C) + t.shape[2:]), 3, 1)
    qc, kc, vc, bc = chunks(q), chunks(k), chunks(v), chunks(beta)
    gc = jnp.cumsum(chunks(g), -1)
    kbeta = kc * bc[..., None]
    vbeta = vc * bc[..., None]
    idx = jnp.arange(C)
    tril = idx[:, None] >= idx[None, :]
    strict = idx[:, None] > idx[None, :]
    decay = jnp.exp(jnp.where(tril, gc[..., :, None] - gc[..., None, :], -jnp.inf))
    m = jnp.where(strict, jnp.einsum('bhncd,bhnkd->bhnck', kbeta, kc) * decay, 0.0)
    eye = jnp.eye(C, dtype=f32)
    t_inv = lax.linalg.triangular_solve(eye + m, jnp.broadcast_to(eye, m.shape), left_side=True,
                                        lower=True, unit_diagonal=True)
    u = t_inv @ vbeta
    w = t_inv @ (kbeta * jnp.exp(gc)[..., None])
    attn_in = jnp.where(tril, jnp.einsum('bhncd,bhnkd->bhnck', qc, kc) * decay, 0.0)
    q_dec = qc * jnp.exp(gc)[..., None]
    k_dec = kc * jnp.exp(gc[..., -1:] - gc)[..., None]
    g_last = jnp.exp(gc[..., -1])

    def step(state, inp):
        u_i, w_i, a_i, qd_i, kd_i, gl_i = inp
        v_new = u_i - w_i @ state
        o = qd_i @ state + a_i @ v_new
        state = state * gl_i[..., None, None] + jnp.swapaxes(kd_i, -1, -2) @ v_new
        return state, o

    xs = tuple(jnp.moveaxis(t, 2, 0) for t in (u, w, attn_in, q_dec, k_dec, g_last))
    _, o = lax.scan(step, jnp.zeros((B, H, Dh, Dh), f32), xs)
    o = o.transpose(1, 0, 3, 2, 4).reshape(B, S, H, Dh)
    o = rms_norm(o, norm_g) * jax.nn.silu(z.astype(f32)).reshape(B, S, H, Dh)
    return o.reshape(B, S, D_GRP)


def rwkv7_mixer(proj, mu, w0, w2, a0, a2, g2, k_k, k_a, r_k, ln_g, ln_b):
    B, S, _ = proj.shape
    H, Dh = RW_HEADS, RW_HEAD_DIM
    x = proj.astype(f32)
    x_prev = jnp.pad(x, ((0, 0), (1, 0), (0, 0)))[:, :-1]
    x = x + (x_prev - x) * mu
    r, k, v, w_lo, a_lo, g_lo = split_cols(x, (D_GRP, D_GRP, D_GRP, RW_DECAY_LORA, RW_AAA_LORA, RW_GATE_LORA))
    w = -jax.nn.softplus(-(w0 + jnp.tanh(w_lo) @ w2)) - 0.5
    decay = jnp.exp(-jnp.exp(w))
    a = jax.nn.sigmoid(a0 + a_lo @ a2)
    g = jax.nn.sigmoid(g_lo) @ g2
    heads = lambda t: t.reshape(B, S, H, Dh)
    kk = l2norm(heads(k * k_k))
    k = k * (1.0 + (a - 1.0) * k_a)
    r, k, v, a, decay = heads(r), heads(k), heads(v), heads(a), heads(decay)

    def step(state, inp):
        r_t, w_t, k_t, v_t, kk_t, a_t = inp
        sa = jnp.einsum('bhij,bhj->bhi', state, -kk_t)
        state = (state * w_t[:, :, None, :] + sa[..., None] * (kk_t * a_t)[:, :, None, :]
                 + v_t[..., None] * k_t[:, :, None, :])
        return state, jnp.einsum('bhij,bhj->bhi', state, r_t)

    xs = tuple(jnp.moveaxis(t, 1, 0) for t in (r, decay, k, v, kk, a))
    _, y = lax.scan(step, jnp.zeros((B, H, Dh, Dh), f32), xs)
    y = jnp.moveaxis(y, 0, 1)
    y = head_layer_norm(y, RW_LN_EPS) * ln_g.reshape(H, Dh) + ln_b.reshape(H, Dh)
    y = y + jnp.sum(r * k * r_k, -1, keepdims=True) * v
    return y.reshape(B, S, D_GRP) * g


def _mm_body(a_ref, b_ref, o_ref, acc_ref):
    k = pl.program_id(2)

    @pl.when(k == 0)
    def _():
        acc_ref[...] = jnp.zeros_like(acc_ref)

    acc_ref[...] += jnp.dot(a_ref[...].astype(bf16), b_ref[...].astype(bf16), preferred_element_type=f32)

    @pl.when(k == pl.num_programs(2) - 1)
    def _():
        o_ref[...] = acc_ref[...].astype(o_ref.dtype)


def matmul(a, b, out_dtype=f32, tm=1024, tn=1024, tk=512, name="matmul"):
    M, K = a.shape
    _, N = b.shape
    tm, tn, tk = min(tm, M), min(tn, N), min(tk, K)
    assert M % tm == 0 and N % tn == 0 and K % tk == 0, (M, N, K, tm, tn, tk)
    return pl.pallas_call(
        _mm_body,
        grid=(M // tm, N // tn, K // tk),
        in_specs=[pl.BlockSpec((tm, tk), lambda i, j, k: (i, k)),
                  pl.BlockSpec((tk, tn), lambda i, j, k: (k, j))],
        out_specs=pl.BlockSpec((tm, tn), lambda i, j, k: (i, j)),
        out_shape=jax.ShapeDtypeStruct((M, N), out_dtype),
        scratch_shapes=[pltpu.VMEM((tm, tn), f32)],
        compiler_params=pltpu.CompilerParams(
            dimension_semantics=("parallel", "parallel", "arbitrary"), vmem_limit_bytes=VMEM_LIMIT),
        name=name,
    )(a, b)


def _moe_body(be_ref, x_ref, wg_ref, wu_ref, wd_ref, o_ref):
    f = pl.program_id(1)
    x = x_ref[...]
    hg = jnp.dot(x, wg_ref[0], preferred_element_type=f32)
    hu = jnp.dot(x, wu_ref[0], preferred_element_type=f32)
    h = (hg * jax.nn.sigmoid(hg) * hu).astype(bf16)
    y = jnp.dot(h, wd_ref[0], preferred_element_type=f32)

    @pl.when(f == 0)
    def _():
        o_ref[...] = y

    @pl.when(f != 0)
    def _():
        o_ref[...] += y


def moe_experts(x_pad, blk_expert, wg, wu, wd, tf=256):
    A_pad, D = x_pad.shape
    n_blk = A_pad // ROW_BLOCK
    nf = D_EXPERT // tf
    grid_spec = pltpu.PrefetchScalarGridSpec(
        num_scalar_prefetch=1,
        grid=(n_blk, nf),
        in_specs=[pl.BlockSpec((ROW_BLOCK, D), lambda i, f, be: (i, 0)),
                  pl.BlockSpec((1, D, tf), lambda i, f, be: (be[i], 0, f)),
                  pl.BlockSpec((1, D, tf), lambda i, f, be: (be[i], 0, f)),
                  pl.BlockSpec((1, tf, D), lambda i, f, be: (be[i], f, 0))],
        out_specs=pl.BlockSpec((ROW_BLOCK, D), lambda i, f, be: (i, 0)),
    )
    return pl.pallas_call(
        _moe_body,
        grid_spec=grid_spec,
        out_shape=jax.ShapeDtypeStruct((A_pad, D), f32),
        compiler_params=pltpu.CompilerParams(
            dimension_semantics=("parallel", "arbitrary"), vmem_limit_bytes=VMEM_LIMIT),
        name="moe_experts",
    )(blk_expert, x_pad, wg, wu, wd)


def hier_moe(xn, router_grp, router_grp_b, router_exp, router_exp_b, w_gate, w_up, w_down):
    B, S, D = xn.shape
    M = B * S
    xf = xn.reshape(M, D)
    grp_p = jax.nn.softmax((xf @ router_grp).astype(f32) + router_grp_b.astype(f32), -1)
    g_idx = jnp.argmax(grp_p, -1)
    g_w = jnp.take_along_axis(grp_p, g_idx[:, None], -1)
    exp_logits = ((xf @ router_exp).astype(f32) + router_exp_b.astype(f32)).reshape(M, N_GROUPS, EXPERTS_PER_GROUP)
    exp_logits = jnp.take_along_axis(exp_logits, g_idx[:, None, None], 1)[:, 0]
    top_p, top_i = lax.top_k(jax.nn.softmax(exp_logits, -1), TOP_K)
    weights = g_w * top_p / jnp.sum(top_p, -1, keepdims=True)
    expert_ids = g_idx[:, None] * EXPERTS_PER_GROUP + top_i
    A = M * TOP_K
    e_flat = expert_ids.reshape(A)
    tok_flat = jnp.repeat(jnp.arange(M), TOP_K)
    w_flat = weights.reshape(A)
    order = jnp.argsort(e_flat)
    e_s, tok_s, w_s = e_flat[order], tok_flat[order], w_flat[order]
    counts = jnp.bincount(e_flat, length=N_EXPERTS)
    padded = (counts + ROW_BLOCK - 1) // ROW_BLOCK * ROW_BLOCK
    start_unpad = jnp.cumsum(counts) - counts
    end_pad = jnp.cumsum(padded)
    start_pad = end_pad - padded
    dest = start_pad[e_s] + (jnp.arange(A) - start_unpad[e_s])
    n_blk = -(-A // ROW_BLOCK) + N_EXPERTS
    A_pad = n_blk * ROW_BLOCK
    x_pad = jnp.zeros((A_pad, D), bf16).at[dest].set(xf[tok_s].astype(bf16))
    tok_pad = jnp.zeros((A_pad,), jnp.int32).at[dest].set(tok_s.astype(jnp.int32))
    w_pad = jnp.zeros((A_pad,), f32).at[dest].set(w_s)
    blk_expert = jnp.minimum(jnp.searchsorted(end_pad, jnp.arange(n_blk) * ROW_BLOCK, side='right'),
                             N_EXPERTS - 1).astype(jnp.int32)
    y_pad = moe_experts(x_pad, blk_expert, w_gate.astype(bf16), w_up.astype(bf16), w_down.astype(bf16))
    y = jnp.zeros((M, D), f32).at[tok_pad].add(w_pad[:, None] * y_pad)
    return y.reshape(B, S, D)


def hybrid_mixer(hn, w_in, w_out, ck_w1, ck_w2, ck_pe, cv_w1, cv_w2, cv_pe, dn_conv, dn_a_log, dn_dt_bias,
                 dn_norm, rw_mu, rw_w0, rw_w2, rw_a0, rw_a2, rw_g2, rw_kk, rw_ka, rw_rk, rw_ln_g, rw_ln_b):
    B, S, D = hn.shape
    pad = (-P_TOTAL) % 1024
    w_in_p = jnp.pad(w_in.astype(bf16), ((0, 0), (0, pad)))
    proj = matmul(hn.reshape(B * S, D).astype(bf16), w_in_p, name="in_proj")[:, :P_TOTAL].reshape(B, S, P_TOTAL)
    p_nsa, p_ret, p_dn, p_rw = split_cols(proj, (NSA_COLS, RET_COLS, DN_COLS, RW_COLS))
    o_nsa = nsa_mixer(*split_cols(p_nsa, (D_GRP,) + (NSA_KV_W,) * 6 + (3 * NSA_HEADS,)),
                      ck_w1, ck_w2, ck_pe, cv_w1, cv_w2, cv_pe)
    o_ret = retention_mixer(*split_cols(p_ret, (D_GRP,) * 4))
    o_dn = deltanet_mixer(*split_cols(p_dn, (D_GRP,) * 3 + (DN_HEADS, DN_HEADS, D_GRP)),
                          dn_conv, dn_a_log, dn_dt_bias, dn_norm)
    o_rw = rwkv7_mixer(p_rw, rw_mu, rw_w0, rw_w2, rw_a0, rw_a2, rw_g2, rw_kk, rw_ka, rw_rk, rw_ln_g, rw_ln_b)
    o = jnp.concatenate([o_nsa, o_ret, o_dn, o_rw], -1).astype(bf16)
    return matmul(o.reshape(B * S, D_MIX), w_out.astype(bf16), name="out_proj").reshape(B, S, D)


def kernel(x, p, norm_mix, w_in, w_out, nsa_ck_w1, nsa_ck_w2, nsa_ck_pe, nsa_cv_w1, nsa_cv_w2, nsa_cv_pe, dn_conv, dn_a_log, dn_dt_bias, dn_norm, rw_mu, rw_w0, rw_w2, rw_a0, rw_a2, rw_g2, rw_kk, rw_ka, rw_rk, rw_ln_g, rw_ln_b, norm_moe, moe_router_grp, moe_router_grp_b, moe_router_exp, moe_router_exp_b, moe_w_gate, moe_w_up, moe_w_down, norm_ple, ple_w, ple_gate, norm_final):
    h = x
    B, S, D = x.shape
    for i in range(DEPTH):
        hn = rms_norm(h, norm_mix[i])
        h = h + hybrid_mixer(hn, w_in[i], w_out[i], nsa_ck_w1[i], nsa_ck_w2[i], nsa_ck_pe[i], nsa_cv_w1[i],
                             nsa_cv_w2[i], nsa_cv_pe[i], dn_conv[i], dn_a_log[i], dn_dt_bias[i], dn_norm[i],
                             rw_mu[i], rw_w0[i], rw_w2[i], rw_a0[i], rw_a2[i], rw_g2[i], rw_kk[i], rw_ka[i],
                             rw_rk[i], rw_ln_g[i], rw_ln_b[i])
        hn = rms_norm(h, norm_moe[i])
        h = h + hier_moe(hn, moe_router_grp[i], moe_router_grp_b[i], moe_router_exp[i], moe_router_exp_b[i],
                         moe_w_gate[i], moe_w_up[i], moe_w_down[i])
        hn = rms_norm(h, norm_ple[i])
        gate = matmul(hn.reshape(B * S, D).astype(bf16), ple_gate[i].astype(bf16), name="ple_gate").reshape(B, S, D)
        pw = matmul(p[i].reshape(B * S, PLE_DIM).astype(bf16), ple_w[i].astype(bf16), name="ple_in").reshape(B, S, D)
        h = h + pw * jax.nn.sigmoid(gate)
    return rms_norm(h, norm_final)
```

```python
import functools

import jax
import jax.numpy as jnp
import numpy as np
from jax import lax
from jax.experimental import pallas as pl
from jax.experimental.pallas import tpu as pltpu

D_MODEL = 4096
BATCH = 2
SEQ = 4096
DEPTH = 2
f32 = jnp.float32
bf16 = jnp.bfloat16
D_MIX = D_MODEL
D_GRP = D_MIX // 4
HEAD_DIM = 128
NORM_EPS = 1e-6
NSA_HEADS = D_GRP // HEAD_DIM
NSA_KV_HEADS = 2
NSA_GROUP = NSA_HEADS // NSA_KV_HEADS
NSA_KV_W = NSA_KV_HEADS * HEAD_DIM
CMP_LEN = 32
CMP_STRIDE = 16
SEL_LEN = 64
N_SEL = 16
N_LOCAL_FORCED = 2
FORCED_SCORE = 1e4
WINDOW = 512
SEL_QBLK = 64
WIN_QBLK = 128
ROPE_THETA = 500000.0
ROPE_DIMS = HEAD_DIM // 4
RET_HEADS = D_GRP // HEAD_DIM
RET_CHUNK = 128
RET_THETA = 10000.0
DN_HEADS = D_GRP // HEAD_DIM
DN_CHUNK = 64
CONV_WIDTH = 4
RW_HEAD_DIM = 64
RW_HEADS = D_GRP // RW_HEAD_DIM
RW_DECAY_LORA = 64
RW_AAA_LORA = 64
RW_GATE_LORA = 160
RW_LN_EPS = 64e-5
N_GROUPS = 4
EXPERTS_PER_GROUP = 8
N_EXPERTS = N_GROUPS * EXPERTS_PER_GROUP
TOP_K = 2
D_EXPERT = 768
ROW_BLOCK = 256
PLE_DIM = 256
NSA_COLS = D_GRP + 6 * NSA_KV_W + 3 * NSA_HEADS
RET_COLS = 4 * D_GRP
DN_COLS = 4 * D_GRP + 2 * DN_HEADS
RW_COLS = 3 * D_GRP + RW_DECAY_LORA + RW_AAA_LORA + RW_GATE_LORA
P_TOTAL = NSA_COLS + RET_COLS + DN_COLS + RW_COLS

RW_LORA = RW_DECAY_LORA + RW_AAA_LORA + RW_GATE_LORA
RW_LORA_PAD = 384
LANES = 128
SUB = 8

OFF_RW = 0
OFF_RET = 3 * D_GRP
OFF_DN = OFF_RET + 4 * D_GRP
OFF_NSA_Q = OFF_DN + 4 * D_GRP
OFF_NSA_KV = OFF_NSA_Q + D_GRP
OFF_LORA = OFF_NSA_KV + 6 * NSA_KV_W
OFF_SMALL = OFF_LORA + RW_LORA_PAD
P_PAD = OFF_SMALL + LANES
assert P_PAD == 14336 and OFF_LORA % RW_LORA_PAD == 0

VMEM_LIMIT = 56 * 1024 * 1024


def split_cols(x, sizes):
    return jnp.split(x, np.cumsum(sizes)[:-1].tolist(), axis=-1)


def rms_norm(x, g, eps=NORM_EPS):
    xf = x.astype(f32)
    y = xf * lax.rsqrt(jnp.mean(xf * xf, -1, keepdims=True) + eps)
    return (y * g.astype(f32)).astype(x.dtype)


def l2norm(x, eps=1e-6):
    return x * lax.rsqrt(jnp.sum(x * x, -1, keepdims=True) + eps)


def head_layer_norm(x, eps):
    mu = jnp.mean(x, -1, keepdims=True)
    return (x - mu) * lax.rsqrt(jnp.var(x, -1, keepdims=True) + eps)


def masked_softmax(s, mask):
    s = jnp.where(mask, s.astype(f32), -jnp.inf)
    m = jnp.max(s, -1, keepdims=True)
    m = jnp.where(jnp.isfinite(m), m, 0.0)
    e = jnp.where(mask, jnp.exp(s - m), 0.0)
    return e / jnp.maximum(jnp.sum(e, -1, keepdims=True), jnp.finfo(f32).tiny)


def rotary(x, freqs, rot_dims):
    S = x.shape[1]
    half = rot_dims // 2
    ang = jnp.arange(S, dtype=f32)[:, None] * freqs[None, :]
    ang = ang.reshape((S,) + (1,) * (x.ndim - 3) + (half,))
    cos, sin = jnp.cos(ang), jnp.sin(ang)
    x1 = x[..., :half].astype(f32)
    x2 = x[..., half:rot_dims].astype(f32)
    out = jnp.concatenate([x1 * cos - x2 * sin, x2 * cos + x1 * sin, x[..., rot_dims:].astype(f32)], -1)
    return out.astype(x.dtype)


def cmp_sel_overlap(S):
    n_cmp = (S - CMP_LEN) // CMP_STRIDE + 1
    c0 = np.arange(n_cmp)[:, None] * CMP_STRIDE
    s0 = np.arange(S // SEL_LEN)[None, :] * SEL_LEN
    ov = np.clip(np.minimum(c0 + CMP_LEN, s0 + SEL_LEN) - np.maximum(c0, s0), 0, None) / CMP_LEN
    return jnp.asarray(ov, f32)


def nsa_mixer(q, kc, vc, ks, vs, kw, vw, gate_logit, ck_w1, ck_w2, ck_pe, cv_w1, cv_w2, cv_pe):
    B, S, _ = q.shape
    G, R, Dh = NSA_KV_HEADS, NSA_GROUP, HEAD_DIM
    scale = Dh ** -0.5
    half = ROPE_DIMS // 2
    freqs = ROPE_THETA ** (-jnp.arange(half, dtype=f32) / half)
    q = rotary(q.reshape(B, S, G, R, Dh), freqs, ROPE_DIMS)
    kc, vc, ks, vs, kw, vw = [t.reshape(B, S, G, Dh) for t in (kc, vc, ks, vs, kw, vw)]
    kc, ks, kw = [rotary(t, freqs, ROPE_DIMS) for t in (kc, ks, kw)]
    pos = jnp.arange(S)
    n_cmp = (S - CMP_LEN) // CMP_STRIDE + 1
    blk_tok = np.arange(n_cmp)[:, None] * CMP_STRIDE + np.arange(CMP_LEN)[None, :]

    def compress(t, w1, w2, pe):
        blk = t[:, blk_tok] + pe[:, None, :]
        blk = blk.transpose(0, 3, 1, 2, 4).reshape(B, G, n_cmp, CMP_LEN * Dh)
        return jax.nn.silu(blk @ w1) @ w2

    k_cmp = compress(kc, ck_w1, ck_w2, ck_pe)
    v_cmp = compress(vc, cv_w1, cv_w2, cv_pe)
    s_cmp = jnp.einsum('bsgrd,bgnd->bgrsn', q, k_cmp) * scale
    cmp_mask = jnp.asarray(blk_tok[:, -1])[None, :] <= pos[:, None]
    p_cmp = masked_softmax(s_cmp, cmp_mask)
    o_cmp = jnp.einsum('bgrsn,bgnd->bsgrd', p_cmp, v_cmp.astype(f32))
    n_sel_blk = S // SEL_LEN
    k_top = min(N_SEL, n_sel_blk)
    imp = jnp.einsum('bgrsn,nj->bgsj', p_cmp, cmp_sel_overlap(S))
    blk = jnp.arange(n_sel_blk)[None, :]
    cur = (pos // SEL_LEN)[:, None]
    visible = blk <= cur
    forced = (blk == 0) | (visible & (blk > cur - N_LOCAL_FORCED))
    imp = jnp.where(forced, FORCED_SCORE, jnp.where(visible, imp, -1.0))
    sel_idx = lax.top_k(imp, k_top)[1]
    ks_blk = ks.reshape(B, n_sel_blk, SEL_LEN, G, Dh).transpose(0, 3, 1, 2, 4)
    vs_blk = vs.reshape(B, n_sel_blk, SEL_LEN, G, Dh).transpose(0, 3, 1, 2, 4)
    n_qb = S // SEL_QBLK
    q_b = q.reshape(B, n_qb, SEL_QBLK, G, R, Dh).transpose(1, 0, 2, 3, 4, 5)
    idx_b = sel_idx.reshape(B, G, n_qb, SEL_QBLK, k_top).transpose(2, 0, 1, 3, 4)
    qpos_b = pos.reshape(n_qb, SEL_QBLK)
    b_ix = jnp.arange(B)[:, None, None, None]
    g_ix = jnp.arange(G)[None, :, None, None]

    def sel_block(args):
        qb, ib, qp = args
        kb = ks_blk[b_ix, g_ix, ib].reshape(B, G, SEL_QBLK, k_top * SEL_LEN, Dh)
        vb = vs_blk[b_ix, g_ix, ib].reshape(B, G, SEL_QBLK, k_top * SEL_LEN, Dh)
        tok = (ib[..., None] * SEL_LEN + jnp.arange(SEL_LEN)).reshape(B, G, SEL_QBLK, k_top * SEL_LEN)
        s = jnp.einsum('bqgrd,bgqkd->bgrqk', qb, kb) * scale
        mask = tok[:, :, None] <= qp[None, None, None, :, None]
        pr = masked_softmax(s, mask)
        return jnp.einsum('bgrqk,bgqkd->bqgrd', pr, vb.astype(f32))

    o_sel = lax.map(sel_block, (q_b, idx_b, qpos_b))
    o_sel = o_sel.transpose(1, 0, 2, 3, 4, 5).reshape(B, S, G, R, Dh)
    n_wb = S // WIN_QBLK
    span = WINDOW + WIN_QBLK
    key_idx = np.arange(n_wb)[:, None] * WIN_QBLK + np.arange(span)[None, :]
    pad = ((0, 0), (WINDOW, 0), (0, 0), (0, 0))
    kb = jnp.pad(kw, pad)[:, key_idx]
    vb = jnp.pad(vw, pad)[:, key_idx]
    qb = q.reshape(B, n_wb, WIN_QBLK, G, R, Dh)
    s_win = jnp.einsum('bnqgrd,bnkgd->bgrnqk', qb, kb) * scale
    kpos = jnp.asarray(key_idx - WINDOW)[:, None, :]
    qpos = pos.reshape(n_wb, WIN_QBLK)[:, :, None]
    win_mask = (kpos >= 0) & (kpos <= qpos) & (kpos > qpos - WINDOW)
    p_win = masked_softmax(s_win, win_mask)
    o_win = jnp.einsum('bgrnqk,bnkgd->bnqgrd', p_win, vb.astype(f32)).reshape(B, S, G, R, Dh)
    gate = jax.nn.sigmoid(gate_logit.astype(f32)).reshape(B, S, G, R, 3)
    o = gate[..., 0:1] * o_cmp + gate[..., 1:2] * o_sel + gate[..., 2:3] * o_win
    return o.reshape(B, S, D_GRP)


def retention_mixer(q, k, v, g):
    B, S, _ = q.shape
    H, Dh, C = RET_HEADS, HEAD_DIM, RET_CHUNK
    n_c = S // C
    freqs = RET_THETA ** (-jnp.linspace(0.0, 1.0, Dh // 2, dtype=f32))
    heads = lambda t: t.reshape(B, S, H, Dh)
    q = rotary(heads(q), freqs, Dh).astype(f32)
    k = rotary(heads(k), freqs, Dh).astype(f32) * Dh ** -0.5
    v = heads(v).astype(f32)
    log_gamma = jnp.log1p(-(2.0 ** (-5.0 - jnp.arange(H, dtype=f32))))
    n = jnp.arange(C, dtype=f32)
    diff = n[:, None] - n[None, :]
    decay_in = jnp.where(diff >= 0, jnp.exp(log_gamma[:, None, None] * jnp.maximum(diff, 0.0)), 0.0)
    xi = jnp.exp(log_gamma[:, None] * (n + 1.0))
    zeta = jnp.exp(log_gamma[:, None] * (C - 1.0 - n))
    gamma_c = jnp.exp(log_gamma * C)
    chunks = lambda t: t.reshape(B, n_c, C, H, Dh).transpose(1, 0, 3, 2, 4)

    def step(state, inp):
        qc, kc, vc = inp
        inner = jnp.einsum('bhnd,bhmd->bhnm', qc, kc) * decay_in
        o = jnp.einsum('bhnm,bhme->bhne', inner, vc) + jnp.einsum('bhnd,bhde->bhne', qc, state) * xi[..., None]
        state = state * gamma_c[:, None, None] + jnp.einsum('bhmd,bhme->bhde', kc * zeta[..., None], vc)
        return state, o

    _, o = lax.scan(step, jnp.zeros((B, H, Dh, Dh), f32), (chunks(q), chunks(k), chunks(v)))
    o = head_layer_norm(o.transpose(1, 0, 3, 2, 4).reshape(B, S, H, Dh), NORM_EPS)
    return jax.nn.silu(g.astype(f32)) * o.reshape(B, S, D_GRP)


def deltanet_mixer(q, k, v, a, b, z, conv_w, a_log, dt_bias, norm_g):
    B, S, _ = q.shape
    H, Dh, C = DN_HEADS, HEAD_DIM, DN_CHUNK
    n_c = S // C
    qkv = jnp.concatenate([q, k, v], -1)
    qkv = lax.conv_general_dilated(qkv, conv_w[:, None, :], window_strides=(1,), padding=[(CONV_WIDTH - 1, 0)],
                                   dimension_numbers=('NWC', 'WIO', 'NWC'), feature_group_count=3 * D_GRP)
    qkv = jax.nn.silu(qkv.astype(f32))
    q, k, v = [t.reshape(B, S, H, Dh) for t in jnp.split(qkv, 3, -1)]
    q = l2norm(q) * Dh ** -0.5
    k = l2norm(k)
    beta = jax.nn.sigmoid(b.astype(f32))
    g = -jnp.exp(a_log.astype(f32)) * jax.nn.softplus(a.astype(f32) + dt_bias.astype(f32))
    chunks = lambda t: jnp.moveaxis(t.reshape((B, n_c, C) + t.shape[2:]), 3, 1)
    qc, kc, vc, bc = chunks(q), chunks(k), chunks(v), chunks(beta)
    gc = jnp.cumsum(chunks(g), -1)
    kbeta = kc * bc[..., None]
    vbeta = vc * bc[..., None]
    idx = jnp.arange(C)
    tril = idx[:, None] >= idx[None, :]
    strict = idx[:, None] > idx[None, :]
    decay = jnp.exp(jnp.where(tril, gc[..., :, None] - gc[..., None, :], -jnp.inf))
    m = jnp.where(strict, jnp.einsum('bhncd,bhnkd->bhnck', kbeta, kc) * decay, 0.0)
    eye = jnp.eye(C, dtype=f32)
    t_inv = lax.linalg.triangular_solve(eye + m, jnp.broadcast_to(eye, m.shape), left_side=True,
                                        lower=True, unit_diagonal=True)
    u = t_inv @ vbeta
    w = t_inv @ (kbeta * jnp.exp(gc)[..., None])
    attn_in = jnp.where(tril, jnp.einsum('bhncd,bhnkd->bhnck', qc, kc) * decay, 0.0)
    q_dec = qc * jnp.exp(gc)[..., None]
    k_dec = kc * jnp.exp(gc[..., -1:] - gc)[..., None]
    g_last = jnp.exp(gc[..., -1])

    def step(state, inp):
        u_i, w_i, a_i, qd_i, kd_i, gl_i = inp
        v_new = u_i - w_i @ state
        o = qd_i @ state + a_i @ v_new
        state = state * gl_i[..., None, None] + jnp.swapaxes(kd_i, -1, -2) @ v_new
        return state, o

    xs = tuple(jnp.moveaxis(t, 2, 0) for t in (u, w, attn_in, q_dec, k_dec, g_last))
    _, o = lax.scan(step, jnp.zeros((B, H, Dh, Dh), f32), xs)
    o = o.transpose(1, 0, 3, 2, 4).reshape(B, S, H, Dh)
    o = rms_norm(o, norm_g) * jax.nn.silu(z.astype(f32)).reshape(B, S, H, Dh)
    return o.reshape(B, S, D_GRP)


def _split2(x):
    h = x.astype(bf16)
    return h, (x - h.astype(f32)).astype(bf16)


def _split3(x):
    h = x.astype(bf16)
    r1 = x - h.astype(f32)
    m = r1.astype(bf16)
    return h, m, (r1 - m.astype(f32)).astype(bf16)


def _rwkv_prep_body(x_ref, xprev_ref, l_ref, lprev_ref, mu_ref, mul_ref, w0_ref, a0_ref, kkw_ref, ka_ref,
                    w2_ref, a2_ref, g2_ref, e_ref, r_ref, w_ref, k_ref, v_ref, kk_ref, b_ref, g_ref, *, tm, S):
    i = pl.program_id(0)
    first = (i * tm) % S == 0

    def token_mix(x, prev8, mu):
        rows = lax.broadcasted_iota(jnp.int32, x.shape, 0)
        prev_row = jnp.where(first, 0.0, prev8[SUB - 1:SUB, :])
        x_prev = jnp.where(rows == 0, prev_row, pltpu.roll(x, 1, axis=0))
        return x + (x_prev - x) * mu

    x = token_mix(x_ref[...], xprev_ref[...], mu_ref[...])
    lora = token_mix(l_ref[...], lprev_ref[...], mul_ref[...])
    r = x[:, 0:D_GRP]
    k = x[:, D_GRP:2 * D_GRP]
    v = x[:, 2 * D_GRP:3 * D_GRP]
    dw = jnp.dot(jnp.tanh(lora).astype(bf16), w2_ref[...], preferred_element_type=f32)
    da = jnp.dot(lora.astype(bf16), a2_ref[...], preferred_element_type=f32)
    g = jnp.dot(jax.nn.sigmoid(lora).astype(bf16), g2_ref[...], preferred_element_type=f32)
    w = -jax.nn.softplus(-(w0_ref[...] + dw)) - 0.5
    decay = jnp.exp(-jnp.exp(w))
    a = jax.nn.sigmoid(a0_ref[...] + da)
    kq = k * kkw_ref[...]
    sq_h, sq_l = _split2(kq * kq)
    ss = (jnp.dot(sq_h, e_ref[...], preferred_element_type=f32)
          + jnp.dot(sq_l, e_ref[...], preferred_element_type=f32))
    kk = kq * lax.rsqrt(ss + 1e-6)
    r_ref[...] = r
    w_ref[...] = decay
    k_ref[...] = k * (1.0 + (a - 1.0) * ka_ref[...])
    v_ref[...] = v
    kk_ref[...] = kk
    b_ref[...] = kk * a
    g_ref[...] = g


def rwkv_prep(proj, rkv_blk, lora_blk, mu, w0, w2, a0, a2, g2, k_k, k_a, S, tm=256):
    M = proj.shape[0]
    assert M % tm == 0 and S % tm == 0
    W3 = 3 * D_GRP
    mu_x = mu[:W3].reshape(1, W3)
    mu_l = jnp.pad(mu[W3:], (0, RW_LORA_PAD - RW_LORA)).reshape(1, RW_LORA_PAD)
    w2p = jnp.zeros((RW_LORA_PAD, D_GRP), f32).at[0:RW_DECAY_LORA].set(w2).astype(bf16)
    a2p = jnp.zeros((RW_LORA_PAD, D_GRP), f32).at[RW_DECAY_LORA:RW_DECAY_LORA + RW_AAA_LORA].set(a2).astype(bf16)
    g2p = jnp.zeros((RW_LORA_PAD, D_GRP), f32).at[RW_DECAY_LORA + RW_AAA_LORA:RW_LORA].set(g2).astype(bf16)
    hid = np.arange(D_GRP) // RW_HEAD_DIM
    e = jnp.asarray(hid[:, None] == hid[None, :], bf16)
    row = lambda t: t.reshape(1, D_GRP)
    vec = pl.BlockSpec((1, D_GRP), lambda i: (0, 0))
    wspec = pl.BlockSpec((RW_LORA_PAD, D_GRP), lambda i: (0, 0))
    ospec = pl.BlockSpec((tm, D_GRP), lambda i: (i, 0))
    prev_row_blk = lambda i: jnp.maximum(i * (tm // SUB) - 1, 0)
    return pl.pallas_call(
        functools.partial(_rwkv_prep_body, tm=tm, S=S),
        grid=(M // tm,),
        in_specs=[pl.BlockSpec((tm, W3), lambda i: (i, rkv_blk)),
                  pl.BlockSpec((SUB, W3), lambda i: (prev_row_blk(i), rkv_blk)),
                  pl.BlockSpec((tm, RW_LORA_PAD), lambda i: (i, lora_blk)),
                  pl.BlockSpec((SUB, RW_LORA_PAD), lambda i: (prev_row_blk(i), lora_blk)),
                  pl.BlockSpec((1, W3), lambda i: (0, 0)),
                  pl.BlockSpec((1, RW_LORA_PAD), lambda i: (0, 0)),
                  vec, vec, vec, vec, wspec, wspec, wspec,
                  pl.BlockSpec((D_GRP, D_GRP), lambda i: (0, 0))],
        out_specs=[ospec] * 7,
        out_shape=[jax.ShapeDtypeStruct((M, D_GRP), f32)] * 7,
        compiler_params=pltpu.CompilerParams(dimension_semantics=("parallel",), vmem_limit_bytes=VMEM_LIMIT),
        name="rwkv_prep",
    )(proj, proj, proj, proj, mu_x, mu_l, row(w0), row(a0), row(k_k), row(k_a), w2p, a2p, g2p, e)


def _rwkv_scan_body(r_ref, w_ref, k_ref, v_ref, kk_ref, b_ref, g_ref, lng_ref, lnb_ref, rk_ref, o_ref, s_ref,
                    *, G, T):
    tb = pl.program_id(2)

    @pl.when(tb == 0)
    def _():
        s_ref[...] = jnp.zeros_like(s_ref)

    Dh = RW_HEAD_DIM
    lo = lax.broadcasted_iota(jnp.int32, (Dh, LANES), 1) < Dh
    lo8 = lax.broadcasted_iota(jnp.int32, (SUB, LANES), 1) < Dh
    sub_iota = lax.broadcasted_iota(jnp.int32, (4 * SUB, LANES), 0)
    row_iota = lax.broadcasted_iota(jnp.int32, (SUB, LANES), 0)

    def seg_sum(x, m):
        s_lo = jnp.sum(jnp.where(m, x, 0.0), axis=-1, keepdims=True)
        s_hi = jnp.sum(jnp.where(m, 0.0, x), axis=-1, keepdims=True)
        return jnp.where(m, s_lo, s_hi)

    def group(gi, carry):
        t0 = pl.multiple_of(gi * SUB, SUB)
        states = [s_ref[p] for p in range(G)]
        cols = [slice(p * LANES, (p + 1) * LANES) for p in range(G)]
        r8b, vpts = [], []
        for p in range(G):
            vh, vm, vl = _split3(v_ref[0, pl.ds(t0, SUB), cols[p]])
            vp = jnp.concatenate([vh, vm, vl, jnp.zeros_like(vh)], axis=0)
            vpts.append(vp.T)
            r8b.append(r_ref[0, pl.ds(t0, SUB), cols[p]].astype(bf16))
        vcs = []
        for i in range(SUB):
            sel = (((sub_iota % SUB) == i) & (sub_iota < 3 * SUB)).astype(bf16)
            step_vcs = []
            for p in range(G):
                vc = jnp.dot(vpts[p], sel, preferred_element_type=f32)
                step_vcs.append(jnp.where(lo, vc[:Dh], vc[Dh:]))
            vcs.append(step_vcs)
        yacc = [jnp.zeros((SUB, LANES), f32) for _ in range(G)]
        tiles = [[ref[0, pl.ds(t0, SUB), cols[p]] for ref in (kk_ref, w_ref, b_ref, k_ref)] for p in range(G)]
        for i in range(SUB):
            for p in range(G):
                kk_r, w_r, b_r, k_r = [jnp.broadcast_to(t8[i:i + 1, :], (Dh, LANES)) for t8 in tiles[p]]
                S = states[p]
                sab = -seg_sum(S * kk_r, lo)
                states[p] = S * w_r + sab * b_r + vcs[i][p] * k_r
            for p in range(G):
                Sb = states[p].astype(bf16)
                zero = jnp.zeros_like(Sb)
                mt = jnp.concatenate([jnp.where(lo, Sb, zero), jnp.where(lo, zero, Sb)], axis=0)
                y8 = lax.dot_general(r8b[p], mt, (((1,), (1,)), ((), ())), preferred_element_type=f32)
                yacc[p] = jnp.where(row_iota == i, y8, yacc[p])
        for p in range(G):
            s_ref[p] = states[p]
            y = yacc[p]
            mean = seg_sum(y, lo8) * (1.0 / Dh)
            d = y - mean
            var = seg_sum(d * d, lo8) * (1.0 / Dh)
            yn = d * lax.rsqrt(var + RW_LN_EPS) * lng_ref[:, cols[p]] + lnb_ref[:, cols[p]]
            r8 = r_ref[0, pl.ds(t0, SUB), cols[p]]
            k8 = k_ref[0, pl.ds(t0, SUB), cols[p]]
            v8 = v_ref[0, pl.ds(t0, SUB), cols[p]]
            bonus = seg_sum(r8 * k8 * rk_ref[:, cols[p]], lo8) * v8
            o_ref[0, pl.ds(t0, SUB), cols[p]] = (yn + bonus) * g_ref[0, pl.ds(t0, SUB), cols[p]]
        return carry

    lax.fori_loop(0, T // SUB, group, 0)


def rwkv_scan(r, w, k, v, kk, b, g, ln_g, ln_b, r_k, G=8, T=256):
    B, S, D = r.shape
    n_pairs = D // LANES
    assert n_pairs % G == 0 and S % T == 0
    spec = pl.BlockSpec((1, T, G * LANES), lambda bi, pi, ti: (bi, ti, pi))
    vec = pl.BlockSpec((1, G * LANES), lambda bi, pi, ti: (0, pi))
    row = lambda t: t.reshape(1, D)
    return pl.pallas_call(
        functools.partial(_rwkv_scan_body, G=G, T=T),
        grid=(B, n_pairs // G, S // T),
        in_specs=[spec] * 7 + [vec] * 3,
        out_specs=spec,
        out_shape=jax.ShapeDtypeStruct((B, S, D), f32),
        scratch_shapes=[pltpu.VMEM((G, RW_HEAD_DIM, LANES), f32)],
        compiler_params=pltpu.CompilerParams(dimension_semantics=("parallel", "parallel", "arbitrary"),
                                             vmem_limit_bytes=VMEM_LIMIT),
        name="rwkv_scan",
    )(r, w, k, v, kk, b, g, row(ln_g), row(ln_b), row(r_k))


def rwkv7_mixer(proj, rkv_blk, lora_blk, mu, w0, w2, a0, a2, g2, k_k, k_a, r_k, ln_g, ln_b, B, S):
    outs = rwkv_prep(proj, rkv_blk, lora_blk, mu, w0, w2, a0, a2, g2, k_k, k_a, S)
    r, w, k, v, kk, b, g = [t.reshape(B, S, D_GRP) for t in outs]
    return rwkv_scan(r, w, k, v, kk, b, g, ln_g, ln_b, r_k.reshape(-1))


def _mm_body(a_ref, b_ref, o_ref, acc_ref):
    k = pl.program_id(2)

    @pl.when(k == 0)
    def _():
        acc_ref[...] = jnp.zeros_like(acc_ref)

    acc_ref[...] += jnp.dot(a_ref[...].astype(bf16), b_ref[...].astype(bf16), preferred_element_type=f32)

    @pl.when(k == pl.num_programs(2) - 1)
    def _():
        o_ref[...] = acc_ref[...].astype(o_ref.dtype)


def matmul(a, b, out_dtype=f32, tm=1024, tn=1024, tk=512, name="matmul"):
    M, K = a.shape
    _, N = b.shape
    tm, tn, tk = min(tm, M), min(tn, N), min(tk, K)
    assert M % tm == 0 and N % tn == 0 and K % tk == 0, (M, N, K, tm, tn, tk)
    return pl.pallas_call(
        _mm_body,
        grid=(M // tm, N // tn, K // tk),
        in_specs=[pl.BlockSpec((tm, tk), lambda i, j, k: (i, k)),
                  pl.BlockSpec((tk, tn), lambda i, j, k: (k, j))],
        out_specs=pl.BlockSpec((tm, tn), lambda i, j, k: (i, j)),
        out_shape=jax.ShapeDtypeStruct((M, N), out_dtype),
        scratch_shapes=[pltpu.VMEM((tm, tn), f32)],
        compiler_params=pltpu.CompilerParams(
            dimension_semantics=("parallel", "parallel", "arbitrary"), vmem_limit_bytes=VMEM_LIMIT),
        name=name,
    )(a, b)


def _moe_body(be_ref, x_ref, wg_ref, wu_ref, wd_ref, o_ref):
    f = pl.program_id(1)
    x = x_ref[...]
    hg = jnp.dot(x, wg_ref[0], preferred_element_type=f32)
    hu = jnp.dot(x, wu_ref[0], preferred_element_type=f32)
    h = (hg * jax.nn.sigmoid(hg) * hu).astype(bf16)
    y = jnp.dot(h, wd_ref[0], preferred_element_type=f32)

    @pl.when(f == 0)
    def _():
        o_ref[...] = y

    @pl.when(f != 0)
    def _():
        o_ref[...] += y


def moe_experts(x_pad, blk_expert, wg, wu, wd, tf=256):
    A_pad, D = x_pad.shape
    n_blk = A_pad // ROW_BLOCK
    nf = D_EXPERT // tf
    grid_spec = pltpu.PrefetchScalarGridSpec(
        num_scalar_prefetch=1,
        grid=(n_blk, nf),
        in_specs=[pl.BlockSpec((ROW_BLOCK, D), lambda i, f, be: (i, 0)),
                  pl.BlockSpec((1, D, tf), lambda i, f, be: (be[i], 0, f)),
                  pl.BlockSpec((1, D, tf), lambda i, f, be: (be[i], 0, f)),
                  pl.BlockSpec((1, tf, D), lambda i, f, be: (be[i], f, 0))],
        out_specs=pl.BlockSpec((ROW_BLOCK, D), lambda i, f, be: (i, 0)),
    )
    return pl.pallas_call(
        _moe_body,
        grid_spec=grid_spec,
        out_shape=jax.ShapeDtypeStruct((A_pad, D), f32),
        compiler_params=pltpu.CompilerParams(
            dimension_semantics=("parallel", "arbitrary"), vmem_limit_bytes=VMEM_LIMIT),
        name="moe_experts",
    )(blk_expert, x_pad, wg, wu, wd)


def hier_moe(xn, router_grp, router_grp_b, router_exp, router_exp_b, w_gate, w_up, w_down):
    B, S, D = xn.shape
    M = B * S
    xf = xn.reshape(M, D)
    grp_p = jax.nn.softmax((xf @ router_grp).astype(f32) + router_grp_b.astype(f32), -1)
    g_idx = jnp.argmax(grp_p, -1)
    g_w = jnp.take_along_axis(grp_p, g_idx[:, None], -1)
    exp_logits = ((xf @ router_exp).astype(f32) + router_exp_b.astype(f32)).reshape(M, N_GROUPS, EXPERTS_PER_GROUP)
    exp_logits = jnp.take_along_axis(exp_logits, g_idx[:, None, None], 1)[:, 0]
    top_p, top_i = lax.top_k(jax.nn.softmax(exp_logits, -1), TOP_K)
    weights = g_w * top_p / jnp.sum(top_p, -1, keepdims=True)
    expert_ids = g_idx[:, None] * EXPERTS_PER_GROUP + top_i
    A = M * TOP_K
    e_flat = expert_ids.reshape(A)
    tok_flat = jnp.repeat(jnp.arange(M), TOP_K)
    w_flat = weights.reshape(A)
    order = jnp.argsort(e_flat)
    e_s, tok_s, w_s = e_flat[order], tok_flat[order], w_flat[order]
    counts = jnp.bincount(e_flat, length=N_EXPERTS)
    padded = (counts + ROW_BLOCK - 1) // ROW_BLOCK * ROW_BLOCK
    start_unpad = jnp.cumsum(counts) - counts
    end_pad = jnp.cumsum(padded)
    start_pad = end_pad - padded
    dest = start_pad[e_s] + (jnp.arange(A) - start_unpad[e_s])
    n_blk = -(-A // ROW_BLOCK) + N_EXPERTS
    A_pad = n_blk * ROW_BLOCK
    x_pad = jnp.zeros((A_pad, D), bf16).at[dest].set(xf[tok_s].astype(bf16))
    tok_pad = jnp.zeros((A_pad,), jnp.int32).at[dest].set(tok_s.astype(jnp.int32))
    w_pad = jnp.zeros((A_pad,), f32).at[dest].set(w_s)
    blk_expert = jnp.minimum(jnp.searchsorted(end_pad, jnp.arange(n_blk) * ROW_BLOCK, side='right'),
                             N_EXPERTS - 1).astype(jnp.int32)
    y_pad = moe_experts(x_pad, blk_expert, w_gate.astype(bf16), w_up.astype(bf16), w_down.astype(bf16))
    y = jnp.zeros((M, D), f32).at[tok_pad].add(w_pad[:, None] * y_pad)
    return y.reshape(B, S, D)


def pack_w_in(w_in):
    w = w_in.astype(bf16)
    n0, r0, d0, w0 = 0, NSA_COLS, NSA_COLS + RET_COLS, NSA_COLS + RET_COLS + DN_COLS
    z = lambda n: jnp.zeros((w.shape[0], n), bf16)
    gate = w[:, n0 + D_GRP + 6 * NSA_KV_W:r0]
    dn_ab = w[:, d0 + 3 * D_GRP:d0 + 3 * D_GRP + 2 * DN_HEADS]
    return jnp.concatenate([
        w[:, w0:w0 + 3 * D_GRP],
        w[:, r0:d0],
        w[:, d0:d0 + 3 * D_GRP], w[:, d0 + 3 * D_GRP + 2 * DN_HEADS:w0],
        w[:, n0:n0 + D_GRP + 6 * NSA_KV_W],
        w[:, w0 + 3 * D_GRP:], z(RW_LORA_PAD - RW_LORA),
        gate, dn_ab, z(LANES - gate.shape[1] - dn_ab.shape[1]),
    ], axis=1)


def hybrid_mixer(hn, w_in, w_out, ck_w1, ck_w2, ck_pe, cv_w1, cv_w2, cv_pe, dn_conv, dn_a_log, dn_dt_bias,
                 dn_norm, rw_mu, rw_w0, rw_w2, rw_a0, rw_a2, rw_g2, rw_kk, rw_ka, rw_rk, rw_ln_g, rw_ln_b):
    B, S, D = hn.shape
    proj = matmul(hn.reshape(B * S, D).astype(bf16), pack_w_in(w_in), name="in_proj")
    p3 = proj.reshape(B, S, P_PAD)
    cols = lambda off, n: p3[:, :, off:off + n]
    kv = [cols(OFF_NSA_KV + j * NSA_KV_W, NSA_KV_W) for j in range(6)]
    o_nsa = nsa_mixer(cols(OFF_NSA_Q, D_GRP), *kv, cols(OFF_SMALL, 3 * NSA_HEADS),
                      ck_w1, ck_w2, ck_pe, cv_w1, cv_w2, cv_pe)
    o_ret = retention_mixer(*[cols(OFF_RET + j * D_GRP, D_GRP) for j in range(4)])
    o_dn = deltanet_mixer(cols(OFF_DN, D_GRP), cols(OFF_DN + D_GRP, D_GRP), cols(OFF_DN + 2 * D_GRP, D_GRP),
                          cols(OFF_SMALL + 3 * NSA_HEADS, DN_HEADS), cols(OFF_SMALL + 3 * NSA_HEADS + DN_HEADS, DN_HEADS),
                          cols(OFF_DN + 3 * D_GRP, D_GRP), dn_conv, dn_a_log, dn_dt_bias, dn_norm)
    o_rw = rwkv7_mixer(proj, OFF_RW // (3 * D_GRP), OFF_LORA // RW_LORA_PAD, rw_mu, rw_w0, rw_w2, rw_a0, rw_a2,
                       rw_g2, rw_kk, rw_ka, rw_rk, rw_ln_g, rw_ln_b, B, S)
    o = jnp.concatenate([o_nsa, o_ret, o_dn, o_rw], -1).astype(bf16)
    return matmul(o.reshape(B * S, D_MIX), w_out.astype(bf16), name="out_proj").reshape(B, S, D)


def kernel(x, p, norm_mix, w_in, w_out, nsa_ck_w1, nsa_ck_w2, nsa_ck_pe, nsa_cv_w1, nsa_cv_w2, nsa_cv_pe, dn_conv, dn_a_log, dn_dt_bias, dn_norm, rw_mu, rw_w0, rw_w2, rw_a0, rw_a2, rw_g2, rw_kk, rw_ka, rw_rk, rw_ln_g, rw_ln_b, norm_moe, moe_router_grp, moe_router_grp_b, moe_router_exp, moe_router_exp_b, moe_w_gate, moe_w_up, moe_w_down, norm_ple, ple_w, ple_gate, norm_final):
    h = x
    B, S, D = x.shape
    for i in range(DEPTH):
        hn = rms_norm(h, norm_mix[i])
        h = h + hybrid_mixer(hn, w_in[i], w_out[i], nsa_ck_w1[i], nsa_ck_w2[i], nsa_ck_pe[i], nsa_cv_w1[i],
                             nsa_cv_w2[i], nsa_cv_pe[i], dn_conv[i], dn_a_log[i], dn_dt_bias[i], dn_norm[i],
                             rw_mu[i], rw_w0[i], rw_w2[i], rw_a0[i], rw_a2[i], rw_g2[i], rw_kk[i], rw_ka[i],
                             rw_rk[i], rw_ln_g[i], rw_ln_b[i])
        hn = rms_norm(h, norm_moe[i])
        h = h + hier_moe(hn, moe_router_grp[i], moe_router_grp_b[i], moe_router_exp[i], moe_router_exp_b[i],
                         moe_w_gate[i], moe_w_up[i], moe_w_down[i])
        hn = rms_norm(h, norm_ple[i])
        gate = matmul(hn.reshape(B * S, D).astype(bf16), ple_gate[i].astype(bf16), name="ple_gate").reshape(B, S, D)
        pw = matmul(p[i].reshape(B * S, PLE_DIM).astype(bf16), ple_w[i].astype(bf16), name="ple_in").reshape(B, S, D)
        h = h + pw * jax.nn.sigmoid(gate)
    return rms_norm(h, norm_final)
```

```python
import functools

import jax
import jax.numpy as jnp
import numpy as np
from jax import lax
from jax.experimental import pallas as pl
from jax.experimental.pallas import tpu as pltpu

D_MODEL = 4096
BATCH = 2
SEQ = 4096
DEPTH = 2
f32 = jnp.float32
bf16 = jnp.bfloat16
D_MIX = D_MODEL
D_GRP = D_MIX // 4
HEAD_DIM = 128
NORM_EPS = 1e-6
NSA_HEADS = D_GRP // HEAD_DIM
NSA_KV_HEADS = 2
NSA_GROUP = NSA_HEADS // NSA_KV_HEADS
NSA_KV_W = NSA_KV_HEADS * HEAD_DIM
CMP_LEN = 32
CMP_STRIDE = 16
SEL_LEN = 64
N_SEL = 16
N_LOCAL_FORCED = 2
FORCED_SCORE = 1e4
WINDOW = 512
SEL_QBLK = 64
WIN_QBLK = 128
ROPE_THETA = 500000.0
ROPE_DIMS = HEAD_DIM // 4
RET_HEADS = D_GRP // HEAD_DIM
RET_CHUNK = 128
RET_THETA = 10000.0
DN_HEADS = D_GRP // HEAD_DIM
DN_CHUNK = 64
CONV_WIDTH = 4
RW_HEAD_DIM = 64
RW_HEADS = D_GRP // RW_HEAD_DIM
RW_DECAY_LORA = 64
RW_AAA_LORA = 64
RW_GATE_LORA = 160
RW_LN_EPS = 64e-5
N_GROUPS = 4
EXPERTS_PER_GROUP = 8
N_EXPERTS = N_GROUPS * EXPERTS_PER_GROUP
TOP_K = 2
D_EXPERT = 768
ROW_BLOCK = 256
PLE_DIM = 256
NSA_COLS = D_GRP + 6 * NSA_KV_W + 3 * NSA_HEADS
RET_COLS = 4 * D_GRP
DN_COLS = 4 * D_GRP + 2 * DN_HEADS
RW_COLS = 3 * D_GRP + RW_DECAY_LORA + RW_AAA_LORA + RW_GATE_LORA
P_TOTAL = NSA_COLS + RET_COLS + DN_COLS + RW_COLS

RW_LORA = RW_DECAY_LORA + RW_AAA_LORA + RW_GATE_LORA
RW_LORA_PAD = 384
LANES = 128
SUB = 8

OFF_RW = 0
OFF_RET = 3 * D_GRP
OFF_DN = OFF_RET + 4 * D_GRP
OFF_NSA_Q = OFF_DN + 4 * D_GRP
OFF_NSA_KV = OFF_NSA_Q + D_GRP
OFF_LORA = OFF_NSA_KV + 6 * NSA_KV_W
OFF_SMALL = OFF_LORA + RW_LORA_PAD
P_PAD = OFF_SMALL + LANES
assert P_PAD == 14336 and OFF_LORA % RW_LORA_PAD == 0

VMEM_LIMIT = 56 * 1024 * 1024


def split_cols(x, sizes):
    return jnp.split(x, np.cumsum(sizes)[:-1].tolist(), axis=-1)


def rms_norm(x, g, eps=NORM_EPS):
    xf = x.astype(f32)
    y = xf * lax.rsqrt(jnp.mean(xf * xf, -1, keepdims=True) + eps)
    return (y * g.astype(f32)).astype(x.dtype)


def l2norm(x, eps=1e-6):
    return x * lax.rsqrt(jnp.sum(x * x, -1, keepdims=True) + eps)


def head_layer_norm(x, eps):
    mu = jnp.mean(x, -1, keepdims=True)
    return (x - mu) * lax.rsqrt(jnp.var(x, -1, keepdims=True) + eps)


def masked_softmax(s, mask):
    s = jnp.where(mask, s.astype(f32), -jnp.inf)
    m = jnp.max(s, -1, keepdims=True)
    m = jnp.where(jnp.isfinite(m), m, 0.0)
    e = jnp.where(mask, jnp.exp(s - m), 0.0)
    return e / jnp.maximum(jnp.sum(e, -1, keepdims=True), jnp.finfo(f32).tiny)


def rotary(x, freqs, rot_dims):
    S = x.shape[1]
    half = rot_dims // 2
    ang = jnp.arange(S, dtype=f32)[:, None] * freqs[None, :]
    ang = ang.reshape((S,) + (1,) * (x.ndim - 3) + (half,))
    cos, sin = jnp.cos(ang), jnp.sin(ang)
    x1 = x[..., :half].astype(f32)
    x2 = x[..., half:rot_dims].astype(f32)
    out = jnp.concatenate([x1 * cos - x2 * sin, x2 * cos + x1 * sin, x[..., rot_dims:].astype(f32)], -1)
    return out.astype(x.dtype)


def cmp_sel_overlap(S):
    n_cmp = (S - CMP_LEN) // CMP_STRIDE + 1
    c0 = np.arange(n_cmp)[:, None] * CMP_STRIDE
    s0 = np.arange(S // SEL_LEN)[None, :] * SEL_LEN
    ov = np.clip(np.minimum(c0 + CMP_LEN, s0 + SEL_LEN) - np.maximum(c0, s0), 0, None) / CMP_LEN
    return jnp.asarray(ov, f32)


def nsa_mixer(q, kc, vc, ks, vs, kw, vw, gate_logit, ck_w1, ck_w2, ck_pe, cv_w1, cv_w2, cv_pe):
    B, S, _ = q.shape
    G, R, Dh = NSA_KV_HEADS, NSA_GROUP, HEAD_DIM
    scale = Dh ** -0.5
    half = ROPE_DIMS // 2
    freqs = ROPE_THETA ** (-jnp.arange(half, dtype=f32) / half)
    q = rotary(q.reshape(B, S, G, R, Dh), freqs, ROPE_DIMS)
    kc, vc, ks, vs, kw, vw = [t.reshape(B, S, G, Dh) for t in (kc, vc, ks, vs, kw, vw)]
    kc, ks, kw = [rotary(t, freqs, ROPE_DIMS) for t in (kc, ks, kw)]
    pos = jnp.arange(S)
    n_cmp = (S - CMP_LEN) // CMP_STRIDE + 1
    blk_tok = np.arange(n_cmp)[:, None] * CMP_STRIDE + np.arange(CMP_LEN)[None, :]

    def compress(t, w1, w2, pe):
        blk = t[:, blk_tok] + pe[:, None, :]
        blk = blk.transpose(0, 3, 1, 2, 4).reshape(B, G, n_cmp, CMP_LEN * Dh)
        return jax.nn.silu(blk @ w1) @ w2

    k_cmp = compress(kc, ck_w1, ck_w2, ck_pe)
    v_cmp = compress(vc, cv_w1, cv_w2, cv_pe)
    s_cmp = jnp.einsum('bsgrd,bgnd->bgrsn', q, k_cmp) * scale
    cmp_mask = jnp.asarray(blk_tok[:, -1])[None, :] <= pos[:, None]
    p_cmp = masked_softmax(s_cmp, cmp_mask)
    o_cmp = jnp.einsum('bgrsn,bgnd->bsgrd', p_cmp, v_cmp.astype(f32))
    n_sel_blk = S // SEL_LEN
    k_top = min(N_SEL, n_sel_blk)
    imp = jnp.einsum('bgrsn,nj->bgsj', p_cmp, cmp_sel_overlap(S))
    blk = jnp.arange(n_sel_blk)[None, :]
    cur = (pos // SEL_LEN)[:, None]
    visible = blk <= cur
    forced = (blk == 0) | (visible & (blk > cur - N_LOCAL_FORCED))
    imp = jnp.where(forced, FORCED_SCORE, jnp.where(visible, imp, -1.0))
    sel_idx = lax.top_k(imp, k_top)[1]
    ks_blk = ks.reshape(B, n_sel_blk, SEL_LEN, G, Dh).transpose(0, 3, 1, 2, 4)
    vs_blk = vs.reshape(B, n_sel_blk, SEL_LEN, G, Dh).transpose(0, 3, 1, 2, 4)
    n_qb = S // SEL_QBLK
    q_b = q.reshape(B, n_qb, SEL_QBLK, G, R, Dh).transpose(1, 0, 2, 3, 4, 5)
    idx_b = sel_idx.reshape(B, G, n_qb, SEL_QBLK, k_top).transpose(2, 0, 1, 3, 4)
    qpos_b = pos.reshape(n_qb, SEL_QBLK)
    b_ix = jnp.arange(B)[:, None, None, None]
    g_ix = jnp.arange(G)[None, :, None, None]

    def sel_block(args):
        qb, ib, qp = args
        kb = ks_blk[b_ix, g_ix, ib].reshape(B, G, SEL_QBLK, k_top * SEL_LEN, Dh)
        vb = vs_blk[b_ix, g_ix, ib].reshape(B, G, SEL_QBLK, k_top * SEL_LEN, Dh)
        tok = (ib[..., None] * SEL_LEN + jnp.arange(SEL_LEN)).reshape(B, G, SEL_QBLK, k_top * SEL_LEN)
        s = jnp.einsum('bqgrd,bgqkd->bgrqk', qb, kb) * scale
        mask = tok[:, :, None] <= qp[None, None, None, :, None]
        pr = masked_softmax(s, mask)
        return jnp.einsum('bgrqk,bgqkd->bqgrd', pr, vb.astype(f32))

    o_sel = lax.map(sel_block, (q_b, idx_b, qpos_b))
    o_sel = o_sel.transpose(1, 0, 2, 3, 4, 5).reshape(B, S, G, R, Dh)
    n_wb = S // WIN_QBLK
    span = WINDOW + WIN_QBLK
    key_idx = np.arange(n_wb)[:, None] * WIN_QBLK + np.arange(span)[None, :]
    pad = ((0, 0), (WINDOW, 0), (0, 0), (0, 0))
    kb = jnp.pad(kw, pad)[:, key_idx]
    vb = jnp.pad(vw, pad)[:, key_idx]
    qb = q.reshape(B, n_wb, WIN_QBLK, G, R, Dh)
    s_win = jnp.einsum('bnqgrd,bnkgd->bgrnqk', qb, kb) * scale
    kpos = jnp.asarray(key_idx - WINDOW)[:, None, :]
    qpos = pos.reshape(n_wb, WIN_QBLK)[:, :, None]
    win_mask = (kpos >= 0) & (kpos <= qpos) & (kpos > qpos - WINDOW)
    p_win = masked_softmax(s_win, win_mask)
    o_win = jnp.einsum('bgrnqk,bnkgd->bnqgrd', p_win, vb.astype(f32)).reshape(B, S, G, R, Dh)
    gate = jax.nn.sigmoid(gate_logit.astype(f32)).reshape(B, S, G, R, 3)
    o = gate[..., 0:1] * o_cmp + gate[..., 1:2] * o_sel + gate[..., 2:3] * o_win
    return o.reshape(B, S, D_GRP)


def retention_mixer(q, k, v, g):
    B, S, _ = q.shape
    H, Dh, C = RET_HEADS, HEAD_DIM, RET_CHUNK
    n_c = S // C
    freqs = RET_THETA ** (-jnp.linspace(0.0, 1.0, Dh // 2, dtype=f32))
    heads = lambda t: t.reshape(B, S, H, Dh)
    q = rotary(heads(q), freqs, Dh).astype(f32)
    k = rotary(heads(k), freqs, Dh).astype(f32) * Dh ** -0.5
    v = heads(v).astype(f32)
    log_gamma = jnp.log1p(-(2.0 ** (-5.0 - jnp.arange(H, dtype=f32))))
    n = jnp.arange(C, dtype=f32)
    diff = n[:, None] - n[None, :]
    decay_in = jnp.where(diff >= 0, jnp.exp(log_gamma[:, None, None] * jnp.maximum(diff, 0.0)), 0.0)
    xi = jnp.exp(log_gamma[:, None] * (n + 1.0))
    zeta = jnp.exp(log_gamma[:, None] * (C - 1.0 - n))
    gamma_c = jnp.exp(log_gamma * C)
    chunks = lambda t: t.reshape(B, n_c, C, H, Dh).transpose(1, 0, 3, 2, 4)

    def step(state, inp):
        qc, kc, vc = inp
        inner = jnp.einsum('bhnd,bhmd->bhnm', qc, kc) * decay_in
        o = jnp.einsum('bhnm,bhme->bhne', inner, vc) + jnp.einsum('bhnd,bhde->bhne', qc, state) * xi[..., None]
        state = state * gamma_c[:, None, None] + jnp.einsum('bhmd,bhme->bhde', kc * zeta[..., None], vc)
        return state, o

    _, o = lax.scan(step, jnp.zeros((B, H, Dh, Dh), f32), (chunks(q), chunks(k), chunks(v)))
    o = head_layer_norm(o.transpose(1, 0, 3, 2, 4).reshape(B, S, H, Dh), NORM_EPS)
    return jax.nn.silu(g.astype(f32)) * o.reshape(B, S, D_GRP)


def deltanet_mixer(q, k, v, a, b, z, conv_w, a_log, dt_bias, norm_g):
    B, S, _ = q.shape
    H, Dh, C = DN_HEADS, HEAD_DIM, DN_CHUNK
    n_c = S // C
    qkv = jnp.concatenate([q, k, v], -1)
    qkv = lax.conv_general_dilated(qkv, conv_w[:, None, :], window_strides=(1,), padding=[(CONV_WIDTH - 1, 0)],
                                   dimension_numbers=('NWC', 'WIO', 'NWC'), feature_group_count=3 * D_GRP)
    qkv = jax.nn.silu(qkv.astype(f32))
    q, k, v = [t.reshape(B, S, H, Dh) for t in jnp.split(qkv, 3, -1)]
    q = l2norm(q) * Dh ** -0.5
    k = l2norm(k)
    beta = jax.nn.sigmoid(b.astype(f32))
    g = -jnp.exp(a_log.astype(f32)) * jax.nn.softplus(a.astype(f32) + dt_bias.astype(f32))
    chunks = lambda t: jnp.moveaxis(t.reshape((B, n_c, C) + t.shape[2:]), 3, 1)
    qc, kc, vc, bc = chunks(q), chunks(k), chunks(v), chunks(beta)
    gc = jnp.cumsum(chunks(g), -1)
    kbeta = kc * bc[..., None]
    vbeta = vc * bc[..., None]
    idx = jnp.arange(C)
    tril = idx[:, None] >= idx[None, :]
    strict = idx[:, None] > idx[None, :]
    decay = jnp.exp(jnp.where(tril, gc[..., :, None] - gc[..., None, :], -jnp.inf))
    m = jnp.where(strict, jnp.einsum('bhncd,bhnkd->bhnck', kbeta, kc) * decay, 0.0)
    eye = jnp.eye(C, dtype=f32)
    t_inv = lax.linalg.triangular_solve(eye + m, jnp.broadcast_to(eye, m.shape), left_side=True,
                                        lower=True, unit_diagonal=True)
    u = t_inv @ vbeta
    w = t_inv @ (kbeta * jnp.exp(gc)[..., None])
    attn_in = jnp.where(tril, jnp.einsum('bhncd,bhnkd->bhnck', qc, kc) * decay, 0.0)
    q_dec = qc * jnp.exp(gc)[..., None]
    k_dec = kc * jnp.exp(gc[..., -1:] - gc)[..., None]
    g_last = jnp.exp(gc[..., -1])

    def step(state, inp):
        u_i, w_i, a_i, qd_i, kd_i, gl_i = inp
        v_new = u_i - w_i @ state
        o = qd_i @ state + a_i @ v_new
        state = state * gl_i[..., None, None] + jnp.swapaxes(kd_i, -1, -2) @ v_new
        return state, o

    xs = tuple(jnp.moveaxis(t, 2, 0) for t in (u, w, attn_in, q_dec, k_dec, g_last))
    _, o = lax.scan(step, jnp.zeros((B, H, Dh, Dh), f32), xs)
    o = o.transpose(1, 0, 3, 2, 4).reshape(B, S, H, Dh)
    o = rms_norm(o, norm_g) * jax.nn.silu(z.astype(f32)).reshape(B, S, H, Dh)
    return o.reshape(B, S, D_GRP)


def _split2(x):
    h = x.astype(bf16)
    return h, (x - h.astype(f32)).astype(bf16)


def _split3(x):
    h = x.astype(bf16)
    r1 = x - h.astype(f32)
    m = r1.astype(bf16)
    return h, m, (r1 - m.astype(f32)).astype(bf16)


def _rwkv_prep_body(x_ref, xprev_ref, l_ref, lprev_ref, mu_ref, mul_ref, w0_ref, a0_ref, kkw_ref, ka_ref,
                    w2_ref, a2_ref, g2_ref, e_ref, r_ref, w_ref, k_ref, v_ref, kk_ref, b_ref, g_ref, *, tm, S):
    i = pl.program_id(0)
    first = (i * tm) % S == 0

    def token_mix(x, prev8, mu):
        rows = lax.broadcasted_iota(jnp.int32, x.shape, 0)
        prev_row = jnp.where(first, 0.0, prev8[SUB - 1:SUB, :])
        x_prev = jnp.where(rows == 0, prev_row, pltpu.roll(x, 1, axis=0))
        return x + (x_prev - x) * mu

    x = token_mix(x_ref[...], xprev_ref[...], mu_ref[...])
    lora = token_mix(l_ref[...], lprev_ref[...], mul_ref[...])
    r = x[:, 0:D_GRP]
    k = x[:, D_GRP:2 * D_GRP]
    v = x[:, 2 * D_GRP:3 * D_GRP]
    dw = jnp.dot(jnp.tanh(lora).astype(bf16), w2_ref[...], preferred_element_type=f32)
    da = jnp.dot(lora.astype(bf16), a2_ref[...], preferred_element_type=f32)
    g = jnp.dot(jax.nn.sigmoid(lora).astype(bf16), g2_ref[...], preferred_element_type=f32)
    w = -jax.nn.softplus(-(w0_ref[...] + dw)) - 0.5
    decay = jnp.exp(-jnp.exp(w))
    a = jax.nn.sigmoid(a0_ref[...] + da)
    kq = k * kkw_ref[...]
    sq_h, sq_l = _split2(kq * kq)
    ss = (jnp.dot(sq_h, e_ref[...], preferred_element_type=f32)
          + jnp.dot(sq_l, e_ref[...], preferred_element_type=f32))
    kk = kq * lax.rsqrt(ss + 1e-6)
    r_ref[...] = r
    w_ref[...] = decay
    k_ref[...] = k * (1.0 + (a - 1.0) * ka_ref[...])
    v_ref[...] = v
    kk_ref[...] = kk
    b_ref[...] = kk * a
    g_ref[...] = g


def rwkv_prep(proj, rkv_blk, lora_blk, mu, w0, w2, a0, a2, g2, k_k, k_a, S, tm=256):
    M = proj.shape[0]
    assert M % tm == 0 and S % tm == 0
    W3 = 3 * D_GRP
    mu_x = mu[:W3].reshape(1, W3)
    mu_l = jnp.pad(mu[W3:], (0, RW_LORA_PAD - RW_LORA)).reshape(1, RW_LORA_PAD)
    w2p = jnp.zeros((RW_LORA_PAD, D_GRP), f32).at[0:RW_DECAY_LORA].set(w2).astype(bf16)
    a2p = jnp.zeros((RW_LORA_PAD, D_GRP), f32).at[RW_DECAY_LORA:RW_DECAY_LORA + RW_AAA_LORA].set(a2).astype(bf16)
    g2p = jnp.zeros((RW_LORA_PAD, D_GRP), f32).at[RW_DECAY_LORA + RW_AAA_LORA:RW_LORA].set(g2).astype(bf16)
    hid = np.arange(D_GRP) // RW_HEAD_DIM
    e = jnp.asarray(hid[:, None] == hid[None, :], bf16)
    row = lambda t: t.reshape(1, D_GRP)
    vec = pl.BlockSpec((1, D_GRP), lambda i: (0, 0))
    wspec = pl.BlockSpec((RW_LORA_PAD, D_GRP), lambda i: (0, 0))
    ospec = pl.BlockSpec((tm, D_GRP), lambda i: (i, 0))
    prev_row_blk = lambda i: jnp.maximum(i * (tm // SUB) - 1, 0)
    return pl.pallas_call(
        functools.partial(_rwkv_prep_body, tm=tm, S=S),
        grid=(M // tm,),
        in_specs=[pl.BlockSpec((tm, W3), lambda i: (i, rkv_blk)),
                  pl.BlockSpec((SUB, W3), lambda i: (prev_row_blk(i), rkv_blk)),
                  pl.BlockSpec((tm, RW_LORA_PAD), lambda i: (i, lora_blk)),
                  pl.BlockSpec((SUB, RW_LORA_PAD), lambda i: (prev_row_blk(i), lora_blk)),
                  pl.BlockSpec((1, W3), lambda i: (0, 0)),
                  pl.BlockSpec((1, RW_LORA_PAD), lambda i: (0, 0)),
                  vec, vec, vec, vec, wspec, wspec, wspec,
                  pl.BlockSpec((D_GRP, D_GRP), lambda i: (0, 0))],
        out_specs=[ospec] * 7,
        out_shape=[jax.ShapeDtypeStruct((M, D_GRP), f32)] * 7,
        compiler_params=pltpu.CompilerParams(dimension_semantics=("parallel",), vmem_limit_bytes=VMEM_LIMIT),
        name="rwkv_prep",
    )(proj, proj, proj, proj, mu_x, mu_l, row(w0), row(a0), row(k_k), row(k_a), w2p, a2p, g2p, e)


def _rwkv_scan_body(r_ref, w_ref, k_ref, v_ref, kk_ref, b_ref, g_ref, lng_ref, lnb_ref, rk_ref, o_ref, s_ref,
                    *, G, T):
    tb = pl.program_id(2)

    @pl.when(tb == 0)
    def _():
        s_ref[...] = jnp.zeros_like(s_ref)

    Dh = RW_HEAD_DIM
    lo = lax.broadcasted_iota(jnp.int32, (Dh, LANES), 1) < Dh
    lo8 = lax.broadcasted_iota(jnp.int32, (SUB, LANES), 1) < Dh
    sub_iota = lax.broadcasted_iota(jnp.int32, (4 * SUB, LANES), 0)
    row_iota = lax.broadcasted_iota(jnp.int32, (SUB, LANES), 0)

    def seg_sum(x, m):
        s_lo = jnp.sum(jnp.where(m, x, 0.0), axis=-1, keepdims=True)
        s_hi = jnp.sum(jnp.where(m, 0.0, x), axis=-1, keepdims=True)
        return jnp.where(m, s_lo, s_hi)

    def group(gi, carry):
        t0 = pl.multiple_of(gi * SUB, SUB)
        states = [s_ref[p] for p in range(G)]
        cols = [slice(p * LANES, (p + 1) * LANES) for p in range(G)]
        r8b, vpts = [], []
        for p in range(G):
            vh, vm, vl = _split3(v_ref[0, pl.ds(t0, SUB), cols[p]])
            vp = jnp.concatenate([vh, vm, vl, jnp.zeros_like(vh)], axis=0)
            vpts.append(vp.T)
            r8b.append(r_ref[0, pl.ds(t0, SUB), cols[p]].astype(bf16))
        vcs = []
        for i in range(SUB):
            sel = (((sub_iota % SUB) == i) & (sub_iota < 3 * SUB)).astype(bf16)
            step_vcs = []
            for p in range(G):
                vc = jnp.dot(vpts[p], sel, preferred_element_type=f32)
                step_vcs.append(jnp.where(lo, vc[:Dh], vc[Dh:]))
            vcs.append(step_vcs)
        yacc = [jnp.zeros((SUB, LANES), f32) for _ in range(G)]
        tiles = [[ref[0, pl.ds(t0, SUB), cols[p]] for ref in (kk_ref, w_ref, b_ref, k_ref)] for p in range(G)]
        for i in range(SUB):
            for p in range(G):
                kk_r, w_r, b_r, k_r = [jnp.broadcast_to(t8[i:i + 1, :], (Dh, LANES)) for t8 in tiles[p]]
                S = states[p]
                sab = -seg_sum(S * kk_r, lo)
                states[p] = S * w_r + sab * b_r + vcs[i][p] * k_r
            for p in range(G):
                Sb = states[p].astype(bf16)
                zero = jnp.zeros_like(Sb)
                mt = jnp.concatenate([jnp.where(lo, Sb, zero), jnp.where(lo, zero, Sb)], axis=0)
                y8 = lax.dot_general(r8b[p], mt, (((1,), (1,)), ((), ())), preferred_element_type=f32)
                yacc[p] = jnp.where(row_iota == i, y8, yacc[p])
        for p in range(G):
            s_ref[p] = states[p]
            y = yacc[p]
            mean = seg_sum(y, lo8) * (1.0 / Dh)
            d = y - mean
            var = seg_sum(d * d, lo8) * (1.0 / Dh)
            yn = d * lax.rsqrt(var + RW_LN_EPS) * lng_ref[:, cols[p]] + lnb_ref[:, cols[p]]
            r8 = r_ref[0, pl.ds(t0, SUB), cols[p]]
            k8 = k_ref[0, pl.ds(t0, SUB), cols[p]]
            v8 = v_ref[0, pl.ds(t0, SUB), cols[p]]
            bonus = seg_sum(r8 * k8 * rk_ref[:, cols[p]], lo8) * v8
            o_ref[0, pl.ds(t0, SUB), cols[p]] = (yn + bonus) * g_ref[0, pl.ds(t0, SUB), cols[p]]
        return carry

    lax.fori_loop(0, T // SUB, group, 0)


def rwkv_scan(r, w, k, v, kk, b, g, ln_g, ln_b, r_k, G=8, T=256):
    B, S, D = r.shape
    n_pairs = D // LANES
    assert n_pairs % G == 0 and S % T == 0
    spec = pl.BlockSpec((1, T, G * LANES), lambda bi, pi, ti: (bi, ti, pi))
    vec = pl.BlockSpec((1, G * LANES), lambda bi, pi, ti: (0, pi))
    row = lambda t: t.reshape(1, D)
    return pl.pallas_call(
        functools.partial(_rwkv_scan_body, G=G, T=T),
        grid=(B, n_pairs // G, S // T),
        in_specs=[spec] * 7 + [vec] * 3,
        out_specs=spec,
        out_shape=jax.ShapeDtypeStruct((B, S, D), f32),
        scratch_shapes=[pltpu.VMEM((G, RW_HEAD_DIM, LANES), f32)],
        compiler_params=pltpu.CompilerParams(dimension_semantics=("parallel", "parallel", "arbitrary"),
                                             vmem_limit_bytes=VMEM_LIMIT),
        name="rwkv_scan",
    )(r, w, k, v, kk, b, g, row(ln_g), row(ln_b), row(r_k))


def rwkv7_mixer(proj, rkv_blk, lora_blk, mu, w0, w2, a0, a2, g2, k_k, k_a, r_k, ln_g, ln_b, B, S):
    outs = rwkv_prep(proj, rkv_blk, lora_blk, mu, w0, w2, a0, a2, g2, k_k, k_a, S)
    r, w, k, v, kk, b, g = [t.reshape(B, S, D_GRP) for t in outs]
    return rwkv_scan(r, w, k, v, kk, b, g, ln_g, ln_b, r_k.reshape(-1))


def _mm_body(a_ref, b_ref, o_ref, acc_ref):
    k = pl.program_id(2)

    @pl.when(k == 0)
    def _():
        acc_ref[...] = jnp.zeros_like(acc_ref)

    acc_ref[...] += jnp.dot(a_ref[...].astype(bf16), b_ref[...].astype(bf16), preferred_element_type=f32)

    @pl.when(k == pl.num_programs(2) - 1)
    def _():
        o_ref[...] = acc_ref[...].astype(o_ref.dtype)


def matmul(a, b, out_dtype=f32, tm=1024, tn=1024, tk=512, name="matmul"):
    M, K = a.shape
    _, N = b.shape
    tm, tn, tk = min(tm, M), min(tn, N), min(tk, K)
    assert M % tm == 0 and N % tn == 0 and K % tk == 0, (M, N, K, tm, tn, tk)
    return pl.pallas_call(
        _mm_body,
        grid=(M // tm, N // tn, K // tk),
        in_specs=[pl.BlockSpec((tm, tk), lambda i, j, k: (i, k)),
                  pl.BlockSpec((tk, tn), lambda i, j, k: (k, j))],
        out_specs=pl.BlockSpec((tm, tn), lambda i, j, k: (i, j)),
        out_shape=jax.ShapeDtypeStruct((M, N), out_dtype),
        scratch_shapes=[pltpu.VMEM((tm, tn), f32)],
        compiler_params=pltpu.CompilerParams(
            dimension_semantics=("parallel", "parallel", "arbitrary"), vmem_limit_bytes=VMEM_LIMIT),
        name=name,
    )(a, b)


def _moe_body(be_ref, x_ref, wg_ref, wu_ref, wd_ref, o_ref):
    f = pl.program_id(1)
    x = x_ref[...]
    hg = jnp.dot(x, wg_ref[0], preferred_element_type=f32)
    hu = jnp.dot(x, wu_ref[0], preferred_element_type=f32)
    h = (hg * jax.nn.sigmoid(hg) * hu).astype(bf16)
    y = jnp.dot(h, wd_ref[0], preferred_element_type=f32)

    @pl.when(f == 0)
    def _():
        o_ref[...] = y

    @pl.when(f != 0)
    def _():
        o_ref[...] += y


def moe_experts(x_pad, blk_expert, wg, wu, wd, tf=256):
    A_pad, D = x_pad.shape
    n_blk = A_pad // ROW_BLOCK
    nf = D_EXPERT // tf
    grid_spec = pltpu.PrefetchScalarGridSpec(
        num_scalar_prefetch=1,
        grid=(n_blk, nf),
        in_specs=[pl.BlockSpec((ROW_BLOCK, D), lambda i, f, be: (i, 0)),
                  pl.BlockSpec((1, D, tf), lambda i, f, be: (be[i], 0, f)),
                  pl.BlockSpec((1, D, tf), lambda i, f, be: (be[i], 0, f)),
                  pl.BlockSpec((1, tf, D), lambda i, f, be: (be[i], f, 0))],
        out_specs=pl.BlockSpec((ROW_BLOCK, D), lambda i, f, be: (i, 0)),
    )
    return pl.pallas_call(
        _moe_body,
        grid_spec=grid_spec,
        out_shape=jax.ShapeDtypeStruct((A_pad, D), f32),
        compiler_params=pltpu.CompilerParams(
            dimension_semantics=("parallel", "arbitrary"), vmem_limit_bytes=VMEM_LIMIT),
        name="moe_experts",
    )(blk_expert, x_pad, wg, wu, wd)


def hier_moe(xn, router_grp, router_grp_b, router_exp, router_exp_b, w_gate, w_up, w_down):
    B, S, D = xn.shape
    M = B * S
    xf = xn.reshape(M, D)
    grp_p = jax.nn.softmax((xf @ router_grp).astype(f32) + router_grp_b.astype(f32), -1)
    g_idx = jnp.argmax(grp_p, -1)
    g_w = jnp.take_along_axis(grp_p, g_idx[:, None], -1)
    exp_logits = ((xf @ router_exp).astype(f32) + router_exp_b.astype(f32)).reshape(M, N_GROUPS, EXPERTS_PER_GROUP)
    exp_logits = jnp.take_along_axis(exp_logits, g_idx[:, None, None], 1)[:, 0]
    top_p, top_i = lax.top_k(jax.nn.softmax(exp_logits, -1), TOP_K)
    weights = g_w * top_p / jnp.sum(top_p, -1, keepdims=True)
    expert_ids = g_idx[:, None] * EXPERTS_PER_GROUP + top_i
    A = M * TOP_K
    e_flat = expert_ids.reshape(A)
    tok_flat = jnp.repeat(jnp.arange(M), TOP_K)
    w_flat = weights.reshape(A)
    order = jnp.argsort(e_flat)
    e_s, tok_s, w_s = e_flat[order], tok_flat[order], w_flat[order]
    counts = jnp.bincount(e_flat, length=N_EXPERTS)
    padded = (counts + ROW_BLOCK - 1) // ROW_BLOCK * ROW_BLOCK
    start_unpad = jnp.cumsum(counts) - counts
    end_pad = jnp.cumsum(padded)
    start_pad = end_pad - padded
    dest = start_pad[e_s] + (jnp.arange(A) - start_unpad[e_s])
    n_blk = -(-A // ROW_BLOCK) + N_EXPERTS
    A_pad = n_blk * ROW_BLOCK
    x_pad = jnp.zeros((A_pad, D), bf16).at[dest].set(xf[tok_s].astype(bf16))
    tok_pad = jnp.zeros((A_pad,), jnp.int32).at[dest].set(tok_s.astype(jnp.int32))
    w_pad = jnp.zeros((A_pad,), f32).at[dest].set(w_s)
    blk_expert = jnp.minimum(jnp.searchsorted(end_pad, jnp.arange(n_blk) * ROW_BLOCK, side='right'),
                             N_EXPERTS - 1).astype(jnp.int32)
    y_pad = moe_experts(x_pad, blk_expert, w_gate.astype(bf16), w_up.astype(bf16), w_down.astype(bf16))
    y = jnp.zeros((M, D), f32).at[tok_pad].add(w_pad[:, None] * y_pad)
    return y.reshape(B, S, D)


def _rope_tables(S, theta, rot_dims, width):
    half = rot_dims // 2
    freqs = theta ** (-np.arange(half, dtype=np.float64) / half)
    ang = np.arange(S, dtype=np.float64)[:, None] * freqs[None, :]
    cos = np.ones((S, width), np.float64)
    sin_lo = np.zeros((S, width), np.float64)
    sin_hi = np.zeros((S, width), np.float64)
    cos[:, :half] = np.cos(ang)
    cos[:, half:rot_dims] = np.cos(ang)
    sin_lo[:, :half] = -np.sin(ang)
    sin_hi[:, half:rot_dims] = np.sin(ang)
    return [jnp.asarray(t, f32) for t in (cos, sin_lo, sin_hi)]


def _rope128(x, cos, sin_lo, sin_hi, half):
    return x * cos + pltpu.roll(x, LANES - half, axis=1) * sin_lo + pltpu.roll(x, half, axis=1) * sin_hi


def _nsa_prep_body(q_ref, kv_ref, cos_ref, sl_ref, sh_ref, qo_ref, kvo_ref, co_ref):
    cos, sl, sh = cos_ref[...], sl_ref[...], sh_ref[...]
    half = ROPE_DIMS // 2
    scale = HEAD_DIM ** -0.5
    for h in range(NSA_HEADS):
        c = slice(h * LANES, (h + 1) * LANES)
        qo_ref[:, c] = (_rope128(q_ref[:, c], cos, sl, sh, half) * scale).astype(bf16)
    for j in range(6):
        for g in range(NSA_KV_HEADS):
            c = slice((j * NSA_KV_HEADS + g) * LANES, (j * NSA_KV_HEADS + g + 1) * LANES)
            x = kv_ref[:, c]
            if j % 2 == 0:
                x = _rope128(x, cos, sl, sh, half)
            if j < 2:
                co_ref[:, c] = x.astype(bf16)
            else:
                kvo_ref[:, slice(c.start - 4 * LANES, c.stop - 4 * LANES)] = x.astype(bf16)


def nsa_prep(proj, S, tm=512):
    M = proj.shape[0]
    assert M % tm == 0 and S % tm == 0
    tabs = _rope_tables(S, ROPE_THETA, ROPE_DIMS, LANES)
    tspec = pl.BlockSpec((tm, LANES), lambda i: (i % (S // tm), 0))
    return pl.pallas_call(
        _nsa_prep_body,
        grid=(M // tm,),
        in_specs=[pl.BlockSpec((tm, D_GRP), lambda i: (i, OFF_NSA_Q // D_GRP)),
                  pl.BlockSpec((tm, 6 * NSA_KV_W), lambda i: (i, OFF_NSA_KV // (6 * NSA_KV_W))),
                  tspec, tspec, tspec],
        out_specs=[pl.BlockSpec((tm, D_GRP), lambda i: (i, 0)),
                   pl.BlockSpec((tm, 4 * NSA_KV_W), lambda i: (i, 0)),
                   pl.BlockSpec((tm, 2 * NSA_KV_W), lambda i: (i, 0))],
        out_shape=[jax.ShapeDtypeStruct((M, D_GRP), bf16), jax.ShapeDtypeStruct((M, 4 * NSA_KV_W), bf16),
                   jax.ShapeDtypeStruct((M, 2 * NSA_KV_W), bf16)],
        compiler_params=pltpu.CompilerParams(dimension_semantics=("parallel",), vmem_limit_bytes=VMEM_LIMIT),
        name="nsa_prep",
    )(proj, proj, *tabs)


def _nsa_compress_body(r_ref, w1_ref, w2_ref, pe_ref, o_ref):
    r = r_ref[0, 0, 0]
    half_w = CMP_STRIDE * HEAD_DIM
    a = jnp.dot(r, w1_ref[0, :half_w, :], preferred_element_type=f32)
    b = jnp.dot(r, w1_ref[0, half_w:, :], preferred_element_type=f32)
    n = r.shape[0]
    pe = jnp.dot(jnp.broadcast_to(pe_ref[0], (SUB, pe_ref.shape[-1])), w1_ref[0], preferred_element_type=f32)[0:1]
    pre = a + pltpu.roll(b, n - 1, axis=0) + pe
    hid = pre * jax.nn.sigmoid(pre)
    o_ref[0, 0, 0] = jnp.dot(hid.astype(bf16), w2_ref[0], preferred_element_type=f32).astype(bf16)


def nsa_compress(c_rot, w1s, w2s, pes, B, S):
    G, Dh = NSA_KV_HEADS, HEAD_DIM
    nc = S // CMP_STRIDE
    r = c_rot.reshape(B, nc, CMP_STRIDE, 2, G, Dh).transpose(3, 0, 4, 1, 2, 5).reshape(2, B, G, nc, CMP_STRIDE * Dh)
    w1 = jnp.stack(w1s).astype(bf16)
    w2 = jnp.stack(w2s).astype(bf16)
    pe = jnp.stack(pes).reshape(2, 1, CMP_LEN * Dh).astype(bf16)
    return pl.pallas_call(
        _nsa_compress_body,
        grid=(2, B, G),
        in_specs=[pl.BlockSpec((1, 1, 1, nc, CMP_STRIDE * Dh), lambda j, b, g: (j, b, g, 0, 0)),
                  pl.BlockSpec((1, CMP_LEN * Dh, Dh), lambda j, b, g: (j, 0, 0)),
                  pl.BlockSpec((1, Dh, Dh), lambda j, b, g: (j, 0, 0)),
                  pl.BlockSpec((1, 1, CMP_LEN * Dh), lambda j, b, g: (j, 0, 0))],
        out_specs=pl.BlockSpec((1, 1, 1, nc, Dh), lambda j, b, g: (j, b, g, 0, 0)),
        out_shape=jax.ShapeDtypeStruct((2, B, G, nc, Dh), bf16),
        compiler_params=pltpu.CompilerParams(dimension_semantics=("parallel", "parallel", "parallel"),
                                             vmem_limit_bytes=VMEM_LIMIT),
        name="nsa_compress",
    )(r, w1, w2, pe)


def _softmax_rows(s, mask):
    s = jnp.where(mask, s, -jnp.inf)
    m = jnp.max(s, axis=-1, keepdims=True)
    m = jnp.where(m == -jnp.inf, 0.0, m)
    e = jnp.exp(s - m)
    return e, jnp.maximum(jnp.sum(e, axis=-1, keepdims=True), jnp.finfo(f32).tiny)


def _nsa_attn_body(q_ref, cmp_ref, ks_ref, vs_ref, kw_ref, vw_ref, gate_ref, ov_ref, o_ref, *, tq, tk, S):
    g = pl.program_id(1)
    qi = pl.program_id(2)
    R, Dh = NSA_GROUP, HEAD_DIM
    rows = R * tq
    n_blk = S // SEL_LEN
    n_cmp = S // CMP_STRIDE - 1
    q0 = qi * tq
    q4 = jnp.concatenate([q_ref[0, :, r * Dh:(r + 1) * Dh] for r in range(R)], axis=0)
    nt = (((1,), (1,)), ((), ()))

    def qpos(shape):
        return q0 + lax.broadcasted_iota(jnp.int32, shape, 0) % tq

    k_cmp, v_cmp = cmp_ref[0, 0, 0], cmp_ref[1, 0, 0]
    ncp = k_cmp.shape[0]
    s = lax.dot_general(q4, k_cmp, nt, preferred_element_type=f32)
    n_idx = lax.broadcasted_iota(jnp.int32, (rows, ncp), 1)
    mask = (n_idx * CMP_STRIDE + (CMP_LEN - 1) <= qpos((rows, ncp))) & (n_idx < n_cmp)
    e, den = _softmax_rows(s, mask)
    p = e / den
    o_cmp = jnp.dot(p.astype(bf16), v_cmp, preferred_element_type=f32)

    psum = p[0:tq]
    for r in range(1, R):
        psum = psum + p[r * tq:(r + 1) * tq]
    ph, plo = _split2(psum)
    imp = (jnp.dot(ph, ov_ref[...], preferred_element_type=f32)
           + jnp.dot(plo, ov_ref[...], preferred_element_type=f32))
    blk = lax.broadcasted_iota(jnp.int32, (tq, LANES), 1)
    cur = (q0 + lax.broadcasted_iota(jnp.int32, (tq, LANES), 0)) // SEL_LEN
    visible = blk <= cur
    forced = (blk == 0) | (visible & (blk > cur - N_LOCAL_FORCED))
    imp = jnp.where(forced, FORCED_SCORE, jnp.where(visible, imp, -1.0))
    imp = jnp.where(blk < n_blk, imp, -jnp.inf)
    sel = jnp.zeros((tq, LANES), f32)
    blk_f = blk.astype(f32)
    for _ in range(min(N_SEL, n_blk)):
        m = jnp.max(imp, axis=-1, keepdims=True)
        first = jnp.min(jnp.where(imp == m, blk_f, float(LANES)), axis=-1, keepdims=True)
        hit = blk_f == first
        sel = jnp.where(hit, 1.0, sel)
        imp = jnp.where(hit, -jnp.inf, imp)
    sel4 = jnp.concatenate([sel.astype(bf16)] * R, axis=0)

    blocks_per_tile = tk // SEL_LEN

    def sel_step(kt, carry):
        m_run, l_run, acc = carry
        k0 = pl.multiple_of(kt * tk, tk)
        kt_k = ks_ref[0, pl.ds(k0, tk), :]
        kt_v = vs_ref[0, pl.ds(k0, tk), :]
        s = lax.dot_general(q4, kt_k, nt, preferred_element_type=f32)
        expand = (lax.broadcasted_iota(jnp.int32, (LANES, tk), 0)
                  == kt * blocks_per_tile + lax.broadcasted_iota(jnp.int32, (LANES, tk), 1) // SEL_LEN)
        chosen = jnp.dot(sel4, expand.astype(bf16), preferred_element_type=f32)
        kpos = k0 + lax.broadcasted_iota(jnp.int32, (rows, tk), 1)
        mask = (chosen > 0.5) & (kpos <= qpos((rows, tk)))
        s = jnp.where(mask, s, -jnp.inf)
        m_new = jnp.maximum(m_run, jnp.max(s, axis=-1, keepdims=True))
        m_safe = jnp.where(m_new == -jnp.inf, 0.0, m_new)
        alpha = jnp.exp(m_run - m_safe)
        e = jnp.exp(s - m_safe)
        l_new = alpha * l_run + jnp.sum(e, axis=-1, keepdims=True)
        acc = alpha * acc + jnp.dot(e.astype(bf16), kt_v, preferred_element_type=f32)
        return m_new, l_new, acc

    n_kt = (q0 + tq - 1) // tk + 1
    init = (jnp.full((rows, 1), -jnp.inf, f32), jnp.zeros((rows, 1), f32), jnp.zeros((rows, Dh), f32))
    _, l_sel, acc_sel = lax.fori_loop(0, n_kt, sel_step, init)
    o_sel = acc_sel / jnp.maximum(l_sel, jnp.finfo(f32).tiny)

    span = WINDOW + tq
    w0 = pl.multiple_of(jnp.maximum(q0 - WINDOW, 0), tq)
    kwin = kw_ref[0, pl.ds(w0, span), :]
    vwin = vw_ref[0, pl.ds(w0, span), :]
    s = lax.dot_general(q4, kwin, nt, preferred_element_type=f32)
    kpos = w0 + lax.broadcasted_iota(jnp.int32, (rows, span), 1)
    qp = qpos((rows, span))
    e, den = _softmax_rows(s, (kpos <= qp) & (kpos > qp - WINDOW))
    o_win = jnp.dot(e.astype(bf16), vwin, preferred_element_type=f32) / den

    gate = jax.nn.sigmoid(gate_ref[0])
    for r in range(R):
        c0 = (g * R + r) * 3
        lane = lax.broadcasted_iota(jnp.int32, (tq, LANES), 1)
        gsel = lambda c: jnp.sum(jnp.where(lane == c0 + c, gate, 0.0), axis=-1, keepdims=True)
        rs = slice(r * tq, (r + 1) * tq)
        o_ref[0, :, r * Dh:(r + 1) * Dh] = gsel(0) * o_cmp[rs] + gsel(1) * o_sel[rs] + gsel(2) * o_win[rs]


def nsa_attention(q_rot, kv_rot, cmp_kv, proj3, B, S, tq=128, tk=512):
    G, R, Dh = NSA_KV_HEADS, NSA_GROUP, HEAD_DIM
    assert S % tk == 0 and tk % tq == 0 and S >= WINDOW + tq and S // SEL_LEN <= LANES
    nc = S // CMP_STRIDE
    c0 = np.arange(nc - 1)[:, None] * CMP_STRIDE
    s0 = np.arange(S // SEL_LEN)[None, :] * SEL_LEN
    ov = np.zeros((nc, LANES), np.float32)
    ov[:nc - 1, :S // SEL_LEN] = np.clip(np.minimum(c0 + CMP_LEN, s0 + SEL_LEN) - np.maximum(c0, s0), 0, None) / CMP_LEN
    kvspec = lambda j: pl.BlockSpec((1, S, Dh), lambda b, g, i: (b, 0, j * G + g))
    return pl.pallas_call(
        functools.partial(_nsa_attn_body, tq=tq, tk=tk, S=S),
        grid=(B, G, S // tq),
        in_specs=[pl.BlockSpec((1, tq, R * Dh), lambda b, g, i: (b, i, g)),
                  pl.BlockSpec((2, 1, 1, nc, Dh), lambda b, g, i: (0, b, g, 0, 0)),
                  kvspec(0), kvspec(1), kvspec(2), kvspec(3),
                  pl.BlockSpec((1, tq, LANES), lambda b, g, i: (b, i, OFF_SMALL // LANES)),
                  pl.BlockSpec((nc, LANES), lambda b, g, i: (0, 0))],
        out_specs=pl.BlockSpec((1, tq, R * Dh), lambda b, g, i: (b, i, g)),
        out_shape=jax.ShapeDtypeStruct((B, S, D_GRP), f32),
        compiler_params=pltpu.CompilerParams(dimension_semantics=("parallel", "parallel", "arbitrary"),
                                             vmem_limit_bytes=VMEM_LIMIT),
        name="nsa_attention",
    )(q_rot, cmp_kv, kv_rot, kv_rot, kv_rot, kv_rot, proj3, jnp.asarray(ov, bf16))


def nsa_mixer_pallas(proj, ck_w1, ck_w2, ck_pe, cv_w1, cv_w2, cv_pe, B, S):
    q_rot, kv_rot, c_rot = nsa_prep(proj, S)
    cmp_kv = nsa_compress(c_rot, (ck_w1, cv_w1), (ck_w2, cv_w2), (ck_pe, cv_pe), B, S)
    return nsa_attention(q_rot.reshape(B, S, D_GRP), kv_rot.reshape(B, S, 4 * NSA_KV_W), cmp_kv,
                         proj.reshape(B, S, P_PAD), B, S)


def _retention_body(q_ref, k_ref, v_ref, g_ref, cos_ref, sin_ref, dec_ref, xi_ref, zeta_ref, gc_ref, o_ref, st_ref,
                    *, T):
    @pl.when(pl.program_id(2) == 0)
    def _():
        st_ref[...] = jnp.zeros_like(st_ref)

    C, Dh = RET_CHUNK, HEAD_DIM
    nt = (((1,), (1,)), ((), ()))
    decay_in, xi, zeta, gamma_c = dec_ref[0], xi_ref[0], zeta_ref[0], gc_ref[0, 0:1, :]
    state = st_ref[...]
    for c in range(T // C):
        rs = slice(c * C, (c + 1) * C)
        cos, sin = cos_ref[rs, :], sin_ref[rs, :]
        rope = lambda x: x * cos + pltpu.roll(x, Dh // 2, axis=1) * sin
        q = rope(q_ref[0, rs, :])
        k = rope(k_ref[0, rs, :]) * Dh ** -0.5
        qb, vb = q.astype(bf16), v_ref[0, rs, :].astype(bf16)
        inner = lax.dot_general(qb, k.astype(bf16), nt, preferred_element_type=f32) * decay_in
        o = (jnp.dot(inner.astype(bf16), vb, preferred_element_type=f32)
             + jnp.dot(qb, state.astype(bf16), preferred_element_type=f32) * xi)
        kz = (k * zeta).astype(bf16)
        state = state * gamma_c + jnp.dot(kz.T, vb, preferred_element_type=f32)
        mu = jnp.mean(o, axis=-1, keepdims=True)
        d = o - mu
        o = d * lax.rsqrt(jnp.mean(d * d, axis=-1, keepdims=True) + NORM_EPS)
        gt = g_ref[0, rs, :]
        o_ref[0, rs, :] = gt * jax.nn.sigmoid(gt) * o
    st_ref[...] = state


def retention_mixer_pallas(proj3, B, S, T=512):
    H, Dh, C = RET_HEADS, HEAD_DIM, RET_CHUNK
    assert S % T == 0 and T % C == 0
    freqs = RET_THETA ** (-np.linspace(0.0, 1.0, Dh // 2))
    ang = np.arange(S, dtype=np.float64)[:, None] * freqs[None, :]
    cos = jnp.asarray(np.concatenate([np.cos(ang), np.cos(ang)], 1), f32)
    sin = jnp.asarray(np.concatenate([-np.sin(ang), np.sin(ang)], 1), f32)
    log_gamma = np.log1p(-(2.0 ** (-5.0 - np.arange(H, dtype=np.float64))))
    n = np.arange(C, dtype=np.float64)
    diff = n[:, None] - n[None, :]
    decay_in = np.where(diff >= 0, np.exp(log_gamma[:, None, None] * np.maximum(diff, 0.0)), 0.0)
    xi = np.exp(log_gamma[:, None] * (n + 1.0))[:, :, None] * np.ones((1, 1, Dh))
    zeta = np.exp(log_gamma[:, None] * (C - 1.0 - n))[:, :, None] * np.ones((1, 1, Dh))
    gamma_c = np.exp(log_gamma * C)[:, None, None] * np.ones((1, SUB, Dh))
    blk = lambda j: pl.BlockSpec((1, T, Dh), lambda b, h, t: (b, t, (OFF_RET + j * D_GRP) // Dh + h))
    tab = pl.BlockSpec((T, Dh), lambda b, h, t: (t, 0))
    per_head = lambda r: pl.BlockSpec((1, r, Dh), lambda b, h, t: (h, 0, 0))
    return pl.pallas_call(
        functools.partial(_retention_body, T=T),
        grid=(B, H, S // T),
        in_specs=[blk(0), blk(1), blk(2), blk(3), tab, tab, per_head(C), per_head(C), per_head(C), per_head(SUB)],
        out_specs=pl.BlockSpec((1, T, Dh), lambda b, h, t: (b, t, h)),
        out_shape=jax.ShapeDtypeStruct((B, S, D_GRP), f32),
        scratch_shapes=[pltpu.VMEM((Dh, Dh), f32)],
        compiler_params=pltpu.CompilerParams(dimension_semantics=("parallel", "parallel", "arbitrary"),
                                             vmem_limit_bytes=VMEM_LIMIT),
        name="retention",
    )(proj3, proj3, proj3, proj3, cos, sin, jnp.asarray(decay_in, f32), jnp.asarray(xi, f32),
      jnp.asarray(zeta, f32), jnp.asarray(gamma_c, f32))


def pack_w_in(w_in):
    w = w_in.astype(bf16)
    n0, r0, d0, w0 = 0, NSA_COLS, NSA_COLS + RET_COLS, NSA_COLS + RET_COLS + DN_COLS
    z = lambda n: jnp.zeros((w.shape[0], n), bf16)
    gate = w[:, n0 + D_GRP + 6 * NSA_KV_W:r0]
    dn_ab = w[:, d0 + 3 * D_GRP:d0 + 3 * D_GRP + 2 * DN_HEADS]
    return jnp.concatenate([
        w[:, w0:w0 + 3 * D_GRP],
        w[:, r0:d0],
        w[:, d0:d0 + 3 * D_GRP], w[:, d0 + 3 * D_GRP + 2 * DN_HEADS:w0],
        w[:, n0:n0 + D_GRP + 6 * NSA_KV_W],
        w[:, w0 + 3 * D_GRP:], z(RW_LORA_PAD - RW_LORA),
        gate, dn_ab, z(LANES - gate.shape[1] - dn_ab.shape[1]),
    ], axis=1)


def hybrid_mixer(hn, w_in, w_out, ck_w1, ck_w2, ck_pe, cv_w1, cv_w2, cv_pe, dn_conv, dn_a_log, dn_dt_bias,
                 dn_norm, rw_mu, rw_w0, rw_w2, rw_a0, rw_a2, rw_g2, rw_kk, rw_ka, rw_rk, rw_ln_g, rw_ln_b):
    B, S, D = hn.shape
    proj = matmul(hn.reshape(B * S, D).astype(bf16), pack_w_in(w_in), name="in_proj")
    p3 = proj.reshape(B, S, P_PAD)
    cols = lambda off, n: p3[:, :, off:off + n]
    o_nsa = nsa_mixer_pallas(proj, ck_w1, ck_w2, ck_pe, cv_w1, cv_w2, cv_pe, B, S)
    o_ret = retention_mixer_pallas(p3, B, S)
    o_dn = deltanet_mixer(cols(OFF_DN, D_GRP), cols(OFF_DN + D_GRP, D_GRP), cols(OFF_DN + 2 * D_GRP, D_GRP),
                          cols(OFF_SMALL + 3 * NSA_HEADS, DN_HEADS), cols(OFF_SMALL + 3 * NSA_HEADS + DN_HEADS, DN_HEADS),
                          cols(OFF_DN + 3 * D_GRP, D_GRP), dn_conv, dn_a_log, dn_dt_bias, dn_norm)
    o_rw = rwkv7_mixer(proj, OFF_RW // (3 * D_GRP), OFF_LORA // RW_LORA_PAD, rw_mu, rw_w0, rw_w2, rw_a0, rw_a2,
                       rw_g2, rw_kk, rw_ka, rw_rk, rw_ln_g, rw_ln_b, B, S)
    o = jnp.concatenate([o_nsa, o_ret, o_dn, o_rw], -1).astype(bf16)
    return matmul(o.reshape(B * S, D_MIX), w_out.astype(bf16), name="out_proj").reshape(B, S, D)


def kernel(x, p, norm_mix, w_in, w_out, nsa_ck_w1, nsa_ck_w2, nsa_ck_pe, nsa_cv_w1, nsa_cv_w2, nsa_cv_pe, dn_conv, dn_a_log, dn_dt_bias, dn_norm, rw_mu, rw_w0, rw_w2, rw_a0, rw_a2, rw_g2, rw_kk, rw_ka, rw_rk, rw_ln_g, rw_ln_b, norm_moe, moe_router_grp, moe_router_grp_b, moe_router_exp, moe_router_exp_b, moe_w_gate, moe_w_up, moe_w_down, norm_ple, ple_w, ple_gate, norm_final):
    h = x
    B, S, D = x.shape
    for i in range(DEPTH):
        hn = rms_norm(h, norm_mix[i])
        h = h + hybrid_mixer(hn, w_in[i], w_out[i], nsa_ck_w1[i], nsa_ck_w2[i], nsa_ck_pe[i], nsa_cv_w1[i],
                             nsa_cv_w2[i], nsa_cv_pe[i], dn_conv[i], dn_a_log[i], dn_dt_bias[i], dn_norm[i],
                             rw_mu[i], rw_w0[i], rw_w2[i], rw_a0[i], rw_a2[i], rw_g2[i], rw_kk[i], rw_ka[i],
                             rw_rk[i], rw_ln_g[i], rw_ln_b[i])
        hn = rms_norm(h, norm_moe[i])
        h = h + hier_moe(hn, moe_router_grp[i], moe_router_grp_b[i], moe_router_exp[i], moe_router_exp_b[i],
                         moe_w_gate[i], moe_w_up[i], moe_w_down[i])
        hn = rms_norm(h, norm_ple[i])
        gate = matmul(hn.reshape(B * S, D).astype(bf16), ple_gate[i].astype(bf16), name="ple_gate").reshape(B, S, D)
        pw = matmul(p[i].reshape(B * S, PLE_DIM).astype(bf16), ple_w[i].astype(bf16), name="ple_in").reshape(B, S, D)
        h = h + pw * jax.nn.sigmoid(gate)
    return rms_norm(h, norm_final)
```

```python
import functools

import jax
import jax.numpy as jnp
import numpy as np
from jax import lax
from jax.experimental import pallas as pl
from jax.experimental.pallas import tpu as pltpu

D_MODEL = 4096
BATCH = 2
SEQ = 4096
DEPTH = 2
f32 = jnp.float32
bf16 = jnp.bfloat16
D_MIX = D_MODEL
D_GRP = D_MIX // 4
HEAD_DIM = 128
NORM_EPS = 1e-6
NSA_HEADS = D_GRP // HEAD_DIM
NSA_KV_HEADS = 2
NSA_GROUP = NSA_HEADS // NSA_KV_HEADS
NSA_KV_W = NSA_KV_HEADS * HEAD_DIM
CMP_LEN = 32
CMP_STRIDE = 16
SEL_LEN = 64
N_SEL = 16
N_LOCAL_FORCED = 2
FORCED_SCORE = 1e4
WINDOW = 512
SEL_QBLK = 64
WIN_QBLK = 128
ROPE_THETA = 500000.0
ROPE_DIMS = HEAD_DIM // 4
RET_HEADS = D_GRP // HEAD_DIM
RET_CHUNK = 128
RET_THETA = 10000.0
DN_HEADS = D_GRP // HEAD_DIM
DN_CHUNK = 64
CONV_WIDTH = 4
RW_HEAD_DIM = 64
RW_HEADS = D_GRP // RW_HEAD_DIM
RW_DECAY_LORA = 64
RW_AAA_LORA = 64
RW_GATE_LORA = 160
RW_LN_EPS = 64e-5
N_GROUPS = 4
EXPERTS_PER_GROUP = 8
N_EXPERTS = N_GROUPS * EXPERTS_PER_GROUP
TOP_K = 2
D_EXPERT = 768
ROW_BLOCK = 256
PLE_DIM = 256
NSA_COLS = D_GRP + 6 * NSA_KV_W + 3 * NSA_HEADS
RET_COLS = 4 * D_GRP
DN_COLS = 4 * D_GRP + 2 * DN_HEADS
RW_COLS = 3 * D_GRP + RW_DECAY_LORA + RW_AAA_LORA + RW_GATE_LORA
P_TOTAL = NSA_COLS + RET_COLS + DN_COLS + RW_COLS

RW_LORA = RW_DECAY_LORA + RW_AAA_LORA + RW_GATE_LORA
RW_LORA_PAD = 384
LANES = 128
SUB = 8

OFF_RW = 0
OFF_RET = 3 * D_GRP
OFF_DN = OFF_RET + 4 * D_GRP
OFF_NSA_Q = OFF_DN + 4 * D_GRP
OFF_NSA_KV = OFF_NSA_Q + D_GRP
OFF_LORA = OFF_NSA_KV + 6 * NSA_KV_W
OFF_SMALL = OFF_LORA + RW_LORA_PAD
P_PAD = OFF_SMALL + LANES
assert P_PAD == 14336 and OFF_LORA % RW_LORA_PAD == 0

VMEM_LIMIT = 56 * 1024 * 1024


def split_cols(x, sizes):
    return jnp.split(x, np.cumsum(sizes)[:-1].tolist(), axis=-1)


def rms_norm(x, g, eps=NORM_EPS):
    xf = x.astype(f32)
    y = xf * lax.rsqrt(jnp.mean(xf * xf, -1, keepdims=True) + eps)
    return (y * g.astype(f32)).astype(x.dtype)


def l2norm(x, eps=1e-6):
    return x * lax.rsqrt(jnp.sum(x * x, -1, keepdims=True) + eps)


def head_layer_norm(x, eps):
    mu = jnp.mean(x, -1, keepdims=True)
    return (x - mu) * lax.rsqrt(jnp.var(x, -1, keepdims=True) + eps)


def masked_softmax(s, mask):
    s = jnp.where(mask, s.astype(f32), -jnp.inf)
    m = jnp.max(s, -1, keepdims=True)
    m = jnp.where(jnp.isfinite(m), m, 0.0)
    e = jnp.where(mask, jnp.exp(s - m), 0.0)
    return e / jnp.maximum(jnp.sum(e, -1, keepdims=True), jnp.finfo(f32).tiny)


def rotary(x, freqs, rot_dims):
    S = x.shape[1]
    half = rot_dims // 2
    ang = jnp.arange(S, dtype=f32)[:, None] * freqs[None, :]
    ang = ang.reshape((S,) + (1,) * (x.ndim - 3) + (half,))
    cos, sin = jnp.cos(ang), jnp.sin(ang)
    x1 = x[..., :half].astype(f32)
    x2 = x[..., half:rot_dims].astype(f32)
    out = jnp.concatenate([x1 * cos - x2 * sin, x2 * cos + x1 * sin, x[..., rot_dims:].astype(f32)], -1)
    return out.astype(x.dtype)


def cmp_sel_overlap(S):
    n_cmp = (S - CMP_LEN) // CMP_STRIDE + 1
    c0 = np.arange(n_cmp)[:, None] * CMP_STRIDE
    s0 = np.arange(S // SEL_LEN)[None, :] * SEL_LEN
    ov = np.clip(np.minimum(c0 + CMP_LEN, s0 + SEL_LEN) - np.maximum(c0, s0), 0, None) / CMP_LEN
    return jnp.asarray(ov, f32)


def nsa_mixer(q, kc, vc, ks, vs, kw, vw, gate_logit, ck_w1, ck_w2, ck_pe, cv_w1, cv_w2, cv_pe):
    B, S, _ = q.shape
    G, R, Dh = NSA_KV_HEADS, NSA_GROUP, HEAD_DIM
    scale = Dh ** -0.5
    half = ROPE_DIMS // 2
    freqs = ROPE_THETA ** (-jnp.arange(half, dtype=f32) / half)
    q = rotary(q.reshape(B, S, G, R, Dh), freqs, ROPE_DIMS)
    kc, vc, ks, vs, kw, vw = [t.reshape(B, S, G, Dh) for t in (kc, vc, ks, vs, kw, vw)]
    kc, ks, kw = [rotary(t, freqs, ROPE_DIMS) for t in (kc, ks, kw)]
    pos = jnp.arange(S)
    n_cmp = (S - CMP_LEN) // CMP_STRIDE + 1
    blk_tok = np.arange(n_cmp)[:, None] * CMP_STRIDE + np.arange(CMP_LEN)[None, :]

    def compress(t, w1, w2, pe):
        blk = t[:, blk_tok] + pe[:, None, :]
        blk = blk.transpose(0, 3, 1, 2, 4).reshape(B, G, n_cmp, CMP_LEN * Dh)
        return jax.nn.silu(blk @ w1) @ w2

    k_cmp = compress(kc, ck_w1, ck_w2, ck_pe)
    v_cmp = compress(vc, cv_w1, cv_w2, cv_pe)
    s_cmp = jnp.einsum('bsgrd,bgnd->bgrsn', q, k_cmp) * scale
    cmp_mask = jnp.asarray(blk_tok[:, -1])[None, :] <= pos[:, None]
    p_cmp = masked_softmax(s_cmp, cmp_mask)
    o_cmp = jnp.einsum('bgrsn,bgnd->bsgrd', p_cmp, v_cmp.astype(f32))
    n_sel_blk = S // SEL_LEN
    k_top = min(N_SEL, n_sel_blk)
    imp = jnp.einsum('bgrsn,nj->bgsj', p_cmp, cmp_sel_overlap(S))
    blk = jnp.arange(n_sel_blk)[None, :]
    cur = (pos // SEL_LEN)[:, None]
    visible = blk <= cur
    forced = (blk == 0) | (visible & (blk > cur - N_LOCAL_FORCED))
    imp = jnp.where(forced, FORCED_SCORE, jnp.where(visible, imp, -1.0))
    sel_idx = lax.top_k(imp, k_top)[1]
    ks_blk = ks.reshape(B, n_sel_blk, SEL_LEN, G, Dh).transpose(0, 3, 1, 2, 4)
    vs_blk = vs.reshape(B, n_sel_blk, SEL_LEN, G, Dh).transpose(0, 3, 1, 2, 4)
    n_qb = S // SEL_QBLK
    q_b = q.reshape(B, n_qb, SEL_QBLK, G, R, Dh).transpose(1, 0, 2, 3, 4, 5)
    idx_b = sel_idx.reshape(B, G, n_qb, SEL_QBLK, k_top).transpose(2, 0, 1, 3, 4)
    qpos_b = pos.reshape(n_qb, SEL_QBLK)
    b_ix = jnp.arange(B)[:, None, None, None]
    g_ix = jnp.arange(G)[None, :, None, None]

    def sel_block(args):
        qb, ib, qp = args
        kb = ks_blk[b_ix, g_ix, ib].reshape(B, G, SEL_QBLK, k_top * SEL_LEN, Dh)
        vb = vs_blk[b_ix, g_ix, ib].reshape(B, G, SEL_QBLK, k_top * SEL_LEN, Dh)
        tok = (ib[..., None] * SEL_LEN + jnp.arange(SEL_LEN)).reshape(B, G, SEL_QBLK, k_top * SEL_LEN)
        s = jnp.einsum('bqgrd,bgqkd->bgrqk', qb, kb) * scale
        mask = tok[:, :, None] <= qp[None, None, None, :, None]
        pr = masked_softmax(s, mask)
        return jnp.einsum('bgrqk,bgqkd->bqgrd', pr, vb.astype(f32))

    o_sel = lax.map(sel_block, (q_b, idx_b, qpos_b))
    o_sel = o_sel.transpose(1, 0, 2, 3, 4, 5).reshape(B, S, G, R, Dh)
    n_wb = S // WIN_QBLK
    span = WINDOW + WIN_QBLK
    key_idx = np.arange(n_wb)[:, None] * WIN_QBLK + np.arange(span)[None, :]
    pad = ((0, 0), (WINDOW, 0), (0, 0), (0, 0))
    kb = jnp.pad(kw, pad)[:, key_idx]
    vb = jnp.pad(vw, pad)[:, key_idx]
    qb = q.reshape(B, n_wb, WIN_QBLK, G, R, Dh)
    s_win = jnp.einsum('bnqgrd,bnkgd->bgrnqk', qb, kb) * scale
    kpos = jnp.asarray(key_idx - WINDOW)[:, None, :]
    qpos = pos.reshape(n_wb, WIN_QBLK)[:, :, None]
    win_mask = (kpos >= 0) & (kpos <= qpos) & (kpos > qpos - WINDOW)
    p_win = masked_softmax(s_win, win_mask)
    o_win = jnp.einsum('bgrnqk,bnkgd->bnqgrd', p_win, vb.astype(f32)).reshape(B, S, G, R, Dh)
    gate = jax.nn.sigmoid(gate_logit.astype(f32)).reshape(B, S, G, R, 3)
    o = gate[..., 0:1] * o_cmp + gate[..., 1:2] * o_sel + gate[..., 2:3] * o_win
    return o.reshape(B, S, D_GRP)


def retention_mixer(q, k, v, g):
    B, S, _ = q.shape
    H, Dh, C = RET_HEADS, HEAD_DIM, RET_CHUNK
    n_c = S // C
    freqs = RET_THETA ** (-jnp.linspace(0.0, 1.0, Dh // 2, dtype=f32))
    heads = lambda t: t.reshape(B, S, H, Dh)
    q = rotary(heads(q), freqs, Dh).astype(f32)
    k = rotary(heads(k), freqs, Dh).astype(f32) * Dh ** -0.5
    v = heads(v).astype(f32)
    log_gamma = jnp.log1p(-(2.0 ** (-5.0 - jnp.arange(H, dtype=f32))))
    n = jnp.arange(C, dtype=f32)
    diff = n[:, None] - n[None, :]
    decay_in = jnp.where(diff >= 0, jnp.exp(log_gamma[:, None, None] * jnp.maximum(diff, 0.0)), 0.0)
    xi = jnp.exp(log_gamma[:, None] * (n + 1.0))
    zeta = jnp.exp(log_gamma[:, None] * (C - 1.0 - n))
    gamma_c = jnp.exp(log_gamma * C)
    chunks = lambda t: t.reshape(B, n_c, C, H, Dh).transpose(1, 0, 3, 2, 4)

    def step(state, inp):
        qc, kc, vc = inp
        inner = jnp.einsum('bhnd,bhmd->bhnm', qc, kc) * decay_in
        o = jnp.einsum('bhnm,bhme->bhne', inner, vc) + jnp.einsum('bhnd,bhde->bhne', qc, state) * xi[..., None]
        state = state * gamma_c[:, None, None] + jnp.einsum('bhmd,bhme->bhde', kc * zeta[..., None], vc)
        return state, o

    _, o = lax.scan(step, jnp.zeros((B, H, Dh, Dh), f32), (chunks(q), chunks(k), chunks(v)))
    o = head_layer_norm(o.transpose(1, 0, 3, 2, 4).reshape(B, S, H, Dh), NORM_EPS)
    return jax.nn.silu(g.astype(f32)) * o.reshape(B, S, D_GRP)


def deltanet_mixer(q, k, v, a, b, z, conv_w, a_log, dt_bias, norm_g):
    B, S, _ = q.shape
    H, Dh, C = DN_HEADS, HEAD_DIM, DN_CHUNK
    n_c = S // C
    qkv = jnp.concatenate([q, k, v], -1)
    qkv = lax.conv_general_dilated(qkv, conv_w[:, None, :], window_strides=(1,), padding=[(CONV_WIDTH - 1, 0)],
                                   dimension_numbers=('NWC', 'WIO', 'NWC'), feature_group_count=3 * D_GRP)
    qkv = jax.nn.silu(qkv.astype(f32))
    q, k, v = [t.reshape(B, S, H, Dh) for t in jnp.split(qkv, 3, -1)]
    q = l2norm(q) * Dh ** -0.5
    k = l2norm(k)
    beta = jax.nn.sigmoid(b.astype(f32))
    g = -jnp.exp(a_log.astype(f32)) * jax.nn.softplus(a.astype(f32) + dt_bias.astype(f32))
    chunks = lambda t: jnp.moveaxis(t.reshape((B, n_c, C) + t.shape[2:]), 3, 1)
    qc, kc, vc, bc = chunks(q), chunks(k), chunks(v), chunks(beta)
    gc = jnp.cumsum(chunks(g), -1)
    kbeta = kc * bc[..., None]
    vbeta = vc * bc[..., None]
    idx = jnp.arange(C)
    tril = idx[:, None] >= idx[None, :]
    strict = idx[:, None] > idx[None, :]
    decay = jnp.exp(jnp.where(tril, gc[..., :, None] - gc[..., None, :], -jnp.inf))
    m = jnp.where(strict, jnp.einsum('bhncd,bhnkd->bhnck', kbeta, kc) * decay, 0.0)
    eye = jnp.eye(C, dtype=f32)
    t_inv = lax.linalg.triangular_solve(eye + m, jnp.broadcast_to(eye, m.shape), left_side=True,
                                        lower=True, unit_diagonal=True)
    u = t_inv @ vbeta
    w = t_inv @ (kbeta * jnp.exp(gc)[..., None])
    attn_in = jnp.where(tril, jnp.einsum('bhncd,bhnkd->bhnck', qc, kc) * decay, 0.0)
    q_dec = qc * jnp.exp(gc)[..., None]
    k_dec = kc * jnp.exp(gc[..., -1:] - gc)[..., None]
    g_last = jnp.exp(gc[..., -1])

    def step(state, inp):
        u_i, w_i, a_i, qd_i, kd_i, gl_i = inp
        v_new = u_i - w_i @ state
        o = qd_i @ state + a_i @ v_new
        state = state * gl_i[..., None, None] + jnp.swapaxes(kd_i, -1, -2) @ v_new
        return state, o

    xs = tuple(jnp.moveaxis(t, 2, 0) for t in (u, w, attn_in, q_dec, k_dec, g_last))
    _, o = lax.scan(step, jnp.zeros((B, H, Dh, Dh), f32), xs)
    o = o.transpose(1, 0, 3, 2, 4).reshape(B, S, H, Dh)
    o = rms_norm(o, norm_g) * jax.nn.silu(z.astype(f32)).reshape(B, S, H, Dh)
    return o.reshape(B, S, D_GRP)


def _split2(x):
    h = x.astype(bf16)
    return h, (x - h.astype(f32)).astype(bf16)


def _split3(x):
    h = x.astype(bf16)
    r1 = x - h.astype(f32)
    m = r1.astype(bf16)
    return h, m, (r1 - m.astype(f32)).astype(bf16)


def _rwkv_prep_body(x_ref, xprev_ref, l_ref, lprev_ref, mu_ref, mul_ref, w0_ref, a0_ref, kkw_ref, ka_ref,
                    w2_ref, a2_ref, g2_ref, e_ref, r_ref, w_ref, k_ref, v_ref, kk_ref, b_ref, g_ref, *, tm, S):
    i = pl.program_id(0)
    first = (i * tm) % S == 0

    def token_mix(x, prev8, mu):
        rows = lax.broadcasted_iota(jnp.int32, x.shape, 0)
        prev_row = jnp.where(first, 0.0, prev8[SUB - 1:SUB, :])
        x_prev = jnp.where(rows == 0, prev_row, pltpu.roll(x, 1, axis=0))
        return x + (x_prev - x) * mu

    x = token_mix(x_ref[...], xprev_ref[...], mu_ref[...])
    lora = token_mix(l_ref[...], lprev_ref[...], mul_ref[...])
    r = x[:, 0:D_GRP]
    k = x[:, D_GRP:2 * D_GRP]
    v = x[:, 2 * D_GRP:3 * D_GRP]
    dw = jnp.dot(jnp.tanh(lora).astype(bf16), w2_ref[...], preferred_element_type=f32)
    da = jnp.dot(lora.astype(bf16), a2_ref[...], preferred_element_type=f32)
    g = jnp.dot(jax.nn.sigmoid(lora).astype(bf16), g2_ref[...], preferred_element_type=f32)
    w = -jax.nn.softplus(-(w0_ref[...] + dw)) - 0.5
    decay = jnp.exp(-jnp.exp(w))
    a = jax.nn.sigmoid(a0_ref[...] + da)
    kq = k * kkw_ref[...]
    sq_h, sq_l = _split2(kq * kq)
    ss = (jnp.dot(sq_h, e_ref[...], preferred_element_type=f32)
          + jnp.dot(sq_l, e_ref[...], preferred_element_type=f32))
    kk = kq * lax.rsqrt(ss + 1e-6)
    r_ref[...] = r
    w_ref[...] = decay
    k_ref[...] = k * (1.0 + (a - 1.0) * ka_ref[...])
    v_ref[...] = v
    kk_ref[...] = kk
    b_ref[...] = kk * a
    g_ref[...] = g


def rwkv_prep(proj, rkv_blk, lora_blk, mu, w0, w2, a0, a2, g2, k_k, k_a, S, tm=256):
    M = proj.shape[0]
    assert M % tm == 0 and S % tm == 0
    W3 = 3 * D_GRP
    mu_x = mu[:W3].reshape(1, W3)
    mu_l = jnp.pad(mu[W3:], (0, RW_LORA_PAD - RW_LORA)).reshape(1, RW_LORA_PAD)
    w2p = jnp.zeros((RW_LORA_PAD, D_GRP), f32).at[0:RW_DECAY_LORA].set(w2).astype(bf16)
    a2p = jnp.zeros((RW_LORA_PAD, D_GRP), f32).at[RW_DECAY_LORA:RW_DECAY_LORA + RW_AAA_LORA].set(a2).astype(bf16)
    g2p = jnp.zeros((RW_LORA_PAD, D_GRP), f32).at[RW_DECAY_LORA + RW_AAA_LORA:RW_LORA].set(g2).astype(bf16)
    hid = np.arange(D_GRP) // RW_HEAD_DIM
    e = jnp.asarray(hid[:, None] == hid[None, :], bf16)
    row = lambda t: t.reshape(1, D_GRP)
    vec = pl.BlockSpec((1, D_GRP), lambda i: (0, 0))
    wspec = pl.BlockSpec((RW_LORA_PAD, D_GRP), lambda i: (0, 0))
    ospec = pl.BlockSpec((tm, D_GRP), lambda i: (i, 0))
    prev_row_blk = lambda i: jnp.maximum(i * (tm // SUB) - 1, 0)
    return pl.pallas_call(
        functools.partial(_rwkv_prep_body, tm=tm, S=S),
        grid=(M // tm,),
        in_specs=[pl.BlockSpec((tm, W3), lambda i: (i, rkv_blk)),
                  pl.BlockSpec((SUB, W3), lambda i: (prev_row_blk(i), rkv_blk)),
                  pl.BlockSpec((tm, RW_LORA_PAD), lambda i: (i, lora_blk)),
                  pl.BlockSpec((SUB, RW_LORA_PAD), lambda i: (prev_row_blk(i), lora_blk)),
                  pl.BlockSpec((1, W3), lambda i: (0, 0)),
                  pl.BlockSpec((1, RW_LORA_PAD), lambda i: (0, 0)),
                  vec, vec, vec, vec, wspec, wspec, wspec,
                  pl.BlockSpec((D_GRP, D_GRP), lambda i: (0, 0))],
        out_specs=[ospec] * 7,
        out_shape=[jax.ShapeDtypeStruct((M, D_GRP), f32)] * 7,
        compiler_params=pltpu.CompilerParams(dimension_semantics=("parallel",), vmem_limit_bytes=VMEM_LIMIT),
        name="rwkv_prep",
    )(proj, proj, proj, proj, mu_x, mu_l, row(w0), row(a0), row(k_k), row(k_a), w2p, a2p, g2p, e)


def _rwkv_scan_body(r_ref, w_ref, k_ref, v_ref, kk_ref, b_ref, g_ref, lng_ref, lnb_ref, rk_ref, o_ref, s_ref,
                    *, G, T):
    tb = pl.program_id(2)

    @pl.when(tb == 0)
    def _():
        s_ref[...] = jnp.zeros_like(s_ref)

    Dh = RW_HEAD_DIM
    lo = lax.broadcasted_iota(jnp.int32, (Dh, LANES), 1) < Dh
    lo8 = lax.broadcasted_iota(jnp.int32, (SUB, LANES), 1) < Dh
    sub_iota = lax.broadcasted_iota(jnp.int32, (4 * SUB, LANES), 0)
    row_iota = lax.broadcasted_iota(jnp.int32, (SUB, LANES), 0)

    def seg_sum(x, m):
        s_lo = jnp.sum(jnp.where(m, x, 0.0), axis=-1, keepdims=True)
        s_hi = jnp.sum(jnp.where(m, 0.0, x), axis=-1, keepdims=True)
        return jnp.where(m, s_lo, s_hi)

    def group(gi, carry):
        t0 = pl.multiple_of(gi * SUB, SUB)
        states = [s_ref[p] for p in range(G)]
        cols = [slice(p * LANES, (p + 1) * LANES) for p in range(G)]
        r8b, vpts = [], []
        for p in range(G):
            vh, vm, vl = _split3(v_ref[0, pl.ds(t0, SUB), cols[p]])
            vp = jnp.concatenate([vh, vm, vl, jnp.zeros_like(vh)], axis=0)
            vpts.append(vp.T)
            r8b.append(r_ref[0, pl.ds(t0, SUB), cols[p]].astype(bf16))
        vcs = []
        for i in range(SUB):
            sel = (((sub_iota % SUB) == i) & (sub_iota < 3 * SUB)).astype(bf16)
            step_vcs = []
            for p in range(G):
                vc = jnp.dot(vpts[p], sel, preferred_element_type=f32)
                step_vcs.append(jnp.where(lo, vc[:Dh], vc[Dh:]))
            vcs.append(step_vcs)
        yacc = [jnp.zeros((SUB, LANES), f32) for _ in range(G)]
        tiles = [[ref[0, pl.ds(t0, SUB), cols[p]] for ref in (kk_ref, w_ref, b_ref, k_ref)] for p in range(G)]
        for i in range(SUB):
            for p in range(G):
                kk_r, w_r, b_r, k_r = [jnp.broadcast_to(t8[i:i + 1, :], (Dh, LANES)) for t8 in tiles[p]]
                S = states[p]
                sab = -seg_sum(S * kk_r, lo)
                states[p] = S * w_r + sab * b_r + vcs[i][p] * k_r
            for p in range(G):
                Sb = states[p].astype(bf16)
                zero = jnp.zeros_like(Sb)
                mt = jnp.concatenate([jnp.where(lo, Sb, zero), jnp.where(lo, zero, Sb)], axis=0)
                y8 = lax.dot_general(r8b[p], mt, (((1,), (1,)), ((), ())), preferred_element_type=f32)
                yacc[p] = jnp.where(row_iota == i, y8, yacc[p])
        for p in range(G):
            s_ref[p] = states[p]
            y = yacc[p]
            mean = seg_sum(y, lo8) * (1.0 / Dh)
            d = y - mean
            var = seg_sum(d * d, lo8) * (1.0 / Dh)
            yn = d * lax.rsqrt(var + RW_LN_EPS) * lng_ref[:, cols[p]] + lnb_ref[:, cols[p]]
            r8 = r_ref[0, pl.ds(t0, SUB), cols[p]]
            k8 = k_ref[0, pl.ds(t0, SUB), cols[p]]
            v8 = v_ref[0, pl.ds(t0, SUB), cols[p]]
            bonus = seg_sum(r8 * k8 * rk_ref[:, cols[p]], lo8) * v8
            o_ref[0, pl.ds(t0, SUB), cols[p]] = (yn + bonus) * g_ref[0, pl.ds(t0, SUB), cols[p]]
        return carry

    lax.fori_loop(0, T // SUB, group, 0)


def rwkv_scan(r, w, k, v, kk, b, g, ln_g, ln_b, r_k, G=8, T=256):
    B, S, D = r.shape
    n_pairs = D // LANES
    assert n_pairs % G == 0 and S % T == 0
    spec = pl.BlockSpec((1, T, G * LANES), lambda bi, pi, ti: (bi, ti, pi))
    vec = pl.BlockSpec((1, G * LANES), lambda bi, pi, ti: (0, pi))
    row = lambda t: t.reshape(1, D)
    return pl.pallas_call(
        functools.partial(_rwkv_scan_body, G=G, T=T),
        grid=(B, n_pairs // G, S // T),
        in_specs=[spec] * 7 + [vec] * 3,
        out_specs=spec,
        out_shape=jax.ShapeDtypeStruct((B, S, D), f32),
        scratch_shapes=[pltpu.VMEM((G, RW_HEAD_DIM, LANES), f32)],
        compiler_params=pltpu.CompilerParams(dimension_semantics=("parallel", "parallel", "arbitrary"),
                                             vmem_limit_bytes=VMEM_LIMIT),
        name="rwkv_scan",
    )(r, w, k, v, kk, b, g, row(ln_g), row(ln_b), row(r_k))


def rwkv7_mixer(proj, rkv_blk, lora_blk, mu, w0, w2, a0, a2, g2, k_k, k_a, r_k, ln_g, ln_b, B, S):
    outs = rwkv_prep(proj, rkv_blk, lora_blk, mu, w0, w2, a0, a2, g2, k_k, k_a, S)
    r, w, k, v, kk, b, g = [t.reshape(B, S, D_GRP) for t in outs]
    return rwkv_scan(r, w, k, v, kk, b, g, ln_g, ln_b, r_k.reshape(-1))


def _mm_body(a_ref, b_ref, o_ref, acc_ref):
    k = pl.program_id(2)

    @pl.when(k == 0)
    def _():
        acc_ref[...] = jnp.zeros_like(acc_ref)

    acc_ref[...] += jnp.dot(a_ref[...].astype(bf16), b_ref[...].astype(bf16), preferred_element_type=f32)

    @pl.when(k == pl.num_programs(2) - 1)
    def _():
        o_ref[...] = acc_ref[...].astype(o_ref.dtype)


def matmul(a, b, out_dtype=f32, tm=1024, tn=1024, tk=512, name="matmul"):
    M, K = a.shape
    _, N = b.shape
    tm, tn, tk = min(tm, M), min(tn, N), min(tk, K)
    assert M % tm == 0 and N % tn == 0 and K % tk == 0, (M, N, K, tm, tn, tk)
    return pl.pallas_call(
        _mm_body,
        grid=(M // tm, N // tn, K // tk),
        in_specs=[pl.BlockSpec((tm, tk), lambda i, j, k: (i, k)),
                  pl.BlockSpec((tk, tn), lambda i, j, k: (k, j))],
        out_specs=pl.BlockSpec((tm, tn), lambda i, j, k: (i, j)),
        out_shape=jax.ShapeDtypeStruct((M, N), out_dtype),
        scratch_shapes=[pltpu.VMEM((tm, tn), f32)],
        compiler_params=pltpu.CompilerParams(
            dimension_semantics=("parallel", "parallel", "arbitrary"), vmem_limit_bytes=VMEM_LIMIT),
        name=name,
    )(a, b)


def _rms_norm_body(x_ref, g_ref, o_ref):
    x = x_ref[...]
    y = x * lax.rsqrt(jnp.mean(x * x, axis=-1, keepdims=True) + NORM_EPS)
    o_ref[...] = (y * g_ref[...]).astype(o_ref.dtype)


def rms_norm_rows(x, g, out_dtype, tm=256):
    M, D = x.shape
    return pl.pallas_call(
        _rms_norm_body,
        grid=(M // tm,),
        in_specs=[pl.BlockSpec((tm, D), lambda i: (i, 0)), pl.BlockSpec((1, D), lambda i: (0, 0))],
        out_specs=pl.BlockSpec((tm, D), lambda i: (i, 0)),
        out_shape=jax.ShapeDtypeStruct((M, D), out_dtype),
        compiler_params=pltpu.CompilerParams(dimension_semantics=("parallel",), vmem_limit_bytes=VMEM_LIMIT),
        name="rms_norm",
    )(x, g.reshape(1, D).astype(f32))


def _out_proj_body(a0_ref, a1_ref, a2_ref, a3_ref, b_ref, h_ref, o_ref, acc_ref):
    k = pl.program_id(2)

    @pl.when(k == 0)
    def _():
        acc_ref[...] = h_ref[...]

    for j, a_ref in enumerate((a0_ref, a1_ref, a2_ref, a3_ref)):
        @pl.when(k == j)
        def _(a_ref=a_ref):
            acc_ref[...] += jnp.dot(a_ref[...].astype(bf16), b_ref[...], preferred_element_type=f32)

    @pl.when(k == pl.num_programs(2) - 1)
    def _():
        o_ref[...] = acc_ref[...]


def out_proj_residual(parts, w, h, tm=1024, tn=1024):
    M, D = h.shape
    aspec = pl.BlockSpec((tm, D_GRP), lambda i, j, k: (i, 0))
    return pl.pallas_call(
        _out_proj_body,
        grid=(M // tm, D // tn, len(parts)),
        in_specs=[aspec] * 4 + [pl.BlockSpec((D_GRP, tn), lambda i, j, k: (k, j)),
                                pl.BlockSpec((tm, tn), lambda i, j, k: (i, j))],
        out_specs=pl.BlockSpec((tm, tn), lambda i, j, k: (i, j)),
        out_shape=jax.ShapeDtypeStruct((M, D), f32),
        scratch_shapes=[pltpu.VMEM((tm, tn), f32)],
        compiler_params=pltpu.CompilerParams(
            dimension_semantics=("parallel", "parallel", "arbitrary"), vmem_limit_bytes=VMEM_LIMIT),
        name="out_proj",
    )(*parts, w, h)


def _ple_body(a_ref, b_ref, p_ref, pw_ref, h_ref, o_ref, acc_ref):
    k = pl.program_id(2)

    @pl.when(k == 0)
    def _():
        acc_ref[...] = jnp.zeros_like(acc_ref)

    acc_ref[...] += jnp.dot(a_ref[...], b_ref[...], preferred_element_type=f32)

    @pl.when(k == pl.num_programs(2) - 1)
    def _():
        emb = jnp.dot(p_ref[...].astype(bf16), pw_ref[...], preferred_element_type=f32)
        o_ref[...] = h_ref[...] + emb * jax.nn.sigmoid(acc_ref[...])


def ple_residual(hn, gate_w, p, ple_w, h, tm=1024, tn=1024, tk=512):
    M, D = h.shape
    return pl.pallas_call(
        _ple_body,
        grid=(M // tm, D // tn, D // tk),
        in_specs=[pl.BlockSpec((tm, tk), lambda i, j, k: (i, k)),
                  pl.BlockSpec((tk, tn), lambda i, j, k: (k, j)),
                  pl.BlockSpec((tm, PLE_DIM), lambda i, j, k: (i, 0)),
                  pl.BlockSpec((PLE_DIM, tn), lambda i, j, k: (0, j)),
                  pl.BlockSpec((tm, tn), lambda i, j, k: (i, j))],
        out_specs=pl.BlockSpec((tm, tn), lambda i, j, k: (i, j)),
        out_shape=jax.ShapeDtypeStruct((M, D), f32),
        scratch_shapes=[pltpu.VMEM((tm, tn), f32)],
        compiler_params=pltpu.CompilerParams(
            dimension_semantics=("parallel", "parallel", "arbitrary"), vmem_limit_bytes=VMEM_LIMIT),
        name="ple",
    )(hn, gate_w, p, ple_w, h)


def _moe_router_body(x_ref, w_ref, b_ref, id_ref, wt_ref):
    logits = jnp.dot(x_ref[...], w_ref[...], preferred_element_type=f32) + b_ref[...]
    lane = lax.broadcasted_iota(jnp.int32, logits.shape, 1).astype(f32)
    ninf = -jnp.inf

    def top1(vals):
        m = jnp.max(vals, axis=-1, keepdims=True)
        idx = jnp.min(jnp.where(vals == m, lane, float(LANES)), axis=-1, keepdims=True)
        return m, idx

    gl = jnp.where(lane < N_GROUPS, logits, ninf)
    g_max, g_idx = top1(gl)
    g_w = 1.0 / jnp.sum(jnp.exp(gl - g_max), axis=-1, keepdims=True)
    lo = N_GROUPS + g_idx * EXPERTS_PER_GROUP
    el = jnp.where((lane >= lo) & (lane < lo + EXPERTS_PER_GROUP), logits, ninf)
    e_max, _ = top1(el)
    pe = jnp.exp(el - e_max)
    p = pe / jnp.sum(pe, axis=-1, keepdims=True)
    p = jnp.where(el == ninf, ninf, p)
    p1, i1 = top1(p)
    p2, i2 = top1(jnp.where(lane == i1, ninf, p))
    scale = g_w / (p1 + p2)
    id_ref[...] = jnp.where(lane == 0, i1 - N_GROUPS, jnp.where(lane == 1, i2 - N_GROUPS, 0.0)).astype(jnp.int32)
    wt_ref[...] = jnp.where(lane == 0, p1 * scale, jnp.where(lane == 1, p2 * scale, 0.0))


def moe_router(xb, router_grp, router_grp_b, router_exp, router_exp_b, tm=512):
    M, D = xb.shape
    n_log = N_GROUPS + N_EXPERTS
    w = jnp.pad(jnp.concatenate([router_grp, router_exp], axis=1), ((0, 0), (0, LANES - n_log))).astype(bf16)
    b = jnp.pad(jnp.concatenate([router_grp_b, router_exp_b]), (0, LANES - n_log)).reshape(1, LANES).astype(f32)
    ids, wts = pl.pallas_call(
        _moe_router_body,
        grid=(M // tm,),
        in_specs=[pl.BlockSpec((tm, D), lambda i: (i, 0)), pl.BlockSpec((D, LANES), lambda i: (0, 0)),
                  pl.BlockSpec((1, LANES), lambda i: (0, 0))],
        out_specs=[pl.BlockSpec((tm, LANES), lambda i: (i, 0))] * 2,
        out_shape=[jax.ShapeDtypeStruct((M, LANES), jnp.int32), jax.ShapeDtypeStruct((M, LANES), f32)],
        compiler_params=pltpu.CompilerParams(dimension_semantics=("parallel",), vmem_limit_bytes=VMEM_LIMIT),
        name="moe_router",
    )(xb, w, b)
    return ids[:, :TOP_K], wts[:, :TOP_K]


def _moe_up_body(be_ref, nu_ref, x_ref, wg_ref, wu_ref, h_ref):
    i = pl.program_id(1)

    @pl.when(i < nu_ref[0])
    def _():
        x = x_ref[...]
        hg = jnp.dot(x, wg_ref[0].astype(bf16), preferred_element_type=f32)
        hu = jnp.dot(x, wu_ref[0].astype(bf16), preferred_element_type=f32)
        h_ref[...] = (hg * jax.nn.sigmoid(hg) * hu).astype(bf16)

    @pl.when(i >= nu_ref[0])
    def _():
        h_ref[...] = jnp.zeros_like(h_ref)


def _moe_down_body(be_ref, nu_ref, h_ref, wd_ref, wt_ref, y_ref):
    i = pl.program_id(0)

    @pl.when(i < nu_ref[0])
    def _():
        y_ref[...] = jnp.dot(h_ref[...], wd_ref[0].astype(bf16), preferred_element_type=f32) * wt_ref[...]

    @pl.when(i >= nu_ref[0])
    def _():
        y_ref[...] = jnp.zeros_like(y_ref)


def moe_experts(x_pad, w_pad, blk_expert, n_used, wg, wu, wd, tf=256):
    A_pad, D = x_pad.shape
    n_blk = A_pad // ROW_BLOCK
    nf = D_EXPERT // tf
    h = pl.pallas_call(
        _moe_up_body,
        grid_spec=pltpu.PrefetchScalarGridSpec(
            num_scalar_prefetch=2,
            grid=(nf, n_blk),
            in_specs=[pl.BlockSpec((ROW_BLOCK, D), lambda f, i, be, nu: (i, 0)),
                      pl.BlockSpec((1, D, tf), lambda f, i, be, nu: (be[i], 0, f)),
                      pl.BlockSpec((1, D, tf), lambda f, i, be, nu: (be[i], 0, f))],
            out_specs=pl.BlockSpec((ROW_BLOCK, tf), lambda f, i, be, nu: (i, f)),
        ),
        out_shape=jax.ShapeDtypeStruct((A_pad, D_EXPERT), bf16),
        compiler_params=pltpu.CompilerParams(
            dimension_semantics=("arbitrary", "arbitrary"), vmem_limit_bytes=VMEM_LIMIT),
        name="moe_up",
    )(blk_expert, n_used, x_pad, wg, wu)
    return pl.pallas_call(
        _moe_down_body,
        grid_spec=pltpu.PrefetchScalarGridSpec(
            num_scalar_prefetch=2,
            grid=(n_blk,),
            in_specs=[pl.BlockSpec((ROW_BLOCK, D_EXPERT), lambda i, be, nu: (i, 0)),
                      pl.BlockSpec((1, D_EXPERT, D), lambda i, be, nu: (be[i], 0, 0)),
                      pl.BlockSpec((ROW_BLOCK, 1), lambda i, be, nu: (i, 0))],
            out_specs=pl.BlockSpec((ROW_BLOCK, D), lambda i, be, nu: (i, 0)),
        ),
        out_shape=jax.ShapeDtypeStruct((A_pad, D), f32),
        compiler_params=pltpu.CompilerParams(dimension_semantics=("arbitrary",), vmem_limit_bytes=VMEM_LIMIT),
        name="moe_down",
    )(blk_expert, n_used, h, wd, w_pad.reshape(A_pad, 1))


def hier_moe(xb, router_grp, router_grp_b, router_exp, router_exp_b, w_gate, w_up, w_down):
    M, D = xb.shape
    expert_ids, weights = moe_router(xb, router_grp, router_grp_b, router_exp, router_exp_b)
    A = M * TOP_K
    e_flat = expert_ids.reshape(A)
    onehot = (e_flat[:, None] == jnp.arange(N_EXPERTS)[None, :]).astype(jnp.int32)
    rank = jnp.take_along_axis(jnp.cumsum(onehot, axis=0), e_flat[:, None], axis=1)[:, 0] - 1
    counts = jnp.sum(onehot, axis=0)
    padded = (counts + ROW_BLOCK - 1) // ROW_BLOCK * ROW_BLOCK
    end_pad = jnp.cumsum(padded)
    dest = (end_pad - padded)[e_flat] + rank
    n_blk = -(-A // ROW_BLOCK) + N_EXPERTS
    A_pad = n_blk * ROW_BLOCK
    tok_pad = jnp.zeros((A_pad,), jnp.int32).at[dest].set(jnp.arange(A, dtype=jnp.int32) // TOP_K)
    w_pad = jnp.zeros((A_pad,), f32).at[dest].set(weights.reshape(A))
    blk_expert = jnp.minimum(jnp.searchsorted(end_pad, jnp.arange(n_blk) * ROW_BLOCK, side='right'),
                             N_EXPERTS - 1).astype(jnp.int32)
    n_used = (end_pad[-1:] // ROW_BLOCK).astype(jnp.int32)
    x_pad = xb[tok_pad]
    y_pad = moe_experts(x_pad, w_pad, blk_expert, n_used, w_gate, w_up, w_down)
    rows = dest.reshape(M, TOP_K)
    return y_pad[rows[:, 0]] + y_pad[rows[:, 1]]


def _rope_tables(S, theta, rot_dims, width):
    half = rot_dims // 2
    freqs = theta ** (-np.arange(half, dtype=np.float64) / half)
    ang = np.arange(S, dtype=np.float64)[:, None] * freqs[None, :]
    cos = np.ones((S, width), np.float64)
    sin_lo = np.zeros((S, width), np.float64)
    sin_hi = np.zeros((S, width), np.float64)
    cos[:, :half] = np.cos(ang)
    cos[:, half:rot_dims] = np.cos(ang)
    sin_lo[:, :half] = -np.sin(ang)
    sin_hi[:, half:rot_dims] = np.sin(ang)
    return [jnp.asarray(t, f32) for t in (cos, sin_lo, sin_hi)]


def _rope128(x, cos, sin_lo, sin_hi, half):
    return x * cos + pltpu.roll(x, LANES - half, axis=1) * sin_lo + pltpu.roll(x, half, axis=1) * sin_hi


def _nsa_prep_body(q_ref, kv_ref, cos_ref, sl_ref, sh_ref, qo_ref, kvo_ref, co_ref):
    cos, sl, sh = cos_ref[...], sl_ref[...], sh_ref[...]
    half = ROPE_DIMS // 2
    scale = HEAD_DIM ** -0.5
    for h in range(NSA_HEADS):
        c = slice(h * LANES, (h + 1) * LANES)
        qo_ref[:, c] = (_rope128(q_ref[:, c], cos, sl, sh, half) * scale).astype(bf16)
    for j in range(6):
        for g in range(NSA_KV_HEADS):
            c = slice((j * NSA_KV_HEADS + g) * LANES, (j * NSA_KV_HEADS + g + 1) * LANES)
            x = kv_ref[:, c]
            if j % 2 == 0:
                x = _rope128(x, cos, sl, sh, half)
            if j < 2:
                co_ref[:, c] = x.astype(bf16)
            else:
                kvo_ref[:, slice(c.start - 4 * LANES, c.stop - 4 * LANES)] = x.astype(bf16)


def nsa_prep(proj, S, tm=512):
    M = proj.shape[0]
    assert M % tm == 0 and S % tm == 0
    tabs = _rope_tables(S, ROPE_THETA, ROPE_DIMS, LANES)
    tspec = pl.BlockSpec((tm, LANES), lambda i: (i % (S // tm), 0))
    return pl.pallas_call(
        _nsa_prep_body,
        grid=(M // tm,),
        in_specs=[pl.BlockSpec((tm, D_GRP), lambda i: (i, OFF_NSA_Q // D_GRP)),
                  pl.BlockSpec((tm, 6 * NSA_KV_W), lambda i: (i, OFF_NSA_KV // (6 * NSA_KV_W))),
                  tspec, tspec, tspec],
        out_specs=[pl.BlockSpec((tm, D_GRP), lambda i: (i, 0)),
                   pl.BlockSpec((tm, 4 * NSA_KV_W), lambda i: (i, 0)),
                   pl.BlockSpec((tm, 2 * NSA_KV_W), lambda i: (i, 0))],
        out_shape=[jax.ShapeDtypeStruct((M, D_GRP), bf16), jax.ShapeDtypeStruct((M, 4 * NSA_KV_W), bf16),
                   jax.ShapeDtypeStruct((M, 2 * NSA_KV_W), bf16)],
        compiler_params=pltpu.CompilerParams(dimension_semantics=("parallel",), vmem_limit_bytes=VMEM_LIMIT),
        name="nsa_prep",
    )(proj, proj, *tabs)


def _nsa_compress_body(r_ref, w1_ref, w2_ref, pe_ref, o_ref):
    r = r_ref[0, 0, 0]
    half_w = CMP_STRIDE * HEAD_DIM
    a = jnp.dot(r, w1_ref[0, :half_w, :], preferred_element_type=f32)
    b = jnp.dot(r, w1_ref[0, half_w:, :], preferred_element_type=f32)
    n = r.shape[0]
    pe = jnp.dot(jnp.broadcast_to(pe_ref[0], (SUB, pe_ref.shape[-1])), w1_ref[0], preferred_element_type=f32)[0:1]
    pre = a + pltpu.roll(b, n - 1, axis=0) + pe
    hid = pre * jax.nn.sigmoid(pre)
    o_ref[0, 0, 0] = jnp.dot(hid.astype(bf16), w2_ref[0], preferred_element_type=f32).astype(bf16)


def nsa_compress(c_rot, w1s, w2s, pes, B, S):
    G, Dh = NSA_KV_HEADS, HEAD_DIM
    nc = S // CMP_STRIDE
    r = c_rot.reshape(B, nc, CMP_STRIDE, 2, G, Dh).transpose(3, 0, 4, 1, 2, 5).reshape(2, B, G, nc, CMP_STRIDE * Dh)
    w1 = jnp.stack(w1s).astype(bf16)
    w2 = jnp.stack(w2s).astype(bf16)
    pe = jnp.stack(pes).reshape(2, 1, CMP_LEN * Dh).astype(bf16)
    return pl.pallas_call(
        _nsa_compress_body,
        grid=(2, B, G),
        in_specs=[pl.BlockSpec((1, 1, 1, nc, CMP_STRIDE * Dh), lambda j, b, g: (j, b, g, 0, 0)),
                  pl.BlockSpec((1, CMP_LEN * Dh, Dh), lambda j, b, g: (j, 0, 0)),
                  pl.BlockSpec((1, Dh, Dh), lambda j, b, g: (j, 0, 0)),
                  pl.BlockSpec((1, 1, CMP_LEN * Dh), lambda j, b, g: (j, 0, 0))],
        out_specs=pl.BlockSpec((1, 1, 1, nc, Dh), lambda j, b, g: (j, b, g, 0, 0)),
        out_shape=jax.ShapeDtypeStruct((2, B, G, nc, Dh), bf16),
        compiler_params=pltpu.CompilerParams(dimension_semantics=("parallel", "parallel", "parallel"),
                                             vmem_limit_bytes=VMEM_LIMIT),
        name="nsa_compress",
    )(r, w1, w2, pe)


def _softmax_rows(s, mask):
    s = jnp.where(mask, s, -jnp.inf)
    m = jnp.max(s, axis=-1, keepdims=True)
    m = jnp.where(m == -jnp.inf, 0.0, m)
    e = jnp.exp(s - m)
    return e, jnp.maximum(jnp.sum(e, axis=-1, keepdims=True), jnp.finfo(f32).tiny)


def _nsa_attn_body(q_ref, cmp_ref, ks_ref, vs_ref, kw_ref, vw_ref, gate_ref, ov_ref, o_ref, *, tq, tk, S):
    g = pl.program_id(1)
    qi = pl.program_id(2)
    R, Dh = NSA_GROUP, HEAD_DIM
    rows = R * tq
    n_blk = S // SEL_LEN
    n_cmp = S // CMP_STRIDE - 1
    q0 = qi * tq
    q4 = jnp.concatenate([q_ref[0, :, r * Dh:(r + 1) * Dh] for r in range(R)], axis=0)
    nt = (((1,), (1,)), ((), ()))

    def qpos(shape):
        return q0 + lax.broadcasted_iota(jnp.int32, shape, 0) % tq

    k_cmp, v_cmp = cmp_ref[0, 0, 0], cmp_ref[1, 0, 0]
    ncp = k_cmp.shape[0]
    s = lax.dot_general(q4, k_cmp, nt, preferred_element_type=f32)
    n_idx = lax.broadcasted_iota(jnp.int32, (rows, ncp), 1)
    mask = (n_idx * CMP_STRIDE + (CMP_LEN - 1) <= qpos((rows, ncp))) & (n_idx < n_cmp)
    e, den = _softmax_rows(s, mask)
    p = e / den
    o_cmp = jnp.dot(p.astype(bf16), v_cmp, preferred_element_type=f32)

    psum = p[0:tq]
    for r in range(1, R):
        psum = psum + p[r * tq:(r + 1) * tq]
    ph, plo = _split2(psum)
    imp = (jnp.dot(ph, ov_ref[...], preferred_element_type=f32)
           + jnp.dot(plo, ov_ref[...], preferred_element_type=f32))
    blk = lax.broadcasted_iota(jnp.int32, (tq, LANES), 1)
    cur = (q0 + lax.broadcasted_iota(jnp.int32, (tq, LANES), 0)) // SEL_LEN
    visible = blk <= cur
    forced = (blk == 0) | (visible & (blk > cur - N_LOCAL_FORCED))
    imp = jnp.where(forced, FORCED_SCORE, jnp.where(visible, imp, -1.0))
    imp = jnp.where(blk < n_blk, imp, -jnp.inf)
    sel = jnp.zeros((tq, LANES), f32)
    blk_f = blk.astype(f32)
    for _ in range(min(N_SEL, n_blk)):
        m = jnp.max(imp, axis=-1, keepdims=True)
        first = jnp.min(jnp.where(imp == m, blk_f, float(LANES)), axis=-1, keepdims=True)
        hit = blk_f == first
        sel = jnp.where(hit, 1.0, sel)
        imp = jnp.where(hit, -jnp.inf, imp)
    sel4 = jnp.concatenate([sel.astype(bf16)] * R, axis=0)

    blocks_per_tile = tk // SEL_LEN

    def sel_step(kt, carry):
        m_run, l_run, acc = carry
        k0 = pl.multiple_of(kt * tk, tk)
        kt_k = ks_ref[0, pl.ds(k0, tk), :]
        kt_v = vs_ref[0, pl.ds(k0, tk), :]
        s = lax.dot_general(q4, kt_k, nt, preferred_element_type=f32)
        expand = (lax.broadcasted_iota(jnp.int32, (LANES, tk), 0)
                  == kt * blocks_per_tile + lax.broadcasted_iota(jnp.int32, (LANES, tk), 1) // SEL_LEN)
        chosen = jnp.dot(sel4, expand.astype(bf16), preferred_element_type=f32)
        kpos = k0 + lax.broadcasted_iota(jnp.int32, (rows, tk), 1)
        mask = (chosen > 0.5) & (kpos <= qpos((rows, tk)))
        s = jnp.where(mask, s, -jnp.inf)
        m_new = jnp.maximum(m_run, jnp.max(s, axis=-1, keepdims=True))
        m_safe = jnp.where(m_new == -jnp.inf, 0.0, m_new)
        alpha = jnp.exp(m_run - m_safe)
        e = jnp.exp(s - m_safe)
        l_new = alpha * l_run + jnp.sum(e, axis=-1, keepdims=True)
        acc = alpha * acc + jnp.dot(e.astype(bf16), kt_v, preferred_element_type=f32)
        return m_new, l_new, acc

    n_kt = (q0 + tq - 1) // tk + 1
    init = (jnp.full((rows, 1), -jnp.inf, f32), jnp.zeros((rows, 1), f32), jnp.zeros((rows, Dh), f32))
    _, l_sel, acc_sel = lax.fori_loop(0, n_kt, sel_step, init)
    o_sel = acc_sel / jnp.maximum(l_sel, jnp.finfo(f32).tiny)

    span = WINDOW + tq
    w0 = pl.multiple_of(jnp.maximum(q0 - WINDOW, 0), tq)
    kwin = kw_ref[0, pl.ds(w0, span), :]
    vwin = vw_ref[0, pl.ds(w0, span), :]
    s = lax.dot_general(q4, kwin, nt, preferred_element_type=f32)
    kpos = w0 + lax.broadcasted_iota(jnp.int32, (rows, span), 1)
    qp = qpos((rows, span))
    e, den = _softmax_rows(s, (kpos <= qp) & (kpos > qp - WINDOW))
    o_win = jnp.dot(e.astype(bf16), vwin, preferred_element_type=f32) / den

    gate = jax.nn.sigmoid(gate_ref[0])
    for r in range(R):
        c0 = (g * R + r) * 3
        lane = lax.broadcasted_iota(jnp.int32, (tq, LANES), 1)
        gsel = lambda c: jnp.sum(jnp.where(lane == c0 + c, gate, 0.0), axis=-1, keepdims=True)
        rs = slice(r * tq, (r + 1) * tq)
        o_ref[0, :, r * Dh:(r + 1) * Dh] = (gsel(0) * o_cmp[rs] + gsel(1) * o_sel[rs]
                                            + gsel(2) * o_win[rs]).astype(o_ref.dtype)


def nsa_attention(q_rot, kv_rot, cmp_kv, proj3, B, S, tq=128, tk=512):
    G, R, Dh = NSA_KV_HEADS, NSA_GROUP, HEAD_DIM
    assert S % tk == 0 and tk % tq == 0 and S >= WINDOW + tq and S // SEL_LEN <= LANES
    nc = S // CMP_STRIDE
    c0 = np.arange(nc - 1)[:, None] * CMP_STRIDE
    s0 = np.arange(S // SEL_LEN)[None, :] * SEL_LEN
    ov = np.zeros((nc, LANES), np.float32)
    ov[:nc - 1, :S // SEL_LEN] = np.clip(np.minimum(c0 + CMP_LEN, s0 + SEL_LEN) - np.maximum(c0, s0), 0, None) / CMP_LEN
    kvspec = lambda j: pl.BlockSpec((1, S, Dh), lambda b, g, i: (b, 0, j * G + g))
    return pl.pallas_call(
        functools.partial(_nsa_attn_body, tq=tq, tk=tk, S=S),
        grid=(B, G, S // tq),
        in_specs=[pl.BlockSpec((1, tq, R * Dh), lambda b, g, i: (b, i, g)),
                  pl.BlockSpec((2, 1, 1, nc, Dh), lambda b, g, i: (0, b, g, 0, 0)),
                  kvspec(0), kvspec(1), kvspec(2), kvspec(3),
                  pl.BlockSpec((1, tq, LANES), lambda b, g, i: (b, i, OFF_SMALL // LANES)),
                  pl.BlockSpec((nc, LANES), lambda b, g, i: (0, 0))],
        out_specs=pl.BlockSpec((1, tq, R * Dh), lambda b, g, i: (b, i, g)),
        out_shape=jax.ShapeDtypeStruct((B, S, D_GRP), bf16),
        compiler_params=pltpu.CompilerParams(dimension_semantics=("parallel", "parallel", "arbitrary"),
                                             vmem_limit_bytes=VMEM_LIMIT),
        name="nsa_attention",
    )(q_rot, cmp_kv, kv_rot, kv_rot, kv_rot, kv_rot, proj3, jnp.asarray(ov, bf16))


def nsa_mixer_pallas(proj, ck_w1, ck_w2, ck_pe, cv_w1, cv_w2, cv_pe, B, S):
    q_rot, kv_rot, c_rot = nsa_prep(proj, S)
    cmp_kv = nsa_compress(c_rot, (ck_w1, cv_w1), (ck_w2, cv_w2), (ck_pe, cv_pe), B, S)
    return nsa_attention(q_rot.reshape(B, S, D_GRP), kv_rot.reshape(B, S, 4 * NSA_KV_W), cmp_kv,
                         proj.reshape(B, S, P_PAD), B, S)


def _retention_body(q_ref, k_ref, v_ref, g_ref, cos_ref, sin_ref, dec_ref, xi_ref, zeta_ref, gc_ref, o_ref, st_ref,
                    *, T):
    @pl.when(pl.program_id(2) == 0)
    def _():
        st_ref[...] = jnp.zeros_like(st_ref)

    C, Dh = RET_CHUNK, HEAD_DIM
    nt = (((1,), (1,)), ((), ()))
    decay_in, xi, zeta, gamma_c = dec_ref[0], xi_ref[0], zeta_ref[0], gc_ref[0, 0:1, :]
    state = st_ref[...]
    for c in range(T // C):
        rs = slice(c * C, (c + 1) * C)
        cos, sin = cos_ref[rs, :], sin_ref[rs, :]
        rope = lambda x: x * cos + pltpu.roll(x, Dh // 2, axis=1) * sin
        q = rope(q_ref[0, rs, :])
        k = rope(k_ref[0, rs, :]) * Dh ** -0.5
        qb, vb = q.astype(bf16), v_ref[0, rs, :].astype(bf16)
        inner = lax.dot_general(qb, k.astype(bf16), nt, preferred_element_type=f32) * decay_in
        o = (jnp.dot(inner.astype(bf16), vb, preferred_element_type=f32)
             + jnp.dot(qb, state.astype(bf16), preferred_element_type=f32) * xi)
        kz = (k * zeta).astype(bf16)
        state = state * gamma_c + jnp.dot(kz.T, vb, preferred_element_type=f32)
        mu = jnp.mean(o, axis=-1, keepdims=True)
        d = o - mu
        o = d * lax.rsqrt(jnp.mean(d * d, axis=-1, keepdims=True) + NORM_EPS)
        gt = g_ref[0, rs, :]
        o_ref[0, rs, :] = (gt * jax.nn.sigmoid(gt) * o).astype(o_ref.dtype)
    st_ref[...] = state


def retention_mixer_pallas(proj3, B, S, T=512):
    H, Dh, C = RET_HEADS, HEAD_DIM, RET_CHUNK
    assert S % T == 0 and T % C == 0
    freqs = RET_THETA ** (-np.linspace(0.0, 1.0, Dh // 2))
    ang = np.arange(S, dtype=np.float64)[:, None] * freqs[None, :]
    cos = jnp.asarray(np.concatenate([np.cos(ang), np.cos(ang)], 1), f32)
    sin = jnp.asarray(np.concatenate([-np.sin(ang), np.sin(ang)], 1), f32)
    log_gamma = np.log1p(-(2.0 ** (-5.0 - np.arange(H, dtype=np.float64))))
    n = np.arange(C, dtype=np.float64)
    diff = n[:, None] - n[None, :]
    decay_in = np.where(diff >= 0, np.exp(log_gamma[:, None, None] * np.maximum(diff, 0.0)), 0.0)
    xi = np.exp(log_gamma[:, None] * (n + 1.0))[:, :, None] * np.ones((1, 1, Dh))
    zeta = np.exp(log_gamma[:, None] * (C - 1.0 - n))[:, :, None] * np.ones((1, 1, Dh))
    gamma_c = np.exp(log_gamma * C)[:, None, None] * np.ones((1, SUB, Dh))
    blk = lambda j: pl.BlockSpec((1, T, Dh), lambda b, h, t: (b, t, (OFF_RET + j * D_GRP) // Dh + h))
    tab = pl.BlockSpec((T, Dh), lambda b, h, t: (t, 0))
    per_head = lambda r: pl.BlockSpec((1, r, Dh), lambda b, h, t: (h, 0, 0))
    return pl.pallas_call(
        functools.partial(_retention_body, T=T),
        grid=(B, H, S // T),
        in_specs=[blk(0), blk(1), blk(2), blk(3), tab, tab, per_head(C), per_head(C), per_head(C), per_head(SUB)],
        out_specs=pl.BlockSpec((1, T, Dh), lambda b, h, t: (b, t, h)),
        out_shape=jax.ShapeDtypeStruct((B, S, D_GRP), bf16),
        scratch_shapes=[pltpu.VMEM((Dh, Dh), f32)],
        compiler_params=pltpu.CompilerParams(dimension_semantics=("parallel", "parallel", "arbitrary"),
                                             vmem_limit_bytes=VMEM_LIMIT),
        name="retention",
    )(proj3, proj3, proj3, proj3, cos, sin, jnp.asarray(decay_in, f32), jnp.asarray(xi, f32),
      jnp.asarray(zeta, f32), jnp.asarray(gamma_c, f32))


def _dot3(a, b):
    ah, al = _split2(a)
    bh, bl = _split2(b)
    return (jnp.dot(ah, bh, preferred_element_type=f32) + jnp.dot(ah, bl, preferred_element_type=f32)
            + jnp.dot(al, bh, preferred_element_type=f32))


def _deltanet_body(q_ref, k_ref, v_ref, z_ref, qp_ref, kp_ref, vp_ref, ab_ref, cw_ref, alog_ref, dtb_ref, ng_ref,
                   o_ref, st_ref, *, T):
    h = pl.program_id(1)
    first = pl.program_id(2) == 0

    @pl.when(first)
    def _():
        st_ref[...] = jnp.zeros_like(st_ref)

    C, Dh = DN_CHUNK, HEAD_DIM
    nt = (((1,), (1,)), ((), ()))

    def conv_silu(x_ref, prev_ref, j):
        prev = jnp.where(first, 0.0, prev_ref[0])
        xcat = jnp.concatenate([prev, x_ref[0]], axis=0)
        w = cw_ref[:, j, :]
        y = sum(w[i:i + 1, :] * xcat[SUB - (CONV_WIDTH - 1) + i:SUB - (CONV_WIDTH - 1) + i + T, :]
                for i in range(CONV_WIDTH))
        return y * jax.nn.sigmoid(y)

    def l2n(x):
        return x * lax.rsqrt(jnp.sum(x * x, axis=-1, keepdims=True) + 1e-6)

    q_all = l2n(conv_silu(q_ref, qp_ref, 0)) * Dh ** -0.5
    k_all = l2n(conv_silu(k_ref, kp_ref, 1))
    v_all = conv_silu(v_ref, vp_ref, 2)
    ab = ab_ref[0]
    lane = lax.broadcasted_iota(jnp.int32, ab.shape, 1)
    a_col = jnp.sum(jnp.where(lane == 3 * NSA_HEADS + h, ab, 0.0), axis=-1, keepdims=True)
    b_col = jnp.sum(jnp.where(lane == 3 * NSA_HEADS + DN_HEADS + h, ab, 0.0), axis=-1, keepdims=True)
    beta_all = jax.nn.sigmoid(b_col)
    g_all = -jnp.exp(alog_ref[0, 0:1, :]) * jax.nn.softplus(a_col + dtb_ref[0, 0:1, :])

    ri = lax.broadcasted_iota(jnp.int32, (C, C), 0)
    ci = lax.broadcasted_iota(jnp.int32, (C, C), 1)
    tril, strict = ri >= ci, ri > ci
    ltri = tril.astype(bf16)
    eye = (ri == ci).astype(f32)
    chunks = [slice(c * C, (c + 1) * C) for c in range(T // C)]
    qs, ks_, vs_ = [q_all[rs] for rs in chunks], [k_all[rs] for rs in chunks], [v_all[rs] for rs in chunks]
    betas = [beta_all[rs] for rs in chunks]
    gcs = []
    for rs in chunks:
        gh, gm, gl = _split3(g_all[rs])
        gcs.append(jnp.dot(ltri, gh, preferred_element_type=f32) + jnp.dot(ltri, gm, preferred_element_type=f32)
                   + jnp.dot(ltri, gl, preferred_element_type=f32))
    decays = [jnp.exp(jnp.where(tril, gc[:, :C] - gc[:, :C].T, -jnp.inf)) for gc in gcs]
    kbs = [k.astype(bf16) for k in ks_]
    kbetas = [k * b for k, b in zip(ks_, betas)]
    ms = [jnp.where(strict, lax.dot_general(kbeta.astype(bf16), kb, nt, preferred_element_type=f32) * d, 0.0)
          for kbeta, kb, d in zip(kbetas, kbs, decays)]
    attns = [jnp.where(tril, lax.dot_general(q.astype(bf16), kb, nt, preferred_element_type=f32) * d, 0.0).astype(bf16)
             for q, kb, d in zip(qs, kbs, decays)]
    t_invs, pws = [eye - m for m in ms], ms
    for _ in range(5):
        pws = [_dot3(pw, pw) for pw in pws]
        t_invs = [_dot3(t, eye + pw) for t, pw in zip(t_invs, pws)]
    egcs = [jnp.exp(gc) for gc in gcs]
    tbs = [t.astype(bf16) for t in t_invs]
    us = [jnp.dot(tb, (v * b).astype(bf16), preferred_element_type=f32) for tb, v, b in zip(tbs, vs_, betas)]
    ws = [jnp.dot(tb, (kbeta * egc).astype(bf16), preferred_element_type=f32).astype(bf16)
          for tb, kbeta, egc in zip(tbs, kbetas, egcs)]
    state = st_ref[...]
    for c, rs in enumerate(chunks):
        g_last = gcs[c][C - 1:C, :]
        q_dec = (qs[c] * egcs[c]).astype(bf16)
        k_dec = (ks_[c] * jnp.exp(g_last - gcs[c])).astype(bf16)
        sb = state.astype(bf16)
        v_new = us[c] - jnp.dot(ws[c], sb, preferred_element_type=f32)
        vnb = v_new.astype(bf16)
        o = jnp.dot(q_dec, sb, preferred_element_type=f32) + jnp.dot(attns[c], vnb, preferred_element_type=f32)
        state = state * jnp.exp(g_last) + jnp.dot(k_dec.T, vnb, preferred_element_type=f32)
        o = o * lax.rsqrt(jnp.mean(o * o, axis=-1, keepdims=True) + NORM_EPS) * ng_ref[...]
        zt = z_ref[0, rs, :]
        o_ref[0, rs, :] = (o * (zt * jax.nn.sigmoid(zt))).astype(o_ref.dtype)
    st_ref[...] = state


def deltanet_mixer_pallas(proj3, conv_w, a_log, dt_bias, norm_g, B, S, T=256):
    H, Dh = DN_HEADS, HEAD_DIM
    assert S % T == 0 and T % DN_CHUNK == 0
    blk = lambda j: pl.BlockSpec((1, T, Dh), lambda b, h, t: (b, t, (OFF_DN + j * D_GRP) // Dh + h))
    prev = lambda j: pl.BlockSpec((1, SUB, Dh), lambda b, h, t: (b, jnp.maximum(t * (T // SUB) - 1, 0),
                                                                (OFF_DN + j * D_GRP) // Dh + h))
    per_head = pl.BlockSpec((1, SUB, Dh), lambda b, h, t: (h, 0, 0))
    bcast = lambda x: jnp.broadcast_to(x.astype(f32)[:, None, None], (H, SUB, Dh))
    return pl.pallas_call(
        functools.partial(_deltanet_body, T=T),
        grid=(B, H, S // T),
        in_specs=[blk(0), blk(1), blk(2), blk(3), prev(0), prev(1), prev(2),
                  pl.BlockSpec((1, T, LANES), lambda b, h, t: (b, t, OFF_SMALL // LANES)),
                  pl.BlockSpec((CONV_WIDTH, 3, Dh), lambda b, h, t: (0, 0, h)),
                  per_head, per_head,
                  pl.BlockSpec((1, Dh), lambda b, h, t: (0, 0))],
        out_specs=pl.BlockSpec((1, T, Dh), lambda b, h, t: (b, t, h)),
        out_shape=jax.ShapeDtypeStruct((B, S, D_GRP), bf16),
        scratch_shapes=[pltpu.VMEM((Dh, Dh), f32)],
        compiler_params=pltpu.CompilerParams(dimension_semantics=("parallel", "parallel", "arbitrary"),
                                             vmem_limit_bytes=VMEM_LIMIT),
        name="deltanet",
    )(proj3, proj3, proj3, proj3, proj3, proj3, proj3, proj3, conv_w.reshape(CONV_WIDTH, 3, D_GRP),
      bcast(a_log), bcast(dt_bias), norm_g.reshape(1, Dh))


def pack_w_in(w_in):
    w = w_in.astype(bf16)
    n0, r0, d0, w0 = 0, NSA_COLS, NSA_COLS + RET_COLS, NSA_COLS + RET_COLS + DN_COLS
    z = lambda n: jnp.zeros((w.shape[0], n), bf16)
    gate = w[:, n0 + D_GRP + 6 * NSA_KV_W:r0]
    dn_ab = w[:, d0 + 3 * D_GRP:d0 + 3 * D_GRP + 2 * DN_HEADS]
    return jnp.concatenate([
        w[:, w0:w0 + 3 * D_GRP],
        w[:, r0:d0],
        w[:, d0:d0 + 3 * D_GRP], w[:, d0 + 3 * D_GRP + 2 * DN_HEADS:w0],
        w[:, n0:n0 + D_GRP + 6 * NSA_KV_W],
        w[:, w0 + 3 * D_GRP:], z(RW_LORA_PAD - RW_LORA),
        gate, dn_ab, z(LANES - gate.shape[1] - dn_ab.shape[1]),
    ], axis=1)


def hybrid_mixer(h, hn, B, S, w_in, w_out, ck_w1, ck_w2, ck_pe, cv_w1, cv_w2, cv_pe, dn_conv, dn_a_log, dn_dt_bias,
                 dn_norm, rw_mu, rw_w0, rw_w2, rw_a0, rw_a2, rw_g2, rw_kk, rw_ka, rw_rk, rw_ln_g, rw_ln_b):
    M = B * S
    proj = matmul(hn, pack_w_in(w_in), name="in_proj")
    p3 = proj.reshape(B, S, P_PAD)
    o_nsa = nsa_mixer_pallas(proj, ck_w1, ck_w2, ck_pe, cv_w1, cv_w2, cv_pe, B, S)
    o_ret = retention_mixer_pallas(p3, B, S)
    o_dn = deltanet_mixer_pallas(p3, dn_conv, dn_a_log, dn_dt_bias, dn_norm, B, S, T=512)
    o_rw = rwkv7_mixer(proj, OFF_RW // (3 * D_GRP), OFF_LORA // RW_LORA_PAD, rw_mu, rw_w0, rw_w2, rw_a0, rw_a2,
                       rw_g2, rw_kk, rw_ka, rw_rk, rw_ln_g, rw_ln_b, B, S)
    parts = [o.reshape(M, D_GRP) for o in (o_nsa, o_ret, o_dn, o_rw)]
    return out_proj_residual(parts, w_out.astype(bf16), h)


def kernel(x, p, norm_mix, w_in, w_out, nsa_ck_w1, nsa_ck_w2, nsa_ck_pe, nsa_cv_w1, nsa_cv_w2, nsa_cv_pe, dn_conv, dn_a_log, dn_dt_bias, dn_norm, rw_mu, rw_w0, rw_w2, rw_a0, rw_a2, rw_g2, rw_kk, rw_ka, rw_rk, rw_ln_g, rw_ln_b, norm_moe, moe_router_grp, moe_router_grp_b, moe_router_exp, moe_router_exp_b, moe_w_gate, moe_w_up, moe_w_down, norm_ple, ple_w, ple_gate, norm_final):
    B, S, D = x.shape
    M = B * S
    h = x.reshape(M, D)
    for i in range(DEPTH):
        hn = rms_norm_rows(h, norm_mix[i], bf16)
        h = hybrid_mixer(h, hn, B, S, w_in[i], w_out[i], nsa_ck_w1[i], nsa_ck_w2[i], nsa_ck_pe[i], nsa_cv_w1[i],
                         nsa_cv_w2[i], nsa_cv_pe[i], dn_conv[i], dn_a_log[i], dn_dt_bias[i], dn_norm[i],
                         rw_mu[i], rw_w0[i], rw_w2[i], rw_a0[i], rw_a2[i], rw_g2[i], rw_kk[i], rw_ka[i],
                         rw_rk[i], rw_ln_g[i], rw_ln_b[i])
        hn = rms_norm_rows(h, norm_moe[i], bf16)
        h = h + hier_moe(hn, moe_router_grp[i], moe_router_grp_b[i], moe_router_exp[i], moe_router_exp_b[i],
                         moe_w_gate[i], moe_w_up[i], moe_w_down[i])
        hn = rms_norm_rows(h, norm_ple[i], bf16)
        h = ple_residual(hn, ple_gate[i].astype(bf16), p[i].reshape(M, PLE_DIM), ple_w[i].astype(bf16), h)
    return rms_norm_rows(h, norm_final, f32).reshape(B, S, D)
```

```python
import functools

import jax
import jax.numpy as jnp
import numpy as np
from jax import lax
from jax.experimental import pallas as pl
from jax.experimental.pallas import tpu as pltpu

D_MODEL = 4096
BATCH = 2
SEQ = 4096
DEPTH = 2
f32 = jnp.float32
bf16 = jnp.bfloat16
D_MIX = D_MODEL
D_GRP = D_MIX // 4
HEAD_DIM = 128
NORM_EPS = 1e-6
NSA_HEADS = D_GRP // HEAD_DIM
NSA_KV_HEADS = 2
NSA_GROUP = NSA_HEADS // NSA_KV_HEADS
NSA_KV_W = NSA_KV_HEADS * HEAD_DIM
CMP_LEN = 32
CMP_STRIDE = 16
SEL_LEN = 64
N_SEL = 16
N_LOCAL_FORCED = 2
FORCED_SCORE = 1e4
WINDOW = 512
SEL_QBLK = 64
WIN_QBLK = 128
ROPE_THETA = 500000.0
ROPE_DIMS = HEAD_DIM // 4
RET_HEADS = D_GRP // HEAD_DIM
RET_CHUNK = 128
RET_THETA = 10000.0
DN_HEADS = D_GRP // HEAD_DIM
DN_CHUNK = 64
CONV_WIDTH = 4
RW_HEAD_DIM = 64
RW_HEADS = D_GRP // RW_HEAD_DIM
RW_DECAY_LORA = 64
RW_AAA_LORA = 64
RW_GATE_LORA = 160
RW_LN_EPS = 64e-5
N_GROUPS = 4
EXPERTS_PER_GROUP = 8
N_EXPERTS = N_GROUPS * EXPERTS_PER_GROUP
TOP_K = 2
D_EXPERT = 768
ROW_BLOCK = 256
PLE_DIM = 256
NSA_COLS = D_GRP + 6 * NSA_KV_W + 3 * NSA_HEADS
RET_COLS = 4 * D_GRP
DN_COLS = 4 * D_GRP + 2 * DN_HEADS
RW_COLS = 3 * D_GRP + RW_DECAY_LORA + RW_AAA_LORA + RW_GATE_LORA
P_TOTAL = NSA_COLS + RET_COLS + DN_COLS + RW_COLS

RW_LORA = RW_DECAY_LORA + RW_AAA_LORA + RW_GATE_LORA
RW_LORA_PAD = 384
RW_GROUPS_PER_BODY = 4
LANES = 128
SUB = 8

OFF_RW = 0
OFF_RET = 3 * D_GRP
OFF_DN = OFF_RET + 4 * D_GRP
OFF_NSA_Q = OFF_DN + 4 * D_GRP
OFF_NSA_KV = OFF_NSA_Q + D_GRP
OFF_LORA = OFF_NSA_KV + 6 * NSA_KV_W
OFF_SMALL = OFF_LORA + RW_LORA_PAD
P_PAD = OFF_SMALL + LANES
assert P_PAD == 14336 and OFF_LORA % RW_LORA_PAD == 0

VMEM_LIMIT = 56 * 1024 * 1024


def split_cols(x, sizes):
    return jnp.split(x, np.cumsum(sizes)[:-1].tolist(), axis=-1)


def rms_norm(x, g, eps=NORM_EPS):
    xf = x.astype(f32)
    y = xf * lax.rsqrt(jnp.mean(xf * xf, -1, keepdims=True) + eps)
    return (y * g.astype(f32)).astype(x.dtype)


def l2norm(x, eps=1e-6):
    return x * lax.rsqrt(jnp.sum(x * x, -1, keepdims=True) + eps)


def head_layer_norm(x, eps):
    mu = jnp.mean(x, -1, keepdims=True)
    return (x - mu) * lax.rsqrt(jnp.var(x, -1, keepdims=True) + eps)


def masked_softmax(s, mask):
    s = jnp.where(mask, s.astype(f32), -jnp.inf)
    m = jnp.max(s, -1, keepdims=True)
    m = jnp.where(jnp.isfinite(m), m, 0.0)
    e = jnp.where(mask, jnp.exp(s - m), 0.0)
    return e / jnp.maximum(jnp.sum(e, -1, keepdims=True), jnp.finfo(f32).tiny)


def rotary(x, freqs, rot_dims):
    S = x.shape[1]
    half = rot_dims // 2
    ang = jnp.arange(S, dtype=f32)[:, None] * freqs[None, :]
    ang = ang.reshape((S,) + (1,) * (x.ndim - 3) + (half,))
    cos, sin = jnp.cos(ang), jnp.sin(ang)
    x1 = x[..., :half].astype(f32)
    x2 = x[..., half:rot_dims].astype(f32)
    out = jnp.concatenate([x1 * cos - x2 * sin, x2 * cos + x1 * sin, x[..., rot_dims:].astype(f32)], -1)
    return out.astype(x.dtype)


def cmp_sel_overlap(S):
    n_cmp = (S - CMP_LEN) // CMP_STRIDE + 1
    c0 = np.arange(n_cmp)[:, None] * CMP_STRIDE
    s0 = np.arange(S // SEL_LEN)[None, :] * SEL_LEN
    ov = np.clip(np.minimum(c0 + CMP_LEN, s0 + SEL_LEN) - np.maximum(c0, s0), 0, None) / CMP_LEN
    return jnp.asarray(ov, f32)


def nsa_mixer(q, kc, vc, ks, vs, kw, vw, gate_logit, ck_w1, ck_w2, ck_pe, cv_w1, cv_w2, cv_pe):
    B, S, _ = q.shape
    G, R, Dh = NSA_KV_HEADS, NSA_GROUP, HEAD_DIM
    scale = Dh ** -0.5
    half = ROPE_DIMS // 2
    freqs = ROPE_THETA ** (-jnp.arange(half, dtype=f32) / half)
    q = rotary(q.reshape(B, S, G, R, Dh), freqs, ROPE_DIMS)
    kc, vc, ks, vs, kw, vw = [t.reshape(B, S, G, Dh) for t in (kc, vc, ks, vs, kw, vw)]
    kc, ks, kw = [rotary(t, freqs, ROPE_DIMS) for t in (kc, ks, kw)]
    pos = jnp.arange(S)
    n_cmp = (S - CMP_LEN) // CMP_STRIDE + 1
    blk_tok = np.arange(n_cmp)[:, None] * CMP_STRIDE + np.arange(CMP_LEN)[None, :]

    def compress(t, w1, w2, pe):
        blk = t[:, blk_tok] + pe[:, None, :]
        blk = blk.transpose(0, 3, 1, 2, 4).reshape(B, G, n_cmp, CMP_LEN * Dh)
        return jax.nn.silu(blk @ w1) @ w2

    k_cmp = compress(kc, ck_w1, ck_w2, ck_pe)
    v_cmp = compress(vc, cv_w1, cv_w2, cv_pe)
    s_cmp = jnp.einsum('bsgrd,bgnd->bgrsn', q, k_cmp) * scale
    cmp_mask = jnp.asarray(blk_tok[:, -1])[None, :] <= pos[:, None]
    p_cmp = masked_softmax(s_cmp, cmp_mask)
    o_cmp = jnp.einsum('bgrsn,bgnd->bsgrd', p_cmp, v_cmp.astype(f32))
    n_sel_blk = S // SEL_LEN
    k_top = min(N_SEL, n_sel_blk)
    imp = jnp.einsum('bgrsn,nj->bgsj', p_cmp, cmp_sel_overlap(S))
    blk = jnp.arange(n_sel_blk)[None, :]
    cur = (pos // SEL_LEN)[:, None]
    visible = blk <= cur
    forced = (blk == 0) | (visible & (blk > cur - N_LOCAL_FORCED))
    imp = jnp.where(forced, FORCED_SCORE, jnp.where(visible, imp, -1.0))
    sel_idx = lax.top_k(imp, k_top)[1]
    ks_blk = ks.reshape(B, n_sel_blk, SEL_LEN, G, Dh).transpose(0, 3, 1, 2, 4)
    vs_blk = vs.reshape(B, n_sel_blk, SEL_LEN, G, Dh).transpose(0, 3, 1, 2, 4)
    n_qb = S // SEL_QBLK
    q_b = q.reshape(B, n_qb, SEL_QBLK, G, R, Dh).transpose(1, 0, 2, 3, 4, 5)
    idx_b = sel_idx.reshape(B, G, n_qb, SEL_QBLK, k_top).transpose(2, 0, 1, 3, 4)
    qpos_b = pos.reshape(n_qb, SEL_QBLK)
    b_ix = jnp.arange(B)[:, None, None, None]
    g_ix = jnp.arange(G)[None, :, None, None]

    def sel_block(args):
        qb, ib, qp = args
        kb = ks_blk[b_ix, g_ix, ib].reshape(B, G, SEL_QBLK, k_top * SEL_LEN, Dh)
        vb = vs_blk[b_ix, g_ix, ib].reshape(B, G, SEL_QBLK, k_top * SEL_LEN, Dh)
        tok = (ib[..., None] * SEL_LEN + jnp.arange(SEL_LEN)).reshape(B, G, SEL_QBLK, k_top * SEL_LEN)
        s = jnp.einsum('bqgrd,bgqkd->bgrqk', qb, kb) * scale
        mask = tok[:, :, None] <= qp[None, None, None, :, None]
        pr = masked_softmax(s, mask)
        return jnp.einsum('bgrqk,bgqkd->bqgrd', pr, vb.astype(f32))

    o_sel = lax.map(sel_block, (q_b, idx_b, qpos_b))
    o_sel = o_sel.transpose(1, 0, 2, 3, 4, 5).reshape(B, S, G, R, Dh)
    n_wb = S // WIN_QBLK
    span = WINDOW + WIN_QBLK
    key_idx = np.arange(n_wb)[:, None] * WIN_QBLK + np.arange(span)[None, :]
    pad = ((0, 0), (WINDOW, 0), (0, 0), (0, 0))
    kb = jnp.pad(kw, pad)[:, key_idx]
    vb = jnp.pad(vw, pad)[:, key_idx]
    qb = q.reshape(B, n_wb, WIN_QBLK, G, R, Dh)
    s_win = jnp.einsum('bnqgrd,bnkgd->bgrnqk', qb, kb) * scale
    kpos = jnp.asarray(key_idx - WINDOW)[:, None, :]
    qpos = pos.reshape(n_wb, WIN_QBLK)[:, :, None]
    win_mask = (kpos >= 0) & (kpos <= qpos) & (kpos > qpos - WINDOW)
    p_win = masked_softmax(s_win, win_mask)
    o_win = jnp.einsum('bgrnqk,bnkgd->bnqgrd', p_win, vb.astype(f32)).reshape(B, S, G, R, Dh)
    gate = jax.nn.sigmoid(gate_logit.astype(f32)).reshape(B, S, G, R, 3)
    o = gate[..., 0:1] * o_cmp + gate[..., 1:2] * o_sel + gate[..., 2:3] * o_win
    return o.reshape(B, S, D_GRP)


def retention_mixer(q, k, v, g):
    B, S, _ = q.shape
    H, Dh, C = RET_HEADS, HEAD_DIM, RET_CHUNK
    n_c = S // C
    freqs = RET_THETA ** (-jnp.linspace(0.0, 1.0, Dh // 2, dtype=f32))
    heads = lambda t: t.reshape(B, S, H, Dh)
    q = rotary(heads(q), freqs, Dh).astype(f32)
    k = rotary(heads(k), freqs, Dh).astype(f32) * Dh ** -0.5
    v = heads(v).astype(f32)
    log_gamma = jnp.log1p(-(2.0 ** (-5.0 - jnp.arange(H, dtype=f32))))
    n = jnp.arange(C, dtype=f32)
    diff = n[:, None] - n[None, :]
    decay_in = jnp.where(diff >= 0, jnp.exp(log_gamma[:, None, None] * jnp.maximum(diff, 0.0)), 0.0)
    xi = jnp.exp(log_gamma[:, None] * (n + 1.0))
    zeta = jnp.exp(log_gamma[:, None] * (C - 1.0 - n))
    gamma_c = jnp.exp(log_gamma * C)
    chunks = lambda t: t.reshape(B, n_c, C, H, Dh).transpose(1, 0, 3, 2, 4)

    def step(state, inp):
        qc, kc, vc = inp
        inner = jnp.einsum('bhnd,bhmd->bhnm', qc, kc) * decay_in
        o = jnp.einsum('bhnm,bhme->bhne', inner, vc) + jnp.einsum('bhnd,bhde->bhne', qc, state) * xi[..., None]
        state = state * gamma_c[:, None, None] + jnp.einsum('bhmd,bhme->bhde', kc * zeta[..., None], vc)
        return state, o

    _, o = lax.scan(step, jnp.zeros((B, H, Dh, Dh), f32), (chunks(q), chunks(k), chunks(v)))
    o = head_layer_norm(o.transpose(1, 0, 3, 2, 4).reshape(B, S, H, Dh), NORM_EPS)
    return jax.nn.silu(g.astype(f32)) * o.reshape(B, S, D_GRP)


def deltanet_mixer(q, k, v, a, b, z, conv_w, a_log, dt_bias, norm_g):
    B, S, _ = q.shape
    H, Dh, C = DN_HEADS, HEAD_DIM, DN_CHUNK
    n_c = S // C
    qkv = jnp.concatenate([q, k, v], -1)
    qkv = lax.conv_general_dilated(qkv, conv_w[:, None, :], window_strides=(1,), padding=[(CONV_WIDTH - 1, 0)],
                                   dimension_numbers=('NWC', 'WIO', 'NWC'), feature_group_count=3 * D_GRP)
    qkv = jax.nn.silu(qkv.astype(f32))
    q, k, v = [t.reshape(B, S, H, Dh) for t in jnp.split(qkv, 3, -1)]
    q = l2norm(q) * Dh ** -0.5
    k = l2norm(k)
    beta = jax.nn.sigmoid(b.astype(f32))
    g = -jnp.exp(a_log.astype(f32)) * jax.nn.softplus(a.astype(f32) + dt_bias.astype(f32))
    chunks = lambda t: jnp.moveaxis(t.reshape((B, n_c, C) + t.shape[2:]), 3, 1)
    qc, kc, vc, bc = chunks(q), chunks(k), chunks(v), chunks(beta)
    gc = jnp.cumsum(chunks(g), -1)
    kbeta = kc * bc[..., None]
    vbeta = vc * bc[..., None]
    idx = jnp.arange(C)
    tril = idx[:, None] >= idx[None, :]
    strict = idx[:, None] > idx[None, :]
    decay = jnp.exp(jnp.where(tril, gc[..., :, None] - gc[..., None, :], -jnp.inf))
    m = jnp.where(strict, jnp.einsum('bhncd,bhnkd->bhnck', kbeta, kc) * decay, 0.0)
    eye = jnp.eye(C, dtype=f32)
    t_inv = lax.linalg.triangular_solve(eye + m, jnp.broadcast_to(eye, m.shape), left_side=True,
                                        lower=True, unit_diagonal=True)
    u = t_inv @ vbeta
    w = t_inv @ (kbeta * jnp.exp(gc)[..., None])
    attn_in = jnp.where(tril, jnp.einsum('bhncd,bhnkd->bhnck', qc, kc) * decay, 0.0)
    q_dec = qc * jnp.exp(gc)[..., None]
    k_dec = kc * jnp.exp(gc[..., -1:] - gc)[..., None]
    g_last = jnp.exp(gc[..., -1])

    def step(state, inp):
        u_i, w_i, a_i, qd_i, kd_i, gl_i = inp
        v_new = u_i - w_i @ state
        o = qd_i @ state + a_i @ v_new
        state = state * gl_i[..., None, None] + jnp.swapaxes(kd_i, -1, -2) @ v_new
        return state, o

    xs = tuple(jnp.moveaxis(t, 2, 0) for t in (u, w, attn_in, q_dec, k_dec, g_last))
    _, o = lax.scan(step, jnp.zeros((B, H, Dh, Dh), f32), xs)
    o = o.transpose(1, 0, 3, 2, 4).reshape(B, S, H, Dh)
    o = rms_norm(o, norm_g) * jax.nn.silu(z.astype(f32)).reshape(B, S, H, Dh)
    return o.reshape(B, S, D_GRP)


def _split2(x):
    h = x.astype(bf16)
    return h, (x - h.astype(f32)).astype(bf16)


def _split3(x):
    h = x.astype(bf16)
    r1 = x - h.astype(f32)
    m = r1.astype(bf16)
    return h, m, (r1 - m.astype(f32)).astype(bf16)


def _rwkv_prep_body(x_ref, xprev_ref, l_ref, lprev_ref, mu_ref, mul_ref, w0_ref, a0_ref, kkw_ref, ka_ref,
                    w2_ref, a2_ref, g2_ref, e_ref, r_ref, w_ref, k_ref, v_ref, kk_ref, b_ref, g_ref, *, tm, S):
    i = pl.program_id(0)
    first = (i * tm) % S == 0

    def token_mix(x, prev8, mu):
        rows = lax.broadcasted_iota(jnp.int32, x.shape, 0)
        prev_row = jnp.where(first, 0.0, prev8[SUB - 1:SUB, :])
        x_prev = jnp.where(rows == 0, prev_row, pltpu.roll(x, 1, axis=0))
        return x + (x_prev - x) * mu

    x = token_mix(x_ref[...], xprev_ref[...], mu_ref[...])
    lora = token_mix(l_ref[...], lprev_ref[...], mul_ref[...])
    r = x[:, 0:D_GRP]
    k = x[:, D_GRP:2 * D_GRP]
    v = x[:, 2 * D_GRP:3 * D_GRP]
    dw = jnp.dot(jnp.tanh(lora).astype(bf16), w2_ref[...], preferred_element_type=f32)
    da = jnp.dot(lora.astype(bf16), a2_ref[...], preferred_element_type=f32)
    g = jnp.dot(jax.nn.sigmoid(lora).astype(bf16), g2_ref[...], preferred_element_type=f32)
    w = -jax.nn.softplus(-(w0_ref[...] + dw)) - 0.5
    decay = jnp.exp(-jnp.exp(w))
    a = jax.nn.sigmoid(a0_ref[...] + da)
    kq = k * kkw_ref[...]
    sq_h, sq_l = _split2(kq * kq)
    ss = (jnp.dot(sq_h, e_ref[...], preferred_element_type=f32)
          + jnp.dot(sq_l, e_ref[...], preferred_element_type=f32))
    kk = kq * lax.rsqrt(ss + 1e-6)
    r_ref[...] = r
    w_ref[...] = decay
    k_ref[...] = k * (1.0 + (a - 1.0) * ka_ref[...])
    v_ref[...] = v
    kk_ref[...] = kk
    b_ref[...] = kk * a
    g_ref[...] = g


def rwkv_prep(proj, rkv_blk, lora_blk, mu, w0, w2, a0, a2, g2, k_k, k_a, S, tm=256):
    M = proj.shape[0]
    assert M % tm == 0 and S % tm == 0
    W3 = 3 * D_GRP
    mu_x = mu[:W3].reshape(1, W3)
    mu_l = jnp.pad(mu[W3:], (0, RW_LORA_PAD - RW_LORA)).reshape(1, RW_LORA_PAD)
    w2p = jnp.zeros((RW_LORA_PAD, D_GRP), f32).at[0:RW_DECAY_LORA].set(w2).astype(bf16)
    a2p = jnp.zeros((RW_LORA_PAD, D_GRP), f32).at[RW_DECAY_LORA:RW_DECAY_LORA + RW_AAA_LORA].set(a2).astype(bf16)
    g2p = jnp.zeros((RW_LORA_PAD, D_GRP), f32).at[RW_DECAY_LORA + RW_AAA_LORA:RW_LORA].set(g2).astype(bf16)
    hid = np.arange(D_GRP) // RW_HEAD_DIM
    e = jnp.asarray(hid[:, None] == hid[None, :], bf16)
    row = lambda t: t.reshape(1, D_GRP)
    vec = pl.BlockSpec((1, D_GRP), lambda i: (0, 0))
    wspec = pl.BlockSpec((RW_LORA_PAD, D_GRP), lambda i: (0, 0))
    ospec = pl.BlockSpec((tm, D_GRP), lambda i: (i, 0))
    prev_row_blk = lambda i: jnp.maximum(i * (tm // SUB) - 1, 0)
    return pl.pallas_call(
        functools.partial(_rwkv_prep_body, tm=tm, S=S),
        grid=(M // tm,),
        in_specs=[pl.BlockSpec((tm, W3), lambda i: (i, rkv_blk)),
                  pl.BlockSpec((SUB, W3), lambda i: (prev_row_blk(i), rkv_blk)),
                  pl.BlockSpec((tm, RW_LORA_PAD), lambda i: (i, lora_blk)),
                  pl.BlockSpec((SUB, RW_LORA_PAD), lambda i: (prev_row_blk(i), lora_blk)),
                  pl.BlockSpec((1, W3), lambda i: (0, 0)),
                  pl.BlockSpec((1, RW_LORA_PAD), lambda i: (0, 0)),
                  vec, vec, vec, vec, wspec, wspec, wspec,
                  pl.BlockSpec((D_GRP, D_GRP), lambda i: (0, 0))],
        out_specs=[ospec] * 7,
        out_shape=[jax.ShapeDtypeStruct((M, D_GRP), f32)] * 7,
        compiler_params=pltpu.CompilerParams(dimension_semantics=("parallel",), vmem_limit_bytes=VMEM_LIMIT),
        name="rwkv_prep",
    )(proj, proj, proj, proj, mu_x, mu_l, row(w0), row(a0), row(k_k), row(k_a), w2p, a2p, g2p, e)


def _rwkv_scan_body(r_ref, w_ref, k_ref, v_ref, kk_ref, b_ref, g_ref, lng_ref, lnb_ref, rk_ref, o_ref, s_ref,
                    vc_ref, y_ref, *, G, T):
    tb = pl.program_id(2)

    @pl.when(tb == 0)
    def _():
        s_ref[...] = jnp.zeros_like(s_ref)

    Dh = RW_HEAD_DIM
    lo = lax.broadcasted_iota(jnp.int32, (Dh, LANES), 1) < Dh
    lo8 = lax.broadcasted_iota(jnp.int32, (SUB, LANES), 1) < Dh
    sub_iota = lax.broadcasted_iota(jnp.int32, (4 * SUB, LANES), 0)
    row_iota = lax.broadcasted_iota(jnp.int32, (SUB, LANES), 0)

    def seg_sum(x, m):
        s_lo = jnp.sum(jnp.where(m, x, 0.0), axis=-1, keepdims=True)
        s_hi = jnp.sum(jnp.where(m, 0.0, x), axis=-1, keepdims=True)
        return jnp.where(m, s_lo, s_hi)

    cols = [slice(p * LANES, (p + 1) * LANES) for p in range(G)]
    n_groups = T // SUB

    k_iota = lax.broadcasted_iota(jnp.int32, (8 * SUB, LANES), 0)
    k_head = k_iota // (4 * SUB) == lax.broadcasted_iota(jnp.int32, (8 * SUB, LANES), 1) // Dh
    k_piece = (k_iota // SUB) % 4

    def v_pieces(gi):
        t0 = pl.multiple_of(gi * SUB, SUB)
        vts = []
        for p in range(G):
            vh, vm, vl = _split3(v_ref[0, pl.ds(t0, SUB), cols[p]])
            vp = jnp.concatenate([vh, vm, vl, jnp.zeros_like(vh)], axis=0)
            vts.append(jnp.concatenate([vp[:, :Dh], vp[:, Dh:]], axis=0).T)
        return vts

    def v_columns(vts, i, slot):
        sel = (k_head & (k_piece < 3) & (k_iota % SUB == i)).astype(bf16)
        for p in range(G):
            vc_ref[slot, i, p] = jnp.dot(vts[p], sel, preferred_element_type=f32)

    vts0 = v_pieces(0)
    for i in range(SUB):
        v_columns(vts0, i, 0)

    def run_group(gi, slot, states):
        t0 = pl.multiple_of(gi * SUB, SUB)
        vts_next = v_pieces(jnp.minimum(gi + 1, n_groups - 1))
        r8b = [r_ref[0, pl.ds(t0, SUB), cols[p]].astype(bf16) for p in range(G)]
        yacc = [jnp.zeros((SUB, LANES), f32) for _ in range(G)]
        tiles = [[ref[0, pl.ds(t0, SUB), cols[p]] for ref in (kk_ref, w_ref, b_ref, k_ref)] for p in range(G)]
        for i in range(SUB):
            for p in range(G):
                kk_r, w_r, b_r, k_r = [jnp.broadcast_to(t8[i:i + 1, :], (Dh, LANES)) for t8 in tiles[p]]
                S = states[p]
                sab = -seg_sum(S * kk_r, lo)
                states[p] = S * w_r + sab * b_r + vc_ref[slot, i, p] * k_r
            for p in range(G):
                Sb = states[p].astype(bf16)
                zero = jnp.zeros_like(Sb)
                mt = jnp.concatenate([jnp.where(lo, Sb, zero), jnp.where(lo, zero, Sb)], axis=0)
                y8 = lax.dot_general(r8b[p], mt, (((1,), (1,)), ((), ())), preferred_element_type=f32)
                yacc[p] = jnp.where(row_iota == i, y8, yacc[p])
            v_columns(vts_next, i, 1 - slot)
        for p in range(G):
            y_ref[pl.ds(t0, SUB), cols[p]] = yacc[p]
        return states

    def group_batch(gj, carry):
        states = [s_ref[p] for p in range(G)]
        for u in range(RW_GROUPS_PER_BODY):
            states = run_group(RW_GROUPS_PER_BODY * gj + u, u % 2, states)
        for p in range(G):
            s_ref[p] = states[p]
        return carry

    assert n_groups % RW_GROUPS_PER_BODY == 0 and RW_GROUPS_PER_BODY % 2 == 0
    lax.fori_loop(0, n_groups // RW_GROUPS_PER_BODY, group_batch, 0)

    lo_t = lax.broadcasted_iota(jnp.int32, (T, LANES), 1) < Dh
    for p in range(G):
        y = y_ref[:, cols[p]]
        mean = seg_sum(y, lo_t) * (1.0 / Dh)
        d = y - mean
        var = seg_sum(d * d, lo_t) * (1.0 / Dh)
        yn = d * lax.rsqrt(var + RW_LN_EPS) * lng_ref[:, cols[p]] + lnb_ref[:, cols[p]]
        bonus = seg_sum(r_ref[0, :, cols[p]] * k_ref[0, :, cols[p]] * rk_ref[:, cols[p]], lo_t) * v_ref[0, :, cols[p]]
        o_ref[0, :, cols[p]] = (yn + bonus) * g_ref[0, :, cols[p]]


def rwkv_scan(r, w, k, v, kk, b, g, ln_g, ln_b, r_k, G=8, T=256):
    B, S, D = r.shape
    n_pairs = D // LANES
    assert n_pairs % G == 0 and S % T == 0
    spec = pl.BlockSpec((1, T, G * LANES), lambda bi, pi, ti: (bi, ti, pi))
    vec = pl.BlockSpec((1, G * LANES), lambda bi, pi, ti: (0, pi))
    row = lambda t: t.reshape(1, D)
    return pl.pallas_call(
        functools.partial(_rwkv_scan_body, G=G, T=T),
        grid=(B, n_pairs // G, S // T),
        in_specs=[spec] * 7 + [vec] * 3,
        out_specs=spec,
        out_shape=jax.ShapeDtypeStruct((B, S, D), f32),
        scratch_shapes=[pltpu.VMEM((G, RW_HEAD_DIM, LANES), f32),
                        pltpu.VMEM((2, SUB, G, RW_HEAD_DIM, LANES), f32),
                        pltpu.VMEM((T, G * LANES), f32)],
        compiler_params=pltpu.CompilerParams(dimension_semantics=("parallel", "parallel", "arbitrary"),
                                             vmem_limit_bytes=VMEM_LIMIT),
        name="rwkv_scan",
    )(r, w, k, v, kk, b, g, row(ln_g), row(ln_b), row(r_k))


def rwkv7_mixer(proj, rkv_blk, lora_blk, mu, w0, w2, a0, a2, g2, k_k, k_a, r_k, ln_g, ln_b, B, S):
    outs = rwkv_prep(proj, rkv_blk, lora_blk, mu, w0, w2, a0, a2, g2, k_k, k_a, S)
    r, w, k, v, kk, b, g = [t.reshape(B, S, D_GRP) for t in outs]
    return rwkv_scan(r, w, k, v, kk, b, g, ln_g, ln_b, r_k.reshape(-1))


def _mm_body(a_ref, b_ref, o_ref, acc_ref):
    k = pl.program_id(2)

    @pl.when(k == 0)
    def _():
        acc_ref[...] = jnp.zeros_like(acc_ref)

    acc_ref[...] += jnp.dot(a_ref[...].astype(bf16), b_ref[...].astype(bf16), preferred_element_type=f32)

    @pl.when(k == pl.num_programs(2) - 1)
    def _():
        o_ref[...] = acc_ref[...].astype(o_ref.dtype)


def matmul(a, b, out_dtype=f32, tm=1024, tn=1024, tk=2048, name="matmul"):
    M, K = a.shape
    _, N = b.shape
    tm, tn, tk = min(tm, M), min(tn, N), min(tk, K)
    assert M % tm == 0 and N % tn == 0 and K % tk == 0, (M, N, K, tm, tn, tk)
    return pl.pallas_call(
        _mm_body,
        grid=(M // tm, N // tn, K // tk),
        in_specs=[pl.BlockSpec((tm, tk), lambda i, j, k: (i, k)),
                  pl.BlockSpec((tk, tn), lambda i, j, k: (k, j))],
        out_specs=pl.BlockSpec((tm, tn), lambda i, j, k: (i, j)),
        out_shape=jax.ShapeDtypeStruct((M, N), out_dtype),
        scratch_shapes=[pltpu.VMEM((tm, tn), f32)],
        compiler_params=pltpu.CompilerParams(
            dimension_semantics=("parallel", "parallel", "arbitrary"), vmem_limit_bytes=VMEM_LIMIT),
        name=name,
    )(a, b)


def _rms_norm_body(x_ref, g_ref, o_ref):
    x = x_ref[...]
    y = x * lax.rsqrt(jnp.mean(x * x, axis=-1, keepdims=True) + NORM_EPS)
    o_ref[...] = (y * g_ref[...]).astype(o_ref.dtype)


def rms_norm_rows(x, g, out_dtype, tm=256):
    M, D = x.shape
    return pl.pallas_call(
        _rms_norm_body,
        grid=(M // tm,),
        in_specs=[pl.BlockSpec((tm, D), lambda i: (i, 0)), pl.BlockSpec((1, D), lambda i: (0, 0))],
        out_specs=pl.BlockSpec((tm, D), lambda i: (i, 0)),
        out_shape=jax.ShapeDtypeStruct((M, D), out_dtype),
        compiler_params=pltpu.CompilerParams(dimension_semantics=("parallel",), vmem_limit_bytes=VMEM_LIMIT),
        name="rms_norm",
    )(x, g.reshape(1, D).astype(f32))


def _out_proj_body(a0_ref, a1_ref, a2_ref, a3_ref, b_ref, h_ref, o_ref, acc_ref):
    k = pl.program_id(2)

    @pl.when(k == 0)
    def _():
        acc_ref[...] = h_ref[...]

    for j, a_ref in enumerate((a0_ref, a1_ref, a2_ref, a3_ref)):
        @pl.when(k == j)
        def _(a_ref=a_ref):
            acc_ref[...] += jnp.dot(a_ref[...].astype(bf16), b_ref[...], preferred_element_type=f32)

    @pl.when(k == pl.num_programs(2) - 1)
    def _():
        o_ref[...] = acc_ref[...]


def out_proj_residual(parts, w, h, tm=1024, tn=1024):
    M, D = h.shape
    aspec = pl.BlockSpec((tm, D_GRP), lambda i, j, k: (i, 0))
    return pl.pallas_call(
        _out_proj_body,
        grid=(M // tm, D // tn, len(parts)),
        in_specs=[aspec] * 4 + [pl.BlockSpec((D_GRP, tn), lambda i, j, k: (k, j)),
                                pl.BlockSpec((tm, tn), lambda i, j, k: (i, j))],
        out_specs=pl.BlockSpec((tm, tn), lambda i, j, k: (i, j)),
        out_shape=jax.ShapeDtypeStruct((M, D), f32),
        scratch_shapes=[pltpu.VMEM((tm, tn), f32)],
        compiler_params=pltpu.CompilerParams(
            dimension_semantics=("parallel", "parallel", "arbitrary"), vmem_limit_bytes=VMEM_LIMIT),
        name="out_proj",
    )(*parts, w, h)


def _ple_body(a_ref, b_ref, p_ref, pw_ref, h_ref, o_ref, acc_ref):
    k = pl.program_id(2)

    @pl.when(k == 0)
    def _():
        acc_ref[...] = jnp.zeros_like(acc_ref)

    acc_ref[...] += jnp.dot(a_ref[...], b_ref[...], preferred_element_type=f32)

    @pl.when(k == pl.num_programs(2) - 1)
    def _():
        emb = jnp.dot(p_ref[...].astype(bf16), pw_ref[...], preferred_element_type=f32)
        o_ref[...] = h_ref[...] + emb * jax.nn.sigmoid(acc_ref[...])


def ple_residual(hn, gate_w, p, ple_w, h, tm=1024, tn=1024, tk=2048):
    M, D = h.shape
    return pl.pallas_call(
        _ple_body,
        grid=(M // tm, D // tn, D // tk),
        in_specs=[pl.BlockSpec((tm, tk), lambda i, j, k: (i, k)),
                  pl.BlockSpec((tk, tn), lambda i, j, k: (k, j)),
                  pl.BlockSpec((tm, PLE_DIM), lambda i, j, k: (i, 0)),
                  pl.BlockSpec((PLE_DIM, tn), lambda i, j, k: (0, j)),
                  pl.BlockSpec((tm, tn), lambda i, j, k: (i, j))],
        out_specs=pl.BlockSpec((tm, tn), lambda i, j, k: (i, j)),
        out_shape=jax.ShapeDtypeStruct((M, D), f32),
        scratch_shapes=[pltpu.VMEM((tm, tn), f32)],
        compiler_params=pltpu.CompilerParams(
            dimension_semantics=("parallel", "parallel", "arbitrary"), vmem_limit_bytes=VMEM_LIMIT),
        name="ple",
    )(hn, gate_w, p, ple_w, h)


def _moe_router_body(x_ref, w_ref, b_ref, id_ref, wt_ref):
    logits = jnp.dot(x_ref[...], w_ref[...], preferred_element_type=f32) + b_ref[...]
    lane = lax.broadcasted_iota(jnp.int32, logits.shape, 1).astype(f32)
    ninf = -jnp.inf

    def top1(vals):
        m = jnp.max(vals, axis=-1, keepdims=True)
        idx = jnp.min(jnp.where(vals == m, lane, float(LANES)), axis=-1, keepdims=True)
        return m, idx

    gl = jnp.where(lane < N_GROUPS, logits, ninf)
    g_max, g_idx = top1(gl)
    g_w = 1.0 / jnp.sum(jnp.exp(gl - g_max), axis=-1, keepdims=True)
    lo = N_GROUPS + g_idx * EXPERTS_PER_GROUP
    el = jnp.where((lane >= lo) & (lane < lo + EXPERTS_PER_GROUP), logits, ninf)
    e_max, _ = top1(el)
    pe = jnp.exp(el - e_max)
    p = pe / jnp.sum(pe, axis=-1, keepdims=True)
    p = jnp.where(el == ninf, ninf, p)
    p1, i1 = top1(p)
    p2, i2 = top1(jnp.where(lane == i1, ninf, p))
    scale = g_w / (p1 + p2)
    id_ref[...] = jnp.where(lane == 0, i1 - N_GROUPS, jnp.where(lane == 1, i2 - N_GROUPS, 0.0)).astype(jnp.int32)
    wt_ref[...] = jnp.where(lane == 0, p1 * scale, jnp.where(lane == 1, p2 * scale, 0.0))


def moe_router(xb, router_grp, router_grp_b, router_exp, router_exp_b, tm=512):
    M, D = xb.shape
    n_log = N_GROUPS + N_EXPERTS
    w = jnp.pad(jnp.concatenate([router_grp, router_exp], axis=1), ((0, 0), (0, LANES - n_log))).astype(bf16)
    b = jnp.pad(jnp.concatenate([router_grp_b, router_exp_b]), (0, LANES - n_log)).reshape(1, LANES).astype(f32)
    ids, wts = pl.pallas_call(
        _moe_router_body,
        grid=(M // tm,),
        in_specs=[pl.BlockSpec((tm, D), lambda i: (i, 0)), pl.BlockSpec((D, LANES), lambda i: (0, 0)),
                  pl.BlockSpec((1, LANES), lambda i: (0, 0))],
        out_specs=[pl.BlockSpec((tm, LANES), lambda i: (i, 0))] * 2,
        out_shape=[jax.ShapeDtypeStruct((M, LANES), jnp.int32), jax.ShapeDtypeStruct((M, LANES), f32)],
        compiler_params=pltpu.CompilerParams(dimension_semantics=("parallel",), vmem_limit_bytes=VMEM_LIMIT),
        name="moe_router",
    )(xb, w, b)
    return ids[:, :TOP_K], wts[:, :TOP_K]


def _moe_up_body(be_ref, nu_ref, x_ref, wg_ref, wu_ref, h_ref):
    i = pl.program_id(1)

    @pl.when(i < nu_ref[0])
    def _():
        x = x_ref[...]
        hg = jnp.dot(x, wg_ref[0].astype(bf16), preferred_element_type=f32)
        hu = jnp.dot(x, wu_ref[0].astype(bf16), preferred_element_type=f32)
        h_ref[...] = (hg * jax.nn.sigmoid(hg) * hu).astype(bf16)

    @pl.when(i >= nu_ref[0])
    def _():
        h_ref[...] = jnp.zeros_like(h_ref)


def _moe_down_body(be_ref, nu_ref, h_ref, wd_ref, wt_ref, y_ref):
    i = pl.program_id(0)

    @pl.when(i < nu_ref[0])
    def _():
        y_ref[...] = jnp.dot(h_ref[...], wd_ref[0].astype(bf16), preferred_element_type=f32) * wt_ref[...]

    @pl.when(i >= nu_ref[0])
    def _():
        y_ref[...] = jnp.zeros_like(y_ref)


def moe_experts(x_pad, w_pad, blk_expert, n_used, wg, wu, wd, tf=256):
    A_pad, D = x_pad.shape
    n_blk = A_pad // ROW_BLOCK
    nf = D_EXPERT // tf
    h = pl.pallas_call(
        _moe_up_body,
        grid_spec=pltpu.PrefetchScalarGridSpec(
            num_scalar_prefetch=2,
            grid=(nf, n_blk),
            in_specs=[pl.BlockSpec((ROW_BLOCK, D), lambda f, i, be, nu: (i, 0)),
                      pl.BlockSpec((1, D, tf), lambda f, i, be, nu: (be[i], 0, f)),
                      pl.BlockSpec((1, D, tf), lambda f, i, be, nu: (be[i], 0, f))],
            out_specs=pl.BlockSpec((ROW_BLOCK, tf), lambda f, i, be, nu: (i, f)),
        ),
        out_shape=jax.ShapeDtypeStruct((A_pad, D_EXPERT), bf16),
        compiler_params=pltpu.CompilerParams(
            dimension_semantics=("arbitrary", "arbitrary"), vmem_limit_bytes=VMEM_LIMIT),
        name="moe_up",
    )(blk_expert, n_used, x_pad, wg, wu)
    return pl.pallas_call(
        _moe_down_body,
        grid_spec=pltpu.PrefetchScalarGridSpec(
            num_scalar_prefetch=2,
            grid=(n_blk,),
            in_specs=[pl.BlockSpec((ROW_BLOCK, D_EXPERT), lambda i, be, nu: (i, 0)),
                      pl.BlockSpec((1, D_EXPERT, D), lambda i, be, nu: (be[i], 0, 0)),
                      pl.BlockSpec((ROW_BLOCK, 1), lambda i, be, nu: (i, 0))],
            out_specs=pl.BlockSpec((ROW_BLOCK, D), lambda i, be, nu: (i, 0)),
        ),
        out_shape=jax.ShapeDtypeStruct((A_pad, D), f32),
        compiler_params=pltpu.CompilerParams(dimension_semantics=("arbitrary",), vmem_limit_bytes=VMEM_LIMIT),
        name="moe_down",
    )(blk_expert, n_used, h, wd, w_pad.reshape(A_pad, 1))


def hier_moe(xb, router_grp, router_grp_b, router_exp, router_exp_b, w_gate, w_up, w_down):
    M, D = xb.shape
    expert_ids, weights = moe_router(xb, router_grp, router_grp_b, router_exp, router_exp_b)
    A = M * TOP_K
    e_flat = expert_ids.reshape(A)
    onehot = (e_flat[:, None] == jnp.arange(N_EXPERTS)[None, :]).astype(jnp.int32)
    rank = jnp.take_along_axis(jnp.cumsum(onehot, axis=0), e_flat[:, None], axis=1)[:, 0] - 1
    counts = jnp.sum(onehot, axis=0)
    padded = (counts + ROW_BLOCK - 1) // ROW_BLOCK * ROW_BLOCK
    end_pad = jnp.cumsum(padded)
    dest = (end_pad - padded)[e_flat] + rank
    n_blk = -(-A // ROW_BLOCK) + N_EXPERTS
    A_pad = n_blk * ROW_BLOCK
    tok_pad = jnp.zeros((A_pad,), jnp.int32).at[dest].set(jnp.arange(A, dtype=jnp.int32) // TOP_K)
    w_pad = jnp.zeros((A_pad,), f32).at[dest].set(weights.reshape(A))
    blk_expert = jnp.minimum(jnp.searchsorted(end_pad, jnp.arange(n_blk) * ROW_BLOCK, side='right'),
                             N_EXPERTS - 1).astype(jnp.int32)
    n_used = (end_pad[-1:] // ROW_BLOCK).astype(jnp.int32)
    x_pad = xb[tok_pad]
    y_pad = moe_experts(x_pad, w_pad, blk_expert, n_used, w_gate, w_up, w_down)
    rows = dest.reshape(M, TOP_K)
    return y_pad[rows[:, 0]] + y_pad[rows[:, 1]]


def _rope_tables(S, theta, rot_dims, width):
    half = rot_dims // 2
    freqs = theta ** (-np.arange(half, dtype=np.float64) / half)
    ang = np.arange(S, dtype=np.float64)[:, None] * freqs[None, :]
    cos = np.ones((S, width), np.float64)
    sin_lo = np.zeros((S, width), np.float64)
    sin_hi = np.zeros((S, width), np.float64)
    cos[:, :half] = np.cos(ang)
    cos[:, half:rot_dims] = np.cos(ang)
    sin_lo[:, :half] = -np.sin(ang)
    sin_hi[:, half:rot_dims] = np.sin(ang)
    return [jnp.asarray(t, f32) for t in (cos, sin_lo, sin_hi)]


def _rope128(x, cos, sin_lo, sin_hi, half):
    return x * cos + pltpu.roll(x, LANES - half, axis=1) * sin_lo + pltpu.roll(x, half, axis=1) * sin_hi


def _nsa_prep_body(q_ref, kv_ref, cos_ref, sl_ref, sh_ref, qo_ref, kvo_ref, co_ref):
    cos, sl, sh = cos_ref[...], sl_ref[...], sh_ref[...]
    half = ROPE_DIMS // 2
    scale = HEAD_DIM ** -0.5
    for h in range(NSA_HEADS):
        c = slice(h * LANES, (h + 1) * LANES)
        qo_ref[:, c] = (_rope128(q_ref[:, c], cos, sl, sh, half) * scale).astype(bf16)
    for j in range(6):
        for g in range(NSA_KV_HEADS):
            c = slice((j * NSA_KV_HEADS + g) * LANES, (j * NSA_KV_HEADS + g + 1) * LANES)
            x = kv_ref[:, c]
            if j % 2 == 0:
                x = _rope128(x, cos, sl, sh, half)
            if j < 2:
                co_ref[:, c] = x.astype(bf16)
            else:
                kvo_ref[:, slice(c.start - 4 * LANES, c.stop - 4 * LANES)] = x.astype(bf16)


def nsa_prep(proj, S, tm=512):
    M = proj.shape[0]
    assert M % tm == 0 and S % tm == 0
    tabs = _rope_tables(S, ROPE_THETA, ROPE_DIMS, LANES)
    tspec = pl.BlockSpec((tm, LANES), lambda i: (i % (S // tm), 0))
    return pl.pallas_call(
        _nsa_prep_body,
        grid=(M // tm,),
        in_specs=[pl.BlockSpec((tm, D_GRP), lambda i: (i, OFF_NSA_Q // D_GRP)),
                  pl.BlockSpec((tm, 6 * NSA_KV_W), lambda i: (i, OFF_NSA_KV // (6 * NSA_KV_W))),
                  tspec, tspec, tspec],
        out_specs=[pl.BlockSpec((tm, D_GRP), lambda i: (i, 0)),
                   pl.BlockSpec((tm, 4 * NSA_KV_W), lambda i: (i, 0)),
                   pl.BlockSpec((tm, 2 * NSA_KV_W), lambda i: (i, 0))],
        out_shape=[jax.ShapeDtypeStruct((M, D_GRP), bf16), jax.ShapeDtypeStruct((M, 4 * NSA_KV_W), bf16),
                   jax.ShapeDtypeStruct((M, 2 * NSA_KV_W), bf16)],
        compiler_params=pltpu.CompilerParams(dimension_semantics=("parallel",), vmem_limit_bytes=VMEM_LIMIT),
        name="nsa_prep",
    )(proj, proj, *tabs)


def _nsa_compress_body(r_ref, w1_ref, w2_ref, pe_ref, o_ref):
    r = r_ref[0, 0, 0]
    half_w = CMP_STRIDE * HEAD_DIM
    a = jnp.dot(r, w1_ref[0, :half_w, :], preferred_element_type=f32)
    b = jnp.dot(r, w1_ref[0, half_w:, :], preferred_element_type=f32)
    n = r.shape[0]
    pe = jnp.dot(jnp.broadcast_to(pe_ref[0], (SUB, pe_ref.shape[-1])), w1_ref[0], preferred_element_type=f32)[0:1]
    pre = a + pltpu.roll(b, n - 1, axis=0) + pe
    hid = pre * jax.nn.sigmoid(pre)
    o_ref[0, 0, 0] = jnp.dot(hid.astype(bf16), w2_ref[0], preferred_element_type=f32).astype(bf16)


def nsa_compress(c_rot, w1s, w2s, pes, B, S):
    G, Dh = NSA_KV_HEADS, HEAD_DIM
    nc = S // CMP_STRIDE
    r = c_rot.reshape(B, nc, CMP_STRIDE, 2, G, Dh).transpose(3, 0, 4, 1, 2, 5).reshape(2, B, G, nc, CMP_STRIDE * Dh)
    w1 = jnp.stack(w1s).astype(bf16)
    w2 = jnp.stack(w2s).astype(bf16)
    pe = jnp.stack(pes).reshape(2, 1, CMP_LEN * Dh).astype(bf16)
    return pl.pallas_call(
        _nsa_compress_body,
        grid=(2, B, G),
        in_specs=[pl.BlockSpec((1, 1, 1, nc, CMP_STRIDE * Dh), lambda j, b, g: (j, b, g, 0, 0)),
                  pl.BlockSpec((1, CMP_LEN * Dh, Dh), lambda j, b, g: (j, 0, 0)),
                  pl.BlockSpec((1, Dh, Dh), lambda j, b, g: (j, 0, 0)),
                  pl.BlockSpec((1, 1, CMP_LEN * Dh), lambda j, b, g: (j, 0, 0))],
        out_specs=pl.BlockSpec((1, 1, 1, nc, Dh), lambda j, b, g: (j, b, g, 0, 0)),
        out_shape=jax.ShapeDtypeStruct((2, B, G, nc, Dh), bf16),
        compiler_params=pltpu.CompilerParams(dimension_semantics=("parallel", "parallel", "parallel"),
                                             vmem_limit_bytes=VMEM_LIMIT),
        name="nsa_compress",
    )(r, w1, w2, pe)


def _softmax_rows(s, mask):
    s = jnp.where(mask, s, -jnp.inf)
    m = jnp.max(s, axis=-1, keepdims=True)
    m = jnp.where(m == -jnp.inf, 0.0, m)
    e = jnp.exp(s - m)
    return e, jnp.maximum(jnp.sum(e, axis=-1, keepdims=True), jnp.finfo(f32).tiny)


def _nsa_attn_body(q_ref, cmp_ref, ks_ref, vs_ref, kw_ref, vw_ref, gate_ref, ov_ref, o_ref, *, tq, tk, S):
    g = pl.program_id(1)
    qi = pl.program_id(2)
    R, Dh = NSA_GROUP, HEAD_DIM
    rows = R * tq
    n_blk = S // SEL_LEN
    n_cmp = S // CMP_STRIDE - 1
    q0 = qi * tq
    q4 = jnp.concatenate([q_ref[0, :, r * Dh:(r + 1) * Dh] for r in range(R)], axis=0)
    nt = (((1,), (1,)), ((), ()))

    def qpos(shape):
        return q0 + lax.broadcasted_iota(jnp.int32, shape, 0) % tq

    k_cmp, v_cmp = cmp_ref[0, 0, 0], cmp_ref[1, 0, 0]
    ncp = k_cmp.shape[0]
    s = lax.dot_general(q4, k_cmp, nt, preferred_element_type=f32)
    n_idx = lax.broadcasted_iota(jnp.int32, (rows, ncp), 1)
    mask = (n_idx * CMP_STRIDE + (CMP_LEN - 1) <= qpos((rows, ncp))) & (n_idx < n_cmp)
    e, den = _softmax_rows(s, mask)
    p = e / den
    o_cmp = jnp.dot(p.astype(bf16), v_cmp, preferred_element_type=f32)

    psum = p[0:tq]
    for r in range(1, R):
        psum = psum + p[r * tq:(r + 1) * tq]
    ph, plo = _split2(psum)
    imp = (jnp.dot(ph, ov_ref[...], preferred_element_type=f32)
           + jnp.dot(plo, ov_ref[...], preferred_element_type=f32))
    blk = lax.broadcasted_iota(jnp.int32, (tq, LANES), 1)
    cur = (q0 + lax.broadcasted_iota(jnp.int32, (tq, LANES), 0)) // SEL_LEN
    visible = blk <= cur
    forced = (blk == 0) | (visible & (blk > cur - N_LOCAL_FORCED))
    imp = jnp.where(forced, FORCED_SCORE, jnp.where(visible, imp, -1.0))
    imp = jnp.where(blk < n_blk, imp, -jnp.inf)
    sel = jnp.zeros((tq, LANES), f32)
    blk_f = blk.astype(f32)
    for _ in range(min(N_SEL, n_blk)):
        m = jnp.max(imp, axis=-1, keepdims=True)
        first = jnp.min(jnp.where(imp == m, blk_f, float(LANES)), axis=-1, keepdims=True)
        hit = blk_f == first
        sel = jnp.where(hit, 1.0, sel)
        imp = jnp.where(hit, -jnp.inf, imp)
    sel4 = jnp.concatenate([sel.astype(bf16)] * R, axis=0)

    blocks_per_tile = tk // SEL_LEN

    def sel_step(kt, carry):
        m_run, l_run, acc = carry
        k0 = pl.multiple_of(kt * tk, tk)
        kt_k = ks_ref[0, pl.ds(k0, tk), :]
        kt_v = vs_ref[0, pl.ds(k0, tk), :]
        s = lax.dot_general(q4, kt_k, nt, preferred_element_type=f32)
        expand = (lax.broadcasted_iota(jnp.int32, (LANES, tk), 0)
                  == kt * blocks_per_tile + lax.broadcasted_iota(jnp.int32, (LANES, tk), 1) // SEL_LEN)
        chosen = jnp.dot(sel4, expand.astype(bf16), preferred_element_type=f32)
        kpos = k0 + lax.broadcasted_iota(jnp.int32, (rows, tk), 1)
        mask = (chosen > 0.5) & (kpos <= qpos((rows, tk)))
        s = jnp.where(mask, s, -jnp.inf)
        m_new = jnp.maximum(m_run, jnp.max(s, axis=-1, keepdims=True))
        m_safe = jnp.where(m_new == -jnp.inf, 0.0, m_new)
        alpha = jnp.exp(m_run - m_safe)
        e = jnp.exp(s - m_safe)
        l_new = alpha * l_run + jnp.sum(e, axis=-1, keepdims=True)
        acc = alpha * acc + jnp.dot(e.astype(bf16), kt_v, preferred_element_type=f32)
        return m_new, l_new, acc

    n_kt = (q0 + tq - 1) // tk + 1
    init = (jnp.full((rows, 1), -jnp.inf, f32), jnp.zeros((rows, 1), f32), jnp.zeros((rows, Dh), f32))
    _, l_sel, acc_sel = lax.fori_loop(0, n_kt, sel_step, init)
    o_sel = acc_sel / jnp.maximum(l_sel, jnp.finfo(f32).tiny)

    span = WINDOW + tq
    w0 = pl.multiple_of(jnp.maximum(q0 - WINDOW, 0), tq)
    kwin = kw_ref[0, pl.ds(w0, span), :]
    vwin = vw_ref[0, pl.ds(w0, span), :]
    s = lax.dot_general(q4, kwin, nt, preferred_element_type=f32)
    kpos = w0 + lax.broadcasted_iota(jnp.int32, (rows, span), 1)
    qp = qpos((rows, span))
    e, den = _softmax_rows(s, (kpos <= qp) & (kpos > qp - WINDOW))
    o_win = jnp.dot(e.astype(bf16), vwin, preferred_element_type=f32) / den

    gate = jax.nn.sigmoid(gate_ref[0])
    for r in range(R):
        c0 = (g * R + r) * 3
        lane = lax.broadcasted_iota(jnp.int32, (tq, LANES), 1)
        gsel = lambda c: jnp.sum(jnp.where(lane == c0 + c, gate, 0.0), axis=-1, keepdims=True)
        rs = slice(r * tq, (r + 1) * tq)
        o_ref[0, :, r * Dh:(r + 1) * Dh] = (gsel(0) * o_cmp[rs] + gsel(1) * o_sel[rs]
                                            + gsel(2) * o_win[rs]).astype(o_ref.dtype)


def nsa_attention(q_rot, kv_rot, cmp_kv, proj3, B, S, tq=128, tk=512):
    G, R, Dh = NSA_KV_HEADS, NSA_GROUP, HEAD_DIM
    assert S % tk == 0 and tk % tq == 0 and S >= WINDOW + tq and S // SEL_LEN <= LANES
    nc = S // CMP_STRIDE
    c0 = np.arange(nc - 1)[:, None] * CMP_STRIDE
    s0 = np.arange(S // SEL_LEN)[None, :] * SEL_LEN
    ov = np.zeros((nc, LANES), np.float32)
    ov[:nc - 1, :S // SEL_LEN] = np.clip(np.minimum(c0 + CMP_LEN, s0 + SEL_LEN) - np.maximum(c0, s0), 0, None) / CMP_LEN
    kvspec = lambda j: pl.BlockSpec((1, S, Dh), lambda b, g, i: (b, 0, j * G + g))
    return pl.pallas_call(
        functools.partial(_nsa_attn_body, tq=tq, tk=tk, S=S),
        grid=(B, G, S // tq),
        in_specs=[pl.BlockSpec((1, tq, R * Dh), lambda b, g, i: (b, i, g)),
                  pl.BlockSpec((2, 1, 1, nc, Dh), lambda b, g, i: (0, b, g, 0, 0)),
                  kvspec(0), kvspec(1), kvspec(2), kvspec(3),
                  pl.BlockSpec((1, tq, LANES), lambda b, g, i: (b, i, OFF_SMALL // LANES)),
                  pl.BlockSpec((nc, LANES), lambda b, g, i: (0, 0))],
        out_specs=pl.BlockSpec((1, tq, R * Dh), lambda b, g, i: (b, i, g)),
        out_shape=jax.ShapeDtypeStruct((B, S, D_GRP), bf16),
        compiler_params=pltpu.CompilerParams(dimension_semantics=("parallel", "parallel", "arbitrary"),
                                             vmem_limit_bytes=VMEM_LIMIT),
        name="nsa_attention",
    )(q_rot, cmp_kv, kv_rot, kv_rot, kv_rot, kv_rot, proj3, jnp.asarray(ov, bf16))


def nsa_mixer_pallas(proj, ck_w1, ck_w2, ck_pe, cv_w1, cv_w2, cv_pe, B, S):
    q_rot, kv_rot, c_rot = nsa_prep(proj, S)
    cmp_kv = nsa_compress(c_rot, (ck_w1, cv_w1), (ck_w2, cv_w2), (ck_pe, cv_pe), B, S)
    return nsa_attention(q_rot.reshape(B, S, D_GRP), kv_rot.reshape(B, S, 4 * NSA_KV_W), cmp_kv,
                         proj.reshape(B, S, P_PAD), B, S)


def _retention_body(q_ref, k_ref, v_ref, g_ref, cos_ref, sin_ref, dec_ref, xi_ref, zeta_ref, gc_ref, o_ref, st_ref,
                    *, T):
    @pl.when(pl.program_id(2) == 0)
    def _():
        st_ref[...] = jnp.zeros_like(st_ref)

    C, Dh = RET_CHUNK, HEAD_DIM
    nt = (((1,), (1,)), ((), ()))
    decay_in, xi, zeta, gamma_c = dec_ref[0], xi_ref[0], zeta_ref[0], gc_ref[0, 0:1, :]
    state = st_ref[...]
    for c in range(T // C):
        rs = slice(c * C, (c + 1) * C)
        cos, sin = cos_ref[rs, :], sin_ref[rs, :]
        rope = lambda x: x * cos + pltpu.roll(x, Dh // 2, axis=1) * sin
        q = rope(q_ref[0, rs, :])
        k = rope(k_ref[0, rs, :]) * Dh ** -0.5
        qb, vb = q.astype(bf16), v_ref[0, rs, :].astype(bf16)
        inner = lax.dot_general(qb, k.astype(bf16), nt, preferred_element_type=f32) * decay_in
        o = (jnp.dot(inner.astype(bf16), vb, preferred_element_type=f32)
             + jnp.dot(qb, state.astype(bf16), preferred_element_type=f32) * xi)
        kz = (k * zeta).astype(bf16)
        state = state * gamma_c + jnp.dot(kz.T, vb, preferred_element_type=f32)
        mu = jnp.mean(o, axis=-1, keepdims=True)
        d = o - mu
        o = d * lax.rsqrt(jnp.mean(d * d, axis=-1, keepdims=True) + NORM_EPS)
        gt = g_ref[0, rs, :]
        o_ref[0, rs, :] = (gt * jax.nn.sigmoid(gt) * o).astype(o_ref.dtype)
    st_ref[...] = state


def retention_mixer_pallas(proj3, B, S, T=512):
    H, Dh, C = RET_HEADS, HEAD_DIM, RET_CHUNK
    assert S % T == 0 and T % C == 0
    freqs = RET_THETA ** (-np.linspace(0.0, 1.0, Dh // 2))
    ang = np.arange(S, dtype=np.float64)[:, None] * freqs[None, :]
    cos = jnp.asarray(np.concatenate([np.cos(ang), np.cos(ang)], 1), f32)
    sin = jnp.asarray(np.concatenate([-np.sin(ang), np.sin(ang)], 1), f32)
    log_gamma = np.log1p(-(2.0 ** (-5.0 - np.arange(H, dtype=np.float64))))
    n = np.arange(C, dtype=np.float64)
    diff = n[:, None] - n[None, :]
    decay_in = np.where(diff >= 0, np.exp(log_gamma[:, None, None] * np.maximum(diff, 0.0)), 0.0)
    xi = np.exp(log_gamma[:, None] * (n + 1.0))[:, :, None] * np.ones((1, 1, Dh))
    zeta = np.exp(log_gamma[:, None] * (C - 1.0 - n))[:, :, None] * np.ones((1, 1, Dh))
    gamma_c = np.exp(log_gamma * C)[:, None, None] * np.ones((1, SUB, Dh))
    blk = lambda j: pl.BlockSpec((1, T, Dh), lambda b, h, t: (b, t, (OFF_RET + j * D_GRP) // Dh + h))
    tab = pl.BlockSpec((T, Dh), lambda b, h, t: (t, 0))
    per_head = lambda r: pl.BlockSpec((1, r, Dh), lambda b, h, t: (h, 0, 0))
    return pl.pallas_call(
        functools.partial(_retention_body, T=T),
        grid=(B, H, S // T),
        in_specs=[blk(0), blk(1), blk(2), blk(3), tab, tab, per_head(C), per_head(C), per_head(C), per_head(SUB)],
        out_specs=pl.BlockSpec((1, T, Dh), lambda b, h, t: (b, t, h)),
        out_shape=jax.ShapeDtypeStruct((B, S, D_GRP), bf16),
        scratch_shapes=[pltpu.VMEM((Dh, Dh), f32)],
        compiler_params=pltpu.CompilerParams(dimension_semantics=("parallel", "parallel", "arbitrary"),
                                             vmem_limit_bytes=VMEM_LIMIT),
        name="retention",
    )(proj3, proj3, proj3, proj3, cos, sin, jnp.asarray(decay_in, f32), jnp.asarray(xi, f32),
      jnp.asarray(zeta, f32), jnp.asarray(gamma_c, f32))


def _dot3(a, b):
    ah, al = _split2(a)
    bh, bl = _split2(b)
    return (jnp.dot(ah, bh, preferred_element_type=f32) + jnp.dot(ah, bl, preferred_element_type=f32)
            + jnp.dot(al, bh, preferred_element_type=f32))


def _deltanet_body(q_ref, k_ref, v_ref, z_ref, qp_ref, kp_ref, vp_ref, ab_ref, cw_ref, alog_ref, dtb_ref, ng_ref,
                   o_ref, st_ref, *, T):
    h = pl.program_id(1)
    first = pl.program_id(2) == 0

    @pl.when(first)
    def _():
        st_ref[...] = jnp.zeros_like(st_ref)

    C, Dh = DN_CHUNK, HEAD_DIM
    nt = (((1,), (1,)), ((), ()))

    def conv_silu(x_ref, prev_ref, j):
        prev = jnp.where(first, 0.0, prev_ref[0])
        xcat = jnp.concatenate([prev, x_ref[0]], axis=0)
        w = cw_ref[:, j, :]
        y = sum(w[i:i + 1, :] * xcat[SUB - (CONV_WIDTH - 1) + i:SUB - (CONV_WIDTH - 1) + i + T, :]
                for i in range(CONV_WIDTH))
        return y * jax.nn.sigmoid(y)

    def l2n(x):
        return x * lax.rsqrt(jnp.sum(x * x, axis=-1, keepdims=True) + 1e-6)

    q_all = l2n(conv_silu(q_ref, qp_ref, 0)) * Dh ** -0.5
    k_all = l2n(conv_silu(k_ref, kp_ref, 1))
    v_all = conv_silu(v_ref, vp_ref, 2)
    ab = ab_ref[0]
    lane = lax.broadcasted_iota(jnp.int32, ab.shape, 1)
    a_col = jnp.sum(jnp.where(lane == 3 * NSA_HEADS + h, ab, 0.0), axis=-1, keepdims=True)
    b_col = jnp.sum(jnp.where(lane == 3 * NSA_HEADS + DN_HEADS + h, ab, 0.0), axis=-1, keepdims=True)
    beta_all = jax.nn.sigmoid(b_col)
    g_all = -jnp.exp(alog_ref[0, 0:1, :]) * jax.nn.softplus(a_col + dtb_ref[0, 0:1, :])

    ri = lax.broadcasted_iota(jnp.int32, (C, C), 0)
    ci = lax.broadcasted_iota(jnp.int32, (C, C), 1)
    tril, strict = ri >= ci, ri > ci
    ltri = tril.astype(bf16)
    eye = (ri == ci).astype(f32)
    chunks = [slice(c * C, (c + 1) * C) for c in range(T // C)]
    qs, ks_, vs_ = [q_all[rs] for rs in chunks], [k_all[rs] for rs in chunks], [v_all[rs] for rs in chunks]
    betas = [beta_all[rs] for rs in chunks]
    gcs = []
    for rs in chunks:
        gh, gm, gl = _split3(g_all[rs])
        gcs.append(jnp.dot(ltri, gh, preferred_element_type=f32) + jnp.dot(ltri, gm, preferred_element_type=f32)
                   + jnp.dot(ltri, gl, preferred_element_type=f32))
    decays = [jnp.exp(jnp.where(tril, gc[:, :C] - gc[:, :C].T, -jnp.inf)) for gc in gcs]
    kbs = [k.astype(bf16) for k in ks_]
    kbetas = [k * b for k, b in zip(ks_, betas)]
    ms = [jnp.where(strict, lax.dot_general(kbeta.astype(bf16), kb, nt, preferred_element_type=f32) * d, 0.0)
          for kbeta, kb, d in zip(kbetas, kbs, decays)]
    attns = [jnp.where(tril, lax.dot_general(q.astype(bf16), kb, nt, preferred_element_type=f32) * d, 0.0).astype(bf16)
             for q, kb, d in zip(qs, kbs, decays)]
    t_invs, pws = [eye - m for m in ms], ms
    for _ in range(5):
        pws = [_dot3(pw, pw) for pw in pws]
        t_invs = [_dot3(t, eye + pw) for t, pw in zip(t_invs, pws)]
    egcs = [jnp.exp(gc) for gc in gcs]
    tbs = [t.astype(bf16) for t in t_invs]
    us = [jnp.dot(tb, (v * b).astype(bf16), preferred_element_type=f32) for tb, v, b in zip(tbs, vs_, betas)]
    ws = [jnp.dot(tb, (kbeta * egc).astype(bf16), preferred_element_type=f32).astype(bf16)
          for tb, kbeta, egc in zip(tbs, kbetas, egcs)]
    state = st_ref[...]
    for c, rs in enumerate(chunks):
        g_last = gcs[c][C - 1:C, :]
        q_dec = (qs[c] * egcs[c]).astype(bf16)
        k_dec = (ks_[c] * jnp.exp(g_last - gcs[c])).astype(bf16)
        sb = state.astype(bf16)
        v_new = us[c] - jnp.dot(ws[c], sb, preferred_element_type=f32)
        vnb = v_new.astype(bf16)
        o = jnp.dot(q_dec, sb, preferred_element_type=f32) + jnp.dot(attns[c], vnb, preferred_element_type=f32)
        state = state * jnp.exp(g_last) + jnp.dot(k_dec.T, vnb, preferred_element_type=f32)
        o = o * lax.rsqrt(jnp.mean(o * o, axis=-1, keepdims=True) + NORM_EPS) * ng_ref[...]
        zt = z_ref[0, rs, :]
        o_ref[0, rs, :] = (o * (zt * jax.nn.sigmoid(zt))).astype(o_ref.dtype)
    st_ref[...] = state


def deltanet_mixer_pallas(proj3, conv_w, a_log, dt_bias, norm_g, B, S, T=256):
    H, Dh = DN_HEADS, HEAD_DIM
    assert S % T == 0 and T % DN_CHUNK == 0
    blk = lambda j: pl.BlockSpec((1, T, Dh), lambda b, h, t: (b, t, (OFF_DN + j * D_GRP) // Dh + h))
    prev = lambda j: pl.BlockSpec((1, SUB, Dh), lambda b, h, t: (b, jnp.maximum(t * (T // SUB) - 1, 0),
                                                                (OFF_DN + j * D_GRP) // Dh + h))
    per_head = pl.BlockSpec((1, SUB, Dh), lambda b, h, t: (h, 0, 0))
    bcast = lambda x: jnp.broadcast_to(x.astype(f32)[:, None, None], (H, SUB, Dh))
    return pl.pallas_call(
        functools.partial(_deltanet_body, T=T),
        grid=(B, H, S // T),
        in_specs=[blk(0), blk(1), blk(2), blk(3), prev(0), prev(1), prev(2),
                  pl.BlockSpec((1, T, LANES), lambda b, h, t: (b, t, OFF_SMALL // LANES)),
                  pl.BlockSpec((CONV_WIDTH, 3, Dh), lambda b, h, t: (0, 0, h)),
                  per_head, per_head,
                  pl.BlockSpec((1, Dh), lambda b, h, t: (0, 0))],
        out_specs=pl.BlockSpec((1, T, Dh), lambda b, h, t: (b, t, h)),
        out_shape=jax.ShapeDtypeStruct((B, S, D_GRP), bf16),
        scratch_shapes=[pltpu.VMEM((Dh, Dh), f32)],
        compiler_params=pltpu.CompilerParams(dimension_semantics=("parallel", "parallel", "arbitrary"),
                                             vmem_limit_bytes=VMEM_LIMIT),
        name="deltanet",
    )(proj3, proj3, proj3, proj3, proj3, proj3, proj3, proj3, conv_w.reshape(CONV_WIDTH, 3, D_GRP),
      bcast(a_log), bcast(dt_bias), norm_g.reshape(1, Dh))


def _w_in_pieces():
    n0, r0, d0, w0 = 0, NSA_COLS, NSA_COLS + RET_COLS, NSA_COLS + RET_COLS + DN_COLS
    n_gate = 3 * NSA_HEADS
    return [
        (OFF_RW, w0, 3 * D_GRP),
        (OFF_RET, r0, 4 * D_GRP),
        (OFF_DN, d0, 3 * D_GRP),
        (OFF_DN + 3 * D_GRP, d0 + 3 * D_GRP + 2 * DN_HEADS, D_GRP),
        (OFF_NSA_Q, n0, D_GRP + 6 * NSA_KV_W),
        (OFF_LORA, w0 + 3 * D_GRP, RW_LORA),
        (OFF_SMALL, n0 + D_GRP + 6 * NSA_KV_W, n_gate),
        (OFF_SMALL + n_gate, d0 + 3 * D_GRP, 2 * DN_HEADS),
    ]


def _pack_w_in_body(w_ref, o_ref):
    rows = o_ref.shape[0]
    for off, width in ((OFF_LORA + RW_LORA, RW_LORA_PAD - RW_LORA), (OFF_SMALL, LANES)):
        o_ref[:, off:off + width] = jnp.zeros((rows, width), o_ref.dtype)
    for dst, src, width in _w_in_pieces():
        o_ref[:, dst:dst + width] = w_ref[:, src:src + width].astype(o_ref.dtype)


def pack_w_in(w_in, tr=128):
    K = w_in.shape[0]
    return pl.pallas_call(
        _pack_w_in_body,
        grid=(K // tr,),
        in_specs=[pl.BlockSpec((tr, P_TOTAL), lambda i: (i, 0))],
        out_specs=pl.BlockSpec((tr, P_PAD), lambda i: (i, 0)),
        out_shape=jax.ShapeDtypeStruct((K, P_PAD), bf16),
        compiler_params=pltpu.CompilerParams(dimension_semantics=("parallel",), vmem_limit_bytes=VMEM_LIMIT),
        name="pack_w_in",
    )(w_in)


def hybrid_mixer(h, hn, B, S, w_in, w_out, ck_w1, ck_w2, ck_pe, cv_w1, cv_w2, cv_pe, dn_conv, dn_a_log, dn_dt_bias,
                 dn_norm, rw_mu, rw_w0, rw_w2, rw_a0, rw_a2, rw_g2, rw_kk, rw_ka, rw_rk, rw_ln_g, rw_ln_b):
    M = B * S
    proj = matmul(hn, pack_w_in(w_in), name="in_proj")
    p3 = proj.reshape(B, S, P_PAD)
    o_nsa = nsa_mixer_pallas(proj, ck_w1, ck_w2, ck_pe, cv_w1, cv_w2, cv_pe, B, S)
    o_ret = retention_mixer_pallas(p3, B, S)
    o_dn = deltanet_mixer_pallas(p3, dn_conv, dn_a_log, dn_dt_bias, dn_norm, B, S, T=512)
    o_rw = rwkv7_mixer(proj, OFF_RW // (3 * D_GRP), OFF_LORA // RW_LORA_PAD, rw_mu, rw_w0, rw_w2, rw_a0, rw_a2,
                       rw_g2, rw_kk, rw_ka, rw_rk, rw_ln_g, rw_ln_b, B, S)
    parts = [o.reshape(M, D_GRP) for o in (o_nsa, o_ret, o_dn, o_rw)]
    return out_proj_residual(parts, w_out.astype(bf16), h)


def kernel(x, p, norm_mix, w_in, w_out, nsa_ck_w1, nsa_ck_w2, nsa_ck_pe, nsa_cv_w1, nsa_cv_w2, nsa_cv_pe, dn_conv, dn_a_log, dn_dt_bias, dn_norm, rw_mu, rw_w0, rw_w2, rw_a0, rw_a2, rw_g2, rw_kk, rw_ka, rw_rk, rw_ln_g, rw_ln_b, norm_moe, moe_router_grp, moe_router_grp_b, moe_router_exp, moe_router_exp_b, moe_w_gate, moe_w_up, moe_w_down, norm_ple, ple_w, ple_gate, norm_final):
    B, S, D = x.shape
    M = B * S
    h = x.reshape(M, D)
    for i in range(DEPTH):
        hn = rms_norm_rows(h, norm_mix[i], bf16)
        h = hybrid_mixer(h, hn, B, S, w_in[i], w_out[i], nsa_ck_w1[i], nsa_ck_w2[i], nsa_ck_pe[i], nsa_cv_w1[i],
                         nsa_cv_w2[i], nsa_cv_pe[i], dn_conv[i], dn_a_log[i], dn_dt_bias[i], dn_norm[i],
                         rw_mu[i], rw_w0[i], rw_w2[i], rw_a0[i], rw_a2[i], rw_g2[i], rw_kk[i], rw_ka[i],
                         rw_rk[i], rw_ln_g[i], rw_ln_b[i])
        hn = rms_norm_rows(h, norm_moe[i], bf16)
        h = h + hier_moe(hn, moe_router_grp[i], moe_router_grp_b[i], moe_router_exp[i], moe_router_exp_b[i],
                         moe_w_gate[i], moe_w_up[i], moe_w_down[i])
        hn = rms_norm_rows(h, norm_ple[i], bf16)
        h = ple_residual(hn, ple_gate[i].astype(bf16), p[i].reshape(M, PLE_DIM), ple_w[i].astype(bf16), h)
    return rms_norm_rows(h, norm_final, f32).reshape(B, S, D)
```

```python
import functools

import jax
import jax.numpy as jnp
import numpy as np
from jax import lax
from jax.experimental import pallas as pl
from jax.experimental.pallas import tpu as pltpu

D_MODEL = 4096
BATCH = 2
SEQ = 4096
DEPTH = 2
f32 = jnp.float32
bf16 = jnp.bfloat16
D_MIX = D_MODEL
D_GRP = D_MIX // 4
HEAD_DIM = 128
NORM_EPS = 1e-6
NSA_HEADS = D_GRP // HEAD_DIM
NSA_KV_HEADS = 2
NSA_GROUP = NSA_HEADS // NSA_KV_HEADS
NSA_KV_W = NSA_KV_HEADS * HEAD_DIM
CMP_LEN = 32
CMP_STRIDE = 16
SEL_LEN = 64
N_SEL = 16
N_LOCAL_FORCED = 2
FORCED_SCORE = 1e4
WINDOW = 512
SEL_QBLK = 64
WIN_QBLK = 128
ROPE_THETA = 500000.0
ROPE_DIMS = HEAD_DIM // 4
RET_HEADS = D_GRP // HEAD_DIM
RET_CHUNK = 128
RET_THETA = 10000.0
DN_HEADS = D_GRP // HEAD_DIM
DN_CHUNK = 64
CONV_WIDTH = 4
RW_HEAD_DIM = 64
RW_HEADS = D_GRP // RW_HEAD_DIM
RW_DECAY_LORA = 64
RW_AAA_LORA = 64
RW_GATE_LORA = 160
RW_LN_EPS = 64e-5
N_GROUPS = 4
EXPERTS_PER_GROUP = 8
N_EXPERTS = N_GROUPS * EXPERTS_PER_GROUP
TOP_K = 2
D_EXPERT = 768
ROW_BLOCK = 256
PLE_DIM = 256
NSA_COLS = D_GRP + 6 * NSA_KV_W + 3 * NSA_HEADS
RET_COLS = 4 * D_GRP
DN_COLS = 4 * D_GRP + 2 * DN_HEADS
RW_COLS = 3 * D_GRP + RW_DECAY_LORA + RW_AAA_LORA + RW_GATE_LORA
P_TOTAL = NSA_COLS + RET_COLS + DN_COLS + RW_COLS

RW_LORA = RW_DECAY_LORA + RW_AAA_LORA + RW_GATE_LORA
RW_LORA_PAD = 384
RW_GROUPS_PER_BODY = 4
LANES = 128
SUB = 8

OFF_RW = 0
OFF_RET = 3 * D_GRP
OFF_DN = OFF_RET + 4 * D_GRP
OFF_NSA_Q = OFF_DN + 4 * D_GRP
OFF_NSA_KV = OFF_NSA_Q + D_GRP
OFF_LORA = OFF_NSA_KV + 6 * NSA_KV_W
OFF_SMALL = OFF_LORA + RW_LORA_PAD
P_PAD = OFF_SMALL + LANES
assert P_PAD == 14336 and OFF_LORA % RW_LORA_PAD == 0

VMEM_LIMIT = 56 * 1024 * 1024


def split_cols(x, sizes):
    return jnp.split(x, np.cumsum(sizes)[:-1].tolist(), axis=-1)


def rms_norm(x, g, eps=NORM_EPS):
    xf = x.astype(f32)
    y = xf * lax.rsqrt(jnp.mean(xf * xf, -1, keepdims=True) + eps)
    return (y * g.astype(f32)).astype(x.dtype)


def l2norm(x, eps=1e-6):
    return x * lax.rsqrt(jnp.sum(x * x, -1, keepdims=True) + eps)


def head_layer_norm(x, eps):
    mu = jnp.mean(x, -1, keepdims=True)
    return (x - mu) * lax.rsqrt(jnp.var(x, -1, keepdims=True) + eps)


def masked_softmax(s, mask):
    s = jnp.where(mask, s.astype(f32), -jnp.inf)
    m = jnp.max(s, -1, keepdims=True)
    m = jnp.where(jnp.isfinite(m), m, 0.0)
    e = jnp.where(mask, jnp.exp(s - m), 0.0)
    return e / jnp.maximum(jnp.sum(e, -1, keepdims=True), jnp.finfo(f32).tiny)


def rotary(x, freqs, rot_dims):
    S = x.shape[1]
    half = rot_dims // 2
    ang = jnp.arange(S, dtype=f32)[:, None] * freqs[None, :]
    ang = ang.reshape((S,) + (1,) * (x.ndim - 3) + (half,))
    cos, sin = jnp.cos(ang), jnp.sin(ang)
    x1 = x[..., :half].astype(f32)
    x2 = x[..., half:rot_dims].astype(f32)
    out = jnp.concatenate([x1 * cos - x2 * sin, x2 * cos + x1 * sin, x[..., rot_dims:].astype(f32)], -1)
    return out.astype(x.dtype)


def cmp_sel_overlap(S):
    n_cmp = (S - CMP_LEN) // CMP_STRIDE + 1
    c0 = np.arange(n_cmp)[:, None] * CMP_STRIDE
    s0 = np.arange(S // SEL_LEN)[None, :] * SEL_LEN
    ov = np.clip(np.minimum(c0 + CMP_LEN, s0 + SEL_LEN) - np.maximum(c0, s0), 0, None) / CMP_LEN
    return jnp.asarray(ov, f32)


def nsa_mixer(q, kc, vc, ks, vs, kw, vw, gate_logit, ck_w1, ck_w2, ck_pe, cv_w1, cv_w2, cv_pe):
    B, S, _ = q.shape
    G, R, Dh = NSA_KV_HEADS, NSA_GROUP, HEAD_DIM
    scale = Dh ** -0.5
    half = ROPE_DIMS // 2
    freqs = ROPE_THETA ** (-jnp.arange(half, dtype=f32) / half)
    q = rotary(q.reshape(B, S, G, R, Dh), freqs, ROPE_DIMS)
    kc, vc, ks, vs, kw, vw = [t.reshape(B, S, G, Dh) for t in (kc, vc, ks, vs, kw, vw)]
    kc, ks, kw = [rotary(t, freqs, ROPE_DIMS) for t in (kc, ks, kw)]
    pos = jnp.arange(S)
    n_cmp = (S - CMP_LEN) // CMP_STRIDE + 1
    blk_tok = np.arange(n_cmp)[:, None] * CMP_STRIDE + np.arange(CMP_LEN)[None, :]

    def compress(t, w1, w2, pe):
        blk = t[:, blk_tok] + pe[:, None, :]
        blk = blk.transpose(0, 3, 1, 2, 4).reshape(B, G, n_cmp, CMP_LEN * Dh)
        return jax.nn.silu(blk @ w1) @ w2

    k_cmp = compress(kc, ck_w1, ck_w2, ck_pe)
    v_cmp = compress(vc, cv_w1, cv_w2, cv_pe)
    s_cmp = jnp.einsum('bsgrd,bgnd->bgrsn', q, k_cmp) * scale
    cmp_mask = jnp.asarray(blk_tok[:, -1])[None, :] <= pos[:, None]
    p_cmp = masked_softmax(s_cmp, cmp_mask)
    o_cmp = jnp.einsum('bgrsn,bgnd->bsgrd', p_cmp, v_cmp.astype(f32))
    n_sel_blk = S // SEL_LEN
    k_top = min(N_SEL, n_sel_blk)
    imp = jnp.einsum('bgrsn,nj->bgsj', p_cmp, cmp_sel_overlap(S))
    blk = jnp.arange(n_sel_blk)[None, :]
    cur = (pos // SEL_LEN)[:, None]
    visible = blk <= cur
    forced = (blk == 0) | (visible & (blk > cur - N_LOCAL_FORCED))
    imp = jnp.where(forced, FORCED_SCORE, jnp.where(visible, imp, -1.0))
    sel_idx = lax.top_k(imp, k_top)[1]
    ks_blk = ks.reshape(B, n_sel_blk, SEL_LEN, G, Dh).transpose(0, 3, 1, 2, 4)
    vs_blk = vs.reshape(B, n_sel_blk, SEL_LEN, G, Dh).transpose(0, 3, 1, 2, 4)
    n_qb = S // SEL_QBLK
    q_b = q.reshape(B, n_qb, SEL_QBLK, G, R, Dh).transpose(1, 0, 2, 3, 4, 5)
    idx_b = sel_idx.reshape(B, G, n_qb, SEL_QBLK, k_top).transpose(2, 0, 1, 3, 4)
    qpos_b = pos.reshape(n_qb, SEL_QBLK)
    b_ix = jnp.arange(B)[:, None, None, None]
    g_ix = jnp.arange(G)[None, :, None, None]

    def sel_block(args):
        qb, ib, qp = args
        kb = ks_blk[b_ix, g_ix, ib].reshape(B, G, SEL_QBLK, k_top * SEL_LEN, Dh)
        vb = vs_blk[b_ix, g_ix, ib].reshape(B, G, SEL_QBLK, k_top * SEL_LEN, Dh)
        tok = (ib[..., None] * SEL_LEN + jnp.arange(SEL_LEN)).reshape(B, G, SEL_QBLK, k_top * SEL_LEN)
        s = jnp.einsum('bqgrd,bgqkd->bgrqk', qb, kb) * scale
        mask = tok[:, :, None] <= qp[None, None, None, :, None]
        pr = masked_softmax(s, mask)
        return jnp.einsum('bgrqk,bgqkd->bqgrd', pr, vb.astype(f32))

    o_sel = lax.map(sel_block, (q_b, idx_b, qpos_b))
    o_sel = o_sel.transpose(1, 0, 2, 3, 4, 5).reshape(B, S, G, R, Dh)
    n_wb = S // WIN_QBLK
    span = WINDOW + WIN_QBLK
    key_idx = np.arange(n_wb)[:, None] * WIN_QBLK + np.arange(span)[None, :]
    pad = ((0, 0), (WINDOW, 0), (0, 0), (0, 0))
    kb = jnp.pad(kw, pad)[:, key_idx]
    vb = jnp.pad(vw, pad)[:, key_idx]
    qb = q.reshape(B, n_wb, WIN_QBLK, G, R, Dh)
    s_win = jnp.einsum('bnqgrd,bnkgd->bgrnqk', qb, kb) * scale
    kpos = jnp.asarray(key_idx - WINDOW)[:, None, :]
    qpos = pos.reshape(n_wb, WIN_QBLK)[:, :, None]
    win_mask = (kpos >= 0) & (kpos <= qpos) & (kpos > qpos - WINDOW)
    p_win = masked_softmax(s_win, win_mask)
    o_win = jnp.einsum('bgrnqk,bnkgd->bnqgrd', p_win, vb.astype(f32)).reshape(B, S, G, R, Dh)
    gate = jax.nn.sigmoid(gate_logit.astype(f32)).reshape(B, S, G, R, 3)
    o = gate[..., 0:1] * o_cmp + gate[..., 1:2] * o_sel + gate[..., 2:3] * o_win
    return o.reshape(B, S, D_GRP)


def retention_mixer(q, k, v, g):
    B, S, _ = q.shape
    H, Dh, C = RET_HEADS, HEAD_DIM, RET_CHUNK
    n_c = S // C
    freqs = RET_THETA ** (-jnp.linspace(0.0, 1.0, Dh // 2, dtype=f32))
    heads = lambda t: t.reshape(B, S, H, Dh)
    q = rotary(heads(q), freqs, Dh).astype(f32)
    k = rotary(heads(k), freqs, Dh).astype(f32) * Dh ** -0.5
    v = heads(v).astype(f32)
    log_gamma = jnp.log1p(-(2.0 ** (-5.0 - jnp.arange(H, dtype=f32))))
    n = jnp.arange(C, dtype=f32)
    diff = n[:, None] - n[None, :]
    decay_in = jnp.where(diff >= 0, jnp.exp(log_gamma[:, None, None] * jnp.maximum(diff, 0.0)), 0.0)
    xi = jnp.exp(log_gamma[:, None] * (n + 1.0))
    zeta = jnp.exp(log_gamma[:, None] * (C - 1.0 - n))
    gamma_c = jnp.exp(log_gamma * C)
    chunks = lambda t: t.reshape(B, n_c, C, H, Dh).transpose(1, 0, 3, 2, 4)

    def step(state, inp):
        qc, kc, vc = inp
        inner = jnp.einsum('bhnd,bhmd->bhnm', qc, kc) * decay_in
        o = jnp.einsum('bhnm,bhme->bhne', inner, vc) + jnp.einsum('bhnd,bhde->bhne', qc, state) * xi[..., None]
        state = state * gamma_c[:, None, None] + jnp.einsum('bhmd,bhme->bhde', kc * zeta[..., None], vc)
        return state, o

    _, o = lax.scan(step, jnp.zeros((B, H, Dh, Dh), f32), (chunks(q), chunks(k), chunks(v)))
    o = head_layer_norm(o.transpose(1, 0, 3, 2, 4).reshape(B, S, H, Dh), NORM_EPS)
    return jax.nn.silu(g.astype(f32)) * o.reshape(B, S, D_GRP)


def deltanet_mixer(q, k, v, a, b, z, conv_w, a_log, dt_bias, norm_g):
    B, S, _ = q.shape
    H, Dh, C = DN_HEADS, HEAD_DIM, DN_CHUNK
    n_c = S // C
    qkv = jnp.concatenate([q, k, v], -1)
    qkv = lax.conv_general_dilated(qkv, conv_w[:, None, :], window_strides=(1,), padding=[(CONV_WIDTH - 1, 0)],
                                   dimension_numbers=('NWC', 'WIO', 'NWC'), feature_group_count=3 * D_GRP)
    qkv = jax.nn.silu(qkv.astype(f32))
    q, k, v = [t.reshape(B, S, H, Dh) for t in jnp.split(qkv, 3, -1)]
    q = l2norm(q) * Dh ** -0.5
    k = l2norm(k)
    beta = jax.nn.sigmoid(b.astype(f32))
    g = -jnp.exp(a_log.astype(f32)) * jax.nn.softplus(a.astype(f32) + dt_bias.astype(f32))
    chunks = lambda t: jnp.moveaxis(t.reshape((B, n_c, C) + t.shape[2:]), 3, 1)
    qc, kc, vc, bc = chunks(q), chunks(k), chunks(v), chunks(beta)
    gc = jnp.cumsum(chunks(g), -1)
    kbeta = kc * bc[..., None]
    vbeta = vc * bc[..., None]
    idx = jnp.arange(C)
    tril = idx[:, None] >= idx[None, :]
    strict = idx[:, None] > idx[None, :]
    decay = jnp.exp(jnp.where(tril, gc[..., :, None] - gc[..., None, :], -jnp.inf))
    m = jnp.where(strict, jnp.einsum('bhncd,bhnkd->bhnck', kbeta, kc) * decay, 0.0)
    eye = jnp.eye(C, dtype=f32)
    t_inv = lax.linalg.triangular_solve(eye + m, jnp.broadcast_to(eye, m.shape), left_side=True,
                                        lower=True, unit_diagonal=True)
    u = t_inv @ vbeta
    w = t_inv @ (kbeta * jnp.exp(gc)[..., None])
    attn_in = jnp.where(tril, jnp.einsum('bhncd,bhnkd->bhnck', qc, kc) * decay, 0.0)
    q_dec = qc * jnp.exp(gc)[..., None]
    k_dec = kc * jnp.exp(gc[..., -1:] - gc)[..., None]
    g_last = jnp.exp(gc[..., -1])

    def step(state, inp):
        u_i, w_i, a_i, qd_i, kd_i, gl_i = inp
        v_new = u_i - w_i @ state
        o = qd_i @ state + a_i @ v_new
        state = state * gl_i[..., None, None] + jnp.swapaxes(kd_i, -1, -2) @ v_new
        return state, o

    xs = tuple(jnp.moveaxis(t, 2, 0) for t in (u, w, attn_in, q_dec, k_dec, g_last))
    _, o = lax.scan(step, jnp.zeros((B, H, Dh, Dh), f32), xs)
    o = o.transpose(1, 0, 3, 2, 4).reshape(B, S, H, Dh)
    o = rms_norm(o, norm_g) * jax.nn.silu(z.astype(f32)).reshape(B, S, H, Dh)
    return o.reshape(B, S, D_GRP)


def _split2(x):
    h = x.astype(bf16)
    return h, (x - h.astype(f32)).astype(bf16)


def _split3(x):
    h = x.astype(bf16)
    r1 = x - h.astype(f32)
    m = r1.astype(bf16)
    return h, m, (r1 - m.astype(f32)).astype(bf16)


def _rwkv_prep_body(x_ref, xprev_ref, l_ref, lprev_ref, mu_ref, mul_ref, w0_ref, a0_ref, kkw_ref, ka_ref,
                    w2_ref, a2_ref, g2_ref, e_ref, r_ref, w_ref, k_ref, v_ref, kk_ref, b_ref, g_ref, *, tm, S):
    i = pl.program_id(0)
    first = (i * tm) % S == 0

    def token_mix(x, prev8, mu):
        rows = lax.broadcasted_iota(jnp.int32, x.shape, 0)
        prev_row = jnp.where(first, 0.0, prev8[SUB - 1:SUB, :])
        x_prev = jnp.where(rows == 0, prev_row, pltpu.roll(x, 1, axis=0))
        return x + (x_prev - x) * mu

    x = token_mix(x_ref[...], xprev_ref[...], mu_ref[...])
    lora = token_mix(l_ref[...], lprev_ref[...], mul_ref[...])
    r = x[:, 0:D_GRP]
    k = x[:, D_GRP:2 * D_GRP]
    v = x[:, 2 * D_GRP:3 * D_GRP]
    dw = jnp.dot(jnp.tanh(lora).astype(bf16), w2_ref[...], preferred_element_type=f32)
    da = jnp.dot(lora.astype(bf16), a2_ref[...], preferred_element_type=f32)
    g = jnp.dot(jax.nn.sigmoid(lora).astype(bf16), g2_ref[...], preferred_element_type=f32)
    w = -jax.nn.softplus(-(w0_ref[...] + dw)) - 0.5
    decay = jnp.exp(-jnp.exp(w))
    a = jax.nn.sigmoid(a0_ref[...] + da)
    kq = k * kkw_ref[...]
    sq_h, sq_l = _split2(kq * kq)
    ss = (jnp.dot(sq_h, e_ref[...], preferred_element_type=f32)
          + jnp.dot(sq_l, e_ref[...], preferred_element_type=f32))
    kk = kq * lax.rsqrt(ss + 1e-6)
    r_ref[...] = r
    w_ref[...] = decay
    k_ref[...] = k * (1.0 + (a - 1.0) * ka_ref[...])
    v_ref[...] = v
    kk_ref[...] = kk
    b_ref[...] = kk * a
    g_ref[...] = g


def rwkv_prep(proj, rkv_blk, lora_blk, mu, w0, w2, a0, a2, g2, k_k, k_a, S, tm=256):
    M = proj.shape[0]
    assert M % tm == 0 and S % tm == 0
    W3 = 3 * D_GRP
    mu_x = mu[:W3].reshape(1, W3)
    mu_l = jnp.pad(mu[W3:], (0, RW_LORA_PAD - RW_LORA)).reshape(1, RW_LORA_PAD)
    w2p = jnp.zeros((RW_LORA_PAD, D_GRP), f32).at[0:RW_DECAY_LORA].set(w2).astype(bf16)
    a2p = jnp.zeros((RW_LORA_PAD, D_GRP), f32).at[RW_DECAY_LORA:RW_DECAY_LORA + RW_AAA_LORA].set(a2).astype(bf16)
    g2p = jnp.zeros((RW_LORA_PAD, D_GRP), f32).at[RW_DECAY_LORA + RW_AAA_LORA:RW_LORA].set(g2).astype(bf16)
    hid = np.arange(D_GRP) // RW_HEAD_DIM
    e = jnp.asarray(hid[:, None] == hid[None, :], bf16)
    row = lambda t: t.reshape(1, D_GRP)
    vec = pl.BlockSpec((1, D_GRP), lambda i: (0, 0))
    wspec = pl.BlockSpec((RW_LORA_PAD, D_GRP), lambda i: (0, 0))
    ospec = pl.BlockSpec((tm, D_GRP), lambda i: (i, 0))
    prev_row_blk = lambda i: jnp.maximum(i * (tm // SUB) - 1, 0)
    return pl.pallas_call(
        functools.partial(_rwkv_prep_body, tm=tm, S=S),
        grid=(M // tm,),
        in_specs=[pl.BlockSpec((tm, W3), lambda i: (i, rkv_blk)),
                  pl.BlockSpec((SUB, W3), lambda i: (prev_row_blk(i), rkv_blk)),
                  pl.BlockSpec((tm, RW_LORA_PAD), lambda i: (i, lora_blk)),
                  pl.BlockSpec((SUB, RW_LORA_PAD), lambda i: (prev_row_blk(i), lora_blk)),
                  pl.BlockSpec((1, W3), lambda i: (0, 0)),
                  pl.BlockSpec((1, RW_LORA_PAD), lambda i: (0, 0)),
                  vec, vec, vec, vec, wspec, wspec, wspec,
                  pl.BlockSpec((D_GRP, D_GRP), lambda i: (0, 0))],
        out_specs=[ospec] * 7,
        out_shape=[jax.ShapeDtypeStruct((M, D_GRP), f32)] * 7,
        compiler_params=pltpu.CompilerParams(dimension_semantics=("parallel",), vmem_limit_bytes=VMEM_LIMIT),
        name="rwkv_prep",
    )(proj, proj, proj, proj, mu_x, mu_l, row(w0), row(a0), row(k_k), row(k_a), w2p, a2p, g2p, e)


def _rwkv_scan_body(r_ref, w_ref, k_ref, v_ref, kk_ref, b_ref, g_ref, lng_ref, lnb_ref, rk_ref, o_ref, s_ref,
                    vc_ref, y_ref, *, G, T):
    tb = pl.program_id(2)

    @pl.when(tb == 0)
    def _():
        s_ref[...] = jnp.zeros_like(s_ref)

    Dh = RW_HEAD_DIM
    lo = lax.broadcasted_iota(jnp.int32, (Dh, LANES), 1) < Dh
    lo8 = lax.broadcasted_iota(jnp.int32, (SUB, LANES), 1) < Dh
    sub_iota = lax.broadcasted_iota(jnp.int32, (4 * SUB, LANES), 0)
    row_iota = lax.broadcasted_iota(jnp.int32, (SUB, LANES), 0)

    def seg_sum(x, m):
        s_lo = jnp.sum(jnp.where(m, x, 0.0), axis=-1, keepdims=True)
        s_hi = jnp.sum(jnp.where(m, 0.0, x), axis=-1, keepdims=True)
        return jnp.where(m, s_lo, s_hi)

    cols = [slice(p * LANES, (p + 1) * LANES) for p in range(G)]
    n_groups = T // SUB

    k_iota = lax.broadcasted_iota(jnp.int32, (8 * SUB, LANES), 0)
    k_head = k_iota // (4 * SUB) == lax.broadcasted_iota(jnp.int32, (8 * SUB, LANES), 1) // Dh
    k_piece = (k_iota // SUB) % 4

    def v_pieces(gi):
        t0 = pl.multiple_of(gi * SUB, SUB)
        vts = []
        for p in range(G):
            vh, vm, vl = _split3(v_ref[0, pl.ds(t0, SUB), cols[p]])
            vp = jnp.concatenate([vh, vm, vl, jnp.zeros_like(vh)], axis=0)
            vts.append(jnp.concatenate([vp[:, :Dh], vp[:, Dh:]], axis=0).T)
        return vts

    def v_columns(vts, i, slot):
        sel = (k_head & (k_piece < 3) & (k_iota % SUB == i)).astype(bf16)
        for p in range(G):
            vc_ref[slot, i, p] = jnp.dot(vts[p], sel, preferred_element_type=f32)

    vts0 = v_pieces(0)
    for i in range(SUB):
        v_columns(vts0, i, 0)

    def run_group(gi, slot, states):
        t0 = pl.multiple_of(gi * SUB, SUB)
        vts_next = v_pieces(jnp.minimum(gi + 1, n_groups - 1))
        r8b = [r_ref[0, pl.ds(t0, SUB), cols[p]].astype(bf16) for p in range(G)]
        yacc = [jnp.zeros((SUB, LANES), f32) for _ in range(G)]
        tiles = [[ref[0, pl.ds(t0, SUB), cols[p]] for ref in (kk_ref, w_ref, b_ref, k_ref)] for p in range(G)]
        for i in range(SUB):
            for p in range(G):
                kk_r, w_r, b_r, k_r = [jnp.broadcast_to(t8[i:i + 1, :], (Dh, LANES)) for t8 in tiles[p]]
                S = states[p]
                sab = -seg_sum(S * kk_r, lo)
                states[p] = S * w_r + sab * b_r + vc_ref[slot, i, p] * k_r
            for p in range(G):
                Sb = states[p].astype(bf16)
                zero = jnp.zeros_like(Sb)
                mt = jnp.concatenate([jnp.where(lo, Sb, zero), jnp.where(lo, zero, Sb)], axis=0)
                y8 = lax.dot_general(r8b[p], mt, (((1,), (1,)), ((), ())), preferred_element_type=f32)
                yacc[p] = jnp.where(row_iota == i, y8, yacc[p])
            v_columns(vts_next, i, 1 - slot)
        for p in range(G):
            y_ref[pl.ds(t0, SUB), cols[p]] = yacc[p]
        return states

    def group_batch(gj, carry):
        states = [s_ref[p] for p in range(G)]
        for u in range(RW_GROUPS_PER_BODY):
            states = run_group(RW_GROUPS_PER_BODY * gj + u, u % 2, states)
        for p in range(G):
            s_ref[p] = states[p]
        return carry

    assert n_groups % RW_GROUPS_PER_BODY == 0 and RW_GROUPS_PER_BODY % 2 == 0
    lax.fori_loop(0, n_groups // RW_GROUPS_PER_BODY, group_batch, 0)

    lo_t = lax.broadcasted_iota(jnp.int32, (T, LANES), 1) < Dh
    for p in range(G):
        y = y_ref[:, cols[p]]
        mean = seg_sum(y, lo_t) * (1.0 / Dh)
        d = y - mean
        var = seg_sum(d * d, lo_t) * (1.0 / Dh)
        yn = d * lax.rsqrt(var + RW_LN_EPS) * lng_ref[:, cols[p]] + lnb_ref[:, cols[p]]
        bonus = seg_sum(r_ref[0, :, cols[p]] * k_ref[0, :, cols[p]] * rk_ref[:, cols[p]], lo_t) * v_ref[0, :, cols[p]]
        o_ref[0, :, cols[p]] = (yn + bonus) * g_ref[0, :, cols[p]]


def rwkv_scan(r, w, k, v, kk, b, g, ln_g, ln_b, r_k, G=8, T=256):
    B, S, D = r.shape
    n_pairs = D // LANES
    assert n_pairs % G == 0 and S % T == 0
    spec = pl.BlockSpec((1, T, G * LANES), lambda bi, pi, ti: (bi, ti, pi))
    vec = pl.BlockSpec((1, G * LANES), lambda bi, pi, ti: (0, pi))
    row = lambda t: t.reshape(1, D)
    return pl.pallas_call(
        functools.partial(_rwkv_scan_body, G=G, T=T),
        grid=(B, n_pairs // G, S // T),
        in_specs=[spec] * 7 + [vec] * 3,
        out_specs=spec,
        out_shape=jax.ShapeDtypeStruct((B, S, D), f32),
        scratch_shapes=[pltpu.VMEM((G, RW_HEAD_DIM, LANES), f32),
                        pltpu.VMEM((2, SUB, G, RW_HEAD_DIM, LANES), f32),
                        pltpu.VMEM((T, G * LANES), f32)],
        compiler_params=pltpu.CompilerParams(dimension_semantics=("parallel", "parallel", "arbitrary"),
                                             vmem_limit_bytes=VMEM_LIMIT),
        name="rwkv_scan",
    )(r, w, k, v, kk, b, g, row(ln_g), row(ln_b), row(r_k))


def rwkv7_mixer(proj, rkv_blk, lora_blk, mu, w0, w2, a0, a2, g2, k_k, k_a, r_k, ln_g, ln_b, B, S):
    outs = rwkv_prep(proj, rkv_blk, lora_blk, mu, w0, w2, a0, a2, g2, k_k, k_a, S)
    r, w, k, v, kk, b, g = [t.reshape(B, S, D_GRP) for t in outs]
    return rwkv_scan(r, w, k, v, kk, b, g, ln_g, ln_b, r_k.reshape(-1))


def _mm_body(a_ref, b_ref, o_ref, acc_ref):
    k = pl.program_id(2)

    @pl.when(k == 0)
    def _():
        acc_ref[...] = jnp.zeros_like(acc_ref)

    acc_ref[...] += jnp.dot(a_ref[...].astype(bf16), b_ref[...].astype(bf16), preferred_element_type=f32)

    @pl.when(k == pl.num_programs(2) - 1)
    def _():
        o_ref[...] = acc_ref[...].astype(o_ref.dtype)


def matmul(a, b, out_dtype=f32, tm=1024, tn=1024, tk=2048, name="matmul"):
    M, K = a.shape
    _, N = b.shape
    tm, tn, tk = min(tm, M), min(tn, N), min(tk, K)
    assert M % tm == 0 and N % tn == 0 and K % tk == 0, (M, N, K, tm, tn, tk)
    return pl.pallas_call(
        _mm_body,
        grid=(M // tm, N // tn, K // tk),
        in_specs=[pl.BlockSpec((tm, tk), lambda i, j, k: (i, k)),
                  pl.BlockSpec((tk, tn), lambda i, j, k: (k, j))],
        out_specs=pl.BlockSpec((tm, tn), lambda i, j, k: (i, j)),
        out_shape=jax.ShapeDtypeStruct((M, N), out_dtype),
        scratch_shapes=[pltpu.VMEM((tm, tn), f32)],
        compiler_params=pltpu.CompilerParams(
            dimension_semantics=("parallel", "parallel", "arbitrary"), vmem_limit_bytes=VMEM_LIMIT),
        name=name,
    )(a, b)


def _rms_norm_body(x_ref, g_ref, o_ref):
    x = x_ref[...]
    y = x * lax.rsqrt(jnp.mean(x * x, axis=-1, keepdims=True) + NORM_EPS)
    o_ref[...] = (y * g_ref[...]).astype(o_ref.dtype)


def rms_norm_rows(x, g, out_dtype, tm=256):
    M, D = x.shape
    return pl.pallas_call(
        _rms_norm_body,
        grid=(M // tm,),
        in_specs=[pl.BlockSpec((tm, D), lambda i: (i, 0)), pl.BlockSpec((1, D), lambda i: (0, 0))],
        out_specs=pl.BlockSpec((tm, D), lambda i: (i, 0)),
        out_shape=jax.ShapeDtypeStruct((M, D), out_dtype),
        compiler_params=pltpu.CompilerParams(dimension_semantics=("parallel",), vmem_limit_bytes=VMEM_LIMIT),
        name="rms_norm",
    )(x, g.reshape(1, D).astype(f32))


def _out_proj_body(a0_ref, a1_ref, a2_ref, a3_ref, b_ref, h_ref, o_ref, acc_ref):
    k = pl.program_id(2)

    @pl.when(k == 0)
    def _():
        acc_ref[...] = h_ref[...]

    for j, a_ref in enumerate((a0_ref, a1_ref, a2_ref, a3_ref)):
        @pl.when(k == j)
        def _(a_ref=a_ref):
            acc_ref[...] += jnp.dot(a_ref[...].astype(bf16), b_ref[...], preferred_element_type=f32)

    @pl.when(k == pl.num_programs(2) - 1)
    def _():
        o_ref[...] = acc_ref[...]


def out_proj_residual(parts, w, h, tm=1024, tn=1024):
    M, D = h.shape
    aspec = pl.BlockSpec((tm, D_GRP), lambda i, j, k: (i, 0))
    return pl.pallas_call(
        _out_proj_body,
        grid=(M // tm, D // tn, len(parts)),
        in_specs=[aspec] * 4 + [pl.BlockSpec((D_GRP, tn), lambda i, j, k: (k, j)),
                                pl.BlockSpec((tm, tn), lambda i, j, k: (i, j))],
        out_specs=pl.BlockSpec((tm, tn), lambda i, j, k: (i, j)),
        out_shape=jax.ShapeDtypeStruct((M, D), f32),
        scratch_shapes=[pltpu.VMEM((tm, tn), f32)],
        compiler_params=pltpu.CompilerParams(
            dimension_semantics=("parallel", "parallel", "arbitrary"), vmem_limit_bytes=VMEM_LIMIT),
        name="out_proj",
    )(*parts, w, h)


def _ple_body(a_ref, b_ref, p_ref, pw_ref, h_ref, o_ref, acc_ref):
    k = pl.program_id(2)

    @pl.when(k == 0)
    def _():
        acc_ref[...] = jnp.zeros_like(acc_ref)

    acc_ref[...] += jnp.dot(a_ref[...], b_ref[...], preferred_element_type=f32)

    @pl.when(k == pl.num_programs(2) - 1)
    def _():
        emb = jnp.dot(p_ref[...].astype(bf16), pw_ref[...], preferred_element_type=f32)
        o_ref[...] = h_ref[...] + emb * jax.nn.sigmoid(acc_ref[...])


def ple_residual(hn, gate_w, p, ple_w, h, tm=1024, tn=1024, tk=2048):
    M, D = h.shape
    return pl.pallas_call(
        _ple_body,
        grid=(M // tm, D // tn, D // tk),
        in_specs=[pl.BlockSpec((tm, tk), lambda i, j, k: (i, k)),
                  pl.BlockSpec((tk, tn), lambda i, j, k: (k, j)),
                  pl.BlockSpec((tm, PLE_DIM), lambda i, j, k: (i, 0)),
                  pl.BlockSpec((PLE_DIM, tn), lambda i, j, k: (0, j)),
                  pl.BlockSpec((tm, tn), lambda i, j, k: (i, j))],
        out_specs=pl.BlockSpec((tm, tn), lambda i, j, k: (i, j)),
        out_shape=jax.ShapeDtypeStruct((M, D), f32),
        scratch_shapes=[pltpu.VMEM((tm, tn), f32)],
        compiler_params=pltpu.CompilerParams(
            dimension_semantics=("parallel", "parallel", "arbitrary"), vmem_limit_bytes=VMEM_LIMIT),
        name="ple",
    )(hn, gate_w, p, ple_w, h)


def _moe_router_body(x_ref, w_ref, b_ref, id_ref, wt_ref):
    logits = jnp.dot(x_ref[...], w_ref[...], preferred_element_type=f32) + b_ref[...]
    lane = lax.broadcasted_iota(jnp.int32, logits.shape, 1).astype(f32)
    ninf = -jnp.inf

    def top1(vals):
        m = jnp.max(vals, axis=-1, keepdims=True)
        idx = jnp.min(jnp.where(vals == m, lane, float(LANES)), axis=-1, keepdims=True)
        return m, idx

    gl = jnp.where(lane < N_GROUPS, logits, ninf)
    g_max, g_idx = top1(gl)
    g_w = 1.0 / jnp.sum(jnp.exp(gl - g_max), axis=-1, keepdims=True)
    lo = N_GROUPS + g_idx * EXPERTS_PER_GROUP
    el = jnp.where((lane >= lo) & (lane < lo + EXPERTS_PER_GROUP), logits, ninf)
    e_max, _ = top1(el)
    pe = jnp.exp(el - e_max)
    p = pe / jnp.sum(pe, axis=-1, keepdims=True)
    p = jnp.where(el == ninf, ninf, p)
    p1, i1 = top1(p)
    p2, i2 = top1(jnp.where(lane == i1, ninf, p))
    scale = g_w / (p1 + p2)
    id_ref[...] = jnp.where(lane == 0, i1 - N_GROUPS, jnp.where(lane == 1, i2 - N_GROUPS, 0.0)).astype(jnp.int32)
    wt_ref[...] = jnp.where(lane == 0, p1 * scale, jnp.where(lane == 1, p2 * scale, 0.0))


def moe_router(xb, router_grp, router_grp_b, router_exp, router_exp_b, tm=512):
    M, D = xb.shape
    n_log = N_GROUPS + N_EXPERTS
    w = jnp.pad(jnp.concatenate([router_grp, router_exp], axis=1), ((0, 0), (0, LANES - n_log))).astype(bf16)
    b = jnp.pad(jnp.concatenate([router_grp_b, router_exp_b]), (0, LANES - n_log)).reshape(1, LANES).astype(f32)
    ids, wts = pl.pallas_call(
        _moe_router_body,
        grid=(M // tm,),
        in_specs=[pl.BlockSpec((tm, D), lambda i: (i, 0)), pl.BlockSpec((D, LANES), lambda i: (0, 0)),
                  pl.BlockSpec((1, LANES), lambda i: (0, 0))],
        out_specs=[pl.BlockSpec((tm, LANES), lambda i: (i, 0))] * 2,
        out_shape=[jax.ShapeDtypeStruct((M, LANES), jnp.int32), jax.ShapeDtypeStruct((M, LANES), f32)],
        compiler_params=pltpu.CompilerParams(dimension_semantics=("parallel",), vmem_limit_bytes=VMEM_LIMIT),
        name="moe_router",
    )(xb, w, b)
    return ids[:, :TOP_K], wts[:, :TOP_K]


def _moe_up_body(be_ref, nu_ref, x_ref, wg_ref, wu_ref, h_ref):
    i = pl.program_id(1)

    @pl.when(i < nu_ref[0])
    def _():
        x = x_ref[...]
        hg = jnp.dot(x, wg_ref[0, 0].astype(bf16), preferred_element_type=f32)
        hu = jnp.dot(x, wu_ref[0, 0].astype(bf16), preferred_element_type=f32)
        h_ref[...] = (hg * jax.nn.sigmoid(hg) * hu).astype(bf16)

    @pl.when(i >= nu_ref[0])
    def _():
        h_ref[...] = jnp.zeros_like(h_ref)


def _moe_down_body(be_ref, nu_ref, h_ref, wd_ref, wt_ref, y_ref):
    i = pl.program_id(0)

    @pl.when(i < nu_ref[0])
    def _():
        y_ref[...] = jnp.dot(h_ref[...], wd_ref[0, 0].astype(bf16), preferred_element_type=f32) * wt_ref[...]

    @pl.when(i >= nu_ref[0])
    def _():
        y_ref[...] = jnp.zeros_like(y_ref)


def moe_experts(x_pad, w_pad, blk_expert, n_used, wg, wu, wd, layer, tf=384):
    A_pad, D = x_pad.shape
    n_blk = A_pad // ROW_BLOCK
    nf = D_EXPERT // tf
    h = pl.pallas_call(
        _moe_up_body,
        grid_spec=pltpu.PrefetchScalarGridSpec(
            num_scalar_prefetch=2,
            grid=(nf, n_blk),
            in_specs=[pl.BlockSpec((ROW_BLOCK, D), lambda f, i, be, nu: (i, 0)),
                      pl.BlockSpec((1, 1, D, tf), lambda f, i, be, nu: (layer, be[i], 0, f)),
                      pl.BlockSpec((1, 1, D, tf), lambda f, i, be, nu: (layer, be[i], 0, f))],
            out_specs=pl.BlockSpec((ROW_BLOCK, tf), lambda f, i, be, nu: (i, f)),
        ),
        out_shape=jax.ShapeDtypeStruct((A_pad, D_EXPERT), bf16),
        compiler_params=pltpu.CompilerParams(
            dimension_semantics=("arbitrary", "arbitrary"), vmem_limit_bytes=VMEM_LIMIT),
        name="moe_up",
    )(blk_expert, n_used, x_pad, wg, wu)
    return pl.pallas_call(
        _moe_down_body,
        grid_spec=pltpu.PrefetchScalarGridSpec(
            num_scalar_prefetch=2,
            grid=(n_blk,),
            in_specs=[pl.BlockSpec((ROW_BLOCK, D_EXPERT), lambda i, be, nu: (i, 0)),
                      pl.BlockSpec((1, 1, D_EXPERT, D), lambda i, be, nu: (layer, be[i], 0, 0)),
                      pl.BlockSpec((ROW_BLOCK, 1), lambda i, be, nu: (i, 0))],
            out_specs=pl.BlockSpec((ROW_BLOCK, D), lambda i, be, nu: (i, 0)),
        ),
        out_shape=jax.ShapeDtypeStruct((A_pad, D), f32),
        compiler_params=pltpu.CompilerParams(dimension_semantics=("arbitrary",), vmem_limit_bytes=VMEM_LIMIT),
        name="moe_down",
    )(blk_expert, n_used, h, wd, w_pad.reshape(A_pad, 1))


def hier_moe(xb, router_grp, router_grp_b, router_exp, router_exp_b, w_gate, w_up, w_down, layer):
    M, D = xb.shape
    expert_ids, weights = moe_router(xb, router_grp, router_grp_b, router_exp, router_exp_b)
    A = M * TOP_K
    e_flat = expert_ids.reshape(A)
    onehot = (e_flat[:, None] == jnp.arange(N_EXPERTS)[None, :]).astype(jnp.int32)
    rank = jnp.take_along_axis(jnp.cumsum(onehot, axis=0), e_flat[:, None], axis=1)[:, 0] - 1
    counts = jnp.sum(onehot, axis=0)
    padded = (counts + ROW_BLOCK - 1) // ROW_BLOCK * ROW_BLOCK
    end_pad = jnp.cumsum(padded)
    dest = (end_pad - padded)[e_flat] + rank
    n_blk = -(-A // ROW_BLOCK) + N_EXPERTS
    A_pad = n_blk * ROW_BLOCK
    tok_pad = jnp.zeros((A_pad,), jnp.int32).at[dest].set(jnp.arange(A, dtype=jnp.int32) // TOP_K)
    w_pad = jnp.zeros((A_pad,), f32).at[dest].set(weights.reshape(A))
    blk_expert = jnp.minimum(jnp.searchsorted(end_pad, jnp.arange(n_blk) * ROW_BLOCK, side='right'),
                             N_EXPERTS - 1).astype(jnp.int32)
    n_used = (end_pad[-1:] // ROW_BLOCK).astype(jnp.int32)
    x_pad = xb[tok_pad]
    y_pad = moe_experts(x_pad, w_pad, blk_expert, n_used, w_gate, w_up, w_down, layer)
    rows = dest.reshape(M, TOP_K)
    return y_pad[rows[:, 0]] + y_pad[rows[:, 1]]


def _rope_tables(S, theta, rot_dims, width):
    half = rot_dims // 2
    freqs = theta ** (-np.arange(half, dtype=np.float64) / half)
    ang = np.arange(S, dtype=np.float64)[:, None] * freqs[None, :]
    cos = np.ones((S, width), np.float64)
    sin_lo = np.zeros((S, width), np.float64)
    sin_hi = np.zeros((S, width), np.float64)
    cos[:, :half] = np.cos(ang)
    cos[:, half:rot_dims] = np.cos(ang)
    sin_lo[:, :half] = -np.sin(ang)
    sin_hi[:, half:rot_dims] = np.sin(ang)
    return [jnp.asarray(t, f32) for t in (cos, sin_lo, sin_hi)]


def _rope128(x, cos, sin_lo, sin_hi, half):
    return x * cos + pltpu.roll(x, LANES - half, axis=1) * sin_lo + pltpu.roll(x, half, axis=1) * sin_hi


def _nsa_prep_body(q_ref, kv_ref, cos_ref, sl_ref, sh_ref, qo_ref, kvo_ref, co_ref):
    cos, sl, sh = cos_ref[...], sl_ref[...], sh_ref[...]
    half = ROPE_DIMS // 2
    scale = HEAD_DIM ** -0.5
    for h in range(NSA_HEADS):
        c = slice(h * LANES, (h + 1) * LANES)
        qo_ref[:, c] = (_rope128(q_ref[:, c], cos, sl, sh, half) * scale).astype(bf16)
    for j in range(6):
        for g in range(NSA_KV_HEADS):
            c = slice((j * NSA_KV_HEADS + g) * LANES, (j * NSA_KV_HEADS + g + 1) * LANES)
            x = kv_ref[:, c]
            if j % 2 == 0:
                x = _rope128(x, cos, sl, sh, half)
            if j < 2:
                co_ref[:, c] = x.astype(bf16)
            else:
                kvo_ref[:, slice(c.start - 4 * LANES, c.stop - 4 * LANES)] = x.astype(bf16)


def nsa_prep(proj, S, tm=512):
    M = proj.shape[0]
    assert M % tm == 0 and S % tm == 0
    tabs = _rope_tables(S, ROPE_THETA, ROPE_DIMS, LANES)
    tspec = pl.BlockSpec((tm, LANES), lambda i: (i % (S // tm), 0))
    return pl.pallas_call(
        _nsa_prep_body,
        grid=(M // tm,),
        in_specs=[pl.BlockSpec((tm, D_GRP), lambda i: (i, OFF_NSA_Q // D_GRP)),
                  pl.BlockSpec((tm, 6 * NSA_KV_W), lambda i: (i, OFF_NSA_KV // (6 * NSA_KV_W))),
                  tspec, tspec, tspec],
        out_specs=[pl.BlockSpec((tm, D_GRP), lambda i: (i, 0)),
                   pl.BlockSpec((tm, 4 * NSA_KV_W), lambda i: (i, 0)),
                   pl.BlockSpec((tm, 2 * NSA_KV_W), lambda i: (i, 0))],
        out_shape=[jax.ShapeDtypeStruct((M, D_GRP), bf16), jax.ShapeDtypeStruct((M, 4 * NSA_KV_W), bf16),
                   jax.ShapeDtypeStruct((M, 2 * NSA_KV_W), bf16)],
        compiler_params=pltpu.CompilerParams(dimension_semantics=("parallel",), vmem_limit_bytes=VMEM_LIMIT),
        name="nsa_prep",
    )(proj, proj, *tabs)


def _nsa_compress_body(r_ref, w1_ref, w2_ref, pe_ref, o_ref):
    r = r_ref[0, 0, 0]
    half_w = CMP_STRIDE * HEAD_DIM
    a = jnp.dot(r, w1_ref[0, :half_w, :], preferred_element_type=f32)
    b = jnp.dot(r, w1_ref[0, half_w:, :], preferred_element_type=f32)
    n = r.shape[0]
    pe = jnp.dot(jnp.broadcast_to(pe_ref[0], (SUB, pe_ref.shape[-1])), w1_ref[0], preferred_element_type=f32)[0:1]
    pre = a + pltpu.roll(b, n - 1, axis=0) + pe
    hid = pre * jax.nn.sigmoid(pre)
    o_ref[0, 0, 0] = jnp.dot(hid.astype(bf16), w2_ref[0], preferred_element_type=f32).astype(bf16)


def nsa_compress(c_rot, w1s, w2s, pes, B, S):
    G, Dh = NSA_KV_HEADS, HEAD_DIM
    nc = S // CMP_STRIDE
    r = c_rot.reshape(B, nc, CMP_STRIDE, 2, G, Dh).transpose(3, 0, 4, 1, 2, 5).reshape(2, B, G, nc, CMP_STRIDE * Dh)
    w1 = jnp.stack(w1s).astype(bf16)
    w2 = jnp.stack(w2s).astype(bf16)
    pe = jnp.stack(pes).reshape(2, 1, CMP_LEN * Dh).astype(bf16)
    return pl.pallas_call(
        _nsa_compress_body,
        grid=(2, B, G),
        in_specs=[pl.BlockSpec((1, 1, 1, nc, CMP_STRIDE * Dh), lambda j, b, g: (j, b, g, 0, 0)),
                  pl.BlockSpec((1, CMP_LEN * Dh, Dh), lambda j, b, g: (j, 0, 0)),
                  pl.BlockSpec((1, Dh, Dh), lambda j, b, g: (j, 0, 0)),
                  pl.BlockSpec((1, 1, CMP_LEN * Dh), lambda j, b, g: (j, 0, 0))],
        out_specs=pl.BlockSpec((1, 1, 1, nc, Dh), lambda j, b, g: (j, b, g, 0, 0)),
        out_shape=jax.ShapeDtypeStruct((2, B, G, nc, Dh), bf16),
        compiler_params=pltpu.CompilerParams(dimension_semantics=("parallel", "parallel", "parallel"),
                                             vmem_limit_bytes=VMEM_LIMIT),
        name="nsa_compress",
    )(r, w1, w2, pe)


def _softmax_rows(s, mask):
    s = jnp.where(mask, s, -jnp.inf)
    m = jnp.max(s, axis=-1, keepdims=True)
    m = jnp.where(m == -jnp.inf, 0.0, m)
    e = jnp.exp(s - m)
    return e, jnp.maximum(jnp.sum(e, axis=-1, keepdims=True), jnp.finfo(f32).tiny)


def _nsa_attn_body(q_ref, cmp_ref, ks_ref, vs_ref, kw_ref, vw_ref, gate_ref, ovt_ref, o_ref, *, tq, tk, S):
    g = pl.program_id(1)
    qi = pl.program_id(2)
    R, Dh = NSA_GROUP, HEAD_DIM
    rows = R * tq
    n_blk = S // SEL_LEN
    n_cmp = S // CMP_STRIDE - 1
    q0 = qi * tq
    q4 = jnp.concatenate([q_ref[0, :, r * Dh:(r + 1) * Dh] for r in range(R)], axis=0)
    nt = (((1,), (1,)), ((), ()))

    def qpos(shape):
        return q0 + lax.broadcasted_iota(jnp.int32, shape, 0) % tq

    k_cmp, v_cmp = cmp_ref[0, 0, 0], cmp_ref[1, 0, 0]
    ncp = k_cmp.shape[0]
    s = lax.dot_general(q4, k_cmp, nt, preferred_element_type=f32)
    n_idx = lax.broadcasted_iota(jnp.int32, (rows, ncp), 1)
    mask = (n_idx * CMP_STRIDE + (CMP_LEN - 1) <= qpos((rows, ncp))) & (n_idx < n_cmp)
    e, den = _softmax_rows(s, mask)
    p = e / den
    o_cmp = jnp.dot(p.astype(bf16), v_cmp, preferred_element_type=f32)

    psum = p[0:tq]
    for r in range(1, R):
        psum = psum + p[r * tq:(r + 1) * tq]
    ph, plo = _split2(psum)
    imp = (lax.dot_general(ovt_ref[...], ph, nt, preferred_element_type=f32)
           + lax.dot_general(ovt_ref[...], plo, nt, preferred_element_type=f32))
    blk = lax.broadcasted_iota(jnp.int32, (n_blk, tq), 0)
    cur = (q0 + lax.broadcasted_iota(jnp.int32, (n_blk, tq), 1)) // SEL_LEN
    visible = blk <= cur
    forced = (blk == 0) | (visible & (blk > cur - N_LOCAL_FORCED))
    imp = jnp.where(forced, FORCED_SCORE, jnp.where(visible, imp, -1.0))
    sel_t = jnp.zeros((n_blk, tq), f32)
    blk_f = blk.astype(f32)
    for _ in range(min(N_SEL, n_blk)):
        m = jnp.max(imp, axis=0, keepdims=True)
        first = jnp.min(jnp.where(imp == m, blk_f, float(n_blk)), axis=0, keepdims=True)
        hit = blk_f == first
        sel_t = jnp.where(hit, 1.0, sel_t)
        imp = jnp.where(hit, -jnp.inf, imp)
    sel4 = jnp.concatenate([sel_t.T.astype(bf16)] * R, axis=0)

    blocks_per_tile = tk // SEL_LEN

    def sel_step(kt, carry):
        m_run, l_run, acc = carry
        k0 = pl.multiple_of(kt * tk, tk)
        kt_k = ks_ref[0, pl.ds(k0, tk), :]
        kt_v = vs_ref[0, pl.ds(k0, tk), :]
        s = lax.dot_general(q4, kt_k, nt, preferred_element_type=f32)
        expand = (lax.broadcasted_iota(jnp.int32, (n_blk, tk), 0)
                  == kt * blocks_per_tile + lax.broadcasted_iota(jnp.int32, (n_blk, tk), 1) // SEL_LEN)
        chosen = jnp.dot(sel4, expand.astype(bf16), preferred_element_type=f32)
        kpos = k0 + lax.broadcasted_iota(jnp.int32, (rows, tk), 1)
        mask = (chosen > 0.5) & (kpos <= qpos((rows, tk)))
        s = jnp.where(mask, s, -jnp.inf)
        m_new = jnp.maximum(m_run, jnp.max(s, axis=-1, keepdims=True))
        m_safe = jnp.where(m_new == -jnp.inf, 0.0, m_new)
        alpha = jnp.exp(m_run - m_safe)
        e = jnp.exp(s - m_safe)
        l_new = alpha * l_run + jnp.sum(e, axis=-1, keepdims=True)
        acc = alpha * acc + jnp.dot(e.astype(bf16), kt_v, preferred_element_type=f32)
        return m_new, l_new, acc

    n_kt = (q0 + tq - 1) // tk + 1
    init = (jnp.full((rows, 1), -jnp.inf, f32), jnp.zeros((rows, 1), f32), jnp.zeros((rows, Dh), f32))
    _, l_sel, acc_sel = lax.fori_loop(0, n_kt, sel_step, init)
    o_sel = acc_sel / jnp.maximum(l_sel, jnp.finfo(f32).tiny)

    span = WINDOW + tq
    w0 = pl.multiple_of(jnp.maximum(q0 - WINDOW, 0), tq)
    kwin = kw_ref[0, pl.ds(w0, span), :]
    vwin = vw_ref[0, pl.ds(w0, span), :]
    s = lax.dot_general(q4, kwin, nt, preferred_element_type=f32)
    kpos = w0 + lax.broadcasted_iota(jnp.int32, (rows, span), 1)
    qp = qpos((rows, span))
    e, den = _softmax_rows(s, (kpos <= qp) & (kpos > qp - WINDOW))
    o_win = jnp.dot(e.astype(bf16), vwin, preferred_element_type=f32) / den

    gate = jax.nn.sigmoid(gate_ref[0])
    for r in range(R):
        c0 = (g * R + r) * 3
        lane = lax.broadcasted_iota(jnp.int32, (tq, LANES), 1)
        gsel = lambda c: jnp.sum(jnp.where(lane == c0 + c, gate, 0.0), axis=-1, keepdims=True)
        rs = slice(r * tq, (r + 1) * tq)
        o_ref[0, :, r * Dh:(r + 1) * Dh] = (gsel(0) * o_cmp[rs] + gsel(1) * o_sel[rs]
                                            + gsel(2) * o_win[rs]).astype(o_ref.dtype)


def nsa_attention(q_rot, kv_rot, cmp_kv, proj3, B, S, tq=256, tk=512):
    G, R, Dh = NSA_KV_HEADS, NSA_GROUP, HEAD_DIM
    n_blk = S // SEL_LEN
    assert S % tk == 0 and tk % tq == 0 and S >= WINDOW + tq and n_blk % SUB == 0
    nc = S // CMP_STRIDE
    c0 = np.arange(nc - 1)[None, :] * CMP_STRIDE
    s0 = np.arange(n_blk)[:, None] * SEL_LEN
    ovt = np.zeros((n_blk, nc), np.float32)
    ovt[:, :nc - 1] = np.clip(np.minimum(c0 + CMP_LEN, s0 + SEL_LEN) - np.maximum(c0, s0), 0, None) / CMP_LEN
    kvspec = lambda j: pl.BlockSpec((1, S, Dh), lambda b, g, i: (b, 0, j * G + g))
    return pl.pallas_call(
        functools.partial(_nsa_attn_body, tq=tq, tk=tk, S=S),
        grid=(B, G, S // tq),
        in_specs=[pl.BlockSpec((1, tq, R * Dh), lambda b, g, i: (b, i, g)),
                  pl.BlockSpec((2, 1, 1, nc, Dh), lambda b, g, i: (0, b, g, 0, 0)),
                  kvspec(0), kvspec(1), kvspec(2), kvspec(3),
                  pl.BlockSpec((1, tq, LANES), lambda b, g, i: (b, i, OFF_SMALL // LANES)),
                  pl.BlockSpec((n_blk, nc), lambda b, g, i: (0, 0))],
        out_specs=pl.BlockSpec((1, tq, R * Dh), lambda b, g, i: (b, i, g)),
        out_shape=jax.ShapeDtypeStruct((B, S, D_GRP), bf16),
        compiler_params=pltpu.CompilerParams(dimension_semantics=("parallel", "parallel", "arbitrary"),
                                             vmem_limit_bytes=VMEM_LIMIT),
        name="nsa_attention",
    )(q_rot, cmp_kv, kv_rot, kv_rot, kv_rot, kv_rot, proj3, jnp.asarray(ovt, bf16))


def nsa_mixer_pallas(proj, ck_w1, ck_w2, ck_pe, cv_w1, cv_w2, cv_pe, B, S):
    q_rot, kv_rot, c_rot = nsa_prep(proj, S)
    cmp_kv = nsa_compress(c_rot, (ck_w1, cv_w1), (ck_w2, cv_w2), (ck_pe, cv_pe), B, S)
    return nsa_attention(q_rot.reshape(B, S, D_GRP), kv_rot.reshape(B, S, 4 * NSA_KV_W), cmp_kv,
                         proj.reshape(B, S, P_PAD), B, S)


def _retention_body(q_ref, k_ref, v_ref, g_ref, cos_ref, sin_ref, dec_ref, xi_ref, zeta_ref, gc_ref, o_ref, st_ref,
                    *, T):
    @pl.when(pl.program_id(2) == 0)
    def _():
        st_ref[...] = jnp.zeros_like(st_ref)

    C, Dh = RET_CHUNK, HEAD_DIM
    nt = (((1,), (1,)), ((), ()))
    decay_in, xi, zeta, gamma_c = dec_ref[0], xi_ref[0], zeta_ref[0], gc_ref[0, 0:1, :]
    state = st_ref[...]
    for c in range(T // C):
        rs = slice(c * C, (c + 1) * C)
        cos, sin = cos_ref[rs, :], sin_ref[rs, :]
        rope = lambda x: x * cos + pltpu.roll(x, Dh // 2, axis=1) * sin
        q = rope(q_ref[0, rs, :])
        k = rope(k_ref[0, rs, :]) * Dh ** -0.5
        qb, vb = q.astype(bf16), v_ref[0, rs, :].astype(bf16)
        inner = lax.dot_general(qb, k.astype(bf16), nt, preferred_element_type=f32) * decay_in
        o = (jnp.dot(inner.astype(bf16), vb, preferred_element_type=f32)
             + jnp.dot(qb, state.astype(bf16), preferred_element_type=f32) * xi)
        kz = (k * zeta).astype(bf16)
        state = state * gamma_c + jnp.dot(kz.T, vb, preferred_element_type=f32)
        mu = jnp.mean(o, axis=-1, keepdims=True)
        d = o - mu
        o = d * lax.rsqrt(jnp.mean(d * d, axis=-1, keepdims=True) + NORM_EPS)
        gt = g_ref[0, rs, :]
        o_ref[0, rs, :] = (gt * jax.nn.sigmoid(gt) * o).astype(o_ref.dtype)
    st_ref[...] = state


def retention_mixer_pallas(proj3, B, S, T=512):
    H, Dh, C = RET_HEADS, HEAD_DIM, RET_CHUNK
    assert S % T == 0 and T % C == 0
    freqs = RET_THETA ** (-np.linspace(0.0, 1.0, Dh // 2))
    ang = np.arange(S, dtype=np.float64)[:, None] * freqs[None, :]
    cos = jnp.asarray(np.concatenate([np.cos(ang), np.cos(ang)], 1), f32)
    sin = jnp.asarray(np.concatenate([-np.sin(ang), np.sin(ang)], 1), f32)
    log_gamma = np.log1p(-(2.0 ** (-5.0 - np.arange(H, dtype=np.float64))))
    n = np.arange(C, dtype=np.float64)
    diff = n[:, None] - n[None, :]
    decay_in = np.where(diff >= 0, np.exp(log_gamma[:, None, None] * np.maximum(diff, 0.0)), 0.0)
    xi = np.exp(log_gamma[:, None] * (n + 1.0))[:, :, None] * np.ones((1, 1, Dh))
    zeta = np.exp(log_gamma[:, None] * (C - 1.0 - n))[:, :, None] * np.ones((1, 1, Dh))
    gamma_c = np.exp(log_gamma * C)[:, None, None] * np.ones((1, SUB, Dh))
    blk = lambda j: pl.BlockSpec((1, T, Dh), lambda b, h, t: (b, t, (OFF_RET + j * D_GRP) // Dh + h))
    tab = pl.BlockSpec((T, Dh), lambda b, h, t: (t, 0))
    per_head = lambda r: pl.BlockSpec((1, r, Dh), lambda b, h, t: (h, 0, 0))
    return pl.pallas_call(
        functools.partial(_retention_body, T=T),
        grid=(B, H, S // T),
        in_specs=[blk(0), blk(1), blk(2), blk(3), tab, tab, per_head(C), per_head(C), per_head(C), per_head(SUB)],
        out_specs=pl.BlockSpec((1, T, Dh), lambda b, h, t: (b, t, h)),
        out_shape=jax.ShapeDtypeStruct((B, S, D_GRP), bf16),
        scratch_shapes=[pltpu.VMEM((Dh, Dh), f32)],
        compiler_params=pltpu.CompilerParams(dimension_semantics=("parallel", "parallel", "arbitrary"),
                                             vmem_limit_bytes=VMEM_LIMIT),
        name="retention",
    )(proj3, proj3, proj3, proj3, cos, sin, jnp.asarray(decay_in, f32), jnp.asarray(xi, f32),
      jnp.asarray(zeta, f32), jnp.asarray(gamma_c, f32))


def _dot3(a, b):
    ah, al = _split2(a)
    bh, bl = _split2(b)
    return (jnp.dot(ah, bh, preferred_element_type=f32) + jnp.dot(ah, bl, preferred_element_type=f32)
            + jnp.dot(al, bh, preferred_element_type=f32))


def _deltanet_body(q_ref, k_ref, v_ref, z_ref, qp_ref, kp_ref, vp_ref, ab_ref, cw_ref, alog_ref, dtb_ref, ng_ref,
                   o_ref, st_ref, *, T):
    h = pl.program_id(1)
    first = pl.program_id(2) == 0

    @pl.when(first)
    def _():
        st_ref[...] = jnp.zeros_like(st_ref)

    C, Dh = DN_CHUNK, HEAD_DIM
    nt = (((1,), (1,)), ((), ()))

    def conv_silu(x_ref, prev_ref, j):
        prev = jnp.where(first, 0.0, prev_ref[0])
        xcat = jnp.concatenate([prev, x_ref[0]], axis=0)
        w = cw_ref[:, j, :]
        y = sum(w[i:i + 1, :] * xcat[SUB - (CONV_WIDTH - 1) + i:SUB - (CONV_WIDTH - 1) + i + T, :]
                for i in range(CONV_WIDTH))
        return y * jax.nn.sigmoid(y)

    def l2n(x):
        return x * lax.rsqrt(jnp.sum(x * x, axis=-1, keepdims=True) + 1e-6)

    q_all = l2n(conv_silu(q_ref, qp_ref, 0)) * Dh ** -0.5
    k_all = l2n(conv_silu(k_ref, kp_ref, 1))
    v_all = conv_silu(v_ref, vp_ref, 2)
    ab = ab_ref[0]
    lane = lax.broadcasted_iota(jnp.int32, ab.shape, 1)
    a_col = jnp.sum(jnp.where(lane == 3 * NSA_HEADS + h, ab, 0.0), axis=-1, keepdims=True)
    b_col = jnp.sum(jnp.where(lane == 3 * NSA_HEADS + DN_HEADS + h, ab, 0.0), axis=-1, keepdims=True)
    beta_all = jax.nn.sigmoid(b_col)
    g_all = -jnp.exp(alog_ref[0, 0:1, :]) * jax.nn.softplus(a_col + dtb_ref[0, 0:1, :])

    ri = lax.broadcasted_iota(jnp.int32, (C, C), 0)
    ci = lax.broadcasted_iota(jnp.int32, (C, C), 1)
    tril, strict = ri >= ci, ri > ci
    ltri = tril.astype(bf16)
    eye = (ri == ci).astype(f32)
    chunks = [slice(c * C, (c + 1) * C) for c in range(T // C)]
    qs, ks_, vs_ = [q_all[rs] for rs in chunks], [k_all[rs] for rs in chunks], [v_all[rs] for rs in chunks]
    betas = [beta_all[rs] for rs in chunks]
    gcs = []
    for rs in chunks:
        gh, gm, gl = _split3(g_all[rs])
        gcs.append(jnp.dot(ltri, gh, preferred_element_type=f32) + jnp.dot(ltri, gm, preferred_element_type=f32)
                   + jnp.dot(ltri, gl, preferred_element_type=f32))
    decays = [jnp.exp(jnp.where(tril, gc[:, :C] - gc[:, :C].T, -jnp.inf)) for gc in gcs]
    kbs = [k.astype(bf16) for k in ks_]
    kbetas = [k * b for k, b in zip(ks_, betas)]
    ms = [jnp.where(strict, lax.dot_general(kbeta.astype(bf16), kb, nt, preferred_element_type=f32) * d, 0.0)
          for kbeta, kb, d in zip(kbetas, kbs, decays)]
    attns = [jnp.where(tril, lax.dot_general(q.astype(bf16), kb, nt, preferred_element_type=f32) * d, 0.0).astype(bf16)
             for q, kb, d in zip(qs, kbs, decays)]
    t_invs, pws = [eye - m for m in ms], ms
    for _ in range(5):
        pws = [_dot3(pw, pw) for pw in pws]
        t_invs = [_dot3(t, eye + pw) for t, pw in zip(t_invs, pws)]
    egcs = [jnp.exp(gc) for gc in gcs]
    tbs = [t.astype(bf16) for t in t_invs]
    us = [jnp.dot(tb, (v * b).astype(bf16), preferred_element_type=f32) for tb, v, b in zip(tbs, vs_, betas)]
    ws = [jnp.dot(tb, (kbeta * egc).astype(bf16), preferred_element_type=f32).astype(bf16)
          for tb, kbeta, egc in zip(tbs, kbetas, egcs)]
    state = st_ref[...]
    for c, rs in enumerate(chunks):
        g_last = gcs[c][C - 1:C, :]
        q_dec = (qs[c] * egcs[c]).astype(bf16)
        k_dec = (ks_[c] * jnp.exp(g_last - gcs[c])).astype(bf16)
        sb = state.astype(bf16)
        v_new = us[c] - jnp.dot(ws[c], sb, preferred_element_type=f32)
        vnb = v_new.astype(bf16)
        o = jnp.dot(q_dec, sb, preferred_element_type=f32) + jnp.dot(attns[c], vnb, preferred_element_type=f32)
        state = state * jnp.exp(g_last) + jnp.dot(k_dec.T, vnb, preferred_element_type=f32)
        o = o * lax.rsqrt(jnp.mean(o * o, axis=-1, keepdims=True) + NORM_EPS) * ng_ref[...]
        zt = z_ref[0, rs, :]
        o_ref[0, rs, :] = (o * (zt * jax.nn.sigmoid(zt))).astype(o_ref.dtype)
    st_ref[...] = state


def deltanet_mixer_pallas(proj3, conv_w, a_log, dt_bias, norm_g, B, S, T=256):
    H, Dh = DN_HEADS, HEAD_DIM
    assert S % T == 0 and T % DN_CHUNK == 0
    blk = lambda j: pl.BlockSpec((1, T, Dh), lambda b, h, t: (b, t, (OFF_DN + j * D_GRP) // Dh + h))
    prev = lambda j: pl.BlockSpec((1, SUB, Dh), lambda b, h, t: (b, jnp.maximum(t * (T // SUB) - 1, 0),
                                                                (OFF_DN + j * D_GRP) // Dh + h))
    per_head = pl.BlockSpec((1, SUB, Dh), lambda b, h, t: (h, 0, 0))
    bcast = lambda x: jnp.broadcast_to(x.astype(f32)[:, None, None], (H, SUB, Dh))
    return pl.pallas_call(
        functools.partial(_deltanet_body, T=T),
        grid=(B, H, S // T),
        in_specs=[blk(0), blk(1), blk(2), blk(3), prev(0), prev(1), prev(2),
                  pl.BlockSpec((1, T, LANES), lambda b, h, t: (b, t, OFF_SMALL // LANES)),
                  pl.BlockSpec((CONV_WIDTH, 3, Dh), lambda b, h, t: (0, 0, h)),
                  per_head, per_head,
                  pl.BlockSpec((1, Dh), lambda b, h, t: (0, 0))],
        out_specs=pl.BlockSpec((1, T, Dh), lambda b, h, t: (b, t, h)),
        out_shape=jax.ShapeDtypeStruct((B, S, D_GRP), bf16),
        scratch_shapes=[pltpu.VMEM((Dh, Dh), f32)],
        compiler_params=pltpu.CompilerParams(dimension_semantics=("parallel", "parallel", "arbitrary"),
                                             vmem_limit_bytes=VMEM_LIMIT),
        name="deltanet",
    )(proj3, proj3, proj3, proj3, proj3, proj3, proj3, proj3, conv_w.reshape(CONV_WIDTH, 3, D_GRP),
      bcast(a_log), bcast(dt_bias), norm_g.reshape(1, Dh))


def _w_in_pieces():
    n0, r0, d0, w0 = 0, NSA_COLS, NSA_COLS + RET_COLS, NSA_COLS + RET_COLS + DN_COLS
    n_gate = 3 * NSA_HEADS
    return [
        (OFF_RW, w0, 3 * D_GRP),
        (OFF_RET, r0, 4 * D_GRP),
        (OFF_DN, d0, 3 * D_GRP),
        (OFF_DN + 3 * D_GRP, d0 + 3 * D_GRP + 2 * DN_HEADS, D_GRP),
        (OFF_NSA_Q, n0, D_GRP + 6 * NSA_KV_W),
        (OFF_LORA, w0 + 3 * D_GRP, RW_LORA),
        (OFF_SMALL, n0 + D_GRP + 6 * NSA_KV_W, n_gate),
        (OFF_SMALL + n_gate, d0 + 3 * D_GRP, 2 * DN_HEADS),
    ]


def _pack_w_in_body(w_ref, o_ref):
    rows = o_ref.shape[0]
    for off, width in ((OFF_LORA + RW_LORA, RW_LORA_PAD - RW_LORA), (OFF_SMALL, LANES)):
        o_ref[:, off:off + width] = jnp.zeros((rows, width), o_ref.dtype)
    for dst, src, width in _w_in_pieces():
        o_ref[:, dst:dst + width] = w_ref[0, :, src:src + width].astype(o_ref.dtype)


def pack_w_in(w_in, layer, tr=128):
    K = w_in.shape[1]
    return pl.pallas_call(
        _pack_w_in_body,
        grid=(K // tr,),
        in_specs=[pl.BlockSpec((1, tr, P_TOTAL), lambda i: (layer, i, 0))],
        out_specs=pl.BlockSpec((tr, P_PAD), lambda i: (i, 0)),
        out_shape=jax.ShapeDtypeStruct((K, P_PAD), bf16),
        compiler_params=pltpu.CompilerParams(dimension_semantics=("parallel",), vmem_limit_bytes=VMEM_LIMIT),
        name="pack_w_in",
    )(w_in)


def hybrid_mixer(h, hn, B, S, layer, w_in, w_out, ck_w1, ck_w2, ck_pe, cv_w1, cv_w2, cv_pe, dn_conv, dn_a_log,
                 dn_dt_bias, dn_norm, rw_mu, rw_w0, rw_w2, rw_a0, rw_a2, rw_g2, rw_kk, rw_ka, rw_rk, rw_ln_g, rw_ln_b):
    M = B * S
    proj = matmul(hn, pack_w_in(w_in, layer), name="in_proj")
    p3 = proj.reshape(B, S, P_PAD)
    o_nsa = nsa_mixer_pallas(proj, ck_w1, ck_w2, ck_pe, cv_w1, cv_w2, cv_pe, B, S)
    o_ret = retention_mixer_pallas(p3, B, S)
    o_dn = deltanet_mixer_pallas(p3, dn_conv, dn_a_log, dn_dt_bias, dn_norm, B, S, T=512)
    o_rw = rwkv7_mixer(proj, OFF_RW // (3 * D_GRP), OFF_LORA // RW_LORA_PAD, rw_mu, rw_w0, rw_w2, rw_a0, rw_a2,
                       rw_g2, rw_kk, rw_ka, rw_rk, rw_ln_g, rw_ln_b, B, S)
    parts = [o.reshape(M, D_GRP) for o in (o_nsa, o_ret, o_dn, o_rw)]
    return out_proj_residual(parts, w_out.astype(bf16), h)


def kernel(x, p, norm_mix, w_in, w_out, nsa_ck_w1, nsa_ck_w2, nsa_ck_pe, nsa_cv_w1, nsa_cv_w2, nsa_cv_pe, dn_conv, dn_a_log, dn_dt_bias, dn_norm, rw_mu, rw_w0, rw_w2, rw_a0, rw_a2, rw_g2, rw_kk, rw_ka, rw_rk, rw_ln_g, rw_ln_b, norm_moe, moe_router_grp, moe_router_grp_b, moe_router_exp, moe_router_exp_b, moe_w_gate, moe_w_up, moe_w_down, norm_ple, ple_w, ple_gate, norm_final):
    B, S, D = x.shape
    M = B * S
    h = x.reshape(M, D)
    for i in range(DEPTH):
        hn = rms_norm_rows(h, norm_mix[i], bf16)
        h = hybrid_mixer(h, hn, B, S, i, w_in, w_out[i], nsa_ck_w1[i], nsa_ck_w2[i], nsa_ck_pe[i], nsa_cv_w1[i],
                         nsa_cv_w2[i], nsa_cv_pe[i], dn_conv[i], dn_a_log[i], dn_dt_bias[i], dn_norm[i],
                         rw_mu[i], rw_w0[i], rw_w2[i], rw_a0[i], rw_a2[i], rw_g2[i], rw_kk[i], rw_ka[i],
                         rw_rk[i], rw_ln_g[i], rw_ln_b[i])
        hn = rms_norm_rows(h, norm_moe[i], bf16)
        h = h + hier_moe(hn, moe_router_grp[i], moe_router_grp_b[i], moe_router_exp[i], moe_router_exp_b[i],
                         moe_w_gate, moe_w_up, moe_w_down, i)
        hn = rms_norm_rows(h, norm_ple[i], bf16)
        h = ple_residual(hn, ple_gate[i].astype(bf16), p[i].reshape(M, PLE_DIM), ple_w[i].astype(bf16), h)
    return rms_norm_rows(h, norm_final, f32).reshape(B, S, D)
```

```python
import functools

import jax
import jax.numpy as jnp
import numpy as np
from jax import lax
from jax.experimental import pallas as pl
from jax.experimental.pallas import tpu as pltpu

D_MODEL = 4096
BATCH = 2
SEQ = 4096
DEPTH = 2
f32 = jnp.float32
bf16 = jnp.bfloat16
D_MIX = D_MODEL
D_GRP = D_MIX // 4
HEAD_DIM = 128
NORM_EPS = 1e-6
NSA_HEADS = D_GRP // HEAD_DIM
NSA_KV_HEADS = 2
NSA_GROUP = NSA_HEADS // NSA_KV_HEADS
NSA_KV_W = NSA_KV_HEADS * HEAD_DIM
CMP_LEN = 32
CMP_STRIDE = 16
SEL_LEN = 64
N_SEL = 16
N_LOCAL_FORCED = 2
FORCED_SCORE = 1e4
WINDOW = 512
SEL_QBLK = 64
WIN_QBLK = 128
ROPE_THETA = 500000.0
ROPE_DIMS = HEAD_DIM // 4
RET_HEADS = D_GRP // HEAD_DIM
RET_CHUNK = 128
RET_THETA = 10000.0
DN_HEADS = D_GRP // HEAD_DIM
DN_CHUNK = 64
CONV_WIDTH = 4
RW_HEAD_DIM = 64
RW_HEADS = D_GRP // RW_HEAD_DIM
RW_DECAY_LORA = 64
RW_AAA_LORA = 64
RW_GATE_LORA = 160
RW_LN_EPS = 64e-5
N_GROUPS = 4
EXPERTS_PER_GROUP = 8
N_EXPERTS = N_GROUPS * EXPERTS_PER_GROUP
TOP_K = 2
D_EXPERT = 768
ROW_BLOCK = 256
PLE_DIM = 256
NSA_COLS = D_GRP + 6 * NSA_KV_W + 3 * NSA_HEADS
RET_COLS = 4 * D_GRP
DN_COLS = 4 * D_GRP + 2 * DN_HEADS
RW_COLS = 3 * D_GRP + RW_DECAY_LORA + RW_AAA_LORA + RW_GATE_LORA
P_TOTAL = NSA_COLS + RET_COLS + DN_COLS + RW_COLS

RW_LORA = RW_DECAY_LORA + RW_AAA_LORA + RW_GATE_LORA
RW_LORA_PAD = 384
RW_GROUPS_PER_BODY = 4
LANES = 128
SUB = 8

OFF_RW = 0
OFF_RET = 3 * D_GRP
OFF_DN = OFF_RET + 4 * D_GRP
OFF_NSA_Q = OFF_DN + 4 * D_GRP
OFF_NSA_KV = OFF_NSA_Q + D_GRP
OFF_LORA = OFF_NSA_KV + 6 * NSA_KV_W
OFF_SMALL = OFF_LORA + RW_LORA_PAD
P_PAD = OFF_SMALL + LANES
assert P_PAD == 14336 and OFF_LORA % RW_LORA_PAD == 0

VMEM_LIMIT = 56 * 1024 * 1024


def split_cols(x, sizes):
    return jnp.split(x, np.cumsum(sizes)[:-1].tolist(), axis=-1)


def rms_norm(x, g, eps=NORM_EPS):
    xf = x.astype(f32)
    y = xf * lax.rsqrt(jnp.mean(xf * xf, -1, keepdims=True) + eps)
    return (y * g.astype(f32)).astype(x.dtype)


def l2norm(x, eps=1e-6):
    return x * lax.rsqrt(jnp.sum(x * x, -1, keepdims=True) + eps)


def head_layer_norm(x, eps):
    mu = jnp.mean(x, -1, keepdims=True)
    return (x - mu) * lax.rsqrt(jnp.var(x, -1, keepdims=True) + eps)


def masked_softmax(s, mask):
    s = jnp.where(mask, s.astype(f32), -jnp.inf)
    m = jnp.max(s, -1, keepdims=True)
    m = jnp.where(jnp.isfinite(m), m, 0.0)
    e = jnp.where(mask, jnp.exp(s - m), 0.0)
    return e / jnp.maximum(jnp.sum(e, -1, keepdims=True), jnp.finfo(f32).tiny)


def rotary(x, freqs, rot_dims):
    S = x.shape[1]
    half = rot_dims // 2
    ang = jnp.arange(S, dtype=f32)[:, None] * freqs[None, :]
    ang = ang.reshape((S,) + (1,) * (x.ndim - 3) + (half,))
    cos, sin = jnp.cos(ang), jnp.sin(ang)
    x1 = x[..., :half].astype(f32)
    x2 = x[..., half:rot_dims].astype(f32)
    out = jnp.concatenate([x1 * cos - x2 * sin, x2 * cos + x1 * sin, x[..., rot_dims:].astype(f32)], -1)
    return out.astype(x.dtype)


def cmp_sel_overlap(S):
    n_cmp = (S - CMP_LEN) // CMP_STRIDE + 1
    c0 = np.arange(n_cmp)[:, None] * CMP_STRIDE
    s0 = np.arange(S // SEL_LEN)[None, :] * SEL_LEN
    ov = np.clip(np.minimum(c0 + CMP_LEN, s0 + SEL_LEN) - np.maximum(c0, s0), 0, None) / CMP_LEN
    return jnp.asarray(ov, f32)


def nsa_mixer(q, kc, vc, ks, vs, kw, vw, gate_logit, ck_w1, ck_w2, ck_pe, cv_w1, cv_w2, cv_pe):
    B, S, _ = q.shape
    G, R, Dh = NSA_KV_HEADS, NSA_GROUP, HEAD_DIM
    scale = Dh ** -0.5
    half = ROPE_DIMS // 2
    freqs = ROPE_THETA ** (-jnp.arange(half, dtype=f32) / half)
    q = rotary(q.reshape(B, S, G, R, Dh), freqs, ROPE_DIMS)
    kc, vc, ks, vs, kw, vw = [t.reshape(B, S, G, Dh) for t in (kc, vc, ks, vs, kw, vw)]
    kc, ks, kw = [rotary(t, freqs, ROPE_DIMS) for t in (kc, ks, kw)]
    pos = jnp.arange(S)
    n_cmp = (S - CMP_LEN) // CMP_STRIDE + 1
    blk_tok = np.arange(n_cmp)[:, None] * CMP_STRIDE + np.arange(CMP_LEN)[None, :]

    def compress(t, w1, w2, pe):
        blk = t[:, blk_tok] + pe[:, None, :]
        blk = blk.transpose(0, 3, 1, 2, 4).reshape(B, G, n_cmp, CMP_LEN * Dh)
        return jax.nn.silu(blk @ w1) @ w2

    k_cmp = compress(kc, ck_w1, ck_w2, ck_pe)
    v_cmp = compress(vc, cv_w1, cv_w2, cv_pe)
    s_cmp = jnp.einsum('bsgrd,bgnd->bgrsn', q, k_cmp) * scale
    cmp_mask = jnp.asarray(blk_tok[:, -1])[None, :] <= pos[:, None]
    p_cmp = masked_softmax(s_cmp, cmp_mask)
    o_cmp = jnp.einsum('bgrsn,bgnd->bsgrd', p_cmp, v_cmp.astype(f32))
    n_sel_blk = S // SEL_LEN
    k_top = min(N_SEL, n_sel_blk)
    imp = jnp.einsum('bgrsn,nj->bgsj', p_cmp, cmp_sel_overlap(S))
    blk = jnp.arange(n_sel_blk)[None, :]
    cur = (pos // SEL_LEN)[:, None]
    visible = blk <= cur
    forced = (blk == 0) | (visible & (blk > cur - N_LOCAL_FORCED))
    imp = jnp.where(forced, FORCED_SCORE, jnp.where(visible, imp, -1.0))
    sel_idx = lax.top_k(imp, k_top)[1]
    ks_blk = ks.reshape(B, n_sel_blk, SEL_LEN, G, Dh).transpose(0, 3, 1, 2, 4)
    vs_blk = vs.reshape(B, n_sel_blk, SEL_LEN, G, Dh).transpose(0, 3, 1, 2, 4)
    n_qb = S // SEL_QBLK
    q_b = q.reshape(B, n_qb, SEL_QBLK, G, R, Dh).transpose(1, 0, 2, 3, 4, 5)
    idx_b = sel_idx.reshape(B, G, n_qb, SEL_QBLK, k_top).transpose(2, 0, 1, 3, 4)
    qpos_b = pos.reshape(n_qb, SEL_QBLK)
    b_ix = jnp.arange(B)[:, None, None, None]
    g_ix = jnp.arange(G)[None, :, None, None]

    def sel_block(args):
        qb, ib, qp = args
        kb = ks_blk[b_ix, g_ix, ib].reshape(B, G, SEL_QBLK, k_top * SEL_LEN, Dh)
        vb = vs_blk[b_ix, g_ix, ib].reshape(B, G, SEL_QBLK, k_top * SEL_LEN, Dh)
        tok = (ib[..., None] * SEL_LEN + jnp.arange(SEL_LEN)).reshape(B, G, SEL_QBLK, k_top * SEL_LEN)
        s = jnp.einsum('bqgrd,bgqkd->bgrqk', qb, kb) * scale
        mask = tok[:, :, None] <= qp[None, None, None, :, None]
        pr = masked_softmax(s, mask)
        return jnp.einsum('bgrqk,bgqkd->bqgrd', pr, vb.astype(f32))

    o_sel = lax.map(sel_block, (q_b, idx_b, qpos_b))
    o_sel = o_sel.transpose(1, 0, 2, 3, 4, 5).reshape(B, S, G, R, Dh)
    n_wb = S // WIN_QBLK
    span = WINDOW + WIN_QBLK
    key_idx = np.arange(n_wb)[:, None] * WIN_QBLK + np.arange(span)[None, :]
    pad = ((0, 0), (WINDOW, 0), (0, 0), (0, 0))
    kb = jnp.pad(kw, pad)[:, key_idx]
    vb = jnp.pad(vw, pad)[:, key_idx]
    qb = q.reshape(B, n_wb, WIN_QBLK, G, R, Dh)
    s_win = jnp.einsum('bnqgrd,bnkgd->bgrnqk', qb, kb) * scale
    kpos = jnp.asarray(key_idx - WINDOW)[:, None, :]
    qpos = pos.reshape(n_wb, WIN_QBLK)[:, :, None]
    win_mask = (kpos >= 0) & (kpos <= qpos) & (kpos > qpos - WINDOW)
    p_win = masked_softmax(s_win, win_mask)
    o_win = jnp.einsum('bgrnqk,bnkgd->bnqgrd', p_win, vb.astype(f32)).reshape(B, S, G, R, Dh)
    gate = jax.nn.sigmoid(gate_logit.astype(f32)).reshape(B, S, G, R, 3)
    o = gate[..., 0:1] * o_cmp + gate[..., 1:2] * o_sel + gate[..., 2:3] * o_win
    return o.reshape(B, S, D_GRP)


def retention_mixer(q, k, v, g):
    B, S, _ = q.shape
    H, Dh, C = RET_HEADS, HEAD_DIM, RET_CHUNK
    n_c = S // C
    freqs = RET_THETA ** (-jnp.linspace(0.0, 1.0, Dh // 2, dtype=f32))
    heads = lambda t: t.reshape(B, S, H, Dh)
    q = rotary(heads(q), freqs, Dh).astype(f32)
    k = rotary(heads(k), freqs, Dh).astype(f32) * Dh ** -0.5
    v = heads(v).astype(f32)
    log_gamma = jnp.log1p(-(2.0 ** (-5.0 - jnp.arange(H, dtype=f32))))
    n = jnp.arange(C, dtype=f32)
    diff = n[:, None] - n[None, :]
    decay_in = jnp.where(diff >= 0, jnp.exp(log_gamma[:, None, None] * jnp.maximum(diff, 0.0)), 0.0)
    xi = jnp.exp(log_gamma[:, None] * (n + 1.0))
    zeta = jnp.exp(log_gamma[:, None] * (C - 1.0 - n))
    gamma_c = jnp.exp(log_gamma * C)
    chunks = lambda t: t.reshape(B, n_c, C, H, Dh).transpose(1, 0, 3, 2, 4)

    def step(state, inp):
        qc, kc, vc = inp
        inner = jnp.einsum('bhnd,bhmd->bhnm', qc, kc) * decay_in
        o = jnp.einsum('bhnm,bhme->bhne', inner, vc) + jnp.einsum('bhnd,bhde->bhne', qc, state) * xi[..., None]
        state = state * gamma_c[:, None, None] + jnp.einsum('bhmd,bhme->bhde', kc * zeta[..., None], vc)
        return state, o

    _, o = lax.scan(step, jnp.zeros((B, H, Dh, Dh), f32), (chunks(q), chunks(k), chunks(v)))
    o = head_layer_norm(o.transpose(1, 0, 3, 2, 4).reshape(B, S, H, Dh), NORM_EPS)
    return jax.nn.silu(g.astype(f32)) * o.reshape(B, S, D_GRP)


def deltanet_mixer(q, k, v, a, b, z, conv_w, a_log, dt_bias, norm_g):
    B, S, _ = q.shape
    H, Dh, C = DN_HEADS, HEAD_DIM, DN_CHUNK
    n_c = S // C
    qkv = jnp.concatenate([q, k, v], -1)
    qkv = lax.conv_general_dilated(qkv, conv_w[:, None, :], window_strides=(1,), padding=[(CONV_WIDTH - 1, 0)],
                                   dimension_numbers=('NWC', 'WIO', 'NWC'), feature_group_count=3 * D_GRP)
    qkv = jax.nn.silu(qkv.astype(f32))
    q, k, v = [t.reshape(B, S, H, Dh) for t in jnp.split(qkv, 3, -1)]
    q = l2norm(q) * Dh ** -0.5
    k = l2norm(k)
    beta = jax.nn.sigmoid(b.astype(f32))
    g = -jnp.exp(a_log.astype(f32)) * jax.nn.softplus(a.astype(f32) + dt_bias.astype(f32))
    chunks = lambda t: jnp.moveaxis(t.reshape((B, n_c, C) + t.shape[2:]), 3, 1)
    qc, kc, vc, bc = chunks(q), chunks(k), chunks(v), chunks(beta)
    gc = jnp.cumsum(chunks(g), -1)
    kbeta = kc * bc[..., None]
    vbeta = vc * bc[..., None]
    idx = jnp.arange(C)
    tril = idx[:, None] >= idx[None, :]
    strict = idx[:, None] > idx[None, :]
    decay = jnp.exp(jnp.where(tril, gc[..., :, None] - gc[..., None, :], -jnp.inf))
    m = jnp.where(strict, jnp.einsum('bhncd,bhnkd->bhnck', kbeta, kc) * decay, 0.0)
    eye = jnp.eye(C, dtype=f32)
    t_inv = lax.linalg.triangular_solve(eye + m, jnp.broadcast_to(eye, m.shape), left_side=True,
                                        lower=True, unit_diagonal=True)
    u = t_inv @ vbeta
    w = t_inv @ (kbeta * jnp.exp(gc)[..., None])
    attn_in = jnp.where(tril, jnp.einsum('bhncd,bhnkd->bhnck', qc, kc) * decay, 0.0)
    q_dec = qc * jnp.exp(gc)[..., None]
    k_dec = kc * jnp.exp(gc[..., -1:] - gc)[..., None]
    g_last = jnp.exp(gc[..., -1])

    def step(state, inp):
        u_i, w_i, a_i, qd_i, kd_i, gl_i = inp
        v_new = u_i - w_i @ state
        o = qd_i @ state + a_i @ v_new
        state = state * gl_i[..., None, None] + jnp.swapaxes(kd_i, -1, -2) @ v_new
        return state, o

    xs = tuple(jnp.moveaxis(t, 2, 0) for t in (u, w, attn_in, q_dec, k_dec, g_last))
    _, o = lax.scan(step, jnp.zeros((B, H, Dh, Dh), f32), xs)
    o = o.transpose(1, 0, 3, 2, 4).reshape(B, S, H, Dh)
    o = rms_norm(o, norm_g) * jax.nn.silu(z.astype(f32)).reshape(B, S, H, Dh)
    return o.reshape(B, S, D_GRP)


def _split2(x):
    h = x.astype(bf16)
    return h, (x - h.astype(f32)).astype(bf16)


def _split3(x):
    h = x.astype(bf16)
    r1 = x - h.astype(f32)
    m = r1.astype(bf16)
    return h, m, (r1 - m.astype(f32)).astype(bf16)


def _rwkv_prep_body(x_ref, xprev_ref, l_ref, lprev_ref, mu_ref, mul_ref, w0_ref, a0_ref, kkw_ref, ka_ref,
                    w2_ref, a2_ref, g2_ref, e_ref, r_ref, w_ref, k_ref, v_ref, kk_ref, b_ref, g_ref, *, tm, S):
    i = pl.program_id(0)
    first = (i * tm) % S == 0

    def token_mix(x, prev8, mu):
        rows = lax.broadcasted_iota(jnp.int32, x.shape, 0)
        prev_row = jnp.where(first, 0.0, prev8[SUB - 1:SUB, :])
        x_prev = jnp.where(rows == 0, prev_row, pltpu.roll(x, 1, axis=0))
        return x + (x_prev - x) * mu

    x = token_mix(x_ref[...], xprev_ref[...], mu_ref[...])
    lora = token_mix(l_ref[...], lprev_ref[...], mul_ref[...])
    r = x[:, 0:D_GRP]
    k = x[:, D_GRP:2 * D_GRP]
    v = x[:, 2 * D_GRP:3 * D_GRP]
    dw = jnp.dot(jnp.tanh(lora).astype(bf16), w2_ref[...], preferred_element_type=f32)
    da = jnp.dot(lora.astype(bf16), a2_ref[...], preferred_element_type=f32)
    g = jnp.dot(jax.nn.sigmoid(lora).astype(bf16), g2_ref[...], preferred_element_type=f32)
    w = -jax.nn.softplus(-(w0_ref[...] + dw)) - 0.5
    decay = jnp.exp(-jnp.exp(w))
    a = jax.nn.sigmoid(a0_ref[...] + da)
    kq = k * kkw_ref[...]
    sq_h, sq_l = _split2(kq * kq)
    ss = (jnp.dot(sq_h, e_ref[...], preferred_element_type=f32)
          + jnp.dot(sq_l, e_ref[...], preferred_element_type=f32))
    kk = kq * lax.rsqrt(ss + 1e-6)
    r_ref[...] = r
    w_ref[...] = decay
    k_ref[...] = k * (1.0 + (a - 1.0) * ka_ref[...])
    v_ref[...] = v
    kk_ref[...] = kk
    b_ref[...] = kk * a
    g_ref[...] = g


def rwkv_prep(proj, rkv_blk, lora_blk, mu, w0, w2, a0, a2, g2, k_k, k_a, S, tm=256):
    M = proj.shape[0]
    assert M % tm == 0 and S % tm == 0
    W3 = 3 * D_GRP
    mu_x = mu[:W3].reshape(1, W3)
    mu_l = jnp.pad(mu[W3:], (0, RW_LORA_PAD - RW_LORA)).reshape(1, RW_LORA_PAD)
    w2p = jnp.zeros((RW_LORA_PAD, D_GRP), f32).at[0:RW_DECAY_LORA].set(w2).astype(bf16)
    a2p = jnp.zeros((RW_LORA_PAD, D_GRP), f32).at[RW_DECAY_LORA:RW_DECAY_LORA + RW_AAA_LORA].set(a2).astype(bf16)
    g2p = jnp.zeros((RW_LORA_PAD, D_GRP), f32).at[RW_DECAY_LORA + RW_AAA_LORA:RW_LORA].set(g2).astype(bf16)
    hid = np.arange(D_GRP) // RW_HEAD_DIM
    e = jnp.asarray(hid[:, None] == hid[None, :], bf16)
    row = lambda t: t.reshape(1, D_GRP)
    vec = pl.BlockSpec((1, D_GRP), lambda i: (0, 0))
    wspec = pl.BlockSpec((RW_LORA_PAD, D_GRP), lambda i: (0, 0))
    ospec = pl.BlockSpec((tm, D_GRP), lambda i: (i, 0))
    prev_row_blk = lambda i: jnp.maximum(i * (tm // SUB) - 1, 0)
    return pl.pallas_call(
        functools.partial(_rwkv_prep_body, tm=tm, S=S),
        grid=(M // tm,),
        in_specs=[pl.BlockSpec((tm, W3), lambda i: (i, rkv_blk)),
                  pl.BlockSpec((SUB, W3), lambda i: (prev_row_blk(i), rkv_blk)),
                  pl.BlockSpec((tm, RW_LORA_PAD), lambda i: (i, lora_blk)),
                  pl.BlockSpec((SUB, RW_LORA_PAD), lambda i: (prev_row_blk(i), lora_blk)),
                  pl.BlockSpec((1, W3), lambda i: (0, 0)),
                  pl.BlockSpec((1, RW_LORA_PAD), lambda i: (0, 0)),
                  vec, vec, vec, vec, wspec, wspec, wspec,
                  pl.BlockSpec((D_GRP, D_GRP), lambda i: (0, 0))],
        out_specs=[ospec] * 7,
        out_shape=[jax.ShapeDtypeStruct((M, D_GRP), f32)] * 7,
        compiler_params=pltpu.CompilerParams(dimension_semantics=("parallel",), vmem_limit_bytes=VMEM_LIMIT),
        name="rwkv_prep",
    )(proj, proj, proj, proj, mu_x, mu_l, row(w0), row(a0), row(k_k), row(k_a), w2p, a2p, g2p, e)


def _rwkv_scan_body(r_ref, w_ref, k_ref, v_ref, kk_ref, b_ref, g_ref, lng_ref, lnb_ref, rk_ref, o_ref, s_ref,
                    vc_ref, y_ref, *, G, T):
    tb = pl.program_id(2)

    @pl.when(tb == 0)
    def _():
        s_ref[...] = jnp.zeros_like(s_ref)

    Dh = RW_HEAD_DIM
    lo = lax.broadcasted_iota(jnp.int32, (Dh, LANES), 1) < Dh
    lo8 = lax.broadcasted_iota(jnp.int32, (SUB, LANES), 1) < Dh
    sub_iota = lax.broadcasted_iota(jnp.int32, (4 * SUB, LANES), 0)
    row_iota = lax.broadcasted_iota(jnp.int32, (SUB, LANES), 0)

    def seg_sum(x, m):
        s_lo = jnp.sum(jnp.where(m, x, 0.0), axis=-1, keepdims=True)
        s_hi = jnp.sum(jnp.where(m, 0.0, x), axis=-1, keepdims=True)
        return jnp.where(m, s_lo, s_hi)

    cols = [slice(p * LANES, (p + 1) * LANES) for p in range(G)]
    n_groups = T // SUB

    k_iota = lax.broadcasted_iota(jnp.int32, (8 * SUB, LANES), 0)
    k_head = k_iota // (4 * SUB) == lax.broadcasted_iota(jnp.int32, (8 * SUB, LANES), 1) // Dh
    k_piece = (k_iota // SUB) % 4

    def v_pieces(gi):
        t0 = pl.multiple_of(gi * SUB, SUB)
        vts = []
        for p in range(G):
            vh, vm, vl = _split3(v_ref[0, pl.ds(t0, SUB), cols[p]])
            vp = jnp.concatenate([vh, vm, vl, jnp.zeros_like(vh)], axis=0)
            vts.append(jnp.concatenate([vp[:, :Dh], vp[:, Dh:]], axis=0).T)
        return vts

    def v_columns(vts, i, slot):
        sel = (k_head & (k_piece < 3) & (k_iota % SUB == i)).astype(bf16)
        for p in range(G):
            vc_ref[slot, i, p] = jnp.dot(vts[p], sel, preferred_element_type=f32)

    vts0 = v_pieces(0)
    for i in range(SUB):
        v_columns(vts0, i, 0)

    def run_group(gi, slot, states):
        t0 = pl.multiple_of(gi * SUB, SUB)
        vts_next = v_pieces(jnp.minimum(gi + 1, n_groups - 1))
        r8b = [r_ref[0, pl.ds(t0, SUB), cols[p]].astype(bf16) for p in range(G)]
        yacc = [jnp.zeros((SUB, LANES), f32) for _ in range(G)]
        tiles = [[ref[0, pl.ds(t0, SUB), cols[p]] for ref in (kk_ref, w_ref, b_ref, k_ref)] for p in range(G)]
        for i in range(SUB):
            for p in range(G):
                kk_r, w_r, b_r, k_r = [jnp.broadcast_to(t8[i:i + 1, :], (Dh, LANES)) for t8 in tiles[p]]
                S = states[p]
                sab = -seg_sum(S * kk_r, lo)
                states[p] = S * w_r + sab * b_r + vc_ref[slot, i, p] * k_r
            for p in range(G):
                Sb = states[p].astype(bf16)
                zero = jnp.zeros_like(Sb)
                mt = jnp.concatenate([jnp.where(lo, Sb, zero), jnp.where(lo, zero, Sb)], axis=0)
                y8 = lax.dot_general(r8b[p], mt, (((1,), (1,)), ((), ())), preferred_element_type=f32)
                yacc[p] = jnp.where(row_iota == i, y8, yacc[p])
            v_columns(vts_next, i, 1 - slot)
        for p in range(G):
            y_ref[pl.ds(t0, SUB), cols[p]] = yacc[p]
        return states

    def group_batch(gj, carry):
        states = [s_ref[p] for p in range(G)]
        for u in range(RW_GROUPS_PER_BODY):
            states = run_group(RW_GROUPS_PER_BODY * gj + u, u % 2, states)
        for p in range(G):
            s_ref[p] = states[p]
        return carry

    assert n_groups % RW_GROUPS_PER_BODY == 0 and RW_GROUPS_PER_BODY % 2 == 0
    lax.fori_loop(0, n_groups // RW_GROUPS_PER_BODY, group_batch, 0)

    lo_t = lax.broadcasted_iota(jnp.int32, (T, LANES), 1) < Dh
    for p in range(G):
        y = y_ref[:, cols[p]]
        mean = seg_sum(y, lo_t) * (1.0 / Dh)
        d = y - mean
        var = seg_sum(d * d, lo_t) * (1.0 / Dh)
        yn = d * lax.rsqrt(var + RW_LN_EPS) * lng_ref[:, cols[p]] + lnb_ref[:, cols[p]]
        bonus = seg_sum(r_ref[0, :, cols[p]] * k_ref[0, :, cols[p]] * rk_ref[:, cols[p]], lo_t) * v_ref[0, :, cols[p]]
        o_ref[0, :, cols[p]] = (yn + bonus) * g_ref[0, :, cols[p]]


def rwkv_scan(r, w, k, v, kk, b, g, ln_g, ln_b, r_k, G=8, T=256):
    B, S, D = r.shape
    n_pairs = D // LANES
    assert n_pairs % G == 0 and S % T == 0
    spec = pl.BlockSpec((1, T, G * LANES), lambda bi, pi, ti: (bi, ti, pi))
    vec = pl.BlockSpec((1, G * LANES), lambda bi, pi, ti: (0, pi))
    row = lambda t: t.reshape(1, D)
    return pl.pallas_call(
        functools.partial(_rwkv_scan_body, G=G, T=T),
        grid=(B, n_pairs // G, S // T),
        in_specs=[spec] * 7 + [vec] * 3,
        out_specs=spec,
        out_shape=jax.ShapeDtypeStruct((B, S, D), f32),
        scratch_shapes=[pltpu.VMEM((G, RW_HEAD_DIM, LANES), f32),
                        pltpu.VMEM((2, SUB, G, RW_HEAD_DIM, LANES), f32),
                        pltpu.VMEM((T, G * LANES), f32)],
        compiler_params=pltpu.CompilerParams(dimension_semantics=("parallel", "parallel", "arbitrary"),
                                             vmem_limit_bytes=VMEM_LIMIT),
        name="rwkv_scan",
    )(r, w, k, v, kk, b, g, row(ln_g), row(ln_b), row(r_k))


def rwkv7_mixer(proj, rkv_blk, lora_blk, mu, w0, w2, a0, a2, g2, k_k, k_a, r_k, ln_g, ln_b, B, S):
    outs = rwkv_prep(proj, rkv_blk, lora_blk, mu, w0, w2, a0, a2, g2, k_k, k_a, S)
    r, w, k, v, kk, b, g = [t.reshape(B, S, D_GRP) for t in outs]
    return rwkv_scan(r, w, k, v, kk, b, g, ln_g, ln_b, r_k.reshape(-1))


def _mm_body(a_ref, b_ref, o_ref, acc_ref):
    k = pl.program_id(2)

    @pl.when(k == 0)
    def _():
        acc_ref[...] = jnp.zeros_like(acc_ref)

    acc_ref[...] += jnp.dot(a_ref[...].astype(bf16), b_ref[...].astype(bf16), preferred_element_type=f32)

    @pl.when(k == pl.num_programs(2) - 1)
    def _():
        o_ref[...] = acc_ref[...].astype(o_ref.dtype)


def matmul(a, b, out_dtype=f32, tm=1024, tn=1024, tk=2048, name="matmul"):
    M, K = a.shape
    _, N = b.shape
    tm, tn, tk = min(tm, M), min(tn, N), min(tk, K)
    assert M % tm == 0 and N % tn == 0 and K % tk == 0, (M, N, K, tm, tn, tk)
    return pl.pallas_call(
        _mm_body,
        grid=(M // tm, N // tn, K // tk),
        in_specs=[pl.BlockSpec((tm, tk), lambda i, j, k: (i, k)),
                  pl.BlockSpec((tk, tn), lambda i, j, k: (k, j))],
        out_specs=pl.BlockSpec((tm, tn), lambda i, j, k: (i, j)),
        out_shape=jax.ShapeDtypeStruct((M, N), out_dtype),
        scratch_shapes=[pltpu.VMEM((tm, tn), f32)],
        compiler_params=pltpu.CompilerParams(
            dimension_semantics=("parallel", "parallel", "arbitrary"), vmem_limit_bytes=VMEM_LIMIT),
        name=name,
    )(a, b)


def _rms_norm_body(x_ref, g_ref, o_ref):
    x = x_ref[...]
    y = x * lax.rsqrt(jnp.mean(x * x, axis=-1, keepdims=True) + NORM_EPS)
    o_ref[...] = (y * g_ref[...]).astype(o_ref.dtype)


def rms_norm_rows(x, g, out_dtype, tm=256):
    M, D = x.shape
    return pl.pallas_call(
        _rms_norm_body,
        grid=(M // tm,),
        in_specs=[pl.BlockSpec((tm, D), lambda i: (i, 0)), pl.BlockSpec((1, D), lambda i: (0, 0))],
        out_specs=pl.BlockSpec((tm, D), lambda i: (i, 0)),
        out_shape=jax.ShapeDtypeStruct((M, D), out_dtype),
        compiler_params=pltpu.CompilerParams(dimension_semantics=("parallel",), vmem_limit_bytes=VMEM_LIMIT),
        name="rms_norm",
    )(x, g.reshape(1, D).astype(f32))


def _out_proj_body(a0_ref, a1_ref, a2_ref, a3_ref, b_ref, h_ref, o_ref, acc_ref):
    k = pl.program_id(2)

    @pl.when(k == 0)
    def _():
        acc_ref[...] = h_ref[...]

    for j, a_ref in enumerate((a0_ref, a1_ref, a2_ref, a3_ref)):
        @pl.when(k == j)
        def _(a_ref=a_ref):
            acc_ref[...] += jnp.dot(a_ref[...].astype(bf16), b_ref[...], preferred_element_type=f32)

    @pl.when(k == pl.num_programs(2) - 1)
    def _():
        o_ref[...] = acc_ref[...]


def out_proj_residual(parts, w, h, tm=1024, tn=1024):
    M, D = h.shape
    aspec = pl.BlockSpec((tm, D_GRP), lambda i, j, k: (i, 0))
    return pl.pallas_call(
        _out_proj_body,
        grid=(M // tm, D // tn, len(parts)),
        in_specs=[aspec] * 4 + [pl.BlockSpec((D_GRP, tn), lambda i, j, k: (k, j)),
                                pl.BlockSpec((tm, tn), lambda i, j, k: (i, j))],
        out_specs=pl.BlockSpec((tm, tn), lambda i, j, k: (i, j)),
        out_shape=jax.ShapeDtypeStruct((M, D), f32),
        scratch_shapes=[pltpu.VMEM((tm, tn), f32)],
        compiler_params=pltpu.CompilerParams(
            dimension_semantics=("parallel", "parallel", "arbitrary"), vmem_limit_bytes=VMEM_LIMIT),
        name="out_proj",
    )(*parts, w, h)


def _ple_body(a_ref, b_ref, p_ref, pw_ref, h_ref, o_ref, acc_ref):
    k = pl.program_id(2)

    @pl.when(k == 0)
    def _():
        acc_ref[...] = jnp.zeros_like(acc_ref)

    acc_ref[...] += jnp.dot(a_ref[...], b_ref[...], preferred_element_type=f32)

    @pl.when(k == pl.num_programs(2) - 1)
    def _():
        emb = jnp.dot(p_ref[...].astype(bf16), pw_ref[...], preferred_element_type=f32)
        o_ref[...] = h_ref[...] + emb * jax.nn.sigmoid(acc_ref[...])


def ple_residual(hn, gate_w, p, ple_w, h, tm=1024, tn=1024, tk=2048):
    M, D = h.shape
    return pl.pallas_call(
        _ple_body,
        grid=(M // tm, D // tn, D // tk),
        in_specs=[pl.BlockSpec((tm, tk), lambda i, j, k: (i, k)),
                  pl.BlockSpec((tk, tn), lambda i, j, k: (k, j)),
                  pl.BlockSpec((tm, PLE_DIM), lambda i, j, k: (i, 0)),
                  pl.BlockSpec((PLE_DIM, tn), lambda i, j, k: (0, j)),
                  pl.BlockSpec((tm, tn), lambda i, j, k: (i, j))],
        out_specs=pl.BlockSpec((tm, tn), lambda i, j, k: (i, j)),
        out_shape=jax.ShapeDtypeStruct((M, D), f32),
        scratch_shapes=[pltpu.VMEM((tm, tn), f32)],
        compiler_params=pltpu.CompilerParams(
            dimension_semantics=("parallel", "parallel", "arbitrary"), vmem_limit_bytes=VMEM_LIMIT),
        name="ple",
    )(hn, gate_w, p, ple_w, h)


def _moe_router_body(x_ref, w_ref, b_ref, id_ref, wt_ref):
    logits = jnp.dot(x_ref[...].astype(bf16), w_ref[...], preferred_element_type=f32) + b_ref[...]
    lane = lax.broadcasted_iota(jnp.int32, logits.shape, 1).astype(f32)
    ninf = -jnp.inf

    def top1(vals):
        m = jnp.max(vals, axis=-1, keepdims=True)
        idx = jnp.min(jnp.where(vals == m, lane, float(LANES)), axis=-1, keepdims=True)
        return m, idx

    gl = jnp.where(lane < N_GROUPS, logits, ninf)
    g_max, g_idx = top1(gl)
    g_w = 1.0 / jnp.sum(jnp.exp(gl - g_max), axis=-1, keepdims=True)
    lo = N_GROUPS + g_idx * EXPERTS_PER_GROUP
    el = jnp.where((lane >= lo) & (lane < lo + EXPERTS_PER_GROUP), logits, ninf)
    e_max, _ = top1(el)
    pe = jnp.exp(el - e_max)
    p = pe / jnp.sum(pe, axis=-1, keepdims=True)
    p = jnp.where(el == ninf, ninf, p)
    p1, i1 = top1(p)
    p2, i2 = top1(jnp.where(lane == i1, ninf, p))
    scale = g_w / (p1 + p2)
    id_ref[...] = jnp.where(lane == 0, i1 - N_GROUPS, jnp.where(lane == 1, i2 - N_GROUPS, 0.0)).astype(jnp.int32)
    wt_ref[...] = jnp.where(lane == 0, p1 * scale, jnp.where(lane == 1, p2 * scale, 0.0))


def moe_router(xb, router_grp, router_grp_b, router_exp, router_exp_b, tm=512):
    M, D = xb.shape
    n_log = N_GROUPS + N_EXPERTS
    w = jnp.pad(jnp.concatenate([router_grp, router_exp], axis=1), ((0, 0), (0, LANES - n_log))).astype(bf16)
    b = jnp.pad(jnp.concatenate([router_grp_b, router_exp_b]), (0, LANES - n_log)).reshape(1, LANES).astype(f32)
    ids, wts = pl.pallas_call(
        _moe_router_body,
        grid=(M // tm,),
        in_specs=[pl.BlockSpec((tm, D), lambda i: (i, 0)), pl.BlockSpec((D, LANES), lambda i: (0, 0)),
                  pl.BlockSpec((1, LANES), lambda i: (0, 0))],
        out_specs=[pl.BlockSpec((tm, LANES), lambda i: (i, 0))] * 2,
        out_shape=[jax.ShapeDtypeStruct((M, LANES), jnp.int32), jax.ShapeDtypeStruct((M, LANES), f32)],
        compiler_params=pltpu.CompilerParams(dimension_semantics=("parallel",), vmem_limit_bytes=VMEM_LIMIT),
        name="moe_router",
    )(xb, w, b)
    return ids[:, :TOP_K], wts[:, :TOP_K]


def _moe_up_body(be_ref, nu_ref, x_ref, wg_ref, wu_ref, h_ref):
    i = pl.program_id(1)

    @pl.when(i < nu_ref[0])
    def _():
        x = x_ref[...].astype(bf16)
        hg = jnp.dot(x, wg_ref[0, 0].astype(bf16), preferred_element_type=f32)
        hu = jnp.dot(x, wu_ref[0, 0].astype(bf16), preferred_element_type=f32)
        h_ref[...] = (hg * jax.nn.sigmoid(hg) * hu).astype(bf16)

    @pl.when(i >= nu_ref[0])
    def _():
        h_ref[...] = jnp.zeros_like(h_ref)


def _moe_down_body(be_ref, nu_ref, h_ref, wd_ref, wt_ref, y_ref):
    i = pl.program_id(0)

    @pl.when(i < nu_ref[0])
    def _():
        y_ref[...] = jnp.dot(h_ref[...], wd_ref[0, 0].astype(bf16), preferred_element_type=f32) * wt_ref[...]

    @pl.when(i >= nu_ref[0])
    def _():
        y_ref[...] = jnp.zeros_like(y_ref)


def moe_experts(x_pad, w_pad, blk_expert, n_used, wg, wu, wd, layer, tf=384):
    A_pad, D = x_pad.shape
    n_blk = A_pad // ROW_BLOCK
    nf = D_EXPERT // tf
    h = pl.pallas_call(
        _moe_up_body,
        grid_spec=pltpu.PrefetchScalarGridSpec(
            num_scalar_prefetch=2,
            grid=(nf, n_blk),
            in_specs=[pl.BlockSpec((ROW_BLOCK, D), lambda f, i, be, nu: (i, 0)),
                      pl.BlockSpec((1, 1, D, tf), lambda f, i, be, nu: (layer, be[i], 0, f)),
                      pl.BlockSpec((1, 1, D, tf), lambda f, i, be, nu: (layer, be[i], 0, f))],
            out_specs=pl.BlockSpec((ROW_BLOCK, tf), lambda f, i, be, nu: (i, f)),
        ),
        out_shape=jax.ShapeDtypeStruct((A_pad, D_EXPERT), bf16),
        compiler_params=pltpu.CompilerParams(
            dimension_semantics=("arbitrary", "arbitrary"), vmem_limit_bytes=VMEM_LIMIT),
        name="moe_up",
    )(blk_expert, n_used, x_pad, wg, wu)
    return pl.pallas_call(
        _moe_down_body,
        grid_spec=pltpu.PrefetchScalarGridSpec(
            num_scalar_prefetch=2,
            grid=(n_blk,),
            in_specs=[pl.BlockSpec((ROW_BLOCK, D_EXPERT), lambda i, be, nu: (i, 0)),
                      pl.BlockSpec((1, 1, D_EXPERT, D), lambda i, be, nu: (layer, be[i], 0, 0)),
                      pl.BlockSpec((ROW_BLOCK, 1), lambda i, be, nu: (i, 0))],
            out_specs=pl.BlockSpec((ROW_BLOCK, D), lambda i, be, nu: (i, 0)),
        ),
        out_shape=jax.ShapeDtypeStruct((A_pad, D), f32),
        compiler_params=pltpu.CompilerParams(dimension_semantics=("arbitrary",), vmem_limit_bytes=VMEM_LIMIT),
        name="moe_down",
    )(blk_expert, n_used, h, wd, w_pad.reshape(A_pad, 1))


def hier_moe(xb, router_grp, router_grp_b, router_exp, router_exp_b, w_gate, w_up, w_down, layer):
    M, D = xb.shape
    expert_ids, weights = moe_router(xb, router_grp, router_grp_b, router_exp, router_exp_b)
    A = M * TOP_K
    e_flat = expert_ids.reshape(A)
    onehot = (e_flat[:, None] == jnp.arange(N_EXPERTS)[None, :]).astype(jnp.int32)
    rank = jnp.take_along_axis(jnp.cumsum(onehot, axis=0), e_flat[:, None], axis=1)[:, 0] - 1
    counts = jnp.sum(onehot, axis=0)
    padded = (counts + ROW_BLOCK - 1) // ROW_BLOCK * ROW_BLOCK
    end_pad = jnp.cumsum(padded)
    dest = (end_pad - padded)[e_flat] + rank
    n_blk = -(-A // ROW_BLOCK) + N_EXPERTS
    A_pad = n_blk * ROW_BLOCK
    tok_pad = jnp.zeros((A_pad,), jnp.int32).at[dest].set(jnp.arange(A, dtype=jnp.int32) // TOP_K)
    w_pad = jnp.zeros((A_pad,), f32).at[dest].set(weights.reshape(A))
    blk_expert = jnp.minimum(jnp.searchsorted(end_pad, jnp.arange(n_blk) * ROW_BLOCK, side='right'),
                             N_EXPERTS - 1).astype(jnp.int32)
    n_used = (end_pad[-1:] // ROW_BLOCK).astype(jnp.int32)
    x_pad = xb[tok_pad]
    y_pad = moe_experts(x_pad, w_pad, blk_expert, n_used, w_gate, w_up, w_down, layer)
    rows = dest.reshape(M, TOP_K)
    return y_pad[rows[:, 0]] + y_pad[rows[:, 1]]


def _rope_tables(S, theta, rot_dims, width):
    half = rot_dims // 2
    freqs = theta ** (-np.arange(half, dtype=np.float64) / half)
    ang = np.arange(S, dtype=np.float64)[:, None] * freqs[None, :]
    cos = np.ones((S, width), np.float64)
    sin_lo = np.zeros((S, width), np.float64)
    sin_hi = np.zeros((S, width), np.float64)
    cos[:, :half] = np.cos(ang)
    cos[:, half:rot_dims] = np.cos(ang)
    sin_lo[:, :half] = -np.sin(ang)
    sin_hi[:, half:rot_dims] = np.sin(ang)
    return [jnp.asarray(t, f32) for t in (cos, sin_lo, sin_hi)]


def _rope128(x, cos, sin_lo, sin_hi, half):
    return x * cos + pltpu.roll(x, LANES - half, axis=1) * sin_lo + pltpu.roll(x, half, axis=1) * sin_hi


def _nsa_prep_body(q_ref, kv_ref, cos_ref, sl_ref, sh_ref, qo_ref, kvo_ref, co_ref):
    cos, sl, sh = cos_ref[...], sl_ref[...], sh_ref[...]
    half = ROPE_DIMS // 2
    scale = HEAD_DIM ** -0.5
    for h in range(NSA_HEADS):
        c = slice(h * LANES, (h + 1) * LANES)
        qo_ref[:, c] = (_rope128(q_ref[:, c], cos, sl, sh, half) * scale).astype(bf16)
    for j in range(6):
        for g in range(NSA_KV_HEADS):
            c = slice((j * NSA_KV_HEADS + g) * LANES, (j * NSA_KV_HEADS + g + 1) * LANES)
            x = kv_ref[:, c]
            if j % 2 == 0:
                x = _rope128(x, cos, sl, sh, half)
            if j < 2:
                co_ref[:, c] = x.astype(bf16)
            else:
                kvo_ref[:, slice(c.start - 4 * LANES, c.stop - 4 * LANES)] = x.astype(bf16)


def nsa_prep(proj, S, tm=512):
    M = proj.shape[0]
    assert M % tm == 0 and S % tm == 0
    tabs = _rope_tables(S, ROPE_THETA, ROPE_DIMS, LANES)
    tspec = pl.BlockSpec((tm, LANES), lambda i: (i % (S // tm), 0))
    return pl.pallas_call(
        _nsa_prep_body,
        grid=(M // tm,),
        in_specs=[pl.BlockSpec((tm, D_GRP), lambda i: (i, OFF_NSA_Q // D_GRP)),
                  pl.BlockSpec((tm, 6 * NSA_KV_W), lambda i: (i, OFF_NSA_KV // (6 * NSA_KV_W))),
                  tspec, tspec, tspec],
        out_specs=[pl.BlockSpec((tm, D_GRP), lambda i: (i, 0)),
                   pl.BlockSpec((tm, 4 * NSA_KV_W), lambda i: (i, 0)),
                   pl.BlockSpec((tm, 2 * NSA_KV_W), lambda i: (i, 0))],
        out_shape=[jax.ShapeDtypeStruct((M, D_GRP), bf16), jax.ShapeDtypeStruct((M, 4 * NSA_KV_W), bf16),
                   jax.ShapeDtypeStruct((M, 2 * NSA_KV_W), bf16)],
        compiler_params=pltpu.CompilerParams(dimension_semantics=("parallel",), vmem_limit_bytes=VMEM_LIMIT),
        name="nsa_prep",
    )(proj, proj, *tabs)


def _nsa_compress_body(r_ref, w1_ref, w2_ref, pe_ref, o_ref):
    r = r_ref[0, 0, 0]
    half_w = CMP_STRIDE * HEAD_DIM
    a = jnp.dot(r, w1_ref[0, :half_w, :], preferred_element_type=f32)
    b = jnp.dot(r, w1_ref[0, half_w:, :], preferred_element_type=f32)
    n = r.shape[0]
    pe = jnp.dot(jnp.broadcast_to(pe_ref[0], (SUB, pe_ref.shape[-1])), w1_ref[0], preferred_element_type=f32)[0:1]
    pre = a + pltpu.roll(b, n - 1, axis=0) + pe
    hid = pre * jax.nn.sigmoid(pre)
    o_ref[0, 0, 0] = jnp.dot(hid.astype(bf16), w2_ref[0], preferred_element_type=f32).astype(bf16)


def nsa_compress(c_rot, w1s, w2s, pes, B, S):
    G, Dh = NSA_KV_HEADS, HEAD_DIM
    nc = S // CMP_STRIDE
    r = c_rot.reshape(B, nc, CMP_STRIDE, 2, G, Dh).transpose(3, 0, 4, 1, 2, 5).reshape(2, B, G, nc, CMP_STRIDE * Dh)
    w1 = jnp.stack(w1s).astype(bf16)
    w2 = jnp.stack(w2s).astype(bf16)
    pe = jnp.stack(pes).reshape(2, 1, CMP_LEN * Dh).astype(bf16)
    return pl.pallas_call(
        _nsa_compress_body,
        grid=(2, B, G),
        in_specs=[pl.BlockSpec((1, 1, 1, nc, CMP_STRIDE * Dh), lambda j, b, g: (j, b, g, 0, 0)),
                  pl.BlockSpec((1, CMP_LEN * Dh, Dh), lambda j, b, g: (j, 0, 0)),
                  pl.BlockSpec((1, Dh, Dh), lambda j, b, g: (j, 0, 0)),
                  pl.BlockSpec((1, 1, CMP_LEN * Dh), lambda j, b, g: (j, 0, 0))],
        out_specs=pl.BlockSpec((1, 1, 1, nc, Dh), lambda j, b, g: (j, b, g, 0, 0)),
        out_shape=jax.ShapeDtypeStruct((2, B, G, nc, Dh), bf16),
        compiler_params=pltpu.CompilerParams(dimension_semantics=("parallel", "parallel", "parallel"),
                                             vmem_limit_bytes=VMEM_LIMIT),
        name="nsa_compress",
    )(r, w1, w2, pe)


def _softmax_rows(s, mask):
    s = jnp.where(mask, s, -jnp.inf)
    m = jnp.max(s, axis=-1, keepdims=True)
    m = jnp.where(m == -jnp.inf, 0.0, m)
    e = jnp.exp(s - m)
    return e, jnp.maximum(jnp.sum(e, axis=-1, keepdims=True), jnp.finfo(f32).tiny)


def _nsa_attn_body(q_ref, cmp_ref, ks_ref, vs_ref, kw_ref, vw_ref, gate_ref, ovt_ref, o_ref, *, tq, tk, S):
    g = pl.program_id(1)
    qi = pl.program_id(2)
    R, Dh = NSA_GROUP, HEAD_DIM
    rows = R * tq
    n_blk = S // SEL_LEN
    n_cmp = S // CMP_STRIDE - 1
    q0 = qi * tq
    q4 = jnp.concatenate([q_ref[0, :, r * Dh:(r + 1) * Dh] for r in range(R)], axis=0)
    nt = (((1,), (1,)), ((), ()))

    def qpos(shape):
        return q0 + lax.broadcasted_iota(jnp.int32, shape, 0) % tq

    k_cmp, v_cmp = cmp_ref[0, 0, 0], cmp_ref[1, 0, 0]
    ncp = k_cmp.shape[0]
    s = lax.dot_general(q4, k_cmp, nt, preferred_element_type=f32)
    n_idx = lax.broadcasted_iota(jnp.int32, (rows, ncp), 1)
    mask = (n_idx * CMP_STRIDE + (CMP_LEN - 1) <= qpos((rows, ncp))) & (n_idx < n_cmp)
    e, den = _softmax_rows(s, mask)
    p = e / den
    o_cmp = jnp.dot(p.astype(bf16), v_cmp, preferred_element_type=f32)

    psum = p[0:tq]
    for r in range(1, R):
        psum = psum + p[r * tq:(r + 1) * tq]
    ph, plo = _split2(psum)
    imp = (lax.dot_general(ovt_ref[...], ph, nt, preferred_element_type=f32)
           + lax.dot_general(ovt_ref[...], plo, nt, preferred_element_type=f32))
    blk = lax.broadcasted_iota(jnp.int32, (n_blk, tq), 0)
    cur = (q0 + lax.broadcasted_iota(jnp.int32, (n_blk, tq), 1)) // SEL_LEN
    visible = blk <= cur
    forced = (blk == 0) | (visible & (blk > cur - N_LOCAL_FORCED))
    imp = jnp.where(forced, FORCED_SCORE, jnp.where(visible, imp, -1.0))
    sel_t = jnp.zeros((n_blk, tq), f32)
    blk_f = blk.astype(f32)
    for _ in range(min(N_SEL, n_blk)):
        m = jnp.max(imp, axis=0, keepdims=True)
        first = jnp.min(jnp.where(imp == m, blk_f, float(n_blk)), axis=0, keepdims=True)
        hit = blk_f == first
        sel_t = jnp.where(hit, 1.0, sel_t)
        imp = jnp.where(hit, -jnp.inf, imp)
    sel4 = jnp.concatenate([sel_t.T.astype(bf16)] * R, axis=0)

    blocks_per_tile = tk // SEL_LEN

    def sel_step(kt, carry):
        m_run, l_run, acc = carry
        k0 = pl.multiple_of(kt * tk, tk)
        kt_k = ks_ref[0, pl.ds(k0, tk), :]
        kt_v = vs_ref[0, pl.ds(k0, tk), :]
        s = lax.dot_general(q4, kt_k, nt, preferred_element_type=f32)
        expand = (lax.broadcasted_iota(jnp.int32, (n_blk, tk), 0)
                  == kt * blocks_per_tile + lax.broadcasted_iota(jnp.int32, (n_blk, tk), 1) // SEL_LEN)
        chosen = jnp.dot(sel4, expand.astype(bf16), preferred_element_type=f32)
        kpos = k0 + lax.broadcasted_iota(jnp.int32, (rows, tk), 1)
        mask = (chosen > 0.5) & (kpos <= qpos((rows, tk)))
        s = jnp.where(mask, s, -jnp.inf)
        m_new = jnp.maximum(m_run, jnp.max(s, axis=-1, keepdims=True))
        m_safe = jnp.where(m_new == -jnp.inf, 0.0, m_new)
        alpha = jnp.exp(m_run - m_safe)
        e = jnp.exp(s - m_safe)
        l_new = alpha * l_run + jnp.sum(e, axis=-1, keepdims=True)
        acc = alpha * acc + jnp.dot(e.astype(bf16), kt_v, preferred_element_type=f32)
        return m_new, l_new, acc

    n_kt = (q0 + tq - 1) // tk + 1
    init = (jnp.full((rows, 1), -jnp.inf, f32), jnp.zeros((rows, 1), f32), jnp.zeros((rows, Dh), f32))
    _, l_sel, acc_sel = lax.fori_loop(0, n_kt, sel_step, init)
    o_sel = acc_sel / jnp.maximum(l_sel, jnp.finfo(f32).tiny)

    span = WINDOW + tq
    w0 = pl.multiple_of(jnp.maximum(q0 - WINDOW, 0), tq)
    kwin = kw_ref[0, pl.ds(w0, span), :]
    vwin = vw_ref[0, pl.ds(w0, span), :]
    s = lax.dot_general(q4, kwin, nt, preferred_element_type=f32)
    kpos = w0 + lax.broadcasted_iota(jnp.int32, (rows, span), 1)
    qp = qpos((rows, span))
    e, den = _softmax_rows(s, (kpos <= qp) & (kpos > qp - WINDOW))
    o_win = jnp.dot(e.astype(bf16), vwin, preferred_element_type=f32) / den

    gate = jax.nn.sigmoid(gate_ref[0])
    for r in range(R):
        c0 = (g * R + r) * 3
        lane = lax.broadcasted_iota(jnp.int32, (tq, LANES), 1)
        gsel = lambda c: jnp.sum(jnp.where(lane == c0 + c, gate, 0.0), axis=-1, keepdims=True)
        rs = slice(r * tq, (r + 1) * tq)
        o_ref[0, :, r * Dh:(r + 1) * Dh] = (gsel(0) * o_cmp[rs] + gsel(1) * o_sel[rs]
                                            + gsel(2) * o_win[rs]).astype(o_ref.dtype)


def nsa_attention(q_rot, kv_rot, cmp_kv, proj3, B, S, tq=256, tk=512):
    G, R, Dh = NSA_KV_HEADS, NSA_GROUP, HEAD_DIM
    n_blk = S // SEL_LEN
    assert S % tk == 0 and tk % tq == 0 and S >= WINDOW + tq and n_blk % SUB == 0
    nc = S // CMP_STRIDE
    c0 = np.arange(nc - 1)[None, :] * CMP_STRIDE
    s0 = np.arange(n_blk)[:, None] * SEL_LEN
    ovt = np.zeros((n_blk, nc), np.float32)
    ovt[:, :nc - 1] = np.clip(np.minimum(c0 + CMP_LEN, s0 + SEL_LEN) - np.maximum(c0, s0), 0, None) / CMP_LEN
    kvspec = lambda j: pl.BlockSpec((1, S, Dh), lambda b, g, i: (b, 0, j * G + g))
    return pl.pallas_call(
        functools.partial(_nsa_attn_body, tq=tq, tk=tk, S=S),
        grid=(B, G, S // tq),
        in_specs=[pl.BlockSpec((1, tq, R * Dh), lambda b, g, i: (b, i, g)),
                  pl.BlockSpec((2, 1, 1, nc, Dh), lambda b, g, i: (0, b, g, 0, 0)),
                  kvspec(0), kvspec(1), kvspec(2), kvspec(3),
                  pl.BlockSpec((1, tq, LANES), lambda b, g, i: (b, i, OFF_SMALL // LANES)),
                  pl.BlockSpec((n_blk, nc), lambda b, g, i: (0, 0))],
        out_specs=pl.BlockSpec((1, tq, R * Dh), lambda b, g, i: (b, i, g)),
        out_shape=jax.ShapeDtypeStruct((B, S, D_GRP), bf16),
        compiler_params=pltpu.CompilerParams(dimension_semantics=("parallel", "parallel", "arbitrary"),
                                             vmem_limit_bytes=VMEM_LIMIT),
        name="nsa_attention",
    )(q_rot, cmp_kv, kv_rot, kv_rot, kv_rot, kv_rot, proj3, jnp.asarray(ovt, bf16))


def nsa_mixer_pallas(proj, ck_w1, ck_w2, ck_pe, cv_w1, cv_w2, cv_pe, B, S):
    q_rot, kv_rot, c_rot = nsa_prep(proj, S)
    cmp_kv = nsa_compress(c_rot, (ck_w1, cv_w1), (ck_w2, cv_w2), (ck_pe, cv_pe), B, S)
    return nsa_attention(q_rot.reshape(B, S, D_GRP), kv_rot.reshape(B, S, 4 * NSA_KV_W), cmp_kv,
                         proj.reshape(B, S, P_PAD), B, S)


def _retention_body(q_ref, k_ref, v_ref, g_ref, cos_ref, sin_ref, dec_ref, xi_ref, zeta_ref, gc_ref, o_ref, st_ref,
                    *, T):
    @pl.when(pl.program_id(2) == 0)
    def _():
        st_ref[...] = jnp.zeros_like(st_ref)

    C, Dh = RET_CHUNK, HEAD_DIM
    nt = (((1,), (1,)), ((), ()))
    decay_in, xi, zeta, gamma_c = dec_ref[0], xi_ref[0], zeta_ref[0], gc_ref[0, 0:1, :]
    state = st_ref[...]
    for c in range(T // C):
        rs = slice(c * C, (c + 1) * C)
        cos, sin = cos_ref[rs, :], sin_ref[rs, :]
        rope = lambda x: x * cos + pltpu.roll(x, Dh // 2, axis=1) * sin
        q = rope(q_ref[0, rs, :])
        k = rope(k_ref[0, rs, :]) * Dh ** -0.5
        qb, vb = q.astype(bf16), v_ref[0, rs, :].astype(bf16)
        inner = lax.dot_general(qb, k.astype(bf16), nt, preferred_element_type=f32) * decay_in
        o = (jnp.dot(inner.astype(bf16), vb, preferred_element_type=f32)
             + jnp.dot(qb, state.astype(bf16), preferred_element_type=f32) * xi)
        kz = (k * zeta).astype(bf16)
        state = state * gamma_c + jnp.dot(kz.T, vb, preferred_element_type=f32)
        mu = jnp.mean(o, axis=-1, keepdims=True)
        d = o - mu
        o = d * lax.rsqrt(jnp.mean(d * d, axis=-1, keepdims=True) + NORM_EPS)
        gt = g_ref[0, rs, :]
        o_ref[0, rs, :] = (gt * jax.nn.sigmoid(gt) * o).astype(o_ref.dtype)
    st_ref[...] = state


def retention_mixer_pallas(proj3, B, S, T=512):
    H, Dh, C = RET_HEADS, HEAD_DIM, RET_CHUNK
    assert S % T == 0 and T % C == 0
    freqs = RET_THETA ** (-np.linspace(0.0, 1.0, Dh // 2))
    ang = np.arange(S, dtype=np.float64)[:, None] * freqs[None, :]
    cos = jnp.asarray(np.concatenate([np.cos(ang), np.cos(ang)], 1), f32)
    sin = jnp.asarray(np.concatenate([-np.sin(ang), np.sin(ang)], 1), f32)
    log_gamma = np.log1p(-(2.0 ** (-5.0 - np.arange(H, dtype=np.float64))))
    n = np.arange(C, dtype=np.float64)
    diff = n[:, None] - n[None, :]
    decay_in = np.where(diff >= 0, np.exp(log_gamma[:, None, None] * np.maximum(diff, 0.0)), 0.0)
    xi = np.exp(log_gamma[:, None] * (n + 1.0))[:, :, None] * np.ones((1, 1, Dh))
    zeta = np.exp(log_gamma[:, None] * (C - 1.0 - n))[:, :, None] * np.ones((1, 1, Dh))
    gamma_c = np.exp(log_gamma * C)[:, None, None] * np.ones((1, SUB, Dh))
    blk = lambda j: pl.BlockSpec((1, T, Dh), lambda b, h, t: (b, t, (OFF_RET + j * D_GRP) // Dh + h))
    tab = pl.BlockSpec((T, Dh), lambda b, h, t: (t, 0))
    per_head = lambda r: pl.BlockSpec((1, r, Dh), lambda b, h, t: (h, 0, 0))
    return pl.pallas_call(
        functools.partial(_retention_body, T=T),
        grid=(B, H, S // T),
        in_specs=[blk(0), blk(1), blk(2), blk(3), tab, tab, per_head(C), per_head(C), per_head(C), per_head(SUB)],
        out_specs=pl.BlockSpec((1, T, Dh), lambda b, h, t: (b, t, h)),
        out_shape=jax.ShapeDtypeStruct((B, S, D_GRP), bf16),
        scratch_shapes=[pltpu.VMEM((Dh, Dh), f32)],
        compiler_params=pltpu.CompilerParams(dimension_semantics=("parallel", "parallel", "arbitrary"),
                                             vmem_limit_bytes=VMEM_LIMIT),
        name="retention",
    )(proj3, proj3, proj3, proj3, cos, sin, jnp.asarray(decay_in, f32), jnp.asarray(xi, f32),
      jnp.asarray(zeta, f32), jnp.asarray(gamma_c, f32))


def _dot3(a, b):
    ah, al = _split2(a)
    bh, bl = _split2(b)
    return (jnp.dot(ah, bh, preferred_element_type=f32) + jnp.dot(ah, bl, preferred_element_type=f32)
            + jnp.dot(al, bh, preferred_element_type=f32))


def _deltanet_body(q_ref, k_ref, v_ref, z_ref, qp_ref, kp_ref, vp_ref, ab_ref, cw_ref, alog_ref, dtb_ref, ng_ref,
                   o_ref, st_ref, *, T, HS):
    hb = pl.program_id(1)
    first = pl.program_id(2) == 0

    @pl.when(first)
    def _():
        st_ref[...] = jnp.zeros_like(st_ref)

    C, Dh = DN_CHUNK, HEAD_DIM
    nt = (((1,), (1,)), ((), ()))
    n_chunks = T // C
    ab = ab_ref[0]
    lane = lax.broadcasted_iota(jnp.int32, ab.shape, 1)

    def conv_silu(x_ref, prev_ref, j, hc):
        prev = jnp.where(first, 0.0, prev_ref[0, :, hc])
        xcat = jnp.concatenate([prev, x_ref[0, :, hc]], axis=0)
        w = cw_ref[:, j, hc]
        y = sum(w[i:i + 1, :] * xcat[SUB - (CONV_WIDTH - 1) + i:SUB - (CONV_WIDTH - 1) + i + T, :]
                for i in range(CONV_WIDTH))
        return y * jax.nn.sigmoid(y)

    def l2n(x):
        return x * lax.rsqrt(jnp.sum(x * x, axis=-1, keepdims=True) + 1e-6)

    units = [(hh, slice(c * C, (c + 1) * C)) for hh in range(HS) for c in range(n_chunks)]
    qs, ks_, vs_, betas, g_rows = [], [], [], [], []
    for hh in range(HS):
        hc = slice(hh * Dh, (hh + 1) * Dh)
        head = hb * HS + hh
        q_all = l2n(conv_silu(q_ref, qp_ref, 0, hc)) * Dh ** -0.5
        k_all = l2n(conv_silu(k_ref, kp_ref, 1, hc))
        v_all = conv_silu(v_ref, vp_ref, 2, hc)
        a_col = jnp.sum(jnp.where(lane == 3 * NSA_HEADS + head, ab, 0.0), axis=-1, keepdims=True)
        b_col = jnp.sum(jnp.where(lane == 3 * NSA_HEADS + DN_HEADS + head, ab, 0.0), axis=-1, keepdims=True)
        beta_all = jax.nn.sigmoid(b_col)
        g_all = -jnp.exp(alog_ref[hh, 0:1, :]) * jax.nn.softplus(a_col + dtb_ref[hh, 0:1, :])
        for c in range(n_chunks):
            rs = slice(c * C, (c + 1) * C)
            qs.append(q_all[rs]); ks_.append(k_all[rs]); vs_.append(v_all[rs])
            betas.append(beta_all[rs]); g_rows.append(g_all[rs])

    ri = lax.broadcasted_iota(jnp.int32, (C, C), 0)
    ci = lax.broadcasted_iota(jnp.int32, (C, C), 1)
    tril, strict = ri >= ci, ri > ci
    ltri = tril.astype(bf16)
    eye = (ri == ci).astype(f32)
    gcs = []
    for g_rows_u in g_rows:
        gh, gm, gl = _split3(g_rows_u)
        gcs.append(jnp.dot(ltri, gh, preferred_element_type=f32) + jnp.dot(ltri, gm, preferred_element_type=f32)
                   + jnp.dot(ltri, gl, preferred_element_type=f32))
    decays = [jnp.exp(jnp.where(tril, gc[:, :C] - gc[:, :C].T, -jnp.inf)) for gc in gcs]
    kbs = [k.astype(bf16) for k in ks_]
    kbetas = [k * b for k, b in zip(ks_, betas)]
    ms = [jnp.where(strict, lax.dot_general(kbeta.astype(bf16), kb, nt, preferred_element_type=f32) * d, 0.0)
          for kbeta, kb, d in zip(kbetas, kbs, decays)]
    attns = [jnp.where(tril, lax.dot_general(q.astype(bf16), kb, nt, preferred_element_type=f32) * d, 0.0).astype(bf16)
             for q, kb, d in zip(qs, kbs, decays)]
    t_invs, pws = [eye - m for m in ms], ms
    for _ in range(5):
        pws = [_dot3(pw, pw) for pw in pws]
        t_invs = [_dot3(t, eye + pw) for t, pw in zip(t_invs, pws)]
    egcs = [jnp.exp(gc) for gc in gcs]
    tbs = [t.astype(bf16) for t in t_invs]
    us = [jnp.dot(tb, (v * b).astype(bf16), preferred_element_type=f32) for tb, v, b in zip(tbs, vs_, betas)]
    ws = [jnp.dot(tb, (kbeta * egc).astype(bf16), preferred_element_type=f32).astype(bf16)
          for tb, kbeta, egc in zip(tbs, kbetas, egcs)]
    states = [st_ref[hh] for hh in range(HS)]
    for c in range(n_chunks):
        for hh in range(HS):
            u, rs = hh * n_chunks + c, slice(c * C, (c + 1) * C)
            g_last = gcs[u][C - 1:C, :]
            q_dec = (qs[u] * egcs[u]).astype(bf16)
            k_dec = (ks_[u] * jnp.exp(g_last - gcs[u])).astype(bf16)
            sb = states[hh].astype(bf16)
            v_new = us[u] - jnp.dot(ws[u], sb, preferred_element_type=f32)
            vnb = v_new.astype(bf16)
            o = jnp.dot(q_dec, sb, preferred_element_type=f32) + jnp.dot(attns[u], vnb, preferred_element_type=f32)
            states[hh] = states[hh] * jnp.exp(g_last) + jnp.dot(k_dec.T, vnb, preferred_element_type=f32)
            o = o * lax.rsqrt(jnp.mean(o * o, axis=-1, keepdims=True) + NORM_EPS) * ng_ref[...]
            zt = z_ref[0, rs, hh * Dh:(hh + 1) * Dh]
            o_ref[0, rs, hh * Dh:(hh + 1) * Dh] = (o * (zt * jax.nn.sigmoid(zt))).astype(o_ref.dtype)
    for hh in range(HS):
        st_ref[hh] = states[hh]


def deltanet_mixer_pallas(proj3, conv_w, a_log, dt_bias, norm_g, B, S, T=256, HS=4):
    H, Dh = DN_HEADS, HEAD_DIM
    W = HS * Dh
    assert S % T == 0 and T % DN_CHUNK == 0 and H % HS == 0
    blk = lambda j: pl.BlockSpec((1, T, W), lambda b, h, t: (b, t, (OFF_DN + j * D_GRP) // W + h))
    prev = lambda j: pl.BlockSpec((1, SUB, W), lambda b, h, t: (b, jnp.maximum(t * (T // SUB) - 1, 0),
                                                               (OFF_DN + j * D_GRP) // W + h))
    per_head = pl.BlockSpec((HS, SUB, Dh), lambda b, h, t: (h, 0, 0))
    bcast = lambda x: jnp.broadcast_to(x.astype(f32)[:, None, None], (H, SUB, Dh))
    return pl.pallas_call(
        functools.partial(_deltanet_body, T=T, HS=HS),
        grid=(B, H // HS, S // T),
        in_specs=[blk(0), blk(1), blk(2), blk(3), prev(0), prev(1), prev(2),
                  pl.BlockSpec((1, T, LANES), lambda b, h, t: (b, t, OFF_SMALL // LANES)),
                  pl.BlockSpec((CONV_WIDTH, 3, W), lambda b, h, t: (0, 0, h)),
                  per_head, per_head,
                  pl.BlockSpec((1, Dh), lambda b, h, t: (0, 0))],
        out_specs=pl.BlockSpec((1, T, W), lambda b, h, t: (b, t, h)),
        out_shape=jax.ShapeDtypeStruct((B, S, D_GRP), bf16),
        scratch_shapes=[pltpu.VMEM((HS, Dh, Dh), f32)],
        compiler_params=pltpu.CompilerParams(dimension_semantics=("parallel", "parallel", "arbitrary"),
                                             vmem_limit_bytes=VMEM_LIMIT),
        name="deltanet",
    )(proj3, proj3, proj3, proj3, proj3, proj3, proj3, proj3, conv_w.reshape(CONV_WIDTH, 3, D_GRP),
      bcast(a_log), bcast(dt_bias), norm_g.reshape(1, Dh))


def _w_in_pieces():
    n0, r0, d0, w0 = 0, NSA_COLS, NSA_COLS + RET_COLS, NSA_COLS + RET_COLS + DN_COLS
    n_gate = 3 * NSA_HEADS
    return [
        (OFF_RW, w0, 3 * D_GRP),
        (OFF_RET, r0, 4 * D_GRP),
        (OFF_DN, d0, 3 * D_GRP),
        (OFF_DN + 3 * D_GRP, d0 + 3 * D_GRP + 2 * DN_HEADS, D_GRP),
        (OFF_NSA_Q, n0, D_GRP + 6 * NSA_KV_W),
        (OFF_LORA, w0 + 3 * D_GRP, RW_LORA),
        (OFF_SMALL, n0 + D_GRP + 6 * NSA_KV_W, n_gate),
        (OFF_SMALL + n_gate, d0 + 3 * D_GRP, 2 * DN_HEADS),
    ]


def _pack_w_in_body(w_ref, o_ref):
    rows = o_ref.shape[0]
    for off, width in ((OFF_LORA + RW_LORA, RW_LORA_PAD - RW_LORA), (OFF_SMALL, LANES)):
        o_ref[:, off:off + width] = jnp.zeros((rows, width), o_ref.dtype)
    for dst, src, width in _w_in_pieces():
        o_ref[:, dst:dst + width] = w_ref[0, :, src:src + width].astype(o_ref.dtype)


def pack_w_in(w_in, layer, tr=128):
    K = w_in.shape[1]
    return pl.pallas_call(
        _pack_w_in_body,
        grid=(K // tr,),
        in_specs=[pl.BlockSpec((1, tr, P_TOTAL), lambda i: (layer, i, 0))],
        out_specs=pl.BlockSpec((tr, P_PAD), lambda i: (i, 0)),
        out_shape=jax.ShapeDtypeStruct((K, P_PAD), bf16),
        compiler_params=pltpu.CompilerParams(dimension_semantics=("parallel",), vmem_limit_bytes=VMEM_LIMIT),
        name="pack_w_in",
    )(w_in)


def hybrid_mixer(h, hn, B, S, layer, w_in, w_out, ck_w1, ck_w2, ck_pe, cv_w1, cv_w2, cv_pe, dn_conv, dn_a_log,
                 dn_dt_bias, dn_norm, rw_mu, rw_w0, rw_w2, rw_a0, rw_a2, rw_g2, rw_kk, rw_ka, rw_rk, rw_ln_g, rw_ln_b):
    M = B * S
    proj = matmul(hn, pack_w_in(w_in, layer), name="in_proj")
    p3 = proj.reshape(B, S, P_PAD)
    o_nsa = nsa_mixer_pallas(proj, ck_w1, ck_w2, ck_pe, cv_w1, cv_w2, cv_pe, B, S)
    o_ret = retention_mixer_pallas(p3, B, S)
    o_dn = deltanet_mixer_pallas(p3, dn_conv, dn_a_log, dn_dt_bias, dn_norm, B, S)
    o_rw = rwkv7_mixer(proj, OFF_RW // (3 * D_GRP), OFF_LORA // RW_LORA_PAD, rw_mu, rw_w0, rw_w2, rw_a0, rw_a2,
                       rw_g2, rw_kk, rw_ka, rw_rk, rw_ln_g, rw_ln_b, B, S)
    parts = [o.reshape(M, D_GRP) for o in (o_nsa, o_ret, o_dn, o_rw)]
    return out_proj_residual(parts, w_out.astype(bf16), h)


def kernel(x, p, norm_mix, w_in, w_out, nsa_ck_w1, nsa_ck_w2, nsa_ck_pe, nsa_cv_w1, nsa_cv_w2, nsa_cv_pe, dn_conv, dn_a_log, dn_dt_bias, dn_norm, rw_mu, rw_w0, rw_w2, rw_a0, rw_a2, rw_g2, rw_kk, rw_ka, rw_rk, rw_ln_g, rw_ln_b, norm_moe, moe_router_grp, moe_router_grp_b, moe_router_exp, moe_router_exp_b, moe_w_gate, moe_w_up, moe_w_down, norm_ple, ple_w, ple_gate, norm_final):
    B, S, D = x.shape
    M = B * S
    h = x.reshape(M, D)
    for i in range(DEPTH):
        hn = rms_norm_rows(h, norm_mix[i], bf16)
        h = hybrid_mixer(h, hn, B, S, i, w_in, w_out[i], nsa_ck_w1[i], nsa_ck_w2[i], nsa_ck_pe[i], nsa_cv_w1[i],
                         nsa_cv_w2[i], nsa_cv_pe[i], dn_conv[i], dn_a_log[i], dn_dt_bias[i], dn_norm[i],
                         rw_mu[i], rw_w0[i], rw_w2[i], rw_a0[i], rw_a2[i], rw_g2[i], rw_kk[i], rw_ka[i],
                         rw_rk[i], rw_ln_g[i], rw_ln_b[i])
        hn = rms_norm_rows(h, norm_moe[i], f32)
        h = h + hier_moe(hn, moe_router_grp[i], moe_router_grp_b[i], moe_router_exp[i], moe_router_exp_b[i],
                         moe_w_gate, moe_w_up, moe_w_down, i)
        hn = rms_norm_rows(h, norm_ple[i], bf16)
        h = ple_residual(hn, ple_gate[i].astype(bf16), p[i].reshape(M, PLE_DIM), ple_w[i].astype(bf16), h)
    return rms_norm_rows(h, norm_final, f32).reshape(B, S, D)
```

```python
import functools

import jax
import jax.numpy as jnp
import numpy as np
from jax import lax
from jax.experimental import pallas as pl
from jax.experimental.pallas import tpu as pltpu

D_MODEL = 4096
BATCH = 2
SEQ = 4096
DEPTH = 2
f32 = jnp.float32
bf16 = jnp.bfloat16
D_MIX = D_MODEL
D_GRP = D_MIX // 4
HEAD_DIM = 128
NORM_EPS = 1e-6
NSA_HEADS = D_GRP // HEAD_DIM
NSA_KV_HEADS = 2
NSA_GROUP = NSA_HEADS // NSA_KV_HEADS
NSA_KV_W = NSA_KV_HEADS * HEAD_DIM
CMP_LEN = 32
CMP_STRIDE = 16
SEL_LEN = 64
N_SEL = 16
N_LOCAL_FORCED = 2
FORCED_SCORE = 1e4
WINDOW = 512
SEL_QBLK = 64
WIN_QBLK = 128
ROPE_THETA = 500000.0
ROPE_DIMS = HEAD_DIM // 4
RET_HEADS = D_GRP // HEAD_DIM
RET_CHUNK = 128
RET_THETA = 10000.0
DN_HEADS = D_GRP // HEAD_DIM
DN_CHUNK = 64
CONV_WIDTH = 4
RW_HEAD_DIM = 64
RW_HEADS = D_GRP // RW_HEAD_DIM
RW_DECAY_LORA = 64
RW_AAA_LORA = 64
RW_GATE_LORA = 160
RW_LN_EPS = 64e-5
N_GROUPS = 4
EXPERTS_PER_GROUP = 8
N_EXPERTS = N_GROUPS * EXPERTS_PER_GROUP
TOP_K = 2
D_EXPERT = 768
ROW_BLOCK = 256
PLE_DIM = 256
NSA_COLS = D_GRP + 6 * NSA_KV_W + 3 * NSA_HEADS
RET_COLS = 4 * D_GRP
DN_COLS = 4 * D_GRP + 2 * DN_HEADS
RW_COLS = 3 * D_GRP + RW_DECAY_LORA + RW_AAA_LORA + RW_GATE_LORA
P_TOTAL = NSA_COLS + RET_COLS + DN_COLS + RW_COLS

RW_LORA = RW_DECAY_LORA + RW_AAA_LORA + RW_GATE_LORA
RW_LORA_PAD = 384
RW_GROUPS_PER_BODY = 4
LANES = 128
SUB = 8

OFF_RW = 0
OFF_RET = 3 * D_GRP
OFF_DN = OFF_RET + 4 * D_GRP
OFF_NSA_Q = OFF_DN + 4 * D_GRP
OFF_NSA_KV = OFF_NSA_Q + D_GRP
OFF_LORA = OFF_NSA_KV + 6 * NSA_KV_W
OFF_SMALL = OFF_LORA + RW_LORA_PAD
SMALL_DN_A = 32
P_PAD = OFF_SMALL + LANES
assert P_PAD == 14336 and OFF_LORA % RW_LORA_PAD == 0

VMEM_LIMIT = 56 * 1024 * 1024


def split_cols(x, sizes):
    return jnp.split(x, np.cumsum(sizes)[:-1].tolist(), axis=-1)


def rms_norm(x, g, eps=NORM_EPS):
    xf = x.astype(f32)
    y = xf * lax.rsqrt(jnp.mean(xf * xf, -1, keepdims=True) + eps)
    return (y * g.astype(f32)).astype(x.dtype)


def l2norm(x, eps=1e-6):
    return x * lax.rsqrt(jnp.sum(x * x, -1, keepdims=True) + eps)


def head_layer_norm(x, eps):
    mu = jnp.mean(x, -1, keepdims=True)
    return (x - mu) * lax.rsqrt(jnp.var(x, -1, keepdims=True) + eps)


def masked_softmax(s, mask):
    s = jnp.where(mask, s.astype(f32), -jnp.inf)
    m = jnp.max(s, -1, keepdims=True)
    m = jnp.where(jnp.isfinite(m), m, 0.0)
    e = jnp.where(mask, jnp.exp(s - m), 0.0)
    return e / jnp.maximum(jnp.sum(e, -1, keepdims=True), jnp.finfo(f32).tiny)


def rotary(x, freqs, rot_dims):
    S = x.shape[1]
    half = rot_dims // 2
    ang = jnp.arange(S, dtype=f32)[:, None] * freqs[None, :]
    ang = ang.reshape((S,) + (1,) * (x.ndim - 3) + (half,))
    cos, sin = jnp.cos(ang), jnp.sin(ang)
    x1 = x[..., :half].astype(f32)
    x2 = x[..., half:rot_dims].astype(f32)
    out = jnp.concatenate([x1 * cos - x2 * sin, x2 * cos + x1 * sin, x[..., rot_dims:].astype(f32)], -1)
    return out.astype(x.dtype)


def cmp_sel_overlap(S):
    n_cmp = (S - CMP_LEN) // CMP_STRIDE + 1
    c0 = np.arange(n_cmp)[:, None] * CMP_STRIDE
    s0 = np.arange(S // SEL_LEN)[None, :] * SEL_LEN
    ov = np.clip(np.minimum(c0 + CMP_LEN, s0 + SEL_LEN) - np.maximum(c0, s0), 0, None) / CMP_LEN
    return jnp.asarray(ov, f32)


def nsa_mixer(q, kc, vc, ks, vs, kw, vw, gate_logit, ck_w1, ck_w2, ck_pe, cv_w1, cv_w2, cv_pe):
    B, S, _ = q.shape
    G, R, Dh = NSA_KV_HEADS, NSA_GROUP, HEAD_DIM
    scale = Dh ** -0.5
    half = ROPE_DIMS // 2
    freqs = ROPE_THETA ** (-jnp.arange(half, dtype=f32) / half)
    q = rotary(q.reshape(B, S, G, R, Dh), freqs, ROPE_DIMS)
    kc, vc, ks, vs, kw, vw = [t.reshape(B, S, G, Dh) for t in (kc, vc, ks, vs, kw, vw)]
    kc, ks, kw = [rotary(t, freqs, ROPE_DIMS) for t in (kc, ks, kw)]
    pos = jnp.arange(S)
    n_cmp = (S - CMP_LEN) // CMP_STRIDE + 1
    blk_tok = np.arange(n_cmp)[:, None] * CMP_STRIDE + np.arange(CMP_LEN)[None, :]

    def compress(t, w1, w2, pe):
        blk = t[:, blk_tok] + pe[:, None, :]
        blk = blk.transpose(0, 3, 1, 2, 4).reshape(B, G, n_cmp, CMP_LEN * Dh)
        return jax.nn.silu(blk @ w1) @ w2

    k_cmp = compress(kc, ck_w1, ck_w2, ck_pe)
    v_cmp = compress(vc, cv_w1, cv_w2, cv_pe)
    s_cmp = jnp.einsum('bsgrd,bgnd->bgrsn', q, k_cmp) * scale
    cmp_mask = jnp.asarray(blk_tok[:, -1])[None, :] <= pos[:, None]
    p_cmp = masked_softmax(s_cmp, cmp_mask)
    o_cmp = jnp.einsum('bgrsn,bgnd->bsgrd', p_cmp, v_cmp.astype(f32))
    n_sel_blk = S // SEL_LEN
    k_top = min(N_SEL, n_sel_blk)
    imp = jnp.einsum('bgrsn,nj->bgsj', p_cmp, cmp_sel_overlap(S))
    blk = jnp.arange(n_sel_blk)[None, :]
    cur = (pos // SEL_LEN)[:, None]
    visible = blk <= cur
    forced = (blk == 0) | (visible & (blk > cur - N_LOCAL_FORCED))
    imp = jnp.where(forced, FORCED_SCORE, jnp.where(visible, imp, -1.0))
    sel_idx = lax.top_k(imp, k_top)[1]
    ks_blk = ks.reshape(B, n_sel_blk, SEL_LEN, G, Dh).transpose(0, 3, 1, 2, 4)
    vs_blk = vs.reshape(B, n_sel_blk, SEL_LEN, G, Dh).transpose(0, 3, 1, 2, 4)
    n_qb = S // SEL_QBLK
    q_b = q.reshape(B, n_qb, SEL_QBLK, G, R, Dh).transpose(1, 0, 2, 3, 4, 5)
    idx_b = sel_idx.reshape(B, G, n_qb, SEL_QBLK, k_top).transpose(2, 0, 1, 3, 4)
    qpos_b = pos.reshape(n_qb, SEL_QBLK)
    b_ix = jnp.arange(B)[:, None, None, None]
    g_ix = jnp.arange(G)[None, :, None, None]

    def sel_block(args):
        qb, ib, qp = args
        kb = ks_blk[b_ix, g_ix, ib].reshape(B, G, SEL_QBLK, k_top * SEL_LEN, Dh)
        vb = vs_blk[b_ix, g_ix, ib].reshape(B, G, SEL_QBLK, k_top * SEL_LEN, Dh)
        tok = (ib[..., None] * SEL_LEN + jnp.arange(SEL_LEN)).reshape(B, G, SEL_QBLK, k_top * SEL_LEN)
        s = jnp.einsum('bqgrd,bgqkd->bgrqk', qb, kb) * scale
        mask = tok[:, :, None] <= qp[None, None, None, :, None]
        pr = masked_softmax(s, mask)
        return jnp.einsum('bgrqk,bgqkd->bqgrd', pr, vb.astype(f32))

    o_sel = lax.map(sel_block, (q_b, idx_b, qpos_b))
    o_sel = o_sel.transpose(1, 0, 2, 3, 4, 5).reshape(B, S, G, R, Dh)
    n_wb = S // WIN_QBLK
    span = WINDOW + WIN_QBLK
    key_idx = np.arange(n_wb)[:, None] * WIN_QBLK + np.arange(span)[None, :]
    pad = ((0, 0), (WINDOW, 0), (0, 0), (0, 0))
    kb = jnp.pad(kw, pad)[:, key_idx]
    vb = jnp.pad(vw, pad)[:, key_idx]
    qb = q.reshape(B, n_wb, WIN_QBLK, G, R, Dh)
    s_win = jnp.einsum('bnqgrd,bnkgd->bgrnqk', qb, kb) * scale
    kpos = jnp.asarray(key_idx - WINDOW)[:, None, :]
    qpos = pos.reshape(n_wb, WIN_QBLK)[:, :, None]
    win_mask = (kpos >= 0) & (kpos <= qpos) & (kpos > qpos - WINDOW)
    p_win = masked_softmax(s_win, win_mask)
    o_win = jnp.einsum('bgrnqk,bnkgd->bnqgrd', p_win, vb.astype(f32)).reshape(B, S, G, R, Dh)
    gate = jax.nn.sigmoid(gate_logit.astype(f32)).reshape(B, S, G, R, 3)
    o = gate[..., 0:1] * o_cmp + gate[..., 1:2] * o_sel + gate[..., 2:3] * o_win
    return o.reshape(B, S, D_GRP)


def retention_mixer(q, k, v, g):
    B, S, _ = q.shape
    H, Dh, C = RET_HEADS, HEAD_DIM, RET_CHUNK
    n_c = S // C
    freqs = RET_THETA ** (-jnp.linspace(0.0, 1.0, Dh // 2, dtype=f32))
    heads = lambda t: t.reshape(B, S, H, Dh)
    q = rotary(heads(q), freqs, Dh).astype(f32)
    k = rotary(heads(k), freqs, Dh).astype(f32) * Dh ** -0.5
    v = heads(v).astype(f32)
    log_gamma = jnp.log1p(-(2.0 ** (-5.0 - jnp.arange(H, dtype=f32))))
    n = jnp.arange(C, dtype=f32)
    diff = n[:, None] - n[None, :]
    decay_in = jnp.where(diff >= 0, jnp.exp(log_gamma[:, None, None] * jnp.maximum(diff, 0.0)), 0.0)
    xi = jnp.exp(log_gamma[:, None] * (n + 1.0))
    zeta = jnp.exp(log_gamma[:, None] * (C - 1.0 - n))
    gamma_c = jnp.exp(log_gamma * C)
    chunks = lambda t: t.reshape(B, n_c, C, H, Dh).transpose(1, 0, 3, 2, 4)

    def step(state, inp):
        qc, kc, vc = inp
        inner = jnp.einsum('bhnd,bhmd->bhnm', qc, kc) * decay_in
        o = jnp.einsum('bhnm,bhme->bhne', inner, vc) + jnp.einsum('bhnd,bhde->bhne', qc, state) * xi[..., None]
        state = state * gamma_c[:, None, None] + jnp.einsum('bhmd,bhme->bhde', kc * zeta[..., None], vc)
        return state, o

    _, o = lax.scan(step, jnp.zeros((B, H, Dh, Dh), f32), (chunks(q), chunks(k), chunks(v)))
    o = head_layer_norm(o.transpose(1, 0, 3, 2, 4).reshape(B, S, H, Dh), NORM_EPS)
    return jax.nn.silu(g.astype(f32)) * o.reshape(B, S, D_GRP)


def deltanet_mixer(q, k, v, a, b, z, conv_w, a_log, dt_bias, norm_g):
    B, S, _ = q.shape
    H, Dh, C = DN_HEADS, HEAD_DIM, DN_CHUNK
    n_c = S // C
    qkv = jnp.concatenate([q, k, v], -1)
    qkv = lax.conv_general_dilated(qkv, conv_w[:, None, :], window_strides=(1,), padding=[(CONV_WIDTH - 1, 0)],
                                   dimension_numbers=('NWC', 'WIO', 'NWC'), feature_group_count=3 * D_GRP)
    qkv = jax.nn.silu(qkv.astype(f32))
    q, k, v = [t.reshape(B, S, H, Dh) for t in jnp.split(qkv, 3, -1)]
    q = l2norm(q) * Dh ** -0.5
    k = l2norm(k)
    beta = jax.nn.sigmoid(b.astype(f32))
    g = -jnp.exp(a_log.astype(f32)) * jax.nn.softplus(a.astype(f32) + dt_bias.astype(f32))
    chunks = lambda t: jnp.moveaxis(t.reshape((B, n_c, C) + t.shape[2:]), 3, 1)
    qc, kc, vc, bc = chunks(q), chunks(k), chunks(v), chunks(beta)
    gc = jnp.cumsum(chunks(g), -1)
    kbeta = kc * bc[..., None]
    vbeta = vc * bc[..., None]
    idx = jnp.arange(C)
    tril = idx[:, None] >= idx[None, :]
    strict = idx[:, None] > idx[None, :]
    decay = jnp.exp(jnp.where(tril, gc[..., :, None] - gc[..., None, :], -jnp.inf))
    m = jnp.where(strict, jnp.einsum('bhncd,bhnkd->bhnck', kbeta, kc) * decay, 0.0)
    eye = jnp.eye(C, dtype=f32)
    t_inv = lax.linalg.triangular_solve(eye + m, jnp.broadcast_to(eye, m.shape), left_side=True,
                                        lower=True, unit_diagonal=True)
    u = t_inv @ vbeta
    w = t_inv @ (kbeta * jnp.exp(gc)[..., None])
    attn_in = jnp.where(tril, jnp.einsum('bhncd,bhnkd->bhnck', qc, kc) * decay, 0.0)
    q_dec = qc * jnp.exp(gc)[..., None]
    k_dec = kc * jnp.exp(gc[..., -1:] - gc)[..., None]
    g_last = jnp.exp(gc[..., -1])

    def step(state, inp):
        u_i, w_i, a_i, qd_i, kd_i, gl_i = inp
        v_new = u_i - w_i @ state
        o = qd_i @ state + a_i @ v_new
        state = state * gl_i[..., None, None] + jnp.swapaxes(kd_i, -1, -2) @ v_new
        return state, o

    xs = tuple(jnp.moveaxis(t, 2, 0) for t in (u, w, attn_in, q_dec, k_dec, g_last))
    _, o = lax.scan(step, jnp.zeros((B, H, Dh, Dh), f32), xs)
    o = o.transpose(1, 0, 3, 2, 4).reshape(B, S, H, Dh)
    o = rms_norm(o, norm_g) * jax.nn.silu(z.astype(f32)).reshape(B, S, H, Dh)
    return o.reshape(B, S, D_GRP)


def _split2(x):
    h = x.astype(bf16)
    return h, (x - h.astype(f32)).astype(bf16)


def _split3(x):
    h = x.astype(bf16)
    r1 = x - h.astype(f32)
    m = r1.astype(bf16)
    return h, m, (r1 - m.astype(f32)).astype(bf16)


def _rwkv_prep_body(x_ref, xprev_ref, l_ref, lprev_ref, mu_ref, mul_ref, w0_ref, a0_ref, kkw_ref, ka_ref,
                    w2_ref, a2_ref, g2_ref, e_ref, r_ref, w_ref, k_ref, v_ref, kk_ref, b_ref, g_ref, *, tm, S):
    i = pl.program_id(0)
    first = (i * tm) % S == 0

    def token_mix(x, prev8, mu):
        rows = lax.broadcasted_iota(jnp.int32, x.shape, 0)
        prev_row = jnp.where(first, 0.0, prev8[SUB - 1:SUB, :])
        x_prev = jnp.where(rows == 0, prev_row, pltpu.roll(x, 1, axis=0))
        return x + (x_prev - x) * mu

    x = token_mix(x_ref[...], xprev_ref[...], mu_ref[...])
    lora = token_mix(l_ref[...], lprev_ref[...], mul_ref[...])
    r = x[:, 0:D_GRP]
    k = x[:, D_GRP:2 * D_GRP]
    v = x[:, 2 * D_GRP:3 * D_GRP]
    dw = jnp.dot(jnp.tanh(lora).astype(bf16), w2_ref[...], preferred_element_type=f32)
    da = jnp.dot(lora.astype(bf16), a2_ref[...], preferred_element_type=f32)
    g = jnp.dot(jax.nn.sigmoid(lora).astype(bf16), g2_ref[...], preferred_element_type=f32)
    w = -jax.nn.softplus(-(w0_ref[...] + dw)) - 0.5
    decay = jnp.exp(-jnp.exp(w))
    a = jax.nn.sigmoid(a0_ref[...] + da)
    kq = k * kkw_ref[...]
    sq_h, sq_l = _split2(kq * kq)
    ss = (jnp.dot(sq_h, e_ref[...], preferred_element_type=f32)
          + jnp.dot(sq_l, e_ref[...], preferred_element_type=f32))
    kk = kq * lax.rsqrt(ss + 1e-6)
    r_ref[...] = r
    w_ref[...] = decay
    k_ref[...] = k * (1.0 + (a - 1.0) * ka_ref[...])
    v_ref[...] = v
    kk_ref[...] = kk
    b_ref[...] = kk * a
    g_ref[...] = g


def rwkv_prep(proj, rkv_blk, lora_blk, mu, w0, w2, a0, a2, g2, k_k, k_a, S, tm=256):
    M = proj.shape[0]
    assert M % tm == 0 and S % tm == 0
    W3 = 3 * D_GRP
    mu_x = mu[:W3].reshape(1, W3)
    mu_l = jnp.pad(mu[W3:], (0, RW_LORA_PAD - RW_LORA)).reshape(1, RW_LORA_PAD)
    w2p = jnp.zeros((RW_LORA_PAD, D_GRP), f32).at[0:RW_DECAY_LORA].set(w2).astype(bf16)
    a2p = jnp.zeros((RW_LORA_PAD, D_GRP), f32).at[RW_DECAY_LORA:RW_DECAY_LORA + RW_AAA_LORA].set(a2).astype(bf16)
    g2p = jnp.zeros((RW_LORA_PAD, D_GRP), f32).at[RW_DECAY_LORA + RW_AAA_LORA:RW_LORA].set(g2).astype(bf16)
    hid = np.arange(D_GRP) // RW_HEAD_DIM
    e = jnp.asarray(hid[:, None] == hid[None, :], bf16)
    row = lambda t: t.reshape(1, D_GRP)
    vec = pl.BlockSpec((1, D_GRP), lambda i: (0, 0))
    wspec = pl.BlockSpec((RW_LORA_PAD, D_GRP), lambda i: (0, 0))
    ospec = pl.BlockSpec((tm, D_GRP), lambda i: (i, 0))
    prev_row_blk = lambda i: jnp.maximum(i * (tm // SUB) - 1, 0)
    return pl.pallas_call(
        functools.partial(_rwkv_prep_body, tm=tm, S=S),
        grid=(M // tm,),
        in_specs=[pl.BlockSpec((tm, W3), lambda i: (i, rkv_blk)),
                  pl.BlockSpec((SUB, W3), lambda i: (prev_row_blk(i), rkv_blk)),
                  pl.BlockSpec((tm, RW_LORA_PAD), lambda i: (i, lora_blk)),
                  pl.BlockSpec((SUB, RW_LORA_PAD), lambda i: (prev_row_blk(i), lora_blk)),
                  pl.BlockSpec((1, W3), lambda i: (0, 0)),
                  pl.BlockSpec((1, RW_LORA_PAD), lambda i: (0, 0)),
                  vec, vec, vec, vec, wspec, wspec, wspec,
                  pl.BlockSpec((D_GRP, D_GRP), lambda i: (0, 0))],
        out_specs=[ospec] * 7,
        out_shape=[jax.ShapeDtypeStruct((M, D_GRP), f32)] * 7,
        compiler_params=pltpu.CompilerParams(dimension_semantics=("parallel",), vmem_limit_bytes=VMEM_LIMIT),
        name="rwkv_prep",
    )(proj, proj, proj, proj, mu_x, mu_l, row(w0), row(a0), row(k_k), row(k_a), w2p, a2p, g2p, e)


def _rwkv_scan_body(r_ref, w_ref, k_ref, v_ref, kk_ref, b_ref, g_ref, lng_ref, lnb_ref, rk_ref, o_ref, s_ref,
                    vc_ref, y_ref, *, G, T):
    tb = pl.program_id(2)

    @pl.when(tb == 0)
    def _():
        s_ref[...] = jnp.zeros_like(s_ref)

    Dh = RW_HEAD_DIM
    lo = lax.broadcasted_iota(jnp.int32, (Dh, LANES), 1) < Dh
    lo8 = lax.broadcasted_iota(jnp.int32, (SUB, LANES), 1) < Dh
    sub_iota = lax.broadcasted_iota(jnp.int32, (4 * SUB, LANES), 0)
    row_iota = lax.broadcasted_iota(jnp.int32, (SUB, LANES), 0)

    def seg_sum(x, m):
        s_lo = jnp.sum(jnp.where(m, x, 0.0), axis=-1, keepdims=True)
        s_hi = jnp.sum(jnp.where(m, 0.0, x), axis=-1, keepdims=True)
        return jnp.where(m, s_lo, s_hi)

    cols = [slice(p * LANES, (p + 1) * LANES) for p in range(G)]
    n_groups = T // SUB

    k_iota = lax.broadcasted_iota(jnp.int32, (8 * SUB, LANES), 0)
    k_head = k_iota // (4 * SUB) == lax.broadcasted_iota(jnp.int32, (8 * SUB, LANES), 1) // Dh
    k_piece = (k_iota // SUB) % 4

    def v_pieces(gi):
        t0 = pl.multiple_of(gi * SUB, SUB)
        vts = []
        for p in range(G):
            vh, vm, vl = _split3(v_ref[0, pl.ds(t0, SUB), cols[p]])
            vp = jnp.concatenate([vh, vm, vl, jnp.zeros_like(vh)], axis=0)
            vts.append(jnp.concatenate([vp[:, :Dh], vp[:, Dh:]], axis=0).T)
        return vts

    def v_columns(vts, i, slot):
        sel = (k_head & (k_piece < 3) & (k_iota % SUB == i)).astype(bf16)
        for p in range(G):
            vc_ref[slot, i, p] = jnp.dot(vts[p], sel, preferred_element_type=f32)

    vts0 = v_pieces(0)
    for i in range(SUB):
        v_columns(vts0, i, 0)

    def run_group(gi, slot, states):
        t0 = pl.multiple_of(gi * SUB, SUB)
        vts_next = v_pieces(jnp.minimum(gi + 1, n_groups - 1))
        r8b = [r_ref[0, pl.ds(t0, SUB), cols[p]].astype(bf16) for p in range(G)]
        yacc = [jnp.zeros((SUB, LANES), f32) for _ in range(G)]
        tiles = [[ref[0, pl.ds(t0, SUB), cols[p]] for ref in (kk_ref, w_ref, b_ref, k_ref)] for p in range(G)]
        for i in range(SUB):
            for p in range(G):
                kk_r, w_r, b_r, k_r = [jnp.broadcast_to(t8[i:i + 1, :], (Dh, LANES)) for t8 in tiles[p]]
                S = states[p]
                sab = -seg_sum(S * kk_r, lo)
                states[p] = S * w_r + sab * b_r + vc_ref[slot, i, p] * k_r
            for p in range(G):
                Sb = states[p].astype(bf16)
                zero = jnp.zeros_like(Sb)
                mt = jnp.concatenate([jnp.where(lo, Sb, zero), jnp.where(lo, zero, Sb)], axis=0)
                y8 = lax.dot_general(r8b[p], mt, (((1,), (1,)), ((), ())), preferred_element_type=f32)
                yacc[p] = jnp.where(row_iota == i, y8, yacc[p])
            v_columns(vts_next, i, 1 - slot)
        for p in range(G):
            y_ref[pl.ds(t0, SUB), cols[p]] = yacc[p]
        return states

    def group_batch(gj, carry):
        states = [s_ref[p] for p in range(G)]
        for u in range(RW_GROUPS_PER_BODY):
            states = run_group(RW_GROUPS_PER_BODY * gj + u, u % 2, states)
        for p in range(G):
            s_ref[p] = states[p]
        return carry

    assert n_groups % RW_GROUPS_PER_BODY == 0 and RW_GROUPS_PER_BODY % 2 == 0
    lax.fori_loop(0, n_groups // RW_GROUPS_PER_BODY, group_batch, 0)

    lo_t = lax.broadcasted_iota(jnp.int32, (T, LANES), 1) < Dh
    for p in range(G):
        y = y_ref[:, cols[p]]
        mean = seg_sum(y, lo_t) * (1.0 / Dh)
        d = y - mean
        var = seg_sum(d * d, lo_t) * (1.0 / Dh)
        yn = d * lax.rsqrt(var + RW_LN_EPS) * lng_ref[:, cols[p]] + lnb_ref[:, cols[p]]
        bonus = seg_sum(r_ref[0, :, cols[p]] * k_ref[0, :, cols[p]] * rk_ref[:, cols[p]], lo_t) * v_ref[0, :, cols[p]]
        o_ref[0, :, cols[p]] = (yn + bonus) * g_ref[0, :, cols[p]]


def rwkv_scan(r, w, k, v, kk, b, g, ln_g, ln_b, r_k, G=8, T=256):
    B, S, D = r.shape
    n_pairs = D // LANES
    assert n_pairs % G == 0 and S % T == 0
    spec = pl.BlockSpec((1, T, G * LANES), lambda bi, pi, ti: (bi, ti, pi))
    vec = pl.BlockSpec((1, G * LANES), lambda bi, pi, ti: (0, pi))
    row = lambda t: t.reshape(1, D)
    return pl.pallas_call(
        functools.partial(_rwkv_scan_body, G=G, T=T),
        grid=(B, n_pairs // G, S // T),
        in_specs=[spec] * 7 + [vec] * 3,
        out_specs=spec,
        out_shape=jax.ShapeDtypeStruct((B, S, D), f32),
        scratch_shapes=[pltpu.VMEM((G, RW_HEAD_DIM, LANES), f32),
                        pltpu.VMEM((2, SUB, G, RW_HEAD_DIM, LANES), f32),
                        pltpu.VMEM((T, G * LANES), f32)],
        compiler_params=pltpu.CompilerParams(dimension_semantics=("parallel", "parallel", "arbitrary"),
                                             vmem_limit_bytes=VMEM_LIMIT),
        name="rwkv_scan",
    )(r, w, k, v, kk, b, g, row(ln_g), row(ln_b), row(r_k))


def rwkv7_mixer(proj, rkv_blk, lora_blk, mu, w0, w2, a0, a2, g2, k_k, k_a, r_k, ln_g, ln_b, B, S):
    outs = rwkv_prep(proj, rkv_blk, lora_blk, mu, w0, w2, a0, a2, g2, k_k, k_a, S)
    r, w, k, v, kk, b, g = [t.reshape(B, S, D_GRP) for t in outs]
    return rwkv_scan(r, w, k, v, kk, b, g, ln_g, ln_b, r_k.reshape(-1))


def _mm_nt_body(a_ref, bt_ref, o_ref, acc_ref):
    k = pl.program_id(2)

    @pl.when(k == 0)
    def _():
        acc_ref[...] = jnp.zeros_like(acc_ref)

    acc_ref[...] += lax.dot_general(a_ref[...], bt_ref[...], (((1,), (1,)), ((), ())), preferred_element_type=f32)

    @pl.when(k == pl.num_programs(2) - 1)
    def _():
        o_ref[...] = acc_ref[...].astype(o_ref.dtype)


def matmul_nt(a, bt, out_dtype=f32, tm=1024, tn=1024, tk=2048, name="matmul_nt"):
    M, K = a.shape
    N, _ = bt.shape
    assert M % tm == 0 and N % tn == 0 and K % tk == 0, (M, N, K, tm, tn, tk)
    return pl.pallas_call(
        _mm_nt_body,
        grid=(M // tm, N // tn, K // tk),
        in_specs=[pl.BlockSpec((tm, tk), lambda i, j, k: (i, k)),
                  pl.BlockSpec((tn, tk), lambda i, j, k: (j, k))],
        out_specs=pl.BlockSpec((tm, tn), lambda i, j, k: (i, j)),
        out_shape=jax.ShapeDtypeStruct((M, N), out_dtype),
        scratch_shapes=[pltpu.VMEM((tm, tn), f32)],
        compiler_params=pltpu.CompilerParams(
            dimension_semantics=("parallel", "parallel", "arbitrary"), vmem_limit_bytes=VMEM_LIMIT),
        name=name,
    )(a, bt)


def _rms_norm_body(x_ref, g_ref, o_ref):
    x = x_ref[...]
    y = x * lax.rsqrt(jnp.mean(x * x, axis=-1, keepdims=True) + NORM_EPS)
    o_ref[...] = (y * g_ref[...]).astype(o_ref.dtype)


def rms_norm_rows(x, g, out_dtype, tm=256):
    M, D = x.shape
    return pl.pallas_call(
        _rms_norm_body,
        grid=(M // tm,),
        in_specs=[pl.BlockSpec((tm, D), lambda i: (i, 0)), pl.BlockSpec((1, D), lambda i: (0, 0))],
        out_specs=pl.BlockSpec((tm, D), lambda i: (i, 0)),
        out_shape=jax.ShapeDtypeStruct((M, D), out_dtype),
        compiler_params=pltpu.CompilerParams(dimension_semantics=("parallel",), vmem_limit_bytes=VMEM_LIMIT),
        name="rms_norm",
    )(x, g.reshape(1, D).astype(f32))


def _out_proj_body(a0_ref, a1_ref, a2_ref, a3_ref, b_ref, h_ref, o_ref, acc_ref):
    k = pl.program_id(2)

    @pl.when(k == 0)
    def _():
        acc_ref[...] = h_ref[...]

    for j, a_ref in enumerate((a0_ref, a1_ref, a2_ref, a3_ref)):
        @pl.when(k == j)
        def _(a_ref=a_ref):
            acc_ref[...] += jnp.dot(a_ref[...].astype(bf16), b_ref[...], preferred_element_type=f32)

    @pl.when(k == pl.num_programs(2) - 1)
    def _():
        o_ref[...] = acc_ref[...]


def out_proj_residual(parts, w, h, tm=1024, tn=1024):
    M, D = h.shape
    aspec = pl.BlockSpec((tm, D_GRP), lambda i, j, k: (i, 0))
    return pl.pallas_call(
        _out_proj_body,
        grid=(M // tm, D // tn, len(parts)),
        in_specs=[aspec] * 4 + [pl.BlockSpec((D_GRP, tn), lambda i, j, k: (k, j)),
                                pl.BlockSpec((tm, tn), lambda i, j, k: (i, j))],
        out_specs=pl.BlockSpec((tm, tn), lambda i, j, k: (i, j)),
        out_shape=jax.ShapeDtypeStruct((M, D), f32),
        scratch_shapes=[pltpu.VMEM((tm, tn), f32)],
        compiler_params=pltpu.CompilerParams(
            dimension_semantics=("parallel", "parallel", "arbitrary"), vmem_limit_bytes=VMEM_LIMIT),
        name="out_proj",
    )(*parts, w, h)


def _ple_body(a_ref, b_ref, p_ref, pw_ref, h_ref, o_ref, acc_ref):
    k = pl.program_id(2)

    @pl.when(k == 0)
    def _():
        acc_ref[...] = jnp.zeros_like(acc_ref)

    acc_ref[...] += jnp.dot(a_ref[...], b_ref[...], preferred_element_type=f32)

    @pl.when(k == pl.num_programs(2) - 1)
    def _():
        emb = jnp.dot(p_ref[...].astype(bf16), pw_ref[...], preferred_element_type=f32)
        o_ref[...] = h_ref[...] + emb * jax.nn.sigmoid(acc_ref[...])


def ple_residual(hn, gate_w, p, ple_w, h, tm=1024, tn=1024, tk=2048):
    M, D = h.shape
    return pl.pallas_call(
        _ple_body,
        grid=(M // tm, D // tn, D // tk),
        in_specs=[pl.BlockSpec((tm, tk), lambda i, j, k: (i, k)),
                  pl.BlockSpec((tk, tn), lambda i, j, k: (k, j)),
                  pl.BlockSpec((tm, PLE_DIM), lambda i, j, k: (i, 0)),
                  pl.BlockSpec((PLE_DIM, tn), lambda i, j, k: (0, j)),
                  pl.BlockSpec((tm, tn), lambda i, j, k: (i, j))],
        out_specs=pl.BlockSpec((tm, tn), lambda i, j, k: (i, j)),
        out_shape=jax.ShapeDtypeStruct((M, D), f32),
        scratch_shapes=[pltpu.VMEM((tm, tn), f32)],
        compiler_params=pltpu.CompilerParams(
            dimension_semantics=("parallel", "parallel", "arbitrary"), vmem_limit_bytes=VMEM_LIMIT),
        name="ple",
    )(hn, gate_w, p, ple_w, h)


def _moe_router_body(x_ref, w_ref, b_ref, id_ref, wt_ref):
    logits = jnp.dot(x_ref[...].astype(bf16), w_ref[...], preferred_element_type=f32) + b_ref[...]
    lane = lax.broadcasted_iota(jnp.int32, logits.shape, 1).astype(f32)
    ninf = -jnp.inf

    def top1(vals):
        m = jnp.max(vals, axis=-1, keepdims=True)
        idx = jnp.min(jnp.where(vals == m, lane, float(LANES)), axis=-1, keepdims=True)
        return m, idx

    gl = jnp.where(lane < N_GROUPS, logits, ninf)
    g_max, g_idx = top1(gl)
    g_w = 1.0 / jnp.sum(jnp.exp(gl - g_max), axis=-1, keepdims=True)
    lo = N_GROUPS + g_idx * EXPERTS_PER_GROUP
    el = jnp.where((lane >= lo) & (lane < lo + EXPERTS_PER_GROUP), logits, ninf)
    e_max, _ = top1(el)
    pe = jnp.exp(el - e_max)
    p = pe / jnp.sum(pe, axis=-1, keepdims=True)
    p = jnp.where(el == ninf, ninf, p)
    p1, i1 = top1(p)
    p2, i2 = top1(jnp.where(lane == i1, ninf, p))
    scale = g_w / (p1 + p2)
    id_ref[...] = jnp.where(lane == 0, i1 - N_GROUPS, jnp.where(lane == 1, i2 - N_GROUPS, 0.0)).astype(jnp.int32)
    wt_ref[...] = jnp.where(lane == 0, p1 * scale, jnp.where(lane == 1, p2 * scale, 0.0))


def moe_router(xb, router_grp, router_grp_b, router_exp, router_exp_b, tm=512):
    M, D = xb.shape
    n_log = N_GROUPS + N_EXPERTS
    w = jnp.pad(jnp.concatenate([router_grp, router_exp], axis=1), ((0, 0), (0, LANES - n_log))).astype(bf16)
    b = jnp.pad(jnp.concatenate([router_grp_b, router_exp_b]), (0, LANES - n_log)).reshape(1, LANES).astype(f32)
    ids, wts = pl.pallas_call(
        _moe_router_body,
        grid=(M // tm,),
        in_specs=[pl.BlockSpec((tm, D), lambda i: (i, 0)), pl.BlockSpec((D, LANES), lambda i: (0, 0)),
                  pl.BlockSpec((1, LANES), lambda i: (0, 0))],
        out_specs=[pl.BlockSpec((tm, LANES), lambda i: (i, 0))] * 2,
        out_shape=[jax.ShapeDtypeStruct((M, LANES), jnp.int32), jax.ShapeDtypeStruct((M, LANES), f32)],
        compiler_params=pltpu.CompilerParams(dimension_semantics=("parallel",), vmem_limit_bytes=VMEM_LIMIT),
        name="moe_router",
    )(xb, w, b)
    return ids[:, :TOP_K], wts[:, :TOP_K]


def _moe_up_body(be_ref, nu_ref, x_ref, wg_ref, wu_ref, h_ref):
    i = pl.program_id(1)

    @pl.when(i < nu_ref[0])
    def _():
        x = x_ref[...].astype(bf16)
        hg = jnp.dot(x, wg_ref[0, 0].astype(bf16), preferred_element_type=f32)
        hu = jnp.dot(x, wu_ref[0, 0].astype(bf16), preferred_element_type=f32)
        h_ref[...] = (hg * jax.nn.sigmoid(hg) * hu).astype(bf16)

    @pl.when(i >= nu_ref[0])
    def _():
        h_ref[...] = jnp.zeros_like(h_ref)


def _moe_down_body(be_ref, nu_ref, h_ref, wd_ref, wt_ref, y_ref):
    i = pl.program_id(0)

    @pl.when(i < nu_ref[0])
    def _():
        y_ref[...] = jnp.dot(h_ref[...], wd_ref[0, 0].astype(bf16), preferred_element_type=f32) * wt_ref[...]

    @pl.when(i >= nu_ref[0])
    def _():
        y_ref[...] = jnp.zeros_like(y_ref)


def moe_experts(x_pad, w_pad, blk_expert, n_used, wg, wu, wd, layer, tf=384):
    A_pad, D = x_pad.shape
    n_blk = A_pad // ROW_BLOCK
    nf = D_EXPERT // tf
    h = pl.pallas_call(
        _moe_up_body,
        grid_spec=pltpu.PrefetchScalarGridSpec(
            num_scalar_prefetch=2,
            grid=(nf, n_blk),
            in_specs=[pl.BlockSpec((ROW_BLOCK, D), lambda f, i, be, nu: (i, 0)),
                      pl.BlockSpec((1, 1, D, tf), lambda f, i, be, nu: (layer, be[i], 0, f)),
                      pl.BlockSpec((1, 1, D, tf), lambda f, i, be, nu: (layer, be[i], 0, f))],
            out_specs=pl.BlockSpec((ROW_BLOCK, tf), lambda f, i, be, nu: (i, f)),
        ),
        out_shape=jax.ShapeDtypeStruct((A_pad, D_EXPERT), bf16),
        compiler_params=pltpu.CompilerParams(
            dimension_semantics=("arbitrary", "arbitrary"), vmem_limit_bytes=VMEM_LIMIT),
        name="moe_up",
    )(blk_expert, n_used, x_pad, wg, wu)
    return pl.pallas_call(
        _moe_down_body,
        grid_spec=pltpu.PrefetchScalarGridSpec(
            num_scalar_prefetch=2,
            grid=(n_blk,),
            in_specs=[pl.BlockSpec((ROW_BLOCK, D_EXPERT), lambda i, be, nu: (i, 0)),
                      pl.BlockSpec((1, 1, D_EXPERT, D), lambda i, be, nu: (layer, be[i], 0, 0)),
                      pl.BlockSpec((ROW_BLOCK, 1), lambda i, be, nu: (i, 0))],
            out_specs=pl.BlockSpec((ROW_BLOCK, D), lambda i, be, nu: (i, 0)),
        ),
        out_shape=jax.ShapeDtypeStruct((A_pad, D), f32),
        compiler_params=pltpu.CompilerParams(dimension_semantics=("arbitrary",), vmem_limit_bytes=VMEM_LIMIT),
        name="moe_down",
    )(blk_expert, n_used, h, wd, w_pad.reshape(A_pad, 1))


def hier_moe(xb, router_grp, router_grp_b, router_exp, router_exp_b, w_gate, w_up, w_down, layer):
    M, D = xb.shape
    expert_ids, weights = moe_router(xb, router_grp, router_grp_b, router_exp, router_exp_b)
    A = M * TOP_K
    e_flat = expert_ids.reshape(A)
    onehot = (e_flat[:, None] == jnp.arange(N_EXPERTS)[None, :]).astype(jnp.int32)
    rank = jnp.take_along_axis(jnp.cumsum(onehot, axis=0), e_flat[:, None], axis=1)[:, 0] - 1
    counts = jnp.sum(onehot, axis=0)
    padded = (counts + ROW_BLOCK - 1) // ROW_BLOCK * ROW_BLOCK
    end_pad = jnp.cumsum(padded)
    dest = (end_pad - padded)[e_flat] + rank
    n_blk = -(-A // ROW_BLOCK) + N_EXPERTS
    A_pad = n_blk * ROW_BLOCK
    tok_pad = (jnp.arange(A_pad, dtype=jnp.int32) % M).at[dest].set(jnp.arange(A, dtype=jnp.int32) // TOP_K)
    w_pad = jnp.zeros((A_pad,), f32).at[dest].set(weights.reshape(A))
    blk_expert = jnp.minimum(jnp.searchsorted(end_pad, jnp.arange(n_blk) * ROW_BLOCK, side='right'),
                             N_EXPERTS - 1).astype(jnp.int32)
    n_used = (end_pad[-1:] // ROW_BLOCK).astype(jnp.int32)
    x_pad = xb[tok_pad]
    y_pad = moe_experts(x_pad, w_pad, blk_expert, n_used, w_gate, w_up, w_down, layer)
    rows = dest.reshape(M, TOP_K)
    return y_pad[rows[:, 0]] + y_pad[rows[:, 1]]


def _rope_tables(S, theta, rot_dims, width):
    half = rot_dims // 2
    freqs = theta ** (-np.arange(half, dtype=np.float64) / half)
    ang = np.arange(S, dtype=np.float64)[:, None] * freqs[None, :]
    cos = np.ones((S, width), np.float64)
    sin_lo = np.zeros((S, width), np.float64)
    sin_hi = np.zeros((S, width), np.float64)
    cos[:, :half] = np.cos(ang)
    cos[:, half:rot_dims] = np.cos(ang)
    sin_lo[:, :half] = -np.sin(ang)
    sin_hi[:, half:rot_dims] = np.sin(ang)
    return [jnp.asarray(t, f32) for t in (cos, sin_lo, sin_hi)]


def _rope128(x, cos, sin_lo, sin_hi, half):
    return x * cos + pltpu.roll(x, LANES - half, axis=1) * sin_lo + pltpu.roll(x, half, axis=1) * sin_hi


def _nsa_prep_body(q_ref, kv_ref, cos_ref, sl_ref, sh_ref, qo_ref, kvo_ref, co_ref):
    cos, sl, sh = cos_ref[...], sl_ref[...], sh_ref[...]
    half = ROPE_DIMS // 2
    scale = HEAD_DIM ** -0.5
    for h in range(NSA_HEADS):
        c = slice(h * LANES, (h + 1) * LANES)
        qo_ref[:, c] = (_rope128(q_ref[:, c], cos, sl, sh, half) * scale).astype(bf16)
    for j in range(6):
        for g in range(NSA_KV_HEADS):
            c = slice((j * NSA_KV_HEADS + g) * LANES, (j * NSA_KV_HEADS + g + 1) * LANES)
            x = kv_ref[:, c]
            if j % 2 == 0:
                x = _rope128(x, cos, sl, sh, half)
            if j < 2:
                co_ref[:, c] = x.astype(bf16)
            else:
                kvo_ref[:, slice(c.start - 4 * LANES, c.stop - 4 * LANES)] = x.astype(bf16)


def nsa_prep(proj, S, tm=512):
    M = proj.shape[0]
    assert M % tm == 0 and S % tm == 0
    tabs = _rope_tables(S, ROPE_THETA, ROPE_DIMS, LANES)
    tspec = pl.BlockSpec((tm, LANES), lambda i: (i % (S // tm), 0))
    return pl.pallas_call(
        _nsa_prep_body,
        grid=(M // tm,),
        in_specs=[pl.BlockSpec((tm, D_GRP), lambda i: (i, OFF_NSA_Q // D_GRP)),
                  pl.BlockSpec((tm, 6 * NSA_KV_W), lambda i: (i, OFF_NSA_KV // (6 * NSA_KV_W))),
                  tspec, tspec, tspec],
        out_specs=[pl.BlockSpec((tm, D_GRP), lambda i: (i, 0)),
                   pl.BlockSpec((tm, 4 * NSA_KV_W), lambda i: (i, 0)),
                   pl.BlockSpec((tm, 2 * NSA_KV_W), lambda i: (i, 0))],
        out_shape=[jax.ShapeDtypeStruct((M, D_GRP), bf16), jax.ShapeDtypeStruct((M, 4 * NSA_KV_W), bf16),
                   jax.ShapeDtypeStruct((M, 2 * NSA_KV_W), bf16)],
        compiler_params=pltpu.CompilerParams(dimension_semantics=("parallel",), vmem_limit_bytes=VMEM_LIMIT),
        name="nsa_prep",
    )(proj, proj, *tabs)


def _nsa_compress_body(r_ref, w1_ref, w2_ref, pe_ref, o_ref):
    r = r_ref[0, 0, 0]
    half_w = CMP_STRIDE * HEAD_DIM
    a = jnp.dot(r, w1_ref[0, :half_w, :], preferred_element_type=f32)
    b = jnp.dot(r, w1_ref[0, half_w:, :], preferred_element_type=f32)
    n = r.shape[0]
    pe = jnp.dot(jnp.broadcast_to(pe_ref[0], (SUB, pe_ref.shape[-1])), w1_ref[0], preferred_element_type=f32)[0:1]
    pre = a + pltpu.roll(b, n - 1, axis=0) + pe
    hid = pre * jax.nn.sigmoid(pre)
    o_ref[0, 0, 0] = jnp.dot(hid.astype(bf16), w2_ref[0], preferred_element_type=f32).astype(bf16)


def nsa_compress(c_rot, w1s, w2s, pes, B, S):
    G, Dh = NSA_KV_HEADS, HEAD_DIM
    nc = S // CMP_STRIDE
    r = c_rot.reshape(B, nc, CMP_STRIDE, 2, G, Dh).transpose(3, 0, 4, 1, 2, 5).reshape(2, B, G, nc, CMP_STRIDE * Dh)
    w1 = jnp.stack(w1s).astype(bf16)
    w2 = jnp.stack(w2s).astype(bf16)
    pe = jnp.stack(pes).reshape(2, 1, CMP_LEN * Dh).astype(bf16)
    return pl.pallas_call(
        _nsa_compress_body,
        grid=(2, B, G),
        in_specs=[pl.BlockSpec((1, 1, 1, nc, CMP_STRIDE * Dh), lambda j, b, g: (j, b, g, 0, 0)),
                  pl.BlockSpec((1, CMP_LEN * Dh, Dh), lambda j, b, g: (j, 0, 0)),
                  pl.BlockSpec((1, Dh, Dh), lambda j, b, g: (j, 0, 0)),
                  pl.BlockSpec((1, 1, CMP_LEN * Dh), lambda j, b, g: (j, 0, 0))],
        out_specs=pl.BlockSpec((1, 1, 1, nc, Dh), lambda j, b, g: (j, b, g, 0, 0)),
        out_shape=jax.ShapeDtypeStruct((2, B, G, nc, Dh), bf16),
        compiler_params=pltpu.CompilerParams(dimension_semantics=("parallel", "parallel", "parallel"),
                                             vmem_limit_bytes=VMEM_LIMIT),
        name="nsa_compress",
    )(r, w1, w2, pe)


def _softmax_rows(s, mask):
    s = jnp.where(mask, s, -jnp.inf)
    m = jnp.max(s, axis=-1, keepdims=True)
    m = jnp.where(m == -jnp.inf, 0.0, m)
    e = jnp.exp(s - m)
    return e, jnp.maximum(jnp.sum(e, axis=-1, keepdims=True), jnp.finfo(f32).tiny)


def _nsa_attn_body(q_ref, cmp_ref, ks_ref, vs_ref, kw_ref, vw_ref, gate_ref, ovt_ref, o_ref, *, tq, tk, S):
    g = pl.program_id(1)
    qi = pl.program_id(2)
    R, Dh = NSA_GROUP, HEAD_DIM
    rows = R * tq
    n_blk = S // SEL_LEN
    n_cmp = S // CMP_STRIDE - 1
    q0 = qi * tq
    q4 = jnp.concatenate([q_ref[0, :, r * Dh:(r + 1) * Dh] for r in range(R)], axis=0)
    nt = (((1,), (1,)), ((), ()))

    def qpos(shape):
        return q0 + lax.broadcasted_iota(jnp.int32, shape, 0) % tq

    k_cmp, v_cmp = cmp_ref[0, 0, 0], cmp_ref[1, 0, 0]
    ncp = k_cmp.shape[0]
    s = lax.dot_general(q4, k_cmp, nt, preferred_element_type=f32)
    n_idx = lax.broadcasted_iota(jnp.int32, (rows, ncp), 1)
    mask = (n_idx * CMP_STRIDE + (CMP_LEN - 1) <= qpos((rows, ncp))) & (n_idx < n_cmp)
    e, den = _softmax_rows(s, mask)
    p = e / den
    o_cmp = jnp.dot(p.astype(bf16), v_cmp, preferred_element_type=f32)

    psum = p[0:tq]
    for r in range(1, R):
        psum = psum + p[r * tq:(r + 1) * tq]
    ph, plo = _split2(psum)
    imp = (lax.dot_general(ovt_ref[...], ph, nt, preferred_element_type=f32)
           + lax.dot_general(ovt_ref[...], plo, nt, preferred_element_type=f32))
    blk = lax.broadcasted_iota(jnp.int32, (n_blk, tq), 0)
    cur = (q0 + lax.broadcasted_iota(jnp.int32, (n_blk, tq), 1)) // SEL_LEN
    visible = blk <= cur
    forced = (blk == 0) | (visible & (blk > cur - N_LOCAL_FORCED))
    imp = jnp.where(forced, FORCED_SCORE, jnp.where(visible, imp, -1.0))
    sel_t = jnp.zeros((n_blk, tq), f32)
    blk_f = blk.astype(f32)
    for _ in range(min(N_SEL, n_blk)):
        m = jnp.max(imp, axis=0, keepdims=True)
        first = jnp.min(jnp.where(imp == m, blk_f, float(n_blk)), axis=0, keepdims=True)
        hit = blk_f == first
        sel_t = jnp.where(hit, 1.0, sel_t)
        imp = jnp.where(hit, -jnp.inf, imp)
    sel4 = jnp.concatenate([sel_t.T.astype(bf16)] * R, axis=0)

    blocks_per_tile = tk // SEL_LEN

    def sel_step(kt, carry):
        m_run, l_run, acc = carry
        k0 = pl.multiple_of(kt * tk, tk)
        kt_k = ks_ref[0, pl.ds(k0, tk), :]
        kt_v = vs_ref[0, pl.ds(k0, tk), :]
        s = lax.dot_general(q4, kt_k, nt, preferred_element_type=f32)
        expand = (lax.broadcasted_iota(jnp.int32, (n_blk, tk), 0)
                  == kt * blocks_per_tile + lax.broadcasted_iota(jnp.int32, (n_blk, tk), 1) // SEL_LEN)
        chosen = jnp.dot(sel4, expand.astype(bf16), preferred_element_type=f32)
        kpos = k0 + lax.broadcasted_iota(jnp.int32, (rows, tk), 1)
        mask = (chosen > 0.5) & (kpos <= qpos((rows, tk)))
        s = jnp.where(mask, s, -jnp.inf)
        m_new = jnp.maximum(m_run, jnp.max(s, axis=-1, keepdims=True))
        m_safe = jnp.where(m_new == -jnp.inf, 0.0, m_new)
        alpha = jnp.exp(m_run - m_safe)
        e = jnp.exp(s - m_safe)
        l_new = alpha * l_run + jnp.sum(e, axis=-1, keepdims=True)
        acc = alpha * acc + jnp.dot(e.astype(bf16), kt_v, preferred_element_type=f32)
        return m_new, l_new, acc

    n_kt = (q0 + tq - 1) // tk + 1
    init = (jnp.full((rows, 1), -jnp.inf, f32), jnp.zeros((rows, 1), f32), jnp.zeros((rows, Dh), f32))
    _, l_sel, acc_sel = lax.fori_loop(0, n_kt, sel_step, init)
    o_sel = acc_sel / jnp.maximum(l_sel, jnp.finfo(f32).tiny)

    span = WINDOW + tq
    w0 = pl.multiple_of(jnp.maximum(q0 - WINDOW, 0), tq)
    kwin = kw_ref[0, pl.ds(w0, span), :]
    vwin = vw_ref[0, pl.ds(w0, span), :]
    s = lax.dot_general(q4, kwin, nt, preferred_element_type=f32)
    kpos = w0 + lax.broadcasted_iota(jnp.int32, (rows, span), 1)
    qp = qpos((rows, span))
    e, den = _softmax_rows(s, (kpos <= qp) & (kpos > qp - WINDOW))
    o_win = jnp.dot(e.astype(bf16), vwin, preferred_element_type=f32) / den

    gate = jax.nn.sigmoid(gate_ref[0])
    for r in range(R):
        c0 = (g * R + r) * 3
        lane = lax.broadcasted_iota(jnp.int32, (tq, LANES), 1)
        gsel = lambda c: jnp.sum(jnp.where(lane == c0 + c, gate, 0.0), axis=-1, keepdims=True)
        rs = slice(r * tq, (r + 1) * tq)
        o_ref[0, :, r * Dh:(r + 1) * Dh] = (gsel(0) * o_cmp[rs] + gsel(1) * o_sel[rs]
                                            + gsel(2) * o_win[rs]).astype(o_ref.dtype)


def nsa_attention(q_rot, kv_rot, cmp_kv, proj3, B, S, tq=256, tk=512):
    G, R, Dh = NSA_KV_HEADS, NSA_GROUP, HEAD_DIM
    n_blk = S // SEL_LEN
    assert S % tk == 0 and tk % tq == 0 and S >= WINDOW + tq and n_blk % SUB == 0
    nc = S // CMP_STRIDE
    c0 = np.arange(nc - 1)[None, :] * CMP_STRIDE
    s0 = np.arange(n_blk)[:, None] * SEL_LEN
    ovt = np.zeros((n_blk, nc), np.float32)
    ovt[:, :nc - 1] = np.clip(np.minimum(c0 + CMP_LEN, s0 + SEL_LEN) - np.maximum(c0, s0), 0, None) / CMP_LEN
    kvspec = lambda j: pl.BlockSpec((1, S, Dh), lambda b, g, i: (b, 0, j * G + g))
    return pl.pallas_call(
        functools.partial(_nsa_attn_body, tq=tq, tk=tk, S=S),
        grid=(B, G, S // tq),
        in_specs=[pl.BlockSpec((1, tq, R * Dh), lambda b, g, i: (b, i, g)),
                  pl.BlockSpec((2, 1, 1, nc, Dh), lambda b, g, i: (0, b, g, 0, 0)),
                  kvspec(0), kvspec(1), kvspec(2), kvspec(3),
                  pl.BlockSpec((1, tq, LANES), lambda b, g, i: (b, i, OFF_SMALL // LANES)),
                  pl.BlockSpec((n_blk, nc), lambda b, g, i: (0, 0))],
        out_specs=pl.BlockSpec((1, tq, R * Dh), lambda b, g, i: (b, i, g)),
        out_shape=jax.ShapeDtypeStruct((B, S, D_GRP), bf16),
        compiler_params=pltpu.CompilerParams(dimension_semantics=("parallel", "parallel", "arbitrary"),
                                             vmem_limit_bytes=VMEM_LIMIT),
        name="nsa_attention",
    )(q_rot, cmp_kv, kv_rot, kv_rot, kv_rot, kv_rot, proj3, jnp.asarray(ovt, bf16))


def nsa_mixer_pallas(proj, ck_w1, ck_w2, ck_pe, cv_w1, cv_w2, cv_pe, B, S):
    q_rot, kv_rot, c_rot = nsa_prep(proj, S)
    cmp_kv = nsa_compress(c_rot, (ck_w1, cv_w1), (ck_w2, cv_w2), (ck_pe, cv_pe), B, S)
    return nsa_attention(q_rot.reshape(B, S, D_GRP), kv_rot.reshape(B, S, 4 * NSA_KV_W), cmp_kv,
                         proj.reshape(B, S, P_PAD), B, S)


def _retention_body(q_ref, k_ref, v_ref, g_ref, cos_ref, sin_ref, dec_ref, xi_ref, zeta_ref, gc_ref, o_ref, st_ref,
                    *, T):
    @pl.when(pl.program_id(2) == 0)
    def _():
        st_ref[...] = jnp.zeros_like(st_ref)

    C, Dh = RET_CHUNK, HEAD_DIM
    nt = (((1,), (1,)), ((), ()))
    decay_in, xi, zeta, gamma_c = dec_ref[0], xi_ref[0], zeta_ref[0], gc_ref[0, 0:1, :]
    state = st_ref[...]
    for c in range(T // C):
        rs = slice(c * C, (c + 1) * C)
        cos, sin = cos_ref[rs, :], sin_ref[rs, :]
        rope = lambda x: x * cos + pltpu.roll(x, Dh // 2, axis=1) * sin
        q = rope(q_ref[0, rs, :])
        k = rope(k_ref[0, rs, :]) * Dh ** -0.5
        qb, vb = q.astype(bf16), v_ref[0, rs, :].astype(bf16)
        inner = lax.dot_general(qb, k.astype(bf16), nt, preferred_element_type=f32) * decay_in
        o = (jnp.dot(inner.astype(bf16), vb, preferred_element_type=f32)
             + jnp.dot(qb, state.astype(bf16), preferred_element_type=f32) * xi)
        kz = (k * zeta).astype(bf16)
        state = state * gamma_c + jnp.dot(kz.T, vb, preferred_element_type=f32)
        mu = jnp.mean(o, axis=-1, keepdims=True)
        d = o - mu
        o = d * lax.rsqrt(jnp.mean(d * d, axis=-1, keepdims=True) + NORM_EPS)
        gt = g_ref[0, rs, :]
        o_ref[0, rs, :] = (gt * jax.nn.sigmoid(gt) * o).astype(o_ref.dtype)
    st_ref[...] = state


def retention_mixer_pallas(proj3, B, S, T=512):
    H, Dh, C = RET_HEADS, HEAD_DIM, RET_CHUNK
    assert S % T == 0 and T % C == 0
    freqs = RET_THETA ** (-np.linspace(0.0, 1.0, Dh // 2))
    ang = np.arange(S, dtype=np.float64)[:, None] * freqs[None, :]
    cos = jnp.asarray(np.concatenate([np.cos(ang), np.cos(ang)], 1), f32)
    sin = jnp.asarray(np.concatenate([-np.sin(ang), np.sin(ang)], 1), f32)
    log_gamma = np.log1p(-(2.0 ** (-5.0 - np.arange(H, dtype=np.float64))))
    n = np.arange(C, dtype=np.float64)
    diff = n[:, None] - n[None, :]
    decay_in = np.where(diff >= 0, np.exp(log_gamma[:, None, None] * np.maximum(diff, 0.0)), 0.0)
    xi = np.exp(log_gamma[:, None] * (n + 1.0))[:, :, None] * np.ones((1, 1, Dh))
    zeta = np.exp(log_gamma[:, None] * (C - 1.0 - n))[:, :, None] * np.ones((1, 1, Dh))
    gamma_c = np.exp(log_gamma * C)[:, None, None] * np.ones((1, SUB, Dh))
    blk = lambda j: pl.BlockSpec((1, T, Dh), lambda b, h, t: (b, t, (OFF_RET + j * D_GRP) // Dh + h))
    tab = pl.BlockSpec((T, Dh), lambda b, h, t: (t, 0))
    per_head = lambda r: pl.BlockSpec((1, r, Dh), lambda b, h, t: (h, 0, 0))
    return pl.pallas_call(
        functools.partial(_retention_body, T=T),
        grid=(B, H, S // T),
        in_specs=[blk(0), blk(1), blk(2), blk(3), tab, tab, per_head(C), per_head(C), per_head(C), per_head(SUB)],
        out_specs=pl.BlockSpec((1, T, Dh), lambda b, h, t: (b, t, h)),
        out_shape=jax.ShapeDtypeStruct((B, S, D_GRP), bf16),
        scratch_shapes=[pltpu.VMEM((Dh, Dh), f32)],
        compiler_params=pltpu.CompilerParams(dimension_semantics=("parallel", "parallel", "arbitrary"),
                                             vmem_limit_bytes=VMEM_LIMIT),
        name="retention",
    )(proj3, proj3, proj3, proj3, cos, sin, jnp.asarray(decay_in, f32), jnp.asarray(xi, f32),
      jnp.asarray(zeta, f32), jnp.asarray(gamma_c, f32))


def _dot3(a, b):
    ah, al = _split2(a)
    bh, bl = _split2(b)
    return (jnp.dot(ah, bh, preferred_element_type=f32) + jnp.dot(ah, bl, preferred_element_type=f32)
            + jnp.dot(al, bh, preferred_element_type=f32))


def _deltanet_body(q_ref, k_ref, v_ref, z_ref, qp_ref, kp_ref, vp_ref, ab_ref, cw_ref, alog_ref, dtb_ref, ng_ref,
                   o_ref, st_ref, *, T, HS):
    hb = pl.program_id(1)
    first = pl.program_id(2) == 0

    @pl.when(first)
    def _():
        st_ref[...] = jnp.zeros_like(st_ref)

    C, Dh = DN_CHUNK, HEAD_DIM
    nt = (((1,), (1,)), ((), ()))
    n_chunks = T // C
    ab = ab_ref[0]
    lane = lax.broadcasted_iota(jnp.int32, ab.shape, 1)

    def conv_silu(x_ref, prev_ref, j, hc):
        prev = jnp.where(first, 0.0, prev_ref[0, :, hc])
        xcat = jnp.concatenate([prev, x_ref[0, :, hc]], axis=0)
        w = cw_ref[:, j, hc]
        y = sum(w[i:i + 1, :] * xcat[SUB - (CONV_WIDTH - 1) + i:SUB - (CONV_WIDTH - 1) + i + T, :]
                for i in range(CONV_WIDTH))
        return y * jax.nn.sigmoid(y)

    def l2n(x):
        return x * lax.rsqrt(jnp.sum(x * x, axis=-1, keepdims=True) + 1e-6)

    units = [(hh, slice(c * C, (c + 1) * C)) for hh in range(HS) for c in range(n_chunks)]
    qs, ks_, vs_, betas, g_rows = [], [], [], [], []
    for hh in range(HS):
        hc = slice(hh * Dh, (hh + 1) * Dh)
        head = hb * HS + hh
        q_all = l2n(conv_silu(q_ref, qp_ref, 0, hc)) * Dh ** -0.5
        k_all = l2n(conv_silu(k_ref, kp_ref, 1, hc))
        v_all = conv_silu(v_ref, vp_ref, 2, hc)
        a_col = jnp.sum(jnp.where(lane == SMALL_DN_A + head, ab, 0.0), axis=-1, keepdims=True)
        b_col = jnp.sum(jnp.where(lane == SMALL_DN_A + DN_HEADS + head, ab, 0.0), axis=-1, keepdims=True)
        beta_all = jax.nn.sigmoid(b_col)
        g_all = -jnp.exp(alog_ref[hh, 0:1, :]) * jax.nn.softplus(a_col + dtb_ref[hh, 0:1, :])
        for c in range(n_chunks):
            rs = slice(c * C, (c + 1) * C)
            qs.append(q_all[rs]); ks_.append(k_all[rs]); vs_.append(v_all[rs])
            betas.append(beta_all[rs]); g_rows.append(g_all[rs])

    ri = lax.broadcasted_iota(jnp.int32, (C, C), 0)
    ci = lax.broadcasted_iota(jnp.int32, (C, C), 1)
    tril, strict = ri >= ci, ri > ci
    ltri = tril.astype(bf16)
    eye = (ri == ci).astype(f32)
    gcs = []
    for g_rows_u in g_rows:
        gh, gm, gl = _split3(g_rows_u)
        gcs.append(jnp.dot(ltri, gh, preferred_element_type=f32) + jnp.dot(ltri, gm, preferred_element_type=f32)
                   + jnp.dot(ltri, gl, preferred_element_type=f32))
    decays = [jnp.exp(jnp.where(tril, gc[:, :C] - gc[:, :C].T, -jnp.inf)) for gc in gcs]
    kbs = [k.astype(bf16) for k in ks_]
    kbetas = [k * b for k, b in zip(ks_, betas)]
    ms = [jnp.where(strict, lax.dot_general(kbeta.astype(bf16), kb, nt, preferred_element_type=f32) * d, 0.0)
          for kbeta, kb, d in zip(kbetas, kbs, decays)]
    attns = [jnp.where(tril, lax.dot_general(q.astype(bf16), kb, nt, preferred_element_type=f32) * d, 0.0).astype(bf16)
             for q, kb, d in zip(qs, kbs, decays)]
    t_invs, pws = [eye - m for m in ms], ms
    for _ in range(5):
        pws = [_dot3(pw, pw) for pw in pws]
        t_invs = [_dot3(t, eye + pw) for t, pw in zip(t_invs, pws)]
    egcs = [jnp.exp(gc) for gc in gcs]
    tbs = [t.astype(bf16) for t in t_invs]
    us = [jnp.dot(tb, (v * b).astype(bf16), preferred_element_type=f32) for tb, v, b in zip(tbs, vs_, betas)]
    ws = [jnp.dot(tb, (kbeta * egc).astype(bf16), preferred_element_type=f32).astype(bf16)
          for tb, kbeta, egc in zip(tbs, kbetas, egcs)]
    states = [st_ref[hh] for hh in range(HS)]
    for c in range(n_chunks):
        for hh in range(HS):
            u, rs = hh * n_chunks + c, slice(c * C, (c + 1) * C)
            g_last = gcs[u][C - 1:C, :]
            q_dec = (qs[u] * egcs[u]).astype(bf16)
            k_dec = (ks_[u] * jnp.exp(g_last - gcs[u])).astype(bf16)
            sb = states[hh].astype(bf16)
            v_new = us[u] - jnp.dot(ws[u], sb, preferred_element_type=f32)
            vnb = v_new.astype(bf16)
            o = jnp.dot(q_dec, sb, preferred_element_type=f32) + jnp.dot(attns[u], vnb, preferred_element_type=f32)
            states[hh] = states[hh] * jnp.exp(g_last) + jnp.dot(k_dec.T, vnb, preferred_element_type=f32)
            o = o * lax.rsqrt(jnp.mean(o * o, axis=-1, keepdims=True) + NORM_EPS) * ng_ref[...]
            zt = z_ref[0, rs, hh * Dh:(hh + 1) * Dh]
            o_ref[0, rs, hh * Dh:(hh + 1) * Dh] = (o * (zt * jax.nn.sigmoid(zt))).astype(o_ref.dtype)
    for hh in range(HS):
        st_ref[hh] = states[hh]


def deltanet_mixer_pallas(proj3, conv_w, a_log, dt_bias, norm_g, B, S, T=256, HS=4):
    H, Dh = DN_HEADS, HEAD_DIM
    W = HS * Dh
    assert S % T == 0 and T % DN_CHUNK == 0 and H % HS == 0
    blk = lambda j: pl.BlockSpec((1, T, W), lambda b, h, t: (b, t, (OFF_DN + j * D_GRP) // W + h))
    prev = lambda j: pl.BlockSpec((1, SUB, W), lambda b, h, t: (b, jnp.maximum(t * (T // SUB) - 1, 0),
                                                               (OFF_DN + j * D_GRP) // W + h))
    per_head = pl.BlockSpec((HS, SUB, Dh), lambda b, h, t: (h, 0, 0))
    bcast = lambda x: jnp.broadcast_to(x.astype(f32)[:, None, None], (H, SUB, Dh))
    return pl.pallas_call(
        functools.partial(_deltanet_body, T=T, HS=HS),
        grid=(B, H // HS, S // T),
        in_specs=[blk(0), blk(1), blk(2), blk(3), prev(0), prev(1), prev(2),
                  pl.BlockSpec((1, T, LANES), lambda b, h, t: (b, t, OFF_SMALL // LANES)),
                  pl.BlockSpec((CONV_WIDTH, 3, W), lambda b, h, t: (0, 0, h)),
                  per_head, per_head,
                  pl.BlockSpec((1, Dh), lambda b, h, t: (0, 0))],
        out_specs=pl.BlockSpec((1, T, W), lambda b, h, t: (b, t, h)),
        out_shape=jax.ShapeDtypeStruct((B, S, D_GRP), bf16),
        scratch_shapes=[pltpu.VMEM((HS, Dh, Dh), f32)],
        compiler_params=pltpu.CompilerParams(dimension_semantics=("parallel", "parallel", "arbitrary"),
                                             vmem_limit_bytes=VMEM_LIMIT),
        name="deltanet",
    )(proj3, proj3, proj3, proj3, proj3, proj3, proj3, proj3, conv_w.reshape(CONV_WIDTH, 3, D_GRP),
      bcast(a_log), bcast(dt_bias), norm_g.reshape(1, Dh))


def _w_in_pieces():
    n0, r0, d0, w0 = 0, NSA_COLS, NSA_COLS + RET_COLS, NSA_COLS + RET_COLS + DN_COLS
    n_gate = 3 * NSA_HEADS
    return [
        (OFF_RW, w0, 3 * D_GRP),
        (OFF_RET, r0, 4 * D_GRP),
        (OFF_DN, d0, 3 * D_GRP),
        (OFF_DN + 3 * D_GRP, d0 + 3 * D_GRP + 2 * DN_HEADS, D_GRP),
        (OFF_NSA_Q, n0, D_GRP + 6 * NSA_KV_W),
        (OFF_LORA, w0 + 3 * D_GRP, RW_LORA),
        (OFF_SMALL, n0 + D_GRP + 6 * NSA_KV_W, n_gate),
        (OFF_SMALL + SMALL_DN_A, d0 + 3 * D_GRP, 2 * DN_HEADS),
    ]


def _pack_w_in_body(w_ref, o_ref):
    cols = o_ref.shape[1]
    for off, width in ((OFF_LORA + RW_LORA, RW_LORA_PAD - RW_LORA), (OFF_SMALL, LANES)):
        o_ref[off:off + width, :] = jnp.zeros((width, cols), o_ref.dtype)
    for dst, src, width in _w_in_pieces():
        o_ref[dst:dst + width, :] = w_ref[0, src:src + width, :].astype(o_ref.dtype)


def pack_w_in_t(w_in, layer, tc=256):
    K = w_in.shape[1]
    w_t = jnp.swapaxes(w_in, 1, 2)
    return pl.pallas_call(
        _pack_w_in_body,
        grid=(K // tc,),
        in_specs=[pl.BlockSpec((1, P_TOTAL, tc), lambda i: (layer, 0, i))],
        out_specs=pl.BlockSpec((P_PAD, tc), lambda i: (0, i)),
        out_shape=jax.ShapeDtypeStruct((P_PAD, K), bf16),
        compiler_params=pltpu.CompilerParams(dimension_semantics=("parallel",), vmem_limit_bytes=VMEM_LIMIT),
        name="pack_w_in",
    )(w_t)


def hybrid_mixer(h, hn, B, S, layer, w_in, w_out, ck_w1, ck_w2, ck_pe, cv_w1, cv_w2, cv_pe, dn_conv, dn_a_log,
                 dn_dt_bias, dn_norm, rw_mu, rw_w0, rw_w2, rw_a0, rw_a2, rw_g2, rw_kk, rw_ka, rw_rk, rw_ln_g, rw_ln_b):
    M = B * S
    proj = matmul_nt(hn, pack_w_in_t(w_in, layer), name="in_proj")
    p3 = proj.reshape(B, S, P_PAD)
    o_nsa = nsa_mixer_pallas(proj, ck_w1, ck_w2, ck_pe, cv_w1, cv_w2, cv_pe, B, S)
    o_ret = retention_mixer_pallas(p3, B, S)
    o_dn = deltanet_mixer_pallas(p3, dn_conv, dn_a_log, dn_dt_bias, dn_norm, B, S)
    o_rw = rwkv7_mixer(proj, OFF_RW // (3 * D_GRP), OFF_LORA // RW_LORA_PAD, rw_mu, rw_w0, rw_w2, rw_a0, rw_a2,
                       rw_g2, rw_kk, rw_ka, rw_rk, rw_ln_g, rw_ln_b, B, S)
    parts = [o.reshape(M, D_GRP) for o in (o_nsa, o_ret, o_dn, o_rw)]
    return out_proj_residual(parts, w_out.astype(bf16), h)


def kernel(x, p, norm_mix, w_in, w_out, nsa_ck_w1, nsa_ck_w2, nsa_ck_pe, nsa_cv_w1, nsa_cv_w2, nsa_cv_pe, dn_conv, dn_a_log, dn_dt_bias, dn_norm, rw_mu, rw_w0, rw_w2, rw_a0, rw_a2, rw_g2, rw_kk, rw_ka, rw_rk, rw_ln_g, rw_ln_b, norm_moe, moe_router_grp, moe_router_grp_b, moe_router_exp, moe_router_exp_b, moe_w_gate, moe_w_up, moe_w_down, norm_ple, ple_w, ple_gate, norm_final):
    B, S, D = x.shape
    M = B * S
    h = x.reshape(M, D)
    for i in range(DEPTH):
        hn = rms_norm_rows(h, norm_mix[i], bf16)
        h = hybrid_mixer(h, hn, B, S, i, w_in, w_out[i], nsa_ck_w1[i], nsa_ck_w2[i], nsa_ck_pe[i], nsa_cv_w1[i],
                         nsa_cv_w2[i], nsa_cv_pe[i], dn_conv[i], dn_a_log[i], dn_dt_bias[i], dn_norm[i],
                         rw_mu[i], rw_w0[i], rw_w2[i], rw_a0[i], rw_a2[i], rw_g2[i], rw_kk[i], rw_ka[i],
                         rw_rk[i], rw_ln_g[i], rw_ln_b[i])
        hn = rms_norm_rows(h, norm_moe[i], f32)
        h = h + hier_moe(hn, moe_router_grp[i], moe_router_grp_b[i], moe_router_exp[i], moe_router_exp_b[i],
                         moe_w_gate, moe_w_up, moe_w_down, i)
        hn = rms_norm_rows(h, norm_ple[i], bf16)
        h = ple_residual(hn, ple_gate[i].astype(bf16), p[i].reshape(M, PLE_DIM), ple_w[i].astype(bf16), h)
    return rms_norm_rows(h, norm_final, f32).reshape(B, S, D)
```

```python
import functools

import jax
import jax.numpy as jnp
import numpy as np
from jax import lax
from jax.experimental import pallas as pl
from jax.experimental.pallas import tpu as pltpu

D_MODEL = 4096
BATCH = 2
SEQ = 4096
DEPTH = 2
f32 = jnp.float32
bf16 = jnp.bfloat16
D_MIX = D_MODEL
D_GRP = D_MIX // 4
HEAD_DIM = 128
NORM_EPS = 1e-6
NSA_HEADS = D_GRP // HEAD_DIM
NSA_KV_HEADS = 2
NSA_GROUP = NSA_HEADS // NSA_KV_HEADS
NSA_KV_W = NSA_KV_HEADS * HEAD_DIM
CMP_LEN = 32
CMP_STRIDE = 16
SEL_LEN = 64
N_SEL = 16
N_LOCAL_FORCED = 2
FORCED_SCORE = 1e4
WINDOW = 512
SEL_QBLK = 64
WIN_QBLK = 128
ROPE_THETA = 500000.0
ROPE_DIMS = HEAD_DIM // 4
RET_HEADS = D_GRP // HEAD_DIM
RET_CHUNK = 128
RET_THETA = 10000.0
DN_HEADS = D_GRP // HEAD_DIM
DN_CHUNK = 64
CONV_WIDTH = 4
RW_HEAD_DIM = 64
RW_HEADS = D_GRP // RW_HEAD_DIM
RW_DECAY_LORA = 64
RW_AAA_LORA = 64
RW_GATE_LORA = 160
RW_LN_EPS = 64e-5
N_GROUPS = 4
EXPERTS_PER_GROUP = 8
N_EXPERTS = N_GROUPS * EXPERTS_PER_GROUP
TOP_K = 2
D_EXPERT = 768
ROW_BLOCK = 256
PLE_DIM = 256
NSA_COLS = D_GRP + 6 * NSA_KV_W + 3 * NSA_HEADS
RET_COLS = 4 * D_GRP
DN_COLS = 4 * D_GRP + 2 * DN_HEADS
RW_COLS = 3 * D_GRP + RW_DECAY_LORA + RW_AAA_LORA + RW_GATE_LORA
P_TOTAL = NSA_COLS + RET_COLS + DN_COLS + RW_COLS

RW_LORA = RW_DECAY_LORA + RW_AAA_LORA + RW_GATE_LORA
RW_LORA_PAD = 384
RW_GROUPS_PER_BODY = 4
LANES = 128
SUB = 8

OFF_RW = 0
OFF_RET = 3 * D_GRP
OFF_DN = OFF_RET + 4 * D_GRP
OFF_NSA_Q = OFF_DN + 4 * D_GRP
OFF_NSA_KV = OFF_NSA_Q + D_GRP
OFF_LORA = OFF_NSA_KV + 6 * NSA_KV_W
OFF_SMALL = OFF_LORA + RW_LORA_PAD
SMALL_DN_A = 32
P_PAD = OFF_SMALL + LANES
assert P_PAD == 14336 and OFF_LORA % RW_LORA_PAD == 0

VMEM_LIMIT = 56 * 1024 * 1024


def split_cols(x, sizes):
    return jnp.split(x, np.cumsum(sizes)[:-1].tolist(), axis=-1)


def rms_norm(x, g, eps=NORM_EPS):
    xf = x.astype(f32)
    y = xf * lax.rsqrt(jnp.mean(xf * xf, -1, keepdims=True) + eps)
    return (y * g.astype(f32)).astype(x.dtype)


def l2norm(x, eps=1e-6):
    return x * lax.rsqrt(jnp.sum(x * x, -1, keepdims=True) + eps)


def head_layer_norm(x, eps):
    mu = jnp.mean(x, -1, keepdims=True)
    return (x - mu) * lax.rsqrt(jnp.var(x, -1, keepdims=True) + eps)


def masked_softmax(s, mask):
    s = jnp.where(mask, s.astype(f32), -jnp.inf)
    m = jnp.max(s, -1, keepdims=True)
    m = jnp.where(jnp.isfinite(m), m, 0.0)
    e = jnp.where(mask, jnp.exp(s - m), 0.0)
    return e / jnp.maximum(jnp.sum(e, -1, keepdims=True), jnp.finfo(f32).tiny)


def rotary(x, freqs, rot_dims):
    S = x.shape[1]
    half = rot_dims // 2
    ang = jnp.arange(S, dtype=f32)[:, None] * freqs[None, :]
    ang = ang.reshape((S,) + (1,) * (x.ndim - 3) + (half,))
    cos, sin = jnp.cos(ang), jnp.sin(ang)
    x1 = x[..., :half].astype(f32)
    x2 = x[..., half:rot_dims].astype(f32)
    out = jnp.concatenate([x1 * cos - x2 * sin, x2 * cos + x1 * sin, x[..., rot_dims:].astype(f32)], -1)
    return out.astype(x.dtype)


def cmp_sel_overlap(S):
    n_cmp = (S - CMP_LEN) // CMP_STRIDE + 1
    c0 = np.arange(n_cmp)[:, None] * CMP_STRIDE
    s0 = np.arange(S // SEL_LEN)[None, :] * SEL_LEN
    ov = np.clip(np.minimum(c0 + CMP_LEN, s0 + SEL_LEN) - np.maximum(c0, s0), 0, None) / CMP_LEN
    return jnp.asarray(ov, f32)


def nsa_mixer(q, kc, vc, ks, vs, kw, vw, gate_logit, ck_w1, ck_w2, ck_pe, cv_w1, cv_w2, cv_pe):
    B, S, _ = q.shape
    G, R, Dh = NSA_KV_HEADS, NSA_GROUP, HEAD_DIM
    scale = Dh ** -0.5
    half = ROPE_DIMS // 2
    freqs = ROPE_THETA ** (-jnp.arange(half, dtype=f32) / half)
    q = rotary(q.reshape(B, S, G, R, Dh), freqs, ROPE_DIMS)
    kc, vc, ks, vs, kw, vw = [t.reshape(B, S, G, Dh) for t in (kc, vc, ks, vs, kw, vw)]
    kc, ks, kw = [rotary(t, freqs, ROPE_DIMS) for t in (kc, ks, kw)]
    pos = jnp.arange(S)
    n_cmp = (S - CMP_LEN) // CMP_STRIDE + 1
    blk_tok = np.arange(n_cmp)[:, None] * CMP_STRIDE + np.arange(CMP_LEN)[None, :]

    def compress(t, w1, w2, pe):
        blk = t[:, blk_tok] + pe[:, None, :]
        blk = blk.transpose(0, 3, 1, 2, 4).reshape(B, G, n_cmp, CMP_LEN * Dh)
        return jax.nn.silu(blk @ w1) @ w2

    k_cmp = compress(kc, ck_w1, ck_w2, ck_pe)
    v_cmp = compress(vc, cv_w1, cv_w2, cv_pe)
    s_cmp = jnp.einsum('bsgrd,bgnd->bgrsn', q, k_cmp) * scale
    cmp_mask = jnp.asarray(blk_tok[:, -1])[None, :] <= pos[:, None]
    p_cmp = masked_softmax(s_cmp, cmp_mask)
    o_cmp = jnp.einsum('bgrsn,bgnd->bsgrd', p_cmp, v_cmp.astype(f32))
    n_sel_blk = S // SEL_LEN
    k_top = min(N_SEL, n_sel_blk)
    imp = jnp.einsum('bgrsn,nj->bgsj', p_cmp, cmp_sel_overlap(S))
    blk = jnp.arange(n_sel_blk)[None, :]
    cur = (pos // SEL_LEN)[:, None]
    visible = blk <= cur
    forced = (blk == 0) | (visible & (blk > cur - N_LOCAL_FORCED))
    imp = jnp.where(forced, FORCED_SCORE, jnp.where(visible, imp, -1.0))
    sel_idx = lax.top_k(imp, k_top)[1]
    ks_blk = ks.reshape(B, n_sel_blk, SEL_LEN, G, Dh).transpose(0, 3, 1, 2, 4)
    vs_blk = vs.reshape(B, n_sel_blk, SEL_LEN, G, Dh).transpose(0, 3, 1, 2, 4)
    n_qb = S // SEL_QBLK
    q_b = q.reshape(B, n_qb, SEL_QBLK, G, R, Dh).transpose(1, 0, 2, 3, 4, 5)
    idx_b = sel_idx.reshape(B, G, n_qb, SEL_QBLK, k_top).transpose(2, 0, 1, 3, 4)
    qpos_b = pos.reshape(n_qb, SEL_QBLK)
    b_ix = jnp.arange(B)[:, None, None, None]
    g_ix = jnp.arange(G)[None, :, None, None]

    def sel_block(args):
        qb, ib, qp = args
        kb = ks_blk[b_ix, g_ix, ib].reshape(B, G, SEL_QBLK, k_top * SEL_LEN, Dh)
        vb = vs_blk[b_ix, g_ix, ib].reshape(B, G, SEL_QBLK, k_top * SEL_LEN, Dh)
        tok = (ib[..., None] * SEL_LEN + jnp.arange(SEL_LEN)).reshape(B, G, SEL_QBLK, k_top * SEL_LEN)
        s = jnp.einsum('bqgrd,bgqkd->bgrqk', qb, kb) * scale
        mask = tok[:, :, None] <= qp[None, None, None, :, None]
        pr = masked_softmax(s, mask)
        return jnp.einsum('bgrqk,bgqkd->bqgrd', pr, vb.astype(f32))

    o_sel = lax.map(sel_block, (q_b, idx_b, qpos_b))
    o_sel = o_sel.transpose(1, 0, 2, 3, 4, 5).reshape(B, S, G, R, Dh)
    n_wb = S // WIN_QBLK
    span = WINDOW + WIN_QBLK
    key_idx = np.arange(n_wb)[:, None] * WIN_QBLK + np.arange(span)[None, :]
    pad = ((0, 0), (WINDOW, 0), (0, 0), (0, 0))
    kb = jnp.pad(kw, pad)[:, key_idx]
    vb = jnp.pad(vw, pad)[:, key_idx]
    qb = q.reshape(B, n_wb, WIN_QBLK, G, R, Dh)
    s_win = jnp.einsum('bnqgrd,bnkgd->bgrnqk', qb, kb) * scale
    kpos = jnp.asarray(key_idx - WINDOW)[:, None, :]
    qpos = pos.reshape(n_wb, WIN_QBLK)[:, :, None]
    win_mask = (kpos >= 0) & (kpos <= qpos) & (kpos > qpos - WINDOW)
    p_win = masked_softmax(s_win, win_mask)
    o_win = jnp.einsum('bgrnqk,bnkgd->bnqgrd', p_win, vb.astype(f32)).reshape(B, S, G, R, Dh)
    gate = jax.nn.sigmoid(gate_logit.astype(f32)).reshape(B, S, G, R, 3)
    o = gate[..., 0:1] * o_cmp + gate[..., 1:2] * o_sel + gate[..., 2:3] * o_win
    return o.reshape(B, S, D_GRP)


def retention_mixer(q, k, v, g):
    B, S, _ = q.shape
    H, Dh, C = RET_HEADS, HEAD_DIM, RET_CHUNK
    n_c = S // C
    freqs = RET_THETA ** (-jnp.linspace(0.0, 1.0, Dh // 2, dtype=f32))
    heads = lambda t: t.reshape(B, S, H, Dh)
    q = rotary(heads(q), freqs, Dh).astype(f32)
    k = rotary(heads(k), freqs, Dh).astype(f32) * Dh ** -0.5
    v = heads(v).astype(f32)
    log_gamma = jnp.log1p(-(2.0 ** (-5.0 - jnp.arange(H, dtype=f32))))
    n = jnp.arange(C, dtype=f32)
    diff = n[:, None] - n[None, :]
    decay_in = jnp.where(diff >= 0, jnp.exp(log_gamma[:, None, None] * jnp.maximum(diff, 0.0)), 0.0)
    xi = jnp.exp(log_gamma[:, None] * (n + 1.0))
    zeta = jnp.exp(log_gamma[:, None] * (C - 1.0 - n))
    gamma_c = jnp.exp(log_gamma * C)
    chunks = lambda t: t.reshape(B, n_c, C, H, Dh).transpose(1, 0, 3, 2, 4)

    def step(state, inp):
        qc, kc, vc = inp
        inner = jnp.einsum('bhnd,bhmd->bhnm', qc, kc) * decay_in
        o = jnp.einsum('bhnm,bhme->bhne', inner, vc) + jnp.einsum('bhnd,bhde->bhne', qc, state) * xi[..., None]
        state = state * gamma_c[:, None, None] + jnp.einsum('bhmd,bhme->bhde', kc * zeta[..., None], vc)
        return state, o

    _, o = lax.scan(step, jnp.zeros((B, H, Dh, Dh), f32), (chunks(q), chunks(k), chunks(v)))
    o = head_layer_norm(o.transpose(1, 0, 3, 2, 4).reshape(B, S, H, Dh), NORM_EPS)
    return jax.nn.silu(g.astype(f32)) * o.reshape(B, S, D_GRP)


def deltanet_mixer(q, k, v, a, b, z, conv_w, a_log, dt_bias, norm_g):
    B, S, _ = q.shape
    H, Dh, C = DN_HEADS, HEAD_DIM, DN_CHUNK
    n_c = S // C
    qkv = jnp.concatenate([q, k, v], -1)
    qkv = lax.conv_general_dilated(qkv, conv_w[:, None, :], window_strides=(1,), padding=[(CONV_WIDTH - 1, 0)],
                                   dimension_numbers=('NWC', 'WIO', 'NWC'), feature_group_count=3 * D_GRP)
    qkv = jax.nn.silu(qkv.astype(f32))
    q, k, v = [t.reshape(B, S, H, Dh) for t in jnp.split(qkv, 3, -1)]
    q = l2norm(q) * Dh ** -0.5
    k = l2norm(k)
    beta = jax.nn.sigmoid(b.astype(f32))
    g = -jnp.exp(a_log.astype(f32)) * jax.nn.softplus(a.astype(f32) + dt_bias.astype(f32))
    chunks = lambda t: jnp.moveaxis(t.reshape((B, n_c, C) + t.shape[2:]), 3, 1)
    qc, kc, vc, bc = chunks(q), chunks(k), chunks(v), chunks(beta)
    gc = jnp.cumsum(chunks(g), -1)
    kbeta = kc * bc[..., None]
    vbeta = vc * bc[..., None]
    idx = jnp.arange(C)
    tril = idx[:, None] >= idx[None, :]
    strict = idx[:, None] > idx[None, :]
    decay = jnp.exp(jnp.where(tril, gc[..., :, None] - gc[..., None, :], -jnp.inf))
    m = jnp.where(strict, jnp.einsum('bhncd,bhnkd->bhnck', kbeta, kc) * decay, 0.0)
    eye = jnp.eye(C, dtype=f32)
    t_inv = lax.linalg.triangular_solve(eye + m, jnp.broadcast_to(eye, m.shape), left_side=True,
                                        lower=True, unit_diagonal=True)
    u = t_inv @ vbeta
    w = t_inv @ (kbeta * jnp.exp(gc)[..., None])
    attn_in = jnp.where(tril, jnp.einsum('bhncd,bhnkd->bhnck', qc, kc) * decay, 0.0)
    q_dec = qc * jnp.exp(gc)[..., None]
    k_dec = kc * jnp.exp(gc[..., -1:] - gc)[..., None]
    g_last = jnp.exp(gc[..., -1])

    def step(state, inp):
        u_i, w_i, a_i, qd_i, kd_i, gl_i = inp
        v_new = u_i - w_i @ state
        o = qd_i @ state + a_i @ v_new
        state = state * gl_i[..., None, None] + jnp.swapaxes(kd_i, -1, -2) @ v_new
        return state, o

    xs = tuple(jnp.moveaxis(t, 2, 0) for t in (u, w, attn_in, q_dec, k_dec, g_last))
    _, o = lax.scan(step, jnp.zeros((B, H, Dh, Dh), f32), xs)
    o = o.transpose(1, 0, 3, 2, 4).reshape(B, S, H, Dh)
    o = rms_norm(o, norm_g) * jax.nn.silu(z.astype(f32)).reshape(B, S, H, Dh)
    return o.reshape(B, S, D_GRP)


def _split2(x):
    h = x.astype(bf16)
    return h, (x - h.astype(f32)).astype(bf16)


def _split3(x):
    h = x.astype(bf16)
    r1 = x - h.astype(f32)
    m = r1.astype(bf16)
    return h, m, (r1 - m.astype(f32)).astype(bf16)


def _rwkv_prep_body(x_ref, xprev_ref, l_ref, lprev_ref, mu_ref, mul_ref, w0_ref, a0_ref, kkw_ref, ka_ref,
                    w2_ref, a2_ref, g2_ref, e_ref, r_ref, w_ref, k_ref, v_ref, kk_ref, b_ref, g_ref, *, tm, S):
    i = pl.program_id(0)
    first = (i * tm) % S == 0

    def token_mix(x, prev8, mu):
        rows = lax.broadcasted_iota(jnp.int32, x.shape, 0)
        prev_row = jnp.where(first, 0.0, prev8[SUB - 1:SUB, :])
        x_prev = jnp.where(rows == 0, prev_row, pltpu.roll(x, 1, axis=0))
        return x + (x_prev - x) * mu

    x = token_mix(x_ref[...], xprev_ref[...], mu_ref[...])
    lora = token_mix(l_ref[...], lprev_ref[...], mul_ref[...])
    r = x[:, 0:D_GRP]
    k = x[:, D_GRP:2 * D_GRP]
    v = x[:, 2 * D_GRP:3 * D_GRP]
    dw = jnp.dot(jnp.tanh(lora).astype(bf16), w2_ref[...], preferred_element_type=f32)
    da = jnp.dot(lora.astype(bf16), a2_ref[...], preferred_element_type=f32)
    g = jnp.dot(jax.nn.sigmoid(lora).astype(bf16), g2_ref[...], preferred_element_type=f32)
    w = -jax.nn.softplus(-(w0_ref[...] + dw)) - 0.5
    decay = jnp.exp(-jnp.exp(w))
    a = jax.nn.sigmoid(a0_ref[...] + da)
    kq = k * kkw_ref[...]
    sq_h, sq_l = _split2(kq * kq)
    ss = (jnp.dot(sq_h, e_ref[...], preferred_element_type=f32)
          + jnp.dot(sq_l, e_ref[...], preferred_element_type=f32))
    kk = kq * lax.rsqrt(ss + 1e-6)
    r_ref[...] = r
    w_ref[...] = decay
    k_ref[...] = k * (1.0 + (a - 1.0) * ka_ref[...])
    v_ref[...] = v
    kk_ref[...] = kk
    b_ref[...] = kk * a
    g_ref[...] = g


def rwkv_prep(proj, rkv_blk, lora_blk, mu, w0, w2, a0, a2, g2, k_k, k_a, S, tm=256):
    M = proj.shape[0]
    assert M % tm == 0 and S % tm == 0
    W3 = 3 * D_GRP
    mu_x = mu[:W3].reshape(1, W3)
    mu_l = jnp.pad(mu[W3:], (0, RW_LORA_PAD - RW_LORA)).reshape(1, RW_LORA_PAD)
    w2p = jnp.zeros((RW_LORA_PAD, D_GRP), f32).at[0:RW_DECAY_LORA].set(w2).astype(bf16)
    a2p = jnp.zeros((RW_LORA_PAD, D_GRP), f32).at[RW_DECAY_LORA:RW_DECAY_LORA + RW_AAA_LORA].set(a2).astype(bf16)
    g2p = jnp.zeros((RW_LORA_PAD, D_GRP), f32).at[RW_DECAY_LORA + RW_AAA_LORA:RW_LORA].set(g2).astype(bf16)
    hid = np.arange(D_GRP) // RW_HEAD_DIM
    e = jnp.asarray(hid[:, None] == hid[None, :], bf16)
    row = lambda t: t.reshape(1, D_GRP)
    vec = pl.BlockSpec((1, D_GRP), lambda i: (0, 0))
    wspec = pl.BlockSpec((RW_LORA_PAD, D_GRP), lambda i: (0, 0))
    ospec = pl.BlockSpec((tm, D_GRP), lambda i: (i, 0))
    prev_row_blk = lambda i: jnp.maximum(i * (tm // SUB) - 1, 0)
    return pl.pallas_call(
        functools.partial(_rwkv_prep_body, tm=tm, S=S),
        grid=(M // tm,),
        in_specs=[pl.BlockSpec((tm, W3), lambda i: (i, rkv_blk)),
                  pl.BlockSpec((SUB, W3), lambda i: (prev_row_blk(i), rkv_blk)),
                  pl.BlockSpec((tm, RW_LORA_PAD), lambda i: (i, lora_blk)),
                  pl.BlockSpec((SUB, RW_LORA_PAD), lambda i: (prev_row_blk(i), lora_blk)),
                  pl.BlockSpec((1, W3), lambda i: (0, 0)),
                  pl.BlockSpec((1, RW_LORA_PAD), lambda i: (0, 0)),
                  vec, vec, vec, vec, wspec, wspec, wspec,
                  pl.BlockSpec((D_GRP, D_GRP), lambda i: (0, 0))],
        out_specs=[ospec] * 7,
        out_shape=[jax.ShapeDtypeStruct((M, D_GRP), f32)] * 7,
        compiler_params=pltpu.CompilerParams(dimension_semantics=("parallel",), vmem_limit_bytes=VMEM_LIMIT),
        name="rwkv_prep",
    )(proj, proj, proj, proj, mu_x, mu_l, row(w0), row(a0), row(k_k), row(k_a), w2p, a2p, g2p, e)


def _rwkv_scan_body(r_ref, w_ref, k_ref, v_ref, kk_ref, b_ref, g_ref, lng_ref, lnb_ref, rk_ref, o_ref, s_ref,
                    vc_ref, y_ref, *, G, T):
    tb = pl.program_id(2)

    @pl.when(tb == 0)
    def _():
        s_ref[...] = jnp.zeros_like(s_ref)

    Dh = RW_HEAD_DIM
    lo = lax.broadcasted_iota(jnp.int32, (Dh, LANES), 1) < Dh
    lo8 = lax.broadcasted_iota(jnp.int32, (SUB, LANES), 1) < Dh
    sub_iota = lax.broadcasted_iota(jnp.int32, (4 * SUB, LANES), 0)
    row_iota = lax.broadcasted_iota(jnp.int32, (SUB, LANES), 0)

    def seg_sum(x, m):
        s_lo = jnp.sum(jnp.where(m, x, 0.0), axis=-1, keepdims=True)
        s_hi = jnp.sum(jnp.where(m, 0.0, x), axis=-1, keepdims=True)
        return jnp.where(m, s_lo, s_hi)

    cols = [slice(p * LANES, (p + 1) * LANES) for p in range(G)]
    n_groups = T // SUB

    k_iota = lax.broadcasted_iota(jnp.int32, (8 * SUB, LANES), 0)
    k_head = k_iota // (4 * SUB) == lax.broadcasted_iota(jnp.int32, (8 * SUB, LANES), 1) // Dh
    k_piece = (k_iota // SUB) % 4

    def v_pieces(gi):
        t0 = pl.multiple_of(gi * SUB, SUB)
        vts = []
        for p in range(G):
            vh, vm, vl = _split3(v_ref[0, pl.ds(t0, SUB), cols[p]])
            vp = jnp.concatenate([vh, vm, vl, jnp.zeros_like(vh)], axis=0)
            vts.append(jnp.concatenate([vp[:, :Dh], vp[:, Dh:]], axis=0).T)
        return vts

    def v_columns(vts, i, slot):
        sel = (k_head & (k_piece < 3) & (k_iota % SUB == i)).astype(bf16)
        for p in range(G):
            vc_ref[slot, i, p] = jnp.dot(vts[p], sel, preferred_element_type=f32)

    vts0 = v_pieces(0)
    for i in range(SUB):
        v_columns(vts0, i, 0)

    def run_group(gi, slot, states):
        t0 = pl.multiple_of(gi * SUB, SUB)
        vts_next = v_pieces(jnp.minimum(gi + 1, n_groups - 1))
        r8b = [r_ref[0, pl.ds(t0, SUB), cols[p]].astype(bf16) for p in range(G)]
        yacc = [jnp.zeros((SUB, LANES), f32) for _ in range(G)]
        tiles = [[ref[0, pl.ds(t0, SUB), cols[p]] for ref in (kk_ref, w_ref, b_ref, k_ref)] for p in range(G)]
        for i in range(SUB):
            for p in range(G):
                kk_r, w_r, b_r, k_r = [jnp.broadcast_to(t8[i:i + 1, :], (Dh, LANES)) for t8 in tiles[p]]
                S = states[p]
                sab = -seg_sum(S * kk_r, lo)
                states[p] = S * w_r + sab * b_r + vc_ref[slot, i, p] * k_r
            for p in range(G):
                Sb = states[p].astype(bf16)
                zero = jnp.zeros_like(Sb)
                mt = jnp.concatenate([jnp.where(lo, Sb, zero), jnp.where(lo, zero, Sb)], axis=0)
                y8 = lax.dot_general(r8b[p], mt, (((1,), (1,)), ((), ())), preferred_element_type=f32)
                yacc[p] = jnp.where(row_iota == i, y8, yacc[p])
            v_columns(vts_next, i, 1 - slot)
        for p in range(G):
            y_ref[pl.ds(t0, SUB), cols[p]] = yacc[p]
        return states

    def group_batch(gj, carry):
        states = [s_ref[p] for p in range(G)]
        for u in range(RW_GROUPS_PER_BODY):
            states = run_group(RW_GROUPS_PER_BODY * gj + u, u % 2, states)
        for p in range(G):
            s_ref[p] = states[p]
        return carry

    assert n_groups % RW_GROUPS_PER_BODY == 0 and RW_GROUPS_PER_BODY % 2 == 0
    lax.fori_loop(0, n_groups // RW_GROUPS_PER_BODY, group_batch, 0)

    lo_t = lax.broadcasted_iota(jnp.int32, (T, LANES), 1) < Dh
    for p in range(G):
        y = y_ref[:, cols[p]]
        mean = seg_sum(y, lo_t) * (1.0 / Dh)
        d = y - mean
        var = seg_sum(d * d, lo_t) * (1.0 / Dh)
        yn = d * lax.rsqrt(var + RW_LN_EPS) * lng_ref[:, cols[p]] + lnb_ref[:, cols[p]]
        bonus = seg_sum(r_ref[0, :, cols[p]] * k_ref[0, :, cols[p]] * rk_ref[:, cols[p]], lo_t) * v_ref[0, :, cols[p]]
        o_ref[0, :, cols[p]] = (yn + bonus) * g_ref[0, :, cols[p]]


def rwkv_scan(r, w, k, v, kk, b, g, ln_g, ln_b, r_k, G=8, T=256):
    B, S, D = r.shape
    n_pairs = D // LANES
    assert n_pairs % G == 0 and S % T == 0
    spec = pl.BlockSpec((1, T, G * LANES), lambda bi, pi, ti: (bi, ti, pi))
    vec = pl.BlockSpec((1, G * LANES), lambda bi, pi, ti: (0, pi))
    row = lambda t: t.reshape(1, D)
    return pl.pallas_call(
        functools.partial(_rwkv_scan_body, G=G, T=T),
        grid=(B, n_pairs // G, S // T),
        in_specs=[spec] * 7 + [vec] * 3,
        out_specs=spec,
        out_shape=jax.ShapeDtypeStruct((B, S, D), f32),
        scratch_shapes=[pltpu.VMEM((G, RW_HEAD_DIM, LANES), f32),
                        pltpu.VMEM((2, SUB, G, RW_HEAD_DIM, LANES), f32),
                        pltpu.VMEM((T, G * LANES), f32)],
        compiler_params=pltpu.CompilerParams(dimension_semantics=("parallel", "parallel", "arbitrary"),
                                             vmem_limit_bytes=VMEM_LIMIT),
        name="rwkv_scan",
    )(r, w, k, v, kk, b, g, row(ln_g), row(ln_b), row(r_k))


def rwkv7_mixer(proj, rkv_blk, lora_blk, mu, w0, w2, a0, a2, g2, k_k, k_a, r_k, ln_g, ln_b, B, S):
    outs = rwkv_prep(proj, rkv_blk, lora_blk, mu, w0, w2, a0, a2, g2, k_k, k_a, S)
    r, w, k, v, kk, b, g = [t.reshape(B, S, D_GRP) for t in outs]
    return rwkv_scan(r, w, k, v, kk, b, g, ln_g, ln_b, r_k.reshape(-1))


def _mm_nt_body(a_ref, bt_ref, o_ref, acc_ref):
    k = pl.program_id(2)

    @pl.when(k == 0)
    def _():
        acc_ref[...] = jnp.zeros_like(acc_ref)

    acc_ref[...] += lax.dot_general(a_ref[...], bt_ref[...], (((1,), (1,)), ((), ())), preferred_element_type=f32)

    @pl.when(k == pl.num_programs(2) - 1)
    def _():
        o_ref[...] = acc_ref[...].astype(o_ref.dtype)


def matmul_nt(a, bt, out_dtype=f32, tm=1024, tn=1024, tk=2048, name="matmul_nt"):
    M, K = a.shape
    N, _ = bt.shape
    assert M % tm == 0 and N % tn == 0 and K % tk == 0, (M, N, K, tm, tn, tk)
    return pl.pallas_call(
        _mm_nt_body,
        grid=(M // tm, N // tn, K // tk),
        in_specs=[pl.BlockSpec((tm, tk), lambda i, j, k: (i, k)),
                  pl.BlockSpec((tn, tk), lambda i, j, k: (j, k))],
        out_specs=pl.BlockSpec((tm, tn), lambda i, j, k: (i, j)),
        out_shape=jax.ShapeDtypeStruct((M, N), out_dtype),
        scratch_shapes=[pltpu.VMEM((tm, tn), f32)],
        compiler_params=pltpu.CompilerParams(
            dimension_semantics=("parallel", "parallel", "arbitrary"), vmem_limit_bytes=VMEM_LIMIT),
        name=name,
    )(a, bt)


def _rms_norm_body(x_ref, g_ref, o_ref):
    x = x_ref[...]
    y = x * lax.rsqrt(jnp.mean(x * x, axis=-1, keepdims=True) + NORM_EPS)
    o_ref[...] = (y * g_ref[...]).astype(o_ref.dtype)


def rms_norm_rows(x, g, out_dtype, tm=256):
    M, D = x.shape
    return pl.pallas_call(
        _rms_norm_body,
        grid=(M // tm,),
        in_specs=[pl.BlockSpec((tm, D), lambda i: (i, 0)), pl.BlockSpec((1, D), lambda i: (0, 0))],
        out_specs=pl.BlockSpec((tm, D), lambda i: (i, 0)),
        out_shape=jax.ShapeDtypeStruct((M, D), out_dtype),
        compiler_params=pltpu.CompilerParams(dimension_semantics=("parallel",), vmem_limit_bytes=VMEM_LIMIT),
        name="rms_norm",
    )(x, g.reshape(1, D).astype(f32))


def _out_proj_body(a0_ref, a1_ref, a2_ref, a3_ref, b_ref, h_ref, o_ref, acc_ref):
    k = pl.program_id(2)

    @pl.when(k == 0)
    def _():
        acc_ref[...] = h_ref[...]

    for j, a_ref in enumerate((a0_ref, a1_ref, a2_ref, a3_ref)):
        @pl.when(k == j)
        def _(a_ref=a_ref):
            acc_ref[...] += jnp.dot(a_ref[...].astype(bf16), b_ref[...], preferred_element_type=f32)

    @pl.when(k == pl.num_programs(2) - 1)
    def _():
        o_ref[...] = acc_ref[...]


def out_proj_residual(parts, w, h, tm=1024, tn=1024):
    M, D = h.shape
    aspec = pl.BlockSpec((tm, D_GRP), lambda i, j, k: (i, 0))
    return pl.pallas_call(
        _out_proj_body,
        grid=(M // tm, D // tn, len(parts)),
        in_specs=[aspec] * 4 + [pl.BlockSpec((D_GRP, tn), lambda i, j, k: (k, j)),
                                pl.BlockSpec((tm, tn), lambda i, j, k: (i, j))],
        out_specs=pl.BlockSpec((tm, tn), lambda i, j, k: (i, j)),
        out_shape=jax.ShapeDtypeStruct((M, D), f32),
        scratch_shapes=[pltpu.VMEM((tm, tn), f32)],
        compiler_params=pltpu.CompilerParams(
            dimension_semantics=("parallel", "parallel", "arbitrary"), vmem_limit_bytes=VMEM_LIMIT),
        name="out_proj",
    )(*parts, w, h)


def _ple_body(a_ref, b_ref, p_ref, pw_ref, h_ref, o_ref, acc_ref):
    k = pl.program_id(2)

    @pl.when(k == 0)
    def _():
        acc_ref[...] = jnp.zeros_like(acc_ref)

    acc_ref[...] += jnp.dot(a_ref[...], b_ref[...], preferred_element_type=f32)

    @pl.when(k == pl.num_programs(2) - 1)
    def _():
        emb = jnp.dot(p_ref[...].astype(bf16), pw_ref[...], preferred_element_type=f32)
        o_ref[...] = h_ref[...] + emb * jax.nn.sigmoid(acc_ref[...])


def ple_residual(hn, gate_w, p, ple_w, h, tm=1024, tn=1024, tk=2048):
    M, D = h.shape
    return pl.pallas_call(
        _ple_body,
        grid=(M // tm, D // tn, D // tk),
        in_specs=[pl.BlockSpec((tm, tk), lambda i, j, k: (i, k)),
                  pl.BlockSpec((tk, tn), lambda i, j, k: (k, j)),
                  pl.BlockSpec((tm, PLE_DIM), lambda i, j, k: (i, 0)),
                  pl.BlockSpec((PLE_DIM, tn), lambda i, j, k: (0, j)),
                  pl.BlockSpec((tm, tn), lambda i, j, k: (i, j))],
        out_specs=pl.BlockSpec((tm, tn), lambda i, j, k: (i, j)),
        out_shape=jax.ShapeDtypeStruct((M, D), f32),
        scratch_shapes=[pltpu.VMEM((tm, tn), f32)],
        compiler_params=pltpu.CompilerParams(
            dimension_semantics=("parallel", "parallel", "arbitrary"), vmem_limit_bytes=VMEM_LIMIT),
        name="ple",
    )(hn, gate_w, p, ple_w, h)


def _moe_router_body(x_ref, w_ref, b_ref, id_ref, wt_ref):
    logits = jnp.dot(x_ref[...].astype(bf16), w_ref[...], preferred_element_type=f32) + b_ref[...]
    lane = lax.broadcasted_iota(jnp.int32, logits.shape, 1).astype(f32)
    ninf = -jnp.inf

    def top1(vals):
        m = jnp.max(vals, axis=-1, keepdims=True)
        idx = jnp.min(jnp.where(vals == m, lane, float(LANES)), axis=-1, keepdims=True)
        return m, idx

    gl = jnp.where(lane < N_GROUPS, logits, ninf)
    g_max, g_idx = top1(gl)
    g_w = 1.0 / jnp.sum(jnp.exp(gl - g_max), axis=-1, keepdims=True)
    lo = N_GROUPS + g_idx * EXPERTS_PER_GROUP
    el = jnp.where((lane >= lo) & (lane < lo + EXPERTS_PER_GROUP), logits, ninf)
    e_max, _ = top1(el)
    pe = jnp.exp(el - e_max)
    p = pe / jnp.sum(pe, axis=-1, keepdims=True)
    p = jnp.where(el == ninf, ninf, p)
    p1, i1 = top1(p)
    p2, i2 = top1(jnp.where(lane == i1, ninf, p))
    scale = g_w / (p1 + p2)
    id_ref[...] = jnp.where(lane == 0, i1 - N_GROUPS, jnp.where(lane == 1, i2 - N_GROUPS, 0.0)).astype(jnp.int32)
    wt_ref[...] = jnp.where(lane == 0, p1 * scale, jnp.where(lane == 1, p2 * scale, 0.0))


def moe_router(xb, router_grp, router_grp_b, router_exp, router_exp_b, tm=512):
    M, D = xb.shape
    n_log = N_GROUPS + N_EXPERTS
    w = jnp.pad(jnp.concatenate([router_grp, router_exp], axis=1), ((0, 0), (0, LANES - n_log))).astype(bf16)
    b = jnp.pad(jnp.concatenate([router_grp_b, router_exp_b]), (0, LANES - n_log)).reshape(1, LANES).astype(f32)
    ids, wts = pl.pallas_call(
        _moe_router_body,
        grid=(M // tm,),
        in_specs=[pl.BlockSpec((tm, D), lambda i: (i, 0)), pl.BlockSpec((D, LANES), lambda i: (0, 0)),
                  pl.BlockSpec((1, LANES), lambda i: (0, 0))],
        out_specs=[pl.BlockSpec((tm, LANES), lambda i: (i, 0))] * 2,
        out_shape=[jax.ShapeDtypeStruct((M, LANES), jnp.int32), jax.ShapeDtypeStruct((M, LANES), f32)],
        compiler_params=pltpu.CompilerParams(dimension_semantics=("parallel",), vmem_limit_bytes=VMEM_LIMIT),
        name="moe_router",
    )(xb, w, b)
    return ids[:, :TOP_K], wts[:, :TOP_K]


def _moe_up_body(be_ref, nu_ref, x_ref, wg_ref, wu_ref, h_ref):
    i = pl.program_id(1)

    @pl.when(i < nu_ref[0])
    def _():
        x = x_ref[...].astype(bf16)
        hg = jnp.dot(x, wg_ref[0, 0].astype(bf16), preferred_element_type=f32)
        hu = jnp.dot(x, wu_ref[0, 0].astype(bf16), preferred_element_type=f32)
        h_ref[...] = (hg * jax.nn.sigmoid(hg) * hu).astype(bf16)

    @pl.when(i >= nu_ref[0])
    def _():
        h_ref[...] = jnp.zeros_like(h_ref)


def _moe_down_body(be_ref, nu_ref, h_ref, wd_ref, wt_ref, y_ref):
    i = pl.program_id(0)

    @pl.when(i < nu_ref[0])
    def _():
        y_ref[...] = jnp.dot(h_ref[...], wd_ref[0, 0].astype(bf16), preferred_element_type=f32) * wt_ref[...]

    @pl.when(i >= nu_ref[0])
    def _():
        y_ref[...] = jnp.zeros_like(y_ref)


def moe_experts(x_pad, w_pad, blk_expert, n_used, wg, wu, wd, layer, tf=384):
    A_pad, D = x_pad.shape
    n_blk = A_pad // ROW_BLOCK
    nf = D_EXPERT // tf
    h = pl.pallas_call(
        _moe_up_body,
        grid_spec=pltpu.PrefetchScalarGridSpec(
            num_scalar_prefetch=2,
            grid=(nf, n_blk),
            in_specs=[pl.BlockSpec((ROW_BLOCK, D), lambda f, i, be, nu: (i, 0)),
                      pl.BlockSpec((1, 1, D, tf), lambda f, i, be, nu: (layer, be[i], 0, f)),
                      pl.BlockSpec((1, 1, D, tf), lambda f, i, be, nu: (layer, be[i], 0, f))],
            out_specs=pl.BlockSpec((ROW_BLOCK, tf), lambda f, i, be, nu: (i, f)),
        ),
        out_shape=jax.ShapeDtypeStruct((A_pad, D_EXPERT), bf16),
        compiler_params=pltpu.CompilerParams(
            dimension_semantics=("arbitrary", "arbitrary"), vmem_limit_bytes=VMEM_LIMIT),
        name="moe_up",
    )(blk_expert, n_used, x_pad, wg, wu)
    return pl.pallas_call(
        _moe_down_body,
        grid_spec=pltpu.PrefetchScalarGridSpec(
            num_scalar_prefetch=2,
            grid=(n_blk,),
            in_specs=[pl.BlockSpec((ROW_BLOCK, D_EXPERT), lambda i, be, nu: (i, 0)),
                      pl.BlockSpec((1, 1, D_EXPERT, D), lambda i, be, nu: (layer, be[i], 0, 0)),
                      pl.BlockSpec((ROW_BLOCK, 1), lambda i, be, nu: (i, 0))],
            out_specs=pl.BlockSpec((ROW_BLOCK, D), lambda i, be, nu: (i, 0)),
        ),
        out_shape=jax.ShapeDtypeStruct((A_pad, D), f32),
        compiler_params=pltpu.CompilerParams(dimension_semantics=("arbitrary",), vmem_limit_bytes=VMEM_LIMIT),
        name="moe_down",
    )(blk_expert, n_used, h, wd, w_pad.reshape(A_pad, 1))


def hier_moe(xb, router_grp, router_grp_b, router_exp, router_exp_b, w_gate, w_up, w_down, layer):
    M, D = xb.shape
    expert_ids, weights = moe_router(xb, router_grp, router_grp_b, router_exp, router_exp_b)
    A = M * TOP_K
    e_flat = expert_ids.reshape(A)
    onehot = (e_flat[:, None] == jnp.arange(N_EXPERTS)[None, :]).astype(jnp.int32)
    rank = jnp.take_along_axis(jnp.cumsum(onehot, axis=0), e_flat[:, None], axis=1)[:, 0] - 1
    counts = jnp.sum(onehot, axis=0)
    padded = (counts + ROW_BLOCK - 1) // ROW_BLOCK * ROW_BLOCK
    end_pad = jnp.cumsum(padded)
    dest = (end_pad - padded)[e_flat] + rank
    n_blk = -(-A // ROW_BLOCK) + N_EXPERTS
    A_pad = n_blk * ROW_BLOCK
    tok_pad = (jnp.arange(A_pad, dtype=jnp.int32) % M).at[dest].set(jnp.arange(A, dtype=jnp.int32) // TOP_K)
    w_pad = jnp.zeros((A_pad,), f32).at[dest].set(weights.reshape(A))
    blk_expert = jnp.minimum(jnp.searchsorted(end_pad, jnp.arange(n_blk) * ROW_BLOCK, side='right'),
                             N_EXPERTS - 1).astype(jnp.int32)
    n_used = (end_pad[-1:] // ROW_BLOCK).astype(jnp.int32)
    x_pad = xb[tok_pad]
    y_pad = moe_experts(x_pad, w_pad, blk_expert, n_used, w_gate, w_up, w_down, layer)
    rows = dest.reshape(M, TOP_K)
    return y_pad[rows[:, 0]] + y_pad[rows[:, 1]]


def _rope_tables(S, theta, rot_dims, width):
    half = rot_dims // 2
    freqs = theta ** (-np.arange(half, dtype=np.float64) / half)
    ang = np.arange(S, dtype=np.float64)[:, None] * freqs[None, :]
    cos = np.ones((S, width), np.float64)
    sin_lo = np.zeros((S, width), np.float64)
    sin_hi = np.zeros((S, width), np.float64)
    cos[:, :half] = np.cos(ang)
    cos[:, half:rot_dims] = np.cos(ang)
    sin_lo[:, :half] = -np.sin(ang)
    sin_hi[:, half:rot_dims] = np.sin(ang)
    return [jnp.asarray(t, f32) for t in (cos, sin_lo, sin_hi)]


def _rope128(x, cos, sin_lo, sin_hi, half):
    return x * cos + pltpu.roll(x, LANES - half, axis=1) * sin_lo + pltpu.roll(x, half, axis=1) * sin_hi


def _nsa_prep_body(q_ref, kv_ref, cos_ref, sl_ref, sh_ref, qo_ref, kvo_ref, co_ref):
    cos, sl, sh = cos_ref[...], sl_ref[...], sh_ref[...]
    half = ROPE_DIMS // 2
    scale = HEAD_DIM ** -0.5
    for h in range(NSA_HEADS):
        c = slice(h * LANES, (h + 1) * LANES)
        qo_ref[:, c] = (_rope128(q_ref[:, c], cos, sl, sh, half) * scale).astype(bf16)
    for j in range(6):
        for g in range(NSA_KV_HEADS):
            c = slice((j * NSA_KV_HEADS + g) * LANES, (j * NSA_KV_HEADS + g + 1) * LANES)
            x = kv_ref[:, c]
            if j % 2 == 0:
                x = _rope128(x, cos, sl, sh, half)
            if j < 2:
                co_ref[:, c] = x.astype(bf16)
            else:
                kvo_ref[:, slice(c.start - 4 * LANES, c.stop - 4 * LANES)] = x.astype(bf16)


def nsa_prep(proj, S, tm=512):
    M = proj.shape[0]
    assert M % tm == 0 and S % tm == 0
    tabs = _rope_tables(S, ROPE_THETA, ROPE_DIMS, LANES)
    tspec = pl.BlockSpec((tm, LANES), lambda i: (i % (S // tm), 0))
    return pl.pallas_call(
        _nsa_prep_body,
        grid=(M // tm,),
        in_specs=[pl.BlockSpec((tm, D_GRP), lambda i: (i, OFF_NSA_Q // D_GRP)),
                  pl.BlockSpec((tm, 6 * NSA_KV_W), lambda i: (i, OFF_NSA_KV // (6 * NSA_KV_W))),
                  tspec, tspec, tspec],
        out_specs=[pl.BlockSpec((tm, D_GRP), lambda i: (i, 0)),
                   pl.BlockSpec((tm, 4 * NSA_KV_W), lambda i: (i, 0)),
                   pl.BlockSpec((tm, 2 * NSA_KV_W), lambda i: (i, 0))],
        out_shape=[jax.ShapeDtypeStruct((M, D_GRP), bf16), jax.ShapeDtypeStruct((M, 4 * NSA_KV_W), bf16),
                   jax.ShapeDtypeStruct((M, 2 * NSA_KV_W), bf16)],
        compiler_params=pltpu.CompilerParams(dimension_semantics=("parallel",), vmem_limit_bytes=VMEM_LIMIT),
        name="nsa_prep",
    )(proj, proj, *tabs)


def _nsa_compress_body(r_ref, w1_ref, w2_ref, pe_ref, o_ref):
    r = r_ref[0, 0, 0]
    half_w = CMP_STRIDE * HEAD_DIM
    a = jnp.dot(r, w1_ref[0, :half_w, :], preferred_element_type=f32)
    b = jnp.dot(r, w1_ref[0, half_w:, :], preferred_element_type=f32)
    n = r.shape[0]
    pe = jnp.dot(jnp.broadcast_to(pe_ref[0], (SUB, pe_ref.shape[-1])), w1_ref[0], preferred_element_type=f32)[0:1]
    pre = a + pltpu.roll(b, n - 1, axis=0) + pe
    hid = pre * jax.nn.sigmoid(pre)
    o_ref[0, 0, 0] = jnp.dot(hid.astype(bf16), w2_ref[0], preferred_element_type=f32).astype(bf16)


def nsa_compress(c_rot, w1s, w2s, pes, B, S):
    G, Dh = NSA_KV_HEADS, HEAD_DIM
    nc = S // CMP_STRIDE
    r = c_rot.reshape(B, nc, CMP_STRIDE, 2, G, Dh).transpose(3, 0, 4, 1, 2, 5).reshape(2, B, G, nc, CMP_STRIDE * Dh)
    w1 = jnp.stack(w1s).astype(bf16)
    w2 = jnp.stack(w2s).astype(bf16)
    pe = jnp.stack(pes).reshape(2, 1, CMP_LEN * Dh).astype(bf16)
    return pl.pallas_call(
        _nsa_compress_body,
        grid=(2, B, G),
        in_specs=[pl.BlockSpec((1, 1, 1, nc, CMP_STRIDE * Dh), lambda j, b, g: (j, b, g, 0, 0)),
                  pl.BlockSpec((1, CMP_LEN * Dh, Dh), lambda j, b, g: (j, 0, 0)),
                  pl.BlockSpec((1, Dh, Dh), lambda j, b, g: (j, 0, 0)),
                  pl.BlockSpec((1, 1, CMP_LEN * Dh), lambda j, b, g: (j, 0, 0))],
        out_specs=pl.BlockSpec((1, 1, 1, nc, Dh), lambda j, b, g: (j, b, g, 0, 0)),
        out_shape=jax.ShapeDtypeStruct((2, B, G, nc, Dh), bf16),
        compiler_params=pltpu.CompilerParams(dimension_semantics=("parallel", "parallel", "parallel"),
                                             vmem_limit_bytes=VMEM_LIMIT),
        name="nsa_compress",
    )(r, w1, w2, pe)


def _softmax_rows(s, mask):
    s = jnp.where(mask, s, -jnp.inf)
    m = jnp.max(s, axis=-1, keepdims=True)
    m = jnp.where(m == -jnp.inf, 0.0, m)
    e = jnp.exp(s - m)
    return e, jnp.maximum(jnp.sum(e, axis=-1, keepdims=True), jnp.finfo(f32).tiny)


def _nsa_attn_body(q_ref, cmp_ref, ks_ref, vs_ref, kw_ref, vw_ref, gate_ref, ovt_ref, o_ref, *, tq, tk, S):
    g = pl.program_id(1)
    qi = pl.program_id(2)
    R, Dh = NSA_GROUP, HEAD_DIM
    rows = R * tq
    n_blk = S // SEL_LEN
    n_cmp = S // CMP_STRIDE - 1
    q0 = qi * tq
    q4 = jnp.concatenate([q_ref[0, :, r * Dh:(r + 1) * Dh] for r in range(R)], axis=0)
    nt = (((1,), (1,)), ((), ()))

    def qpos(shape):
        return q0 + lax.broadcasted_iota(jnp.int32, shape, 0) % tq

    k_cmp, v_cmp = cmp_ref[0, 0, 0], cmp_ref[1, 0, 0]
    ncp = k_cmp.shape[0]
    s = lax.dot_general(q4, k_cmp, nt, preferred_element_type=f32)
    n_idx = lax.broadcasted_iota(jnp.int32, (rows, ncp), 1)
    mask = (n_idx * CMP_STRIDE + (CMP_LEN - 1) <= qpos((rows, ncp))) & (n_idx < n_cmp)
    e, den = _softmax_rows(s, mask)
    p = e / den
    o_cmp = jnp.dot(p.astype(bf16), v_cmp, preferred_element_type=f32)

    psum = p[0:tq]
    for r in range(1, R):
        psum = psum + p[r * tq:(r + 1) * tq]
    ph, plo = _split2(psum)
    imp = (lax.dot_general(ovt_ref[...], ph, nt, preferred_element_type=f32)
           + lax.dot_general(ovt_ref[...], plo, nt, preferred_element_type=f32))
    blk = lax.broadcasted_iota(jnp.int32, (n_blk, tq), 0)
    cur = (q0 + lax.broadcasted_iota(jnp.int32, (n_blk, tq), 1)) // SEL_LEN
    visible = blk <= cur
    forced = (blk == 0) | (visible & (blk > cur - N_LOCAL_FORCED))
    imp = jnp.where(forced, FORCED_SCORE, jnp.where(visible, imp, -1.0))
    sel_t = jnp.zeros((n_blk, tq), f32)
    blk_f = blk.astype(f32)
    for _ in range(min(N_SEL, n_blk)):
        m = jnp.max(imp, axis=0, keepdims=True)
        first = jnp.min(jnp.where(imp == m, blk_f, float(n_blk)), axis=0, keepdims=True)
        hit = blk_f == first
        sel_t = jnp.where(hit, 1.0, sel_t)
        imp = jnp.where(hit, -jnp.inf, imp)
    sel_q = jnp.concatenate([sel_t.T, jnp.ones((tq, LANES - n_blk), f32)], axis=1).astype(bf16)

    blocks_per_tile = tk // SEL_LEN

    big = float(2 ** 30)
    kb = sel_q.shape[1]

    def sel_step(kt, carry, causal):
        m_run, l_run, acc = carry
        k0 = pl.multiple_of(kt * tk, tk)
        kt_k = ks_ref[0, pl.ds(k0, tk), :]
        kt_v = vs_ref[0, pl.ds(k0, tk), :]
        s = lax.dot_general(q4, kt_k, nt, preferred_element_type=f32)
        row = lax.broadcasted_iota(jnp.int32, (kb, tk), 0)
        in_blk = row == kt * blocks_per_tile + lax.broadcasted_iota(jnp.int32, (kb, tk), 1) // SEL_LEN
        expand = jnp.where(in_blk, big, jnp.where(row == n_blk, -big, 0.0)).astype(bf16)
        bias = jnp.dot(sel_q, expand, preferred_element_type=f32)
        s = (s.reshape(R, tq, tk) + bias[None]).reshape(rows, tk)
        if causal:
            kpos = k0 + lax.broadcasted_iota(jnp.int32, (rows, tk), 1)
            s = jnp.where(kpos <= qpos((rows, tk)), s, -big)
        m_new = jnp.maximum(m_run, jnp.max(s, axis=-1, keepdims=True))
        alpha = jnp.exp(m_run - m_new)
        e = jnp.exp(s - m_new)
        l_new = alpha * l_run + jnp.sum(e, axis=-1, keepdims=True)
        acc = alpha * acc + jnp.dot(e.astype(bf16), kt_v, preferred_element_type=f32)
        return m_new, l_new, acc

    last_kt = (q0 + tq - 1) // tk
    init = (jnp.full((rows, 1), -jnp.inf, f32), jnp.zeros((rows, 1), f32), jnp.zeros((rows, Dh), f32))
    carry = lax.fori_loop(0, last_kt, functools.partial(sel_step, causal=False), init)
    _, l_sel, acc_sel = sel_step(last_kt, carry, causal=True)
    o_sel = acc_sel / l_sel

    span = WINDOW + tq
    w0 = pl.multiple_of(jnp.maximum(q0 - WINDOW, 0), tq)
    kwin = kw_ref[0, pl.ds(w0, span), :]
    vwin = vw_ref[0, pl.ds(w0, span), :]
    s = lax.dot_general(q4, kwin, nt, preferred_element_type=f32)
    kpos = w0 + lax.broadcasted_iota(jnp.int32, (rows, span), 1)
    qp = qpos((rows, span))
    e, den = _softmax_rows(s, (kpos <= qp) & (kpos > qp - WINDOW))
    o_win = jnp.dot(e.astype(bf16), vwin, preferred_element_type=f32) / den

    gate = jax.nn.sigmoid(gate_ref[0])
    for r in range(R):
        c0 = (g * R + r) * 3
        lane = lax.broadcasted_iota(jnp.int32, (tq, LANES), 1)
        gsel = lambda c: jnp.sum(jnp.where(lane == c0 + c, gate, 0.0), axis=-1, keepdims=True)
        rs = slice(r * tq, (r + 1) * tq)
        o_ref[0, :, r * Dh:(r + 1) * Dh] = (gsel(0) * o_cmp[rs] + gsel(1) * o_sel[rs]
                                            + gsel(2) * o_win[rs]).astype(o_ref.dtype)


def nsa_attention(q_rot, kv_rot, cmp_kv, proj3, B, S, tq=256, tk=512):
    G, R, Dh = NSA_KV_HEADS, NSA_GROUP, HEAD_DIM
    n_blk = S // SEL_LEN
    assert S % tk == 0 and tk % tq == 0 and S >= WINDOW + tq and n_blk % SUB == 0 and n_blk < LANES
    nc = S // CMP_STRIDE
    c0 = np.arange(nc - 1)[None, :] * CMP_STRIDE
    s0 = np.arange(n_blk)[:, None] * SEL_LEN
    ovt = np.zeros((n_blk, nc), np.float32)
    ovt[:, :nc - 1] = np.clip(np.minimum(c0 + CMP_LEN, s0 + SEL_LEN) - np.maximum(c0, s0), 0, None) / CMP_LEN
    kvspec = lambda j: pl.BlockSpec((1, S, Dh), lambda b, g, i: (b, 0, j * G + g))
    return pl.pallas_call(
        functools.partial(_nsa_attn_body, tq=tq, tk=tk, S=S),
        grid=(B, G, S // tq),
        in_specs=[pl.BlockSpec((1, tq, R * Dh), lambda b, g, i: (b, i, g)),
                  pl.BlockSpec((2, 1, 1, nc, Dh), lambda b, g, i: (0, b, g, 0, 0)),
                  kvspec(0), kvspec(1), kvspec(2), kvspec(3),
                  pl.BlockSpec((1, tq, LANES), lambda b, g, i: (b, i, OFF_SMALL // LANES)),
                  pl.BlockSpec((n_blk, nc), lambda b, g, i: (0, 0))],
        out_specs=pl.BlockSpec((1, tq, R * Dh), lambda b, g, i: (b, i, g)),
        out_shape=jax.ShapeDtypeStruct((B, S, D_GRP), bf16),
        compiler_params=pltpu.CompilerParams(dimension_semantics=("parallel", "parallel", "arbitrary"),
                                             vmem_limit_bytes=VMEM_LIMIT),
        name="nsa_attention",
    )(q_rot, cmp_kv, kv_rot, kv_rot, kv_rot, kv_rot, proj3, jnp.asarray(ovt, bf16))


def nsa_mixer_pallas(proj, ck_w1, ck_w2, ck_pe, cv_w1, cv_w2, cv_pe, B, S):
    q_rot, kv_rot, c_rot = nsa_prep(proj, S)
    cmp_kv = nsa_compress(c_rot, (ck_w1, cv_w1), (ck_w2, cv_w2), (ck_pe, cv_pe), B, S)
    return nsa_attention(q_rot.reshape(B, S, D_GRP), kv_rot.reshape(B, S, 4 * NSA_KV_W), cmp_kv,
                         proj.reshape(B, S, P_PAD), B, S)


def _retention_body(q_ref, k_ref, v_ref, g_ref, cos_ref, sin_ref, dec_ref, xi_ref, zeta_ref, gc_ref, o_ref, st_ref,
                    *, T):
    @pl.when(pl.program_id(2) == 0)
    def _():
        st_ref[...] = jnp.zeros_like(st_ref)

    C, Dh = RET_CHUNK, HEAD_DIM
    nt = (((1,), (1,)), ((), ()))
    decay_in, xi, zeta, gamma_c = dec_ref[0], xi_ref[0], zeta_ref[0], gc_ref[0, 0:1, :]
    state = st_ref[...]
    for c in range(T // C):
        rs = slice(c * C, (c + 1) * C)
        cos, sin = cos_ref[rs, :], sin_ref[rs, :]
        rope = lambda x: x * cos + pltpu.roll(x, Dh // 2, axis=1) * sin
        q = rope(q_ref[0, rs, :])
        k = rope(k_ref[0, rs, :]) * Dh ** -0.5
        qb, vb = q.astype(bf16), v_ref[0, rs, :].astype(bf16)
        inner = lax.dot_general(qb, k.astype(bf16), nt, preferred_element_type=f32) * decay_in
        o = (jnp.dot(inner.astype(bf16), vb, preferred_element_type=f32)
             + jnp.dot(qb, state.astype(bf16), preferred_element_type=f32) * xi)
        kz = (k * zeta).astype(bf16)
        state = state * gamma_c + jnp.dot(kz.T, vb, preferred_element_type=f32)
        mu = jnp.mean(o, axis=-1, keepdims=True)
        d = o - mu
        o = d * lax.rsqrt(jnp.mean(d * d, axis=-1, keepdims=True) + NORM_EPS)
        gt = g_ref[0, rs, :]
        o_ref[0, rs, :] = (gt * jax.nn.sigmoid(gt) * o).astype(o_ref.dtype)
    st_ref[...] = state


def retention_mixer_pallas(proj3, B, S, T=512):
    H, Dh, C = RET_HEADS, HEAD_DIM, RET_CHUNK
    assert S % T == 0 and T % C == 0
    freqs = RET_THETA ** (-np.linspace(0.0, 1.0, Dh // 2))
    ang = np.arange(S, dtype=np.float64)[:, None] * freqs[None, :]
    cos = jnp.asarray(np.concatenate([np.cos(ang), np.cos(ang)], 1), f32)
    sin = jnp.asarray(np.concatenate([-np.sin(ang), np.sin(ang)], 1), f32)
    log_gamma = np.log1p(-(2.0 ** (-5.0 - np.arange(H, dtype=np.float64))))
    n = np.arange(C, dtype=np.float64)
    diff = n[:, None] - n[None, :]
    decay_in = np.where(diff >= 0, np.exp(log_gamma[:, None, None] * np.maximum(diff, 0.0)), 0.0)
    xi = np.exp(log_gamma[:, None] * (n + 1.0))[:, :, None] * np.ones((1, 1, Dh))
    zeta = np.exp(log_gamma[:, None] * (C - 1.0 - n))[:, :, None] * np.ones((1, 1, Dh))
    gamma_c = np.exp(log_gamma * C)[:, None, None] * np.ones((1, SUB, Dh))
    blk = lambda j: pl.BlockSpec((1, T, Dh), lambda b, h, t: (b, t, (OFF_RET + j * D_GRP) // Dh + h))
    tab = pl.BlockSpec((T, Dh), lambda b, h, t: (t, 0))
    per_head = lambda r: pl.BlockSpec((1, r, Dh), lambda b, h, t: (h, 0, 0))
    return pl.pallas_call(
        functools.partial(_retention_body, T=T),
        grid=(B, H, S // T),
        in_specs=[blk(0), blk(1), blk(2), blk(3), tab, tab, per_head(C), per_head(C), per_head(C), per_head(SUB)],
        out_specs=pl.BlockSpec((1, T, Dh), lambda b, h, t: (b, t, h)),
        out_shape=jax.ShapeDtypeStruct((B, S, D_GRP), bf16),
        scratch_shapes=[pltpu.VMEM((Dh, Dh), f32)],
        compiler_params=pltpu.CompilerParams(dimension_semantics=("parallel", "parallel", "arbitrary"),
                                             vmem_limit_bytes=VMEM_LIMIT),
        name="retention",
    )(proj3, proj3, proj3, proj3, cos, sin, jnp.asarray(decay_in, f32), jnp.asarray(xi, f32),
      jnp.asarray(zeta, f32), jnp.asarray(gamma_c, f32))


def _dot3(a, b):
    ah, al = _split2(a)
    bh, bl = _split2(b)
    return (jnp.dot(ah, bh, preferred_element_type=f32) + jnp.dot(ah, bl, preferred_element_type=f32)
            + jnp.dot(al, bh, preferred_element_type=f32))


def _deltanet_body(q_ref, k_ref, v_ref, z_ref, qp_ref, kp_ref, vp_ref, ab_ref, cw_ref, alog_ref, dtb_ref, ng_ref,
                   o_ref, st_ref, *, T, HS):
    hb = pl.program_id(1)
    first = pl.program_id(2) == 0

    @pl.when(first)
    def _():
        st_ref[...] = jnp.zeros_like(st_ref)

    C, Dh = DN_CHUNK, HEAD_DIM
    nt = (((1,), (1,)), ((), ()))
    n_chunks = T // C
    ab = ab_ref[0]
    lane = lax.broadcasted_iota(jnp.int32, ab.shape, 1)

    def conv_silu(x_ref, prev_ref, j, hc):
        prev = jnp.where(first, 0.0, prev_ref[0, :, hc])
        xcat = jnp.concatenate([prev, x_ref[0, :, hc]], axis=0)
        w = cw_ref[:, j, hc]
        y = sum(w[i:i + 1, :] * xcat[SUB - (CONV_WIDTH - 1) + i:SUB - (CONV_WIDTH - 1) + i + T, :]
                for i in range(CONV_WIDTH))
        return y * jax.nn.sigmoid(y)

    def l2n(x):
        return x * lax.rsqrt(jnp.sum(x * x, axis=-1, keepdims=True) + 1e-6)

    units = [(hh, slice(c * C, (c + 1) * C)) for hh in range(HS) for c in range(n_chunks)]
    qs, ks_, vs_, betas, g_rows = [], [], [], [], []
    for hh in range(HS):
        hc = slice(hh * Dh, (hh + 1) * Dh)
        head = hb * HS + hh
        q_all = l2n(conv_silu(q_ref, qp_ref, 0, hc)) * Dh ** -0.5
        k_all = l2n(conv_silu(k_ref, kp_ref, 1, hc))
        v_all = conv_silu(v_ref, vp_ref, 2, hc)
        a_col = jnp.sum(jnp.where(lane == SMALL_DN_A + head, ab, 0.0), axis=-1, keepdims=True)
        b_col = jnp.sum(jnp.where(lane == SMALL_DN_A + DN_HEADS + head, ab, 0.0), axis=-1, keepdims=True)
        beta_all = jax.nn.sigmoid(b_col)
        g_all = -jnp.exp(alog_ref[hh, 0:1, :]) * jax.nn.softplus(a_col + dtb_ref[hh, 0:1, :])
        for c in range(n_chunks):
            rs = slice(c * C, (c + 1) * C)
            qs.append(q_all[rs]); ks_.append(k_all[rs]); vs_.append(v_all[rs])
            betas.append(beta_all[rs]); g_rows.append(g_all[rs])

    ri = lax.broadcasted_iota(jnp.int32, (C, C), 0)
    ci = lax.broadcasted_iota(jnp.int32, (C, C), 1)
    tril, strict = ri >= ci, ri > ci
    ltri = tril.astype(bf16)
    eye = (ri == ci).astype(f32)
    gcs = []
    for g_rows_u in g_rows:
        gh, gm, gl = _split3(g_rows_u)
        gcs.append(jnp.dot(ltri, gh, preferred_element_type=f32) + jnp.dot(ltri, gm, preferred_element_type=f32)
                   + jnp.dot(ltri, gl, preferred_element_type=f32))
    decays = [jnp.exp(jnp.where(tril, gc[:, :C] - gc[:, :C].T, -jnp.inf)) for gc in gcs]
    kbs = [k.astype(bf16) for k in ks_]
    kbetas = [k * b for k, b in zip(ks_, betas)]
    ms = [jnp.where(strict, lax.dot_general(kbeta.astype(bf16), kb, nt, preferred_element_type=f32) * d, 0.0)
          for kbeta, kb, d in zip(kbetas, kbs, decays)]
    attns = [jnp.where(tril, lax.dot_general(q.astype(bf16), kb, nt, preferred_element_type=f32) * d, 0.0).astype(bf16)
             for q, kb, d in zip(qs, kbs, decays)]
    t_invs, pws = [eye - m for m in ms], ms
    for _ in range(5):
        pws = [_dot3(pw, pw) for pw in pws]
        t_invs = [_dot3(t, eye + pw) for t, pw in zip(t_invs, pws)]
    egcs = [jnp.exp(gc) for gc in gcs]
    tbs = [t.astype(bf16) for t in t_invs]
    us = [jnp.dot(tb, (v * b).astype(bf16), preferred_element_type=f32) for tb, v, b in zip(tbs, vs_, betas)]
    ws = [jnp.dot(tb, (kbeta * egc).astype(bf16), preferred_element_type=f32).astype(bf16)
          for tb, kbeta, egc in zip(tbs, kbetas, egcs)]
    states = [st_ref[hh] for hh in range(HS)]
    for c in range(n_chunks):
        for hh in range(HS):
            u, rs = hh * n_chunks + c, slice(c * C, (c + 1) * C)
            g_last = gcs[u][C - 1:C, :]
            q_dec = (qs[u] * egcs[u]).astype(bf16)
            k_dec = (ks_[u] * jnp.exp(g_last - gcs[u])).astype(bf16)
            sb = states[hh].astype(bf16)
            v_new = us[u] - jnp.dot(ws[u], sb, preferred_element_type=f32)
            vnb = v_new.astype(bf16)
            o = jnp.dot(q_dec, sb, preferred_element_type=f32) + jnp.dot(attns[u], vnb, preferred_element_type=f32)
            states[hh] = states[hh] * jnp.exp(g_last) + jnp.dot(k_dec.T, vnb, preferred_element_type=f32)
            o = o * lax.rsqrt(jnp.mean(o * o, axis=-1, keepdims=True) + NORM_EPS) * ng_ref[...]
            zt = z_ref[0, rs, hh * Dh:(hh + 1) * Dh]
            o_ref[0, rs, hh * Dh:(hh + 1) * Dh] = (o * (zt * jax.nn.sigmoid(zt))).astype(o_ref.dtype)
    for hh in range(HS):
        st_ref[hh] = states[hh]


def deltanet_mixer_pallas(proj3, conv_w, a_log, dt_bias, norm_g, B, S, T=256, HS=4):
    H, Dh = DN_HEADS, HEAD_DIM
    W = HS * Dh
    assert S % T == 0 and T % DN_CHUNK == 0 and H % HS == 0
    blk = lambda j: pl.BlockSpec((1, T, W), lambda b, h, t: (b, t, (OFF_DN + j * D_GRP) // W + h))
    prev = lambda j: pl.BlockSpec((1, SUB, W), lambda b, h, t: (b, jnp.maximum(t * (T // SUB) - 1, 0),
                                                               (OFF_DN + j * D_GRP) // W + h))
    per_head = pl.BlockSpec((HS, SUB, Dh), lambda b, h, t: (h, 0, 0))
    bcast = lambda x: jnp.broadcast_to(x.astype(f32)[:, None, None], (H, SUB, Dh))
    return pl.pallas_call(
        functools.partial(_deltanet_body, T=T, HS=HS),
        grid=(B, H // HS, S // T),
        in_specs=[blk(0), blk(1), blk(2), blk(3), prev(0), prev(1), prev(2),
                  pl.BlockSpec((1, T, LANES), lambda b, h, t: (b, t, OFF_SMALL // LANES)),
                  pl.BlockSpec((CONV_WIDTH, 3, W), lambda b, h, t: (0, 0, h)),
                  per_head, per_head,
                  pl.BlockSpec((1, Dh), lambda b, h, t: (0, 0))],
        out_specs=pl.BlockSpec((1, T, W), lambda b, h, t: (b, t, h)),
        out_shape=jax.ShapeDtypeStruct((B, S, D_GRP), bf16),
        scratch_shapes=[pltpu.VMEM((HS, Dh, Dh), f32)],
        compiler_params=pltpu.CompilerParams(dimension_semantics=("parallel", "parallel", "arbitrary"),
                                             vmem_limit_bytes=VMEM_LIMIT),
        name="deltanet",
    )(proj3, proj3, proj3, proj3, proj3, proj3, proj3, proj3, conv_w.reshape(CONV_WIDTH, 3, D_GRP),
      bcast(a_log), bcast(dt_bias), norm_g.reshape(1, Dh))


def _w_in_pieces():
    n0, r0, d0, w0 = 0, NSA_COLS, NSA_COLS + RET_COLS, NSA_COLS + RET_COLS + DN_COLS
    n_gate = 3 * NSA_HEADS
    return [
        (OFF_RW, w0, 3 * D_GRP),
        (OFF_RET, r0, 4 * D_GRP),
        (OFF_DN, d0, 3 * D_GRP),
        (OFF_DN + 3 * D_GRP, d0 + 3 * D_GRP + 2 * DN_HEADS, D_GRP),
        (OFF_NSA_Q, n0, D_GRP + 6 * NSA_KV_W),
        (OFF_LORA, w0 + 3 * D_GRP, RW_LORA),
        (OFF_SMALL, n0 + D_GRP + 6 * NSA_KV_W, n_gate),
        (OFF_SMALL + SMALL_DN_A, d0 + 3 * D_GRP, 2 * DN_HEADS),
    ]


def _pack_w_in_body(w_ref, o_ref):
    cols = o_ref.shape[1]
    for off, width in ((OFF_LORA + RW_LORA, RW_LORA_PAD - RW_LORA), (OFF_SMALL, LANES)):
        o_ref[off:off + width, :] = jnp.zeros((width, cols), o_ref.dtype)
    for dst, src, width in _w_in_pieces():
        o_ref[dst:dst + width, :] = w_ref[0, src:src + width, :].astype(o_ref.dtype)


def pack_w_in_t(w_in, layer, tc=256):
    K = w_in.shape[1]
    w_t = jnp.swapaxes(w_in, 1, 2)
    return pl.pallas_call(
        _pack_w_in_body,
        grid=(K // tc,),
        in_specs=[pl.BlockSpec((1, P_TOTAL, tc), lambda i: (layer, 0, i))],
        out_specs=pl.BlockSpec((P_PAD, tc), lambda i: (0, i)),
        out_shape=jax.ShapeDtypeStruct((P_PAD, K), bf16),
        compiler_params=pltpu.CompilerParams(dimension_semantics=("parallel",), vmem_limit_bytes=VMEM_LIMIT),
        name="pack_w_in",
    )(w_t)


def hybrid_mixer(h, hn, B, S, layer, w_in, w_out, ck_w1, ck_w2, ck_pe, cv_w1, cv_w2, cv_pe, dn_conv, dn_a_log,
                 dn_dt_bias, dn_norm, rw_mu, rw_w0, rw_w2, rw_a0, rw_a2, rw_g2, rw_kk, rw_ka, rw_rk, rw_ln_g, rw_ln_b):
    M = B * S
    proj = matmul_nt(hn, pack_w_in_t(w_in, layer), tm=2048, tk=1024, name="in_proj")
    p3 = proj.reshape(B, S, P_PAD)
    o_nsa = nsa_mixer_pallas(proj, ck_w1, ck_w2, ck_pe, cv_w1, cv_w2, cv_pe, B, S)
    o_ret = retention_mixer_pallas(p3, B, S)
    o_dn = deltanet_mixer_pallas(p3, dn_conv, dn_a_log, dn_dt_bias, dn_norm, B, S)
    o_rw = rwkv7_mixer(proj, OFF_RW // (3 * D_GRP), OFF_LORA // RW_LORA_PAD, rw_mu, rw_w0, rw_w2, rw_a0, rw_a2,
                       rw_g2, rw_kk, rw_ka, rw_rk, rw_ln_g, rw_ln_b, B, S)
    parts = [o.reshape(M, D_GRP) for o in (o_nsa, o_ret, o_dn, o_rw)]
    return out_proj_residual(parts, w_out.astype(bf16), h)


def kernel(x, p, norm_mix, w_in, w_out, nsa_ck_w1, nsa_ck_w2, nsa_ck_pe, nsa_cv_w1, nsa_cv_w2, nsa_cv_pe, dn_conv, dn_a_log, dn_dt_bias, dn_norm, rw_mu, rw_w0, rw_w2, rw_a0, rw_a2, rw_g2, rw_kk, rw_ka, rw_rk, rw_ln_g, rw_ln_b, norm_moe, moe_router_grp, moe_router_grp_b, moe_router_exp, moe_router_exp_b, moe_w_gate, moe_w_up, moe_w_down, norm_ple, ple_w, ple_gate, norm_final):
    B, S, D = x.shape
    M = B * S
    h = x.reshape(M, D)
    for i in range(DEPTH):
        hn = rms_norm_rows(h, norm_mix[i], bf16)
        h = hybrid_mixer(h, hn, B, S, i, w_in, w_out[i], nsa_ck_w1[i], nsa_ck_w2[i], nsa_ck_pe[i], nsa_cv_w1[i],
                         nsa_cv_w2[i], nsa_cv_pe[i], dn_conv[i], dn_a_log[i], dn_dt_bias[i], dn_norm[i],
                         rw_mu[i], rw_w0[i], rw_w2[i], rw_a0[i], rw_a2[i], rw_g2[i], rw_kk[i], rw_ka[i],
                         rw_rk[i], rw_ln_g[i], rw_ln_b[i])
        hn = rms_norm_rows(h, norm_moe[i], f32)
        h = h + hier_moe(hn, moe_router_grp[i], moe_router_grp_b[i], moe_router_exp[i], moe_router_exp_b[i],
                         moe_w_gate, moe_w_up, moe_w_down, i)
        hn = rms_norm_rows(h, norm_ple[i], bf16)
        h = ple_residual(hn, ple_gate[i].astype(bf16), p[i].reshape(M, PLE_DIM), ple_w[i].astype(bf16), h)
    return rms_norm_rows(h, norm_final, f32).reshape(B, S, D)
```

```python
import functools

import jax
import jax.numpy as jnp
import numpy as np
from jax import lax
from jax.experimental import pallas as pl
from jax.experimental.pallas import tpu as pltpu

D_MODEL = 4096
DEPTH = 2
f32 = jnp.float32
bf16 = jnp.bfloat16
D_MIX = D_MODEL
D_GRP = D_MIX // 4
HEAD_DIM = 128
NORM_EPS = 1e-6
NSA_HEADS = D_GRP // HEAD_DIM
NSA_KV_HEADS = 2
NSA_GROUP = NSA_HEADS // NSA_KV_HEADS
NSA_KV_W = NSA_KV_HEADS * HEAD_DIM
CMP_LEN = 32
CMP_STRIDE = 16
SEL_LEN = 64
N_SEL = 16
N_LOCAL_FORCED = 2
FORCED_SCORE = 1e4
WINDOW = 512
ROPE_THETA = 500000.0
ROPE_DIMS = HEAD_DIM // 4
RET_HEADS = D_GRP // HEAD_DIM
RET_CHUNK = 128
RET_THETA = 10000.0
DN_HEADS = D_GRP // HEAD_DIM
DN_CHUNK = 64
CONV_WIDTH = 4
RW_HEAD_DIM = 64
RW_DECAY_LORA = 64
RW_AAA_LORA = 64
RW_GATE_LORA = 160
RW_LN_EPS = 64e-5
N_GROUPS = 4
EXPERTS_PER_GROUP = 8
N_EXPERTS = N_GROUPS * EXPERTS_PER_GROUP
TOP_K = 2
D_EXPERT = 768
ROW_BLOCK = 256
PLE_DIM = 256
NSA_COLS = D_GRP + 6 * NSA_KV_W + 3 * NSA_HEADS
RET_COLS = 4 * D_GRP
DN_COLS = 4 * D_GRP + 2 * DN_HEADS
RW_COLS = 3 * D_GRP + RW_DECAY_LORA + RW_AAA_LORA + RW_GATE_LORA
P_TOTAL = NSA_COLS + RET_COLS + DN_COLS + RW_COLS

RW_LORA = RW_DECAY_LORA + RW_AAA_LORA + RW_GATE_LORA
RW_LORA_PAD = 384
RW_GROUPS_PER_BODY = 4
LANES = 128
SUB = 8

OFF_RW = 0
OFF_RET = 3 * D_GRP
OFF_DN = OFF_RET + 4 * D_GRP
OFF_NSA_Q = OFF_DN + 4 * D_GRP
OFF_NSA_KV = OFF_NSA_Q + D_GRP
OFF_LORA = OFF_NSA_KV + 6 * NSA_KV_W
OFF_SMALL = OFF_LORA + RW_LORA_PAD
SMALL_DN_A = 32
P_PAD = OFF_SMALL + LANES
assert P_PAD == 14336 and OFF_LORA % RW_LORA_PAD == 0

V7X_VMEM_BYTES = 64 * 1024 * 1024
VMEM_LIMIT = V7X_VMEM_BYTES * 7 // 8


def _split2(x):
    h = x.astype(bf16)
    return h, (x - h.astype(f32)).astype(bf16)


def _split3(x):
    h = x.astype(bf16)
    r1 = x - h.astype(f32)
    m = r1.astype(bf16)
    return h, m, (r1 - m.astype(f32)).astype(bf16)


def _rwkv_prep_body(x_ref, xprev_ref, l_ref, lprev_ref, mu_ref, mul_ref, w0_ref, a0_ref, kkw_ref, ka_ref,
                    w2_ref, a2_ref, g2_ref, e_ref, r_ref, w_ref, k_ref, v_ref, kk_ref, b_ref, g_ref, *, tm, S):
    i = pl.program_id(0)
    first = (i * tm) % S == 0

    def token_mix(x, prev8, mu):
        rows = lax.broadcasted_iota(jnp.int32, x.shape, 0)
        prev_row = jnp.where(first, 0.0, prev8[SUB - 1:SUB, :])
        x_prev = jnp.where(rows == 0, prev_row, pltpu.roll(x, 1, axis=0))
        return x + (x_prev - x) * mu

    x = token_mix(x_ref[...], xprev_ref[...], mu_ref[...])
    lora = token_mix(l_ref[...], lprev_ref[...], mul_ref[...])
    r = x[:, 0:D_GRP]
    k = x[:, D_GRP:2 * D_GRP]
    v = x[:, 2 * D_GRP:3 * D_GRP]
    dw = jnp.dot(jnp.tanh(lora).astype(bf16), w2_ref[...], preferred_element_type=f32)
    da = jnp.dot(lora.astype(bf16), a2_ref[...], preferred_element_type=f32)
    g = jnp.dot(jax.nn.sigmoid(lora).astype(bf16), g2_ref[...], preferred_element_type=f32)
    w = -jax.nn.softplus(-(w0_ref[...] + dw)) - 0.5
    decay = jnp.exp(-jnp.exp(w))
    a = jax.nn.sigmoid(a0_ref[...] + da)
    kq = k * kkw_ref[...]
    sq_h, sq_l = _split2(kq * kq)
    ss = (jnp.dot(sq_h, e_ref[...], preferred_element_type=f32)
          + jnp.dot(sq_l, e_ref[...], preferred_element_type=f32))
    kk = kq * lax.rsqrt(ss + 1e-6)
    r_ref[...] = r
    w_ref[...] = decay
    k_ref[...] = k * (1.0 + (a - 1.0) * ka_ref[...])
    v_ref[...] = v
    kk_ref[...] = kk
    b_ref[...] = kk * a
    g_ref[...] = g


def rwkv_prep(proj, rkv_blk, lora_blk, mu, w0, w2, a0, a2, g2, k_k, k_a, S, tm=256):
    M = proj.shape[0]
    assert M % tm == 0 and S % tm == 0
    W3 = 3 * D_GRP
    mu_x = mu[:W3].reshape(1, W3)
    mu_l = jnp.pad(mu[W3:], (0, RW_LORA_PAD - RW_LORA)).reshape(1, RW_LORA_PAD)
    w2p = jnp.zeros((RW_LORA_PAD, D_GRP), f32).at[0:RW_DECAY_LORA].set(w2).astype(bf16)
    a2p = jnp.zeros((RW_LORA_PAD, D_GRP), f32).at[RW_DECAY_LORA:RW_DECAY_LORA + RW_AAA_LORA].set(a2).astype(bf16)
    g2p = jnp.zeros((RW_LORA_PAD, D_GRP), f32).at[RW_DECAY_LORA + RW_AAA_LORA:RW_LORA].set(g2).astype(bf16)
    hid = np.arange(D_GRP) // RW_HEAD_DIM
    e = jnp.asarray(hid[:, None] == hid[None, :], bf16)
    row = lambda t: t.reshape(1, D_GRP)
    vec = pl.BlockSpec((1, D_GRP), lambda i: (0, 0))
    wspec = pl.BlockSpec((RW_LORA_PAD, D_GRP), lambda i: (0, 0))
    ospec = pl.BlockSpec((tm, D_GRP), lambda i: (i, 0))
    prev_row_blk = lambda i: jnp.maximum(i * (tm // SUB) - 1, 0)
    return pl.pallas_call(
        functools.partial(_rwkv_prep_body, tm=tm, S=S),
        grid=(M // tm,),
        in_specs=[pl.BlockSpec((tm, W3), lambda i: (i, rkv_blk)),
                  pl.BlockSpec((SUB, W3), lambda i: (prev_row_blk(i), rkv_blk)),
                  pl.BlockSpec((tm, RW_LORA_PAD), lambda i: (i, lora_blk)),
                  pl.BlockSpec((SUB, RW_LORA_PAD), lambda i: (prev_row_blk(i), lora_blk)),
                  pl.BlockSpec((1, W3), lambda i: (0, 0)),
                  pl.BlockSpec((1, RW_LORA_PAD), lambda i: (0, 0)),
                  vec, vec, vec, vec, wspec, wspec, wspec,
                  pl.BlockSpec((D_GRP, D_GRP), lambda i: (0, 0))],
        out_specs=[ospec] * 7,
        out_shape=[jax.ShapeDtypeStruct((M, D_GRP), f32)] * 7,
        compiler_params=pltpu.CompilerParams(dimension_semantics=("parallel",), vmem_limit_bytes=VMEM_LIMIT),
        name="rwkv_prep",
    )(proj, proj, proj, proj, mu_x, mu_l, row(w0), row(a0), row(k_k), row(k_a), w2p, a2p, g2p, e)


def _rwkv_scan_body(r_ref, w_ref, k_ref, v_ref, kk_ref, b_ref, g_ref, lng_ref, lnb_ref, rk_ref, o_ref, s_ref,
                    vc_ref, y_ref, *, G, T):
    tb = pl.program_id(2)

    @pl.when(tb == 0)
    def _():
        s_ref[...] = jnp.zeros_like(s_ref)

    Dh = RW_HEAD_DIM
    lo = lax.broadcasted_iota(jnp.int32, (Dh, LANES), 1) < Dh
    lo8 = lax.broadcasted_iota(jnp.int32, (SUB, LANES), 1) < Dh
    sub_iota = lax.broadcasted_iota(jnp.int32, (4 * SUB, LANES), 0)
    row_iota = lax.broadcasted_iota(jnp.int32, (SUB, LANES), 0)

    def seg_sum(x, m):
        s_lo = jnp.sum(jnp.where(m, x, 0.0), axis=-1, keepdims=True)
        s_hi = jnp.sum(jnp.where(m, 0.0, x), axis=-1, keepdims=True)
        return jnp.where(m, s_lo, s_hi)

    cols = [slice(p * LANES, (p + 1) * LANES) for p in range(G)]
    n_groups = T // SUB

    k_iota = lax.broadcasted_iota(jnp.int32, (8 * SUB, LANES), 0)
    k_head = k_iota // (4 * SUB) == lax.broadcasted_iota(jnp.int32, (8 * SUB, LANES), 1) // Dh
    k_piece = (k_iota // SUB) % 4

    def v_pieces(gi):
        t0 = pl.multiple_of(gi * SUB, SUB)
        vts = []
        for p in range(G):
            vh, vm, vl = _split3(v_ref[0, pl.ds(t0, SUB), cols[p]])
            vp = jnp.concatenate([vh, vm, vl, jnp.zeros_like(vh)], axis=0)
            vts.append(jnp.concatenate([vp[:, :Dh], vp[:, Dh:]], axis=0).T)
        return vts

    def v_columns(vts, i, slot):
        sel = (k_head & (k_piece < 3) & (k_iota % SUB == i)).astype(bf16)
        for p in range(G):
            vc_ref[slot, i, p] = jnp.dot(vts[p], sel, preferred_element_type=f32)

    vts0 = v_pieces(0)
    for i in range(SUB):
        v_columns(vts0, i, 0)

    def run_group(gi, slot, states):
        t0 = pl.multiple_of(gi * SUB, SUB)
        vts_next = v_pieces(jnp.minimum(gi + 1, n_groups - 1))
        r8b = [r_ref[0, pl.ds(t0, SUB), cols[p]].astype(bf16) for p in range(G)]
        yacc = [jnp.zeros((SUB, LANES), f32) for _ in range(G)]
        tiles = [[ref[0, pl.ds(t0, SUB), cols[p]] for ref in (kk_ref, w_ref, b_ref, k_ref)] for p in range(G)]
        for i in range(SUB):
            for p in range(G):
                kk_r, w_r, b_r, k_r = [jnp.broadcast_to(t8[i:i + 1, :], (Dh, LANES)) for t8 in tiles[p]]
                S = states[p]
                sab = -seg_sum(S * kk_r, lo)
                states[p] = S * w_r + sab * b_r + vc_ref[slot, i, p] * k_r
            for p in range(G):
                Sb = states[p].astype(bf16)
                zero = jnp.zeros_like(Sb)
                mt = jnp.concatenate([jnp.where(lo, Sb, zero), jnp.where(lo, zero, Sb)], axis=0)
                y8 = lax.dot_general(r8b[p], mt, (((1,), (1,)), ((), ())), preferred_element_type=f32)
                yacc[p] = jnp.where(row_iota == i, y8, yacc[p])
            v_columns(vts_next, i, 1 - slot)
        for p in range(G):
            y_ref[pl.ds(t0, SUB), cols[p]] = yacc[p]
        return states

    def group_batch(gj, carry):
        states = [s_ref[p] for p in range(G)]
        for u in range(RW_GROUPS_PER_BODY):
            states = run_group(RW_GROUPS_PER_BODY * gj + u, u % 2, states)
        for p in range(G):
            s_ref[p] = states[p]
        return carry

    assert n_groups % RW_GROUPS_PER_BODY == 0 and RW_GROUPS_PER_BODY % 2 == 0
    lax.fori_loop(0, n_groups // RW_GROUPS_PER_BODY, group_batch, 0)

    lo_t = lax.broadcasted_iota(jnp.int32, (T, LANES), 1) < Dh
    for p in range(G):
        y = y_ref[:, cols[p]]
        mean = seg_sum(y, lo_t) * (1.0 / Dh)
        d = y - mean
        var = seg_sum(d * d, lo_t) * (1.0 / Dh)
        yn = d * lax.rsqrt(var + RW_LN_EPS) * lng_ref[:, cols[p]] + lnb_ref[:, cols[p]]
        bonus = seg_sum(r_ref[0, :, cols[p]] * k_ref[0, :, cols[p]] * rk_ref[:, cols[p]], lo_t) * v_ref[0, :, cols[p]]
        o_ref[0, :, cols[p]] = (yn + bonus) * g_ref[0, :, cols[p]]


def rwkv_scan(r, w, k, v, kk, b, g, ln_g, ln_b, r_k, G=8, T=512):
    B, S, D = r.shape
    n_pairs = D // LANES
    assert n_pairs % G == 0 and S % T == 0
    spec = pl.BlockSpec((1, T, G * LANES), lambda bi, pi, ti: (bi, ti, pi))
    vec = pl.BlockSpec((1, G * LANES), lambda bi, pi, ti: (0, pi))
    row = lambda t: t.reshape(1, D)
    return pl.pallas_call(
        functools.partial(_rwkv_scan_body, G=G, T=T),
        grid=(B, n_pairs // G, S // T),
        in_specs=[spec] * 7 + [vec] * 3,
        out_specs=spec,
        out_shape=jax.ShapeDtypeStruct((B, S, D), f32),
        scratch_shapes=[pltpu.VMEM((G, RW_HEAD_DIM, LANES), f32),
                        pltpu.VMEM((2, SUB, G, RW_HEAD_DIM, LANES), f32),
                        pltpu.VMEM((T, G * LANES), f32)],
        compiler_params=pltpu.CompilerParams(dimension_semantics=("parallel", "parallel", "arbitrary"),
                                             vmem_limit_bytes=VMEM_LIMIT),
        name="rwkv_scan",
    )(r, w, k, v, kk, b, g, row(ln_g), row(ln_b), row(r_k))


def rwkv7_mixer(proj, rkv_blk, lora_blk, mu, w0, w2, a0, a2, g2, k_k, k_a, r_k, ln_g, ln_b, B, S):
    outs = rwkv_prep(proj, rkv_blk, lora_blk, mu, w0, w2, a0, a2, g2, k_k, k_a, S)
    r, w, k, v, kk, b, g = [t.reshape(B, S, D_GRP) for t in outs]
    return rwkv_scan(r, w, k, v, kk, b, g, ln_g, ln_b, r_k.reshape(-1))


def _mm_nt_body(a_ref, bt_ref, o_ref, acc_ref):
    k = pl.program_id(2)

    @pl.when(k == 0)
    def _():
        acc_ref[...] = jnp.zeros_like(acc_ref)

    acc_ref[...] += lax.dot_general(a_ref[...], bt_ref[...], (((1,), (1,)), ((), ())), preferred_element_type=f32)

    @pl.when(k == pl.num_programs(2) - 1)
    def _():
        o_ref[...] = acc_ref[...].astype(o_ref.dtype)


def matmul_nt(a, bt, out_dtype=f32, tm=1024, tn=1024, tk=2048, name="matmul_nt"):
    M, K = a.shape
    N, _ = bt.shape
    assert M % tm == 0 and N % tn == 0 and K % tk == 0, (M, N, K, tm, tn, tk)
    return pl.pallas_call(
        _mm_nt_body,
        grid=(M // tm, N // tn, K // tk),
        in_specs=[pl.BlockSpec((tm, tk), lambda i, j, k: (i, k)),
                  pl.BlockSpec((tn, tk), lambda i, j, k: (j, k))],
        out_specs=pl.BlockSpec((tm, tn), lambda i, j, k: (i, j)),
        out_shape=jax.ShapeDtypeStruct((M, N), out_dtype),
        scratch_shapes=[pltpu.VMEM((tm, tn), f32)],
        compiler_params=pltpu.CompilerParams(
            dimension_semantics=("parallel", "parallel", "arbitrary"), vmem_limit_bytes=VMEM_LIMIT),
        name=name,
    )(a, bt)


def _rms_norm_body(x_ref, g_ref, o_ref):
    x = x_ref[...]
    y = x * lax.rsqrt(jnp.mean(x * x, axis=-1, keepdims=True) + NORM_EPS)
    o_ref[...] = (y * g_ref[...]).astype(o_ref.dtype)


def rms_norm_rows(x, g, out_dtype, tm=256):
    M, D = x.shape
    return pl.pallas_call(
        _rms_norm_body,
        grid=(M // tm,),
        in_specs=[pl.BlockSpec((tm, D), lambda i: (i, 0)), pl.BlockSpec((1, D), lambda i: (0, 0))],
        out_specs=pl.BlockSpec((tm, D), lambda i: (i, 0)),
        out_shape=jax.ShapeDtypeStruct((M, D), out_dtype),
        compiler_params=pltpu.CompilerParams(dimension_semantics=("parallel",), vmem_limit_bytes=VMEM_LIMIT),
        name="rms_norm",
    )(x, g.reshape(1, D).astype(f32))


def _out_proj_body(a0_ref, a1_ref, a2_ref, a3_ref, b_ref, h_ref, o_ref, acc_ref):
    k = pl.program_id(2)

    @pl.when(k == 0)
    def _():
        acc_ref[...] = h_ref[...]

    for j, a_ref in enumerate((a0_ref, a1_ref, a2_ref, a3_ref)):
        @pl.when(k == j)
        def _(a_ref=a_ref):
            acc_ref[...] += jnp.dot(a_ref[...].astype(bf16), b_ref[...], preferred_element_type=f32)

    @pl.when(k == pl.num_programs(2) - 1)
    def _():
        o_ref[...] = acc_ref[...]


def out_proj_residual(parts, w, h, tm=1024, tn=1024):
    M, D = h.shape
    aspec = pl.BlockSpec((tm, D_GRP), lambda i, j, k: (i, 0))
    return pl.pallas_call(
        _out_proj_body,
        grid=(M // tm, D // tn, len(parts)),
        in_specs=[aspec] * 4 + [pl.BlockSpec((D_GRP, tn), lambda i, j, k: (k, j)),
                                pl.BlockSpec((tm, tn), lambda i, j, k: (i, j))],
        out_specs=pl.BlockSpec((tm, tn), lambda i, j, k: (i, j)),
        out_shape=jax.ShapeDtypeStruct((M, D), f32),
        scratch_shapes=[pltpu.VMEM((tm, tn), f32)],
        compiler_params=pltpu.CompilerParams(
            dimension_semantics=("parallel", "parallel", "arbitrary"), vmem_limit_bytes=VMEM_LIMIT),
        name="out_proj",
    )(*parts, w, h)


def _ple_body(a_ref, b_ref, p_ref, pw_ref, h_ref, o_ref, acc_ref):
    k = pl.program_id(2)

    @pl.when(k == 0)
    def _():
        acc_ref[...] = jnp.zeros_like(acc_ref)

    acc_ref[...] += jnp.dot(a_ref[...], b_ref[...], preferred_element_type=f32)

    @pl.when(k == pl.num_programs(2) - 1)
    def _():
        emb = jnp.dot(p_ref[...].astype(bf16), pw_ref[...], preferred_element_type=f32)
        o_ref[...] = h_ref[...] + emb * jax.nn.sigmoid(acc_ref[...])


def ple_residual(hn, gate_w, p, ple_w, h, tm=1024, tn=1024, tk=2048):
    M, D = h.shape
    return pl.pallas_call(
        _ple_body,
        grid=(M // tm, D // tn, D // tk),
        in_specs=[pl.BlockSpec((tm, tk), lambda i, j, k: (i, k)),
                  pl.BlockSpec((tk, tn), lambda i, j, k: (k, j)),
                  pl.BlockSpec((tm, PLE_DIM), lambda i, j, k: (i, 0)),
                  pl.BlockSpec((PLE_DIM, tn), lambda i, j, k: (0, j)),
                  pl.BlockSpec((tm, tn), lambda i, j, k: (i, j))],
        out_specs=pl.BlockSpec((tm, tn), lambda i, j, k: (i, j)),
        out_shape=jax.ShapeDtypeStruct((M, D), f32),
        scratch_shapes=[pltpu.VMEM((tm, tn), f32)],
        compiler_params=pltpu.CompilerParams(
            dimension_semantics=("parallel", "parallel", "arbitrary"), vmem_limit_bytes=VMEM_LIMIT),
        name="ple",
    )(hn, gate_w, p, ple_w, h)


def _moe_router_body(x_ref, w_ref, b_ref, id_ref, wt_ref):
    logits = jnp.dot(x_ref[...].astype(bf16), w_ref[...], preferred_element_type=f32) + b_ref[...]
    lane = lax.broadcasted_iota(jnp.int32, logits.shape, 1).astype(f32)
    ninf = -jnp.inf

    def top1(vals):
        m = jnp.max(vals, axis=-1, keepdims=True)
        idx = jnp.min(jnp.where(vals == m, lane, float(LANES)), axis=-1, keepdims=True)
        return m, idx

    gl = jnp.where(lane < N_GROUPS, logits, ninf)
    g_max, g_idx = top1(gl)
    g_w = 1.0 / jnp.sum(jnp.exp(gl - g_max), axis=-1, keepdims=True)
    lo = N_GROUPS + g_idx * EXPERTS_PER_GROUP
    el = jnp.where((lane >= lo) & (lane < lo + EXPERTS_PER_GROUP), logits, ninf)
    e_max, _ = top1(el)
    pe = jnp.exp(el - e_max)
    p = pe / jnp.sum(pe, axis=-1, keepdims=True)
    p = jnp.where(el == ninf, ninf, p)
    p1, i1 = top1(p)
    p2, i2 = top1(jnp.where(lane == i1, ninf, p))
    scale = g_w / (p1 + p2)
    id_ref[...] = jnp.where(lane == 0, i1 - N_GROUPS, jnp.where(lane == 1, i2 - N_GROUPS, 0.0)).astype(jnp.int32)
    wt_ref[...] = jnp.where(lane == 0, p1 * scale, jnp.where(lane == 1, p2 * scale, 0.0))


def moe_router(xb, router_grp, router_grp_b, router_exp, router_exp_b, tm=512):
    M, D = xb.shape
    n_log = N_GROUPS + N_EXPERTS
    w = jnp.pad(jnp.concatenate([router_grp, router_exp], axis=1), ((0, 0), (0, LANES - n_log))).astype(bf16)
    b = jnp.pad(jnp.concatenate([router_grp_b, router_exp_b]), (0, LANES - n_log)).reshape(1, LANES).astype(f32)
    ids, wts = pl.pallas_call(
        _moe_router_body,
        grid=(M // tm,),
        in_specs=[pl.BlockSpec((tm, D), lambda i: (i, 0)), pl.BlockSpec((D, LANES), lambda i: (0, 0)),
                  pl.BlockSpec((1, LANES), lambda i: (0, 0))],
        out_specs=[pl.BlockSpec((tm, LANES), lambda i: (i, 0))] * 2,
        out_shape=[jax.ShapeDtypeStruct((M, LANES), jnp.int32), jax.ShapeDtypeStruct((M, LANES), f32)],
        compiler_params=pltpu.CompilerParams(dimension_semantics=("parallel",), vmem_limit_bytes=VMEM_LIMIT),
        name="moe_router",
    )(xb, w, b)
    return ids[:, :TOP_K], wts[:, :TOP_K]


def _moe_up_body(be_ref, nu_ref, x_ref, wg_ref, wu_ref, h_ref, *, tf):
    i = pl.program_id(0)

    @pl.when(i < nu_ref[0])
    def _():
        x = x_ref[...].astype(bf16)
        for f in range(D_EXPERT // tf):
            c = slice(f * tf, (f + 1) * tf)
            hg = jnp.dot(x, wg_ref[0, 0, :, c].astype(bf16), preferred_element_type=f32)
            hu = jnp.dot(x, wu_ref[0, 0, :, c].astype(bf16), preferred_element_type=f32)
            h_ref[:, c] = (hg * jax.nn.sigmoid(hg) * hu).astype(bf16)

    @pl.when(i >= nu_ref[0])
    def _():
        h_ref[...] = jnp.zeros_like(h_ref)


def _moe_down_body(be_ref, nu_ref, h_ref, wd_ref, wt_ref, y_ref):
    i = pl.program_id(0)

    @pl.when(i < nu_ref[0])
    def _():
        y_ref[...] = jnp.dot(h_ref[...], wd_ref[0, 0].astype(bf16), preferred_element_type=f32) * wt_ref[...]

    @pl.when(i >= nu_ref[0])
    def _():
        y_ref[...] = jnp.zeros_like(y_ref)


def moe_experts(x_pad, w_pad, blk_expert, n_used, wg, wu, wd, layer, tf=256):
    A_pad, D = x_pad.shape
    n_blk = A_pad // ROW_BLOCK
    wspec = pl.BlockSpec((1, 1, D, D_EXPERT), lambda i, be, nu: (layer, be[i], 0, 0), pipeline_mode=pl.Buffered(1))
    h = pl.pallas_call(
        functools.partial(_moe_up_body, tf=tf),
        grid_spec=pltpu.PrefetchScalarGridSpec(
            num_scalar_prefetch=2,
            grid=(n_blk,),
            in_specs=[pl.BlockSpec((ROW_BLOCK, D), lambda i, be, nu: (i, 0)), wspec, wspec],
            out_specs=pl.BlockSpec((ROW_BLOCK, D_EXPERT), lambda i, be, nu: (i, 0)),
        ),
        out_shape=jax.ShapeDtypeStruct((A_pad, D_EXPERT), bf16),
        compiler_params=pltpu.CompilerParams(dimension_semantics=("arbitrary",), vmem_limit_bytes=VMEM_LIMIT),
        name="moe_up",
    )(blk_expert, n_used, x_pad, wg, wu)
    return pl.pallas_call(
        _moe_down_body,
        grid_spec=pltpu.PrefetchScalarGridSpec(
            num_scalar_prefetch=2,
            grid=(n_blk,),
            in_specs=[pl.BlockSpec((ROW_BLOCK, D_EXPERT), lambda i, be, nu: (i, 0)),
                      pl.BlockSpec((1, 1, D_EXPERT, D), lambda i, be, nu: (layer, be[i], 0, 0)),
                      pl.BlockSpec((ROW_BLOCK, 1), lambda i, be, nu: (i, 0))],
            out_specs=pl.BlockSpec((ROW_BLOCK, D), lambda i, be, nu: (i, 0)),
        ),
        out_shape=jax.ShapeDtypeStruct((A_pad, D), f32),
        compiler_params=pltpu.CompilerParams(dimension_semantics=("arbitrary",), vmem_limit_bytes=VMEM_LIMIT),
        name="moe_down",
    )(blk_expert, n_used, h, wd, w_pad.reshape(A_pad, 1))


def hier_moe(xb, router_grp, router_grp_b, router_exp, router_exp_b, w_gate, w_up, w_down, layer):
    M, D = xb.shape
    expert_ids, weights = moe_router(xb, router_grp, router_grp_b, router_exp, router_exp_b)
    A = M * TOP_K
    e_flat = expert_ids.reshape(A)
    onehot = (e_flat[:, None] == jnp.arange(N_EXPERTS)[None, :]).astype(jnp.int32)
    rank = jnp.take_along_axis(jnp.cumsum(onehot, axis=0), e_flat[:, None], axis=1)[:, 0] - 1
    counts = jnp.sum(onehot, axis=0)
    padded = (counts + ROW_BLOCK - 1) // ROW_BLOCK * ROW_BLOCK
    end_pad = jnp.cumsum(padded)
    dest = (end_pad - padded)[e_flat] + rank
    n_blk = -(-A // ROW_BLOCK) + N_EXPERTS
    A_pad = n_blk * ROW_BLOCK
    tok_pad = (jnp.arange(A_pad, dtype=jnp.int32) % M).at[dest].set(jnp.arange(A, dtype=jnp.int32) // TOP_K)
    w_pad = jnp.zeros((A_pad,), f32).at[dest].set(weights.reshape(A))
    blk_expert = jnp.minimum(jnp.searchsorted(end_pad, jnp.arange(n_blk) * ROW_BLOCK, side='right'),
                             N_EXPERTS - 1).astype(jnp.int32)
    n_used = (end_pad[-1:] // ROW_BLOCK).astype(jnp.int32)
    x_pad = xb[tok_pad]
    y_pad = moe_experts(x_pad, w_pad, blk_expert, n_used, w_gate, w_up, w_down, layer)
    rows = dest.reshape(M, TOP_K)
    return y_pad[rows[:, 0]] + y_pad[rows[:, 1]]


def _rope_tables(S, theta, rot_dims, width):
    half = rot_dims // 2
    freqs = theta ** (-np.arange(half, dtype=np.float64) / half)
    ang = np.arange(S, dtype=np.float64)[:, None] * freqs[None, :]
    cos = np.ones((S, width), np.float64)
    sin_lo = np.zeros((S, width), np.float64)
    sin_hi = np.zeros((S, width), np.float64)
    cos[:, :half] = np.cos(ang)
    cos[:, half:rot_dims] = np.cos(ang)
    sin_lo[:, :half] = -np.sin(ang)
    sin_hi[:, half:rot_dims] = np.sin(ang)
    return [jnp.asarray(t, f32) for t in (cos, sin_lo, sin_hi)]


def _rope128(x, cos, sin_lo, sin_hi, half):
    return x * cos + pltpu.roll(x, LANES - half, axis=1) * sin_lo + pltpu.roll(x, half, axis=1) * sin_hi


def _nsa_prep_body(q_ref, kv_ref, cos_ref, sl_ref, sh_ref, qo_ref, kvo_ref, co_ref):
    cos, sl, sh = cos_ref[...], sl_ref[...], sh_ref[...]
    half = ROPE_DIMS // 2
    scale = HEAD_DIM ** -0.5
    for h in range(NSA_HEADS):
        c = slice(h * LANES, (h + 1) * LANES)
        qo_ref[:, c] = (_rope128(q_ref[:, c], cos, sl, sh, half) * scale).astype(bf16)
    for j in range(6):
        for g in range(NSA_KV_HEADS):
            c = slice((j * NSA_KV_HEADS + g) * LANES, (j * NSA_KV_HEADS + g + 1) * LANES)
            x = kv_ref[:, c]
            if j % 2 == 0:
                x = _rope128(x, cos, sl, sh, half)
            if j < 2:
                co_ref[:, c] = x.astype(bf16)
            else:
                kvo_ref[:, slice(c.start - 4 * LANES, c.stop - 4 * LANES)] = x.astype(bf16)


def nsa_prep(proj, S, tm=512):
    M = proj.shape[0]
    assert M % tm == 0 and S % tm == 0
    tabs = _rope_tables(S, ROPE_THETA, ROPE_DIMS, LANES)
    tspec = pl.BlockSpec((tm, LANES), lambda i: (i % (S // tm), 0))
    return pl.pallas_call(
        _nsa_prep_body,
        grid=(M // tm,),
        in_specs=[pl.BlockSpec((tm, D_GRP), lambda i: (i, OFF_NSA_Q // D_GRP)),
                  pl.BlockSpec((tm, 6 * NSA_KV_W), lambda i: (i, OFF_NSA_KV // (6 * NSA_KV_W))),
                  tspec, tspec, tspec],
        out_specs=[pl.BlockSpec((tm, D_GRP), lambda i: (i, 0)),
                   pl.BlockSpec((tm, 4 * NSA_KV_W), lambda i: (i, 0)),
                   pl.BlockSpec((tm, 2 * NSA_KV_W), lambda i: (i, 0))],
        out_shape=[jax.ShapeDtypeStruct((M, D_GRP), bf16), jax.ShapeDtypeStruct((M, 4 * NSA_KV_W), bf16),
                   jax.ShapeDtypeStruct((M, 2 * NSA_KV_W), bf16)],
        compiler_params=pltpu.CompilerParams(dimension_semantics=("parallel",), vmem_limit_bytes=VMEM_LIMIT),
        name="nsa_prep",
    )(proj, proj, *tabs)


def _nsa_compress_body(r_ref, w1_ref, w2_ref, pe_ref, o_ref):
    r = r_ref[0, 0, 0]
    half_w = CMP_STRIDE * HEAD_DIM
    a = jnp.dot(r, w1_ref[0, :half_w, :], preferred_element_type=f32)
    b = jnp.dot(r, w1_ref[0, half_w:, :], preferred_element_type=f32)
    n = r.shape[0]
    pe = jnp.dot(jnp.broadcast_to(pe_ref[0], (SUB, pe_ref.shape[-1])), w1_ref[0], preferred_element_type=f32)[0:1]
    pre = a + pltpu.roll(b, n - 1, axis=0) + pe
    hid = pre * jax.nn.sigmoid(pre)
    o_ref[0, 0, 0] = jnp.dot(hid.astype(bf16), w2_ref[0], preferred_element_type=f32).astype(bf16)


def nsa_compress(c_rot, w1s, w2s, pes, B, S):
    G, Dh = NSA_KV_HEADS, HEAD_DIM
    nc = S // CMP_STRIDE
    r = c_rot.reshape(B, nc, CMP_STRIDE, 2, G, Dh).transpose(3, 0, 4, 1, 2, 5).reshape(2, B, G, nc, CMP_STRIDE * Dh)
    w1 = jnp.stack(w1s).astype(bf16)
    w2 = jnp.stack(w2s).astype(bf16)
    pe = jnp.stack(pes).reshape(2, 1, CMP_LEN * Dh).astype(bf16)
    return pl.pallas_call(
        _nsa_compress_body,
        grid=(2, B, G),
        in_specs=[pl.BlockSpec((1, 1, 1, nc, CMP_STRIDE * Dh), lambda j, b, g: (j, b, g, 0, 0)),
                  pl.BlockSpec((1, CMP_LEN * Dh, Dh), lambda j, b, g: (j, 0, 0)),
                  pl.BlockSpec((1, Dh, Dh), lambda j, b, g: (j, 0, 0)),
                  pl.BlockSpec((1, 1, CMP_LEN * Dh), lambda j, b, g: (j, 0, 0))],
        out_specs=pl.BlockSpec((1, 1, 1, nc, Dh), lambda j, b, g: (j, b, g, 0, 0)),
        out_shape=jax.ShapeDtypeStruct((2, B, G, nc, Dh), bf16),
        compiler_params=pltpu.CompilerParams(dimension_semantics=("parallel", "parallel", "parallel"),
                                             vmem_limit_bytes=VMEM_LIMIT),
        name="nsa_compress",
    )(r, w1, w2, pe)


def _softmax_rows(s, mask):
    s = jnp.where(mask, s, -jnp.inf)
    m = jnp.max(s, axis=-1, keepdims=True)
    m = jnp.where(m == -jnp.inf, 0.0, m)
    e = jnp.exp(s - m)
    return e, jnp.maximum(jnp.sum(e, axis=-1, keepdims=True), jnp.finfo(f32).tiny)


def _nsa_attn_body(q_ref, cmp_ref, ks_ref, vs_ref, kw_ref, vw_ref, gate_ref, ovt_ref, o_ref, *, tq, tk, S):
    g = pl.program_id(1)
    qi = pl.program_id(2)
    R, Dh = NSA_GROUP, HEAD_DIM
    rows = R * tq
    n_blk = S // SEL_LEN
    n_cmp = S // CMP_STRIDE - 1
    q0 = qi * tq
    q4 = jnp.concatenate([q_ref[0, :, r * Dh:(r + 1) * Dh] for r in range(R)], axis=0)
    nt = (((1,), (1,)), ((), ()))

    def qpos(shape):
        return q0 + lax.broadcasted_iota(jnp.int32, shape, 0) % tq

    k_cmp, v_cmp = cmp_ref[0, 0, 0], cmp_ref[1, 0, 0]
    ncp = k_cmp.shape[0]
    s = lax.dot_general(q4, k_cmp, nt, preferred_element_type=f32)
    n_idx = lax.broadcasted_iota(jnp.int32, (rows, ncp), 1)
    mask = (n_idx * CMP_STRIDE + (CMP_LEN - 1) <= qpos((rows, ncp))) & (n_idx < n_cmp)
    e, den = _softmax_rows(s, mask)
    p = e / den
    o_cmp = jnp.dot(p.astype(bf16), v_cmp, preferred_element_type=f32)

    psum = p[0:tq]
    for r in range(1, R):
        psum = psum + p[r * tq:(r + 1) * tq]
    ph, plo = _split2(psum)
    imp = (lax.dot_general(ovt_ref[...], ph, nt, preferred_element_type=f32)
           + lax.dot_general(ovt_ref[...], plo, nt, preferred_element_type=f32))
    blk = lax.broadcasted_iota(jnp.int32, (n_blk, tq), 0)
    cur = (q0 + lax.broadcasted_iota(jnp.int32, (n_blk, tq), 1)) // SEL_LEN
    visible = blk <= cur
    forced = (blk == 0) | (visible & (blk > cur - N_LOCAL_FORCED))
    imp = jnp.where(forced, FORCED_SCORE, jnp.where(visible, imp, -1.0))
    sel_t = jnp.zeros((n_blk, tq), f32)
    blk_f = blk.astype(f32)
    for _ in range(min(N_SEL, n_blk)):
        m = jnp.max(imp, axis=0, keepdims=True)
        first = jnp.min(jnp.where(imp == m, blk_f, float(n_blk)), axis=0, keepdims=True)
        hit = blk_f == first
        sel_t = jnp.where(hit, 1.0, sel_t)
        imp = jnp.where(hit, -jnp.inf, imp)
    sel_q = jnp.concatenate([sel_t.T, jnp.ones((tq, LANES - n_blk), f32)], axis=1).astype(bf16)

    blocks_per_tile = tk // SEL_LEN

    big = float(2 ** 30)
    kb = sel_q.shape[1]

    def sel_step(kt, carry, causal):
        m_run, l_run, acc = carry
        k0 = pl.multiple_of(kt * tk, tk)
        kt_k = ks_ref[0, pl.ds(k0, tk), :]
        kt_v = vs_ref[0, pl.ds(k0, tk), :]
        s = lax.dot_general(q4, kt_k, nt, preferred_element_type=f32)
        row = lax.broadcasted_iota(jnp.int32, (kb, tk), 0)
        in_blk = row == kt * blocks_per_tile + lax.broadcasted_iota(jnp.int32, (kb, tk), 1) // SEL_LEN
        expand = jnp.where(in_blk, big, jnp.where(row == n_blk, -big, 0.0)).astype(bf16)
        bias = jnp.dot(sel_q, expand, preferred_element_type=f32)
        s = (s.reshape(R, tq, tk) + bias[None]).reshape(rows, tk)
        if causal:
            kpos = k0 + lax.broadcasted_iota(jnp.int32, (rows, tk), 1)
            s = jnp.where(kpos <= qpos((rows, tk)), s, -big)
        m_new = jnp.maximum(m_run, jnp.max(s, axis=-1, keepdims=True))
        alpha = jnp.exp(m_run - m_new)
        e = jnp.exp(s - m_new)
        l_new = alpha * l_run + jnp.sum(e, axis=-1, keepdims=True)
        acc = alpha * acc + jnp.dot(e.astype(bf16), kt_v, preferred_element_type=f32)
        return m_new, l_new, acc

    last_kt = (q0 + tq - 1) // tk
    init = (jnp.full((rows, 1), -jnp.inf, f32), jnp.zeros((rows, 1), f32), jnp.zeros((rows, Dh), f32))
    carry = lax.fori_loop(0, last_kt, functools.partial(sel_step, causal=False), init)
    _, l_sel, acc_sel = sel_step(last_kt, carry, causal=True)
    o_sel = acc_sel / l_sel

    span = WINDOW + tq
    w0 = pl.multiple_of(jnp.maximum(q0 - WINDOW, 0), tq)
    kwin = kw_ref[0, pl.ds(w0, span), :]
    vwin = vw_ref[0, pl.ds(w0, span), :]
    s = lax.dot_general(q4, kwin, nt, preferred_element_type=f32)
    kpos = w0 + lax.broadcasted_iota(jnp.int32, (rows, span), 1)
    qp = qpos((rows, span))
    e, den = _softmax_rows(s, (kpos <= qp) & (kpos > qp - WINDOW))
    o_win = jnp.dot(e.astype(bf16), vwin, preferred_element_type=f32) / den

    gate = jax.nn.sigmoid(gate_ref[0])
    for r in range(R):
        c0 = (g * R + r) * 3
        lane = lax.broadcasted_iota(jnp.int32, (tq, LANES), 1)
        gsel = lambda c: jnp.sum(jnp.where(lane == c0 + c, gate, 0.0), axis=-1, keepdims=True)
        rs = slice(r * tq, (r + 1) * tq)
        o_ref[0, :, r * Dh:(r + 1) * Dh] = (gsel(0) * o_cmp[rs] + gsel(1) * o_sel[rs]
                                            + gsel(2) * o_win[rs]).astype(o_ref.dtype)


def nsa_attention(q_rot, kv_rot, cmp_kv, proj3, B, S, tq=256, tk=512):
    G, R, Dh = NSA_KV_HEADS, NSA_GROUP, HEAD_DIM
    n_blk = S // SEL_LEN
    assert S % tk == 0 and tk % tq == 0 and S >= WINDOW + tq and n_blk % SUB == 0 and n_blk < LANES
    nc = S // CMP_STRIDE
    c0 = np.arange(nc - 1)[None, :] * CMP_STRIDE
    s0 = np.arange(n_blk)[:, None] * SEL_LEN
    ovt = np.zeros((n_blk, nc), np.float32)
    ovt[:, :nc - 1] = np.clip(np.minimum(c0 + CMP_LEN, s0 + SEL_LEN) - np.maximum(c0, s0), 0, None) / CMP_LEN
    kvspec = lambda j: pl.BlockSpec((1, S, Dh), lambda b, g, i: (b, 0, j * G + g))
    return pl.pallas_call(
        functools.partial(_nsa_attn_body, tq=tq, tk=tk, S=S),
        grid=(B, G, S // tq),
        in_specs=[pl.BlockSpec((1, tq, R * Dh), lambda b, g, i: (b, i, g)),
                  pl.BlockSpec((2, 1, 1, nc, Dh), lambda b, g, i: (0, b, g, 0, 0)),
                  kvspec(0), kvspec(1), kvspec(2), kvspec(3),
                  pl.BlockSpec((1, tq, LANES), lambda b, g, i: (b, i, OFF_SMALL // LANES)),
                  pl.BlockSpec((n_blk, nc), lambda b, g, i: (0, 0))],
        out_specs=pl.BlockSpec((1, tq, R * Dh), lambda b, g, i: (b, i, g)),
        out_shape=jax.ShapeDtypeStruct((B, S, D_GRP), bf16),
        compiler_params=pltpu.CompilerParams(dimension_semantics=("parallel", "parallel", "arbitrary"),
                                             vmem_limit_bytes=VMEM_LIMIT),
        name="nsa_attention",
    )(q_rot, cmp_kv, kv_rot, kv_rot, kv_rot, kv_rot, proj3, jnp.asarray(ovt, bf16))


def nsa_mixer_pallas(proj, ck_w1, ck_w2, ck_pe, cv_w1, cv_w2, cv_pe, B, S):
    q_rot, kv_rot, c_rot = nsa_prep(proj, S)
    cmp_kv = nsa_compress(c_rot, (ck_w1, cv_w1), (ck_w2, cv_w2), (ck_pe, cv_pe), B, S)
    return nsa_attention(q_rot.reshape(B, S, D_GRP), kv_rot.reshape(B, S, 4 * NSA_KV_W), cmp_kv,
                         proj.reshape(B, S, P_PAD), B, S)


def _retention_body(q_ref, k_ref, v_ref, g_ref, cos_ref, sin_ref, dec_ref, xi_ref, zeta_ref, gc_ref, o_ref, st_ref,
                    *, T):
    @pl.when(pl.program_id(2) == 0)
    def _():
        st_ref[...] = jnp.zeros_like(st_ref)

    C, Dh = RET_CHUNK, HEAD_DIM
    nt = (((1,), (1,)), ((), ()))
    decay_in, xi, zeta, gamma_c = dec_ref[0], xi_ref[0], zeta_ref[0], gc_ref[0, 0:1, :]
    state = st_ref[...]
    for c in range(T // C):
        rs = slice(c * C, (c + 1) * C)
        cos, sin = cos_ref[rs, :], sin_ref[rs, :]
        rope = lambda x: x * cos + pltpu.roll(x, Dh // 2, axis=1) * sin
        q = rope(q_ref[0, rs, :])
        k = rope(k_ref[0, rs, :]) * Dh ** -0.5
        qb, vb = q.astype(bf16), v_ref[0, rs, :].astype(bf16)
        inner = lax.dot_general(qb, k.astype(bf16), nt, preferred_element_type=f32) * decay_in
        o = (jnp.dot(inner.astype(bf16), vb, preferred_element_type=f32)
             + jnp.dot(qb, state.astype(bf16), preferred_element_type=f32) * xi)
        kz = (k * zeta).astype(bf16)
        state = state * gamma_c + jnp.dot(kz.T, vb, preferred_element_type=f32)
        mu = jnp.mean(o, axis=-1, keepdims=True)
        d = o - mu
        o = d * lax.rsqrt(jnp.mean(d * d, axis=-1, keepdims=True) + NORM_EPS)
        gt = g_ref[0, rs, :]
        o_ref[0, rs, :] = (gt * jax.nn.sigmoid(gt) * o).astype(o_ref.dtype)
    st_ref[...] = state


def retention_mixer_pallas(proj3, B, S, T=512):
    H, Dh, C = RET_HEADS, HEAD_DIM, RET_CHUNK
    assert S % T == 0 and T % C == 0
    freqs = RET_THETA ** (-np.linspace(0.0, 1.0, Dh // 2))
    ang = np.arange(S, dtype=np.float64)[:, None] * freqs[None, :]
    cos = jnp.asarray(np.concatenate([np.cos(ang), np.cos(ang)], 1), f32)
    sin = jnp.asarray(np.concatenate([-np.sin(ang), np.sin(ang)], 1), f32)
    log_gamma = np.log1p(-(2.0 ** (-5.0 - np.arange(H, dtype=np.float64))))
    n = np.arange(C, dtype=np.float64)
    diff = n[:, None] - n[None, :]
    decay_in = np.where(diff >= 0, np.exp(log_gamma[:, None, None] * np.maximum(diff, 0.0)), 0.0)
    xi = np.exp(log_gamma[:, None] * (n + 1.0))[:, :, None] * np.ones((1, 1, Dh))
    zeta = np.exp(log_gamma[:, None] * (C - 1.0 - n))[:, :, None] * np.ones((1, 1, Dh))
    gamma_c = np.exp(log_gamma * C)[:, None, None] * np.ones((1, SUB, Dh))
    blk = lambda j: pl.BlockSpec((1, T, Dh), lambda b, h, t: (b, t, (OFF_RET + j * D_GRP) // Dh + h))
    tab = pl.BlockSpec((T, Dh), lambda b, h, t: (t, 0))
    per_head = lambda r: pl.BlockSpec((1, r, Dh), lambda b, h, t: (h, 0, 0))
    return pl.pallas_call(
        functools.partial(_retention_body, T=T),
        grid=(B, H, S // T),
        in_specs=[blk(0), blk(1), blk(2), blk(3), tab, tab, per_head(C), per_head(C), per_head(C), per_head(SUB)],
        out_specs=pl.BlockSpec((1, T, Dh), lambda b, h, t: (b, t, h)),
        out_shape=jax.ShapeDtypeStruct((B, S, D_GRP), bf16),
        scratch_shapes=[pltpu.VMEM((Dh, Dh), f32)],
        compiler_params=pltpu.CompilerParams(dimension_semantics=("parallel", "parallel", "arbitrary"),
                                             vmem_limit_bytes=VMEM_LIMIT),
        name="retention",
    )(proj3, proj3, proj3, proj3, cos, sin, jnp.asarray(decay_in, f32), jnp.asarray(xi, f32),
      jnp.asarray(zeta, f32), jnp.asarray(gamma_c, f32))


def _dot3(a, b):
    ah, al = _split2(a)
    bh, bl = _split2(b)
    return (jnp.dot(ah, bh, preferred_element_type=f32) + jnp.dot(ah, bl, preferred_element_type=f32)
            + jnp.dot(al, bh, preferred_element_type=f32))


def _deltanet_body(q_ref, k_ref, v_ref, z_ref, qp_ref, kp_ref, vp_ref, ab_ref, cw_ref, alog_ref, dtb_ref, ng_ref,
                   o_ref, st_ref, *, T, HS):
    hb = pl.program_id(1)
    first = pl.program_id(2) == 0

    @pl.when(first)
    def _():
        st_ref[...] = jnp.zeros_like(st_ref)

    C, Dh = DN_CHUNK, HEAD_DIM
    nt = (((1,), (1,)), ((), ()))
    n_chunks = T // C
    ab = ab_ref[0]
    lane = lax.broadcasted_iota(jnp.int32, ab.shape, 1)

    def conv_silu(x_ref, prev_ref, j, hc):
        prev = jnp.where(first, 0.0, prev_ref[0, :, hc])
        xcat = jnp.concatenate([prev, x_ref[0, :, hc]], axis=0)
        w = cw_ref[:, j, hc]
        y = sum(w[i:i + 1, :] * xcat[SUB - (CONV_WIDTH - 1) + i:SUB - (CONV_WIDTH - 1) + i + T, :]
                for i in range(CONV_WIDTH))
        return y * jax.nn.sigmoid(y)

    def l2n(x):
        return x * lax.rsqrt(jnp.sum(x * x, axis=-1, keepdims=True) + 1e-6)

    units = [(hh, slice(c * C, (c + 1) * C)) for hh in range(HS) for c in range(n_chunks)]
    qs, ks_, vs_, betas, g_rows = [], [], [], [], []
    for hh in range(HS):
        hc = slice(hh * Dh, (hh + 1) * Dh)
        head = hb * HS + hh
        q_all = l2n(conv_silu(q_ref, qp_ref, 0, hc)) * Dh ** -0.5
        k_all = l2n(conv_silu(k_ref, kp_ref, 1, hc))
        v_all = conv_silu(v_ref, vp_ref, 2, hc)
        a_col = jnp.sum(jnp.where(lane == SMALL_DN_A + head, ab, 0.0), axis=-1, keepdims=True)
        b_col = jnp.sum(jnp.where(lane == SMALL_DN_A + DN_HEADS + head, ab, 0.0), axis=-1, keepdims=True)
        beta_all = jax.nn.sigmoid(b_col)
        g_all = -jnp.exp(alog_ref[hh, 0:1, :]) * jax.nn.softplus(a_col + dtb_ref[hh, 0:1, :])
        for c in range(n_chunks):
            rs = slice(c * C, (c + 1) * C)
            qs.append(q_all[rs]); ks_.append(k_all[rs]); vs_.append(v_all[rs])
            betas.append(beta_all[rs]); g_rows.append(g_all[rs])

    ri = lax.broadcasted_iota(jnp.int32, (C, C), 0)
    ci = lax.broadcasted_iota(jnp.int32, (C, C), 1)
    tril, strict = ri >= ci, ri > ci
    ltri = tril.astype(bf16)
    eye = (ri == ci).astype(f32)
    gcs = []
    for g_rows_u in g_rows:
        gh, gm, gl = _split3(g_rows_u)
        gcs.append(jnp.dot(ltri, gh, preferred_element_type=f32) + jnp.dot(ltri, gm, preferred_element_type=f32)
                   + jnp.dot(ltri, gl, preferred_element_type=f32))
    decays = [jnp.exp(jnp.where(tril, gc[:, :C] - gc[:, :C].T, -jnp.inf)) for gc in gcs]
    kbs = [k.astype(bf16) for k in ks_]
    kbetas = [k * b for k, b in zip(ks_, betas)]
    ms = [jnp.where(strict, lax.dot_general(kbeta.astype(bf16), kb, nt, preferred_element_type=f32) * d, 0.0)
          for kbeta, kb, d in zip(kbetas, kbs, decays)]
    attns = [jnp.where(tril, lax.dot_general(q.astype(bf16), kb, nt, preferred_element_type=f32) * d, 0.0).astype(bf16)
             for q, kb, d in zip(qs, kbs, decays)]
    t_invs, pws = [eye - m for m in ms], ms
    for _ in range(5):
        pws = [_dot3(pw, pw) for pw in pws]
        t_invs = [_dot3(t, eye + pw) for t, pw in zip(t_invs, pws)]
    egcs = [jnp.exp(gc) for gc in gcs]
    tbs = [t.astype(bf16) for t in t_invs]
    us = [jnp.dot(tb, (v * b).astype(bf16), preferred_element_type=f32) for tb, v, b in zip(tbs, vs_, betas)]
    ws = [jnp.dot(tb, (kbeta * egc).astype(bf16), preferred_element_type=f32).astype(bf16)
          for tb, kbeta, egc in zip(tbs, kbetas, egcs)]
    states = [st_ref[hh] for hh in range(HS)]
    for c in range(n_chunks):
        for hh in range(HS):
            u, rs = hh * n_chunks + c, slice(c * C, (c + 1) * C)
            g_last = gcs[u][C - 1:C, :]
            q_dec = (qs[u] * egcs[u]).astype(bf16)
            k_dec = (ks_[u] * jnp.exp(g_last - gcs[u])).astype(bf16)
            sb = states[hh].astype(bf16)
            v_new = us[u] - jnp.dot(ws[u], sb, preferred_element_type=f32)
            vnb = v_new.astype(bf16)
            o = jnp.dot(q_dec, sb, preferred_element_type=f32) + jnp.dot(attns[u], vnb, preferred_element_type=f32)
            states[hh] = states[hh] * jnp.exp(g_last) + jnp.dot(k_dec.T, vnb, preferred_element_type=f32)
            o = o * lax.rsqrt(jnp.mean(o * o, axis=-1, keepdims=True) + NORM_EPS) * ng_ref[...]
            zt = z_ref[0, rs, hh * Dh:(hh + 1) * Dh]
            o_ref[0, rs, hh * Dh:(hh + 1) * Dh] = (o * (zt * jax.nn.sigmoid(zt))).astype(o_ref.dtype)
    for hh in range(HS):
        st_ref[hh] = states[hh]


def deltanet_mixer_pallas(proj3, conv_w, a_log, dt_bias, norm_g, B, S, T=256, HS=4):
    H, Dh = DN_HEADS, HEAD_DIM
    W = HS * Dh
    assert S % T == 0 and T % DN_CHUNK == 0 and H % HS == 0
    blk = lambda j: pl.BlockSpec((1, T, W), lambda b, h, t: (b, t, (OFF_DN + j * D_GRP) // W + h))
    prev = lambda j: pl.BlockSpec((1, SUB, W), lambda b, h, t: (b, jnp.maximum(t * (T // SUB) - 1, 0),
                                                               (OFF_DN + j * D_GRP) // W + h))
    per_head = pl.BlockSpec((HS, SUB, Dh), lambda b, h, t: (h, 0, 0))
    bcast = lambda x: jnp.broadcast_to(x.astype(f32)[:, None, None], (H, SUB, Dh))
    return pl.pallas_call(
        functools.partial(_deltanet_body, T=T, HS=HS),
        grid=(B, H // HS, S // T),
        in_specs=[blk(0), blk(1), blk(2), blk(3), prev(0), prev(1), prev(2),
                  pl.BlockSpec((1, T, LANES), lambda b, h, t: (b, t, OFF_SMALL // LANES)),
                  pl.BlockSpec((CONV_WIDTH, 3, W), lambda b, h, t: (0, 0, h)),
                  per_head, per_head,
                  pl.BlockSpec((1, Dh), lambda b, h, t: (0, 0))],
        out_specs=pl.BlockSpec((1, T, W), lambda b, h, t: (b, t, h)),
        out_shape=jax.ShapeDtypeStruct((B, S, D_GRP), bf16),
        scratch_shapes=[pltpu.VMEM((HS, Dh, Dh), f32)],
        compiler_params=pltpu.CompilerParams(dimension_semantics=("parallel", "parallel", "arbitrary"),
                                             vmem_limit_bytes=VMEM_LIMIT),
        name="deltanet",
    )(proj3, proj3, proj3, proj3, proj3, proj3, proj3, proj3, conv_w.reshape(CONV_WIDTH, 3, D_GRP),
      bcast(a_log), bcast(dt_bias), norm_g.reshape(1, Dh))


def _w_in_pieces():
    n0, r0, d0, w0 = 0, NSA_COLS, NSA_COLS + RET_COLS, NSA_COLS + RET_COLS + DN_COLS
    n_gate = 3 * NSA_HEADS
    return [
        (OFF_RW, w0, 3 * D_GRP),
        (OFF_RET, r0, 4 * D_GRP),
        (OFF_DN, d0, 3 * D_GRP),
        (OFF_DN + 3 * D_GRP, d0 + 3 * D_GRP + 2 * DN_HEADS, D_GRP),
        (OFF_NSA_Q, n0, D_GRP + 6 * NSA_KV_W),
        (OFF_LORA, w0 + 3 * D_GRP, RW_LORA),
        (OFF_SMALL, n0 + D_GRP + 6 * NSA_KV_W, n_gate),
        (OFF_SMALL + SMALL_DN_A, d0 + 3 * D_GRP, 2 * DN_HEADS),
    ]


def _pack_w_in_body(w_ref, o_ref):
    cols = o_ref.shape[1]
    for off, width in ((OFF_LORA + RW_LORA, RW_LORA_PAD - RW_LORA), (OFF_SMALL, LANES)):
        o_ref[off:off + width, :] = jnp.zeros((width, cols), o_ref.dtype)
    for dst, src, width in _w_in_pieces():
        o_ref[dst:dst + width, :] = w_ref[0, src:src + width, :].astype(o_ref.dtype)


def pack_w_in_t(w_in, layer, tc=256):
    K = w_in.shape[1]
    w_t = jnp.swapaxes(w_in, 1, 2)
    return pl.pallas_call(
        _pack_w_in_body,
        grid=(K // tc,),
        in_specs=[pl.BlockSpec((1, P_TOTAL, tc), lambda i: (layer, 0, i))],
        out_specs=pl.BlockSpec((P_PAD, tc), lambda i: (0, i)),
        out_shape=jax.ShapeDtypeStruct((P_PAD, K), bf16),
        compiler_params=pltpu.CompilerParams(dimension_semantics=("parallel",), vmem_limit_bytes=VMEM_LIMIT),
        name="pack_w_in",
    )(w_t)


def hybrid_mixer(h, hn, B, S, layer, w_in, w_out, ck_w1, ck_w2, ck_pe, cv_w1, cv_w2, cv_pe, dn_conv, dn_a_log,
                 dn_dt_bias, dn_norm, rw_mu, rw_w0, rw_w2, rw_a0, rw_a2, rw_g2, rw_kk, rw_ka, rw_rk, rw_ln_g, rw_ln_b):
    M = B * S
    proj = matmul_nt(hn, pack_w_in_t(w_in, layer), name="in_proj")
    p3 = proj.reshape(B, S, P_PAD)
    o_nsa = nsa_mixer_pallas(proj, ck_w1, ck_w2, ck_pe, cv_w1, cv_w2, cv_pe, B, S)
    o_ret = retention_mixer_pallas(p3, B, S)
    o_dn = deltanet_mixer_pallas(p3, dn_conv, dn_a_log, dn_dt_bias, dn_norm, B, S)
    o_rw = rwkv7_mixer(proj, OFF_RW // (3 * D_GRP), OFF_LORA // RW_LORA_PAD, rw_mu, rw_w0, rw_w2, rw_a0, rw_a2,
                       rw_g2, rw_kk, rw_ka, rw_rk, rw_ln_g, rw_ln_b, B, S)
    parts = [o.reshape(M, D_GRP) for o in (o_nsa, o_ret, o_dn, o_rw)]
    return out_proj_residual(parts, w_out.astype(bf16), h)


def kernel(x, p, norm_mix, w_in, w_out, nsa_ck_w1, nsa_ck_w2, nsa_ck_pe, nsa_cv_w1, nsa_cv_w2, nsa_cv_pe, dn_conv, dn_a_log, dn_dt_bias, dn_norm, rw_mu, rw_w0, rw_w2, rw_a0, rw_a2, rw_g2, rw_kk, rw_ka, rw_rk, rw_ln_g, rw_ln_b, norm_moe, moe_router_grp, moe_router_grp_b, moe_router_exp, moe_router_exp_b, moe_w_gate, moe_w_up, moe_w_down, norm_ple, ple_w, ple_gate, norm_final):
    B, S, D = x.shape
    M = B * S
    h = x.reshape(M, D)
    for i in range(DEPTH):
        hn = rms_norm_rows(h, norm_mix[i], bf16)
        h = hybrid_mixer(h, hn, B, S, i, w_in, w_out[i], nsa_ck_w1[i], nsa_ck_w2[i], nsa_ck_pe[i], nsa_cv_w1[i],
                         nsa_cv_w2[i], nsa_cv_pe[i], dn_conv[i], dn_a_log[i], dn_dt_bias[i], dn_norm[i],
                         rw_mu[i], rw_w0[i], rw_w2[i], rw_a0[i], rw_a2[i], rw_g2[i], rw_kk[i], rw_ka[i],
                         rw_rk[i], rw_ln_g[i], rw_ln_b[i])
        hn = rms_norm_rows(h, norm_moe[i], f32)
        h = h + hier_moe(hn, moe_router_grp[i], moe_router_grp_b[i], moe_router_exp[i], moe_router_exp_b[i],
                         moe_w_gate, moe_w_up, moe_w_down, i)
        hn = rms_norm_rows(h, norm_ple[i], bf16)
        h = ple_residual(hn, ple_gate[i].astype(bf16), p[i].reshape(M, PLE_DIM), ple_w[i].astype(bf16), h)
    return rms_norm_rows(h, norm_final, f32).reshape(B, S, D)
```

```python
import functools

import jax
import jax.numpy as jnp
import numpy as np
from jax import lax
from jax.experimental import pallas as pl
from jax.experimental.pallas import tpu as pltpu

D_MODEL = 4096
DEPTH = 2
f32 = jnp.float32
bf16 = jnp.bfloat16
D_MIX = D_MODEL
D_GRP = D_MIX // 4
HEAD_DIM = 128
NORM_EPS = 1e-6
L2NORM_EPS = 1e-6
NSA_HEADS = D_GRP // HEAD_DIM
NSA_KV_HEADS = 2
NSA_GROUP = NSA_HEADS // NSA_KV_HEADS
NSA_KV_W = NSA_KV_HEADS * HEAD_DIM
CMP_LEN = 32
CMP_STRIDE = 16
SEL_LEN = 64
N_SEL = 16
N_LOCAL_FORCED = 2
FORCED_SCORE = 1e4
WINDOW = 512
ROPE_THETA = 500000.0
ROPE_DIMS = HEAD_DIM // 4
RET_HEADS = D_GRP // HEAD_DIM
RET_CHUNK = 128
RET_THETA = 10000.0
DN_HEADS = D_GRP // HEAD_DIM
DN_CHUNK = 64
CONV_WIDTH = 4
RW_HEAD_DIM = 64
RW_DECAY_LORA = 64
RW_AAA_LORA = 64
RW_GATE_LORA = 160
RW_LN_EPS = 64e-5
N_GROUPS = 4
EXPERTS_PER_GROUP = 8
N_EXPERTS = N_GROUPS * EXPERTS_PER_GROUP
TOP_K = 2
D_EXPERT = 768
ROW_BLOCK = 256
PLE_DIM = 256
NSA_COLS = D_GRP + 6 * NSA_KV_W + 3 * NSA_HEADS
RET_COLS = 4 * D_GRP
DN_COLS = 4 * D_GRP + 2 * DN_HEADS
RW_COLS = 3 * D_GRP + RW_DECAY_LORA + RW_AAA_LORA + RW_GATE_LORA
P_TOTAL = NSA_COLS + RET_COLS + DN_COLS + RW_COLS

RW_LORA = RW_DECAY_LORA + RW_AAA_LORA + RW_GATE_LORA
RW_LORA_PAD = 384
RW_GROUPS_PER_BODY = 4
LANES = 128
SUB = 8

OFF_RW = 0
OFF_RET = 3 * D_GRP
OFF_DN = OFF_RET + 4 * D_GRP
OFF_NSA_Q = OFF_DN + 4 * D_GRP
OFF_NSA_KV = OFF_NSA_Q + D_GRP
OFF_LORA = OFF_NSA_KV + 6 * NSA_KV_W
OFF_SMALL = OFF_LORA + RW_LORA_PAD
SMALL_DN_A = 32
P_PAD = OFF_SMALL + LANES
assert P_PAD == 14336 and OFF_LORA % RW_LORA_PAD == 0

V7X_VMEM_BYTES = 64 * 1024 * 1024
VMEM_LIMIT = V7X_VMEM_BYTES * 7 // 8


def _split2(x):
    h = x.astype(bf16)
    return h, (x - h.astype(f32)).astype(bf16)


def _split3(x):
    h = x.astype(bf16)
    r1 = x - h.astype(f32)
    m = r1.astype(bf16)
    return h, m, (r1 - m.astype(f32)).astype(bf16)


def _rwkv_prep_body(x_ref, xprev_ref, l_ref, lprev_ref, mu_ref, mul_ref, w0_ref, a0_ref, kkw_ref, ka_ref,
                    w2_ref, a2_ref, g2_ref, e_ref, r_ref, w_ref, k_ref, v_ref, kk_ref, b_ref, g_ref, *, tm, S):
    i = pl.program_id(0)
    first = (i * tm) % S == 0

    def token_mix(x, prev8, mu):
        rows = lax.broadcasted_iota(jnp.int32, x.shape, 0)
        prev_row = jnp.where(first, 0.0, prev8[SUB - 1:SUB, :])
        x_prev = jnp.where(rows == 0, prev_row, pltpu.roll(x, 1, axis=0))
        return x + (x_prev - x) * mu

    x = token_mix(x_ref[...], xprev_ref[...], mu_ref[...])
    lora = token_mix(l_ref[...], lprev_ref[...], mul_ref[...])
    r = x[:, 0:D_GRP]
    k = x[:, D_GRP:2 * D_GRP]
    v = x[:, 2 * D_GRP:3 * D_GRP]
    dw = jnp.dot(jnp.tanh(lora).astype(bf16), w2_ref[...], preferred_element_type=f32)
    da = jnp.dot(lora.astype(bf16), a2_ref[...], preferred_element_type=f32)
    g = jnp.dot(jax.nn.sigmoid(lora).astype(bf16), g2_ref[...], preferred_element_type=f32)
    w = -jax.nn.softplus(-(w0_ref[...] + dw)) - 0.5
    decay = jnp.exp(-jnp.exp(w))
    a = jax.nn.sigmoid(a0_ref[...] + da)
    kq = k * kkw_ref[...]
    sq_h, sq_l = _split2(kq * kq)
    ss = (jnp.dot(sq_h, e_ref[...], preferred_element_type=f32)
          + jnp.dot(sq_l, e_ref[...], preferred_element_type=f32))
    kk = kq * lax.rsqrt(ss + L2NORM_EPS)
    r_ref[...] = r
    w_ref[...] = decay
    k_ref[...] = k * (1.0 + (a - 1.0) * ka_ref[...])
    v_ref[...] = v
    kk_ref[...] = kk
    b_ref[...] = kk * a
    g_ref[...] = g


def rwkv_prep(proj, rkv_blk, lora_blk, mu, w0, w2, a0, a2, g2, k_k, k_a, S, tm=256):
    M = proj.shape[0]
    assert M % tm == 0 and S % tm == 0
    W3 = 3 * D_GRP
    mu_x = mu[:W3].reshape(1, W3)
    mu_l = jnp.pad(mu[W3:], (0, RW_LORA_PAD - RW_LORA)).reshape(1, RW_LORA_PAD)
    w2p = jnp.zeros((RW_LORA_PAD, D_GRP), f32).at[0:RW_DECAY_LORA].set(w2).astype(bf16)
    a2p = jnp.zeros((RW_LORA_PAD, D_GRP), f32).at[RW_DECAY_LORA:RW_DECAY_LORA + RW_AAA_LORA].set(a2).astype(bf16)
    g2p = jnp.zeros((RW_LORA_PAD, D_GRP), f32).at[RW_DECAY_LORA + RW_AAA_LORA:RW_LORA].set(g2).astype(bf16)
    hid = np.arange(D_GRP) // RW_HEAD_DIM
    e = jnp.asarray(hid[:, None] == hid[None, :], bf16)
    row = lambda t: t.reshape(1, D_GRP)
    vec = pl.BlockSpec((1, D_GRP), lambda i: (0, 0))
    wspec = pl.BlockSpec((RW_LORA_PAD, D_GRP), lambda i: (0, 0))
    ospec = pl.BlockSpec((tm, D_GRP), lambda i: (i, 0))
    prev_row_blk = lambda i: jnp.maximum(i * (tm // SUB) - 1, 0)
    return pl.pallas_call(
        functools.partial(_rwkv_prep_body, tm=tm, S=S),
        grid=(M // tm,),
        in_specs=[pl.BlockSpec((tm, W3), lambda i: (i, rkv_blk)),
                  pl.BlockSpec((SUB, W3), lambda i: (prev_row_blk(i), rkv_blk)),
                  pl.BlockSpec((tm, RW_LORA_PAD), lambda i: (i, lora_blk)),
                  pl.BlockSpec((SUB, RW_LORA_PAD), lambda i: (prev_row_blk(i), lora_blk)),
                  pl.BlockSpec((1, W3), lambda i: (0, 0)),
                  pl.BlockSpec((1, RW_LORA_PAD), lambda i: (0, 0)),
                  vec, vec, vec, vec, wspec, wspec, wspec,
                  pl.BlockSpec((D_GRP, D_GRP), lambda i: (0, 0))],
        out_specs=[ospec] * 7,
        out_shape=[jax.ShapeDtypeStruct((M, D_GRP), f32)] * 7,
        compiler_params=pltpu.CompilerParams(dimension_semantics=("parallel",), vmem_limit_bytes=VMEM_LIMIT),
        name="rwkv_prep",
    )(proj, proj, proj, proj, mu_x, mu_l, row(w0), row(a0), row(k_k), row(k_a), w2p, a2p, g2p, e)


def _rwkv_scan_body(r_ref, w_ref, k_ref, v_ref, kk_ref, b_ref, g_ref, lng_ref, lnb_ref, rk_ref, o_ref, s_ref,
                    vc_ref, y_ref, *, G, T):
    tb = pl.program_id(2)

    @pl.when(tb == 0)
    def _():
        s_ref[...] = jnp.zeros_like(s_ref)

    Dh = RW_HEAD_DIM
    lo = lax.broadcasted_iota(jnp.int32, (Dh, LANES), 1) < Dh
    lo8 = lax.broadcasted_iota(jnp.int32, (SUB, LANES), 1) < Dh
    sub_iota = lax.broadcasted_iota(jnp.int32, (4 * SUB, LANES), 0)
    row_iota = lax.broadcasted_iota(jnp.int32, (SUB, LANES), 0)

    def seg_sum(x, m):
        s_lo = jnp.sum(jnp.where(m, x, 0.0), axis=-1, keepdims=True)
        s_hi = jnp.sum(jnp.where(m, 0.0, x), axis=-1, keepdims=True)
        return jnp.where(m, s_lo, s_hi)

    cols = [slice(p * LANES, (p + 1) * LANES) for p in range(G)]
    n_groups = T // SUB

    k_iota = lax.broadcasted_iota(jnp.int32, (8 * SUB, LANES), 0)
    k_head = k_iota // (4 * SUB) == lax.broadcasted_iota(jnp.int32, (8 * SUB, LANES), 1) // Dh
    k_piece = (k_iota // SUB) % 4

    def v_pieces(gi):
        t0 = pl.multiple_of(gi * SUB, SUB)
        vts = []
        for p in range(G):
            vh, vm, vl = _split3(v_ref[0, pl.ds(t0, SUB), cols[p]])
            vp = jnp.concatenate([vh, vm, vl, jnp.zeros_like(vh)], axis=0)
            vts.append(jnp.concatenate([vp[:, :Dh], vp[:, Dh:]], axis=0).T)
        return vts

    def v_columns(vts, i, slot):
        sel = (k_head & (k_piece < 3) & (k_iota % SUB == i)).astype(bf16)
        for p in range(G):
            vc_ref[slot, i, p] = jnp.dot(vts[p], sel, preferred_element_type=f32)

    vts0 = v_pieces(0)
    for i in range(SUB):
        v_columns(vts0, i, 0)

    def run_group(gi, slot, states):
        t0 = pl.multiple_of(gi * SUB, SUB)
        vts_next = v_pieces(jnp.minimum(gi + 1, n_groups - 1))
        r8b = [r_ref[0, pl.ds(t0, SUB), cols[p]].astype(bf16) for p in range(G)]
        yacc = [jnp.zeros((SUB, LANES), f32) for _ in range(G)]
        tiles = [[ref[0, pl.ds(t0, SUB), cols[p]] for ref in (kk_ref, w_ref, b_ref, k_ref)] for p in range(G)]
        for i in range(SUB):
            for p in range(G):
                kk_r, w_r, b_r, k_r = [jnp.broadcast_to(t8[i:i + 1, :], (Dh, LANES)) for t8 in tiles[p]]
                S = states[p]
                sab = -seg_sum(S * kk_r, lo)
                states[p] = S * w_r + sab * b_r + vc_ref[slot, i, p] * k_r
            for p in range(G):
                Sb = states[p].astype(bf16)
                zero = jnp.zeros_like(Sb)
                mt = jnp.concatenate([jnp.where(lo, Sb, zero), jnp.where(lo, zero, Sb)], axis=0)
                y8 = lax.dot_general(r8b[p], mt, (((1,), (1,)), ((), ())), preferred_element_type=f32)
                yacc[p] = jnp.where(row_iota == i, y8, yacc[p])
            v_columns(vts_next, i, 1 - slot)
        for p in range(G):
            y_ref[pl.ds(t0, SUB), cols[p]] = yacc[p]
        return states

    def group_batch(gj, carry):
        states = [s_ref[p] for p in range(G)]
        for u in range(RW_GROUPS_PER_BODY):
            states = run_group(RW_GROUPS_PER_BODY * gj + u, u % 2, states)
        for p in range(G):
            s_ref[p] = states[p]
        return carry

    assert n_groups % RW_GROUPS_PER_BODY == 0 and RW_GROUPS_PER_BODY % 2 == 0
    lax.fori_loop(0, n_groups // RW_GROUPS_PER_BODY, group_batch, 0)

    lo_t = lax.broadcasted_iota(jnp.int32, (T, LANES), 1) < Dh
    for p in range(G):
        y = y_ref[:, cols[p]]
        mean = seg_sum(y, lo_t) * (1.0 / Dh)
        d = y - mean
        var = seg_sum(d * d, lo_t) * (1.0 / Dh)
        yn = d * lax.rsqrt(var + RW_LN_EPS) * lng_ref[:, cols[p]] + lnb_ref[:, cols[p]]
        bonus = seg_sum(r_ref[0, :, cols[p]] * k_ref[0, :, cols[p]] * rk_ref[:, cols[p]], lo_t) * v_ref[0, :, cols[p]]
        o_ref[0, :, cols[p]] = (yn + bonus) * g_ref[0, :, cols[p]]


def rwkv_scan(r, w, k, v, kk, b, g, ln_g, ln_b, r_k, G=8, T=512):
    B, S, D = r.shape
    n_pairs = D // LANES
    assert n_pairs % G == 0 and S % T == 0
    spec = pl.BlockSpec((1, T, G * LANES), lambda bi, pi, ti: (bi, ti, pi))
    vec = pl.BlockSpec((1, G * LANES), lambda bi, pi, ti: (0, pi))
    row = lambda t: t.reshape(1, D)
    return pl.pallas_call(
        functools.partial(_rwkv_scan_body, G=G, T=T),
        grid=(B, n_pairs // G, S // T),
        in_specs=[spec] * 7 + [vec] * 3,
        out_specs=spec,
        out_shape=jax.ShapeDtypeStruct((B, S, D), f32),
        scratch_shapes=[pltpu.VMEM((G, RW_HEAD_DIM, LANES), f32),
                        pltpu.VMEM((2, SUB, G, RW_HEAD_DIM, LANES), f32),
                        pltpu.VMEM((T, G * LANES), f32)],
        compiler_params=pltpu.CompilerParams(dimension_semantics=("parallel", "parallel", "arbitrary"),
                                             vmem_limit_bytes=VMEM_LIMIT),
        name="rwkv_scan",
    )(r, w, k, v, kk, b, g, row(ln_g), row(ln_b), row(r_k))


def rwkv7_mixer(proj, rkv_blk, lora_blk, mu, w0, w2, a0, a2, g2, k_k, k_a, r_k, ln_g, ln_b, B, S):
    outs = rwkv_prep(proj, rkv_blk, lora_blk, mu, w0, w2, a0, a2, g2, k_k, k_a, S)
    r, w, k, v, kk, b, g = [t.reshape(B, S, D_GRP) for t in outs]
    return rwkv_scan(r, w, k, v, kk, b, g, ln_g, ln_b, r_k.reshape(-1))


def _mm_nt_body(a_ref, bt_ref, o_ref, acc_ref):
    k = pl.program_id(2)

    @pl.when(k == 0)
    def _():
        acc_ref[...] = jnp.zeros_like(acc_ref)

    acc_ref[...] += lax.dot_general(a_ref[...], bt_ref[...], (((1,), (1,)), ((), ())), preferred_element_type=f32)

    @pl.when(k == pl.num_programs(2) - 1)
    def _():
        o_ref[...] = acc_ref[...].astype(o_ref.dtype)


def matmul_nt(a, bt, out_dtype=f32, tm=1024, tn=1024, tk=2048, name="matmul_nt"):
    M, K = a.shape
    N, _ = bt.shape
    assert M % tm == 0 and N % tn == 0 and K % tk == 0, (M, N, K, tm, tn, tk)
    return pl.pallas_call(
        _mm_nt_body,
        grid=(M // tm, N // tn, K // tk),
        in_specs=[pl.BlockSpec((tm, tk), lambda i, j, k: (i, k)),
                  pl.BlockSpec((tn, tk), lambda i, j, k: (j, k))],
        out_specs=pl.BlockSpec((tm, tn), lambda i, j, k: (i, j)),
        out_shape=jax.ShapeDtypeStruct((M, N), out_dtype),
        scratch_shapes=[pltpu.VMEM((tm, tn), f32)],
        compiler_params=pltpu.CompilerParams(
            dimension_semantics=("parallel", "parallel", "arbitrary"), vmem_limit_bytes=VMEM_LIMIT),
        name=name,
    )(a, bt)


def _rms_norm_body(x_ref, g_ref, o_ref):
    x = x_ref[...]
    y = x * lax.rsqrt(jnp.mean(x * x, axis=-1, keepdims=True) + NORM_EPS)
    o_ref[...] = (y * g_ref[...]).astype(o_ref.dtype)


def rms_norm_rows(x, g, out_dtype, tm=256):
    M, D = x.shape
    return pl.pallas_call(
        _rms_norm_body,
        grid=(M // tm,),
        in_specs=[pl.BlockSpec((tm, D), lambda i: (i, 0)), pl.BlockSpec((1, D), lambda i: (0, 0))],
        out_specs=pl.BlockSpec((tm, D), lambda i: (i, 0)),
        out_shape=jax.ShapeDtypeStruct((M, D), out_dtype),
        compiler_params=pltpu.CompilerParams(dimension_semantics=("parallel",), vmem_limit_bytes=VMEM_LIMIT),
        name="rms_norm",
    )(x, g.reshape(1, D).astype(f32))


def _out_proj_body(a0_ref, a1_ref, a2_ref, a3_ref, b_ref, h_ref, o_ref, acc_ref):
    k = pl.program_id(2)

    @pl.when(k == 0)
    def _():
        acc_ref[...] = h_ref[...]

    for j, a_ref in enumerate((a0_ref, a1_ref, a2_ref, a3_ref)):
        @pl.when(k == j)
        def _(a_ref=a_ref):
            acc_ref[...] += jnp.dot(a_ref[...].astype(bf16), b_ref[...], preferred_element_type=f32)

    @pl.when(k == pl.num_programs(2) - 1)
    def _():
        o_ref[...] = acc_ref[...]


def out_proj_residual(parts, w, h, tm=1024, tn=1024):
    M, D = h.shape
    aspec = pl.BlockSpec((tm, D_GRP), lambda i, j, k: (i, 0))
    return pl.pallas_call(
        _out_proj_body,
        grid=(M // tm, D // tn, len(parts)),
        in_specs=[aspec] * 4 + [pl.BlockSpec((D_GRP, tn), lambda i, j, k: (k, j)),
                                pl.BlockSpec((tm, tn), lambda i, j, k: (i, j))],
        out_specs=pl.BlockSpec((tm, tn), lambda i, j, k: (i, j)),
        out_shape=jax.ShapeDtypeStruct((M, D), f32),
        scratch_shapes=[pltpu.VMEM((tm, tn), f32)],
        compiler_params=pltpu.CompilerParams(
            dimension_semantics=("parallel", "parallel", "arbitrary"), vmem_limit_bytes=VMEM_LIMIT),
        name="out_proj",
    )(*parts, w, h)


def _ple_body(a_ref, b_ref, p_ref, pw_ref, h_ref, o_ref, acc_ref):
    k = pl.program_id(2)

    @pl.when(k == 0)
    def _():
        acc_ref[...] = jnp.zeros_like(acc_ref)

    acc_ref[...] += jnp.dot(a_ref[...], b_ref[...], preferred_element_type=f32)

    @pl.when(k == pl.num_programs(2) - 1)
    def _():
        emb = jnp.dot(p_ref[...].astype(bf16), pw_ref[...], preferred_element_type=f32)
        o_ref[...] = h_ref[...] + emb * jax.nn.sigmoid(acc_ref[...])


def ple_residual(hn, gate_w, p, ple_w, h, tm=1024, tn=1024, tk=2048):
    M, D = h.shape
    return pl.pallas_call(
        _ple_body,
        grid=(M // tm, D // tn, D // tk),
        in_specs=[pl.BlockSpec((tm, tk), lambda i, j, k: (i, k)),
                  pl.BlockSpec((tk, tn), lambda i, j, k: (k, j)),
                  pl.BlockSpec((tm, PLE_DIM), lambda i, j, k: (i, 0)),
                  pl.BlockSpec((PLE_DIM, tn), lambda i, j, k: (0, j)),
                  pl.BlockSpec((tm, tn), lambda i, j, k: (i, j))],
        out_specs=pl.BlockSpec((tm, tn), lambda i, j, k: (i, j)),
        out_shape=jax.ShapeDtypeStruct((M, D), f32),
        scratch_shapes=[pltpu.VMEM((tm, tn), f32)],
        compiler_params=pltpu.CompilerParams(
            dimension_semantics=("parallel", "parallel", "arbitrary"), vmem_limit_bytes=VMEM_LIMIT),
        name="ple",
    )(hn, gate_w, p, ple_w, h)


def _moe_router_body(x_ref, w_ref, b_ref, id_ref, wt_ref):
    logits = jnp.dot(x_ref[...].astype(bf16), w_ref[...], preferred_element_type=f32) + b_ref[...]
    lane = lax.broadcasted_iota(jnp.int32, logits.shape, 1).astype(f32)
    ninf = -jnp.inf

    def top1(vals):
        m = jnp.max(vals, axis=-1, keepdims=True)
        idx = jnp.min(jnp.where(vals == m, lane, float(LANES)), axis=-1, keepdims=True)
        return m, idx

    gl = jnp.where(lane < N_GROUPS, logits, ninf)
    g_max, g_idx = top1(gl)
    g_w = 1.0 / jnp.sum(jnp.exp(gl - g_max), axis=-1, keepdims=True)
    lo = N_GROUPS + g_idx * EXPERTS_PER_GROUP
    el = jnp.where((lane >= lo) & (lane < lo + EXPERTS_PER_GROUP), logits, ninf)
    e_max, _ = top1(el)
    pe = jnp.exp(el - e_max)
    p = pe / jnp.sum(pe, axis=-1, keepdims=True)
    p = jnp.where(el == ninf, ninf, p)
    p1, i1 = top1(p)
    p2, i2 = top1(jnp.where(lane == i1, ninf, p))
    scale = g_w / (p1 + p2)
    id_ref[...] = jnp.where(lane == 0, i1 - N_GROUPS, jnp.where(lane == 1, i2 - N_GROUPS, 0.0)).astype(jnp.int32)
    wt_ref[...] = jnp.where(lane == 0, p1 * scale, jnp.where(lane == 1, p2 * scale, 0.0))


def moe_router(xb, router_grp, router_grp_b, router_exp, router_exp_b, tm=512):
    M, D = xb.shape
    n_log = N_GROUPS + N_EXPERTS
    w = jnp.pad(jnp.concatenate([router_grp, router_exp], axis=1), ((0, 0), (0, LANES - n_log))).astype(bf16)
    b = jnp.pad(jnp.concatenate([router_grp_b, router_exp_b]), (0, LANES - n_log)).reshape(1, LANES).astype(f32)
    ids, wts = pl.pallas_call(
        _moe_router_body,
        grid=(M // tm,),
        in_specs=[pl.BlockSpec((tm, D), lambda i: (i, 0)), pl.BlockSpec((D, LANES), lambda i: (0, 0)),
                  pl.BlockSpec((1, LANES), lambda i: (0, 0))],
        out_specs=[pl.BlockSpec((tm, LANES), lambda i: (i, 0))] * 2,
        out_shape=[jax.ShapeDtypeStruct((M, LANES), jnp.int32), jax.ShapeDtypeStruct((M, LANES), f32)],
        compiler_params=pltpu.CompilerParams(dimension_semantics=("parallel",), vmem_limit_bytes=VMEM_LIMIT),
        name="moe_router",
    )(xb, w, b)
    return ids[:, :TOP_K], wts[:, :TOP_K]


def _moe_up_body(be_ref, nu_ref, x_ref, wg_ref, wu_ref, h_ref, *, tf):
    i = pl.program_id(0)

    @pl.when(i < nu_ref[0])
    def _():
        x = x_ref[...].astype(bf16)
        for f in range(D_EXPERT // tf):
            c = slice(f * tf, (f + 1) * tf)
            hg = jnp.dot(x, wg_ref[0, 0, :, c].astype(bf16), preferred_element_type=f32)
            hu = jnp.dot(x, wu_ref[0, 0, :, c].astype(bf16), preferred_element_type=f32)
            h_ref[:, c] = (hg * jax.nn.sigmoid(hg) * hu).astype(bf16)

    @pl.when(i >= nu_ref[0])
    def _():
        h_ref[...] = jnp.zeros_like(h_ref)


def _moe_down_body(be_ref, nu_ref, h_ref, wd_ref, wt_ref, y_ref):
    i = pl.program_id(0)

    @pl.when(i < nu_ref[0])
    def _():
        y_ref[...] = jnp.dot(h_ref[...], wd_ref[0, 0].astype(bf16), preferred_element_type=f32) * wt_ref[...]

    @pl.when(i >= nu_ref[0])
    def _():
        y_ref[...] = jnp.zeros_like(y_ref)


def moe_experts(x_pad, w_pad, blk_expert, n_used, wg, wu, wd, layer, tf=256):
    A_pad, D = x_pad.shape
    n_blk = A_pad // ROW_BLOCK
    wspec = pl.BlockSpec((1, 1, D, D_EXPERT), lambda i, be, nu: (layer, be[i], 0, 0), pipeline_mode=pl.Buffered(1))
    h = pl.pallas_call(
        functools.partial(_moe_up_body, tf=tf),
        grid_spec=pltpu.PrefetchScalarGridSpec(
            num_scalar_prefetch=2,
            grid=(n_blk,),
            in_specs=[pl.BlockSpec((ROW_BLOCK, D), lambda i, be, nu: (i, 0)), wspec, wspec],
            out_specs=pl.BlockSpec((ROW_BLOCK, D_EXPERT), lambda i, be, nu: (i, 0)),
        ),
        out_shape=jax.ShapeDtypeStruct((A_pad, D_EXPERT), bf16),
        compiler_params=pltpu.CompilerParams(dimension_semantics=("arbitrary",), vmem_limit_bytes=VMEM_LIMIT),
        name="moe_up",
    )(blk_expert, n_used, x_pad, wg, wu)
    return pl.pallas_call(
        _moe_down_body,
        grid_spec=pltpu.PrefetchScalarGridSpec(
            num_scalar_prefetch=2,
            grid=(n_blk,),
            in_specs=[pl.BlockSpec((ROW_BLOCK, D_EXPERT), lambda i, be, nu: (i, 0)),
                      pl.BlockSpec((1, 1, D_EXPERT, D), lambda i, be, nu: (layer, be[i], 0, 0)),
                      pl.BlockSpec((ROW_BLOCK, 1), lambda i, be, nu: (i, 0))],
            out_specs=pl.BlockSpec((ROW_BLOCK, D), lambda i, be, nu: (i, 0)),
        ),
        out_shape=jax.ShapeDtypeStruct((A_pad, D), f32),
        compiler_params=pltpu.CompilerParams(dimension_semantics=("arbitrary",), vmem_limit_bytes=VMEM_LIMIT),
        name="moe_down",
    )(blk_expert, n_used, h, wd, w_pad.reshape(A_pad, 1))


def hier_moe(xb, router_grp, router_grp_b, router_exp, router_exp_b, w_gate, w_up, w_down, layer):
    M, D = xb.shape
    expert_ids, weights = moe_router(xb, router_grp, router_grp_b, router_exp, router_exp_b)
    A = M * TOP_K
    e_flat = expert_ids.reshape(A)
    onehot = (e_flat[:, None] == jnp.arange(N_EXPERTS)[None, :]).astype(jnp.int32)
    rank = jnp.take_along_axis(jnp.cumsum(onehot, axis=0), e_flat[:, None], axis=1)[:, 0] - 1
    counts = jnp.sum(onehot, axis=0)
    padded = (counts + ROW_BLOCK - 1) // ROW_BLOCK * ROW_BLOCK
    end_pad = jnp.cumsum(padded)
    dest = (end_pad - padded)[e_flat] + rank
    n_blk = -(-A // ROW_BLOCK) + N_EXPERTS
    A_pad = n_blk * ROW_BLOCK
    tok_pad = (jnp.arange(A_pad, dtype=jnp.int32) % M).at[dest].set(jnp.arange(A, dtype=jnp.int32) // TOP_K)
    w_pad = jnp.zeros((A_pad,), f32).at[dest].set(weights.reshape(A))
    blk_expert = jnp.minimum(jnp.searchsorted(end_pad, jnp.arange(n_blk) * ROW_BLOCK, side='right'),
                             N_EXPERTS - 1).astype(jnp.int32)
    n_used = (end_pad[-1:] // ROW_BLOCK).astype(jnp.int32)
    x_pad = xb[tok_pad]
    y_pad = moe_experts(x_pad, w_pad, blk_expert, n_used, w_gate, w_up, w_down, layer)
    rows = dest.reshape(M, TOP_K)
    return y_pad[rows[:, 0]] + y_pad[rows[:, 1]]


def _rope_tables(S, theta, rot_dims, width):
    half = rot_dims // 2
    freqs = theta ** (-np.arange(half, dtype=np.float64) / half)
    ang = np.arange(S, dtype=np.float64)[:, None] * freqs[None, :]
    cos = np.ones((S, width), np.float64)
    sin_lo = np.zeros((S, width), np.float64)
    sin_hi = np.zeros((S, width), np.float64)
    cos[:, :half] = np.cos(ang)
    cos[:, half:rot_dims] = np.cos(ang)
    sin_lo[:, :half] = -np.sin(ang)
    sin_hi[:, half:rot_dims] = np.sin(ang)
    return [jnp.asarray(t, f32) for t in (cos, sin_lo, sin_hi)]


def _rope128(x, cos, sin_lo, sin_hi, half):
    return x * cos + pltpu.roll(x, LANES - half, axis=1) * sin_lo + pltpu.roll(x, half, axis=1) * sin_hi


def _nsa_prep_body(q_ref, kv_ref, cos_ref, sl_ref, sh_ref, qo_ref, kvo_ref, co_ref):
    cos, sl, sh = cos_ref[...], sl_ref[...], sh_ref[...]
    half = ROPE_DIMS // 2
    scale = HEAD_DIM ** -0.5
    for h in range(NSA_HEADS):
        c = slice(h * LANES, (h + 1) * LANES)
        qo_ref[:, c] = (_rope128(q_ref[:, c], cos, sl, sh, half) * scale).astype(bf16)
    for j in range(6):
        for g in range(NSA_KV_HEADS):
            c = slice((j * NSA_KV_HEADS + g) * LANES, (j * NSA_KV_HEADS + g + 1) * LANES)
            x = kv_ref[:, c]
            if j % 2 == 0:
                x = _rope128(x, cos, sl, sh, half)
            if j < 2:
                co_ref[:, c] = x.astype(bf16)
            else:
                kvo_ref[:, slice(c.start - 4 * LANES, c.stop - 4 * LANES)] = x.astype(bf16)


def nsa_prep(proj, S, tm=512):
    M = proj.shape[0]
    assert M % tm == 0 and S % tm == 0
    tabs = _rope_tables(S, ROPE_THETA, ROPE_DIMS, LANES)
    tspec = pl.BlockSpec((tm, LANES), lambda i: (i % (S // tm), 0))
    return pl.pallas_call(
        _nsa_prep_body,
        grid=(M // tm,),
        in_specs=[pl.BlockSpec((tm, D_GRP), lambda i: (i, OFF_NSA_Q // D_GRP)),
                  pl.BlockSpec((tm, 6 * NSA_KV_W), lambda i: (i, OFF_NSA_KV // (6 * NSA_KV_W))),
                  tspec, tspec, tspec],
        out_specs=[pl.BlockSpec((tm, D_GRP), lambda i: (i, 0)),
                   pl.BlockSpec((tm, 4 * NSA_KV_W), lambda i: (i, 0)),
                   pl.BlockSpec((tm, 2 * NSA_KV_W), lambda i: (i, 0))],
        out_shape=[jax.ShapeDtypeStruct((M, D_GRP), bf16), jax.ShapeDtypeStruct((M, 4 * NSA_KV_W), bf16),
                   jax.ShapeDtypeStruct((M, 2 * NSA_KV_W), bf16)],
        compiler_params=pltpu.CompilerParams(dimension_semantics=("parallel",), vmem_limit_bytes=VMEM_LIMIT),
        name="nsa_prep",
    )(proj, proj, *tabs)


def _nsa_compress_body(r_ref, w1_ref, w2_ref, pe_ref, o_ref):
    r = r_ref[0, 0, 0]
    half_w = CMP_STRIDE * HEAD_DIM
    a = jnp.dot(r, w1_ref[0, :half_w, :], preferred_element_type=f32)
    b = jnp.dot(r, w1_ref[0, half_w:, :], preferred_element_type=f32)
    n = r.shape[0]
    pe = jnp.dot(jnp.broadcast_to(pe_ref[0], (SUB, pe_ref.shape[-1])), w1_ref[0], preferred_element_type=f32)[0:1]
    pre = a + pltpu.roll(b, n - 1, axis=0) + pe
    hid = pre * jax.nn.sigmoid(pre)
    o_ref[0, 0, 0] = jnp.dot(hid.astype(bf16), w2_ref[0], preferred_element_type=f32).astype(bf16)


def nsa_compress(c_rot, w1s, w2s, pes, B, S):
    G, Dh = NSA_KV_HEADS, HEAD_DIM
    nc = S // CMP_STRIDE
    r = c_rot.reshape(B, nc, CMP_STRIDE, 2, G, Dh).transpose(3, 0, 4, 1, 2, 5).reshape(2, B, G, nc, CMP_STRIDE * Dh)
    w1 = jnp.stack(w1s).astype(bf16)
    w2 = jnp.stack(w2s).astype(bf16)
    pe = jnp.stack(pes).reshape(2, 1, CMP_LEN * Dh).astype(bf16)
    return pl.pallas_call(
        _nsa_compress_body,
        grid=(2, B, G),
        in_specs=[pl.BlockSpec((1, 1, 1, nc, CMP_STRIDE * Dh), lambda j, b, g: (j, b, g, 0, 0)),
                  pl.BlockSpec((1, CMP_LEN * Dh, Dh), lambda j, b, g: (j, 0, 0)),
                  pl.BlockSpec((1, Dh, Dh), lambda j, b, g: (j, 0, 0)),
                  pl.BlockSpec((1, 1, CMP_LEN * Dh), lambda j, b, g: (j, 0, 0))],
        out_specs=pl.BlockSpec((1, 1, 1, nc, Dh), lambda j, b, g: (j, b, g, 0, 0)),
        out_shape=jax.ShapeDtypeStruct((2, B, G, nc, Dh), bf16),
        compiler_params=pltpu.CompilerParams(dimension_semantics=("parallel", "parallel", "parallel"),
                                             vmem_limit_bytes=VMEM_LIMIT),
        name="nsa_compress",
    )(r, w1, w2, pe)


def _softmax_rows(s, mask):
    s = jnp.where(mask, s, -jnp.inf)
    m = jnp.max(s, axis=-1, keepdims=True)
    m = jnp.where(m == -jnp.inf, 0.0, m)
    e = jnp.exp(s - m)
    return e, jnp.maximum(jnp.sum(e, axis=-1, keepdims=True), jnp.finfo(f32).tiny)


def _nsa_attn_body(q_ref, cmp_ref, ks_ref, vs_ref, kw_ref, vw_ref, gate_ref, ovt_ref, o_ref, *, tq, tk, S):
    g = pl.program_id(1)
    qi = pl.program_id(2)
    R, Dh = NSA_GROUP, HEAD_DIM
    rows = R * tq
    n_blk = S // SEL_LEN
    n_cmp = S // CMP_STRIDE - 1
    q0 = qi * tq
    q4 = jnp.concatenate([q_ref[0, :, r * Dh:(r + 1) * Dh] for r in range(R)], axis=0)
    nt = (((1,), (1,)), ((), ()))

    def qpos(shape):
        return q0 + lax.broadcasted_iota(jnp.int32, shape, 0) % tq

    k_cmp, v_cmp = cmp_ref[0, 0, 0], cmp_ref[1, 0, 0]
    ncp = k_cmp.shape[0]
    s = lax.dot_general(q4, k_cmp, nt, preferred_element_type=f32)
    n_idx = lax.broadcasted_iota(jnp.int32, (rows, ncp), 1)
    mask = (n_idx * CMP_STRIDE + (CMP_LEN - 1) <= qpos((rows, ncp))) & (n_idx < n_cmp)
    e, den = _softmax_rows(s, mask)
    p = e / den
    o_cmp = jnp.dot(p.astype(bf16), v_cmp, preferred_element_type=f32)

    psum = p[0:tq]
    for r in range(1, R):
        psum = psum + p[r * tq:(r + 1) * tq]
    ph, plo = _split2(psum)
    imp = (lax.dot_general(ovt_ref[...], ph, nt, preferred_element_type=f32)
           + lax.dot_general(ovt_ref[...], plo, nt, preferred_element_type=f32))
    blk = lax.broadcasted_iota(jnp.int32, (n_blk, tq), 0)
    cur = (q0 + lax.broadcasted_iota(jnp.int32, (n_blk, tq), 1)) // SEL_LEN
    visible = blk <= cur
    forced = (blk == 0) | (visible & (blk > cur - N_LOCAL_FORCED))
    imp = jnp.where(forced, FORCED_SCORE, jnp.where(visible, imp, -1.0))
    sel_t = jnp.zeros((n_blk, tq), f32)
    blk_f = blk.astype(f32)
    for _ in range(min(N_SEL, n_blk)):
        m = jnp.max(imp, axis=0, keepdims=True)
        first = jnp.min(jnp.where(imp == m, blk_f, float(n_blk)), axis=0, keepdims=True)
        hit = blk_f == first
        sel_t = jnp.where(hit, 1.0, sel_t)
        imp = jnp.where(hit, -jnp.inf, imp)
    sel_q = jnp.concatenate([sel_t.T, jnp.ones((tq, LANES - n_blk), f32)], axis=1).astype(bf16)

    blocks_per_tile = tk // SEL_LEN

    big = float(2 ** 30)
    kb = sel_q.shape[1]

    def sel_step(kt, carry, causal):
        m_run, l_run, acc = carry
        k0 = pl.multiple_of(kt * tk, tk)
        kt_k = ks_ref[0, pl.ds(k0, tk), :]
        kt_v = vs_ref[0, pl.ds(k0, tk), :]
        s = lax.dot_general(q4, kt_k, nt, preferred_element_type=f32)
        row = lax.broadcasted_iota(jnp.int32, (kb, tk), 0)
        in_blk = row == kt * blocks_per_tile + lax.broadcasted_iota(jnp.int32, (kb, tk), 1) // SEL_LEN
        expand = jnp.where(in_blk, big, jnp.where(row == n_blk, -big, 0.0)).astype(bf16)
        bias = jnp.dot(sel_q, expand, preferred_element_type=f32)
        s = (s.reshape(R, tq, tk) + bias[None]).reshape(rows, tk)
        if causal:
            kpos = k0 + lax.broadcasted_iota(jnp.int32, (rows, tk), 1)
            s = jnp.where(kpos <= qpos((rows, tk)), s, -big)
        m_new = jnp.maximum(m_run, jnp.max(s, axis=-1, keepdims=True))
        alpha = jnp.exp(m_run - m_new)
        e = jnp.exp(s - m_new)
        l_new = alpha * l_run + jnp.sum(e, axis=-1, keepdims=True)
        acc = alpha * acc + jnp.dot(e.astype(bf16), kt_v, preferred_element_type=f32)
        return m_new, l_new, acc

    last_kt = (q0 + tq - 1) // tk
    init = (jnp.full((rows, 1), -jnp.inf, f32), jnp.zeros((rows, 1), f32), jnp.zeros((rows, Dh), f32))
    carry = lax.fori_loop(0, last_kt, functools.partial(sel_step, causal=False), init)
    _, l_sel, acc_sel = sel_step(last_kt, carry, causal=True)
    o_sel = acc_sel / l_sel

    span = WINDOW + tq
    w0 = pl.multiple_of(jnp.maximum(q0 - WINDOW, 0), tq)
    kwin = kw_ref[0, pl.ds(w0, span), :]
    vwin = vw_ref[0, pl.ds(w0, span), :]
    s = lax.dot_general(q4, kwin, nt, preferred_element_type=f32)
    kpos = w0 + lax.broadcasted_iota(jnp.int32, (rows, span), 1)
    qp = qpos((rows, span))
    e, den = _softmax_rows(s, (kpos <= qp) & (kpos > qp - WINDOW))
    o_win = jnp.dot(e.astype(bf16), vwin, preferred_element_type=f32) / den

    gate = jax.nn.sigmoid(gate_ref[0])
    for r in range(R):
        c0 = (g * R + r) * 3
        lane = lax.broadcasted_iota(jnp.int32, (tq, LANES), 1)
        gsel = lambda c: jnp.sum(jnp.where(lane == c0 + c, gate, 0.0), axis=-1, keepdims=True)
        rs = slice(r * tq, (r + 1) * tq)
        o_ref[0, :, r * Dh:(r + 1) * Dh] = (gsel(0) * o_cmp[rs] + gsel(1) * o_sel[rs]
                                            + gsel(2) * o_win[rs]).astype(o_ref.dtype)


def nsa_attention(q_rot, kv_rot, cmp_kv, proj3, B, S, tq=256, tk=512):
    G, R, Dh = NSA_KV_HEADS, NSA_GROUP, HEAD_DIM
    n_blk = S // SEL_LEN
    assert S % tk == 0 and tk % tq == 0 and S >= WINDOW + tq and n_blk % SUB == 0 and n_blk < LANES
    nc = S // CMP_STRIDE
    c0 = np.arange(nc - 1)[None, :] * CMP_STRIDE
    s0 = np.arange(n_blk)[:, None] * SEL_LEN
    ovt = np.zeros((n_blk, nc), np.float32)
    ovt[:, :nc - 1] = np.clip(np.minimum(c0 + CMP_LEN, s0 + SEL_LEN) - np.maximum(c0, s0), 0, None) / CMP_LEN
    kvspec = lambda j: pl.BlockSpec((1, S, Dh), lambda b, g, i: (b, 0, j * G + g))
    return pl.pallas_call(
        functools.partial(_nsa_attn_body, tq=tq, tk=tk, S=S),
        grid=(B, G, S // tq),
        in_specs=[pl.BlockSpec((1, tq, R * Dh), lambda b, g, i: (b, i, g)),
                  pl.BlockSpec((2, 1, 1, nc, Dh), lambda b, g, i: (0, b, g, 0, 0)),
                  kvspec(0), kvspec(1), kvspec(2), kvspec(3),
                  pl.BlockSpec((1, tq, LANES), lambda b, g, i: (b, i, OFF_SMALL // LANES)),
                  pl.BlockSpec((n_blk, nc), lambda b, g, i: (0, 0))],
        out_specs=pl.BlockSpec((1, tq, R * Dh), lambda b, g, i: (b, i, g)),
        out_shape=jax.ShapeDtypeStruct((B, S, D_GRP), bf16),
        compiler_params=pltpu.CompilerParams(dimension_semantics=("parallel", "parallel", "arbitrary"),
                                             vmem_limit_bytes=VMEM_LIMIT),
        name="nsa_attention",
    )(q_rot, cmp_kv, kv_rot, kv_rot, kv_rot, kv_rot, proj3, jnp.asarray(ovt, bf16))


def nsa_mixer_pallas(proj, ck_w1, ck_w2, ck_pe, cv_w1, cv_w2, cv_pe, B, S):
    q_rot, kv_rot, c_rot = nsa_prep(proj, S)
    cmp_kv = nsa_compress(c_rot, (ck_w1, cv_w1), (ck_w2, cv_w2), (ck_pe, cv_pe), B, S)
    return nsa_attention(q_rot.reshape(B, S, D_GRP), kv_rot.reshape(B, S, 4 * NSA_KV_W), cmp_kv,
                         proj.reshape(B, S, P_PAD), B, S)


def _retention_body(q_ref, k_ref, v_ref, g_ref, cos_ref, sin_ref, dec_ref, xi_ref, zeta_ref, gc_ref, o_ref, st_ref,
                    *, T):
    @pl.when(pl.program_id(2) == 0)
    def _():
        st_ref[...] = jnp.zeros_like(st_ref)

    C, Dh = RET_CHUNK, HEAD_DIM
    nt = (((1,), (1,)), ((), ()))
    decay_in, xi, zeta, gamma_c = dec_ref[0], xi_ref[0], zeta_ref[0], gc_ref[0, 0:1, :]
    state = st_ref[...]
    for c in range(T // C):
        rs = slice(c * C, (c + 1) * C)
        cos, sin = cos_ref[rs, :], sin_ref[rs, :]
        rope = lambda x: x * cos + pltpu.roll(x, Dh // 2, axis=1) * sin
        q = rope(q_ref[0, rs, :])
        k = rope(k_ref[0, rs, :]) * Dh ** -0.5
        qb, vb = q.astype(bf16), v_ref[0, rs, :].astype(bf16)
        inner = lax.dot_general(qb, k.astype(bf16), nt, preferred_element_type=f32) * decay_in
        o = (jnp.dot(inner.astype(bf16), vb, preferred_element_type=f32)
             + jnp.dot(qb, state.astype(bf16), preferred_element_type=f32) * xi)
        kz = (k * zeta).astype(bf16)
        state = state * gamma_c + jnp.dot(kz.T, vb, preferred_element_type=f32)
        mu = jnp.mean(o, axis=-1, keepdims=True)
        d = o - mu
        o = d * lax.rsqrt(jnp.mean(d * d, axis=-1, keepdims=True) + NORM_EPS)
        gt = g_ref[0, rs, :]
        o_ref[0, rs, :] = (gt * jax.nn.sigmoid(gt) * o).astype(o_ref.dtype)
    st_ref[...] = state


def retention_mixer_pallas(proj3, B, S, T=512):
    H, Dh, C = RET_HEADS, HEAD_DIM, RET_CHUNK
    assert S % T == 0 and T % C == 0
    freqs = RET_THETA ** (-np.linspace(0.0, 1.0, Dh // 2))
    ang = np.arange(S, dtype=np.float64)[:, None] * freqs[None, :]
    cos = jnp.asarray(np.concatenate([np.cos(ang), np.cos(ang)], 1), f32)
    sin = jnp.asarray(np.concatenate([-np.sin(ang), np.sin(ang)], 1), f32)
    log_gamma = np.log1p(-(2.0 ** (-5.0 - np.arange(H, dtype=np.float64))))
    n = np.arange(C, dtype=np.float64)
    diff = n[:, None] - n[None, :]
    decay_in = np.where(diff >= 0, np.exp(log_gamma[:, None, None] * np.maximum(diff, 0.0)), 0.0)
    xi = np.exp(log_gamma[:, None] * (n + 1.0))[:, :, None] * np.ones((1, 1, Dh))
    zeta = np.exp(log_gamma[:, None] * (C - 1.0 - n))[:, :, None] * np.ones((1, 1, Dh))
    gamma_c = np.exp(log_gamma * C)[:, None, None] * np.ones((1, SUB, Dh))
    blk = lambda j: pl.BlockSpec((1, T, Dh), lambda b, h, t: (b, t, (OFF_RET + j * D_GRP) // Dh + h))
    tab = pl.BlockSpec((T, Dh), lambda b, h, t: (t, 0))
    per_head = lambda r: pl.BlockSpec((1, r, Dh), lambda b, h, t: (h, 0, 0))
    return pl.pallas_call(
        functools.partial(_retention_body, T=T),
        grid=(B, H, S // T),
        in_specs=[blk(0), blk(1), blk(2), blk(3), tab, tab, per_head(C), per_head(C), per_head(C), per_head(SUB)],
        out_specs=pl.BlockSpec((1, T, Dh), lambda b, h, t: (b, t, h)),
        out_shape=jax.ShapeDtypeStruct((B, S, D_GRP), bf16),
        scratch_shapes=[pltpu.VMEM((Dh, Dh), f32)],
        compiler_params=pltpu.CompilerParams(dimension_semantics=("parallel", "parallel", "arbitrary"),
                                             vmem_limit_bytes=VMEM_LIMIT),
        name="retention",
    )(proj3, proj3, proj3, proj3, cos, sin, jnp.asarray(decay_in, f32), jnp.asarray(xi, f32),
      jnp.asarray(zeta, f32), jnp.asarray(gamma_c, f32))


def _dot3(a, b):
    ah, al = _split2(a)
    bh, bl = _split2(b)
    return (jnp.dot(ah, bh, preferred_element_type=f32) + jnp.dot(ah, bl, preferred_element_type=f32)
            + jnp.dot(al, bh, preferred_element_type=f32))


def _deltanet_body(q_ref, k_ref, v_ref, z_ref, qp_ref, kp_ref, vp_ref, ab_ref, cw_ref, alog_ref, dtb_ref, ng_ref,
                   o_ref, st_ref, *, T, HS):
    hb = pl.program_id(1)
    first = pl.program_id(2) == 0

    @pl.when(first)
    def _():
        st_ref[...] = jnp.zeros_like(st_ref)

    C, Dh = DN_CHUNK, HEAD_DIM
    nt = (((1,), (1,)), ((), ()))
    n_chunks = T // C
    ab = ab_ref[0]
    lane = lax.broadcasted_iota(jnp.int32, ab.shape, 1)

    def conv_silu(x_ref, prev_ref, j, hc):
        prev = jnp.where(first, 0.0, prev_ref[0, :, hc])
        xcat = jnp.concatenate([prev, x_ref[0, :, hc]], axis=0)
        w = cw_ref[:, j, hc]
        y = sum(w[i:i + 1, :] * xcat[SUB - (CONV_WIDTH - 1) + i:SUB - (CONV_WIDTH - 1) + i + T, :]
                for i in range(CONV_WIDTH))
        return y * jax.nn.sigmoid(y)

    def l2n(x):
        return x * lax.rsqrt(jnp.sum(x * x, axis=-1, keepdims=True) + L2NORM_EPS)

    units = [(hh, slice(c * C, (c + 1) * C)) for hh in range(HS) for c in range(n_chunks)]
    qs, ks_, vs_, betas, g_rows = [], [], [], [], []
    for hh in range(HS):
        hc = slice(hh * Dh, (hh + 1) * Dh)
        head = hb * HS + hh
        q_all = l2n(conv_silu(q_ref, qp_ref, 0, hc)) * Dh ** -0.5
        k_all = l2n(conv_silu(k_ref, kp_ref, 1, hc))
        v_all = conv_silu(v_ref, vp_ref, 2, hc)
        a_col = jnp.sum(jnp.where(lane == SMALL_DN_A + head, ab, 0.0), axis=-1, keepdims=True)
        b_col = jnp.sum(jnp.where(lane == SMALL_DN_A + DN_HEADS + head, ab, 0.0), axis=-1, keepdims=True)
        beta_all = jax.nn.sigmoid(b_col)
        g_all = -jnp.exp(alog_ref[hh, 0:1, :]) * jax.nn.softplus(a_col + dtb_ref[hh, 0:1, :])
        for c in range(n_chunks):
            rs = slice(c * C, (c + 1) * C)
            qs.append(q_all[rs]); ks_.append(k_all[rs]); vs_.append(v_all[rs])
            betas.append(beta_all[rs]); g_rows.append(g_all[rs])

    ri = lax.broadcasted_iota(jnp.int32, (C, C), 0)
    ci = lax.broadcasted_iota(jnp.int32, (C, C), 1)
    tril, strict = ri >= ci, ri > ci
    ltri = tril.astype(bf16)
    eye = (ri == ci).astype(f32)
    gcs = []
    for g_rows_u in g_rows:
        gh, gm, gl = _split3(g_rows_u)
        gcs.append(jnp.dot(ltri, gh, preferred_element_type=f32) + jnp.dot(ltri, gm, preferred_element_type=f32)
                   + jnp.dot(ltri, gl, preferred_element_type=f32))
    decays = [jnp.exp(jnp.where(tril, gc[:, :C] - gc[:, :C].T, -jnp.inf)) for gc in gcs]
    kbs = [k.astype(bf16) for k in ks_]
    kbetas = [k * b for k, b in zip(ks_, betas)]
    ms = [jnp.where(strict, lax.dot_general(kbeta.astype(bf16), kb, nt, preferred_element_type=f32) * d, 0.0)
          for kbeta, kb, d in zip(kbetas, kbs, decays)]
    attns = [jnp.where(tril, lax.dot_general(q.astype(bf16), kb, nt, preferred_element_type=f32) * d, 0.0).astype(bf16)
             for q, kb, d in zip(qs, kbs, decays)]
    t_invs, pws = [eye - m for m in ms], ms
    for _ in range(5):
        pws = [_dot3(pw, pw) for pw in pws]
        t_invs = [_dot3(t, eye + pw) for t, pw in zip(t_invs, pws)]
    egcs = [jnp.exp(gc) for gc in gcs]
    tbs = [t.astype(bf16) for t in t_invs]
    us = [jnp.dot(tb, (v * b).astype(bf16), preferred_element_type=f32) for tb, v, b in zip(tbs, vs_, betas)]
    ws = [jnp.dot(tb, (kbeta * egc).astype(bf16), preferred_element_type=f32).astype(bf16)
          for tb, kbeta, egc in zip(tbs, kbetas, egcs)]
    states = [st_ref[hh] for hh in range(HS)]
    for c in range(n_chunks):
        for hh in range(HS):
            u, rs = hh * n_chunks + c, slice(c * C, (c + 1) * C)
            g_last = gcs[u][C - 1:C, :]
            q_dec = (qs[u] * egcs[u]).astype(bf16)
            k_dec = (ks_[u] * jnp.exp(g_last - gcs[u])).astype(bf16)
            sb = states[hh].astype(bf16)
            v_new = us[u] - jnp.dot(ws[u], sb, preferred_element_type=f32)
            vnb = v_new.astype(bf16)
            o = jnp.dot(q_dec, sb, preferred_element_type=f32) + jnp.dot(attns[u], vnb, preferred_element_type=f32)
            states[hh] = states[hh] * jnp.exp(g_last) + jnp.dot(k_dec.T, vnb, preferred_element_type=f32)
            o = o * lax.rsqrt(jnp.mean(o * o, axis=-1, keepdims=True) + NORM_EPS) * ng_ref[...]
            zt = z_ref[0, rs, hh * Dh:(hh + 1) * Dh]
            o_ref[0, rs, hh * Dh:(hh + 1) * Dh] = (o * (zt * jax.nn.sigmoid(zt))).astype(o_ref.dtype)
    for hh in range(HS):
        st_ref[hh] = states[hh]


def deltanet_mixer_pallas(proj3, conv_w, a_log, dt_bias, norm_g, B, S, T=256, HS=4):
    H, Dh = DN_HEADS, HEAD_DIM
    W = HS * Dh
    assert S % T == 0 and T % DN_CHUNK == 0 and H % HS == 0
    blk = lambda j: pl.BlockSpec((1, T, W), lambda b, h, t: (b, t, (OFF_DN + j * D_GRP) // W + h))
    prev = lambda j: pl.BlockSpec((1, SUB, W), lambda b, h, t: (b, jnp.maximum(t * (T // SUB) - 1, 0),
                                                               (OFF_DN + j * D_GRP) // W + h))
    per_head = pl.BlockSpec((HS, SUB, Dh), lambda b, h, t: (h, 0, 0))
    bcast = lambda x: jnp.broadcast_to(x.astype(f32)[:, None, None], (H, SUB, Dh))
    return pl.pallas_call(
        functools.partial(_deltanet_body, T=T, HS=HS),
        grid=(B, H // HS, S // T),
        in_specs=[blk(0), blk(1), blk(2), blk(3), prev(0), prev(1), prev(2),
                  pl.BlockSpec((1, T, LANES), lambda b, h, t: (b, t, OFF_SMALL // LANES)),
                  pl.BlockSpec((CONV_WIDTH, 3, W), lambda b, h, t: (0, 0, h)),
                  per_head, per_head,
                  pl.BlockSpec((1, Dh), lambda b, h, t: (0, 0))],
        out_specs=pl.BlockSpec((1, T, W), lambda b, h, t: (b, t, h)),
        out_shape=jax.ShapeDtypeStruct((B, S, D_GRP), bf16),
        scratch_shapes=[pltpu.VMEM((HS, Dh, Dh), f32)],
        compiler_params=pltpu.CompilerParams(dimension_semantics=("parallel", "parallel", "arbitrary"),
                                             vmem_limit_bytes=VMEM_LIMIT),
        name="deltanet",
    )(proj3, proj3, proj3, proj3, proj3, proj3, proj3, proj3, conv_w.reshape(CONV_WIDTH, 3, D_GRP),
      bcast(a_log), bcast(dt_bias), norm_g.reshape(1, Dh))


def _w_in_pieces():
    n0, r0, d0, w0 = 0, NSA_COLS, NSA_COLS + RET_COLS, NSA_COLS + RET_COLS + DN_COLS
    n_gate = 3 * NSA_HEADS
    return [
        (OFF_RW, w0, 3 * D_GRP),
        (OFF_RET, r0, 4 * D_GRP),
        (OFF_DN, d0, 3 * D_GRP),
        (OFF_DN + 3 * D_GRP, d0 + 3 * D_GRP + 2 * DN_HEADS, D_GRP),
        (OFF_NSA_Q, n0, D_GRP + 6 * NSA_KV_W),
        (OFF_LORA, w0 + 3 * D_GRP, RW_LORA),
        (OFF_SMALL, n0 + D_GRP + 6 * NSA_KV_W, n_gate),
        (OFF_SMALL + SMALL_DN_A, d0 + 3 * D_GRP, 2 * DN_HEADS),
    ]


def _pack_w_in_body(w_ref, o_ref):
    cols = o_ref.shape[1]
    for off, width in ((OFF_LORA + RW_LORA, RW_LORA_PAD - RW_LORA), (OFF_SMALL, LANES)):
        o_ref[off:off + width, :] = jnp.zeros((width, cols), o_ref.dtype)
    for dst, src, width in _w_in_pieces():
        o_ref[dst:dst + width, :] = w_ref[0, src:src + width, :].astype(o_ref.dtype)


def pack_w_in_t(w_in, layer, tc=256):
    K = w_in.shape[1]
    w_t = jnp.swapaxes(w_in, 1, 2)
    return pl.pallas_call(
        _pack_w_in_body,
        grid=(K // tc,),
        in_specs=[pl.BlockSpec((1, P_TOTAL, tc), lambda i: (layer, 0, i))],
        out_specs=pl.BlockSpec((P_PAD, tc), lambda i: (0, i)),
        out_shape=jax.ShapeDtypeStruct((P_PAD, K), bf16),
        compiler_params=pltpu.CompilerParams(dimension_semantics=("parallel",), vmem_limit_bytes=VMEM_LIMIT),
        name="pack_w_in",
    )(w_t)


def hybrid_mixer(h, hn, B, S, layer, w_in, w_out, ck_w1, ck_w2, ck_pe, cv_w1, cv_w2, cv_pe, dn_conv, dn_a_log,
                 dn_dt_bias, dn_norm, rw_mu, rw_w0, rw_w2, rw_a0, rw_a2, rw_g2, rw_kk, rw_ka, rw_rk, rw_ln_g, rw_ln_b):
    M = B * S
    proj = matmul_nt(hn, pack_w_in_t(w_in, layer), name="in_proj")
    p3 = proj.reshape(B, S, P_PAD)
    o_nsa = nsa_mixer_pallas(proj, ck_w1, ck_w2, ck_pe, cv_w1, cv_w2, cv_pe, B, S)
    o_ret = retention_mixer_pallas(p3, B, S)
    o_dn = deltanet_mixer_pallas(p3, dn_conv, dn_a_log, dn_dt_bias, dn_norm, B, S)
    o_rw = rwkv7_mixer(proj, OFF_RW // (3 * D_GRP), OFF_LORA // RW_LORA_PAD, rw_mu, rw_w0, rw_w2, rw_a0, rw_a2,
                       rw_g2, rw_kk, rw_ka, rw_rk, rw_ln_g, rw_ln_b, B, S)
    parts = [o.reshape(M, D_GRP) for o in (o_nsa, o_ret, o_dn, o_rw)]
    return out_proj_residual(parts, w_out.astype(bf16), h)


def kernel(x, p, norm_mix, w_in, w_out, nsa_ck_w1, nsa_ck_w2, nsa_ck_pe, nsa_cv_w1, nsa_cv_w2, nsa_cv_pe, dn_conv, dn_a_log, dn_dt_bias, dn_norm, rw_mu, rw_w0, rw_w2, rw_a0, rw_a2, rw_g2, rw_kk, rw_ka, rw_rk, rw_ln_g, rw_ln_b, norm_moe, moe_router_grp, moe_router_grp_b, moe_router_exp, moe_router_exp_b, moe_w_gate, moe_w_up, moe_w_down, norm_ple, ple_w, ple_gate, norm_final):
    B, S, D = x.shape
    M = B * S
    h = x.reshape(M, D)
    for i in range(DEPTH):
        hn = rms_norm_rows(h, norm_mix[i], bf16)
        h = hybrid_mixer(h, hn, B, S, i, w_in, w_out[i], nsa_ck_w1[i], nsa_ck_w2[i], nsa_ck_pe[i], nsa_cv_w1[i],
                         nsa_cv_w2[i], nsa_cv_pe[i], dn_conv[i], dn_a_log[i], dn_dt_bias[i], dn_norm[i],
                         rw_mu[i], rw_w0[i], rw_w2[i], rw_a0[i], rw_a2[i], rw_g2[i], rw_kk[i], rw_ka[i],
                         rw_rk[i], rw_ln_g[i], rw_ln_b[i])
        hn = rms_norm_rows(h, norm_moe[i], bf16)
        h = h + hier_moe(hn, moe_router_grp[i], moe_router_grp_b[i], moe_router_exp[i], moe_router_exp_b[i],
                         moe_w_gate, moe_w_up, moe_w_down, i)
        hn = rms_norm_rows(h, norm_ple[i], bf16)
        h = ple_residual(hn, ple_gate[i].astype(bf16), p[i].reshape(M, PLE_DIM), ple_w[i].astype(bf16), h)
    return rms_norm_rows(h, norm_final, f32).reshape(B, S, D)
```

```python
import functools

import jax
import jax.numpy as jnp
import numpy as np
from jax import lax
from jax.experimental import pallas as pl
from jax.experimental.pallas import tpu as pltpu

D_MODEL = 4096
DEPTH = 2
f32 = jnp.float32
bf16 = jnp.bfloat16
D_MIX = D_MODEL
D_GRP = D_MIX // 4
HEAD_DIM = 128
NORM_EPS = 1e-6
L2NORM_EPS = 1e-6
NSA_HEADS = D_GRP // HEAD_DIM
NSA_KV_HEADS = 2
NSA_GROUP = NSA_HEADS // NSA_KV_HEADS
NSA_KV_W = NSA_KV_HEADS * HEAD_DIM
CMP_LEN = 32
CMP_STRIDE = 16
SEL_LEN = 64
N_SEL = 16
N_LOCAL_FORCED = 2
FORCED_SCORE = 1e4
WINDOW = 512
ROPE_THETA = 500000.0
ROPE_DIMS = HEAD_DIM // 4
RET_HEADS = D_GRP // HEAD_DIM
RET_CHUNK = 128
RET_THETA = 10000.0
DN_HEADS = D_GRP // HEAD_DIM
DN_CHUNK = 64
CONV_WIDTH = 4
RW_HEAD_DIM = 64
RW_DECAY_LORA = 64
RW_AAA_LORA = 64
RW_GATE_LORA = 160
RW_LN_EPS = 64e-5
N_GROUPS = 4
EXPERTS_PER_GROUP = 8
N_EXPERTS = N_GROUPS * EXPERTS_PER_GROUP
TOP_K = 2
D_EXPERT = 768
ROW_BLOCK = 256
PLE_DIM = 256
NSA_COLS = D_GRP + 6 * NSA_KV_W + 3 * NSA_HEADS
RET_COLS = 4 * D_GRP
DN_COLS = 4 * D_GRP + 2 * DN_HEADS
RW_COLS = 3 * D_GRP + RW_DECAY_LORA + RW_AAA_LORA + RW_GATE_LORA
P_TOTAL = NSA_COLS + RET_COLS + DN_COLS + RW_COLS

RW_LORA = RW_DECAY_LORA + RW_AAA_LORA + RW_GATE_LORA
RW_LORA_PAD = 384
RW_GROUPS_PER_BODY = 4
LANES = 128
SUB = 8

OFF_RW = 0
OFF_RET = 3 * D_GRP
OFF_DN = OFF_RET + 4 * D_GRP
OFF_NSA_Q = OFF_DN + 4 * D_GRP
OFF_NSA_KV = OFF_NSA_Q + D_GRP
OFF_LORA = OFF_NSA_KV + 6 * NSA_KV_W
OFF_SMALL = OFF_LORA + RW_LORA_PAD
SMALL_DN_A = 32
P_PAD = OFF_SMALL + LANES
assert P_PAD == 14336 and OFF_LORA % RW_LORA_PAD == 0

V7X_VMEM_BYTES = 64 * 1024 * 1024
VMEM_LIMIT = V7X_VMEM_BYTES * 7 // 8


def _split2(x):
    h = x.astype(bf16)
    return h, (x - h.astype(f32)).astype(bf16)


def _split3(x):
    h = x.astype(bf16)
    r1 = x - h.astype(f32)
    m = r1.astype(bf16)
    return h, m, (r1 - m.astype(f32)).astype(bf16)


def _rwkv_prep_body(x_ref, xprev_ref, l_ref, lprev_ref, mu_ref, mul_ref, w0_ref, a0_ref, kkw_ref, ka_ref,
                    w2_ref, a2_ref, g2_ref, e_ref, r_ref, w_ref, k_ref, v_ref, kk_ref, b_ref, g_ref, *, tm, S):
    i = pl.program_id(0)
    first = (i * tm) % S == 0

    def token_mix(x, prev8, mu):
        rows = lax.broadcasted_iota(jnp.int32, x.shape, 0)
        prev_row = jnp.where(first, 0.0, prev8[SUB - 1:SUB, :])
        x_prev = jnp.where(rows == 0, prev_row, pltpu.roll(x, 1, axis=0))
        return x + (x_prev - x) * mu

    x = token_mix(x_ref[...], xprev_ref[...], mu_ref[...])
    lora = token_mix(l_ref[...], lprev_ref[...], mul_ref[...])
    r = x[:, 0:D_GRP]
    k = x[:, D_GRP:2 * D_GRP]
    v = x[:, 2 * D_GRP:3 * D_GRP]
    dw = jnp.dot(jnp.tanh(lora).astype(bf16), w2_ref[...], preferred_element_type=f32)
    da = jnp.dot(lora.astype(bf16), a2_ref[...], preferred_element_type=f32)
    g = jnp.dot(jax.nn.sigmoid(lora).astype(bf16), g2_ref[...], preferred_element_type=f32)
    w = -jax.nn.softplus(-(w0_ref[...] + dw)) - 0.5
    decay = jnp.exp(-jnp.exp(w))
    a = jax.nn.sigmoid(a0_ref[...] + da)
    kq = k * kkw_ref[...]
    sq_h, sq_l = _split2(kq * kq)
    ss = (jnp.dot(sq_h, e_ref[...], preferred_element_type=f32)
          + jnp.dot(sq_l, e_ref[...], preferred_element_type=f32))
    kk = kq * lax.rsqrt(ss + L2NORM_EPS)
    r_ref[...] = r
    w_ref[...] = decay
    k_ref[...] = k * (1.0 + (a - 1.0) * ka_ref[...])
    v_ref[...] = v
    kk_ref[...] = kk
    b_ref[...] = kk * a
    g_ref[...] = g


def rwkv_prep(proj, rkv_blk, lora_blk, mu, w0, w2, a0, a2, g2, k_k, k_a, S, tm=256):
    M = proj.shape[0]
    assert M % tm == 0 and S % tm == 0
    W3 = 3 * D_GRP
    mu_x = mu[:W3].reshape(1, W3)
    mu_l = jnp.pad(mu[W3:], (0, RW_LORA_PAD - RW_LORA)).reshape(1, RW_LORA_PAD)
    w2p = jnp.zeros((RW_LORA_PAD, D_GRP), f32).at[0:RW_DECAY_LORA].set(w2).astype(bf16)
    a2p = jnp.zeros((RW_LORA_PAD, D_GRP), f32).at[RW_DECAY_LORA:RW_DECAY_LORA + RW_AAA_LORA].set(a2).astype(bf16)
    g2p = jnp.zeros((RW_LORA_PAD, D_GRP), f32).at[RW_DECAY_LORA + RW_AAA_LORA:RW_LORA].set(g2).astype(bf16)
    hid = np.arange(D_GRP) // RW_HEAD_DIM
    e = jnp.asarray(hid[:, None] == hid[None, :], bf16)
    row = lambda t: t.reshape(1, D_GRP)
    vec = pl.BlockSpec((1, D_GRP), lambda i: (0, 0))
    wspec = pl.BlockSpec((RW_LORA_PAD, D_GRP), lambda i: (0, 0))
    ospec = pl.BlockSpec((tm, D_GRP), lambda i: (i, 0))
    prev_row_blk = lambda i: jnp.maximum(i * (tm // SUB) - 1, 0)
    return pl.pallas_call(
        functools.partial(_rwkv_prep_body, tm=tm, S=S),
        grid=(M // tm,),
        in_specs=[pl.BlockSpec((tm, W3), lambda i: (i, rkv_blk)),
                  pl.BlockSpec((SUB, W3), lambda i: (prev_row_blk(i), rkv_blk)),
                  pl.BlockSpec((tm, RW_LORA_PAD), lambda i: (i, lora_blk)),
                  pl.BlockSpec((SUB, RW_LORA_PAD), lambda i: (prev_row_blk(i), lora_blk)),
                  pl.BlockSpec((1, W3), lambda i: (0, 0)),
                  pl.BlockSpec((1, RW_LORA_PAD), lambda i: (0, 0)),
                  vec, vec, vec, vec, wspec, wspec, wspec,
                  pl.BlockSpec((D_GRP, D_GRP), lambda i: (0, 0))],
        out_specs=[ospec] * 7,
        out_shape=[jax.ShapeDtypeStruct((M, D_GRP), f32)] * 7,
        compiler_params=pltpu.CompilerParams(dimension_semantics=("parallel",), vmem_limit_bytes=VMEM_LIMIT),
        name="rwkv_prep",
    )(proj, proj, proj, proj, mu_x, mu_l, row(w0), row(a0), row(k_k), row(k_a), w2p, a2p, g2p, e)


def _rwkv_scan_body(r_ref, w_ref, k_ref, v_ref, kk_ref, b_ref, g_ref, lng_ref, lnb_ref, rk_ref, o_ref, s_ref,
                    vc_ref, y_ref, *, G, T):
    tb = pl.program_id(2)

    @pl.when(tb == 0)
    def _():
        s_ref[...] = jnp.zeros_like(s_ref)

    Dh = RW_HEAD_DIM
    lo = lax.broadcasted_iota(jnp.int32, (Dh, LANES), 1) < Dh
    lo8 = lax.broadcasted_iota(jnp.int32, (SUB, LANES), 1) < Dh
    sub_iota = lax.broadcasted_iota(jnp.int32, (4 * SUB, LANES), 0)
    row_iota = lax.broadcasted_iota(jnp.int32, (SUB, LANES), 0)

    def seg_sum(x, m):
        s_lo = jnp.sum(jnp.where(m, x, 0.0), axis=-1, keepdims=True)
        s_hi = jnp.sum(jnp.where(m, 0.0, x), axis=-1, keepdims=True)
        return jnp.where(m, s_lo, s_hi)

    cols = [slice(p * LANES, (p + 1) * LANES) for p in range(G)]
    n_groups = T // SUB

    k_iota = lax.broadcasted_iota(jnp.int32, (8 * SUB, LANES), 0)
    k_head = k_iota // (4 * SUB) == lax.broadcasted_iota(jnp.int32, (8 * SUB, LANES), 1) // Dh
    k_piece = (k_iota // SUB) % 4

    def v_pieces(gi):
        t0 = pl.multiple_of(gi * SUB, SUB)
        vts = []
        for p in range(G):
            vh, vm, vl = _split3(v_ref[0, pl.ds(t0, SUB), cols[p]])
            vp = jnp.concatenate([vh, vm, vl, jnp.zeros_like(vh)], axis=0)
            vts.append(jnp.concatenate([vp[:, :Dh], vp[:, Dh:]], axis=0).T)
        return vts

    def v_columns(vts, i, slot):
        sel = (k_head & (k_piece < 3) & (k_iota % SUB == i)).astype(bf16)
        for p in range(G):
            vc_ref[slot, i, p] = jnp.dot(vts[p], sel, preferred_element_type=f32)

    vts0 = v_pieces(0)
    for i in range(SUB):
        v_columns(vts0, i, 0)

    def run_group(gi, slot, states):
        t0 = pl.multiple_of(gi * SUB, SUB)
        vts_next = v_pieces(jnp.minimum(gi + 1, n_groups - 1))
        r8b = [r_ref[0, pl.ds(t0, SUB), cols[p]].astype(bf16) for p in range(G)]
        yacc = [jnp.zeros((SUB, LANES), f32) for _ in range(G)]
        tiles = [[ref[0, pl.ds(t0, SUB), cols[p]] for ref in (kk_ref, w_ref, b_ref, k_ref)] for p in range(G)]
        for i in range(SUB):
            for p in range(G):
                kk_r, w_r, b_r, k_r = [jnp.broadcast_to(t8[i:i + 1, :], (Dh, LANES)) for t8 in tiles[p]]
                S = states[p]
                sab = -seg_sum(S * kk_r, lo)
                states[p] = S * w_r + sab * b_r + vc_ref[slot, i, p] * k_r
            for p in range(G):
                Sb = states[p].astype(bf16)
                zero = jnp.zeros_like(Sb)
                mt = jnp.concatenate([jnp.where(lo, Sb, zero), jnp.where(lo, zero, Sb)], axis=0)
                y8 = lax.dot_general(r8b[p], mt, (((1,), (1,)), ((), ())), preferred_element_type=f32)
                yacc[p] = jnp.where(row_iota == i, y8, yacc[p])
            v_columns(vts_next, i, 1 - slot)
        for p in range(G):
            y_ref[pl.ds(t0, SUB), cols[p]] = yacc[p]
        return states

    def group_batch(gj, carry):
        states = [s_ref[p] for p in range(G)]
        for u in range(RW_GROUPS_PER_BODY):
            states = run_group(RW_GROUPS_PER_BODY * gj + u, u % 2, states)
        for p in range(G):
            s_ref[p] = states[p]
        return carry

    assert n_groups % RW_GROUPS_PER_BODY == 0 and RW_GROUPS_PER_BODY % 2 == 0
    lax.fori_loop(0, n_groups // RW_GROUPS_PER_BODY, group_batch, 0)

    lo_t = lax.broadcasted_iota(jnp.int32, (T, LANES), 1) < Dh
    for p in range(G):
        y = y_ref[:, cols[p]]
        mean = seg_sum(y, lo_t) * (1.0 / Dh)
        d = y - mean
        var = seg_sum(d * d, lo_t) * (1.0 / Dh)
        yn = d * lax.rsqrt(var + RW_LN_EPS) * lng_ref[:, cols[p]] + lnb_ref[:, cols[p]]
        bonus = seg_sum(r_ref[0, :, cols[p]] * k_ref[0, :, cols[p]] * rk_ref[:, cols[p]], lo_t) * v_ref[0, :, cols[p]]
        o_ref[0, :, cols[p]] = (yn + bonus) * g_ref[0, :, cols[p]]


def rwkv_scan(r, w, k, v, kk, b, g, ln_g, ln_b, r_k, G=8, T=512):
    B, S, D = r.shape
    n_pairs = D // LANES
    assert n_pairs % G == 0 and S % T == 0
    spec = pl.BlockSpec((1, T, G * LANES), lambda bi, pi, ti: (bi, ti, pi))
    vec = pl.BlockSpec((1, G * LANES), lambda bi, pi, ti: (0, pi))
    row = lambda t: t.reshape(1, D)
    return pl.pallas_call(
        functools.partial(_rwkv_scan_body, G=G, T=T),
        grid=(B, n_pairs // G, S // T),
        in_specs=[spec] * 7 + [vec] * 3,
        out_specs=spec,
        out_shape=jax.ShapeDtypeStruct((B, S, D), f32),
        scratch_shapes=[pltpu.VMEM((G, RW_HEAD_DIM, LANES), f32),
                        pltpu.VMEM((2, SUB, G, RW_HEAD_DIM, LANES), f32),
                        pltpu.VMEM((T, G * LANES), f32)],
        compiler_params=pltpu.CompilerParams(dimension_semantics=("parallel", "parallel", "arbitrary"),
                                             vmem_limit_bytes=VMEM_LIMIT),
        name="rwkv_scan",
    )(r, w, k, v, kk, b, g, row(ln_g), row(ln_b), row(r_k))


def rwkv7_mixer(proj, rkv_blk, lora_blk, mu, w0, w2, a0, a2, g2, k_k, k_a, r_k, ln_g, ln_b, B, S):
    outs = rwkv_prep(proj, rkv_blk, lora_blk, mu, w0, w2, a0, a2, g2, k_k, k_a, S)
    r, w, k, v, kk, b, g = [t.reshape(B, S, D_GRP) for t in outs]
    return rwkv_scan(r, w, k, v, kk, b, g, ln_g, ln_b, r_k.reshape(-1))


def _mm_nt_body(a_ref, bt_ref, o_ref, acc_ref):
    k = pl.program_id(2)

    @pl.when(k == 0)
    def _():
        acc_ref[...] = jnp.zeros_like(acc_ref)

    acc_ref[...] += lax.dot_general(a_ref[...], bt_ref[...], (((1,), (1,)), ((), ())), preferred_element_type=f32)

    @pl.when(k == pl.num_programs(2) - 1)
    def _():
        o_ref[...] = acc_ref[...].astype(o_ref.dtype)


def matmul_nt(a, bt, out_dtype=f32, tm=1024, tn=1024, tk=2048, name="matmul_nt"):
    M, K = a.shape
    N, _ = bt.shape
    assert M % tm == 0 and N % tn == 0 and K % tk == 0, (M, N, K, tm, tn, tk)
    return pl.pallas_call(
        _mm_nt_body,
        grid=(M // tm, N // tn, K // tk),
        in_specs=[pl.BlockSpec((tm, tk), lambda i, j, k: (i, k)),
                  pl.BlockSpec((tn, tk), lambda i, j, k: (j, k))],
        out_specs=pl.BlockSpec((tm, tn), lambda i, j, k: (i, j)),
        out_shape=jax.ShapeDtypeStruct((M, N), out_dtype),
        scratch_shapes=[pltpu.VMEM((tm, tn), f32)],
        compiler_params=pltpu.CompilerParams(
            dimension_semantics=("parallel", "parallel", "arbitrary"), vmem_limit_bytes=VMEM_LIMIT),
        name=name,
    )(a, bt)


def _rms_norm_body(x_ref, g_ref, o_ref):
    x = x_ref[...]
    y = x * lax.rsqrt(jnp.mean(x * x, axis=-1, keepdims=True) + NORM_EPS)
    o_ref[...] = (y * g_ref[...]).astype(o_ref.dtype)


def rms_norm_rows(x, g, out_dtype, tm=256):
    M, D = x.shape
    return pl.pallas_call(
        _rms_norm_body,
        grid=(M // tm,),
        in_specs=[pl.BlockSpec((tm, D), lambda i: (i, 0)), pl.BlockSpec((1, D), lambda i: (0, 0))],
        out_specs=pl.BlockSpec((tm, D), lambda i: (i, 0)),
        out_shape=jax.ShapeDtypeStruct((M, D), out_dtype),
        compiler_params=pltpu.CompilerParams(dimension_semantics=("parallel",), vmem_limit_bytes=VMEM_LIMIT),
        name="rms_norm",
    )(x, g.reshape(1, D).astype(f32))


def _out_proj_body(a0_ref, a1_ref, a2_ref, a3_ref, b_ref, h_ref, o_ref, acc_ref):
    k = pl.program_id(2)

    @pl.when(k == 0)
    def _():
        acc_ref[...] = h_ref[...]

    for j, a_ref in enumerate((a0_ref, a1_ref, a2_ref, a3_ref)):
        @pl.when(k == j)
        def _(a_ref=a_ref):
            acc_ref[...] += jnp.dot(a_ref[...].astype(bf16), b_ref[...], preferred_element_type=f32)

    @pl.when(k == pl.num_programs(2) - 1)
    def _():
        o_ref[...] = acc_ref[...]


def out_proj_residual(parts, w, h, tm=1024, tn=1024):
    M, D = h.shape
    aspec = pl.BlockSpec((tm, D_GRP), lambda i, j, k: (i, 0))
    return pl.pallas_call(
        _out_proj_body,
        grid=(M // tm, D // tn, len(parts)),
        in_specs=[aspec] * 4 + [pl.BlockSpec((D_GRP, tn), lambda i, j, k: (k, j)),
                                pl.BlockSpec((tm, tn), lambda i, j, k: (i, j))],
        out_specs=pl.BlockSpec((tm, tn), lambda i, j, k: (i, j)),
        out_shape=jax.ShapeDtypeStruct((M, D), f32),
        scratch_shapes=[pltpu.VMEM((tm, tn), f32)],
        compiler_params=pltpu.CompilerParams(
            dimension_semantics=("parallel", "parallel", "arbitrary"), vmem_limit_bytes=VMEM_LIMIT),
        name="out_proj",
    )(*parts, w, h)


def _ple_body(a_ref, b_ref, p_ref, pw_ref, h_ref, o_ref, acc_ref):
    k = pl.program_id(2)

    @pl.when(k == 0)
    def _():
        acc_ref[...] = jnp.zeros_like(acc_ref)

    acc_ref[...] += jnp.dot(a_ref[...], b_ref[...], preferred_element_type=f32)

    @pl.when(k == pl.num_programs(2) - 1)
    def _():
        emb = jnp.dot(p_ref[...].astype(bf16), pw_ref[...], preferred_element_type=f32)
        o_ref[...] = h_ref[...] + emb * jax.nn.sigmoid(acc_ref[...])


def ple_residual(hn, gate_w, p, ple_w, h, tm=1024, tn=1024, tk=2048):
    M, D = h.shape
    return pl.pallas_call(
        _ple_body,
        grid=(M // tm, D // tn, D // tk),
        in_specs=[pl.BlockSpec((tm, tk), lambda i, j, k: (i, k)),
                  pl.BlockSpec((tk, tn), lambda i, j, k: (k, j)),
                  pl.BlockSpec((tm, PLE_DIM), lambda i, j, k: (i, 0)),
                  pl.BlockSpec((PLE_DIM, tn), lambda i, j, k: (0, j)),
                  pl.BlockSpec((tm, tn), lambda i, j, k: (i, j))],
        out_specs=pl.BlockSpec((tm, tn), lambda i, j, k: (i, j)),
        out_shape=jax.ShapeDtypeStruct((M, D), f32),
        scratch_shapes=[pltpu.VMEM((tm, tn), f32)],
        compiler_params=pltpu.CompilerParams(
            dimension_semantics=("parallel", "parallel", "arbitrary"), vmem_limit_bytes=VMEM_LIMIT),
        name="ple",
    )(hn, gate_w, p, ple_w, h)


def _moe_router_body(x_ref, w_ref, b_ref, id_ref, wt_ref):
    logits = jnp.dot(x_ref[...].astype(bf16), w_ref[...], preferred_element_type=f32) + b_ref[...]
    lane = lax.broadcasted_iota(jnp.int32, logits.shape, 1).astype(f32)
    ninf = -jnp.inf

    def top1(vals):
        m = jnp.max(vals, axis=-1, keepdims=True)
        idx = jnp.min(jnp.where(vals == m, lane, float(LANES)), axis=-1, keepdims=True)
        return m, idx

    gl = jnp.where(lane < N_GROUPS, logits, ninf)
    g_max, g_idx = top1(gl)
    g_w = 1.0 / jnp.sum(jnp.exp(gl - g_max), axis=-1, keepdims=True)
    lo = N_GROUPS + g_idx * EXPERTS_PER_GROUP
    el = jnp.where((lane >= lo) & (lane < lo + EXPERTS_PER_GROUP), logits, ninf)
    e_max, _ = top1(el)
    pe = jnp.exp(el - e_max)
    p = pe / jnp.sum(pe, axis=-1, keepdims=True)
    p = jnp.where(el == ninf, ninf, p)
    p1, i1 = top1(p)
    p2, i2 = top1(jnp.where(lane == i1, ninf, p))
    scale = g_w / (p1 + p2)
    id_ref[...] = jnp.where(lane == 0, i1 - N_GROUPS, jnp.where(lane == 1, i2 - N_GROUPS, 0.0)).astype(jnp.int32)
    wt_ref[...] = jnp.where(lane == 0, p1 * scale, jnp.where(lane == 1, p2 * scale, 0.0))


def moe_router(xb, router_grp, router_grp_b, router_exp, router_exp_b, tm=512):
    M, D = xb.shape
    n_log = N_GROUPS + N_EXPERTS
    w = jnp.pad(jnp.concatenate([router_grp, router_exp], axis=1), ((0, 0), (0, LANES - n_log))).astype(bf16)
    b = jnp.pad(jnp.concatenate([router_grp_b, router_exp_b]), (0, LANES - n_log)).reshape(1, LANES).astype(f32)
    ids, wts = pl.pallas_call(
        _moe_router_body,
        grid=(M // tm,),
        in_specs=[pl.BlockSpec((tm, D), lambda i: (i, 0)), pl.BlockSpec((D, LANES), lambda i: (0, 0)),
                  pl.BlockSpec((1, LANES), lambda i: (0, 0))],
        out_specs=[pl.BlockSpec((tm, LANES), lambda i: (i, 0))] * 2,
        out_shape=[jax.ShapeDtypeStruct((M, LANES), jnp.int32), jax.ShapeDtypeStruct((M, LANES), f32)],
        compiler_params=pltpu.CompilerParams(dimension_semantics=("parallel",), vmem_limit_bytes=VMEM_LIMIT),
        name="moe_router",
    )(xb, w, b)
    return ids[:, :TOP_K], wts[:, :TOP_K]


def _moe_up_body(be_ref, nu_ref, x_ref, wg_ref, wu_ref, h_ref, *, tf):
    i = pl.program_id(0)

    @pl.when(i < nu_ref[0])
    def _():
        x = x_ref[...].astype(bf16)
        for f in range(D_EXPERT // tf):
            c = slice(f * tf, (f + 1) * tf)
            hg = jnp.dot(x, wg_ref[0, 0, :, c].astype(bf16), preferred_element_type=f32)
            hu = jnp.dot(x, wu_ref[0, 0, :, c].astype(bf16), preferred_element_type=f32)
            h_ref[:, c] = (hg * jax.nn.sigmoid(hg) * hu).astype(bf16)

    @pl.when(i >= nu_ref[0])
    def _():
        h_ref[...] = jnp.zeros_like(h_ref)


def _moe_down_body(be_ref, nu_ref, h_ref, wd_ref, wt_ref, y_ref):
    i = pl.program_id(0)

    @pl.when(i < nu_ref[0])
    def _():
        y_ref[...] = jnp.dot(h_ref[...], wd_ref[0, 0].astype(bf16), preferred_element_type=f32) * wt_ref[...]

    @pl.when(i >= nu_ref[0])
    def _():
        y_ref[...] = jnp.zeros_like(y_ref)


def moe_experts(x_pad, w_pad, blk_expert, n_used, wg, wu, wd, layer, tf=256):
    A_pad, D = x_pad.shape
    n_blk = A_pad // ROW_BLOCK
    wmap = lambda i, be, nu: (layer, be[i], 0, 0)
    wspec = pl.BlockSpec((1, 1, D, D_EXPERT), wmap, pipeline_mode=pl.Buffered(1))
    wspec2 = pl.BlockSpec((1, 1, D, D_EXPERT), wmap)
    h = pl.pallas_call(
        functools.partial(_moe_up_body, tf=tf),
        grid_spec=pltpu.PrefetchScalarGridSpec(
            num_scalar_prefetch=2,
            grid=(n_blk,),
            in_specs=[pl.BlockSpec((ROW_BLOCK, D), lambda i, be, nu: (i, 0)), wspec2, wspec],
            out_specs=pl.BlockSpec((ROW_BLOCK, D_EXPERT), lambda i, be, nu: (i, 0)),
        ),
        out_shape=jax.ShapeDtypeStruct((A_pad, D_EXPERT), bf16),
        compiler_params=pltpu.CompilerParams(dimension_semantics=("arbitrary",), vmem_limit_bytes=VMEM_LIMIT),
        name="moe_up",
    )(blk_expert, n_used, x_pad, wg, wu)
    return pl.pallas_call(
        _moe_down_body,
        grid_spec=pltpu.PrefetchScalarGridSpec(
            num_scalar_prefetch=2,
            grid=(n_blk,),
            in_specs=[pl.BlockSpec((ROW_BLOCK, D_EXPERT), lambda i, be, nu: (i, 0)),
                      pl.BlockSpec((1, 1, D_EXPERT, D), lambda i, be, nu: (layer, be[i], 0, 0)),
                      pl.BlockSpec((ROW_BLOCK, 1), lambda i, be, nu: (i, 0))],
            out_specs=pl.BlockSpec((ROW_BLOCK, D), lambda i, be, nu: (i, 0)),
        ),
        out_shape=jax.ShapeDtypeStruct((A_pad, D), f32),
        compiler_params=pltpu.CompilerParams(dimension_semantics=("arbitrary",), vmem_limit_bytes=VMEM_LIMIT),
        name="moe_down",
    )(blk_expert, n_used, h, wd, w_pad.reshape(A_pad, 1))


def hier_moe(xb, router_grp, router_grp_b, router_exp, router_exp_b, w_gate, w_up, w_down, layer):
    M, D = xb.shape
    expert_ids, weights = moe_router(xb, router_grp, router_grp_b, router_exp, router_exp_b)
    A = M * TOP_K
    e_flat = expert_ids.reshape(A)
    onehot = (e_flat[:, None] == jnp.arange(N_EXPERTS)[None, :]).astype(jnp.int32)
    rank = jnp.take_along_axis(jnp.cumsum(onehot, axis=0), e_flat[:, None], axis=1)[:, 0] - 1
    counts = jnp.sum(onehot, axis=0)
    padded = (counts + ROW_BLOCK - 1) // ROW_BLOCK * ROW_BLOCK
    end_pad = jnp.cumsum(padded)
    dest = (end_pad - padded)[e_flat] + rank
    n_blk = -(-A // ROW_BLOCK) + N_EXPERTS
    A_pad = n_blk * ROW_BLOCK
    tok_pad = (jnp.arange(A_pad, dtype=jnp.int32) % M).at[dest].set(jnp.arange(A, dtype=jnp.int32) // TOP_K)
    w_pad = jnp.zeros((A_pad,), f32).at[dest].set(weights.reshape(A))
    blk_expert = jnp.minimum(jnp.searchsorted(end_pad, jnp.arange(n_blk) * ROW_BLOCK, side='right'),
                             N_EXPERTS - 1).astype(jnp.int32)
    n_used = (end_pad[-1:] // ROW_BLOCK).astype(jnp.int32)
    x_pad = xb[tok_pad]
    y_pad = moe_experts(x_pad, w_pad, blk_expert, n_used, w_gate, w_up, w_down, layer)
    rows = dest.reshape(M, TOP_K)
    return y_pad[rows[:, 0]] + y_pad[rows[:, 1]]


def _rope_tables(S, theta, rot_dims, width):
    half = rot_dims // 2
    freqs = theta ** (-np.arange(half, dtype=np.float64) / half)
    ang = np.arange(S, dtype=np.float64)[:, None] * freqs[None, :]
    cos = np.ones((S, width), np.float64)
    sin_lo = np.zeros((S, width), np.float64)
    sin_hi = np.zeros((S, width), np.float64)
    cos[:, :half] = np.cos(ang)
    cos[:, half:rot_dims] = np.cos(ang)
    sin_lo[:, :half] = -np.sin(ang)
    sin_hi[:, half:rot_dims] = np.sin(ang)
    return [jnp.asarray(t, f32) for t in (cos, sin_lo, sin_hi)]


def _rope128(x, cos, sin_lo, sin_hi, half):
    return x * cos + pltpu.roll(x, LANES - half, axis=1) * sin_lo + pltpu.roll(x, half, axis=1) * sin_hi


def _nsa_prep_body(q_ref, kv_ref, cos_ref, sl_ref, sh_ref, qo_ref, kvo_ref, co_ref):
    cos, sl, sh = cos_ref[...], sl_ref[...], sh_ref[...]
    half = ROPE_DIMS // 2
    scale = HEAD_DIM ** -0.5
    for h in range(NSA_HEADS):
        c = slice(h * LANES, (h + 1) * LANES)
        qo_ref[:, c] = (_rope128(q_ref[:, c], cos, sl, sh, half) * scale).astype(bf16)
    for j in range(6):
        for g in range(NSA_KV_HEADS):
            c = slice((j * NSA_KV_HEADS + g) * LANES, (j * NSA_KV_HEADS + g + 1) * LANES)
            x = kv_ref[:, c]
            if j % 2 == 0:
                x = _rope128(x, cos, sl, sh, half)
            if j < 2:
                co_ref[:, c] = x.astype(bf16)
            else:
                kvo_ref[:, slice(c.start - 4 * LANES, c.stop - 4 * LANES)] = x.astype(bf16)


def nsa_prep(proj, S, tm=512):
    M = proj.shape[0]
    assert M % tm == 0 and S % tm == 0
    tabs = _rope_tables(S, ROPE_THETA, ROPE_DIMS, LANES)
    tspec = pl.BlockSpec((tm, LANES), lambda i: (i % (S // tm), 0))
    return pl.pallas_call(
        _nsa_prep_body,
        grid=(M // tm,),
        in_specs=[pl.BlockSpec((tm, D_GRP), lambda i: (i, OFF_NSA_Q // D_GRP)),
                  pl.BlockSpec((tm, 6 * NSA_KV_W), lambda i: (i, OFF_NSA_KV // (6 * NSA_KV_W))),
                  tspec, tspec, tspec],
        out_specs=[pl.BlockSpec((tm, D_GRP), lambda i: (i, 0)),
                   pl.BlockSpec((tm, 4 * NSA_KV_W), lambda i: (i, 0)),
                   pl.BlockSpec((tm, 2 * NSA_KV_W), lambda i: (i, 0))],
        out_shape=[jax.ShapeDtypeStruct((M, D_GRP), bf16), jax.ShapeDtypeStruct((M, 4 * NSA_KV_W), bf16),
                   jax.ShapeDtypeStruct((M, 2 * NSA_KV_W), bf16)],
        compiler_params=pltpu.CompilerParams(dimension_semantics=("parallel",), vmem_limit_bytes=VMEM_LIMIT),
        name="nsa_prep",
    )(proj, proj, *tabs)


def _nsa_compress_body(r_ref, w1_ref, w2_ref, pe_ref, o_ref):
    r = r_ref[0, 0, 0]
    half_w = CMP_STRIDE * HEAD_DIM
    a = jnp.dot(r, w1_ref[0, :half_w, :], preferred_element_type=f32)
    b = jnp.dot(r, w1_ref[0, half_w:, :], preferred_element_type=f32)
    n = r.shape[0]
    pe = jnp.dot(jnp.broadcast_to(pe_ref[0], (SUB, pe_ref.shape[-1])), w1_ref[0], preferred_element_type=f32)[0:1]
    pre = a + pltpu.roll(b, n - 1, axis=0) + pe
    hid = pre * jax.nn.sigmoid(pre)
    o_ref[0, 0, 0] = jnp.dot(hid.astype(bf16), w2_ref[0], preferred_element_type=f32).astype(bf16)


def nsa_compress(c_rot, w1s, w2s, pes, B, S):
    G, Dh = NSA_KV_HEADS, HEAD_DIM
    nc = S // CMP_STRIDE
    r = c_rot.reshape(B, nc, CMP_STRIDE, 2, G, Dh).transpose(3, 0, 4, 1, 2, 5).reshape(2, B, G, nc, CMP_STRIDE * Dh)
    w1 = jnp.stack(w1s).astype(bf16)
    w2 = jnp.stack(w2s).astype(bf16)
    pe = jnp.stack(pes).reshape(2, 1, CMP_LEN * Dh).astype(bf16)
    return pl.pallas_call(
        _nsa_compress_body,
        grid=(2, B, G),
        in_specs=[pl.BlockSpec((1, 1, 1, nc, CMP_STRIDE * Dh), lambda j, b, g: (j, b, g, 0, 0)),
                  pl.BlockSpec((1, CMP_LEN * Dh, Dh), lambda j, b, g: (j, 0, 0)),
                  pl.BlockSpec((1, Dh, Dh), lambda j, b, g: (j, 0, 0)),
                  pl.BlockSpec((1, 1, CMP_LEN * Dh), lambda j, b, g: (j, 0, 0))],
        out_specs=pl.BlockSpec((1, 1, 1, nc, Dh), lambda j, b, g: (j, b, g, 0, 0)),
        out_shape=jax.ShapeDtypeStruct((2, B, G, nc, Dh), bf16),
        compiler_params=pltpu.CompilerParams(dimension_semantics=("parallel", "parallel", "parallel"),
                                             vmem_limit_bytes=VMEM_LIMIT),
        name="nsa_compress",
    )(r, w1, w2, pe)


def _softmax_rows(s, mask):
    s = jnp.where(mask, s, -jnp.inf)
    m = jnp.max(s, axis=-1, keepdims=True)
    m = jnp.where(m == -jnp.inf, 0.0, m)
    e = jnp.exp(s - m)
    return e, jnp.maximum(jnp.sum(e, axis=-1, keepdims=True), jnp.finfo(f32).tiny)


def _nsa_attn_body(q_ref, cmp_ref, ks_ref, vs_ref, kw_ref, vw_ref, gate_ref, ovt_ref, o_ref, *, tq, tk, S):
    g = pl.program_id(1)
    qi = pl.program_id(2)
    R, Dh = NSA_GROUP, HEAD_DIM
    rows = R * tq
    n_blk = S // SEL_LEN
    n_cmp = S // CMP_STRIDE - 1
    q0 = qi * tq
    q4 = jnp.concatenate([q_ref[0, :, r * Dh:(r + 1) * Dh] for r in range(R)], axis=0)
    nt = (((1,), (1,)), ((), ()))

    def qpos(shape):
        return q0 + lax.broadcasted_iota(jnp.int32, shape, 0) % tq

    k_cmp, v_cmp = cmp_ref[0, 0, 0], cmp_ref[1, 0, 0]
    ncp = k_cmp.shape[0]
    s = lax.dot_general(q4, k_cmp, nt, preferred_element_type=f32)
    n_idx = lax.broadcasted_iota(jnp.int32, (rows, ncp), 1)
    mask = (n_idx * CMP_STRIDE + (CMP_LEN - 1) <= qpos((rows, ncp))) & (n_idx < n_cmp)
    e, den = _softmax_rows(s, mask)
    p = e / den
    o_cmp = jnp.dot(p.astype(bf16), v_cmp, preferred_element_type=f32)

    psum = p[0:tq]
    for r in range(1, R):
        psum = psum + p[r * tq:(r + 1) * tq]
    ph, plo = _split2(psum)
    imp = (lax.dot_general(ovt_ref[...], ph, nt, preferred_element_type=f32)
           + lax.dot_general(ovt_ref[...], plo, nt, preferred_element_type=f32))
    blk = lax.broadcasted_iota(jnp.int32, (n_blk, tq), 0)
    cur = (q0 + lax.broadcasted_iota(jnp.int32, (n_blk, tq), 1)) // SEL_LEN
    visible = blk <= cur
    forced = (blk == 0) | (visible & (blk > cur - N_LOCAL_FORCED))
    imp = jnp.where(forced, FORCED_SCORE, jnp.where(visible, imp, -1.0))
    sel_t = jnp.zeros((n_blk, tq), f32)
    blk_f = blk.astype(f32)
    for _ in range(min(N_SEL, n_blk)):
        m = jnp.max(imp, axis=0, keepdims=True)
        first = jnp.min(jnp.where(imp == m, blk_f, float(n_blk)), axis=0, keepdims=True)
        hit = blk_f == first
        sel_t = jnp.where(hit, 1.0, sel_t)
        imp = jnp.where(hit, -jnp.inf, imp)
    sel_q = jnp.concatenate([sel_t.T, jnp.ones((tq, LANES - n_blk), f32)], axis=1).astype(bf16)

    blocks_per_tile = tk // SEL_LEN

    big = float(2 ** 30)
    kb = sel_q.shape[1]

    def sel_step(kt, carry, causal):
        m_run, l_run, acc = carry
        k0 = pl.multiple_of(kt * tk, tk)
        kt_k = ks_ref[0, pl.ds(k0, tk), :]
        kt_v = vs_ref[0, pl.ds(k0, tk), :]
        s = lax.dot_general(q4, kt_k, nt, preferred_element_type=f32)
        row = lax.broadcasted_iota(jnp.int32, (kb, tk), 0)
        in_blk = row == kt * blocks_per_tile + lax.broadcasted_iota(jnp.int32, (kb, tk), 1) // SEL_LEN
        expand = jnp.where(in_blk, big, jnp.where(row == n_blk, -big, 0.0)).astype(bf16)
        bias = jnp.dot(sel_q, expand, preferred_element_type=f32)
        s = (s.reshape(R, tq, tk) + bias[None]).reshape(rows, tk)
        if causal:
            kpos = k0 + lax.broadcasted_iota(jnp.int32, (rows, tk), 1)
            s = jnp.where(kpos <= qpos((rows, tk)), s, -big)
        m_new = jnp.maximum(m_run, jnp.max(s, axis=-1, keepdims=True))
        alpha = jnp.exp(m_run - m_new)
        e = jnp.exp(s - m_new)
        l_new = alpha * l_run + jnp.sum(e, axis=-1, keepdims=True)
        acc = alpha * acc + jnp.dot(e.astype(bf16), kt_v, preferred_element_type=f32)
        return m_new, l_new, acc

    last_kt = (q0 + tq - 1) // tk
    init = (jnp.full((rows, 1), -jnp.inf, f32), jnp.zeros((rows, 1), f32), jnp.zeros((rows, Dh), f32))
    carry = lax.fori_loop(0, last_kt, functools.partial(sel_step, causal=False), init)
    _, l_sel, acc_sel = sel_step(last_kt, carry, causal=True)
    o_sel = acc_sel / l_sel

    span = WINDOW + tq
    w0 = pl.multiple_of(jnp.maximum(q0 - WINDOW, 0), tq)
    kwin = kw_ref[0, pl.ds(w0, span), :]
    vwin = vw_ref[0, pl.ds(w0, span), :]
    s = lax.dot_general(q4, kwin, nt, preferred_element_type=f32)
    kpos = w0 + lax.broadcasted_iota(jnp.int32, (rows, span), 1)
    qp = qpos((rows, span))
    e, den = _softmax_rows(s, (kpos <= qp) & (kpos > qp - WINDOW))
    o_win = jnp.dot(e.astype(bf16), vwin, preferred_element_type=f32) / den

    gate = jax.nn.sigmoid(gate_ref[0])
    for r in range(R):
        c0 = (g * R + r) * 3
        lane = lax.broadcasted_iota(jnp.int32, (tq, LANES), 1)
        gsel = lambda c: jnp.sum(jnp.where(lane == c0 + c, gate, 0.0), axis=-1, keepdims=True)
        rs = slice(r * tq, (r + 1) * tq)
        o_ref[0, :, r * Dh:(r + 1) * Dh] = (gsel(0) * o_cmp[rs] + gsel(1) * o_sel[rs]
                                            + gsel(2) * o_win[rs]).astype(o_ref.dtype)


def nsa_attention(q_rot, kv_rot, cmp_kv, proj3, B, S, tq=256, tk=512):
    G, R, Dh = NSA_KV_HEADS, NSA_GROUP, HEAD_DIM
    n_blk = S // SEL_LEN
    assert S % tk == 0 and tk % tq == 0 and S >= WINDOW + tq and n_blk % SUB == 0 and n_blk < LANES
    nc = S // CMP_STRIDE
    c0 = np.arange(nc - 1)[None, :] * CMP_STRIDE
    s0 = np.arange(n_blk)[:, None] * SEL_LEN
    ovt = np.zeros((n_blk, nc), np.float32)
    ovt[:, :nc - 1] = np.clip(np.minimum(c0 + CMP_LEN, s0 + SEL_LEN) - np.maximum(c0, s0), 0, None) / CMP_LEN
    kvspec = lambda j: pl.BlockSpec((1, S, Dh), lambda b, g, i: (b, 0, j * G + g))
    return pl.pallas_call(
        functools.partial(_nsa_attn_body, tq=tq, tk=tk, S=S),
        grid=(B, G, S // tq),
        in_specs=[pl.BlockSpec((1, tq, R * Dh), lambda b, g, i: (b, i, g)),
                  pl.BlockSpec((2, 1, 1, nc, Dh), lambda b, g, i: (0, b, g, 0, 0)),
                  kvspec(0), kvspec(1), kvspec(2), kvspec(3),
                  pl.BlockSpec((1, tq, LANES), lambda b, g, i: (b, i, OFF_SMALL // LANES)),
                  pl.BlockSpec((n_blk, nc), lambda b, g, i: (0, 0))],
        out_specs=pl.BlockSpec((1, tq, R * Dh), lambda b, g, i: (b, i, g)),
        out_shape=jax.ShapeDtypeStruct((B, S, D_GRP), bf16),
        compiler_params=pltpu.CompilerParams(dimension_semantics=("parallel", "parallel", "arbitrary"),
                                             vmem_limit_bytes=VMEM_LIMIT),
        name="nsa_attention",
    )(q_rot, cmp_kv, kv_rot, kv_rot, kv_rot, kv_rot, proj3, jnp.asarray(ovt, bf16))


def nsa_mixer_pallas(proj, ck_w1, ck_w2, ck_pe, cv_w1, cv_w2, cv_pe, B, S):
    q_rot, kv_rot, c_rot = nsa_prep(proj, S)
    cmp_kv = nsa_compress(c_rot, (ck_w1, cv_w1), (ck_w2, cv_w2), (ck_pe, cv_pe), B, S)
    return nsa_attention(q_rot.reshape(B, S, D_GRP), kv_rot.reshape(B, S, 4 * NSA_KV_W), cmp_kv,
                         proj.reshape(B, S, P_PAD), B, S)


def _retention_body(q_ref, k_ref, v_ref, g_ref, cos_ref, sin_ref, dec_ref, xi_ref, zeta_ref, gc_ref, o_ref, st_ref,
                    *, T, HS):
    @pl.when(pl.program_id(2) == 0)
    def _():
        st_ref[...] = jnp.zeros_like(st_ref)

    C, Dh = RET_CHUNK, HEAD_DIM
    nt = (((1,), (1,)), ((), ()))
    pre = {}
    for hh in range(HS):
        hc = slice(hh * Dh, (hh + 1) * Dh)
        for c in range(T // C):
            rs = slice(c * C, (c + 1) * C)
            cos, sin = cos_ref[rs, :], sin_ref[rs, :]
            rope = lambda x: x * cos + pltpu.roll(x, Dh // 2, axis=1) * sin
            q = rope(q_ref[0, rs, hc])
            k = rope(k_ref[0, rs, hc]) * Dh ** -0.5
            qb, vb = q.astype(bf16), v_ref[0, rs, hc].astype(bf16)
            inner = lax.dot_general(qb, k.astype(bf16), nt, preferred_element_type=f32) * dec_ref[hh]
            o_in = jnp.dot(inner.astype(bf16), vb, preferred_element_type=f32)
            kv = jnp.dot((k * zeta_ref[hh]).astype(bf16).T, vb, preferred_element_type=f32)
            pre[hh, c] = (qb, o_in, kv)
    states = [st_ref[hh] for hh in range(HS)]
    for c in range(T // C):
        rs = slice(c * C, (c + 1) * C)
        for hh in range(HS):
            hc = slice(hh * Dh, (hh + 1) * Dh)
            qb, o_in, kv = pre[hh, c]
            o = o_in + jnp.dot(qb, states[hh].astype(bf16), preferred_element_type=f32) * xi_ref[hh]
            states[hh] = states[hh] * gc_ref[hh, 0:1, :] + kv
            mu = jnp.mean(o, axis=-1, keepdims=True)
            d = o - mu
            o = d * lax.rsqrt(jnp.mean(d * d, axis=-1, keepdims=True) + NORM_EPS)
            gt = g_ref[0, rs, hc]
            o_ref[0, rs, hc] = (gt * jax.nn.sigmoid(gt) * o).astype(o_ref.dtype)
    for hh in range(HS):
        st_ref[hh] = states[hh]


def retention_mixer_pallas(proj3, B, S, T=512, HS=2):
    H, Dh, C = RET_HEADS, HEAD_DIM, RET_CHUNK
    W = HS * Dh
    assert S % T == 0 and T % C == 0 and H % HS == 0
    freqs = RET_THETA ** (-np.linspace(0.0, 1.0, Dh // 2))
    ang = np.arange(S, dtype=np.float64)[:, None] * freqs[None, :]
    cos = jnp.asarray(np.concatenate([np.cos(ang), np.cos(ang)], 1), f32)
    sin = jnp.asarray(np.concatenate([-np.sin(ang), np.sin(ang)], 1), f32)
    log_gamma = np.log1p(-(2.0 ** (-5.0 - np.arange(H, dtype=np.float64))))
    n = np.arange(C, dtype=np.float64)
    diff = n[:, None] - n[None, :]
    decay_in = np.where(diff >= 0, np.exp(log_gamma[:, None, None] * np.maximum(diff, 0.0)), 0.0)
    xi = np.exp(log_gamma[:, None] * (n + 1.0))[:, :, None] * np.ones((1, 1, Dh))
    zeta = np.exp(log_gamma[:, None] * (C - 1.0 - n))[:, :, None] * np.ones((1, 1, Dh))
    gamma_c = np.exp(log_gamma * C)[:, None, None] * np.ones((1, SUB, Dh))
    blk = lambda j: pl.BlockSpec((1, T, W), lambda b, h, t: (b, t, (OFF_RET + j * D_GRP) // W + h))
    tab = pl.BlockSpec((T, Dh), lambda b, h, t: (t, 0))
    per_head = lambda r: pl.BlockSpec((HS, r, Dh), lambda b, h, t: (h, 0, 0))
    return pl.pallas_call(
        functools.partial(_retention_body, T=T, HS=HS),
        grid=(B, H // HS, S // T),
        in_specs=[blk(0), blk(1), blk(2), blk(3), tab, tab, per_head(C), per_head(C), per_head(C), per_head(SUB)],
        out_specs=pl.BlockSpec((1, T, W), lambda b, h, t: (b, t, h)),
        out_shape=jax.ShapeDtypeStruct((B, S, D_GRP), bf16),
        scratch_shapes=[pltpu.VMEM((HS, Dh, Dh), f32)],
        compiler_params=pltpu.CompilerParams(dimension_semantics=("parallel", "parallel", "arbitrary"),
                                             vmem_limit_bytes=VMEM_LIMIT),
        name="retention",
    )(proj3, proj3, proj3, proj3, cos, sin, jnp.asarray(decay_in, f32), jnp.asarray(xi, f32),
      jnp.asarray(zeta, f32), jnp.asarray(gamma_c, f32))


def _dot3(a, b):
    ah, al = _split2(a)
    bh, bl = _split2(b)
    return (jnp.dot(ah, bh, preferred_element_type=f32) + jnp.dot(ah, bl, preferred_element_type=f32)
            + jnp.dot(al, bh, preferred_element_type=f32))


def _deltanet_body(q_ref, k_ref, v_ref, z_ref, qp_ref, kp_ref, vp_ref, ab_ref, cw_ref, alog_ref, dtb_ref, ng_ref,
                   o_ref, st_ref, *, T, HS):
    hb = pl.program_id(1)
    first = pl.program_id(2) == 0

    @pl.when(first)
    def _():
        st_ref[...] = jnp.zeros_like(st_ref)

    C, Dh = DN_CHUNK, HEAD_DIM
    nt = (((1,), (1,)), ((), ()))
    n_chunks = T // C
    ab = ab_ref[0]
    lane = lax.broadcasted_iota(jnp.int32, ab.shape, 1)

    def conv_silu(x_ref, prev_ref, j, hc):
        prev = jnp.where(first, 0.0, prev_ref[0, :, hc])
        xcat = jnp.concatenate([prev, x_ref[0, :, hc]], axis=0)
        w = cw_ref[:, j, hc]
        y = sum(w[i:i + 1, :] * xcat[SUB - (CONV_WIDTH - 1) + i:SUB - (CONV_WIDTH - 1) + i + T, :]
                for i in range(CONV_WIDTH))
        return y * jax.nn.sigmoid(y)

    def l2n(x):
        return x * lax.rsqrt(jnp.sum(x * x, axis=-1, keepdims=True) + L2NORM_EPS)

    units = [(hh, slice(c * C, (c + 1) * C)) for hh in range(HS) for c in range(n_chunks)]
    qs, ks_, vs_, betas, g_rows = [], [], [], [], []
    for hh in range(HS):
        hc = slice(hh * Dh, (hh + 1) * Dh)
        head = hb * HS + hh
        q_all = l2n(conv_silu(q_ref, qp_ref, 0, hc)) * Dh ** -0.5
        k_all = l2n(conv_silu(k_ref, kp_ref, 1, hc))
        v_all = conv_silu(v_ref, vp_ref, 2, hc)
        a_col = jnp.sum(jnp.where(lane == SMALL_DN_A + head, ab, 0.0), axis=-1, keepdims=True)
        b_col = jnp.sum(jnp.where(lane == SMALL_DN_A + DN_HEADS + head, ab, 0.0), axis=-1, keepdims=True)
        beta_all = jax.nn.sigmoid(b_col)
        g_all = -jnp.exp(alog_ref[hh, 0:1, :]) * jax.nn.softplus(a_col + dtb_ref[hh, 0:1, :])
        for c in range(n_chunks):
            rs = slice(c * C, (c + 1) * C)
            qs.append(q_all[rs]); ks_.append(k_all[rs]); vs_.append(v_all[rs])
            betas.append(beta_all[rs]); g_rows.append(g_all[rs])

    ri = lax.broadcasted_iota(jnp.int32, (C, C), 0)
    ci = lax.broadcasted_iota(jnp.int32, (C, C), 1)
    tril, strict = ri >= ci, ri > ci
    ltri = tril.astype(bf16)
    eye = (ri == ci).astype(f32)
    gcs = []
    for g_rows_u in g_rows:
        gh, gm, gl = _split3(g_rows_u)
        gcs.append(jnp.dot(ltri, gh, preferred_element_type=f32) + jnp.dot(ltri, gm, preferred_element_type=f32)
                   + jnp.dot(ltri, gl, preferred_element_type=f32))
    decays = [jnp.exp(jnp.where(tril, gc[:, :C] - gc[:, :C].T, -jnp.inf)) for gc in gcs]
    kbs = [k.astype(bf16) for k in ks_]
    kbetas = [k * b for k, b in zip(ks_, betas)]
    ms = [jnp.where(strict, lax.dot_general(kbeta.astype(bf16), kb, nt, preferred_element_type=f32) * d, 0.0)
          for kbeta, kb, d in zip(kbetas, kbs, decays)]
    attns = [jnp.where(tril, lax.dot_general(q.astype(bf16), kb, nt, preferred_element_type=f32) * d, 0.0).astype(bf16)
             for q, kb, d in zip(qs, kbs, decays)]
    t_invs, pws = [eye - m for m in ms], ms
    for _ in range(5):
        pws = [_dot3(pw, pw) for pw in pws]
        t_invs = [_dot3(t, eye + pw) for t, pw in zip(t_invs, pws)]
    egcs = [jnp.exp(gc) for gc in gcs]
    tbs = [t.astype(bf16) for t in t_invs]
    us = [jnp.dot(tb, (v * b).astype(bf16), preferred_element_type=f32) for tb, v, b in zip(tbs, vs_, betas)]
    ws = [jnp.dot(tb, (kbeta * egc).astype(bf16), preferred_element_type=f32).astype(bf16)
          for tb, kbeta, egc in zip(tbs, kbetas, egcs)]
    states = [st_ref[hh] for hh in range(HS)]
    for c in range(n_chunks):
        for hh in range(HS):
            u, rs = hh * n_chunks + c, slice(c * C, (c + 1) * C)
            g_last = gcs[u][C - 1:C, :]
            q_dec = (qs[u] * egcs[u]).astype(bf16)
            k_dec = (ks_[u] * jnp.exp(g_last - gcs[u])).astype(bf16)
            sb = states[hh].astype(bf16)
            v_new = us[u] - jnp.dot(ws[u], sb, preferred_element_type=f32)
            vnb = v_new.astype(bf16)
            o = jnp.dot(q_dec, sb, preferred_element_type=f32) + jnp.dot(attns[u], vnb, preferred_element_type=f32)
            states[hh] = states[hh] * jnp.exp(g_last) + jnp.dot(k_dec.T, vnb, preferred_element_type=f32)
            o = o * lax.rsqrt(jnp.mean(o * o, axis=-1, keepdims=True) + NORM_EPS) * ng_ref[...]
            zt = z_ref[0, rs, hh * Dh:(hh + 1) * Dh]
            o_ref[0, rs, hh * Dh:(hh + 1) * Dh] = (o * (zt * jax.nn.sigmoid(zt))).astype(o_ref.dtype)
    for hh in range(HS):
        st_ref[hh] = states[hh]


def deltanet_mixer_pallas(proj3, conv_w, a_log, dt_bias, norm_g, B, S, T=256, HS=4):
    H, Dh = DN_HEADS, HEAD_DIM
    W = HS * Dh
    assert S % T == 0 and T % DN_CHUNK == 0 and H % HS == 0
    blk = lambda j: pl.BlockSpec((1, T, W), lambda b, h, t: (b, t, (OFF_DN + j * D_GRP) // W + h))
    prev = lambda j: pl.BlockSpec((1, SUB, W), lambda b, h, t: (b, jnp.maximum(t * (T // SUB) - 1, 0),
                                                               (OFF_DN + j * D_GRP) // W + h))
    per_head = pl.BlockSpec((HS, SUB, Dh), lambda b, h, t: (h, 0, 0))
    bcast = lambda x: jnp.broadcast_to(x.astype(f32)[:, None, None], (H, SUB, Dh))
    return pl.pallas_call(
        functools.partial(_deltanet_body, T=T, HS=HS),
        grid=(B, H // HS, S // T),
        in_specs=[blk(0), blk(1), blk(2), blk(3), prev(0), prev(1), prev(2),
                  pl.BlockSpec((1, T, LANES), lambda b, h, t: (b, t, OFF_SMALL // LANES)),
                  pl.BlockSpec((CONV_WIDTH, 3, W), lambda b, h, t: (0, 0, h)),
                  per_head, per_head,
                  pl.BlockSpec((1, Dh), lambda b, h, t: (0, 0))],
        out_specs=pl.BlockSpec((1, T, W), lambda b, h, t: (b, t, h)),
        out_shape=jax.ShapeDtypeStruct((B, S, D_GRP), bf16),
        scratch_shapes=[pltpu.VMEM((HS, Dh, Dh), f32)],
        compiler_params=pltpu.CompilerParams(dimension_semantics=("parallel", "parallel", "arbitrary"),
                                             vmem_limit_bytes=VMEM_LIMIT),
        name="deltanet",
    )(proj3, proj3, proj3, proj3, proj3, proj3, proj3, proj3, conv_w.reshape(CONV_WIDTH, 3, D_GRP),
      bcast(a_log), bcast(dt_bias), norm_g.reshape(1, Dh))


def _w_in_pieces():
    n0, r0, d0, w0 = 0, NSA_COLS, NSA_COLS + RET_COLS, NSA_COLS + RET_COLS + DN_COLS
    n_gate = 3 * NSA_HEADS
    return [
        (OFF_RW, w0, 3 * D_GRP),
        (OFF_RET, r0, 4 * D_GRP),
        (OFF_DN, d0, 3 * D_GRP),
        (OFF_DN + 3 * D_GRP, d0 + 3 * D_GRP + 2 * DN_HEADS, D_GRP),
        (OFF_NSA_Q, n0, D_GRP + 6 * NSA_KV_W),
        (OFF_LORA, w0 + 3 * D_GRP, RW_LORA),
        (OFF_SMALL, n0 + D_GRP + 6 * NSA_KV_W, n_gate),
        (OFF_SMALL + SMALL_DN_A, d0 + 3 * D_GRP, 2 * DN_HEADS),
    ]


def _pack_w_in_body(w_ref, o_ref):
    cols = o_ref.shape[1]
    for off, width in ((OFF_LORA + RW_LORA, RW_LORA_PAD - RW_LORA), (OFF_SMALL, LANES)):
        o_ref[off:off + width, :] = jnp.zeros((width, cols), o_ref.dtype)
    for dst, src, width in _w_in_pieces():
        o_ref[dst:dst + width, :] = w_ref[0, src:src + width, :].astype(o_ref.dtype)


def pack_w_in_t(w_in, layer, tc=256):
    K = w_in.shape[1]
    w_t = jnp.swapaxes(w_in, 1, 2)
    return pl.pallas_call(
        _pack_w_in_body,
        grid=(K // tc,),
        in_specs=[pl.BlockSpec((1, P_TOTAL, tc), lambda i: (layer, 0, i))],
        out_specs=pl.BlockSpec((P_PAD, tc), lambda i: (0, i)),
        out_shape=jax.ShapeDtypeStruct((P_PAD, K), bf16),
        compiler_params=pltpu.CompilerParams(dimension_semantics=("parallel",), vmem_limit_bytes=VMEM_LIMIT),
        name="pack_w_in",
    )(w_t)


def hybrid_mixer(h, hn, B, S, layer, w_in, w_out, ck_w1, ck_w2, ck_pe, cv_w1, cv_w2, cv_pe, dn_conv, dn_a_log,
                 dn_dt_bias, dn_norm, rw_mu, rw_w0, rw_w2, rw_a0, rw_a2, rw_g2, rw_kk, rw_ka, rw_rk, rw_ln_g, rw_ln_b):
    M = B * S
    proj = matmul_nt(hn, pack_w_in_t(w_in, layer), name="in_proj")
    p3 = proj.reshape(B, S, P_PAD)
    o_nsa = nsa_mixer_pallas(proj, ck_w1, ck_w2, ck_pe, cv_w1, cv_w2, cv_pe, B, S)
    o_ret = retention_mixer_pallas(p3, B, S)
    o_dn = deltanet_mixer_pallas(p3, dn_conv, dn_a_log, dn_dt_bias, dn_norm, B, S)
    o_rw = rwkv7_mixer(proj, OFF_RW // (3 * D_GRP), OFF_LORA // RW_LORA_PAD, rw_mu, rw_w0, rw_w2, rw_a0, rw_a2,
                       rw_g2, rw_kk, rw_ka, rw_rk, rw_ln_g, rw_ln_b, B, S)
    parts = [o.reshape(M, D_GRP) for o in (o_nsa, o_ret, o_dn, o_rw)]
    return out_proj_residual(parts, w_out.astype(bf16), h)


def kernel(x, p, norm_mix, w_in, w_out, nsa_ck_w1, nsa_ck_w2, nsa_ck_pe, nsa_cv_w1, nsa_cv_w2, nsa_cv_pe, dn_conv, dn_a_log, dn_dt_bias, dn_norm, rw_mu, rw_w0, rw_w2, rw_a0, rw_a2, rw_g2, rw_kk, rw_ka, rw_rk, rw_ln_g, rw_ln_b, norm_moe, moe_router_grp, moe_router_grp_b, moe_router_exp, moe_router_exp_b, moe_w_gate, moe_w_up, moe_w_down, norm_ple, ple_w, ple_gate, norm_final):
    B, S, D = x.shape
    M = B * S
    h = x.reshape(M, D)
    for i in range(DEPTH):
        hn = rms_norm_rows(h, norm_mix[i], bf16)
        h = hybrid_mixer(h, hn, B, S, i, w_in, w_out[i], nsa_ck_w1[i], nsa_ck_w2[i], nsa_ck_pe[i], nsa_cv_w1[i],
                         nsa_cv_w2[i], nsa_cv_pe[i], dn_conv[i], dn_a_log[i], dn_dt_bias[i], dn_norm[i],
                         rw_mu[i], rw_w0[i], rw_w2[i], rw_a0[i], rw_a2[i], rw_g2[i], rw_kk[i], rw_ka[i],
                         rw_rk[i], rw_ln_g[i], rw_ln_b[i])
        hn = rms_norm_rows(h, norm_moe[i], bf16)
        h = h + hier_moe(hn, moe_router_grp[i], moe_router_grp_b[i], moe_router_exp[i], moe_router_exp_b[i],
                         moe_w_gate, moe_w_up, moe_w_down, i)
        hn = rms_norm_rows(h, norm_ple[i], bf16)
        h = ple_residual(hn, ple_gate[i].astype(bf16), p[i].reshape(M, PLE_DIM), ple_w[i].astype(bf16), h)
    return rms_norm_rows(h, norm_final, f32).reshape(B, S, D)
```

```python
import functools

import jax
import jax.numpy as jnp
import numpy as np
from jax import lax
from jax.experimental import pallas as pl
from jax.experimental.pallas import tpu as pltpu

D_MODEL = 4096
DEPTH = 2
f32 = jnp.float32
bf16 = jnp.bfloat16
D_MIX = D_MODEL
D_GRP = D_MIX // 4
HEAD_DIM = 128
NORM_EPS = 1e-6
L2NORM_EPS = 1e-6
NSA_HEADS = D_GRP // HEAD_DIM
NSA_KV_HEADS = 2
NSA_GROUP = NSA_HEADS // NSA_KV_HEADS
NSA_KV_W = NSA_KV_HEADS * HEAD_DIM
CMP_LEN = 32
CMP_STRIDE = 16
SEL_LEN = 64
N_SEL = 16
N_LOCAL_FORCED = 2
FORCED_SCORE = 1e4
WINDOW = 512
ROPE_THETA = 500000.0
ROPE_DIMS = HEAD_DIM // 4
RET_HEADS = D_GRP // HEAD_DIM
RET_CHUNK = 128
RET_THETA = 10000.0
DN_HEADS = D_GRP // HEAD_DIM
DN_CHUNK = 64
CONV_WIDTH = 4
RW_HEAD_DIM = 64
RW_DECAY_LORA = 64
RW_AAA_LORA = 64
RW_GATE_LORA = 160
RW_LN_EPS = 64e-5
N_GROUPS = 4
EXPERTS_PER_GROUP = 8
N_EXPERTS = N_GROUPS * EXPERTS_PER_GROUP
TOP_K = 2
D_EXPERT = 768
ROW_BLOCK = 256
PLE_DIM = 256
NSA_COLS = D_GRP + 6 * NSA_KV_W + 3 * NSA_HEADS
RET_COLS = 4 * D_GRP
DN_COLS = 4 * D_GRP + 2 * DN_HEADS
RW_COLS = 3 * D_GRP + RW_DECAY_LORA + RW_AAA_LORA + RW_GATE_LORA
P_TOTAL = NSA_COLS + RET_COLS + DN_COLS + RW_COLS

RW_LORA = RW_DECAY_LORA + RW_AAA_LORA + RW_GATE_LORA
RW_LORA_PAD = 384
RW_GROUPS_PER_BODY = 4
LANES = 128
SUB = 8

OFF_RW = 0
OFF_RET = 3 * D_GRP
OFF_DN = OFF_RET + 4 * D_GRP
OFF_NSA_Q = OFF_DN + 4 * D_GRP
OFF_NSA_KV = OFF_NSA_Q + D_GRP
OFF_LORA = OFF_NSA_KV + 6 * NSA_KV_W
OFF_SMALL = OFF_LORA + RW_LORA_PAD
SMALL_DN_A = 32
P_PAD = OFF_SMALL + LANES
assert P_PAD == 14336 and OFF_LORA % RW_LORA_PAD == 0

V7X_VMEM_BYTES = 64 * 1024 * 1024
VMEM_LIMIT = V7X_VMEM_BYTES * 7 // 8


def _split2(x):
    h = x.astype(bf16)
    return h, (x - h.astype(f32)).astype(bf16)


def _split3(x):
    h = x.astype(bf16)
    r1 = x - h.astype(f32)
    m = r1.astype(bf16)
    return h, m, (r1 - m.astype(f32)).astype(bf16)


def _rwkv_prep_body(x_ref, xprev_ref, l_ref, lprev_ref, mu_ref, mul_ref, w0_ref, a0_ref, kkw_ref, ka_ref,
                    w2_ref, a2_ref, g2_ref, e_ref, r_ref, w_ref, k_ref, v_ref, kk_ref, b_ref, g_ref, *, tm, S):
    i = pl.program_id(0)
    first = (i * tm) % S == 0

    def token_mix(x, prev8, mu):
        rows = lax.broadcasted_iota(jnp.int32, x.shape, 0)
        prev_row = jnp.where(first, 0.0, prev8[SUB - 1:SUB, :])
        x_prev = jnp.where(rows == 0, prev_row, pltpu.roll(x, 1, axis=0))
        return x + (x_prev - x) * mu

    x = token_mix(x_ref[...], xprev_ref[...], mu_ref[...])
    lora = token_mix(l_ref[...], lprev_ref[...], mul_ref[...])
    r = x[:, 0:D_GRP]
    k = x[:, D_GRP:2 * D_GRP]
    v = x[:, 2 * D_GRP:3 * D_GRP]
    dw = jnp.dot(jnp.tanh(lora).astype(bf16), w2_ref[...], preferred_element_type=f32)
    da = jnp.dot(lora.astype(bf16), a2_ref[...], preferred_element_type=f32)
    g = jnp.dot(jax.nn.sigmoid(lora).astype(bf16), g2_ref[...], preferred_element_type=f32)
    w = -jax.nn.softplus(-(w0_ref[...] + dw)) - 0.5
    decay = jnp.exp(-jnp.exp(w))
    a = jax.nn.sigmoid(a0_ref[...] + da)
    kq = k * kkw_ref[...]
    sq_h, sq_l = _split2(kq * kq)
    ss = (jnp.dot(sq_h, e_ref[...], preferred_element_type=f32)
          + jnp.dot(sq_l, e_ref[...], preferred_element_type=f32))
    kk = kq * lax.rsqrt(ss + L2NORM_EPS)
    r_ref[...] = r
    w_ref[...] = decay
    k_ref[...] = k * (1.0 + (a - 1.0) * ka_ref[...])
    v_ref[...] = v
    kk_ref[...] = kk
    b_ref[...] = kk * a
    g_ref[...] = g


def rwkv_prep(proj, rkv_blk, lora_blk, mu, w0, w2, a0, a2, g2, k_k, k_a, S, tm=256):
    M = proj.shape[0]
    assert M % tm == 0 and S % tm == 0
    W3 = 3 * D_GRP
    mu_x = mu[:W3].reshape(1, W3)
    mu_l = jnp.pad(mu[W3:], (0, RW_LORA_PAD - RW_LORA)).reshape(1, RW_LORA_PAD)
    w2p = jnp.zeros((RW_LORA_PAD, D_GRP), f32).at[0:RW_DECAY_LORA].set(w2).astype(bf16)
    a2p = jnp.zeros((RW_LORA_PAD, D_GRP), f32).at[RW_DECAY_LORA:RW_DECAY_LORA + RW_AAA_LORA].set(a2).astype(bf16)
    g2p = jnp.zeros((RW_LORA_PAD, D_GRP), f32).at[RW_DECAY_LORA + RW_AAA_LORA:RW_LORA].set(g2).astype(bf16)
    hid = np.arange(D_GRP) // RW_HEAD_DIM
    e = jnp.asarray(hid[:, None] == hid[None, :], bf16)
    row = lambda t: t.reshape(1, D_GRP)
    vec = pl.BlockSpec((1, D_GRP), lambda i: (0, 0))
    wspec = pl.BlockSpec((RW_LORA_PAD, D_GRP), lambda i: (0, 0))
    ospec = pl.BlockSpec((tm, D_GRP), lambda i: (i, 0))
    prev_row_blk = lambda i: jnp.maximum(i * (tm // SUB) - 1, 0)
    return pl.pallas_call(
        functools.partial(_rwkv_prep_body, tm=tm, S=S),
        grid=(M // tm,),
        in_specs=[pl.BlockSpec((tm, W3), lambda i: (i, rkv_blk)),
                  pl.BlockSpec((SUB, W3), lambda i: (prev_row_blk(i), rkv_blk)),
                  pl.BlockSpec((tm, RW_LORA_PAD), lambda i: (i, lora_blk)),
                  pl.BlockSpec((SUB, RW_LORA_PAD), lambda i: (prev_row_blk(i), lora_blk)),
                  pl.BlockSpec((1, W3), lambda i: (0, 0)),
                  pl.BlockSpec((1, RW_LORA_PAD), lambda i: (0, 0)),
                  vec, vec, vec, vec, wspec, wspec, wspec,
                  pl.BlockSpec((D_GRP, D_GRP), lambda i: (0, 0))],
        out_specs=[ospec] * 7,
        out_shape=[jax.ShapeDtypeStruct((M, D_GRP), f32)] * 7,
        compiler_params=pltpu.CompilerParams(dimension_semantics=("parallel",), vmem_limit_bytes=VMEM_LIMIT),
        name="rwkv_prep",
    )(proj, proj, proj, proj, mu_x, mu_l, row(w0), row(a0), row(k_k), row(k_a), w2p, a2p, g2p, e)


def _rwkv_scan_body(r_ref, w_ref, k_ref, v_ref, kk_ref, b_ref, g_ref, lng_ref, lnb_ref, rk_ref, o_ref, s_ref,
                    vc_ref, y_ref, *, G, T):
    tb = pl.program_id(2)

    @pl.when(tb == 0)
    def _():
        s_ref[...] = jnp.zeros_like(s_ref)

    Dh = RW_HEAD_DIM
    lo = lax.broadcasted_iota(jnp.int32, (Dh, LANES), 1) < Dh
    lo8 = lax.broadcasted_iota(jnp.int32, (SUB, LANES), 1) < Dh
    sub_iota = lax.broadcasted_iota(jnp.int32, (4 * SUB, LANES), 0)
    row_iota = lax.broadcasted_iota(jnp.int32, (SUB, LANES), 0)

    def seg_sum(x, m):
        s_lo = jnp.sum(jnp.where(m, x, 0.0), axis=-1, keepdims=True)
        s_hi = jnp.sum(jnp.where(m, 0.0, x), axis=-1, keepdims=True)
        return jnp.where(m, s_lo, s_hi)

    cols = [slice(p * LANES, (p + 1) * LANES) for p in range(G)]
    n_groups = T // SUB

    k_iota = lax.broadcasted_iota(jnp.int32, (8 * SUB, LANES), 0)
    k_head = k_iota // (4 * SUB) == lax.broadcasted_iota(jnp.int32, (8 * SUB, LANES), 1) // Dh
    k_piece = (k_iota // SUB) % 4

    def v_pieces(gi):
        t0 = pl.multiple_of(gi * SUB, SUB)
        vts = []
        for p in range(G):
            vh, vm, vl = _split3(v_ref[0, pl.ds(t0, SUB), cols[p]])
            vp = jnp.concatenate([vh, vm, vl, jnp.zeros_like(vh)], axis=0)
            vts.append(jnp.concatenate([vp[:, :Dh], vp[:, Dh:]], axis=0).T)
        return vts

    def v_columns(vts, i, slot):
        sel = (k_head & (k_piece < 3) & (k_iota % SUB == i)).astype(bf16)
        for p in range(G):
            vc_ref[slot, i, p] = jnp.dot(vts[p], sel, preferred_element_type=f32)

    vts0 = v_pieces(0)
    for i in range(SUB):
        v_columns(vts0, i, 0)

    def run_group(gi, slot, states):
        t0 = pl.multiple_of(gi * SUB, SUB)
        vts_next = v_pieces(jnp.minimum(gi + 1, n_groups - 1))
        r8b = [r_ref[0, pl.ds(t0, SUB), cols[p]].astype(bf16) for p in range(G)]
        yacc = [jnp.zeros((SUB, LANES), f32) for _ in range(G)]
        tiles = [[ref[0, pl.ds(t0, SUB), cols[p]] for ref in (kk_ref, w_ref, b_ref, k_ref)] for p in range(G)]
        for i in range(SUB):
            for p in range(G):
                kk_r, w_r, b_r, k_r = [jnp.broadcast_to(t8[i:i + 1, :], (Dh, LANES)) for t8 in tiles[p]]
                S = states[p]
                sab = -seg_sum(S * kk_r, lo)
                states[p] = S * w_r + sab * b_r + vc_ref[slot, i, p] * k_r
            for p in range(G):
                Sb = states[p].astype(bf16)
                zero = jnp.zeros_like(Sb)
                mt = jnp.concatenate([jnp.where(lo, Sb, zero), jnp.where(lo, zero, Sb)], axis=0)
                y8 = lax.dot_general(r8b[p], mt, (((1,), (1,)), ((), ())), preferred_element_type=f32)
                yacc[p] = jnp.where(row_iota == i, y8, yacc[p])
            v_columns(vts_next, i, 1 - slot)
        for p in range(G):
            y_ref[pl.ds(t0, SUB), cols[p]] = yacc[p]
        return states

    def group_batch(gj, carry):
        states = [s_ref[p] for p in range(G)]
        for u in range(RW_GROUPS_PER_BODY):
            states = run_group(RW_GROUPS_PER_BODY * gj + u, u % 2, states)
        for p in range(G):
            s_ref[p] = states[p]
        return carry

    assert n_groups % RW_GROUPS_PER_BODY == 0 and RW_GROUPS_PER_BODY % 2 == 0
    lax.fori_loop(0, n_groups // RW_GROUPS_PER_BODY, group_batch, 0)

    lo_t = lax.broadcasted_iota(jnp.int32, (T, LANES), 1) < Dh
    for p in range(G):
        y = y_ref[:, cols[p]]
        mean = seg_sum(y, lo_t) * (1.0 / Dh)
        d = y - mean
        var = seg_sum(d * d, lo_t) * (1.0 / Dh)
        yn = d * lax.rsqrt(var + RW_LN_EPS) * lng_ref[:, cols[p]] + lnb_ref[:, cols[p]]
        bonus = seg_sum(r_ref[0, :, cols[p]] * k_ref[0, :, cols[p]] * rk_ref[:, cols[p]], lo_t) * v_ref[0, :, cols[p]]
        o_ref[0, :, cols[p]] = (yn + bonus) * g_ref[0, :, cols[p]]


def rwkv_scan(r, w, k, v, kk, b, g, ln_g, ln_b, r_k, G=8, T=512):
    B, S, D = r.shape
    n_pairs = D // LANES
    assert n_pairs % G == 0 and S % T == 0
    spec = pl.BlockSpec((1, T, G * LANES), lambda bi, pi, ti: (bi, ti, pi))
    vec = pl.BlockSpec((1, G * LANES), lambda bi, pi, ti: (0, pi))
    row = lambda t: t.reshape(1, D)
    return pl.pallas_call(
        functools.partial(_rwkv_scan_body, G=G, T=T),
        grid=(B, n_pairs // G, S // T),
        in_specs=[spec] * 7 + [vec] * 3,
        out_specs=spec,
        out_shape=jax.ShapeDtypeStruct((B, S, D), f32),
        scratch_shapes=[pltpu.VMEM((G, RW_HEAD_DIM, LANES), f32),
                        pltpu.VMEM((2, SUB, G, RW_HEAD_DIM, LANES), f32),
                        pltpu.VMEM((T, G * LANES), f32)],
        compiler_params=pltpu.CompilerParams(dimension_semantics=("parallel", "parallel", "arbitrary"),
                                             vmem_limit_bytes=VMEM_LIMIT),
        name="rwkv_scan",
    )(r, w, k, v, kk, b, g, row(ln_g), row(ln_b), row(r_k))


def rwkv7_mixer(proj, rkv_blk, lora_blk, mu, w0, w2, a0, a2, g2, k_k, k_a, r_k, ln_g, ln_b, B, S):
    outs = rwkv_prep(proj, rkv_blk, lora_blk, mu, w0, w2, a0, a2, g2, k_k, k_a, S)
    r, w, k, v, kk, b, g = [t.reshape(B, S, D_GRP) for t in outs]
    return rwkv_scan(r, w, k, v, kk, b, g, ln_g, ln_b, r_k.reshape(-1))


def _mm_nt_body(a_ref, bt_ref, o_ref, acc_ref):
    k = pl.program_id(2)

    @pl.when(k == 0)
    def _():
        acc_ref[...] = jnp.zeros_like(acc_ref)

    acc_ref[...] += lax.dot_general(a_ref[...], bt_ref[...], (((1,), (1,)), ((), ())), preferred_element_type=f32)

    @pl.when(k == pl.num_programs(2) - 1)
    def _():
        o_ref[...] = acc_ref[...].astype(o_ref.dtype)


def matmul_nt(a, bt, layer, out_dtype=f32, tm=1024, tn=1024, tk=2048, name="matmul_nt"):
    M, K = a.shape
    _, N, _ = bt.shape
    assert M % tm == 0 and N % tn == 0 and K % tk == 0, (M, N, K, tm, tn, tk)
    return pl.pallas_call(
        _mm_nt_body,
        grid=(M // tm, N // tn, K // tk),
        in_specs=[pl.BlockSpec((tm, tk), lambda i, j, k: (i, k)),
                  pl.BlockSpec((None, tn, tk), lambda i, j, k: (layer, j, k))],
        out_specs=pl.BlockSpec((tm, tn), lambda i, j, k: (i, j)),
        out_shape=jax.ShapeDtypeStruct((M, N), out_dtype),
        scratch_shapes=[pltpu.VMEM((tm, tn), f32)],
        compiler_params=pltpu.CompilerParams(
            dimension_semantics=("parallel", "parallel", "arbitrary"), vmem_limit_bytes=VMEM_LIMIT),
        name=name,
    )(a, bt)


def _rms_norm_body(x_ref, g_ref, o_ref):
    x = x_ref[...]
    y = x * lax.rsqrt(jnp.mean(x * x, axis=-1, keepdims=True) + NORM_EPS)
    o_ref[...] = (y * g_ref[...]).astype(o_ref.dtype)


def rms_norm_rows(x, g, out_dtype, tm=256):
    M, D = x.shape
    return pl.pallas_call(
        _rms_norm_body,
        grid=(M // tm,),
        in_specs=[pl.BlockSpec((tm, D), lambda i: (i, 0)), pl.BlockSpec((1, D), lambda i: (0, 0))],
        out_specs=pl.BlockSpec((tm, D), lambda i: (i, 0)),
        out_shape=jax.ShapeDtypeStruct((M, D), out_dtype),
        compiler_params=pltpu.CompilerParams(dimension_semantics=("parallel",), vmem_limit_bytes=VMEM_LIMIT),
        name="rms_norm",
    )(x, g.reshape(1, D).astype(f32))


def _out_proj_body(a0_ref, a1_ref, a2_ref, a3_ref, b_ref, h_ref, o_ref, acc_ref):
    k = pl.program_id(2)

    @pl.when(k == 0)
    def _():
        acc_ref[...] = h_ref[...]

    for j, a_ref in enumerate((a0_ref, a1_ref, a2_ref, a3_ref)):
        @pl.when(k == j)
        def _(a_ref=a_ref):
            acc_ref[...] += jnp.dot(a_ref[...].astype(bf16), b_ref[...], preferred_element_type=f32)

    @pl.when(k == pl.num_programs(2) - 1)
    def _():
        o_ref[...] = acc_ref[...]


def out_proj_residual(parts, w, h, tm=1024, tn=1024):
    M, D = h.shape
    aspec = pl.BlockSpec((tm, D_GRP), lambda i, j, k: (i, 0))
    return pl.pallas_call(
        _out_proj_body,
        grid=(M // tm, D // tn, len(parts)),
        in_specs=[aspec] * 4 + [pl.BlockSpec((D_GRP, tn), lambda i, j, k: (k, j)),
                                pl.BlockSpec((tm, tn), lambda i, j, k: (i, j))],
        out_specs=pl.BlockSpec((tm, tn), lambda i, j, k: (i, j)),
        out_shape=jax.ShapeDtypeStruct((M, D), f32),
        scratch_shapes=[pltpu.VMEM((tm, tn), f32)],
        compiler_params=pltpu.CompilerParams(
            dimension_semantics=("parallel", "parallel", "arbitrary"), vmem_limit_bytes=VMEM_LIMIT),
        name="out_proj",
    )(*parts, w, h)


def _ple_body(a_ref, b_ref, p_ref, pw_ref, h_ref, o_ref, acc_ref):
    k = pl.program_id(2)

    @pl.when(k == 0)
    def _():
        acc_ref[...] = jnp.zeros_like(acc_ref)

    acc_ref[...] += jnp.dot(a_ref[...], b_ref[...], preferred_element_type=f32)

    @pl.when(k == pl.num_programs(2) - 1)
    def _():
        emb = jnp.dot(p_ref[...].astype(bf16), pw_ref[...], preferred_element_type=f32)
        o_ref[...] = h_ref[...] + emb * jax.nn.sigmoid(acc_ref[...])


def ple_residual(hn, gate_w, p, ple_w, h, tm=1024, tn=1024, tk=2048):
    M, D = h.shape
    return pl.pallas_call(
        _ple_body,
        grid=(M // tm, D // tn, D // tk),
        in_specs=[pl.BlockSpec((tm, tk), lambda i, j, k: (i, k)),
                  pl.BlockSpec((tk, tn), lambda i, j, k: (k, j)),
                  pl.BlockSpec((tm, PLE_DIM), lambda i, j, k: (i, 0)),
                  pl.BlockSpec((PLE_DIM, tn), lambda i, j, k: (0, j)),
                  pl.BlockSpec((tm, tn), lambda i, j, k: (i, j))],
        out_specs=pl.BlockSpec((tm, tn), lambda i, j, k: (i, j)),
        out_shape=jax.ShapeDtypeStruct((M, D), f32),
        scratch_shapes=[pltpu.VMEM((tm, tn), f32)],
        compiler_params=pltpu.CompilerParams(
            dimension_semantics=("parallel", "parallel", "arbitrary"), vmem_limit_bytes=VMEM_LIMIT),
        name="ple",
    )(hn, gate_w, p, ple_w, h)


def _moe_router_body(x_ref, w_ref, b_ref, id_ref, wt_ref):
    logits = jnp.dot(x_ref[...].astype(bf16), w_ref[...], preferred_element_type=f32) + b_ref[...]
    lane = lax.broadcasted_iota(jnp.int32, logits.shape, 1).astype(f32)
    ninf = -jnp.inf

    def top1(vals):
        m = jnp.max(vals, axis=-1, keepdims=True)
        idx = jnp.min(jnp.where(vals == m, lane, float(LANES)), axis=-1, keepdims=True)
        return m, idx

    gl = jnp.where(lane < N_GROUPS, logits, ninf)
    g_max, g_idx = top1(gl)
    g_w = 1.0 / jnp.sum(jnp.exp(gl - g_max), axis=-1, keepdims=True)
    lo = N_GROUPS + g_idx * EXPERTS_PER_GROUP
    el = jnp.where((lane >= lo) & (lane < lo + EXPERTS_PER_GROUP), logits, ninf)
    e_max, _ = top1(el)
    pe = jnp.exp(el - e_max)
    p = pe / jnp.sum(pe, axis=-1, keepdims=True)
    p = jnp.where(el == ninf, ninf, p)
    p1, i1 = top1(p)
    p2, i2 = top1(jnp.where(lane == i1, ninf, p))
    scale = g_w / (p1 + p2)
    id_ref[...] = jnp.where(lane == 0, i1 - N_GROUPS, jnp.where(lane == 1, i2 - N_GROUPS, 0.0)).astype(jnp.int32)
    wt_ref[...] = jnp.where(lane == 0, p1 * scale, jnp.where(lane == 1, p2 * scale, 0.0))


def moe_router(xb, router_grp, router_grp_b, router_exp, router_exp_b, tm=512):
    M, D = xb.shape
    n_log = N_GROUPS + N_EXPERTS
    w = jnp.pad(jnp.concatenate([router_grp, router_exp], axis=1), ((0, 0), (0, LANES - n_log))).astype(bf16)
    b = jnp.pad(jnp.concatenate([router_grp_b, router_exp_b]), (0, LANES - n_log)).reshape(1, LANES).astype(f32)
    ids, wts = pl.pallas_call(
        _moe_router_body,
        grid=(M // tm,),
        in_specs=[pl.BlockSpec((tm, D), lambda i: (i, 0)), pl.BlockSpec((D, LANES), lambda i: (0, 0)),
                  pl.BlockSpec((1, LANES), lambda i: (0, 0))],
        out_specs=[pl.BlockSpec((tm, LANES), lambda i: (i, 0))] * 2,
        out_shape=[jax.ShapeDtypeStruct((M, LANES), jnp.int32), jax.ShapeDtypeStruct((M, LANES), f32)],
        compiler_params=pltpu.CompilerParams(dimension_semantics=("parallel",), vmem_limit_bytes=VMEM_LIMIT),
        name="moe_router",
    )(xb, w, b)
    return ids[:, :TOP_K], wts[:, :TOP_K]


def _moe_up_body(be_ref, nu_ref, x_ref, wg_ref, wu_ref, h_ref, *, tf):
    i = pl.program_id(0)

    @pl.when(i < nu_ref[0])
    def _():
        x = x_ref[...].astype(bf16)
        for f in range(D_EXPERT // tf):
            c = slice(f * tf, (f + 1) * tf)
            hg = jnp.dot(x, wg_ref[0, 0, :, c].astype(bf16), preferred_element_type=f32)
            hu = jnp.dot(x, wu_ref[0, 0, :, c].astype(bf16), preferred_element_type=f32)
            h_ref[:, c] = (hg * jax.nn.sigmoid(hg) * hu).astype(bf16)

    @pl.when(i >= nu_ref[0])
    def _():
        h_ref[...] = jnp.zeros_like(h_ref)


def _moe_down_body(be_ref, nu_ref, h_ref, wd_ref, wt_ref, y_ref):
    i = pl.program_id(0)

    @pl.when(i < nu_ref[0])
    def _():
        y_ref[...] = jnp.dot(h_ref[...], wd_ref[0, 0].astype(bf16), preferred_element_type=f32) * wt_ref[...]

    @pl.when(i >= nu_ref[0])
    def _():
        y_ref[...] = jnp.zeros_like(y_ref)


def moe_experts(x_pad, w_pad, blk_expert, n_used, wg, wu, wd, layer, tf=256):
    A_pad, D = x_pad.shape
    n_blk = A_pad // ROW_BLOCK
    wmap = lambda i, be, nu: (layer, be[i], 0, 0)
    wspec = pl.BlockSpec((1, 1, D, D_EXPERT), wmap, pipeline_mode=pl.Buffered(1))
    wspec2 = pl.BlockSpec((1, 1, D, D_EXPERT), wmap)
    h = pl.pallas_call(
        functools.partial(_moe_up_body, tf=tf),
        grid_spec=pltpu.PrefetchScalarGridSpec(
            num_scalar_prefetch=2,
            grid=(n_blk,),
            in_specs=[pl.BlockSpec((ROW_BLOCK, D), lambda i, be, nu: (i, 0)), wspec2, wspec],
            out_specs=pl.BlockSpec((ROW_BLOCK, D_EXPERT), lambda i, be, nu: (i, 0)),
        ),
        out_shape=jax.ShapeDtypeStruct((A_pad, D_EXPERT), bf16),
        compiler_params=pltpu.CompilerParams(dimension_semantics=("arbitrary",), vmem_limit_bytes=VMEM_LIMIT),
        name="moe_up",
    )(blk_expert, n_used, x_pad, wg, wu)
    return pl.pallas_call(
        _moe_down_body,
        grid_spec=pltpu.PrefetchScalarGridSpec(
            num_scalar_prefetch=2,
            grid=(n_blk,),
            in_specs=[pl.BlockSpec((ROW_BLOCK, D_EXPERT), lambda i, be, nu: (i, 0)),
                      pl.BlockSpec((1, 1, D_EXPERT, D), lambda i, be, nu: (layer, be[i], 0, 0)),
                      pl.BlockSpec((ROW_BLOCK, 1), lambda i, be, nu: (i, 0))],
            out_specs=pl.BlockSpec((ROW_BLOCK, D), lambda i, be, nu: (i, 0)),
        ),
        out_shape=jax.ShapeDtypeStruct((A_pad, D), f32),
        compiler_params=pltpu.CompilerParams(dimension_semantics=("arbitrary",), vmem_limit_bytes=VMEM_LIMIT),
        name="moe_down",
    )(blk_expert, n_used, h, wd, w_pad.reshape(A_pad, 1))


def hier_moe(xb, router_grp, router_grp_b, router_exp, router_exp_b, w_gate, w_up, w_down, layer):
    M, D = xb.shape
    expert_ids, weights = moe_router(xb, router_grp, router_grp_b, router_exp, router_exp_b)
    A = M * TOP_K
    e_flat = expert_ids.reshape(A)
    onehot = (e_flat[:, None] == jnp.arange(N_EXPERTS)[None, :]).astype(jnp.int32)
    rank = jnp.take_along_axis(jnp.cumsum(onehot, axis=0), e_flat[:, None], axis=1)[:, 0] - 1
    counts = jnp.sum(onehot, axis=0)
    padded = (counts + ROW_BLOCK - 1) // ROW_BLOCK * ROW_BLOCK
    end_pad = jnp.cumsum(padded)
    dest = (end_pad - padded)[e_flat] + rank
    n_blk = -(-A // ROW_BLOCK) + N_EXPERTS
    A_pad = n_blk * ROW_BLOCK
    tok_pad = (jnp.arange(A_pad, dtype=jnp.int32) % M).at[dest].set(jnp.arange(A, dtype=jnp.int32) // TOP_K)
    w_pad = jnp.zeros((A_pad,), f32).at[dest].set(weights.reshape(A))
    blk_expert = jnp.minimum(jnp.searchsorted(end_pad, jnp.arange(n_blk) * ROW_BLOCK, side='right'),
                             N_EXPERTS - 1).astype(jnp.int32)
    n_used = (end_pad[-1:] // ROW_BLOCK).astype(jnp.int32)
    x_pad = xb[tok_pad]
    y_pad = moe_experts(x_pad, w_pad, blk_expert, n_used, w_gate, w_up, w_down, layer)
    rows = dest.reshape(M, TOP_K)
    return y_pad[rows[:, 0]] + y_pad[rows[:, 1]]


def _rope_tables(S, theta, rot_dims, width):
    half = rot_dims // 2
    freqs = theta ** (-np.arange(half, dtype=np.float64) / half)
    ang = np.arange(S, dtype=np.float64)[:, None] * freqs[None, :]
    cos = np.ones((S, width), np.float64)
    sin_lo = np.zeros((S, width), np.float64)
    sin_hi = np.zeros((S, width), np.float64)
    cos[:, :half] = np.cos(ang)
    cos[:, half:rot_dims] = np.cos(ang)
    sin_lo[:, :half] = -np.sin(ang)
    sin_hi[:, half:rot_dims] = np.sin(ang)
    return [jnp.asarray(t, f32) for t in (cos, sin_lo, sin_hi)]


def _rope128(x, cos, sin_lo, sin_hi, half):
    return x * cos + pltpu.roll(x, LANES - half, axis=1) * sin_lo + pltpu.roll(x, half, axis=1) * sin_hi


def _nsa_prep_body(q_ref, kv_ref, cos_ref, sl_ref, sh_ref, qo_ref, kvo_ref, co_ref):
    cos, sl, sh = cos_ref[...], sl_ref[...], sh_ref[...]
    half = ROPE_DIMS // 2
    scale = HEAD_DIM ** -0.5
    for h in range(NSA_HEADS):
        c = slice(h * LANES, (h + 1) * LANES)
        qo_ref[:, c] = (_rope128(q_ref[:, c], cos, sl, sh, half) * scale).astype(bf16)
    for j in range(6):
        for g in range(NSA_KV_HEADS):
            c = slice((j * NSA_KV_HEADS + g) * LANES, (j * NSA_KV_HEADS + g + 1) * LANES)
            x = kv_ref[:, c]
            if j % 2 == 0:
                x = _rope128(x, cos, sl, sh, half)
            if j < 2:
                co_ref[:, c] = x.astype(bf16)
            else:
                kvo_ref[:, slice(c.start - 4 * LANES, c.stop - 4 * LANES)] = x.astype(bf16)


def nsa_prep(proj, S, tm=512):
    M = proj.shape[0]
    assert M % tm == 0 and S % tm == 0
    tabs = _rope_tables(S, ROPE_THETA, ROPE_DIMS, LANES)
    tspec = pl.BlockSpec((tm, LANES), lambda i: (i % (S // tm), 0))
    return pl.pallas_call(
        _nsa_prep_body,
        grid=(M // tm,),
        in_specs=[pl.BlockSpec((tm, D_GRP), lambda i: (i, OFF_NSA_Q // D_GRP)),
                  pl.BlockSpec((tm, 6 * NSA_KV_W), lambda i: (i, OFF_NSA_KV // (6 * NSA_KV_W))),
                  tspec, tspec, tspec],
        out_specs=[pl.BlockSpec((tm, D_GRP), lambda i: (i, 0)),
                   pl.BlockSpec((tm, 4 * NSA_KV_W), lambda i: (i, 0)),
                   pl.BlockSpec((tm, 2 * NSA_KV_W), lambda i: (i, 0))],
        out_shape=[jax.ShapeDtypeStruct((M, D_GRP), bf16), jax.ShapeDtypeStruct((M, 4 * NSA_KV_W), bf16),
                   jax.ShapeDtypeStruct((M, 2 * NSA_KV_W), bf16)],
        compiler_params=pltpu.CompilerParams(dimension_semantics=("parallel",), vmem_limit_bytes=VMEM_LIMIT),
        name="nsa_prep",
    )(proj, proj, *tabs)


def _nsa_compress_body(r_ref, w1_ref, w2_ref, pe_ref, o_ref):
    r = r_ref[0, 0, 0]
    half_w = CMP_STRIDE * HEAD_DIM
    a = jnp.dot(r, w1_ref[0, :half_w, :], preferred_element_type=f32)
    b = jnp.dot(r, w1_ref[0, half_w:, :], preferred_element_type=f32)
    n = r.shape[0]
    pe = jnp.dot(jnp.broadcast_to(pe_ref[0], (SUB, pe_ref.shape[-1])), w1_ref[0], preferred_element_type=f32)[0:1]
    pre = a + pltpu.roll(b, n - 1, axis=0) + pe
    hid = pre * jax.nn.sigmoid(pre)
    o_ref[0, 0, 0] = jnp.dot(hid.astype(bf16), w2_ref[0], preferred_element_type=f32).astype(bf16)


def nsa_compress(c_rot, w1s, w2s, pes, B, S):
    G, Dh = NSA_KV_HEADS, HEAD_DIM
    nc = S // CMP_STRIDE
    r = c_rot.reshape(B, nc, CMP_STRIDE, 2, G, Dh).transpose(3, 0, 4, 1, 2, 5).reshape(2, B, G, nc, CMP_STRIDE * Dh)
    w1 = jnp.stack(w1s).astype(bf16)
    w2 = jnp.stack(w2s).astype(bf16)
    pe = jnp.stack(pes).reshape(2, 1, CMP_LEN * Dh).astype(bf16)
    return pl.pallas_call(
        _nsa_compress_body,
        grid=(2, B, G),
        in_specs=[pl.BlockSpec((1, 1, 1, nc, CMP_STRIDE * Dh), lambda j, b, g: (j, b, g, 0, 0)),
                  pl.BlockSpec((1, CMP_LEN * Dh, Dh), lambda j, b, g: (j, 0, 0)),
                  pl.BlockSpec((1, Dh, Dh), lambda j, b, g: (j, 0, 0)),
                  pl.BlockSpec((1, 1, CMP_LEN * Dh), lambda j, b, g: (j, 0, 0))],
        out_specs=pl.BlockSpec((1, 1, 1, nc, Dh), lambda j, b, g: (j, b, g, 0, 0)),
        out_shape=jax.ShapeDtypeStruct((2, B, G, nc, Dh), bf16),
        compiler_params=pltpu.CompilerParams(dimension_semantics=("parallel", "parallel", "parallel"),
                                             vmem_limit_bytes=VMEM_LIMIT),
        name="nsa_compress",
    )(r, w1, w2, pe)


def _softmax_rows(s, mask):
    s = jnp.where(mask, s, -jnp.inf)
    m = jnp.max(s, axis=-1, keepdims=True)
    m = jnp.where(m == -jnp.inf, 0.0, m)
    e = jnp.exp(s - m)
    return e, jnp.maximum(jnp.sum(e, axis=-1, keepdims=True), jnp.finfo(f32).tiny)


def _nsa_attn_body(q_ref, cmp_ref, ks_ref, vs_ref, kw_ref, vw_ref, gate_ref, ovt_ref, o_ref, *, tq, tk, S):
    g = pl.program_id(1)
    qi = pl.program_id(2)
    R, Dh = NSA_GROUP, HEAD_DIM
    rows = R * tq
    n_blk = S // SEL_LEN
    n_cmp = S // CMP_STRIDE - 1
    q0 = qi * tq
    q4 = jnp.concatenate([q_ref[0, :, r * Dh:(r + 1) * Dh] for r in range(R)], axis=0)
    nt = (((1,), (1,)), ((), ()))

    def qpos(shape):
        return q0 + lax.broadcasted_iota(jnp.int32, shape, 0) % tq

    k_cmp, v_cmp = cmp_ref[0, 0, 0], cmp_ref[1, 0, 0]
    ncp = k_cmp.shape[0]
    s = lax.dot_general(q4, k_cmp, nt, preferred_element_type=f32)
    n_idx = lax.broadcasted_iota(jnp.int32, (rows, ncp), 1)
    mask = (n_idx * CMP_STRIDE + (CMP_LEN - 1) <= qpos((rows, ncp))) & (n_idx < n_cmp)
    e, den = _softmax_rows(s, mask)
    p = e / den
    o_cmp = jnp.dot(p.astype(bf16), v_cmp, preferred_element_type=f32)

    psum = p[0:tq]
    for r in range(1, R):
        psum = psum + p[r * tq:(r + 1) * tq]
    ph, plo = _split2(psum)
    imp = (lax.dot_general(ovt_ref[...], ph, nt, preferred_element_type=f32)
           + lax.dot_general(ovt_ref[...], plo, nt, preferred_element_type=f32))
    blk = lax.broadcasted_iota(jnp.int32, (n_blk, tq), 0)
    cur = (q0 + lax.broadcasted_iota(jnp.int32, (n_blk, tq), 1)) // SEL_LEN
    visible = blk <= cur
    forced = (blk == 0) | (visible & (blk > cur - N_LOCAL_FORCED))
    imp = jnp.where(forced, FORCED_SCORE, jnp.where(visible, imp, -1.0))
    sel_t = jnp.zeros((n_blk, tq), f32)
    blk_f = blk.astype(f32)
    for _ in range(min(N_SEL, n_blk)):
        m = jnp.max(imp, axis=0, keepdims=True)
        first = jnp.min(jnp.where(imp == m, blk_f, float(n_blk)), axis=0, keepdims=True)
        hit = blk_f == first
        sel_t = jnp.where(hit, 1.0, sel_t)
        imp = jnp.where(hit, -jnp.inf, imp)
    sel_q = jnp.concatenate([sel_t.T, jnp.ones((tq, LANES - n_blk), f32)], axis=1).astype(bf16)

    blocks_per_tile = tk // SEL_LEN

    big = float(2 ** 30)
    kb = sel_q.shape[1]

    def sel_step(kt, carry, causal):
        m_run, l_run, acc = carry
        k0 = pl.multiple_of(kt * tk, tk)
        kt_k = ks_ref[0, pl.ds(k0, tk), :]
        kt_v = vs_ref[0, pl.ds(k0, tk), :]
        s = lax.dot_general(q4, kt_k, nt, preferred_element_type=f32)
        row = lax.broadcasted_iota(jnp.int32, (kb, tk), 0)
        in_blk = row == kt * blocks_per_tile + lax.broadcasted_iota(jnp.int32, (kb, tk), 1) // SEL_LEN
        expand = jnp.where(in_blk, big, jnp.where(row == n_blk, -big, 0.0)).astype(bf16)
        bias = jnp.dot(sel_q, expand, preferred_element_type=f32)
        s = (s.reshape(R, tq, tk) + bias[None]).reshape(rows, tk)
        if causal:
            kpos = k0 + lax.broadcasted_iota(jnp.int32, (rows, tk), 1)
            s = jnp.where(kpos <= qpos((rows, tk)), s, -big)
        m_new = jnp.maximum(m_run, jnp.max(s, axis=-1, keepdims=True))
        alpha = jnp.exp(m_run - m_new)
        e = jnp.exp(s - m_new)
        l_new = alpha * l_run + jnp.sum(e, axis=-1, keepdims=True)
        acc = alpha * acc + jnp.dot(e.astype(bf16), kt_v, preferred_element_type=f32)
        return m_new, l_new, acc

    last_kt = (q0 + tq - 1) // tk
    init = (jnp.full((rows, 1), -jnp.inf, f32), jnp.zeros((rows, 1), f32), jnp.zeros((rows, Dh), f32))
    carry = lax.fori_loop(0, last_kt, functools.partial(sel_step, causal=False), init)
    _, l_sel, acc_sel = sel_step(last_kt, carry, causal=True)
    o_sel = acc_sel / l_sel

    span = WINDOW + tq
    w0 = pl.multiple_of(jnp.maximum(q0 - WINDOW, 0), tq)
    kwin = kw_ref[0, pl.ds(w0, span), :]
    vwin = vw_ref[0, pl.ds(w0, span), :]
    s = lax.dot_general(q4, kwin, nt, preferred_element_type=f32)
    kpos = w0 + lax.broadcasted_iota(jnp.int32, (rows, span), 1)
    qp = qpos((rows, span))
    e, den = _softmax_rows(s, (kpos <= qp) & (kpos > qp - WINDOW))
    o_win = jnp.dot(e.astype(bf16), vwin, preferred_element_type=f32) / den

    gate = jax.nn.sigmoid(gate_ref[0])
    for r in range(R):
        c0 = (g * R + r) * 3
        lane = lax.broadcasted_iota(jnp.int32, (tq, LANES), 1)
        gsel = lambda c: jnp.sum(jnp.where(lane == c0 + c, gate, 0.0), axis=-1, keepdims=True)
        rs = slice(r * tq, (r + 1) * tq)
        o_ref[0, :, r * Dh:(r + 1) * Dh] = (gsel(0) * o_cmp[rs] + gsel(1) * o_sel[rs]
                                            + gsel(2) * o_win[rs]).astype(o_ref.dtype)


def nsa_attention(q_rot, kv_rot, cmp_kv, proj3, B, S, tq=256, tk=512):
    G, R, Dh = NSA_KV_HEADS, NSA_GROUP, HEAD_DIM
    n_blk = S // SEL_LEN
    assert S % tk == 0 and tk % tq == 0 and S >= WINDOW + tq and n_blk % SUB == 0 and n_blk < LANES
    nc = S // CMP_STRIDE
    c0 = np.arange(nc - 1)[None, :] * CMP_STRIDE
    s0 = np.arange(n_blk)[:, None] * SEL_LEN
    ovt = np.zeros((n_blk, nc), np.float32)
    ovt[:, :nc - 1] = np.clip(np.minimum(c0 + CMP_LEN, s0 + SEL_LEN) - np.maximum(c0, s0), 0, None) / CMP_LEN
    kvspec = lambda j: pl.BlockSpec((1, S, Dh), lambda b, g, i: (b, 0, j * G + g))
    return pl.pallas_call(
        functools.partial(_nsa_attn_body, tq=tq, tk=tk, S=S),
        grid=(B, G, S // tq),
        in_specs=[pl.BlockSpec((1, tq, R * Dh), lambda b, g, i: (b, i, g)),
                  pl.BlockSpec((2, 1, 1, nc, Dh), lambda b, g, i: (0, b, g, 0, 0)),
                  kvspec(0), kvspec(1), kvspec(2), kvspec(3),
                  pl.BlockSpec((1, tq, LANES), lambda b, g, i: (b, i, OFF_SMALL // LANES)),
                  pl.BlockSpec((n_blk, nc), lambda b, g, i: (0, 0))],
        out_specs=pl.BlockSpec((1, tq, R * Dh), lambda b, g, i: (b, i, g)),
        out_shape=jax.ShapeDtypeStruct((B, S, D_GRP), bf16),
        compiler_params=pltpu.CompilerParams(dimension_semantics=("parallel", "parallel", "arbitrary"),
                                             vmem_limit_bytes=VMEM_LIMIT),
        name="nsa_attention",
    )(q_rot, cmp_kv, kv_rot, kv_rot, kv_rot, kv_rot, proj3, jnp.asarray(ovt, bf16))


def nsa_mixer_pallas(proj, ck_w1, ck_w2, ck_pe, cv_w1, cv_w2, cv_pe, B, S):
    q_rot, kv_rot, c_rot = nsa_prep(proj, S)
    cmp_kv = nsa_compress(c_rot, (ck_w1, cv_w1), (ck_w2, cv_w2), (ck_pe, cv_pe), B, S)
    return nsa_attention(q_rot.reshape(B, S, D_GRP), kv_rot.reshape(B, S, 4 * NSA_KV_W), cmp_kv,
                         proj.reshape(B, S, P_PAD), B, S)


def _retention_body(q_ref, k_ref, v_ref, g_ref, cos_ref, sin_ref, dec_ref, xi_ref, zeta_ref, gc_ref, o_ref, st_ref,
                    *, T, HS):
    @pl.when(pl.program_id(2) == 0)
    def _():
        st_ref[...] = jnp.zeros_like(st_ref)

    C, Dh = RET_CHUNK, HEAD_DIM
    nt = (((1,), (1,)), ((), ()))
    pre = {}
    for hh in range(HS):
        hc = slice(hh * Dh, (hh + 1) * Dh)
        for c in range(T // C):
            rs = slice(c * C, (c + 1) * C)
            cos, sin = cos_ref[rs, :], sin_ref[rs, :]
            rope = lambda x: x * cos + pltpu.roll(x, Dh // 2, axis=1) * sin
            q = rope(q_ref[0, rs, hc])
            k = rope(k_ref[0, rs, hc]) * Dh ** -0.5
            qb, vb = q.astype(bf16), v_ref[0, rs, hc].astype(bf16)
            inner = lax.dot_general(qb, k.astype(bf16), nt, preferred_element_type=f32) * dec_ref[hh]
            o_in = jnp.dot(inner.astype(bf16), vb, preferred_element_type=f32)
            kv = jnp.dot((k * zeta_ref[hh]).astype(bf16).T, vb, preferred_element_type=f32)
            pre[hh, c] = (qb, o_in, kv)
    states = [st_ref[hh] for hh in range(HS)]
    for c in range(T // C):
        rs = slice(c * C, (c + 1) * C)
        for hh in range(HS):
            hc = slice(hh * Dh, (hh + 1) * Dh)
            qb, o_in, kv = pre[hh, c]
            o = o_in + jnp.dot(qb, states[hh].astype(bf16), preferred_element_type=f32) * xi_ref[hh]
            states[hh] = states[hh] * gc_ref[hh, 0:1, :] + kv
            mu = jnp.mean(o, axis=-1, keepdims=True)
            d = o - mu
            o = d * lax.rsqrt(jnp.mean(d * d, axis=-1, keepdims=True) + NORM_EPS)
            gt = g_ref[0, rs, hc]
            o_ref[0, rs, hc] = (gt * jax.nn.sigmoid(gt) * o).astype(o_ref.dtype)
    for hh in range(HS):
        st_ref[hh] = states[hh]


def retention_mixer_pallas(proj3, B, S, T=512, HS=2):
    H, Dh, C = RET_HEADS, HEAD_DIM, RET_CHUNK
    W = HS * Dh
    assert S % T == 0 and T % C == 0 and H % HS == 0
    freqs = RET_THETA ** (-np.linspace(0.0, 1.0, Dh // 2))
    ang = np.arange(S, dtype=np.float64)[:, None] * freqs[None, :]
    cos = jnp.asarray(np.concatenate([np.cos(ang), np.cos(ang)], 1), f32)
    sin = jnp.asarray(np.concatenate([-np.sin(ang), np.sin(ang)], 1), f32)
    log_gamma = np.log1p(-(2.0 ** (-5.0 - np.arange(H, dtype=np.float64))))
    n = np.arange(C, dtype=np.float64)
    diff = n[:, None] - n[None, :]
    decay_in = np.where(diff >= 0, np.exp(log_gamma[:, None, None] * np.maximum(diff, 0.0)), 0.0)
    xi = np.exp(log_gamma[:, None] * (n + 1.0))[:, :, None] * np.ones((1, 1, Dh))
    zeta = np.exp(log_gamma[:, None] * (C - 1.0 - n))[:, :, None] * np.ones((1, 1, Dh))
    gamma_c = np.exp(log_gamma * C)[:, None, None] * np.ones((1, SUB, Dh))
    blk = lambda j: pl.BlockSpec((1, T, W), lambda b, h, t: (b, t, (OFF_RET + j * D_GRP) // W + h))
    tab = pl.BlockSpec((T, Dh), lambda b, h, t: (t, 0))
    per_head = lambda r: pl.BlockSpec((HS, r, Dh), lambda b, h, t: (h, 0, 0))
    return pl.pallas_call(
        functools.partial(_retention_body, T=T, HS=HS),
        grid=(B, H // HS, S // T),
        in_specs=[blk(0), blk(1), blk(2), blk(3), tab, tab, per_head(C), per_head(C), per_head(C), per_head(SUB)],
        out_specs=pl.BlockSpec((1, T, W), lambda b, h, t: (b, t, h)),
        out_shape=jax.ShapeDtypeStruct((B, S, D_GRP), bf16),
        scratch_shapes=[pltpu.VMEM((HS, Dh, Dh), f32)],
        compiler_params=pltpu.CompilerParams(dimension_semantics=("parallel", "parallel", "arbitrary"),
                                             vmem_limit_bytes=VMEM_LIMIT),
        name="retention",
    )(proj3, proj3, proj3, proj3, cos, sin, jnp.asarray(decay_in, f32), jnp.asarray(xi, f32),
      jnp.asarray(zeta, f32), jnp.asarray(gamma_c, f32))


def _dot3(a, b):
    ah, al = _split2(a)
    bh, bl = _split2(b)
    return (jnp.dot(ah, bh, preferred_element_type=f32) + jnp.dot(ah, bl, preferred_element_type=f32)
            + jnp.dot(al, bh, preferred_element_type=f32))


def _deltanet_body(q_ref, k_ref, v_ref, z_ref, qp_ref, kp_ref, vp_ref, ab_ref, cw_ref, alog_ref, dtb_ref, ng_ref,
                   o_ref, st_ref, *, T, HS):
    hb = pl.program_id(1)
    first = pl.program_id(2) == 0

    @pl.when(first)
    def _():
        st_ref[...] = jnp.zeros_like(st_ref)

    C, Dh = DN_CHUNK, HEAD_DIM
    nt = (((1,), (1,)), ((), ()))
    n_chunks = T // C
    ab = ab_ref[0]
    lane = lax.broadcasted_iota(jnp.int32, ab.shape, 1)

    def conv_silu(x_ref, prev_ref, j, hc):
        prev = jnp.where(first, 0.0, prev_ref[0, :, hc])
        xcat = jnp.concatenate([prev, x_ref[0, :, hc]], axis=0)
        w = cw_ref[:, j, hc]
        y = sum(w[i:i + 1, :] * xcat[SUB - (CONV_WIDTH - 1) + i:SUB - (CONV_WIDTH - 1) + i + T, :]
                for i in range(CONV_WIDTH))
        return y * jax.nn.sigmoid(y)

    def l2n(x):
        return x * lax.rsqrt(jnp.sum(x * x, axis=-1, keepdims=True) + L2NORM_EPS)

    units = [(hh, slice(c * C, (c + 1) * C)) for hh in range(HS) for c in range(n_chunks)]
    qs, ks_, vs_, betas, g_rows = [], [], [], [], []
    for hh in range(HS):
        hc = slice(hh * Dh, (hh + 1) * Dh)
        head = hb * HS + hh
        q_all = l2n(conv_silu(q_ref, qp_ref, 0, hc)) * Dh ** -0.5
        k_all = l2n(conv_silu(k_ref, kp_ref, 1, hc))
        v_all = conv_silu(v_ref, vp_ref, 2, hc)
        a_col = jnp.sum(jnp.where(lane == SMALL_DN_A + head, ab, 0.0), axis=-1, keepdims=True)
        b_col = jnp.sum(jnp.where(lane == SMALL_DN_A + DN_HEADS + head, ab, 0.0), axis=-1, keepdims=True)
        beta_all = jax.nn.sigmoid(b_col)
        g_all = -jnp.exp(alog_ref[hh, 0:1, :]) * jax.nn.softplus(a_col + dtb_ref[hh, 0:1, :])
        for c in range(n_chunks):
            rs = slice(c * C, (c + 1) * C)
            qs.append(q_all[rs]); ks_.append(k_all[rs]); vs_.append(v_all[rs])
            betas.append(beta_all[rs]); g_rows.append(g_all[rs])

    ri = lax.broadcasted_iota(jnp.int32, (C, C), 0)
    ci = lax.broadcasted_iota(jnp.int32, (C, C), 1)
    tril, strict = ri >= ci, ri > ci
    ltri = tril.astype(bf16)
    eye = (ri == ci).astype(f32)
    gcs = []
    for g_rows_u in g_rows:
        gh, gm, gl = _split3(g_rows_u)
        gcs.append(jnp.dot(ltri, gh, preferred_element_type=f32) + jnp.dot(ltri, gm, preferred_element_type=f32)
                   + jnp.dot(ltri, gl, preferred_element_type=f32))
    decays = [jnp.exp(jnp.where(tril, gc[:, :C] - gc[:, :C].T, -jnp.inf)) for gc in gcs]
    kbs = [k.astype(bf16) for k in ks_]
    kbetas = [k * b for k, b in zip(ks_, betas)]
    ms = [jnp.where(strict, lax.dot_general(kbeta.astype(bf16), kb, nt, preferred_element_type=f32) * d, 0.0)
          for kbeta, kb, d in zip(kbetas, kbs, decays)]
    attns = [jnp.where(tril, lax.dot_general(q.astype(bf16), kb, nt, preferred_element_type=f32) * d, 0.0).astype(bf16)
             for q, kb, d in zip(qs, kbs, decays)]
    t_invs, pws = [eye - m for m in ms], ms
    for _ in range(5):
        pws = [_dot3(pw, pw) for pw in pws]
        t_invs = [_dot3(t, eye + pw) for t, pw in zip(t_invs, pws)]
    egcs = [jnp.exp(gc) for gc in gcs]
    tbs = [t.astype(bf16) for t in t_invs]
    us = [jnp.dot(tb, (v * b).astype(bf16), preferred_element_type=f32) for tb, v, b in zip(tbs, vs_, betas)]
    ws = [jnp.dot(tb, (kbeta * egc).astype(bf16), preferred_element_type=f32).astype(bf16)
          for tb, kbeta, egc in zip(tbs, kbetas, egcs)]
    states = [st_ref[hh] for hh in range(HS)]
    for c in range(n_chunks):
        for hh in range(HS):
            u, rs = hh * n_chunks + c, slice(c * C, (c + 1) * C)
            g_last = gcs[u][C - 1:C, :]
            q_dec = (qs[u] * egcs[u]).astype(bf16)
            k_dec = (ks_[u] * jnp.exp(g_last - gcs[u])).astype(bf16)
            sb = states[hh].astype(bf16)
            v_new = us[u] - jnp.dot(ws[u], sb, preferred_element_type=f32)
            vnb = v_new.astype(bf16)
            o = jnp.dot(q_dec, sb, preferred_element_type=f32) + jnp.dot(attns[u], vnb, preferred_element_type=f32)
            states[hh] = states[hh] * jnp.exp(g_last) + jnp.dot(k_dec.T, vnb, preferred_element_type=f32)
            o = o * lax.rsqrt(jnp.mean(o * o, axis=-1, keepdims=True) + NORM_EPS) * ng_ref[...]
            zt = z_ref[0, rs, hh * Dh:(hh + 1) * Dh]
            o_ref[0, rs, hh * Dh:(hh + 1) * Dh] = (o * (zt * jax.nn.sigmoid(zt))).astype(o_ref.dtype)
    for hh in range(HS):
        st_ref[hh] = states[hh]


def deltanet_mixer_pallas(proj3, conv_w, a_log, dt_bias, norm_g, B, S, T=256, HS=4):
    H, Dh = DN_HEADS, HEAD_DIM
    W = HS * Dh
    assert S % T == 0 and T % DN_CHUNK == 0 and H % HS == 0
    blk = lambda j: pl.BlockSpec((1, T, W), lambda b, h, t: (b, t, (OFF_DN + j * D_GRP) // W + h))
    prev = lambda j: pl.BlockSpec((1, SUB, W), lambda b, h, t: (b, jnp.maximum(t * (T // SUB) - 1, 0),
                                                               (OFF_DN + j * D_GRP) // W + h))
    per_head = pl.BlockSpec((HS, SUB, Dh), lambda b, h, t: (h, 0, 0))
    bcast = lambda x: jnp.broadcast_to(x.astype(f32)[:, None, None], (H, SUB, Dh))
    return pl.pallas_call(
        functools.partial(_deltanet_body, T=T, HS=HS),
        grid=(B, H // HS, S // T),
        in_specs=[blk(0), blk(1), blk(2), blk(3), prev(0), prev(1), prev(2),
                  pl.BlockSpec((1, T, LANES), lambda b, h, t: (b, t, OFF_SMALL // LANES)),
                  pl.BlockSpec((CONV_WIDTH, 3, W), lambda b, h, t: (0, 0, h)),
                  per_head, per_head,
                  pl.BlockSpec((1, Dh), lambda b, h, t: (0, 0))],
        out_specs=pl.BlockSpec((1, T, W), lambda b, h, t: (b, t, h)),
        out_shape=jax.ShapeDtypeStruct((B, S, D_GRP), bf16),
        scratch_shapes=[pltpu.VMEM((HS, Dh, Dh), f32)],
        compiler_params=pltpu.CompilerParams(dimension_semantics=("parallel", "parallel", "arbitrary"),
                                             vmem_limit_bytes=VMEM_LIMIT),
        name="deltanet",
    )(proj3, proj3, proj3, proj3, proj3, proj3, proj3, proj3, conv_w.reshape(CONV_WIDTH, 3, D_GRP),
      bcast(a_log), bcast(dt_bias), norm_g.reshape(1, Dh))


def _w_in_pieces():
    n0, r0, d0, w0 = 0, NSA_COLS, NSA_COLS + RET_COLS, NSA_COLS + RET_COLS + DN_COLS
    n_gate = 3 * NSA_HEADS
    return [
        (OFF_RW, w0, 3 * D_GRP),
        (OFF_RET, r0, 4 * D_GRP),
        (OFF_DN, d0, 3 * D_GRP),
        (OFF_DN + 3 * D_GRP, d0 + 3 * D_GRP + 2 * DN_HEADS, D_GRP),
        (OFF_NSA_Q, n0, D_GRP + 6 * NSA_KV_W),
        (OFF_LORA, w0 + 3 * D_GRP, RW_LORA),
        (OFF_SMALL, n0 + D_GRP + 6 * NSA_KV_W, n_gate),
        (OFF_SMALL + SMALL_DN_A, d0 + 3 * D_GRP, 2 * DN_HEADS),
    ]


def _pack_w_in_body(w_ref, o_ref):
    cols = o_ref.shape[2]
    for off, width in ((OFF_LORA + RW_LORA, RW_LORA_PAD - RW_LORA), (OFF_SMALL, LANES)):
        o_ref[0, off:off + width, :] = jnp.zeros((width, cols), o_ref.dtype)
    for dst, src, width in _w_in_pieces():
        o_ref[0, dst:dst + width, :] = w_ref[0, src:src + width, :].astype(o_ref.dtype)


def pack_w_in_t(w_in, tc=256):
    L, K = w_in.shape[0], w_in.shape[1]
    w_t = jnp.swapaxes(w_in, 1, 2)
    return pl.pallas_call(
        _pack_w_in_body,
        grid=(L, K // tc),
        in_specs=[pl.BlockSpec((1, P_TOTAL, tc), lambda l, i: (l, 0, i))],
        out_specs=pl.BlockSpec((1, P_PAD, tc), lambda l, i: (l, 0, i)),
        out_shape=jax.ShapeDtypeStruct((L, P_PAD, K), bf16),
        compiler_params=pltpu.CompilerParams(dimension_semantics=("parallel", "parallel"),
                                             vmem_limit_bytes=VMEM_LIMIT),
        name="pack_w_in",
    )(w_t)


def hybrid_mixer(h, hn, B, S, layer, w_in_packed, w_out, ck_w1, ck_w2, ck_pe, cv_w1, cv_w2, cv_pe, dn_conv, dn_a_log,
                 dn_dt_bias, dn_norm, rw_mu, rw_w0, rw_w2, rw_a0, rw_a2, rw_g2, rw_kk, rw_ka, rw_rk, rw_ln_g, rw_ln_b):
    M = B * S
    proj = matmul_nt(hn, w_in_packed, layer, tn=2048, tk=1024, name="in_proj")
    p3 = proj.reshape(B, S, P_PAD)
    o_nsa = nsa_mixer_pallas(proj, ck_w1, ck_w2, ck_pe, cv_w1, cv_w2, cv_pe, B, S)
    o_ret = retention_mixer_pallas(p3, B, S)
    o_dn = deltanet_mixer_pallas(p3, dn_conv, dn_a_log, dn_dt_bias, dn_norm, B, S)
    o_rw = rwkv7_mixer(proj, OFF_RW // (3 * D_GRP), OFF_LORA // RW_LORA_PAD, rw_mu, rw_w0, rw_w2, rw_a0, rw_a2,
                       rw_g2, rw_kk, rw_ka, rw_rk, rw_ln_g, rw_ln_b, B, S)
    parts = [o.reshape(M, D_GRP) for o in (o_nsa, o_ret, o_dn, o_rw)]
    return out_proj_residual(parts, w_out.astype(bf16), h)


def kernel(x, p, norm_mix, w_in, w_out, nsa_ck_w1, nsa_ck_w2, nsa_ck_pe, nsa_cv_w1, nsa_cv_w2, nsa_cv_pe, dn_conv, dn_a_log, dn_dt_bias, dn_norm, rw_mu, rw_w0, rw_w2, rw_a0, rw_a2, rw_g2, rw_kk, rw_ka, rw_rk, rw_ln_g, rw_ln_b, norm_moe, moe_router_grp, moe_router_grp_b, moe_router_exp, moe_router_exp_b, moe_w_gate, moe_w_up, moe_w_down, norm_ple, ple_w, ple_gate, norm_final):
    B, S, D = x.shape
    M = B * S
    h = x.reshape(M, D)
    w_in_packed = pack_w_in_t(w_in)
    for i in range(DEPTH):
        hn = rms_norm_rows(h, norm_mix[i], bf16)
        h = hybrid_mixer(h, hn, B, S, i, w_in_packed, w_out[i], nsa_ck_w1[i], nsa_ck_w2[i], nsa_ck_pe[i], nsa_cv_w1[i],
                         nsa_cv_w2[i], nsa_cv_pe[i], dn_conv[i], dn_a_log[i], dn_dt_bias[i], dn_norm[i],
                         rw_mu[i], rw_w0[i], rw_w2[i], rw_a0[i], rw_a2[i], rw_g2[i], rw_kk[i], rw_ka[i],
                         rw_rk[i], rw_ln_g[i], rw_ln_b[i])
        hn = rms_norm_rows(h, norm_moe[i], bf16)
        h = h + hier_moe(hn, moe_router_grp[i], moe_router_grp_b[i], moe_router_exp[i], moe_router_exp_b[i],
                         moe_w_gate, moe_w_up, moe_w_down, i)
        hn = rms_norm_rows(h, norm_ple[i], bf16)
        h = ple_residual(hn, ple_gate[i].astype(bf16), p[i].reshape(M, PLE_DIM), ple_w[i].astype(bf16), h)
    return rms_norm_rows(h, norm_final, f32).reshape(B, S, D)
```

```python
import functools

import jax
import jax.numpy as jnp
import numpy as np
from jax import lax
from jax.experimental import pallas as pl
from jax.experimental.pallas import tpu as pltpu

D_MODEL = 4096
DEPTH = 2
f32 = jnp.float32
bf16 = jnp.bfloat16
D_MIX = D_MODEL
D_GRP = D_MIX // 4
HEAD_DIM = 128
NORM_EPS = 1e-6
L2NORM_EPS = 1e-6
NSA_HEADS = D_GRP // HEAD_DIM
NSA_KV_HEADS = 2
NSA_GROUP = NSA_HEADS // NSA_KV_HEADS
NSA_KV_W = NSA_KV_HEADS * HEAD_DIM
CMP_LEN = 32
CMP_STRIDE = 16
SEL_LEN = 64
N_SEL = 16
N_LOCAL_FORCED = 2
FORCED_SCORE = 1e4
WINDOW = 512
ROPE_THETA = 500000.0
ROPE_DIMS = HEAD_DIM // 4
RET_HEADS = D_GRP // HEAD_DIM
RET_CHUNK = 128
RET_THETA = 10000.0
DN_HEADS = D_GRP // HEAD_DIM
DN_CHUNK = 64
CONV_WIDTH = 4
RW_HEAD_DIM = 64
RW_DECAY_LORA = 64
RW_AAA_LORA = 64
RW_GATE_LORA = 160
RW_LN_EPS = 64e-5
N_GROUPS = 4
EXPERTS_PER_GROUP = 8
N_EXPERTS = N_GROUPS * EXPERTS_PER_GROUP
TOP_K = 2
D_EXPERT = 768
ROW_BLOCK = 256
PLE_DIM = 256
NSA_COLS = D_GRP + 6 * NSA_KV_W + 3 * NSA_HEADS
RET_COLS = 4 * D_GRP
DN_COLS = 4 * D_GRP + 2 * DN_HEADS
RW_COLS = 3 * D_GRP + RW_DECAY_LORA + RW_AAA_LORA + RW_GATE_LORA
P_TOTAL = NSA_COLS + RET_COLS + DN_COLS + RW_COLS

RW_LORA = RW_DECAY_LORA + RW_AAA_LORA + RW_GATE_LORA
RW_LORA_PAD = 384
RW_GROUPS_PER_BODY = 4
LANES = 128
SUB = 8

OFF_RW = 0
OFF_RET = 3 * D_GRP
OFF_DN = OFF_RET + 4 * D_GRP
OFF_NSA_Q = OFF_DN + 4 * D_GRP
OFF_NSA_KV = OFF_NSA_Q + D_GRP
OFF_LORA = OFF_NSA_KV + 6 * NSA_KV_W
OFF_SMALL = OFF_LORA + RW_LORA_PAD
SMALL_DN_A = 32
P_PAD = OFF_SMALL + LANES
assert P_PAD == 14336 and OFF_LORA % RW_LORA_PAD == 0

V7X_VMEM_BYTES = 64 * 1024 * 1024
VMEM_LIMIT = V7X_VMEM_BYTES * 7 // 8


def _split2(x):
    h = x.astype(bf16)
    return h, (x - h.astype(f32)).astype(bf16)


def _split3(x):
    h = x.astype(bf16)
    r1 = x - h.astype(f32)
    m = r1.astype(bf16)
    return h, m, (r1 - m.astype(f32)).astype(bf16)


def _rwkv_prep_body(x_ref, xprev_ref, l_ref, lprev_ref, mu_ref, mul_ref, w0_ref, a0_ref, kkw_ref, ka_ref,
                    w2_ref, a2_ref, g2_ref, e_ref, r_ref, w_ref, k_ref, v_ref, kk_ref, b_ref, g_ref, *, tm, S):
    i = pl.program_id(0)
    first = (i * tm) % S == 0

    def token_mix(x, prev8, mu):
        rows = lax.broadcasted_iota(jnp.int32, x.shape, 0)
        prev_row = jnp.where(first, 0.0, prev8[SUB - 1:SUB, :])
        x_prev = jnp.where(rows == 0, prev_row, pltpu.roll(x, 1, axis=0))
        return x + (x_prev - x) * mu

    x = token_mix(x_ref[...], xprev_ref[...], mu_ref[...])
    lora = token_mix(l_ref[...], lprev_ref[...], mul_ref[...])
    r = x[:, 0:D_GRP]
    k = x[:, D_GRP:2 * D_GRP]
    v = x[:, 2 * D_GRP:3 * D_GRP]
    dw = jnp.dot(jnp.tanh(lora).astype(bf16), w2_ref[...], preferred_element_type=f32)
    da = jnp.dot(lora.astype(bf16), a2_ref[...], preferred_element_type=f32)
    g = jnp.dot(jax.nn.sigmoid(lora).astype(bf16), g2_ref[...], preferred_element_type=f32)
    w = -jax.nn.softplus(-(w0_ref[...] + dw)) - 0.5
    decay = jnp.exp(-jnp.exp(w))
    a = jax.nn.sigmoid(a0_ref[...] + da)
    kq = k * kkw_ref[...]
    sq_h, sq_l = _split2(kq * kq)
    ss = (jnp.dot(sq_h, e_ref[...], preferred_element_type=f32)
          + jnp.dot(sq_l, e_ref[...], preferred_element_type=f32))
    kk = kq * lax.rsqrt(ss + L2NORM_EPS)
    r_ref[...] = r
    w_ref[...] = decay
    k_ref[...] = k * (1.0 + (a - 1.0) * ka_ref[...])
    v_ref[...] = v
    kk_ref[...] = kk
    b_ref[...] = kk * a
    g_ref[...] = g


def rwkv_prep(proj, rkv_blk, lora_blk, mu, w0, w2, a0, a2, g2, k_k, k_a, S, tm=256):
    M = proj.shape[0]
    assert M % tm == 0 and S % tm == 0
    W3 = 3 * D_GRP
    mu_x = mu[:W3].reshape(1, W3)
    mu_l = jnp.pad(mu[W3:], (0, RW_LORA_PAD - RW_LORA)).reshape(1, RW_LORA_PAD)
    w2p = jnp.zeros((RW_LORA_PAD, D_GRP), f32).at[0:RW_DECAY_LORA].set(w2).astype(bf16)
    a2p = jnp.zeros((RW_LORA_PAD, D_GRP), f32).at[RW_DECAY_LORA:RW_DECAY_LORA + RW_AAA_LORA].set(a2).astype(bf16)
    g2p = jnp.zeros((RW_LORA_PAD, D_GRP), f32).at[RW_DECAY_LORA + RW_AAA_LORA:RW_LORA].set(g2).astype(bf16)
    hid = np.arange(D_GRP) // RW_HEAD_DIM
    e = jnp.asarray(hid[:, None] == hid[None, :], bf16)
    row = lambda t: t.reshape(1, D_GRP)
    vec = pl.BlockSpec((1, D_GRP), lambda i: (0, 0))
    wspec = pl.BlockSpec((RW_LORA_PAD, D_GRP), lambda i: (0, 0))
    ospec = pl.BlockSpec((tm, D_GRP), lambda i: (i, 0))
    prev_row_blk = lambda i: jnp.maximum(i * (tm // SUB) - 1, 0)
    return pl.pallas_call(
        functools.partial(_rwkv_prep_body, tm=tm, S=S),
        grid=(M // tm,),
        in_specs=[pl.BlockSpec((tm, W3), lambda i: (i, rkv_blk)),
                  pl.BlockSpec((SUB, W3), lambda i: (prev_row_blk(i), rkv_blk)),
                  pl.BlockSpec((tm, RW_LORA_PAD), lambda i: (i, lora_blk)),
                  pl.BlockSpec((SUB, RW_LORA_PAD), lambda i: (prev_row_blk(i), lora_blk)),
                  pl.BlockSpec((1, W3), lambda i: (0, 0)),
                  pl.BlockSpec((1, RW_LORA_PAD), lambda i: (0, 0)),
                  vec, vec, vec, vec, wspec, wspec, wspec,
                  pl.BlockSpec((D_GRP, D_GRP), lambda i: (0, 0))],
        out_specs=[ospec] * 7,
        out_shape=[jax.ShapeDtypeStruct((M, D_GRP), f32)] * 7,
        compiler_params=pltpu.CompilerParams(dimension_semantics=("parallel",), vmem_limit_bytes=VMEM_LIMIT),
        name="rwkv_prep",
    )(proj, proj, proj, proj, mu_x, mu_l, row(w0), row(a0), row(k_k), row(k_a), w2p, a2p, g2p, e)


def _rwkv_scan_body(r_ref, w_ref, k_ref, v_ref, kk_ref, b_ref, g_ref, lng_ref, lnb_ref, rk_ref, o_ref, s_ref,
                    vc_ref, y_ref, *, G, T):
    tb = pl.program_id(2)

    @pl.when(tb == 0)
    def _():
        s_ref[...] = jnp.zeros_like(s_ref)

    Dh = RW_HEAD_DIM
    lo = lax.broadcasted_iota(jnp.int32, (Dh, LANES), 1) < Dh
    lo8 = lax.broadcasted_iota(jnp.int32, (SUB, LANES), 1) < Dh
    sub_iota = lax.broadcasted_iota(jnp.int32, (4 * SUB, LANES), 0)
    row_iota = lax.broadcasted_iota(jnp.int32, (SUB, LANES), 0)

    def seg_sum(x, m):
        s_lo = jnp.sum(jnp.where(m, x, 0.0), axis=-1, keepdims=True)
        s_hi = jnp.sum(jnp.where(m, 0.0, x), axis=-1, keepdims=True)
        return jnp.where(m, s_lo, s_hi)

    cols = [slice(p * LANES, (p + 1) * LANES) for p in range(G)]
    n_groups = T // SUB

    k_iota = lax.broadcasted_iota(jnp.int32, (8 * SUB, LANES), 0)
    k_head = k_iota // (4 * SUB) == lax.broadcasted_iota(jnp.int32, (8 * SUB, LANES), 1) // Dh
    k_piece = (k_iota // SUB) % 4

    def v_pieces(gi):
        t0 = pl.multiple_of(gi * SUB, SUB)
        vts = []
        for p in range(G):
            vh, vm, vl = _split3(v_ref[0, pl.ds(t0, SUB), cols[p]])
            vp = jnp.concatenate([vh, vm, vl, jnp.zeros_like(vh)], axis=0)
            vts.append(jnp.concatenate([vp[:, :Dh], vp[:, Dh:]], axis=0).T)
        return vts

    def v_columns(vts, i, slot):
        sel = (k_head & (k_piece < 3) & (k_iota % SUB == i)).astype(bf16)
        for p in range(G):
            vc_ref[slot, i, p] = jnp.dot(vts[p], sel, preferred_element_type=f32)

    vts0 = v_pieces(0)
    for i in range(SUB):
        v_columns(vts0, i, 0)

    def run_group(gi, slot, states):
        t0 = pl.multiple_of(gi * SUB, SUB)
        vts_next = v_pieces(jnp.minimum(gi + 1, n_groups - 1))
        r8b = [r_ref[0, pl.ds(t0, SUB), cols[p]].astype(bf16) for p in range(G)]
        yacc = [jnp.zeros((SUB, LANES), f32) for _ in range(G)]
        tiles = [[ref[0, pl.ds(t0, SUB), cols[p]] for ref in (kk_ref, w_ref, b_ref, k_ref)] for p in range(G)]
        for i in range(SUB):
            for p in range(G):
                kk_r, w_r, b_r, k_r = [jnp.broadcast_to(t8[i:i + 1, :], (Dh, LANES)) for t8 in tiles[p]]
                S = states[p]
                sab = -seg_sum(S * kk_r, lo)
                states[p] = S * w_r + sab * b_r + vc_ref[slot, i, p] * k_r
            for p in range(G):
                Sb = states[p].astype(bf16)
                zero = jnp.zeros_like(Sb)
                mt = jnp.concatenate([jnp.where(lo, Sb, zero), jnp.where(lo, zero, Sb)], axis=0)
                y8 = lax.dot_general(r8b[p], mt, (((1,), (1,)), ((), ())), preferred_element_type=f32)
                yacc[p] = jnp.where(row_iota == i, y8, yacc[p])
            v_columns(vts_next, i, 1 - slot)
        for p in range(G):
            y_ref[pl.ds(t0, SUB), cols[p]] = yacc[p]
        return states

    def group_batch(gj, carry):
        states = [s_ref[p] for p in range(G)]
        for u in range(RW_GROUPS_PER_BODY):
            states = run_group(RW_GROUPS_PER_BODY * gj + u, u % 2, states)
        for p in range(G):
            s_ref[p] = states[p]
        return carry

    assert n_groups % RW_GROUPS_PER_BODY == 0 and RW_GROUPS_PER_BODY % 2 == 0
    lax.fori_loop(0, n_groups // RW_GROUPS_PER_BODY, group_batch, 0)

    lo_t = lax.broadcasted_iota(jnp.int32, (T, LANES), 1) < Dh
    for p in range(G):
        y = y_ref[:, cols[p]]
        mean = seg_sum(y, lo_t) * (1.0 / Dh)
        d = y - mean
        var = seg_sum(d * d, lo_t) * (1.0 / Dh)
        yn = d * lax.rsqrt(var + RW_LN_EPS) * lng_ref[:, cols[p]] + lnb_ref[:, cols[p]]
        bonus = seg_sum(r_ref[0, :, cols[p]] * k_ref[0, :, cols[p]] * rk_ref[:, cols[p]], lo_t) * v_ref[0, :, cols[p]]
        o_ref[0, :, cols[p]] = (yn + bonus) * g_ref[0, :, cols[p]]


def rwkv_scan(r, w, k, v, kk, b, g, ln_g, ln_b, r_k, G=8, T=512):
    B, S, D = r.shape
    n_pairs = D // LANES
    assert n_pairs % G == 0 and S % T == 0
    spec = pl.BlockSpec((1, T, G * LANES), lambda bi, pi, ti: (bi, ti, pi))
    vec = pl.BlockSpec((1, G * LANES), lambda bi, pi, ti: (0, pi))
    row = lambda t: t.reshape(1, D)
    return pl.pallas_call(
        functools.partial(_rwkv_scan_body, G=G, T=T),
        grid=(B, n_pairs // G, S // T),
        in_specs=[spec] * 7 + [vec] * 3,
        out_specs=spec,
        out_shape=jax.ShapeDtypeStruct((B, S, D), f32),
        scratch_shapes=[pltpu.VMEM((G, RW_HEAD_DIM, LANES), f32),
                        pltpu.VMEM((2, SUB, G, RW_HEAD_DIM, LANES), f32),
                        pltpu.VMEM((T, G * LANES), f32)],
        compiler_params=pltpu.CompilerParams(dimension_semantics=("parallel", "parallel", "arbitrary"),
                                             vmem_limit_bytes=VMEM_LIMIT),
        name="rwkv_scan",
    )(r, w, k, v, kk, b, g, row(ln_g), row(ln_b), row(r_k))


def rwkv7_mixer(proj, rkv_blk, lora_blk, mu, w0, w2, a0, a2, g2, k_k, k_a, r_k, ln_g, ln_b, B, S):
    outs = rwkv_prep(proj, rkv_blk, lora_blk, mu, w0, w2, a0, a2, g2, k_k, k_a, S)
    r, w, k, v, kk, b, g = [t.reshape(B, S, D_GRP) for t in outs]
    return rwkv_scan(r, w, k, v, kk, b, g, ln_g, ln_b, r_k.reshape(-1))


def _mm_nt_body(a_ref, bt_ref, o_ref, acc_ref):
    k = pl.program_id(2)

    @pl.when(k == 0)
    def _():
        acc_ref[...] = jnp.zeros_like(acc_ref)

    acc_ref[...] += lax.dot_general(a_ref[...], bt_ref[...], (((1,), (1,)), ((), ())), preferred_element_type=f32)

    @pl.when(k == pl.num_programs(2) - 1)
    def _():
        o_ref[...] = acc_ref[...].astype(o_ref.dtype)


def matmul_nt(a, bt, layer, out_dtype=f32, tm=1024, tn=1024, tk=2048, name="matmul_nt"):
    M, K = a.shape
    _, N, _ = bt.shape
    assert M % tm == 0 and N % tn == 0 and K % tk == 0, (M, N, K, tm, tn, tk)
    return pl.pallas_call(
        _mm_nt_body,
        grid=(M // tm, N // tn, K // tk),
        in_specs=[pl.BlockSpec((tm, tk), lambda i, j, k: (i, k)),
                  pl.BlockSpec((None, tn, tk), lambda i, j, k: (layer, j, k))],
        out_specs=pl.BlockSpec((tm, tn), lambda i, j, k: (i, j)),
        out_shape=jax.ShapeDtypeStruct((M, N), out_dtype),
        scratch_shapes=[pltpu.VMEM((tm, tn), f32)],
        compiler_params=pltpu.CompilerParams(
            dimension_semantics=("parallel", "parallel", "arbitrary"), vmem_limit_bytes=VMEM_LIMIT),
        name=name,
    )(a, bt)


def _rms_norm_body(x_ref, g_ref, o_ref):
    x = x_ref[...]
    y = x * lax.rsqrt(jnp.mean(x * x, axis=-1, keepdims=True) + NORM_EPS)
    o_ref[...] = (y * g_ref[...]).astype(o_ref.dtype)


def rms_norm_rows(x, g, out_dtype, tm=256):
    M, D = x.shape
    return pl.pallas_call(
        _rms_norm_body,
        grid=(M // tm,),
        in_specs=[pl.BlockSpec((tm, D), lambda i: (i, 0)), pl.BlockSpec((1, D), lambda i: (0, 0))],
        out_specs=pl.BlockSpec((tm, D), lambda i: (i, 0)),
        out_shape=jax.ShapeDtypeStruct((M, D), out_dtype),
        compiler_params=pltpu.CompilerParams(dimension_semantics=("parallel",), vmem_limit_bytes=VMEM_LIMIT),
        name="rms_norm",
    )(x, g.reshape(1, D).astype(f32))


def _out_proj_body(a0_ref, a1_ref, a2_ref, a3_ref, b_ref, h_ref, o_ref, acc_ref):
    k = pl.program_id(2)

    @pl.when(k == 0)
    def _():
        acc_ref[...] = h_ref[...]

    for j, a_ref in enumerate((a0_ref, a1_ref, a2_ref, a3_ref)):
        @pl.when(k == j)
        def _(a_ref=a_ref):
            acc_ref[...] += jnp.dot(a_ref[...].astype(bf16), b_ref[...], preferred_element_type=f32)

    @pl.when(k == pl.num_programs(2) - 1)
    def _():
        o_ref[...] = acc_ref[...]


def out_proj_residual(parts, w, h, tm=1024, tn=1024):
    M, D = h.shape
    aspec = pl.BlockSpec((tm, D_GRP), lambda i, j, k: (i, 0))
    return pl.pallas_call(
        _out_proj_body,
        grid=(M // tm, D // tn, len(parts)),
        in_specs=[aspec] * 4 + [pl.BlockSpec((D_GRP, tn), lambda i, j, k: (k, j)),
                                pl.BlockSpec((tm, tn), lambda i, j, k: (i, j))],
        out_specs=pl.BlockSpec((tm, tn), lambda i, j, k: (i, j)),
        out_shape=jax.ShapeDtypeStruct((M, D), f32),
        scratch_shapes=[pltpu.VMEM((tm, tn), f32)],
        compiler_params=pltpu.CompilerParams(
            dimension_semantics=("parallel", "parallel", "arbitrary"), vmem_limit_bytes=VMEM_LIMIT),
        name="out_proj",
    )(*parts, w, h)


def _ple_body(a_ref, b_ref, p_ref, pw_ref, h_ref, o_ref, acc_ref):
    k = pl.program_id(2)

    @pl.when(k == 0)
    def _():
        acc_ref[...] = jnp.zeros_like(acc_ref)

    acc_ref[...] += jnp.dot(a_ref[...], b_ref[...], preferred_element_type=f32)

    @pl.when(k == pl.num_programs(2) - 1)
    def _():
        emb = jnp.dot(p_ref[...].astype(bf16), pw_ref[...], preferred_element_type=f32)
        o_ref[...] = h_ref[...] + emb * jax.nn.sigmoid(acc_ref[...])


def ple_residual(hn, gate_w, p, ple_w, h, tm=1024, tn=1024, tk=2048):
    M, D = h.shape
    return pl.pallas_call(
        _ple_body,
        grid=(M // tm, D // tn, D // tk),
        in_specs=[pl.BlockSpec((tm, tk), lambda i, j, k: (i, k)),
                  pl.BlockSpec((tk, tn), lambda i, j, k: (k, j)),
                  pl.BlockSpec((tm, PLE_DIM), lambda i, j, k: (i, 0)),
                  pl.BlockSpec((PLE_DIM, tn), lambda i, j, k: (0, j)),
                  pl.BlockSpec((tm, tn), lambda i, j, k: (i, j))],
        out_specs=pl.BlockSpec((tm, tn), lambda i, j, k: (i, j)),
        out_shape=jax.ShapeDtypeStruct((M, D), f32),
        scratch_shapes=[pltpu.VMEM((tm, tn), f32)],
        compiler_params=pltpu.CompilerParams(
            dimension_semantics=("parallel", "parallel", "arbitrary"), vmem_limit_bytes=VMEM_LIMIT),
        name="ple",
    )(hn, gate_w, p, ple_w, h)


def _moe_router_body(x_ref, w_ref, b_ref, id_ref, wt_ref):
    logits = jnp.dot(x_ref[...].astype(bf16), w_ref[...], preferred_element_type=f32) + b_ref[...]
    lane = lax.broadcasted_iota(jnp.int32, logits.shape, 1).astype(f32)
    ninf = -jnp.inf

    def top1(vals):
        m = jnp.max(vals, axis=-1, keepdims=True)
        idx = jnp.min(jnp.where(vals == m, lane, float(LANES)), axis=-1, keepdims=True)
        return m, idx

    gl = jnp.where(lane < N_GROUPS, logits, ninf)
    g_max, g_idx = top1(gl)
    g_w = 1.0 / jnp.sum(jnp.exp(gl - g_max), axis=-1, keepdims=True)
    lo = N_GROUPS + g_idx * EXPERTS_PER_GROUP
    el = jnp.where((lane >= lo) & (lane < lo + EXPERTS_PER_GROUP), logits, ninf)
    e_max, _ = top1(el)
    pe = jnp.exp(el - e_max)
    p = pe / jnp.sum(pe, axis=-1, keepdims=True)
    p = jnp.where(el == ninf, ninf, p)
    p1, i1 = top1(p)
    p2, i2 = top1(jnp.where(lane == i1, ninf, p))
    scale = g_w / (p1 + p2)
    id_ref[...] = jnp.where(lane == 0, i1 - N_GROUPS, jnp.where(lane == 1, i2 - N_GROUPS, 0.0)).astype(jnp.int32)
    wt_ref[...] = jnp.where(lane == 0, p1 * scale, jnp.where(lane == 1, p2 * scale, 0.0))


def moe_router(xb, router_grp, router_grp_b, router_exp, router_exp_b, tm=512):
    M, D = xb.shape
    n_log = N_GROUPS + N_EXPERTS
    w = jnp.pad(jnp.concatenate([router_grp, router_exp], axis=1), ((0, 0), (0, LANES - n_log))).astype(bf16)
    b = jnp.pad(jnp.concatenate([router_grp_b, router_exp_b]), (0, LANES - n_log)).reshape(1, LANES).astype(f32)
    ids, wts = pl.pallas_call(
        _moe_router_body,
        grid=(M // tm,),
        in_specs=[pl.BlockSpec((tm, D), lambda i: (i, 0)), pl.BlockSpec((D, LANES), lambda i: (0, 0)),
                  pl.BlockSpec((1, LANES), lambda i: (0, 0))],
        out_specs=[pl.BlockSpec((tm, LANES), lambda i: (i, 0))] * 2,
        out_shape=[jax.ShapeDtypeStruct((M, LANES), jnp.int32), jax.ShapeDtypeStruct((M, LANES), f32)],
        compiler_params=pltpu.CompilerParams(dimension_semantics=("parallel",), vmem_limit_bytes=VMEM_LIMIT),
        name="moe_router",
    )(xb, w, b)
    return ids[:, :TOP_K], wts[:, :TOP_K]


def _moe_up_body(be_ref, nu_ref, x_ref, wg_ref, wu_ref, h_ref, *, tf):
    i = pl.program_id(0)

    @pl.when(i < nu_ref[0])
    def _():
        x = x_ref[...].astype(bf16)
        for f in range(D_EXPERT // tf):
            c = slice(f * tf, (f + 1) * tf)
            hg = jnp.dot(x, wg_ref[0, 0, :, c].astype(bf16), preferred_element_type=f32)
            hu = jnp.dot(x, wu_ref[0, 0, :, c].astype(bf16), preferred_element_type=f32)
            h_ref[:, c] = (hg * jax.nn.sigmoid(hg) * hu).astype(bf16)

    @pl.when(i >= nu_ref[0])
    def _():
        h_ref[...] = jnp.zeros_like(h_ref)


def _moe_down_body(be_ref, nu_ref, h_ref, wd_ref, wt_ref, y_ref):
    i = pl.program_id(0)

    @pl.when(i < nu_ref[0])
    def _():
        y_ref[...] = jnp.dot(h_ref[...], wd_ref[0, 0].astype(bf16), preferred_element_type=f32) * wt_ref[...]

    @pl.when(i >= nu_ref[0])
    def _():
        y_ref[...] = jnp.zeros_like(y_ref)


def moe_experts(x_pad, w_pad, blk_expert, n_used, wg, wu, wd, layer, tf=256):
    A_pad, D = x_pad.shape
    n_blk = A_pad // ROW_BLOCK
    wmap = lambda i, be, nu: (layer, be[i], 0, 0)
    wspec = pl.BlockSpec((1, 1, D, D_EXPERT), wmap, pipeline_mode=pl.Buffered(1))
    wspec2 = pl.BlockSpec((1, 1, D, D_EXPERT), wmap)
    h = pl.pallas_call(
        functools.partial(_moe_up_body, tf=tf),
        grid_spec=pltpu.PrefetchScalarGridSpec(
            num_scalar_prefetch=2,
            grid=(n_blk,),
            in_specs=[pl.BlockSpec((ROW_BLOCK, D), lambda i, be, nu: (i, 0)), wspec2, wspec],
            out_specs=pl.BlockSpec((ROW_BLOCK, D_EXPERT), lambda i, be, nu: (i, 0)),
        ),
        out_shape=jax.ShapeDtypeStruct((A_pad, D_EXPERT), bf16),
        compiler_params=pltpu.CompilerParams(dimension_semantics=("arbitrary",), vmem_limit_bytes=VMEM_LIMIT),
        name="moe_up",
    )(blk_expert, n_used, x_pad, wg, wu)
    return pl.pallas_call(
        _moe_down_body,
        grid_spec=pltpu.PrefetchScalarGridSpec(
            num_scalar_prefetch=2,
            grid=(n_blk,),
            in_specs=[pl.BlockSpec((ROW_BLOCK, D_EXPERT), lambda i, be, nu: (i, 0)),
                      pl.BlockSpec((1, 1, D_EXPERT, D), lambda i, be, nu: (layer, be[i], 0, 0)),
                      pl.BlockSpec((ROW_BLOCK, 1), lambda i, be, nu: (i, 0))],
            out_specs=pl.BlockSpec((ROW_BLOCK, D), lambda i, be, nu: (i, 0)),
        ),
        out_shape=jax.ShapeDtypeStruct((A_pad, D), f32),
        compiler_params=pltpu.CompilerParams(dimension_semantics=("arbitrary",), vmem_limit_bytes=VMEM_LIMIT),
        name="moe_down",
    )(blk_expert, n_used, h, wd, w_pad.reshape(A_pad, 1))


def hier_moe(xb, router_grp, router_grp_b, router_exp, router_exp_b, w_gate, w_up, w_down, layer):
    M, D = xb.shape
    expert_ids, weights = moe_router(xb, router_grp, router_grp_b, router_exp, router_exp_b)
    A = M * TOP_K
    e_flat = expert_ids.reshape(A)
    onehot = (e_flat[:, None] == jnp.arange(N_EXPERTS)[None, :]).astype(jnp.int32)
    rank = jnp.take_along_axis(jnp.cumsum(onehot, axis=0), e_flat[:, None], axis=1)[:, 0] - 1
    counts = jnp.sum(onehot, axis=0)
    padded = (counts + ROW_BLOCK - 1) // ROW_BLOCK * ROW_BLOCK
    end_pad = jnp.cumsum(padded)
    dest = (end_pad - padded)[e_flat] + rank
    n_blk = -(-A // ROW_BLOCK) + N_EXPERTS
    A_pad = n_blk * ROW_BLOCK
    asg = jnp.full((A_pad,), -1, jnp.int32).at[dest].set(jnp.arange(A, dtype=jnp.int32))
    tok_pad = jnp.where(asg >= 0, asg // TOP_K, jnp.arange(A_pad, dtype=jnp.int32) % M)
    w_pad = jnp.where(asg >= 0, weights.reshape(A)[jnp.maximum(asg, 0)], 0.0)
    blk_expert = jnp.minimum(jnp.searchsorted(end_pad, jnp.arange(n_blk) * ROW_BLOCK, side='right'),
                             N_EXPERTS - 1).astype(jnp.int32)
    n_used = (end_pad[-1:] // ROW_BLOCK).astype(jnp.int32)
    x_pad = xb[tok_pad]
    y_pad = moe_experts(x_pad, w_pad, blk_expert, n_used, w_gate, w_up, w_down, layer)
    rows = dest.reshape(M, TOP_K)
    return y_pad[rows[:, 0]] + y_pad[rows[:, 1]]


def _rope_tables(S, theta, rot_dims, width):
    half = rot_dims // 2
    freqs = theta ** (-np.arange(half, dtype=np.float64) / half)
    ang = np.arange(S, dtype=np.float64)[:, None] * freqs[None, :]
    cos = np.ones((S, width), np.float64)
    sin_lo = np.zeros((S, width), np.float64)
    sin_hi = np.zeros((S, width), np.float64)
    cos[:, :half] = np.cos(ang)
    cos[:, half:rot_dims] = np.cos(ang)
    sin_lo[:, :half] = -np.sin(ang)
    sin_hi[:, half:rot_dims] = np.sin(ang)
    return [jnp.asarray(t, f32) for t in (cos, sin_lo, sin_hi)]


def _rope128(x, cos, sin_lo, sin_hi, half):
    return x * cos + pltpu.roll(x, LANES - half, axis=1) * sin_lo + pltpu.roll(x, half, axis=1) * sin_hi


def _nsa_prep_body(q_ref, kv_ref, cos_ref, sl_ref, sh_ref, qo_ref, kvo_ref, co_ref):
    cos, sl, sh = cos_ref[...], sl_ref[...], sh_ref[...]
    half = ROPE_DIMS // 2
    scale = HEAD_DIM ** -0.5
    for h in range(NSA_HEADS):
        c = slice(h * LANES, (h + 1) * LANES)
        qo_ref[:, c] = (_rope128(q_ref[:, c], cos, sl, sh, half) * scale).astype(bf16)
    for j in range(6):
        for g in range(NSA_KV_HEADS):
            c = slice((j * NSA_KV_HEADS + g) * LANES, (j * NSA_KV_HEADS + g + 1) * LANES)
            x = kv_ref[:, c]
            if j % 2 == 0:
                x = _rope128(x, cos, sl, sh, half)
            if j < 2:
                co_ref[:, c] = x.astype(bf16)
            else:
                kvo_ref[:, slice(c.start - 4 * LANES, c.stop - 4 * LANES)] = x.astype(bf16)


def nsa_prep(proj, S, tm=512):
    M = proj.shape[0]
    assert M % tm == 0 and S % tm == 0
    tabs = _rope_tables(S, ROPE_THETA, ROPE_DIMS, LANES)
    tspec = pl.BlockSpec((tm, LANES), lambda i: (i % (S // tm), 0))
    return pl.pallas_call(
        _nsa_prep_body,
        grid=(M // tm,),
        in_specs=[pl.BlockSpec((tm, D_GRP), lambda i: (i, OFF_NSA_Q // D_GRP)),
                  pl.BlockSpec((tm, 6 * NSA_KV_W), lambda i: (i, OFF_NSA_KV // (6 * NSA_KV_W))),
                  tspec, tspec, tspec],
        out_specs=[pl.BlockSpec((tm, D_GRP), lambda i: (i, 0)),
                   pl.BlockSpec((tm, 4 * NSA_KV_W), lambda i: (i, 0)),
                   pl.BlockSpec((tm, 2 * NSA_KV_W), lambda i: (i, 0))],
        out_shape=[jax.ShapeDtypeStruct((M, D_GRP), bf16), jax.ShapeDtypeStruct((M, 4 * NSA_KV_W), bf16),
                   jax.ShapeDtypeStruct((M, 2 * NSA_KV_W), bf16)],
        compiler_params=pltpu.CompilerParams(dimension_semantics=("parallel",), vmem_limit_bytes=VMEM_LIMIT),
        name="nsa_prep",
    )(proj, proj, *tabs)


def _nsa_compress_body(r_ref, w1_ref, w2_ref, pe_ref, o_ref):
    r = r_ref[0, 0, 0]
    half_w = CMP_STRIDE * HEAD_DIM
    a = jnp.dot(r, w1_ref[0, :half_w, :], preferred_element_type=f32)
    b = jnp.dot(r, w1_ref[0, half_w:, :], preferred_element_type=f32)
    n = r.shape[0]
    pe = jnp.dot(jnp.broadcast_to(pe_ref[0], (SUB, pe_ref.shape[-1])), w1_ref[0], preferred_element_type=f32)[0:1]
    pre = a + pltpu.roll(b, n - 1, axis=0) + pe
    hid = pre * jax.nn.sigmoid(pre)
    o_ref[0, 0, 0] = jnp.dot(hid.astype(bf16), w2_ref[0], preferred_element_type=f32).astype(bf16)


def nsa_compress(c_rot, w1s, w2s, pes, B, S):
    G, Dh = NSA_KV_HEADS, HEAD_DIM
    nc = S // CMP_STRIDE
    r = c_rot.reshape(B, nc, CMP_STRIDE, 2, G, Dh).transpose(3, 0, 4, 1, 2, 5).reshape(2, B, G, nc, CMP_STRIDE * Dh)
    w1 = jnp.stack(w1s).astype(bf16)
    w2 = jnp.stack(w2s).astype(bf16)
    pe = jnp.stack(pes).reshape(2, 1, CMP_LEN * Dh).astype(bf16)
    return pl.pallas_call(
        _nsa_compress_body,
        grid=(2, B, G),
        in_specs=[pl.BlockSpec((1, 1, 1, nc, CMP_STRIDE * Dh), lambda j, b, g: (j, b, g, 0, 0)),
                  pl.BlockSpec((1, CMP_LEN * Dh, Dh), lambda j, b, g: (j, 0, 0)),
                  pl.BlockSpec((1, Dh, Dh), lambda j, b, g: (j, 0, 0)),
                  pl.BlockSpec((1, 1, CMP_LEN * Dh), lambda j, b, g: (j, 0, 0))],
        out_specs=pl.BlockSpec((1, 1, 1, nc, Dh), lambda j, b, g: (j, b, g, 0, 0)),
        out_shape=jax.ShapeDtypeStruct((2, B, G, nc, Dh), bf16),
        compiler_params=pltpu.CompilerParams(dimension_semantics=("parallel", "parallel", "parallel"),
                                             vmem_limit_bytes=VMEM_LIMIT),
        name="nsa_compress",
    )(r, w1, w2, pe)


def _softmax_rows(s, mask):
    s = jnp.where(mask, s, -jnp.inf)
    m = jnp.max(s, axis=-1, keepdims=True)
    m = jnp.where(m == -jnp.inf, 0.0, m)
    e = jnp.exp(s - m)
    return e, jnp.maximum(jnp.sum(e, axis=-1, keepdims=True), jnp.finfo(f32).tiny)


def _nsa_attn_body(q_ref, cmp_ref, ks_ref, vs_ref, kw_ref, vw_ref, gate_ref, ovt_ref, o_ref, *, tq, tk, S):
    g = pl.program_id(1)
    qi = pl.program_id(2)
    R, Dh = NSA_GROUP, HEAD_DIM
    rows = R * tq
    n_blk = S // SEL_LEN
    n_cmp = S // CMP_STRIDE - 1
    q0 = qi * tq
    q4 = jnp.concatenate([q_ref[0, :, r * Dh:(r + 1) * Dh] for r in range(R)], axis=0)
    nt = (((1,), (1,)), ((), ()))

    def qpos(shape):
        return q0 + lax.broadcasted_iota(jnp.int32, shape, 0) % tq

    k_cmp, v_cmp = cmp_ref[0, 0, 0], cmp_ref[1, 0, 0]
    ncp = k_cmp.shape[0]
    s = lax.dot_general(q4, k_cmp, nt, preferred_element_type=f32)
    n_idx = lax.broadcasted_iota(jnp.int32, (rows, ncp), 1)
    mask = (n_idx * CMP_STRIDE + (CMP_LEN - 1) <= qpos((rows, ncp))) & (n_idx < n_cmp)
    e, den = _softmax_rows(s, mask)
    p = e / den
    o_cmp = jnp.dot(p.astype(bf16), v_cmp, preferred_element_type=f32)

    psum = p[0:tq]
    for r in range(1, R):
        psum = psum + p[r * tq:(r + 1) * tq]
    ph, plo = _split2(psum)
    imp = (lax.dot_general(ovt_ref[...], ph, nt, preferred_element_type=f32)
           + lax.dot_general(ovt_ref[...], plo, nt, preferred_element_type=f32))
    blk = lax.broadcasted_iota(jnp.int32, (n_blk, tq), 0)
    cur = (q0 + lax.broadcasted_iota(jnp.int32, (n_blk, tq), 1)) // SEL_LEN
    visible = blk <= cur
    forced = (blk == 0) | (visible & (blk > cur - N_LOCAL_FORCED))
    imp = jnp.where(forced, FORCED_SCORE, jnp.where(visible, imp, -1.0))
    sel_t = jnp.zeros((n_blk, tq), f32)
    blk_f = blk.astype(f32)
    for _ in range(min(N_SEL, n_blk)):
        m = jnp.max(imp, axis=0, keepdims=True)
        first = jnp.min(jnp.where(imp == m, blk_f, float(n_blk)), axis=0, keepdims=True)
        hit = blk_f == first
        sel_t = jnp.where(hit, 1.0, sel_t)
        imp = jnp.where(hit, -jnp.inf, imp)
    sel_q = jnp.concatenate([sel_t.T, jnp.ones((tq, LANES - n_blk), f32)], axis=1).astype(bf16)

    blocks_per_tile = tk // SEL_LEN

    big = float(2 ** 30)
    kb = sel_q.shape[1]

    def sel_step(kt, carry, causal):
        m_run, l_run, acc = carry
        k0 = pl.multiple_of(kt * tk, tk)
        kt_k = ks_ref[0, pl.ds(k0, tk), :]
        kt_v = vs_ref[0, pl.ds(k0, tk), :]
        s = lax.dot_general(q4, kt_k, nt, preferred_element_type=f32)
        row = lax.broadcasted_iota(jnp.int32, (kb, tk), 0)
        in_blk = row == kt * blocks_per_tile + lax.broadcasted_iota(jnp.int32, (kb, tk), 1) // SEL_LEN
        expand = jnp.where(in_blk, big, jnp.where(row == n_blk, -big, 0.0)).astype(bf16)
        bias = jnp.dot(sel_q, expand, preferred_element_type=f32)
        s = (s.reshape(R, tq, tk) + bias[None]).reshape(rows, tk)
        if causal:
            kpos = k0 + lax.broadcasted_iota(jnp.int32, (rows, tk), 1)
            s = jnp.where(kpos <= qpos((rows, tk)), s, -big)
        m_new = jnp.maximum(m_run, jnp.max(s, axis=-1, keepdims=True))
        alpha = jnp.exp(m_run - m_new)
        e = jnp.exp(s - m_new)
        l_new = alpha * l_run + jnp.sum(e, axis=-1, keepdims=True)
        acc = alpha * acc + jnp.dot(e.astype(bf16), kt_v, preferred_element_type=f32)
        return m_new, l_new, acc

    last_kt = (q0 + tq - 1) // tk
    init = (jnp.full((rows, 1), -jnp.inf, f32), jnp.zeros((rows, 1), f32), jnp.zeros((rows, Dh), f32))
    carry = lax.fori_loop(0, last_kt, functools.partial(sel_step, causal=False), init)
    _, l_sel, acc_sel = sel_step(last_kt, carry, causal=True)
    o_sel = acc_sel / l_sel

    span = WINDOW + tq
    w0 = pl.multiple_of(jnp.maximum(q0 - WINDOW, 0), tq)
    kwin = kw_ref[0, pl.ds(w0, span), :]
    vwin = vw_ref[0, pl.ds(w0, span), :]
    s = lax.dot_general(q4, kwin, nt, preferred_element_type=f32)
    kpos = w0 + lax.broadcasted_iota(jnp.int32, (rows, span), 1)
    qp = qpos((rows, span))
    e, den = _softmax_rows(s, (kpos <= qp) & (kpos > qp - WINDOW))
    o_win = jnp.dot(e.astype(bf16), vwin, preferred_element_type=f32) / den

    gate = jax.nn.sigmoid(gate_ref[0])
    for r in range(R):
        c0 = (g * R + r) * 3
        lane = lax.broadcasted_iota(jnp.int32, (tq, LANES), 1)
        gsel = lambda c: jnp.sum(jnp.where(lane == c0 + c, gate, 0.0), axis=-1, keepdims=True)
        rs = slice(r * tq, (r + 1) * tq)
        o_ref[0, :, r * Dh:(r + 1) * Dh] = (gsel(0) * o_cmp[rs] + gsel(1) * o_sel[rs]
                                            + gsel(2) * o_win[rs]).astype(o_ref.dtype)


def nsa_attention(q_rot, kv_rot, cmp_kv, proj3, B, S, tq=256, tk=512):
    G, R, Dh = NSA_KV_HEADS, NSA_GROUP, HEAD_DIM
    n_blk = S // SEL_LEN
    assert S % tk == 0 and tk % tq == 0 and S >= WINDOW + tq and n_blk % SUB == 0 and n_blk < LANES
    nc = S // CMP_STRIDE
    c0 = np.arange(nc - 1)[None, :] * CMP_STRIDE
    s0 = np.arange(n_blk)[:, None] * SEL_LEN
    ovt = np.zeros((n_blk, nc), np.float32)
    ovt[:, :nc - 1] = np.clip(np.minimum(c0 + CMP_LEN, s0 + SEL_LEN) - np.maximum(c0, s0), 0, None) / CMP_LEN
    kvspec = lambda j: pl.BlockSpec((1, S, Dh), lambda b, g, i: (b, 0, j * G + g))
    return pl.pallas_call(
        functools.partial(_nsa_attn_body, tq=tq, tk=tk, S=S),
        grid=(B, G, S // tq),
        in_specs=[pl.BlockSpec((1, tq, R * Dh), lambda b, g, i: (b, i, g)),
                  pl.BlockSpec((2, 1, 1, nc, Dh), lambda b, g, i: (0, b, g, 0, 0)),
                  kvspec(0), kvspec(1), kvspec(2), kvspec(3),
                  pl.BlockSpec((1, tq, LANES), lambda b, g, i: (b, i, OFF_SMALL // LANES)),
                  pl.BlockSpec((n_blk, nc), lambda b, g, i: (0, 0))],
        out_specs=pl.BlockSpec((1, tq, R * Dh), lambda b, g, i: (b, i, g)),
        out_shape=jax.ShapeDtypeStruct((B, S, D_GRP), bf16),
        compiler_params=pltpu.CompilerParams(dimension_semantics=("parallel", "parallel", "arbitrary"),
                                             vmem_limit_bytes=VMEM_LIMIT),
        name="nsa_attention",
    )(q_rot, cmp_kv, kv_rot, kv_rot, kv_rot, kv_rot, proj3, jnp.asarray(ovt, bf16))


def nsa_mixer_pallas(proj, ck_w1, ck_w2, ck_pe, cv_w1, cv_w2, cv_pe, B, S):
    q_rot, kv_rot, c_rot = nsa_prep(proj, S)
    cmp_kv = nsa_compress(c_rot, (ck_w1, cv_w1), (ck_w2, cv_w2), (ck_pe, cv_pe), B, S)
    return nsa_attention(q_rot.reshape(B, S, D_GRP), kv_rot.reshape(B, S, 4 * NSA_KV_W), cmp_kv,
                         proj.reshape(B, S, P_PAD), B, S)


def _retention_body(q_ref, k_ref, v_ref, g_ref, cos_ref, sin_ref, dec_ref, xi_ref, zeta_ref, gc_ref, o_ref, st_ref,
                    *, T, HS):
    @pl.when(pl.program_id(2) == 0)
    def _():
        st_ref[...] = jnp.zeros_like(st_ref)

    C, Dh = RET_CHUNK, HEAD_DIM
    nt = (((1,), (1,)), ((), ()))
    pre = {}
    for hh in range(HS):
        hc = slice(hh * Dh, (hh + 1) * Dh)
        for c in range(T // C):
            rs = slice(c * C, (c + 1) * C)
            cos, sin = cos_ref[rs, :], sin_ref[rs, :]
            rope = lambda x: x * cos + pltpu.roll(x, Dh // 2, axis=1) * sin
            q = rope(q_ref[0, rs, hc])
            k = rope(k_ref[0, rs, hc]) * Dh ** -0.5
            qb, vb = q.astype(bf16), v_ref[0, rs, hc].astype(bf16)
            inner = lax.dot_general(qb, k.astype(bf16), nt, preferred_element_type=f32) * dec_ref[hh]
            o_in = jnp.dot(inner.astype(bf16), vb, preferred_element_type=f32)
            kv = jnp.dot((k * zeta_ref[hh]).astype(bf16).T, vb, preferred_element_type=f32)
            pre[hh, c] = (qb, o_in, kv)
    states = [st_ref[hh] for hh in range(HS)]
    for c in range(T // C):
        rs = slice(c * C, (c + 1) * C)
        for hh in range(HS):
            hc = slice(hh * Dh, (hh + 1) * Dh)
            qb, o_in, kv = pre[hh, c]
            o = o_in + jnp.dot(qb, states[hh].astype(bf16), preferred_element_type=f32) * xi_ref[hh]
            states[hh] = states[hh] * gc_ref[hh, 0:1, :] + kv
            mu = jnp.mean(o, axis=-1, keepdims=True)
            d = o - mu
            o = d * lax.rsqrt(jnp.mean(d * d, axis=-1, keepdims=True) + NORM_EPS)
            gt = g_ref[0, rs, hc]
            o_ref[0, rs, hc] = (gt * jax.nn.sigmoid(gt) * o).astype(o_ref.dtype)
    for hh in range(HS):
        st_ref[hh] = states[hh]


def retention_mixer_pallas(proj3, B, S, T=512, HS=2):
    H, Dh, C = RET_HEADS, HEAD_DIM, RET_CHUNK
    W = HS * Dh
    assert S % T == 0 and T % C == 0 and H % HS == 0
    freqs = RET_THETA ** (-np.linspace(0.0, 1.0, Dh // 2))
    ang = np.arange(S, dtype=np.float64)[:, None] * freqs[None, :]
    cos = jnp.asarray(np.concatenate([np.cos(ang), np.cos(ang)], 1), f32)
    sin = jnp.asarray(np.concatenate([-np.sin(ang), np.sin(ang)], 1), f32)
    log_gamma = np.log1p(-(2.0 ** (-5.0 - np.arange(H, dtype=np.float64))))
    n = np.arange(C, dtype=np.float64)
    diff = n[:, None] - n[None, :]
    decay_in = np.where(diff >= 0, np.exp(log_gamma[:, None, None] * np.maximum(diff, 0.0)), 0.0)
    xi = np.exp(log_gamma[:, None] * (n + 1.0))[:, :, None] * np.ones((1, 1, Dh))
    zeta = np.exp(log_gamma[:, None] * (C - 1.0 - n))[:, :, None] * np.ones((1, 1, Dh))
    gamma_c = np.exp(log_gamma * C)[:, None, None] * np.ones((1, SUB, Dh))
    blk = lambda j: pl.BlockSpec((1, T, W), lambda b, h, t: (b, t, (OFF_RET + j * D_GRP) // W + h))
    tab = pl.BlockSpec((T, Dh), lambda b, h, t: (t, 0))
    per_head = lambda r: pl.BlockSpec((HS, r, Dh), lambda b, h, t: (h, 0, 0))
    return pl.pallas_call(
        functools.partial(_retention_body, T=T, HS=HS),
        grid=(B, H // HS, S // T),
        in_specs=[blk(0), blk(1), blk(2), blk(3), tab, tab, per_head(C), per_head(C), per_head(C), per_head(SUB)],
        out_specs=pl.BlockSpec((1, T, W), lambda b, h, t: (b, t, h)),
        out_shape=jax.ShapeDtypeStruct((B, S, D_GRP), bf16),
        scratch_shapes=[pltpu.VMEM((HS, Dh, Dh), f32)],
        compiler_params=pltpu.CompilerParams(dimension_semantics=("parallel", "parallel", "arbitrary"),
                                             vmem_limit_bytes=VMEM_LIMIT),
        name="retention",
    )(proj3, proj3, proj3, proj3, cos, sin, jnp.asarray(decay_in, f32), jnp.asarray(xi, f32),
      jnp.asarray(zeta, f32), jnp.asarray(gamma_c, f32))


def _dot3(a, b):
    ah, al = _split2(a)
    bh, bl = _split2(b)
    return (jnp.dot(ah, bh, preferred_element_type=f32) + jnp.dot(ah, bl, preferred_element_type=f32)
            + jnp.dot(al, bh, preferred_element_type=f32))


def _deltanet_body(q_ref, k_ref, v_ref, z_ref, qp_ref, kp_ref, vp_ref, ab_ref, cw_ref, alog_ref, dtb_ref, ng_ref,
                   o_ref, st_ref, *, T, HS):
    hb = pl.program_id(1)
    first = pl.program_id(2) == 0

    @pl.when(first)
    def _():
        st_ref[...] = jnp.zeros_like(st_ref)

    C, Dh = DN_CHUNK, HEAD_DIM
    nt = (((1,), (1,)), ((), ()))
    n_chunks = T // C
    ab = ab_ref[0]
    lane = lax.broadcasted_iota(jnp.int32, ab.shape, 1)

    def conv_silu(x_ref, prev_ref, j, hc):
        prev = jnp.where(first, 0.0, prev_ref[0, :, hc])
        xcat = jnp.concatenate([prev, x_ref[0, :, hc]], axis=0)
        w = cw_ref[:, j, hc]
        y = sum(w[i:i + 1, :] * xcat[SUB - (CONV_WIDTH - 1) + i:SUB - (CONV_WIDTH - 1) + i + T, :]
                for i in range(CONV_WIDTH))
        return y * jax.nn.sigmoid(y)

    def l2n(x):
        return x * lax.rsqrt(jnp.sum(x * x, axis=-1, keepdims=True) + L2NORM_EPS)

    units = [(hh, slice(c * C, (c + 1) * C)) for hh in range(HS) for c in range(n_chunks)]
    qs, ks_, vs_, betas, g_rows = [], [], [], [], []
    for hh in range(HS):
        hc = slice(hh * Dh, (hh + 1) * Dh)
        head = hb * HS + hh
        q_all = l2n(conv_silu(q_ref, qp_ref, 0, hc)) * Dh ** -0.5
        k_all = l2n(conv_silu(k_ref, kp_ref, 1, hc))
        v_all = conv_silu(v_ref, vp_ref, 2, hc)
        a_col = jnp.sum(jnp.where(lane == SMALL_DN_A + head, ab, 0.0), axis=-1, keepdims=True)
        b_col = jnp.sum(jnp.where(lane == SMALL_DN_A + DN_HEADS + head, ab, 0.0), axis=-1, keepdims=True)
        beta_all = jax.nn.sigmoid(b_col)
        g_all = -jnp.exp(alog_ref[hh, 0:1, :]) * jax.nn.softplus(a_col + dtb_ref[hh, 0:1, :])
        for c in range(n_chunks):
            rs = slice(c * C, (c + 1) * C)
            qs.append(q_all[rs]); ks_.append(k_all[rs]); vs_.append(v_all[rs])
            betas.append(beta_all[rs]); g_rows.append(g_all[rs])

    ri = lax.broadcasted_iota(jnp.int32, (C, C), 0)
    ci = lax.broadcasted_iota(jnp.int32, (C, C), 1)
    tril, strict = ri >= ci, ri > ci
    ltri = tril.astype(bf16)
    eye = (ri == ci).astype(f32)
    gcs = []
    for g_rows_u in g_rows:
        gh, gm, gl = _split3(g_rows_u)
        gcs.append(jnp.dot(ltri, gh, preferred_element_type=f32) + jnp.dot(ltri, gm, preferred_element_type=f32)
                   + jnp.dot(ltri, gl, preferred_element_type=f32))
    decays = [jnp.exp(jnp.where(tril, gc[:, :C] - gc[:, :C].T, -jnp.inf)) for gc in gcs]
    kbs = [k.astype(bf16) for k in ks_]
    kbetas = [k * b for k, b in zip(ks_, betas)]
    ms = [jnp.where(strict, lax.dot_general(kbeta.astype(bf16), kb, nt, preferred_element_type=f32) * d, 0.0)
          for kbeta, kb, d in zip(kbetas, kbs, decays)]
    attns = [jnp.where(tril, lax.dot_general(q.astype(bf16), kb, nt, preferred_element_type=f32) * d, 0.0).astype(bf16)
             for q, kb, d in zip(qs, kbs, decays)]
    t_invs, pws = [eye - m for m in ms], ms
    for _ in range(5):
        pws = [_dot3(pw, pw) for pw in pws]
        t_invs = [t + _dot3(t, pw) for t, pw in zip(t_invs, pws)]
    egcs = [jnp.exp(gc) for gc in gcs]
    tbs = [t.astype(bf16) for t in t_invs]
    us = [jnp.dot(tb, (v * b).astype(bf16), preferred_element_type=f32) for tb, v, b in zip(tbs, vs_, betas)]
    ws = [jnp.dot(tb, (kbeta * egc).astype(bf16), preferred_element_type=f32).astype(bf16)
          for tb, kbeta, egc in zip(tbs, kbetas, egcs)]
    states = [st_ref[hh] for hh in range(HS)]
    for c in range(n_chunks):
        for hh in range(HS):
            u, rs = hh * n_chunks + c, slice(c * C, (c + 1) * C)
            g_last = gcs[u][C - 1:C, :]
            q_dec = (qs[u] * egcs[u]).astype(bf16)
            k_dec = (ks_[u] * jnp.exp(g_last - gcs[u])).astype(bf16)
            sb = states[hh].astype(bf16)
            v_new = us[u] - jnp.dot(ws[u], sb, preferred_element_type=f32)
            vnb = v_new.astype(bf16)
            o = jnp.dot(q_dec, sb, preferred_element_type=f32) + jnp.dot(attns[u], vnb, preferred_element_type=f32)
            states[hh] = states[hh] * jnp.exp(g_last) + jnp.dot(k_dec.T, vnb, preferred_element_type=f32)
            o = o * lax.rsqrt(jnp.mean(o * o, axis=-1, keepdims=True) + NORM_EPS) * ng_ref[...]
            zt = z_ref[0, rs, hh * Dh:(hh + 1) * Dh]
            o_ref[0, rs, hh * Dh:(hh + 1) * Dh] = (o * (zt * jax.nn.sigmoid(zt))).astype(o_ref.dtype)
    for hh in range(HS):
        st_ref[hh] = states[hh]


def deltanet_mixer_pallas(proj3, conv_w, a_log, dt_bias, norm_g, B, S, T=256, HS=4):
    H, Dh = DN_HEADS, HEAD_DIM
    W = HS * Dh
    assert S % T == 0 and T % DN_CHUNK == 0 and H % HS == 0
    blk = lambda j: pl.BlockSpec((1, T, W), lambda b, h, t: (b, t, (OFF_DN + j * D_GRP) // W + h))
    prev = lambda j: pl.BlockSpec((1, SUB, W), lambda b, h, t: (b, jnp.maximum(t * (T // SUB) - 1, 0),
                                                               (OFF_DN + j * D_GRP) // W + h))
    per_head = pl.BlockSpec((HS, SUB, Dh), lambda b, h, t: (h, 0, 0))
    bcast = lambda x: jnp.broadcast_to(x.astype(f32)[:, None, None], (H, SUB, Dh))
    return pl.pallas_call(
        functools.partial(_deltanet_body, T=T, HS=HS),
        grid=(B, H // HS, S // T),
        in_specs=[blk(0), blk(1), blk(2), blk(3), prev(0), prev(1), prev(2),
                  pl.BlockSpec((1, T, LANES), lambda b, h, t: (b, t, OFF_SMALL // LANES)),
                  pl.BlockSpec((CONV_WIDTH, 3, W), lambda b, h, t: (0, 0, h)),
                  per_head, per_head,
                  pl.BlockSpec((1, Dh), lambda b, h, t: (0, 0))],
        out_specs=pl.BlockSpec((1, T, W), lambda b, h, t: (b, t, h)),
        out_shape=jax.ShapeDtypeStruct((B, S, D_GRP), bf16),
        scratch_shapes=[pltpu.VMEM((HS, Dh, Dh), f32)],
        compiler_params=pltpu.CompilerParams(dimension_semantics=("parallel", "parallel", "arbitrary"),
                                             vmem_limit_bytes=VMEM_LIMIT),
        name="deltanet",
    )(proj3, proj3, proj3, proj3, proj3, proj3, proj3, proj3, conv_w.reshape(CONV_WIDTH, 3, D_GRP),
      bcast(a_log), bcast(dt_bias), norm_g.reshape(1, Dh))


def _w_in_pieces():
    n0, r0, d0, w0 = 0, NSA_COLS, NSA_COLS + RET_COLS, NSA_COLS + RET_COLS + DN_COLS
    n_gate = 3 * NSA_HEADS
    return [
        (OFF_RW, w0, 3 * D_GRP),
        (OFF_RET, r0, 4 * D_GRP),
        (OFF_DN, d0, 3 * D_GRP),
        (OFF_DN + 3 * D_GRP, d0 + 3 * D_GRP + 2 * DN_HEADS, D_GRP),
        (OFF_NSA_Q, n0, D_GRP + 6 * NSA_KV_W),
        (OFF_LORA, w0 + 3 * D_GRP, RW_LORA),
        (OFF_SMALL, n0 + D_GRP + 6 * NSA_KV_W, n_gate),
        (OFF_SMALL + SMALL_DN_A, d0 + 3 * D_GRP, 2 * DN_HEADS),
    ]


def _pack_w_in_body(w_ref, o_ref):
    cols = o_ref.shape[2]
    for off, width in ((OFF_LORA + RW_LORA, RW_LORA_PAD - RW_LORA), (OFF_SMALL, LANES)):
        o_ref[0, off:off + width, :] = jnp.zeros((width, cols), o_ref.dtype)
    for dst, src, width in _w_in_pieces():
        o_ref[0, dst:dst + width, :] = w_ref[0, src:src + width, :].astype(o_ref.dtype)


def pack_w_in_t(w_in, tc=256):
    L, K = w_in.shape[0], w_in.shape[1]
    w_t = jnp.swapaxes(w_in, 1, 2)
    return pl.pallas_call(
        _pack_w_in_body,
        grid=(L, K // tc),
        in_specs=[pl.BlockSpec((1, P_TOTAL, tc), lambda l, i: (l, 0, i))],
        out_specs=pl.BlockSpec((1, P_PAD, tc), lambda l, i: (l, 0, i)),
        out_shape=jax.ShapeDtypeStruct((L, P_PAD, K), bf16),
        compiler_params=pltpu.CompilerParams(dimension_semantics=("parallel", "parallel"),
                                             vmem_limit_bytes=VMEM_LIMIT),
        name="pack_w_in",
    )(w_t)


def hybrid_mixer(h, hn, B, S, layer, w_in_packed, w_out, ck_w1, ck_w2, ck_pe, cv_w1, cv_w2, cv_pe, dn_conv, dn_a_log,
                 dn_dt_bias, dn_norm, rw_mu, rw_w0, rw_w2, rw_a0, rw_a2, rw_g2, rw_kk, rw_ka, rw_rk, rw_ln_g, rw_ln_b):
    M = B * S
    proj = matmul_nt(hn, w_in_packed, layer, name="in_proj")
    p3 = proj.reshape(B, S, P_PAD)
    o_nsa = nsa_mixer_pallas(proj, ck_w1, ck_w2, ck_pe, cv_w1, cv_w2, cv_pe, B, S)
    o_ret = retention_mixer_pallas(p3, B, S)
    o_dn = deltanet_mixer_pallas(p3, dn_conv, dn_a_log, dn_dt_bias, dn_norm, B, S)
    o_rw = rwkv7_mixer(proj, OFF_RW // (3 * D_GRP), OFF_LORA // RW_LORA_PAD, rw_mu, rw_w0, rw_w2, rw_a0, rw_a2,
                       rw_g2, rw_kk, rw_ka, rw_rk, rw_ln_g, rw_ln_b, B, S)
    parts = [o.reshape(M, D_GRP) for o in (o_nsa, o_ret, o_dn, o_rw)]
    return out_proj_residual(parts, w_out.astype(bf16), h)


def kernel(x, p, norm_mix, w_in, w_out, nsa_ck_w1, nsa_ck_w2, nsa_ck_pe, nsa_cv_w1, nsa_cv_w2, nsa_cv_pe, dn_conv, dn_a_log, dn_dt_bias, dn_norm, rw_mu, rw_w0, rw_w2, rw_a0, rw_a2, rw_g2, rw_kk, rw_ka, rw_rk, rw_ln_g, rw_ln_b, norm_moe, moe_router_grp, moe_router_grp_b, moe_router_exp, moe_router_exp_b, moe_w_gate, moe_w_up, moe_w_down, norm_ple, ple_w, ple_gate, norm_final):
    B, S, D = x.shape
    M = B * S
    h = x.reshape(M, D)
    w_in_packed = pack_w_in_t(w_in)
    for i in range(DEPTH):
        hn = rms_norm_rows(h, norm_mix[i], bf16)
        h = hybrid_mixer(h, hn, B, S, i, w_in_packed, w_out[i], nsa_ck_w1[i], nsa_ck_w2[i], nsa_ck_pe[i], nsa_cv_w1[i],
                         nsa_cv_w2[i], nsa_cv_pe[i], dn_conv[i], dn_a_log[i], dn_dt_bias[i], dn_norm[i],
                         rw_mu[i], rw_w0[i], rw_w2[i], rw_a0[i], rw_a2[i], rw_g2[i], rw_kk[i], rw_ka[i],
                         rw_rk[i], rw_ln_g[i], rw_ln_b[i])
        hn = rms_norm_rows(h, norm_moe[i], bf16)
        h = h + hier_moe(hn, moe_router_grp[i], moe_router_grp_b[i], moe_router_exp[i], moe_router_exp_b[i],
                         moe_w_gate, moe_w_up, moe_w_down, i)
        hn = rms_norm_rows(h, norm_ple[i], bf16)
        h = ple_residual(hn, ple_gate[i].astype(bf16), p[i].reshape(M, PLE_DIM), ple_w[i].astype(bf16), h)
    return rms_norm_rows(h, norm_final, f32).reshape(B, S, D)
```

```python
import functools

import jax
import jax.numpy as jnp
import numpy as np
from jax import lax
from jax.experimental import pallas as pl
from jax.experimental.pallas import tpu as pltpu

D_MODEL = 4096
DEPTH = 2
f32 = jnp.float32
bf16 = jnp.bfloat16
D_MIX = D_MODEL
D_GRP = D_MIX // 4
HEAD_DIM = 128
NORM_EPS = 1e-6
L2NORM_EPS = 1e-6
NSA_HEADS = D_GRP // HEAD_DIM
NSA_KV_HEADS = 2
NSA_GROUP = NSA_HEADS // NSA_KV_HEADS
NSA_KV_W = NSA_KV_HEADS * HEAD_DIM
CMP_LEN = 32
CMP_STRIDE = 16
SEL_LEN = 64
N_SEL = 16
N_LOCAL_FORCED = 2
FORCED_SCORE = 1e4
WINDOW = 512
ROPE_THETA = 500000.0
ROPE_DIMS = HEAD_DIM // 4
RET_HEADS = D_GRP // HEAD_DIM
RET_CHUNK = 128
RET_THETA = 10000.0
DN_HEADS = D_GRP // HEAD_DIM
DN_CHUNK = 64
CONV_WIDTH = 4
RW_HEAD_DIM = 64
RW_DECAY_LORA = 64
RW_AAA_LORA = 64
RW_GATE_LORA = 160
RW_LN_EPS = 64e-5
N_GROUPS = 4
EXPERTS_PER_GROUP = 8
N_EXPERTS = N_GROUPS * EXPERTS_PER_GROUP
TOP_K = 2
D_EXPERT = 768
ROW_BLOCK = 256
PLE_DIM = 256
NSA_COLS = D_GRP + 6 * NSA_KV_W + 3 * NSA_HEADS
RET_COLS = 4 * D_GRP
DN_COLS = 4 * D_GRP + 2 * DN_HEADS
RW_COLS = 3 * D_GRP + RW_DECAY_LORA + RW_AAA_LORA + RW_GATE_LORA
P_TOTAL = NSA_COLS + RET_COLS + DN_COLS + RW_COLS

RW_LORA = RW_DECAY_LORA + RW_AAA_LORA + RW_GATE_LORA
RW_LORA_PAD = 384
RW_GROUPS_PER_BODY = 4
LANES = 128
SUB = 8

OFF_RW = 0
OFF_RET = 3 * D_GRP
OFF_DN = OFF_RET + 4 * D_GRP
OFF_NSA_Q = OFF_DN + 4 * D_GRP
OFF_NSA_KV = OFF_NSA_Q + D_GRP
OFF_LORA = OFF_NSA_KV + 6 * NSA_KV_W
OFF_SMALL = OFF_LORA + RW_LORA_PAD
SMALL_DN_A = 32
P_PAD = OFF_SMALL + LANES
assert P_PAD == 14336 and OFF_LORA % RW_LORA_PAD == 0

V7X_VMEM_BYTES = 64 * 1024 * 1024
VMEM_LIMIT = V7X_VMEM_BYTES * 7 // 8


def _split2(x):
    h = x.astype(bf16)
    return h, (x - h.astype(f32)).astype(bf16)


def _split3(x):
    h = x.astype(bf16)
    r1 = x - h.astype(f32)
    m = r1.astype(bf16)
    return h, m, (r1 - m.astype(f32)).astype(bf16)


def _rwkv_prep_body(x_ref, xprev_ref, l_ref, lprev_ref, mu_ref, mul_ref, w0_ref, a0_ref, kkw_ref, ka_ref,
                    w2_ref, a2_ref, g2_ref, e_ref, r_ref, w_ref, k_ref, v_ref, kk_ref, b_ref, g_ref, *, tm, S):
    i = pl.program_id(0)
    first = (i * tm) % S == 0

    def token_mix(x, prev8, mu):
        rows = lax.broadcasted_iota(jnp.int32, x.shape, 0)
        prev_row = jnp.where(first, 0.0, prev8[SUB - 1:SUB, :])
        x_prev = jnp.where(rows == 0, prev_row, pltpu.roll(x, 1, axis=0))
        return x + (x_prev - x) * mu

    x = token_mix(x_ref[...], xprev_ref[...], mu_ref[...])
    lora = token_mix(l_ref[...], lprev_ref[...], mul_ref[...])
    r = x[:, 0:D_GRP]
    k = x[:, D_GRP:2 * D_GRP]
    v = x[:, 2 * D_GRP:3 * D_GRP]
    dw = jnp.dot(jnp.tanh(lora).astype(bf16), w2_ref[...], preferred_element_type=f32)
    da = jnp.dot(lora.astype(bf16), a2_ref[...], preferred_element_type=f32)
    g = jnp.dot(jax.nn.sigmoid(lora).astype(bf16), g2_ref[...], preferred_element_type=f32)
    w = -jax.nn.softplus(-(w0_ref[...] + dw)) - 0.5
    decay = jnp.exp(-jnp.exp(w))
    a = jax.nn.sigmoid(a0_ref[...] + da)
    kq = k * kkw_ref[...]
    sq_h, sq_l = _split2(kq * kq)
    ss = (jnp.dot(sq_h, e_ref[...], preferred_element_type=f32)
          + jnp.dot(sq_l, e_ref[...], preferred_element_type=f32))
    kk = kq * lax.rsqrt(ss + L2NORM_EPS)
    r_ref[...] = r
    w_ref[...] = decay
    k_ref[...] = k * (1.0 + (a - 1.0) * ka_ref[...])
    v_ref[...] = v
    kk_ref[...] = kk
    b_ref[...] = kk * a
    g_ref[...] = g


def rwkv_prep(proj, rkv_blk, lora_blk, mu, w0, w2, a0, a2, g2, k_k, k_a, S, tm=256):
    M = proj.shape[0]
    assert M % tm == 0 and S % tm == 0
    W3 = 3 * D_GRP
    mu_x = mu[:W3].reshape(1, W3)
    mu_l = jnp.pad(mu[W3:], (0, RW_LORA_PAD - RW_LORA)).reshape(1, RW_LORA_PAD)
    w2p = jnp.zeros((RW_LORA_PAD, D_GRP), f32).at[0:RW_DECAY_LORA].set(w2).astype(bf16)
    a2p = jnp.zeros((RW_LORA_PAD, D_GRP), f32).at[RW_DECAY_LORA:RW_DECAY_LORA + RW_AAA_LORA].set(a2).astype(bf16)
    g2p = jnp.zeros((RW_LORA_PAD, D_GRP), f32).at[RW_DECAY_LORA + RW_AAA_LORA:RW_LORA].set(g2).astype(bf16)
    hid = np.arange(D_GRP) // RW_HEAD_DIM
    e = jnp.asarray(hid[:, None] == hid[None, :], bf16)
    row = lambda t: t.reshape(1, D_GRP)
    vec = pl.BlockSpec((1, D_GRP), lambda i: (0, 0))
    wspec = pl.BlockSpec((RW_LORA_PAD, D_GRP), lambda i: (0, 0))
    ospec = pl.BlockSpec((tm, D_GRP), lambda i: (i, 0))
    prev_row_blk = lambda i: jnp.maximum(i * (tm // SUB) - 1, 0)
    return pl.pallas_call(
        functools.partial(_rwkv_prep_body, tm=tm, S=S),
        grid=(M // tm,),
        in_specs=[pl.BlockSpec((tm, W3), lambda i: (i, rkv_blk)),
                  pl.BlockSpec((SUB, W3), lambda i: (prev_row_blk(i), rkv_blk)),
                  pl.BlockSpec((tm, RW_LORA_PAD), lambda i: (i, lora_blk)),
                  pl.BlockSpec((SUB, RW_LORA_PAD), lambda i: (prev_row_blk(i), lora_blk)),
                  pl.BlockSpec((1, W3), lambda i: (0, 0)),
                  pl.BlockSpec((1, RW_LORA_PAD), lambda i: (0, 0)),
                  vec, vec, vec, vec, wspec, wspec, wspec,
                  pl.BlockSpec((D_GRP, D_GRP), lambda i: (0, 0))],
        out_specs=[ospec] * 7,
        out_shape=[jax.ShapeDtypeStruct((M, D_GRP), f32)] * 7,
        compiler_params=pltpu.CompilerParams(dimension_semantics=("parallel",), vmem_limit_bytes=VMEM_LIMIT),
        name="rwkv_prep",
    )(proj, proj, proj, proj, mu_x, mu_l, row(w0), row(a0), row(k_k), row(k_a), w2p, a2p, g2p, e)


def _rwkv_scan_body(r_ref, w_ref, k_ref, v_ref, kk_ref, b_ref, g_ref, lng_ref, lnb_ref, rk_ref, o_ref, s_ref,
                    vc_ref, y_ref, *, G, T):
    tb = pl.program_id(2)

    @pl.when(tb == 0)
    def _():
        s_ref[...] = jnp.zeros_like(s_ref)

    Dh = RW_HEAD_DIM
    lo = lax.broadcasted_iota(jnp.int32, (Dh, LANES), 1) < Dh
    lo8 = lax.broadcasted_iota(jnp.int32, (SUB, LANES), 1) < Dh
    sub_iota = lax.broadcasted_iota(jnp.int32, (4 * SUB, LANES), 0)
    row_iota = lax.broadcasted_iota(jnp.int32, (SUB, LANES), 0)

    def seg_sum(x, m):
        s_lo = jnp.sum(jnp.where(m, x, 0.0), axis=-1, keepdims=True)
        s_hi = jnp.sum(jnp.where(m, 0.0, x), axis=-1, keepdims=True)
        return jnp.where(m, s_lo, s_hi)

    cols = [slice(p * LANES, (p + 1) * LANES) for p in range(G)]
    n_groups = T // SUB

    k_iota = lax.broadcasted_iota(jnp.int32, (8 * SUB, LANES), 0)
    k_head = k_iota // (4 * SUB) == lax.broadcasted_iota(jnp.int32, (8 * SUB, LANES), 1) // Dh
    k_piece = (k_iota // SUB) % 4

    def v_pieces(gi):
        t0 = pl.multiple_of(gi * SUB, SUB)
        vts = []
        for p in range(G):
            vh, vm, vl = _split3(v_ref[0, pl.ds(t0, SUB), cols[p]])
            vp = jnp.concatenate([vh, vm, vl, jnp.zeros_like(vh)], axis=0)
            vts.append(jnp.concatenate([vp[:, :Dh], vp[:, Dh:]], axis=0).T)
        return vts

    def v_columns(vts, i, slot):
        sel = (k_head & (k_piece < 3) & (k_iota % SUB == i)).astype(bf16)
        for p in range(G):
            vc_ref[slot, i, p] = jnp.dot(vts[p], sel, preferred_element_type=f32)

    vts0 = v_pieces(0)
    for i in range(SUB):
        v_columns(vts0, i, 0)

    def run_group(gi, slot, states):
        t0 = pl.multiple_of(gi * SUB, SUB)
        vts_next = v_pieces(jnp.minimum(gi + 1, n_groups - 1))
        r8b = [r_ref[0, pl.ds(t0, SUB), cols[p]].astype(bf16) for p in range(G)]
        yacc = [jnp.zeros((SUB, LANES), f32) for _ in range(G)]
        tiles = [[ref[0, pl.ds(t0, SUB), cols[p]] for ref in (kk_ref, w_ref, b_ref, k_ref)] for p in range(G)]
        for i in range(SUB):
            for p in range(G):
                kk_r, w_r, b_r, k_r = [jnp.broadcast_to(t8[i:i + 1, :], (Dh, LANES)) for t8 in tiles[p]]
                S = states[p]
                sab = -seg_sum(S * kk_r, lo)
                states[p] = S * w_r + sab * b_r + vc_ref[slot, i, p] * k_r
            for p in range(G):
                Sb = states[p].astype(bf16)
                zero = jnp.zeros_like(Sb)
                mt = jnp.concatenate([jnp.where(lo, Sb, zero), jnp.where(lo, zero, Sb)], axis=0)
                y8 = lax.dot_general(r8b[p], mt, (((1,), (1,)), ((), ())), preferred_element_type=f32)
                yacc[p] = jnp.where(row_iota == i, y8, yacc[p])
            v_columns(vts_next, i, 1 - slot)
        for p in range(G):
            y_ref[pl.ds(t0, SUB), cols[p]] = yacc[p]
        return states

    def group_batch(gj, carry):
        states = [s_ref[p] for p in range(G)]
        for u in range(RW_GROUPS_PER_BODY):
            states = run_group(RW_GROUPS_PER_BODY * gj + u, u % 2, states)
        for p in range(G):
            s_ref[p] = states[p]
        return carry

    assert n_groups % RW_GROUPS_PER_BODY == 0 and RW_GROUPS_PER_BODY % 2 == 0
    lax.fori_loop(0, n_groups // RW_GROUPS_PER_BODY, group_batch, 0)

    lo_t = lax.broadcasted_iota(jnp.int32, (T, LANES), 1) < Dh
    for p in range(G):
        y = y_ref[:, cols[p]]
        mean = seg_sum(y, lo_t) * (1.0 / Dh)
        d = y - mean
        var = seg_sum(d * d, lo_t) * (1.0 / Dh)
        yn = d * lax.rsqrt(var + RW_LN_EPS) * lng_ref[:, cols[p]] + lnb_ref[:, cols[p]]
        bonus = seg_sum(r_ref[0, :, cols[p]] * k_ref[0, :, cols[p]] * rk_ref[:, cols[p]], lo_t) * v_ref[0, :, cols[p]]
        o_ref[0, :, cols[p]] = (yn + bonus) * g_ref[0, :, cols[p]]


def rwkv_scan(r, w, k, v, kk, b, g, ln_g, ln_b, r_k, G=8, T=512):
    B, S, D = r.shape
    n_pairs = D // LANES
    assert n_pairs % G == 0 and S % T == 0
    spec = pl.BlockSpec((1, T, G * LANES), lambda bi, pi, ti: (bi, ti, pi))
    vec = pl.BlockSpec((1, G * LANES), lambda bi, pi, ti: (0, pi))
    row = lambda t: t.reshape(1, D)
    return pl.pallas_call(
        functools.partial(_rwkv_scan_body, G=G, T=T),
        grid=(B, n_pairs // G, S // T),
        in_specs=[spec] * 7 + [vec] * 3,
        out_specs=spec,
        out_shape=jax.ShapeDtypeStruct((B, S, D), f32),
        scratch_shapes=[pltpu.VMEM((G, RW_HEAD_DIM, LANES), f32),
                        pltpu.VMEM((2, SUB, G, RW_HEAD_DIM, LANES), f32),
                        pltpu.VMEM((T, G * LANES), f32)],
        compiler_params=pltpu.CompilerParams(dimension_semantics=("parallel", "parallel", "arbitrary"),
                                             vmem_limit_bytes=VMEM_LIMIT),
        name="rwkv_scan",
    )(r, w, k, v, kk, b, g, row(ln_g), row(ln_b), row(r_k))


def rwkv7_mixer(proj, rkv_blk, lora_blk, mu, w0, w2, a0, a2, g2, k_k, k_a, r_k, ln_g, ln_b, B, S):
    outs = rwkv_prep(proj, rkv_blk, lora_blk, mu, w0, w2, a0, a2, g2, k_k, k_a, S)
    r, w, k, v, kk, b, g = [t.reshape(B, S, D_GRP) for t in outs]
    return rwkv_scan(r, w, k, v, kk, b, g, ln_g, ln_b, r_k.reshape(-1))


def _mm_nt_body(a_ref, bt_ref, o_ref, acc_ref):
    k = pl.program_id(2)

    @pl.when(k == 0)
    def _():
        acc_ref[...] = jnp.zeros_like(acc_ref)

    acc_ref[...] += lax.dot_general(a_ref[...], bt_ref[...], (((1,), (1,)), ((), ())), preferred_element_type=f32)

    @pl.when(k == pl.num_programs(2) - 1)
    def _():
        o_ref[...] = acc_ref[...].astype(o_ref.dtype)


def matmul_nt(a, bt, layer, out_dtype=f32, tm=1024, tn=1024, tk=2048, name="matmul_nt"):
    M, K = a.shape
    _, N, _ = bt.shape
    assert M % tm == 0 and N % tn == 0 and K % tk == 0, (M, N, K, tm, tn, tk)
    return pl.pallas_call(
        _mm_nt_body,
        grid=(M // tm, N // tn, K // tk),
        in_specs=[pl.BlockSpec((tm, tk), lambda i, j, k: (i, k)),
                  pl.BlockSpec((None, tn, tk), lambda i, j, k: (layer, j, k))],
        out_specs=pl.BlockSpec((tm, tn), lambda i, j, k: (i, j)),
        out_shape=jax.ShapeDtypeStruct((M, N), out_dtype),
        scratch_shapes=[pltpu.VMEM((tm, tn), f32)],
        compiler_params=pltpu.CompilerParams(
            dimension_semantics=("parallel", "parallel", "arbitrary"), vmem_limit_bytes=VMEM_LIMIT),
        name=name,
    )(a, bt)


def _rms_norm_body(x_ref, g_ref, o_ref):
    x = x_ref[...]
    y = x * lax.rsqrt(jnp.mean(x * x, axis=-1, keepdims=True) + NORM_EPS)
    o_ref[...] = (y * g_ref[...]).astype(o_ref.dtype)


def rms_norm_rows(x, g, out_dtype, tm=256):
    M, D = x.shape
    return pl.pallas_call(
        _rms_norm_body,
        grid=(M // tm,),
        in_specs=[pl.BlockSpec((tm, D), lambda i: (i, 0)), pl.BlockSpec((1, D), lambda i: (0, 0))],
        out_specs=pl.BlockSpec((tm, D), lambda i: (i, 0)),
        out_shape=jax.ShapeDtypeStruct((M, D), out_dtype),
        compiler_params=pltpu.CompilerParams(dimension_semantics=("parallel",), vmem_limit_bytes=VMEM_LIMIT),
        name="rms_norm",
    )(x, g.reshape(1, D).astype(f32))


def _out_proj_body(a0_ref, a1_ref, a2_ref, a3_ref, b_ref, h_ref, o_ref, acc_ref):
    k = pl.program_id(2)

    @pl.when(k == 0)
    def _():
        acc_ref[...] = h_ref[...]

    for j, a_ref in enumerate((a0_ref, a1_ref, a2_ref, a3_ref)):
        @pl.when(k == j)
        def _(a_ref=a_ref):
            acc_ref[...] += jnp.dot(a_ref[...].astype(bf16), b_ref[...], preferred_element_type=f32)

    @pl.when(k == pl.num_programs(2) - 1)
    def _():
        o_ref[...] = acc_ref[...]


def out_proj_residual(parts, w, h, tm=1024, tn=1024):
    M, D = h.shape
    aspec = pl.BlockSpec((tm, D_GRP), lambda i, j, k: (i, 0))
    return pl.pallas_call(
        _out_proj_body,
        grid=(M // tm, D // tn, len(parts)),
        in_specs=[aspec] * 4 + [pl.BlockSpec((D_GRP, tn), lambda i, j, k: (k, j)),
                                pl.BlockSpec((tm, tn), lambda i, j, k: (i, j))],
        out_specs=pl.BlockSpec((tm, tn), lambda i, j, k: (i, j)),
        out_shape=jax.ShapeDtypeStruct((M, D), f32),
        scratch_shapes=[pltpu.VMEM((tm, tn), f32)],
        compiler_params=pltpu.CompilerParams(
            dimension_semantics=("parallel", "parallel", "arbitrary"), vmem_limit_bytes=VMEM_LIMIT),
        name="out_proj",
    )(*parts, w, h)


def _ple_body(a_ref, b_ref, p_ref, pw_ref, h_ref, o_ref, acc_ref):
    k = pl.program_id(2)

    @pl.when(k == 0)
    def _():
        acc_ref[...] = jnp.zeros_like(acc_ref)

    acc_ref[...] += jnp.dot(a_ref[...], b_ref[...], preferred_element_type=f32)

    @pl.when(k == pl.num_programs(2) - 1)
    def _():
        emb = jnp.dot(p_ref[...].astype(bf16), pw_ref[...], preferred_element_type=f32)
        o_ref[...] = h_ref[...] + emb * jax.nn.sigmoid(acc_ref[...])


def ple_residual(hn, gate_w, p, ple_w, h, tm=1024, tn=1024, tk=2048):
    M, D = h.shape
    return pl.pallas_call(
        _ple_body,
        grid=(M // tm, D // tn, D // tk),
        in_specs=[pl.BlockSpec((tm, tk), lambda i, j, k: (i, k)),
                  pl.BlockSpec((tk, tn), lambda i, j, k: (k, j)),
                  pl.BlockSpec((tm, PLE_DIM), lambda i, j, k: (i, 0)),
                  pl.BlockSpec((PLE_DIM, tn), lambda i, j, k: (0, j)),
                  pl.BlockSpec((tm, tn), lambda i, j, k: (i, j))],
        out_specs=pl.BlockSpec((tm, tn), lambda i, j, k: (i, j)),
        out_shape=jax.ShapeDtypeStruct((M, D), f32),
        scratch_shapes=[pltpu.VMEM((tm, tn), f32)],
        compiler_params=pltpu.CompilerParams(
            dimension_semantics=("parallel", "parallel", "arbitrary"), vmem_limit_bytes=VMEM_LIMIT),
        name="ple",
    )(hn, gate_w, p, ple_w, h)


def _moe_router_body(x_ref, w_ref, b_ref, id_ref, wt_ref):
    logits = jnp.dot(x_ref[...].astype(bf16), w_ref[...], preferred_element_type=f32) + b_ref[...]
    lane = lax.broadcasted_iota(jnp.int32, logits.shape, 1).astype(f32)
    ninf = -jnp.inf

    def top1(vals):
        m = jnp.max(vals, axis=-1, keepdims=True)
        idx = jnp.min(jnp.where(vals == m, lane, float(LANES)), axis=-1, keepdims=True)
        return m, idx

    gl = jnp.where(lane < N_GROUPS, logits, ninf)
    g_max, g_idx = top1(gl)
    g_w = 1.0 / jnp.sum(jnp.exp(gl - g_max), axis=-1, keepdims=True)
    lo = N_GROUPS + g_idx * EXPERTS_PER_GROUP
    el = jnp.where((lane >= lo) & (lane < lo + EXPERTS_PER_GROUP), logits, ninf)
    e_max, _ = top1(el)
    pe = jnp.exp(el - e_max)
    p = pe / jnp.sum(pe, axis=-1, keepdims=True)
    p = jnp.where(el == ninf, ninf, p)
    p1, i1 = top1(p)
    p2, i2 = top1(jnp.where(lane == i1, ninf, p))
    scale = g_w / (p1 + p2)
    id_ref[...] = jnp.where(lane == 0, i1 - N_GROUPS, jnp.where(lane == 1, i2 - N_GROUPS, 0.0)).astype(jnp.int32)
    wt_ref[...] = jnp.where(lane == 0, p1 * scale, jnp.where(lane == 1, p2 * scale, 0.0))


def moe_router(xb, router_grp, router_grp_b, router_exp, router_exp_b, tm=512):
    M, D = xb.shape
    n_log = N_GROUPS + N_EXPERTS
    w = jnp.pad(jnp.concatenate([router_grp, router_exp], axis=1), ((0, 0), (0, LANES - n_log))).astype(bf16)
    b = jnp.pad(jnp.concatenate([router_grp_b, router_exp_b]), (0, LANES - n_log)).reshape(1, LANES).astype(f32)
    ids, wts = pl.pallas_call(
        _moe_router_body,
        grid=(M // tm,),
        in_specs=[pl.BlockSpec((tm, D), lambda i: (i, 0)), pl.BlockSpec((D, LANES), lambda i: (0, 0)),
                  pl.BlockSpec((1, LANES), lambda i: (0, 0))],
        out_specs=[pl.BlockSpec((tm, LANES), lambda i: (i, 0))] * 2,
        out_shape=[jax.ShapeDtypeStruct((M, LANES), jnp.int32), jax.ShapeDtypeStruct((M, LANES), f32)],
        compiler_params=pltpu.CompilerParams(dimension_semantics=("parallel",), vmem_limit_bytes=VMEM_LIMIT),
        name="moe_router",
    )(xb, w, b)
    return ids[:, :TOP_K], wts[:, :TOP_K]


def _moe_up_body(be_ref, nu_ref, x_ref, wg_ref, wu_ref, h_ref, *, tf):
    i = pl.program_id(0)

    @pl.when(i < nu_ref[0])
    def _():
        x = x_ref[...].astype(bf16)
        for f in range(D_EXPERT // tf):
            c = slice(f * tf, (f + 1) * tf)
            hg = jnp.dot(x, wg_ref[0, 0, :, c].astype(bf16), preferred_element_type=f32)
            hu = jnp.dot(x, wu_ref[0, 0, :, c].astype(bf16), preferred_element_type=f32)
            h_ref[:, c] = (hg * jax.nn.sigmoid(hg) * hu).astype(bf16)

    @pl.when(i >= nu_ref[0])
    def _():
        h_ref[...] = jnp.zeros_like(h_ref)


def _moe_down_body(be_ref, nu_ref, h_ref, wd_ref, wt_ref, y_ref):
    i = pl.program_id(0)

    @pl.when(i < nu_ref[0])
    def _():
        y_ref[...] = jnp.dot(h_ref[...], wd_ref[0, 0].astype(bf16), preferred_element_type=f32) * wt_ref[...]

    @pl.when(i >= nu_ref[0])
    def _():
        y_ref[...] = jnp.zeros_like(y_ref)


def moe_experts(x_pad, w_pad, blk_expert, n_used, wg, wu, wd, layer, tf=256):
    A_pad, D = x_pad.shape
    n_blk = A_pad // ROW_BLOCK
    wmap = lambda i, be, nu: (layer, be[i], 0, 0)
    wspec = pl.BlockSpec((1, 1, D, D_EXPERT), wmap, pipeline_mode=pl.Buffered(1))
    wspec2 = pl.BlockSpec((1, 1, D, D_EXPERT), wmap)
    h = pl.pallas_call(
        functools.partial(_moe_up_body, tf=tf),
        grid_spec=pltpu.PrefetchScalarGridSpec(
            num_scalar_prefetch=2,
            grid=(n_blk,),
            in_specs=[pl.BlockSpec((ROW_BLOCK, D), lambda i, be, nu: (i, 0)), wspec2, wspec],
            out_specs=pl.BlockSpec((ROW_BLOCK, D_EXPERT), lambda i, be, nu: (i, 0)),
        ),
        out_shape=jax.ShapeDtypeStruct((A_pad, D_EXPERT), bf16),
        compiler_params=pltpu.CompilerParams(dimension_semantics=("arbitrary",), vmem_limit_bytes=VMEM_LIMIT),
        name="moe_up",
    )(blk_expert, n_used, x_pad, wg, wu)
    return pl.pallas_call(
        _moe_down_body,
        grid_spec=pltpu.PrefetchScalarGridSpec(
            num_scalar_prefetch=2,
            grid=(n_blk,),
            in_specs=[pl.BlockSpec((ROW_BLOCK, D_EXPERT), lambda i, be, nu: (i, 0)),
                      pl.BlockSpec((1, 1, D_EXPERT, D), lambda i, be, nu: (layer, be[i], 0, 0)),
                      pl.BlockSpec((ROW_BLOCK, 1), lambda i, be, nu: (i, 0))],
            out_specs=pl.BlockSpec((ROW_BLOCK, D), lambda i, be, nu: (i, 0)),
        ),
        out_shape=jax.ShapeDtypeStruct((A_pad, D), f32),
        compiler_params=pltpu.CompilerParams(dimension_semantics=("arbitrary",), vmem_limit_bytes=VMEM_LIMIT),
        name="moe_down",
    )(blk_expert, n_used, h, wd, w_pad.reshape(A_pad, 1))


def hier_moe(xb, router_grp, router_grp_b, router_exp, router_exp_b, w_gate, w_up, w_down, layer):
    M, D = xb.shape
    expert_ids, weights = moe_router(xb, router_grp, router_grp_b, router_exp, router_exp_b)
    A = M * TOP_K
    e_flat = expert_ids.reshape(A)
    onehot = (e_flat[:, None] == jnp.arange(N_EXPERTS)[None, :]).astype(jnp.int32)
    rank = jnp.take_along_axis(jnp.cumsum(onehot, axis=0), e_flat[:, None], axis=1)[:, 0] - 1
    counts = jnp.sum(onehot, axis=0)
    padded = (counts + ROW_BLOCK - 1) // ROW_BLOCK * ROW_BLOCK
    end_pad = jnp.cumsum(padded)
    dest = (end_pad - padded)[e_flat] + rank
    n_blk = -(-A // ROW_BLOCK) + N_EXPERTS
    A_pad = n_blk * ROW_BLOCK
    asg = jnp.full((A_pad,), -1, jnp.int32).at[dest].set(jnp.arange(A, dtype=jnp.int32))
    src = jnp.where(asg >= 0, asg, jnp.arange(A_pad, dtype=jnp.int32) % A)
    tok_pad = src // TOP_K
    w_pad = jnp.where(asg >= 0, weights.reshape(A)[src], 0.0)
    blk_expert = jnp.minimum(jnp.searchsorted(end_pad, jnp.arange(n_blk) * ROW_BLOCK, side='right'),
                             N_EXPERTS - 1).astype(jnp.int32)
    n_used = (end_pad[-1:] // ROW_BLOCK).astype(jnp.int32)
    x_pad = xb[tok_pad]
    y_pad = moe_experts(x_pad, w_pad, blk_expert, n_used, w_gate, w_up, w_down, layer)
    rows = dest.reshape(M, TOP_K)
    return y_pad[rows[:, 0]] + y_pad[rows[:, 1]]


def _rope_tables(S, theta, rot_dims, width):
    half = rot_dims // 2
    freqs = theta ** (-np.arange(half, dtype=np.float64) / half)
    ang = np.arange(S, dtype=np.float64)[:, None] * freqs[None, :]
    cos = np.ones((S, width), np.float64)
    sin_lo = np.zeros((S, width), np.float64)
    sin_hi = np.zeros((S, width), np.float64)
    cos[:, :half] = np.cos(ang)
    cos[:, half:rot_dims] = np.cos(ang)
    sin_lo[:, :half] = -np.sin(ang)
    sin_hi[:, half:rot_dims] = np.sin(ang)
    return [jnp.asarray(t, f32) for t in (cos, sin_lo, sin_hi)]


def _rope128(x, cos, sin_lo, sin_hi, half):
    return x * cos + pltpu.roll(x, LANES - half, axis=1) * sin_lo + pltpu.roll(x, half, axis=1) * sin_hi


def _nsa_prep_body(q_ref, kv_ref, cos_ref, sl_ref, sh_ref, qo_ref, kvo_ref, co_ref):
    cos, sl, sh = cos_ref[...], sl_ref[...], sh_ref[...]
    half = ROPE_DIMS // 2
    scale = HEAD_DIM ** -0.5
    for h in range(NSA_HEADS):
        c = slice(h * LANES, (h + 1) * LANES)
        qo_ref[:, c] = (_rope128(q_ref[:, c], cos, sl, sh, half) * scale).astype(bf16)
    for j in range(6):
        for g in range(NSA_KV_HEADS):
            c = slice((j * NSA_KV_HEADS + g) * LANES, (j * NSA_KV_HEADS + g + 1) * LANES)
            x = kv_ref[:, c]
            if j % 2 == 0:
                x = _rope128(x, cos, sl, sh, half)
            if j < 2:
                co_ref[:, c] = x.astype(bf16)
            else:
                kvo_ref[:, slice(c.start - 4 * LANES, c.stop - 4 * LANES)] = x.astype(bf16)


def nsa_prep(proj, S, tm=512):
    M = proj.shape[0]
    assert M % tm == 0 and S % tm == 0
    tabs = _rope_tables(S, ROPE_THETA, ROPE_DIMS, LANES)
    tspec = pl.BlockSpec((tm, LANES), lambda i: (i % (S // tm), 0))
    return pl.pallas_call(
        _nsa_prep_body,
        grid=(M // tm,),
        in_specs=[pl.BlockSpec((tm, D_GRP), lambda i: (i, OFF_NSA_Q // D_GRP)),
                  pl.BlockSpec((tm, 6 * NSA_KV_W), lambda i: (i, OFF_NSA_KV // (6 * NSA_KV_W))),
                  tspec, tspec, tspec],
        out_specs=[pl.BlockSpec((tm, D_GRP), lambda i: (i, 0)),
                   pl.BlockSpec((tm, 4 * NSA_KV_W), lambda i: (i, 0)),
                   pl.BlockSpec((tm, 2 * NSA_KV_W), lambda i: (i, 0))],
        out_shape=[jax.ShapeDtypeStruct((M, D_GRP), bf16), jax.ShapeDtypeStruct((M, 4 * NSA_KV_W), bf16),
                   jax.ShapeDtypeStruct((M, 2 * NSA_KV_W), bf16)],
        compiler_params=pltpu.CompilerParams(dimension_semantics=("parallel",), vmem_limit_bytes=VMEM_LIMIT),
        name="nsa_prep",
    )(proj, proj, *tabs)


def _nsa_compress_body(r_ref, w1_ref, w2_ref, pe_ref, o_ref):
    r = r_ref[0, 0, 0]
    half_w = CMP_STRIDE * HEAD_DIM
    a = jnp.dot(r, w1_ref[0, :half_w, :], preferred_element_type=f32)
    b = jnp.dot(r, w1_ref[0, half_w:, :], preferred_element_type=f32)
    n = r.shape[0]
    pe = jnp.dot(jnp.broadcast_to(pe_ref[0], (SUB, pe_ref.shape[-1])), w1_ref[0], preferred_element_type=f32)[0:1]
    pre = a + pltpu.roll(b, n - 1, axis=0) + pe
    hid = pre * jax.nn.sigmoid(pre)
    o_ref[0, 0, 0] = jnp.dot(hid.astype(bf16), w2_ref[0], preferred_element_type=f32).astype(bf16)


def nsa_compress(c_rot, w1s, w2s, pes, B, S):
    G, Dh = NSA_KV_HEADS, HEAD_DIM
    nc = S // CMP_STRIDE
    r = c_rot.reshape(B, nc, CMP_STRIDE, 2, G, Dh).transpose(3, 0, 4, 1, 2, 5).reshape(2, B, G, nc, CMP_STRIDE * Dh)
    w1 = jnp.stack(w1s).astype(bf16)
    w2 = jnp.stack(w2s).astype(bf16)
    pe = jnp.stack(pes).reshape(2, 1, CMP_LEN * Dh).astype(bf16)
    return pl.pallas_call(
        _nsa_compress_body,
        grid=(2, B, G),
        in_specs=[pl.BlockSpec((1, 1, 1, nc, CMP_STRIDE * Dh), lambda j, b, g: (j, b, g, 0, 0)),
                  pl.BlockSpec((1, CMP_LEN * Dh, Dh), lambda j, b, g: (j, 0, 0)),
                  pl.BlockSpec((1, Dh, Dh), lambda j, b, g: (j, 0, 0)),
                  pl.BlockSpec((1, 1, CMP_LEN * Dh), lambda j, b, g: (j, 0, 0))],
        out_specs=pl.BlockSpec((1, 1, 1, nc, Dh), lambda j, b, g: (j, b, g, 0, 0)),
        out_shape=jax.ShapeDtypeStruct((2, B, G, nc, Dh), bf16),
        compiler_params=pltpu.CompilerParams(dimension_semantics=("parallel", "parallel", "parallel"),
                                             vmem_limit_bytes=VMEM_LIMIT),
        name="nsa_compress",
    )(r, w1, w2, pe)


def _softmax_rows(s, mask):
    s = jnp.where(mask, s, -jnp.inf)
    m = jnp.max(s, axis=-1, keepdims=True)
    m = jnp.where(m == -jnp.inf, 0.0, m)
    e = jnp.exp(s - m)
    return e, jnp.maximum(jnp.sum(e, axis=-1, keepdims=True), jnp.finfo(f32).tiny)


def _nsa_attn_body(q_ref, cmp_ref, ks_ref, vs_ref, kw_ref, vw_ref, gate_ref, ovt_ref, o_ref, *, tq, tk, S):
    g = pl.program_id(1)
    qi = pl.program_id(2)
    R, Dh = NSA_GROUP, HEAD_DIM
    rows = R * tq
    n_blk = S // SEL_LEN
    n_cmp = S // CMP_STRIDE - 1
    q0 = qi * tq
    q4 = jnp.concatenate([q_ref[0, :, r * Dh:(r + 1) * Dh] for r in range(R)], axis=0)
    nt = (((1,), (1,)), ((), ()))

    def qpos(shape):
        return q0 + lax.broadcasted_iota(jnp.int32, shape, 0) % tq

    k_cmp, v_cmp = cmp_ref[0, 0, 0], cmp_ref[1, 0, 0]
    ncp = k_cmp.shape[0]
    s = lax.dot_general(q4, k_cmp, nt, preferred_element_type=f32)
    n_idx = lax.broadcasted_iota(jnp.int32, (rows, ncp), 1)
    mask = (n_idx * CMP_STRIDE + (CMP_LEN - 1) <= qpos((rows, ncp))) & (n_idx < n_cmp)
    e, den = _softmax_rows(s, mask)
    p = e / den
    o_cmp = jnp.dot(p.astype(bf16), v_cmp, preferred_element_type=f32)

    psum = p[0:tq]
    for r in range(1, R):
        psum = psum + p[r * tq:(r + 1) * tq]
    ph, plo = _split2(psum)
    imp = (lax.dot_general(ovt_ref[...], ph, nt, preferred_element_type=f32)
           + lax.dot_general(ovt_ref[...], plo, nt, preferred_element_type=f32))
    blk = lax.broadcasted_iota(jnp.int32, (n_blk, tq), 0)
    cur = (q0 + lax.broadcasted_iota(jnp.int32, (n_blk, tq), 1)) // SEL_LEN
    visible = blk <= cur
    forced = (blk == 0) | (visible & (blk > cur - N_LOCAL_FORCED))
    imp = jnp.where(forced, FORCED_SCORE, jnp.where(visible, imp, -1.0))
    sel_t = jnp.zeros((n_blk, tq), f32)
    blk_f = blk.astype(f32)
    for _ in range(min(N_SEL, n_blk)):
        m = jnp.max(imp, axis=0, keepdims=True)
        first = jnp.min(jnp.where(imp == m, blk_f, float(n_blk)), axis=0, keepdims=True)
        hit = blk_f == first
        sel_t = jnp.where(hit, 1.0, sel_t)
        imp = jnp.where(hit, -jnp.inf, imp)
    sel_q = jnp.concatenate([sel_t.T, jnp.ones((tq, LANES - n_blk), f32)], axis=1).astype(bf16)

    blocks_per_tile = tk // SEL_LEN

    big = float(2 ** 30)
    kb = sel_q.shape[1]

    def sel_step(kt, carry, causal):
        m_run, l_run, acc = carry
        k0 = pl.multiple_of(kt * tk, tk)
        kt_k = ks_ref[0, pl.ds(k0, tk), :]
        kt_v = vs_ref[0, pl.ds(k0, tk), :]
        s = lax.dot_general(q4, kt_k, nt, preferred_element_type=f32)
        row = lax.broadcasted_iota(jnp.int32, (kb, tk), 0)
        in_blk = row == kt * blocks_per_tile + lax.broadcasted_iota(jnp.int32, (kb, tk), 1) // SEL_LEN
        expand = jnp.where(in_blk, big, jnp.where(row == n_blk, -big, 0.0)).astype(bf16)
        bias = jnp.dot(sel_q, expand, preferred_element_type=f32)
        s = (s.reshape(R, tq, tk) + bias[None]).reshape(rows, tk)
        if causal:
            kpos = k0 + lax.broadcasted_iota(jnp.int32, (rows, tk), 1)
            s = jnp.where(kpos <= qpos((rows, tk)), s, -big)
        m_new = jnp.maximum(m_run, jnp.max(s, axis=-1, keepdims=True))
        alpha = jnp.exp(m_run - m_new)
        e = jnp.exp(s - m_new)
        l_new = alpha * l_run + jnp.sum(e, axis=-1, keepdims=True)
        acc = alpha * acc + jnp.dot(e.astype(bf16), kt_v, preferred_element_type=f32)
        return m_new, l_new, acc

    last_kt = (q0 + tq - 1) // tk
    init = (jnp.full((rows, 1), -jnp.inf, f32), jnp.zeros((rows, 1), f32), jnp.zeros((rows, Dh), f32))
    carry = lax.fori_loop(0, last_kt, functools.partial(sel_step, causal=False), init)
    _, l_sel, acc_sel = sel_step(last_kt, carry, causal=True)
    o_sel = acc_sel / l_sel

    span = WINDOW + tq
    w0 = pl.multiple_of(jnp.maximum(q0 - WINDOW, 0), tq)
    kwin = kw_ref[0, pl.ds(w0, span), :]
    vwin = vw_ref[0, pl.ds(w0, span), :]
    s = lax.dot_general(q4, kwin, nt, preferred_element_type=f32)
    kpos = w0 + lax.broadcasted_iota(jnp.int32, (rows, span), 1)
    qp = qpos((rows, span))
    e, den = _softmax_rows(s, (kpos <= qp) & (kpos > qp - WINDOW))
    o_win = jnp.dot(e.astype(bf16), vwin, preferred_element_type=f32) / den

    gate = jax.nn.sigmoid(gate_ref[0])
    for r in range(R):
        c0 = (g * R + r) * 3
        lane = lax.broadcasted_iota(jnp.int32, (tq, LANES), 1)
        gsel = lambda c: jnp.sum(jnp.where(lane == c0 + c, gate, 0.0), axis=-1, keepdims=True)
        rs = slice(r * tq, (r + 1) * tq)
        o_ref[0, :, r * Dh:(r + 1) * Dh] = (gsel(0) * o_cmp[rs] + gsel(1) * o_sel[rs]
                                            + gsel(2) * o_win[rs]).astype(o_ref.dtype)


def nsa_attention(q_rot, kv_rot, cmp_kv, proj3, B, S, tq=256, tk=512):
    G, R, Dh = NSA_KV_HEADS, NSA_GROUP, HEAD_DIM
    n_blk = S // SEL_LEN
    assert S % tk == 0 and tk % tq == 0 and S >= WINDOW + tq and n_blk % SUB == 0 and n_blk < LANES
    nc = S // CMP_STRIDE
    c0 = np.arange(nc - 1)[None, :] * CMP_STRIDE
    s0 = np.arange(n_blk)[:, None] * SEL_LEN
    ovt = np.zeros((n_blk, nc), np.float32)
    ovt[:, :nc - 1] = np.clip(np.minimum(c0 + CMP_LEN, s0 + SEL_LEN) - np.maximum(c0, s0), 0, None) / CMP_LEN
    kvspec = lambda j: pl.BlockSpec((1, S, Dh), lambda b, g, i: (b, 0, j * G + g))
    return pl.pallas_call(
        functools.partial(_nsa_attn_body, tq=tq, tk=tk, S=S),
        grid=(B, G, S // tq),
        in_specs=[pl.BlockSpec((1, tq, R * Dh), lambda b, g, i: (b, i, g)),
                  pl.BlockSpec((2, 1, 1, nc, Dh), lambda b, g, i: (0, b, g, 0, 0)),
                  kvspec(0), kvspec(1), kvspec(2), kvspec(3),
                  pl.BlockSpec((1, tq, LANES), lambda b, g, i: (b, i, OFF_SMALL // LANES)),
                  pl.BlockSpec((n_blk, nc), lambda b, g, i: (0, 0))],
        out_specs=pl.BlockSpec((1, tq, R * Dh), lambda b, g, i: (b, i, g)),
        out_shape=jax.ShapeDtypeStruct((B, S, D_GRP), bf16),
        compiler_params=pltpu.CompilerParams(dimension_semantics=("parallel", "parallel", "arbitrary"),
                                             vmem_limit_bytes=VMEM_LIMIT),
        name="nsa_attention",
    )(q_rot, cmp_kv, kv_rot, kv_rot, kv_rot, kv_rot, proj3, jnp.asarray(ovt, bf16))


def nsa_mixer_pallas(proj, ck_w1, ck_w2, ck_pe, cv_w1, cv_w2, cv_pe, B, S):
    q_rot, kv_rot, c_rot = nsa_prep(proj, S)
    cmp_kv = nsa_compress(c_rot, (ck_w1, cv_w1), (ck_w2, cv_w2), (ck_pe, cv_pe), B, S)
    return nsa_attention(q_rot.reshape(B, S, D_GRP), kv_rot.reshape(B, S, 4 * NSA_KV_W), cmp_kv,
                         proj.reshape(B, S, P_PAD), B, S)


def _retention_body(q_ref, k_ref, v_ref, g_ref, cos_ref, sin_ref, dec_ref, xi_ref, zeta_ref, gc_ref, o_ref, st_ref,
                    *, T, HS):
    @pl.when(pl.program_id(2) == 0)
    def _():
        st_ref[...] = jnp.zeros_like(st_ref)

    C, Dh = RET_CHUNK, HEAD_DIM
    nt = (((1,), (1,)), ((), ()))
    pre = {}
    for hh in range(HS):
        hc = slice(hh * Dh, (hh + 1) * Dh)
        for c in range(T // C):
            rs = slice(c * C, (c + 1) * C)
            cos, sin = cos_ref[rs, :], sin_ref[rs, :]
            rope = lambda x: x * cos + pltpu.roll(x, Dh // 2, axis=1) * sin
            q = rope(q_ref[0, rs, hc])
            k = rope(k_ref[0, rs, hc]) * Dh ** -0.5
            qb, vb = q.astype(bf16), v_ref[0, rs, hc].astype(bf16)
            inner = lax.dot_general(qb, k.astype(bf16), nt, preferred_element_type=f32) * dec_ref[hh]
            o_in = jnp.dot(inner.astype(bf16), vb, preferred_element_type=f32)
            kv = jnp.dot((k * zeta_ref[hh]).astype(bf16).T, vb, preferred_element_type=f32)
            pre[hh, c] = (qb, o_in, kv)
    states = [st_ref[hh] for hh in range(HS)]
    for c in range(T // C):
        rs = slice(c * C, (c + 1) * C)
        for hh in range(HS):
            hc = slice(hh * Dh, (hh + 1) * Dh)
            qb, o_in, kv = pre[hh, c]
            o = o_in + jnp.dot(qb, states[hh].astype(bf16), preferred_element_type=f32) * xi_ref[hh]
            states[hh] = states[hh] * gc_ref[hh, 0:1, :] + kv
            mu = jnp.mean(o, axis=-1, keepdims=True)
            d = o - mu
            o = d * lax.rsqrt(jnp.mean(d * d, axis=-1, keepdims=True) + NORM_EPS)
            gt = g_ref[0, rs, hc]
            o_ref[0, rs, hc] = (gt * jax.nn.sigmoid(gt) * o).astype(o_ref.dtype)
    for hh in range(HS):
        st_ref[hh] = states[hh]


def retention_mixer_pallas(proj3, B, S, T=512, HS=2):
    H, Dh, C = RET_HEADS, HEAD_DIM, RET_CHUNK
    W = HS * Dh
    assert S % T == 0 and T % C == 0 and H % HS == 0
    freqs = RET_THETA ** (-np.linspace(0.0, 1.0, Dh // 2))
    ang = np.arange(S, dtype=np.float64)[:, None] * freqs[None, :]
    cos = jnp.asarray(np.concatenate([np.cos(ang), np.cos(ang)], 1), f32)
    sin = jnp.asarray(np.concatenate([-np.sin(ang), np.sin(ang)], 1), f32)
    log_gamma = np.log1p(-(2.0 ** (-5.0 - np.arange(H, dtype=np.float64))))
    n = np.arange(C, dtype=np.float64)
    diff = n[:, None] - n[None, :]
    decay_in = np.where(diff >= 0, np.exp(log_gamma[:, None, None] * np.maximum(diff, 0.0)), 0.0)
    xi = np.exp(log_gamma[:, None] * (n + 1.0))[:, :, None] * np.ones((1, 1, Dh))
    zeta = np.exp(log_gamma[:, None] * (C - 1.0 - n))[:, :, None] * np.ones((1, 1, Dh))
    gamma_c = np.exp(log_gamma * C)[:, None, None] * np.ones((1, SUB, Dh))
    blk = lambda j: pl.BlockSpec((1, T, W), lambda b, h, t: (b, t, (OFF_RET + j * D_GRP) // W + h))
    tab = pl.BlockSpec((T, Dh), lambda b, h, t: (t, 0))
    per_head = lambda r: pl.BlockSpec((HS, r, Dh), lambda b, h, t: (h, 0, 0))
    return pl.pallas_call(
        functools.partial(_retention_body, T=T, HS=HS),
        grid=(B, H // HS, S // T),
        in_specs=[blk(0), blk(1), blk(2), blk(3), tab, tab, per_head(C), per_head(C), per_head(C), per_head(SUB)],
        out_specs=pl.BlockSpec((1, T, W), lambda b, h, t: (b, t, h)),
        out_shape=jax.ShapeDtypeStruct((B, S, D_GRP), bf16),
        scratch_shapes=[pltpu.VMEM((HS, Dh, Dh), f32)],
        compiler_params=pltpu.CompilerParams(dimension_semantics=("parallel", "parallel", "arbitrary"),
                                             vmem_limit_bytes=VMEM_LIMIT),
        name="retention",
    )(proj3, proj3, proj3, proj3, cos, sin, jnp.asarray(decay_in, f32), jnp.asarray(xi, f32),
      jnp.asarray(zeta, f32), jnp.asarray(gamma_c, f32))


def _dot3(a, b):
    ah, al = _split2(a)
    bh, bl = _split2(b)
    return (jnp.dot(ah, bh, preferred_element_type=f32) + jnp.dot(ah, bl, preferred_element_type=f32)
            + jnp.dot(al, bh, preferred_element_type=f32))


def _deltanet_body(q_ref, k_ref, v_ref, z_ref, qp_ref, kp_ref, vp_ref, ab_ref, cw_ref, alog_ref, dtb_ref, ng_ref,
                   o_ref, st_ref, *, T, HS):
    hb = pl.program_id(1)
    first = pl.program_id(2) == 0

    @pl.when(first)
    def _():
        st_ref[...] = jnp.zeros_like(st_ref)

    C, Dh = DN_CHUNK, HEAD_DIM
    nt = (((1,), (1,)), ((), ()))
    n_chunks = T // C
    ab = ab_ref[0]
    lane = lax.broadcasted_iota(jnp.int32, ab.shape, 1)

    def conv_silu(x_ref, prev_ref, j, hc):
        prev = jnp.where(first, 0.0, prev_ref[0, :, hc])
        xcat = jnp.concatenate([prev, x_ref[0, :, hc]], axis=0)
        w = cw_ref[:, j, hc]
        y = sum(w[i:i + 1, :] * xcat[SUB - (CONV_WIDTH - 1) + i:SUB - (CONV_WIDTH - 1) + i + T, :]
                for i in range(CONV_WIDTH))
        return y * jax.nn.sigmoid(y)

    def l2n(x):
        return x * lax.rsqrt(jnp.sum(x * x, axis=-1, keepdims=True) + L2NORM_EPS)

    units = [(hh, slice(c * C, (c + 1) * C)) for hh in range(HS) for c in range(n_chunks)]
    qs, ks_, vs_, betas, g_rows = [], [], [], [], []
    for hh in range(HS):
        hc = slice(hh * Dh, (hh + 1) * Dh)
        head = hb * HS + hh
        q_all = l2n(conv_silu(q_ref, qp_ref, 0, hc)) * Dh ** -0.5
        k_all = l2n(conv_silu(k_ref, kp_ref, 1, hc))
        v_all = conv_silu(v_ref, vp_ref, 2, hc)
        a_col = jnp.sum(jnp.where(lane == SMALL_DN_A + head, ab, 0.0), axis=-1, keepdims=True)
        b_col = jnp.sum(jnp.where(lane == SMALL_DN_A + DN_HEADS + head, ab, 0.0), axis=-1, keepdims=True)
        beta_all = jax.nn.sigmoid(b_col)
        g_all = -jnp.exp(alog_ref[hh, 0:1, :]) * jax.nn.softplus(a_col + dtb_ref[hh, 0:1, :])
        for c in range(n_chunks):
            rs = slice(c * C, (c + 1) * C)
            qs.append(q_all[rs]); ks_.append(k_all[rs]); vs_.append(v_all[rs])
            betas.append(beta_all[rs]); g_rows.append(g_all[rs])

    ri = lax.broadcasted_iota(jnp.int32, (C, C), 0)
    ci = lax.broadcasted_iota(jnp.int32, (C, C), 1)
    tril, strict = ri >= ci, ri > ci
    ltri = tril.astype(bf16)
    eye = (ri == ci).astype(f32)
    gcs = []
    for g_rows_u in g_rows:
        gh, gm, gl = _split3(g_rows_u)
        gcs.append(jnp.dot(ltri, gh, preferred_element_type=f32) + jnp.dot(ltri, gm, preferred_element_type=f32)
                   + jnp.dot(ltri, gl, preferred_element_type=f32))
    decays = [jnp.exp(jnp.where(tril, gc[:, :C] - gc[:, :C].T, -jnp.inf)) for gc in gcs]
    kbs = [k.astype(bf16) for k in ks_]
    kbetas = [k * b for k, b in zip(ks_, betas)]
    ms = [jnp.where(strict, lax.dot_general(kbeta.astype(bf16), kb, nt, preferred_element_type=f32) * d, 0.0)
          for kbeta, kb, d in zip(kbetas, kbs, decays)]
    attns = [jnp.where(tril, lax.dot_general(q.astype(bf16), kb, nt, preferred_element_type=f32) * d, 0.0).astype(bf16)
             for q, kb, d in zip(qs, kbs, decays)]
    t_invs, pws = [eye - m for m in ms], ms
    for _ in range(5):
        pws = [_dot3(pw, pw) for pw in pws]
        t_invs = [t + _dot3(t, pw) for t, pw in zip(t_invs, pws)]
    egcs = [jnp.exp(gc) for gc in gcs]
    tbs = [t.astype(bf16) for t in t_invs]
    us = [jnp.dot(tb, (v * b).astype(bf16), preferred_element_type=f32) for tb, v, b in zip(tbs, vs_, betas)]
    ws = [jnp.dot(tb, (kbeta * egc).astype(bf16), preferred_element_type=f32).astype(bf16)
          for tb, kbeta, egc in zip(tbs, kbetas, egcs)]
    states = [st_ref[hh] for hh in range(HS)]
    for c in range(n_chunks):
        for hh in range(HS):
            u, rs = hh * n_chunks + c, slice(c * C, (c + 1) * C)
            g_last = gcs[u][C - 1:C, :]
            q_dec = (qs[u] * egcs[u]).astype(bf16)
            k_dec = (ks_[u] * jnp.exp(g_last - gcs[u])).astype(bf16)
            sb = states[hh].astype(bf16)
            v_new = us[u] - jnp.dot(ws[u], sb, preferred_element_type=f32)
            vnb = v_new.astype(bf16)
            o = jnp.dot(q_dec, sb, preferred_element_type=f32) + jnp.dot(attns[u], vnb, preferred_element_type=f32)
            states[hh] = states[hh] * jnp.exp(g_last) + jnp.dot(k_dec.T, vnb, preferred_element_type=f32)
            o = o * lax.rsqrt(jnp.mean(o * o, axis=-1, keepdims=True) + NORM_EPS) * ng_ref[...]
            zt = z_ref[0, rs, hh * Dh:(hh + 1) * Dh]
            o_ref[0, rs, hh * Dh:(hh + 1) * Dh] = (o * (zt * jax.nn.sigmoid(zt))).astype(o_ref.dtype)
    for hh in range(HS):
        st_ref[hh] = states[hh]


def deltanet_mixer_pallas(proj3, conv_w, a_log, dt_bias, norm_g, B, S, T=256, HS=4):
    H, Dh = DN_HEADS, HEAD_DIM
    W = HS * Dh
    assert S % T == 0 and T % DN_CHUNK == 0 and H % HS == 0
    blk = lambda j: pl.BlockSpec((1, T, W), lambda b, h, t: (b, t, (OFF_DN + j * D_GRP) // W + h))
    prev = lambda j: pl.BlockSpec((1, SUB, W), lambda b, h, t: (b, jnp.maximum(t * (T // SUB) - 1, 0),
                                                               (OFF_DN + j * D_GRP) // W + h))
    per_head = pl.BlockSpec((HS, SUB, Dh), lambda b, h, t: (h, 0, 0))
    bcast = lambda x: jnp.broadcast_to(x.astype(f32)[:, None, None], (H, SUB, Dh))
    return pl.pallas_call(
        functools.partial(_deltanet_body, T=T, HS=HS),
        grid=(B, H // HS, S // T),
        in_specs=[blk(0), blk(1), blk(2), blk(3), prev(0), prev(1), prev(2),
                  pl.BlockSpec((1, T, LANES), lambda b, h, t: (b, t, OFF_SMALL // LANES)),
                  pl.BlockSpec((CONV_WIDTH, 3, W), lambda b, h, t: (0, 0, h)),
                  per_head, per_head,
                  pl.BlockSpec((1, Dh), lambda b, h, t: (0, 0))],
        out_specs=pl.BlockSpec((1, T, W), lambda b, h, t: (b, t, h)),
        out_shape=jax.ShapeDtypeStruct((B, S, D_GRP), bf16),
        scratch_shapes=[pltpu.VMEM((HS, Dh, Dh), f32)],
        compiler_params=pltpu.CompilerParams(dimension_semantics=("parallel", "parallel", "arbitrary"),
                                             vmem_limit_bytes=VMEM_LIMIT),
        name="deltanet",
    )(proj3, proj3, proj3, proj3, proj3, proj3, proj3, proj3, conv_w.reshape(CONV_WIDTH, 3, D_GRP),
      bcast(a_log), bcast(dt_bias), norm_g.reshape(1, Dh))


def _w_in_pieces():
    n0, r0, d0, w0 = 0, NSA_COLS, NSA_COLS + RET_COLS, NSA_COLS + RET_COLS + DN_COLS
    n_gate = 3 * NSA_HEADS
    return [
        (OFF_RW, w0, 3 * D_GRP),
        (OFF_RET, r0, 4 * D_GRP),
        (OFF_DN, d0, 3 * D_GRP),
        (OFF_DN + 3 * D_GRP, d0 + 3 * D_GRP + 2 * DN_HEADS, D_GRP),
        (OFF_NSA_Q, n0, D_GRP + 6 * NSA_KV_W),
        (OFF_LORA, w0 + 3 * D_GRP, RW_LORA),
        (OFF_SMALL, n0 + D_GRP + 6 * NSA_KV_W, n_gate),
        (OFF_SMALL + SMALL_DN_A, d0 + 3 * D_GRP, 2 * DN_HEADS),
    ]


def _pack_w_in_body(w_ref, o_ref):
    cols = o_ref.shape[2]
    for off, width in ((OFF_LORA + RW_LORA, RW_LORA_PAD - RW_LORA), (OFF_SMALL, LANES)):
        o_ref[0, off:off + width, :] = jnp.zeros((width, cols), o_ref.dtype)
    for dst, src, width in _w_in_pieces():
        o_ref[0, dst:dst + width, :] = w_ref[0, src:src + width, :].astype(o_ref.dtype)


def pack_w_in_t(w_in, tc=256):
    L, K = w_in.shape[0], w_in.shape[1]
    w_t = jnp.swapaxes(w_in, 1, 2)
    return pl.pallas_call(
        _pack_w_in_body,
        grid=(L, K // tc),
        in_specs=[pl.BlockSpec((1, P_TOTAL, tc), lambda l, i: (l, 0, i))],
        out_specs=pl.BlockSpec((1, P_PAD, tc), lambda l, i: (l, 0, i)),
        out_shape=jax.ShapeDtypeStruct((L, P_PAD, K), bf16),
        compiler_params=pltpu.CompilerParams(dimension_semantics=("parallel", "parallel"),
                                             vmem_limit_bytes=VMEM_LIMIT),
        name="pack_w_in",
    )(w_t)


def hybrid_mixer(h, hn, B, S, layer, w_in_packed, w_out, ck_w1, ck_w2, ck_pe, cv_w1, cv_w2, cv_pe, dn_conv, dn_a_log,
                 dn_dt_bias, dn_norm, rw_mu, rw_w0, rw_w2, rw_a0, rw_a2, rw_g2, rw_kk, rw_ka, rw_rk, rw_ln_g, rw_ln_b):
    M = B * S
    proj = matmul_nt(hn, w_in_packed, layer, name="in_proj")
    p3 = proj.reshape(B, S, P_PAD)
    o_nsa = nsa_mixer_pallas(proj, ck_w1, ck_w2, ck_pe, cv_w1, cv_w2, cv_pe, B, S)
    o_ret = retention_mixer_pallas(p3, B, S)
    o_dn = deltanet_mixer_pallas(p3, dn_conv, dn_a_log, dn_dt_bias, dn_norm, B, S)
    o_rw = rwkv7_mixer(proj, OFF_RW // (3 * D_GRP), OFF_LORA // RW_LORA_PAD, rw_mu, rw_w0, rw_w2, rw_a0, rw_a2,
                       rw_g2, rw_kk, rw_ka, rw_rk, rw_ln_g, rw_ln_b, B, S)
    parts = [o.reshape(M, D_GRP) for o in (o_nsa, o_ret, o_dn, o_rw)]
    return out_proj_residual(parts, w_out.astype(bf16), h)


def kernel(x, p, norm_mix, w_in, w_out, nsa_ck_w1, nsa_ck_w2, nsa_ck_pe, nsa_cv_w1, nsa_cv_w2, nsa_cv_pe, dn_conv, dn_a_log, dn_dt_bias, dn_norm, rw_mu, rw_w0, rw_w2, rw_a0, rw_a2, rw_g2, rw_kk, rw_ka, rw_rk, rw_ln_g, rw_ln_b, norm_moe, moe_router_grp, moe_router_grp_b, moe_router_exp, moe_router_exp_b, moe_w_gate, moe_w_up, moe_w_down, norm_ple, ple_w, ple_gate, norm_final):
    B, S, D = x.shape
    M = B * S
    h = x.reshape(M, D)
    w_in_packed = pack_w_in_t(w_in)
    for i in range(DEPTH):
        hn = rms_norm_rows(h, norm_mix[i], bf16)
        h = hybrid_mixer(h, hn, B, S, i, w_in_packed, w_out[i], nsa_ck_w1[i], nsa_ck_w2[i], nsa_ck_pe[i], nsa_cv_w1[i],
                         nsa_cv_w2[i], nsa_cv_pe[i], dn_conv[i], dn_a_log[i], dn_dt_bias[i], dn_norm[i],
                         rw_mu[i], rw_w0[i], rw_w2[i], rw_a0[i], rw_a2[i], rw_g2[i], rw_kk[i], rw_ka[i],
                         rw_rk[i], rw_ln_g[i], rw_ln_b[i])
        hn = rms_norm_rows(h, norm_moe[i], bf16)
        h = h + hier_moe(hn, moe_router_grp[i], moe_router_grp_b[i], moe_router_exp[i], moe_router_exp_b[i],
                         moe_w_gate, moe_w_up, moe_w_down, i)
        hn = rms_norm_rows(h, norm_ple[i], bf16)
        h = ple_residual(hn, ple_gate[i].astype(bf16), p[i].reshape(M, PLE_DIM), ple_w[i].astype(bf16), h)
    return rms_norm_rows(h, norm_final, f32).reshape(B, S, D)
```

```python
import functools

import jax
import jax.numpy as jnp
import numpy as np
from jax import lax
from jax.experimental import pallas as pl
from jax.experimental.pallas import tpu as pltpu

D_MODEL = 4096
DEPTH = 2
f32 = jnp.float32
bf16 = jnp.bfloat16
D_MIX = D_MODEL
D_GRP = D_MIX // 4
HEAD_DIM = 128
NORM_EPS = 1e-6
L2NORM_EPS = 1e-6
NSA_HEADS = D_GRP // HEAD_DIM
NSA_KV_HEADS = 2
NSA_GROUP = NSA_HEADS // NSA_KV_HEADS
NSA_KV_W = NSA_KV_HEADS * HEAD_DIM
CMP_LEN = 32
CMP_STRIDE = 16
SEL_LEN = 64
N_SEL = 16
N_LOCAL_FORCED = 2
FORCED_SCORE = 1e4
WINDOW = 512
ROPE_THETA = 500000.0
ROPE_DIMS = HEAD_DIM // 4
RET_HEADS = D_GRP // HEAD_DIM
RET_CHUNK = 128
RET_THETA = 10000.0
DN_HEADS = D_GRP // HEAD_DIM
DN_CHUNK = 64
CONV_WIDTH = 4
RW_HEAD_DIM = 64
RW_DECAY_LORA = 64
RW_AAA_LORA = 64
RW_GATE_LORA = 160
RW_LN_EPS = 64e-5
N_GROUPS = 4
EXPERTS_PER_GROUP = 8
N_EXPERTS = N_GROUPS * EXPERTS_PER_GROUP
TOP_K = 2
D_EXPERT = 768
ROW_BLOCK = 256
PLE_DIM = 256
NSA_COLS = D_GRP + 6 * NSA_KV_W + 3 * NSA_HEADS
RET_COLS = 4 * D_GRP
DN_COLS = 4 * D_GRP + 2 * DN_HEADS
RW_COLS = 3 * D_GRP + RW_DECAY_LORA + RW_AAA_LORA + RW_GATE_LORA
P_TOTAL = NSA_COLS + RET_COLS + DN_COLS + RW_COLS

RW_LORA = RW_DECAY_LORA + RW_AAA_LORA + RW_GATE_LORA
RW_LORA_PAD = 384
RW_GROUPS_PER_BODY = 4
LANES = 128
SUB = 8

OFF_RW = 0
OFF_RET = 3 * D_GRP
OFF_DN = OFF_RET + 4 * D_GRP
OFF_NSA_Q = OFF_DN + 4 * D_GRP
OFF_NSA_KV = OFF_NSA_Q + D_GRP
OFF_LORA = OFF_NSA_KV + 6 * NSA_KV_W
OFF_SMALL = OFF_LORA + RW_LORA_PAD
SMALL_DN_A = 32
P_PAD = OFF_SMALL + LANES
assert P_PAD == 14336 and OFF_LORA % RW_LORA_PAD == 0

V7X_VMEM_BYTES = 64 * 1024 * 1024
VMEM_LIMIT = V7X_VMEM_BYTES * 7 // 8


def _split2(x):
    h = x.astype(bf16)
    return h, (x - h.astype(f32)).astype(bf16)


def _split3(x):
    h = x.astype(bf16)
    r1 = x - h.astype(f32)
    m = r1.astype(bf16)
    return h, m, (r1 - m.astype(f32)).astype(bf16)


def _rwkv_prep_body(x_ref, xprev_ref, l_ref, lprev_ref, mu_ref, mul_ref, w0_ref, a0_ref, kkw_ref, ka_ref,
                    w2_ref, a2_ref, g2_ref, e_ref, r_ref, w_ref, k_ref, v_ref, kk_ref, b_ref, g_ref, *, tm, S):
    i = pl.program_id(0)
    first = (i * tm) % S == 0

    def token_mix(x, prev8, mu):
        rows = lax.broadcasted_iota(jnp.int32, x.shape, 0)
        prev_row = jnp.where(first, 0.0, prev8[SUB - 1:SUB, :])
        x_prev = jnp.where(rows == 0, prev_row, pltpu.roll(x, 1, axis=0))
        return x + (x_prev - x) * mu

    x = token_mix(x_ref[...], xprev_ref[...], mu_ref[...])
    lora = token_mix(l_ref[...], lprev_ref[...], mul_ref[...])
    r = x[:, 0:D_GRP]
    k = x[:, D_GRP:2 * D_GRP]
    v = x[:, 2 * D_GRP:3 * D_GRP]
    dw = jnp.dot(jnp.tanh(lora).astype(bf16), w2_ref[...], preferred_element_type=f32)
    da = jnp.dot(lora.astype(bf16), a2_ref[...], preferred_element_type=f32)
    g = jnp.dot(jax.nn.sigmoid(lora).astype(bf16), g2_ref[...], preferred_element_type=f32)
    w = -jax.nn.softplus(-(w0_ref[...] + dw)) - 0.5
    decay = jnp.exp(-jnp.exp(w))
    a = jax.nn.sigmoid(a0_ref[...] + da)
    kq = k * kkw_ref[...]
    sq_h, sq_l = _split2(kq * kq)
    ss = (jnp.dot(sq_h, e_ref[...], preferred_element_type=f32)
          + jnp.dot(sq_l, e_ref[...], preferred_element_type=f32))
    kk = kq * lax.rsqrt(ss + L2NORM_EPS)
    r_ref[...] = r
    w_ref[...] = decay
    k_ref[...] = k * (1.0 + (a - 1.0) * ka_ref[...])
    v_ref[...] = v
    kk_ref[...] = kk
    b_ref[...] = kk * a
    g_ref[...] = g


def rwkv_prep(proj, rkv_blk, lora_blk, mu, w0, w2, a0, a2, g2, k_k, k_a, S, tm=256):
    M = proj.shape[0]
    assert M % tm == 0 and S % tm == 0
    W3 = 3 * D_GRP
    mu_x = mu[:W3].reshape(1, W3)
    mu_l = jnp.pad(mu[W3:], (0, RW_LORA_PAD - RW_LORA)).reshape(1, RW_LORA_PAD)
    w2p = jnp.zeros((RW_LORA_PAD, D_GRP), f32).at[0:RW_DECAY_LORA].set(w2).astype(bf16)
    a2p = jnp.zeros((RW_LORA_PAD, D_GRP), f32).at[RW_DECAY_LORA:RW_DECAY_LORA + RW_AAA_LORA].set(a2).astype(bf16)
    g2p = jnp.zeros((RW_LORA_PAD, D_GRP), f32).at[RW_DECAY_LORA + RW_AAA_LORA:RW_LORA].set(g2).astype(bf16)
    hid = np.arange(D_GRP) // RW_HEAD_DIM
    e = jnp.asarray(hid[:, None] == hid[None, :], bf16)
    row = lambda t: t.reshape(1, D_GRP)
    vec = pl.BlockSpec((1, D_GRP), lambda i: (0, 0))
    wspec = pl.BlockSpec((RW_LORA_PAD, D_GRP), lambda i: (0, 0))
    ospec = pl.BlockSpec((tm, D_GRP), lambda i: (i, 0))
    prev_row_blk = lambda i: jnp.maximum(i * (tm // SUB) - 1, 0)
    return pl.pallas_call(
        functools.partial(_rwkv_prep_body, tm=tm, S=S),
        grid=(M // tm,),
        in_specs=[pl.BlockSpec((tm, W3), lambda i: (i, rkv_blk)),
                  pl.BlockSpec((SUB, W3), lambda i: (prev_row_blk(i), rkv_blk)),
                  pl.BlockSpec((tm, RW_LORA_PAD), lambda i: (i, lora_blk)),
                  pl.BlockSpec((SUB, RW_LORA_PAD), lambda i: (prev_row_blk(i), lora_blk)),
                  pl.BlockSpec((1, W3), lambda i: (0, 0)),
                  pl.BlockSpec((1, RW_LORA_PAD), lambda i: (0, 0)),
                  vec, vec, vec, vec, wspec, wspec, wspec,
                  pl.BlockSpec((D_GRP, D_GRP), lambda i: (0, 0))],
        out_specs=[ospec] * 7,
        out_shape=[jax.ShapeDtypeStruct((M, D_GRP), f32)] * 7,
        compiler_params=pltpu.CompilerParams(dimension_semantics=("parallel",), vmem_limit_bytes=VMEM_LIMIT),
        name="rwkv_prep",
    )(proj, proj, proj, proj, mu_x, mu_l, row(w0), row(a0), row(k_k), row(k_a), w2p, a2p, g2p, e)


def _rwkv_scan_body(r_ref, w_ref, k_ref, v_ref, kk_ref, b_ref, g_ref, lng_ref, lnb_ref, rk_ref, o_ref, s_ref,
                    vc_ref, y_ref, *, G, T):
    tb = pl.program_id(2)

    @pl.when(tb == 0)
    def _():
        s_ref[...] = jnp.zeros_like(s_ref)

    Dh = RW_HEAD_DIM
    lo = lax.broadcasted_iota(jnp.int32, (Dh, LANES), 1) < Dh
    lo8 = lax.broadcasted_iota(jnp.int32, (SUB, LANES), 1) < Dh
    sub_iota = lax.broadcasted_iota(jnp.int32, (4 * SUB, LANES), 0)
    row_iota = lax.broadcasted_iota(jnp.int32, (SUB, LANES), 0)

    def seg_sum(x, m):
        s_lo = jnp.sum(jnp.where(m, x, 0.0), axis=-1, keepdims=True)
        s_hi = jnp.sum(jnp.where(m, 0.0, x), axis=-1, keepdims=True)
        return jnp.where(m, s_lo, s_hi)

    cols = [slice(p * LANES, (p + 1) * LANES) for p in range(G)]
    n_groups = T // SUB

    k_iota = lax.broadcasted_iota(jnp.int32, (8 * SUB, LANES), 0)
    k_head = k_iota // (4 * SUB) == lax.broadcasted_iota(jnp.int32, (8 * SUB, LANES), 1) // Dh
    k_piece = (k_iota // SUB) % 4

    def v_pieces(gi):
        t0 = pl.multiple_of(gi * SUB, SUB)
        vts = []
        for p in range(G):
            vh, vm, vl = _split3(v_ref[0, pl.ds(t0, SUB), cols[p]])
            vp = jnp.concatenate([vh, vm, vl, jnp.zeros_like(vh)], axis=0)
            vts.append(jnp.concatenate([vp[:, :Dh], vp[:, Dh:]], axis=0).T)
        return vts

    def v_columns(vts, i, slot):
        sel = (k_head & (k_piece < 3) & (k_iota % SUB == i)).astype(bf16)
        for p in range(G):
            vc_ref[slot, i, p] = jnp.dot(vts[p], sel, preferred_element_type=f32)

    vts0 = v_pieces(0)
    for i in range(SUB):
        v_columns(vts0, i, 0)

    def run_group(gi, slot, states):
        t0 = pl.multiple_of(gi * SUB, SUB)
        vts_next = v_pieces(jnp.minimum(gi + 1, n_groups - 1))
        r8b = [r_ref[0, pl.ds(t0, SUB), cols[p]].astype(bf16) for p in range(G)]
        yacc = [jnp.zeros((SUB, LANES), f32) for _ in range(G)]
        tiles = [[ref[0, pl.ds(t0, SUB), cols[p]] for ref in (kk_ref, w_ref, b_ref, k_ref)] for p in range(G)]
        for i in range(SUB):
            for p in range(G):
                kk_r, w_r, b_r, k_r = [jnp.broadcast_to(t8[i:i + 1, :], (Dh, LANES)) for t8 in tiles[p]]
                S = states[p]
                sab = -seg_sum(S * kk_r, lo)
                states[p] = S * w_r + sab * b_r + vc_ref[slot, i, p] * k_r
            for p in range(G):
                Sb = states[p].astype(bf16)
                zero = jnp.zeros_like(Sb)
                mt = jnp.concatenate([jnp.where(lo, Sb, zero), jnp.where(lo, zero, Sb)], axis=0)
                y8 = lax.dot_general(r8b[p], mt, (((1,), (1,)), ((), ())), preferred_element_type=f32)
                yacc[p] = jnp.where(row_iota == i, y8, yacc[p])
            v_columns(vts_next, i, 1 - slot)
        for p in range(G):
            y_ref[pl.ds(t0, SUB), cols[p]] = yacc[p]
        return states

    def group_batch(gj, carry):
        states = [s_ref[p] for p in range(G)]
        for u in range(RW_GROUPS_PER_BODY):
            states = run_group(RW_GROUPS_PER_BODY * gj + u, u % 2, states)
        for p in range(G):
            s_ref[p] = states[p]
        return carry

    assert n_groups % RW_GROUPS_PER_BODY == 0 and RW_GROUPS_PER_BODY % 2 == 0
    lax.fori_loop(0, n_groups // RW_GROUPS_PER_BODY, group_batch, 0)

    lo_t = lax.broadcasted_iota(jnp.int32, (T, LANES), 1) < Dh
    for p in range(G):
        y = y_ref[:, cols[p]]
        mean = seg_sum(y, lo_t) * (1.0 / Dh)
        d = y - mean
        var = seg_sum(d * d, lo_t) * (1.0 / Dh)
        yn = d * lax.rsqrt(var + RW_LN_EPS) * lng_ref[:, cols[p]] + lnb_ref[:, cols[p]]
        bonus = seg_sum(r_ref[0, :, cols[p]] * k_ref[0, :, cols[p]] * rk_ref[:, cols[p]], lo_t) * v_ref[0, :, cols[p]]
        o_ref[0, :, cols[p]] = (yn + bonus) * g_ref[0, :, cols[p]]


def rwkv_scan(r, w, k, v, kk, b, g, ln_g, ln_b, r_k, G=8, T=512):
    B, S, D = r.shape
    n_pairs = D // LANES
    assert n_pairs % G == 0 and S % T == 0
    spec = pl.BlockSpec((1, T, G * LANES), lambda bi, pi, ti: (bi, ti, pi))
    vec = pl.BlockSpec((1, G * LANES), lambda bi, pi, ti: (0, pi))
    row = lambda t: t.reshape(1, D)
    return pl.pallas_call(
        functools.partial(_rwkv_scan_body, G=G, T=T),
        grid=(B, n_pairs // G, S // T),
        in_specs=[spec] * 7 + [vec] * 3,
        out_specs=spec,
        out_shape=jax.ShapeDtypeStruct((B, S, D), f32),
        scratch_shapes=[pltpu.VMEM((G, RW_HEAD_DIM, LANES), f32),
                        pltpu.VMEM((2, SUB, G, RW_HEAD_DIM, LANES), f32),
                        pltpu.VMEM((T, G * LANES), f32)],
        compiler_params=pltpu.CompilerParams(dimension_semantics=("parallel", "parallel", "arbitrary"),
                                             vmem_limit_bytes=VMEM_LIMIT),
        name="rwkv_scan",
    )(r, w, k, v, kk, b, g, row(ln_g), row(ln_b), row(r_k))


def rwkv7_mixer(proj, rkv_blk, lora_blk, mu, w0, w2, a0, a2, g2, k_k, k_a, r_k, ln_g, ln_b, B, S):
    outs = rwkv_prep(proj, rkv_blk, lora_blk, mu, w0, w2, a0, a2, g2, k_k, k_a, S)
    r, w, k, v, kk, b, g = [t.reshape(B, S, D_GRP) for t in outs]
    return rwkv_scan(r, w, k, v, kk, b, g, ln_g, ln_b, r_k.reshape(-1))


def _mm_nt_body(a_ref, bt_ref, o_ref, acc_ref):
    k = pl.program_id(2)

    @pl.when(k == 0)
    def _():
        acc_ref[...] = jnp.zeros_like(acc_ref)

    acc_ref[...] += lax.dot_general(a_ref[...], bt_ref[...], (((1,), (1,)), ((), ())), preferred_element_type=f32)

    @pl.when(k == pl.num_programs(2) - 1)
    def _():
        o_ref[...] = acc_ref[...].astype(o_ref.dtype)


def matmul_nt(a, bt, layer, out_dtype=f32, tm=1024, tn=1024, tk=2048, name="matmul_nt"):
    M, K = a.shape
    _, N, _ = bt.shape
    assert M % tm == 0 and N % tn == 0 and K % tk == 0, (M, N, K, tm, tn, tk)
    return pl.pallas_call(
        _mm_nt_body,
        grid=(M // tm, N // tn, K // tk),
        in_specs=[pl.BlockSpec((tm, tk), lambda i, j, k: (i, k)),
                  pl.BlockSpec((None, tn, tk), lambda i, j, k: (layer, j, k))],
        out_specs=pl.BlockSpec((tm, tn), lambda i, j, k: (i, j)),
        out_shape=jax.ShapeDtypeStruct((M, N), out_dtype),
        scratch_shapes=[pltpu.VMEM((tm, tn), f32)],
        compiler_params=pltpu.CompilerParams(
            dimension_semantics=("parallel", "parallel", "arbitrary"), vmem_limit_bytes=VMEM_LIMIT),
        name=name,
    )(a, bt)


def _rms_norm_body(x_ref, g_ref, o_ref):
    x = x_ref[...]
    y = x * lax.rsqrt(jnp.mean(x * x, axis=-1, keepdims=True) + NORM_EPS)
    o_ref[...] = (y * g_ref[...]).astype(o_ref.dtype)


def rms_norm_rows(x, g, out_dtype, tm=256):
    M, D = x.shape
    return pl.pallas_call(
        _rms_norm_body,
        grid=(M // tm,),
        in_specs=[pl.BlockSpec((tm, D), lambda i: (i, 0)), pl.BlockSpec((1, D), lambda i: (0, 0))],
        out_specs=pl.BlockSpec((tm, D), lambda i: (i, 0)),
        out_shape=jax.ShapeDtypeStruct((M, D), out_dtype),
        compiler_params=pltpu.CompilerParams(dimension_semantics=("parallel",), vmem_limit_bytes=VMEM_LIMIT),
        name="rms_norm",
    )(x, g.reshape(1, D).astype(f32))


def _out_proj_body(a0_ref, a1_ref, a2_ref, a3_ref, b_ref, h_ref, o_ref, acc_ref):
    k = pl.program_id(2)

    @pl.when(k == 0)
    def _():
        acc_ref[...] = h_ref[...]

    for j, a_ref in enumerate((a0_ref, a1_ref, a2_ref, a3_ref)):
        @pl.when(k == j)
        def _(a_ref=a_ref):
            acc_ref[...] += jnp.dot(a_ref[...].astype(bf16), b_ref[...], preferred_element_type=f32)

    @pl.when(k == pl.num_programs(2) - 1)
    def _():
        o_ref[...] = acc_ref[...]


def out_proj_residual(parts, w, h, tm=1024, tn=1024):
    M, D = h.shape
    aspec = pl.BlockSpec((tm, D_GRP), lambda i, j, k: (i, 0))
    return pl.pallas_call(
        _out_proj_body,
        grid=(M // tm, D // tn, len(parts)),
        in_specs=[aspec] * 4 + [pl.BlockSpec((D_GRP, tn), lambda i, j, k: (k, j)),
                                pl.BlockSpec((tm, tn), lambda i, j, k: (i, j))],
        out_specs=pl.BlockSpec((tm, tn), lambda i, j, k: (i, j)),
        out_shape=jax.ShapeDtypeStruct((M, D), f32),
        scratch_shapes=[pltpu.VMEM((tm, tn), f32)],
        compiler_params=pltpu.CompilerParams(
            dimension_semantics=("parallel", "parallel", "arbitrary"), vmem_limit_bytes=VMEM_LIMIT),
        name="out_proj",
    )(*parts, w, h)


def _ple_body(a_ref, b_ref, p_ref, pw_ref, h_ref, o_ref, acc_ref):
    k = pl.program_id(2)

    @pl.when(k == 0)
    def _():
        acc_ref[...] = jnp.zeros_like(acc_ref)

    acc_ref[...] += jnp.dot(a_ref[...], b_ref[...], preferred_element_type=f32)

    @pl.when(k == pl.num_programs(2) - 1)
    def _():
        emb = jnp.dot(p_ref[...].astype(bf16), pw_ref[...], preferred_element_type=f32)
        o_ref[...] = h_ref[...] + emb * jax.nn.sigmoid(acc_ref[...])


def ple_residual(hn, gate_w, p, ple_w, h, tm=1024, tn=1024, tk=2048):
    M, D = h.shape
    return pl.pallas_call(
        _ple_body,
        grid=(M // tm, D // tn, D // tk),
        in_specs=[pl.BlockSpec((tm, tk), lambda i, j, k: (i, k)),
                  pl.BlockSpec((tk, tn), lambda i, j, k: (k, j)),
                  pl.BlockSpec((tm, PLE_DIM), lambda i, j, k: (i, 0)),
                  pl.BlockSpec((PLE_DIM, tn), lambda i, j, k: (0, j)),
                  pl.BlockSpec((tm, tn), lambda i, j, k: (i, j))],
        out_specs=pl.BlockSpec((tm, tn), lambda i, j, k: (i, j)),
        out_shape=jax.ShapeDtypeStruct((M, D), f32),
        scratch_shapes=[pltpu.VMEM((tm, tn), f32)],
        compiler_params=pltpu.CompilerParams(
            dimension_semantics=("parallel", "parallel", "arbitrary"), vmem_limit_bytes=VMEM_LIMIT),
        name="ple",
    )(hn, gate_w, p, ple_w, h)


def _moe_router_body(x_ref, w_ref, b_ref, id_ref, wt_ref):
    logits = jnp.dot(x_ref[...].astype(bf16), w_ref[...], preferred_element_type=f32) + b_ref[...]
    lane = lax.broadcasted_iota(jnp.int32, logits.shape, 1).astype(f32)
    ninf = -jnp.inf

    def top1(vals):
        m = jnp.max(vals, axis=-1, keepdims=True)
        idx = jnp.min(jnp.where(vals == m, lane, float(LANES)), axis=-1, keepdims=True)
        return m, idx

    gl = jnp.where(lane < N_GROUPS, logits, ninf)
    g_max, g_idx = top1(gl)
    g_w = 1.0 / jnp.sum(jnp.exp(gl - g_max), axis=-1, keepdims=True)
    lo = N_GROUPS + g_idx * EXPERTS_PER_GROUP
    el = jnp.where((lane >= lo) & (lane < lo + EXPERTS_PER_GROUP), logits, ninf)
    e_max, _ = top1(el)
    pe = jnp.exp(el - e_max)
    p = pe / jnp.sum(pe, axis=-1, keepdims=True)
    p = jnp.where(el == ninf, ninf, p)
    p1, i1 = top1(p)
    p2, i2 = top1(jnp.where(lane == i1, ninf, p))
    scale = g_w / (p1 + p2)
    id_ref[...] = jnp.where(lane == 0, i1 - N_GROUPS, jnp.where(lane == 1, i2 - N_GROUPS, 0.0)).astype(jnp.int32)
    wt_ref[...] = jnp.where(lane == 0, p1 * scale, jnp.where(lane == 1, p2 * scale, 0.0))


def moe_router(xb, router_grp, router_grp_b, router_exp, router_exp_b, tm=512):
    M, D = xb.shape
    n_log = N_GROUPS + N_EXPERTS
    w = jnp.pad(jnp.concatenate([router_grp, router_exp], axis=1), ((0, 0), (0, LANES - n_log))).astype(bf16)
    b = jnp.pad(jnp.concatenate([router_grp_b, router_exp_b]), (0, LANES - n_log)).reshape(1, LANES).astype(f32)
    ids, wts = pl.pallas_call(
        _moe_router_body,
        grid=(M // tm,),
        in_specs=[pl.BlockSpec((tm, D), lambda i: (i, 0)), pl.BlockSpec((D, LANES), lambda i: (0, 0)),
                  pl.BlockSpec((1, LANES), lambda i: (0, 0))],
        out_specs=[pl.BlockSpec((tm, LANES), lambda i: (i, 0))] * 2,
        out_shape=[jax.ShapeDtypeStruct((M, LANES), jnp.int32), jax.ShapeDtypeStruct((M, LANES), f32)],
        compiler_params=pltpu.CompilerParams(dimension_semantics=("parallel",), vmem_limit_bytes=VMEM_LIMIT),
        name="moe_router",
    )(xb, w, b)
    return ids[:, :TOP_K], wts[:, :TOP_K]


def _moe_up_body(be_ref, nu_ref, x_ref, wg_ref, wu_ref, h_ref, *, tf):
    i = pl.program_id(0)

    @pl.when(i < nu_ref[0])
    def _():
        x = x_ref[...].astype(bf16)
        for f in range(D_EXPERT // tf):
            c = slice(f * tf, (f + 1) * tf)
            hg = jnp.dot(x, wg_ref[0, 0, :, c].astype(bf16), preferred_element_type=f32)
            hu = jnp.dot(x, wu_ref[0, 0, :, c].astype(bf16), preferred_element_type=f32)
            h_ref[:, c] = (hg * jax.nn.sigmoid(hg) * hu).astype(bf16)

    @pl.when(i >= nu_ref[0])
    def _():
        h_ref[...] = jnp.zeros_like(h_ref)


def _moe_down_body(be_ref, nu_ref, h_ref, wd_ref, wt_ref, y_ref):
    i = pl.program_id(0)

    @pl.when(i < nu_ref[0])
    def _():
        y_ref[...] = jnp.dot(h_ref[...], wd_ref[0, 0].astype(bf16), preferred_element_type=f32) * wt_ref[...]

    @pl.when(i >= nu_ref[0])
    def _():
        y_ref[...] = jnp.zeros_like(y_ref)


def moe_experts(x_pad, w_pad, blk_expert, n_used, wg, wu, wd, layer, tf=256):
    A_pad, D = x_pad.shape
    n_blk = A_pad // ROW_BLOCK
    wmap = lambda i, be, nu: (layer, be[i], 0, 0)
    wspec = pl.BlockSpec((1, 1, D, D_EXPERT), wmap, pipeline_mode=pl.Buffered(1))
    wspec2 = pl.BlockSpec((1, 1, D, D_EXPERT), wmap)
    h = pl.pallas_call(
        functools.partial(_moe_up_body, tf=tf),
        grid_spec=pltpu.PrefetchScalarGridSpec(
            num_scalar_prefetch=2,
            grid=(n_blk,),
            in_specs=[pl.BlockSpec((ROW_BLOCK, D), lambda i, be, nu: (i, 0)), wspec2, wspec],
            out_specs=pl.BlockSpec((ROW_BLOCK, D_EXPERT), lambda i, be, nu: (i, 0)),
        ),
        out_shape=jax.ShapeDtypeStruct((A_pad, D_EXPERT), bf16),
        compiler_params=pltpu.CompilerParams(dimension_semantics=("arbitrary",), vmem_limit_bytes=VMEM_LIMIT),
        name="moe_up",
    )(blk_expert, n_used, x_pad, wg, wu)
    return pl.pallas_call(
        _moe_down_body,
        grid_spec=pltpu.PrefetchScalarGridSpec(
            num_scalar_prefetch=2,
            grid=(n_blk,),
            in_specs=[pl.BlockSpec((ROW_BLOCK, D_EXPERT), lambda i, be, nu: (i, 0)),
                      pl.BlockSpec((1, 1, D_EXPERT, D), lambda i, be, nu: (layer, be[i], 0, 0)),
                      pl.BlockSpec((ROW_BLOCK, 1), lambda i, be, nu: (i, 0))],
            out_specs=pl.BlockSpec((ROW_BLOCK, D), lambda i, be, nu: (i, 0)),
        ),
        out_shape=jax.ShapeDtypeStruct((A_pad, D), f32),
        compiler_params=pltpu.CompilerParams(dimension_semantics=("arbitrary",), vmem_limit_bytes=VMEM_LIMIT),
        name="moe_down",
    )(blk_expert, n_used, h, wd, w_pad.reshape(A_pad, 1))


def hier_moe(xb, router_grp, router_grp_b, router_exp, router_exp_b, w_gate, w_up, w_down, layer):
    M, D = xb.shape
    expert_ids, weights = moe_router(xb, router_grp, router_grp_b, router_exp, router_exp_b)
    A = M * TOP_K
    e_flat = expert_ids.reshape(A)
    onehot = (e_flat[:, None] == jnp.arange(N_EXPERTS)[None, :]).astype(jnp.int32)
    rank = jnp.take_along_axis(jnp.cumsum(onehot, axis=0), e_flat[:, None], axis=1)[:, 0] - 1
    counts = jnp.sum(onehot, axis=0)
    padded = (counts + ROW_BLOCK - 1) // ROW_BLOCK * ROW_BLOCK
    end_pad = jnp.cumsum(padded)
    dest = (end_pad - padded)[e_flat] + rank
    n_blk = -(-A // ROW_BLOCK) + N_EXPERTS
    A_pad = n_blk * ROW_BLOCK
    tok_pad = (jnp.arange(A_pad, dtype=jnp.int32) % M).at[dest].set(jnp.arange(A, dtype=jnp.int32) // TOP_K)
    w_pad = jnp.zeros((A_pad,), f32).at[dest].set(weights.reshape(A))
    blk_expert = jnp.minimum(jnp.searchsorted(end_pad, jnp.arange(n_blk) * ROW_BLOCK, side='right'),
                             N_EXPERTS - 1).astype(jnp.int32)
    n_used = (end_pad[-1:] // ROW_BLOCK).astype(jnp.int32)
    x_pad = xb[tok_pad]
    y_pad = moe_experts(x_pad, w_pad, blk_expert, n_used, w_gate, w_up, w_down, layer)
    rows = dest.reshape(M, TOP_K)
    return y_pad[rows[:, 0]] + y_pad[rows[:, 1]]


def _rope_tables(S, theta, rot_dims, width):
    half = rot_dims // 2
    freqs = theta ** (-np.arange(half, dtype=np.float64) / half)
    ang = np.arange(S, dtype=np.float64)[:, None] * freqs[None, :]
    cos = np.ones((S, width), np.float64)
    sin_lo = np.zeros((S, width), np.float64)
    sin_hi = np.zeros((S, width), np.float64)
    cos[:, :half] = np.cos(ang)
    cos[:, half:rot_dims] = np.cos(ang)
    sin_lo[:, :half] = -np.sin(ang)
    sin_hi[:, half:rot_dims] = np.sin(ang)
    return [jnp.asarray(t, f32) for t in (cos, sin_lo, sin_hi)]


def _rope128(x, cos, sin_lo, sin_hi, half):
    return x * cos + pltpu.roll(x, LANES - half, axis=1) * sin_lo + pltpu.roll(x, half, axis=1) * sin_hi


def _nsa_prep_body(q_ref, kv_ref, cos_ref, sl_ref, sh_ref, qo_ref, kvo_ref, co_ref):
    cos, sl, sh = cos_ref[...], sl_ref[...], sh_ref[...]
    half = ROPE_DIMS // 2
    scale = HEAD_DIM ** -0.5
    for h in range(NSA_HEADS):
        c = slice(h * LANES, (h + 1) * LANES)
        qo_ref[:, c] = (_rope128(q_ref[:, c], cos, sl, sh, half) * scale).astype(bf16)
    for j in range(6):
        for g in range(NSA_KV_HEADS):
            c = slice((j * NSA_KV_HEADS + g) * LANES, (j * NSA_KV_HEADS + g + 1) * LANES)
            x = kv_ref[:, c]
            if j % 2 == 0:
                x = _rope128(x, cos, sl, sh, half)
            if j < 2:
                co_ref[:, c] = x.astype(bf16)
            else:
                kvo_ref[:, slice(c.start - 4 * LANES, c.stop - 4 * LANES)] = x.astype(bf16)


def nsa_prep(proj, S, tm=512):
    M = proj.shape[0]
    assert M % tm == 0 and S % tm == 0
    tabs = _rope_tables(S, ROPE_THETA, ROPE_DIMS, LANES)
    tspec = pl.BlockSpec((tm, LANES), lambda i: (i % (S // tm), 0))
    return pl.pallas_call(
        _nsa_prep_body,
        grid=(M // tm,),
        in_specs=[pl.BlockSpec((tm, D_GRP), lambda i: (i, OFF_NSA_Q // D_GRP)),
                  pl.BlockSpec((tm, 6 * NSA_KV_W), lambda i: (i, OFF_NSA_KV // (6 * NSA_KV_W))),
                  tspec, tspec, tspec],
        out_specs=[pl.BlockSpec((tm, D_GRP), lambda i: (i, 0)),
                   pl.BlockSpec((tm, 4 * NSA_KV_W), lambda i: (i, 0)),
                   pl.BlockSpec((tm, 2 * NSA_KV_W), lambda i: (i, 0))],
        out_shape=[jax.ShapeDtypeStruct((M, D_GRP), bf16), jax.ShapeDtypeStruct((M, 4 * NSA_KV_W), bf16),
                   jax.ShapeDtypeStruct((M, 2 * NSA_KV_W), bf16)],
        compiler_params=pltpu.CompilerParams(dimension_semantics=("parallel",), vmem_limit_bytes=VMEM_LIMIT),
        name="nsa_prep",
    )(proj, proj, *tabs)


def _nsa_compress_body(r_ref, w1_ref, w2_ref, pe_ref, o_ref):
    r = r_ref[0, 0, 0]
    half_w = CMP_STRIDE * HEAD_DIM
    a = jnp.dot(r, w1_ref[0, :half_w, :], preferred_element_type=f32)
    b = jnp.dot(r, w1_ref[0, half_w:, :], preferred_element_type=f32)
    n = r.shape[0]
    pe = jnp.dot(jnp.broadcast_to(pe_ref[0], (SUB, pe_ref.shape[-1])), w1_ref[0], preferred_element_type=f32)[0:1]
    pre = a + pltpu.roll(b, n - 1, axis=0) + pe
    hid = pre * jax.nn.sigmoid(pre)
    o_ref[0, 0, 0] = jnp.dot(hid.astype(bf16), w2_ref[0], preferred_element_type=f32).astype(bf16)


def nsa_compress(c_rot, w1s, w2s, pes, B, S):
    G, Dh = NSA_KV_HEADS, HEAD_DIM
    nc = S // CMP_STRIDE
    r = c_rot.reshape(B, nc, CMP_STRIDE, 2, G, Dh).transpose(3, 0, 4, 1, 2, 5).reshape(2, B, G, nc, CMP_STRIDE * Dh)
    w1 = jnp.stack(w1s).astype(bf16)
    w2 = jnp.stack(w2s).astype(bf16)
    pe = jnp.stack(pes).reshape(2, 1, CMP_LEN * Dh).astype(bf16)
    return pl.pallas_call(
        _nsa_compress_body,
        grid=(2, B, G),
        in_specs=[pl.BlockSpec((1, 1, 1, nc, CMP_STRIDE * Dh), lambda j, b, g: (j, b, g, 0, 0)),
                  pl.BlockSpec((1, CMP_LEN * Dh, Dh), lambda j, b, g: (j, 0, 0)),
                  pl.BlockSpec((1, Dh, Dh), lambda j, b, g: (j, 0, 0)),
                  pl.BlockSpec((1, 1, CMP_LEN * Dh), lambda j, b, g: (j, 0, 0))],
        out_specs=pl.BlockSpec((1, 1, 1, nc, Dh), lambda j, b, g: (j, b, g, 0, 0)),
        out_shape=jax.ShapeDtypeStruct((2, B, G, nc, Dh), bf16),
        compiler_params=pltpu.CompilerParams(dimension_semantics=("parallel", "parallel", "parallel"),
                                             vmem_limit_bytes=VMEM_LIMIT),
        name="nsa_compress",
    )(r, w1, w2, pe)


def _softmax_rows(s, mask):
    s = jnp.where(mask, s, -jnp.inf)
    m = jnp.max(s, axis=-1, keepdims=True)
    m = jnp.where(m == -jnp.inf, 0.0, m)
    e = jnp.exp(s - m)
    return e, jnp.maximum(jnp.sum(e, axis=-1, keepdims=True), jnp.finfo(f32).tiny)


def _nsa_attn_body(q_ref, cmp_ref, ks_ref, vs_ref, kw_ref, vw_ref, gate_ref, ovt_ref, o_ref, *, tq, tk, S):
    g = pl.program_id(1)
    qi = pl.program_id(2)
    R, Dh = NSA_GROUP, HEAD_DIM
    rows = R * tq
    n_blk = S // SEL_LEN
    n_cmp = S // CMP_STRIDE - 1
    q0 = qi * tq
    q4 = jnp.concatenate([q_ref[0, :, r * Dh:(r + 1) * Dh] for r in range(R)], axis=0)
    nt = (((1,), (1,)), ((), ()))

    def qpos(shape):
        return q0 + lax.broadcasted_iota(jnp.int32, shape, 0) % tq

    k_cmp, v_cmp = cmp_ref[0, 0, 0], cmp_ref[1, 0, 0]
    ncp = k_cmp.shape[0]
    s = lax.dot_general(q4, k_cmp, nt, preferred_element_type=f32)
    n_idx = lax.broadcasted_iota(jnp.int32, (rows, ncp), 1)
    mask = (n_idx * CMP_STRIDE + (CMP_LEN - 1) <= qpos((rows, ncp))) & (n_idx < n_cmp)
    e, den = _softmax_rows(s, mask)
    p = e / den
    o_cmp = jnp.dot(p.astype(bf16), v_cmp, preferred_element_type=f32)

    psum = p[0:tq]
    for r in range(1, R):
        psum = psum + p[r * tq:(r + 1) * tq]
    ph, plo = _split2(psum)
    imp = (lax.dot_general(ovt_ref[...], ph, nt, preferred_element_type=f32)
           + lax.dot_general(ovt_ref[...], plo, nt, preferred_element_type=f32))
    blk = lax.broadcasted_iota(jnp.int32, (n_blk, tq), 0)
    cur = (q0 + lax.broadcasted_iota(jnp.int32, (n_blk, tq), 1)) // SEL_LEN
    visible = blk <= cur
    forced = (blk == 0) | (visible & (blk > cur - N_LOCAL_FORCED))
    imp = jnp.where(forced, FORCED_SCORE, jnp.where(visible, imp, -1.0))
    sel_t = jnp.zeros((n_blk, tq), f32)
    blk_f = blk.astype(f32)
    for _ in range(min(N_SEL, n_blk)):
        m = jnp.max(imp, axis=0, keepdims=True)
        first = jnp.min(jnp.where(imp == m, blk_f, float(n_blk)), axis=0, keepdims=True)
        hit = blk_f == first
        sel_t = jnp.where(hit, 1.0, sel_t)
        imp = jnp.where(hit, -jnp.inf, imp)
    sel_q = jnp.concatenate([sel_t.T, jnp.ones((tq, LANES - n_blk), f32)], axis=1).astype(bf16)

    blocks_per_tile = tk // SEL_LEN

    big = float(2 ** 30)
    kb = sel_q.shape[1]

    def sel_step(kt, carry, causal):
        m_run, l_run, acc = carry
        k0 = pl.multiple_of(kt * tk, tk)
        kt_k = ks_ref[0, pl.ds(k0, tk), :]
        kt_v = vs_ref[0, pl.ds(k0, tk), :]
        s = lax.dot_general(q4, kt_k, nt, preferred_element_type=f32)
        row = lax.broadcasted_iota(jnp.int32, (kb, tk), 0)
        in_blk = row == kt * blocks_per_tile + lax.broadcasted_iota(jnp.int32, (kb, tk), 1) // SEL_LEN
        expand = jnp.where(in_blk, big, jnp.where(row == n_blk, -big, 0.0)).astype(bf16)
        bias = jnp.dot(sel_q, expand, preferred_element_type=f32)
        s = (s.reshape(R, tq, tk) + bias[None]).reshape(rows, tk)
        if causal:
            kpos = k0 + lax.broadcasted_iota(jnp.int32, (rows, tk), 1)
            s = jnp.where(kpos <= qpos((rows, tk)), s, -big)
        m_new = jnp.maximum(m_run, jnp.max(s, axis=-1, keepdims=True))
        alpha = jnp.exp(m_run - m_new)
        e = jnp.exp(s - m_new)
        l_new = alpha * l_run + jnp.sum(e, axis=-1, keepdims=True)
        acc = alpha * acc + jnp.dot(e.astype(bf16), kt_v, preferred_element_type=f32)
        return m_new, l_new, acc

    last_kt = (q0 + tq - 1) // tk
    init = (jnp.full((rows, 1), -jnp.inf, f32), jnp.zeros((rows, 1), f32), jnp.zeros((rows, Dh), f32))
    carry = lax.fori_loop(0, last_kt, functools.partial(sel_step, causal=False), init)
    _, l_sel, acc_sel = sel_step(last_kt, carry, causal=True)
    o_sel = acc_sel / l_sel

    span = WINDOW + tq
    w0 = pl.multiple_of(jnp.maximum(q0 - WINDOW, 0), tq)
    kwin = kw_ref[0, pl.ds(w0, span), :]
    vwin = vw_ref[0, pl.ds(w0, span), :]
    s = lax.dot_general(q4, kwin, nt, preferred_element_type=f32)
    kpos = w0 + lax.broadcasted_iota(jnp.int32, (rows, span), 1)
    qp = qpos((rows, span))
    e, den = _softmax_rows(s, (kpos <= qp) & (kpos > qp - WINDOW))
    o_win = jnp.dot(e.astype(bf16), vwin, preferred_element_type=f32) / den

    gate = jax.nn.sigmoid(gate_ref[0])
    for r in range(R):
        c0 = (g * R + r) * 3
        lane = lax.broadcasted_iota(jnp.int32, (tq, LANES), 1)
        gsel = lambda c: jnp.sum(jnp.where(lane == c0 + c, gate, 0.0), axis=-1, keepdims=True)
        rs = slice(r * tq, (r + 1) * tq)
        o_ref[0, :, r * Dh:(r + 1) * Dh] = (gsel(0) * o_cmp[rs] + gsel(1) * o_sel[rs]
                                            + gsel(2) * o_win[rs]).astype(o_ref.dtype)


def nsa_attention(q_rot, kv_rot, cmp_kv, proj3, B, S, tq=256, tk=512):
    G, R, Dh = NSA_KV_HEADS, NSA_GROUP, HEAD_DIM
    n_blk = S // SEL_LEN
    assert S % tk == 0 and tk % tq == 0 and S >= WINDOW + tq and n_blk % SUB == 0 and n_blk < LANES
    nc = S // CMP_STRIDE
    c0 = np.arange(nc - 1)[None, :] * CMP_STRIDE
    s0 = np.arange(n_blk)[:, None] * SEL_LEN
    ovt = np.zeros((n_blk, nc), np.float32)
    ovt[:, :nc - 1] = np.clip(np.minimum(c0 + CMP_LEN, s0 + SEL_LEN) - np.maximum(c0, s0), 0, None) / CMP_LEN
    kvspec = lambda j: pl.BlockSpec((1, S, Dh), lambda b, g, i: (b, 0, j * G + g))
    return pl.pallas_call(
        functools.partial(_nsa_attn_body, tq=tq, tk=tk, S=S),
        grid=(B, G, S // tq),
        in_specs=[pl.BlockSpec((1, tq, R * Dh), lambda b, g, i: (b, i, g)),
                  pl.BlockSpec((2, 1, 1, nc, Dh), lambda b, g, i: (0, b, g, 0, 0)),
                  kvspec(0), kvspec(1), kvspec(2), kvspec(3),
                  pl.BlockSpec((1, tq, LANES), lambda b, g, i: (b, i, OFF_SMALL // LANES)),
                  pl.BlockSpec((n_blk, nc), lambda b, g, i: (0, 0))],
        out_specs=pl.BlockSpec((1, tq, R * Dh), lambda b, g, i: (b, i, g)),
        out_shape=jax.ShapeDtypeStruct((B, S, D_GRP), bf16),
        compiler_params=pltpu.CompilerParams(dimension_semantics=("parallel", "parallel", "arbitrary"),
                                             vmem_limit_bytes=VMEM_LIMIT),
        name="nsa_attention",
    )(q_rot, cmp_kv, kv_rot, kv_rot, kv_rot, kv_rot, proj3, jnp.asarray(ovt, bf16))


def nsa_mixer_pallas(proj, ck_w1, ck_w2, ck_pe, cv_w1, cv_w2, cv_pe, B, S):
    q_rot, kv_rot, c_rot = nsa_prep(proj, S)
    cmp_kv = nsa_compress(c_rot, (ck_w1, cv_w1), (ck_w2, cv_w2), (ck_pe, cv_pe), B, S)
    return nsa_attention(q_rot.reshape(B, S, D_GRP), kv_rot.reshape(B, S, 4 * NSA_KV_W), cmp_kv,
                         proj.reshape(B, S, P_PAD), B, S)


def _retention_body(q_ref, k_ref, v_ref, g_ref, cos_ref, sin_ref, dec_ref, xi_ref, zeta_ref, gc_ref, o_ref, st_ref,
                    *, T, HS):
    @pl.when(pl.program_id(2) == 0)
    def _():
        st_ref[...] = jnp.zeros_like(st_ref)

    C, Dh = RET_CHUNK, HEAD_DIM
    nt = (((1,), (1,)), ((), ()))
    pre = {}
    for hh in range(HS):
        hc = slice(hh * Dh, (hh + 1) * Dh)
        for c in range(T // C):
            rs = slice(c * C, (c + 1) * C)
            cos, sin = cos_ref[rs, :], sin_ref[rs, :]
            rope = lambda x: x * cos + pltpu.roll(x, Dh // 2, axis=1) * sin
            q = rope(q_ref[0, rs, hc])
            k = rope(k_ref[0, rs, hc]) * Dh ** -0.5
            qb, vb = q.astype(bf16), v_ref[0, rs, hc].astype(bf16)
            inner = lax.dot_general(qb, k.astype(bf16), nt, preferred_element_type=f32) * dec_ref[hh]
            o_in = jnp.dot(inner.astype(bf16), vb, preferred_element_type=f32)
            kv = jnp.dot((k * zeta_ref[hh]).astype(bf16).T, vb, preferred_element_type=f32)
            pre[hh, c] = (qb, o_in, kv)
    states = [st_ref[hh] for hh in range(HS)]
    for c in range(T // C):
        rs = slice(c * C, (c + 1) * C)
        for hh in range(HS):
            hc = slice(hh * Dh, (hh + 1) * Dh)
            qb, o_in, kv = pre[hh, c]
            o = o_in + jnp.dot(qb, states[hh].astype(bf16), preferred_element_type=f32) * xi_ref[hh]
            states[hh] = states[hh] * gc_ref[hh, 0:1, :] + kv
            mu = jnp.mean(o, axis=-1, keepdims=True)
            d = o - mu
            o = d * lax.rsqrt(jnp.mean(d * d, axis=-1, keepdims=True) + NORM_EPS)
            gt = g_ref[0, rs, hc]
            o_ref[0, rs, hc] = (gt * jax.nn.sigmoid(gt) * o).astype(o_ref.dtype)
    for hh in range(HS):
        st_ref[hh] = states[hh]


def retention_mixer_pallas(proj3, B, S, T=512, HS=2):
    H, Dh, C = RET_HEADS, HEAD_DIM, RET_CHUNK
    W = HS * Dh
    assert S % T == 0 and T % C == 0 and H % HS == 0
    freqs = RET_THETA ** (-np.linspace(0.0, 1.0, Dh // 2))
    ang = np.arange(S, dtype=np.float64)[:, None] * freqs[None, :]
    cos = jnp.asarray(np.concatenate([np.cos(ang), np.cos(ang)], 1), f32)
    sin = jnp.asarray(np.concatenate([-np.sin(ang), np.sin(ang)], 1), f32)
    log_gamma = np.log1p(-(2.0 ** (-5.0 - np.arange(H, dtype=np.float64))))
    n = np.arange(C, dtype=np.float64)
    diff = n[:, None] - n[None, :]
    decay_in = np.where(diff >= 0, np.exp(log_gamma[:, None, None] * np.maximum(diff, 0.0)), 0.0)
    xi = np.exp(log_gamma[:, None] * (n + 1.0))[:, :, None] * np.ones((1, 1, Dh))
    zeta = np.exp(log_gamma[:, None] * (C - 1.0 - n))[:, :, None] * np.ones((1, 1, Dh))
    gamma_c = np.exp(log_gamma * C)[:, None, None] * np.ones((1, SUB, Dh))
    blk = lambda j: pl.BlockSpec((1, T, W), lambda b, h, t: (b, t, (OFF_RET + j * D_GRP) // W + h))
    tab = pl.BlockSpec((T, Dh), lambda b, h, t: (t, 0))
    per_head = lambda r: pl.BlockSpec((HS, r, Dh), lambda b, h, t: (h, 0, 0))
    return pl.pallas_call(
        functools.partial(_retention_body, T=T, HS=HS),
        grid=(B, H // HS, S // T),
        in_specs=[blk(0), blk(1), blk(2), blk(3), tab, tab, per_head(C), per_head(C), per_head(C), per_head(SUB)],
        out_specs=pl.BlockSpec((1, T, W), lambda b, h, t: (b, t, h)),
        out_shape=jax.ShapeDtypeStruct((B, S, D_GRP), bf16),
        scratch_shapes=[pltpu.VMEM((HS, Dh, Dh), f32)],
        compiler_params=pltpu.CompilerParams(dimension_semantics=("parallel", "parallel", "arbitrary"),
                                             vmem_limit_bytes=VMEM_LIMIT),
        name="retention",
    )(proj3, proj3, proj3, proj3, cos, sin, jnp.asarray(decay_in, f32), jnp.asarray(xi, f32),
      jnp.asarray(zeta, f32), jnp.asarray(gamma_c, f32))


def _dot3(a, b):
    ah, al = _split2(a)
    bh, bl = _split2(b)
    return (jnp.dot(ah, bh, preferred_element_type=f32) + jnp.dot(ah, bl, preferred_element_type=f32)
            + jnp.dot(al, bh, preferred_element_type=f32))


def _deltanet_body(q_ref, k_ref, v_ref, z_ref, qp_ref, kp_ref, vp_ref, ab_ref, cw_ref, alog_ref, dtb_ref, ng_ref,
                   o_ref, st_ref, *, T, HS):
    hb = pl.program_id(1)
    first = pl.program_id(2) == 0

    @pl.when(first)
    def _():
        st_ref[...] = jnp.zeros_like(st_ref)

    C, Dh = DN_CHUNK, HEAD_DIM
    nt = (((1,), (1,)), ((), ()))
    n_chunks = T // C
    ab = ab_ref[0]
    lane = lax.broadcasted_iota(jnp.int32, ab.shape, 1)

    def conv_silu(x_ref, prev_ref, j, hc):
        prev = jnp.where(first, 0.0, prev_ref[0, :, hc])
        xcat = jnp.concatenate([prev, x_ref[0, :, hc]], axis=0)
        w = cw_ref[:, j, hc]
        y = sum(w[i:i + 1, :] * xcat[SUB - (CONV_WIDTH - 1) + i:SUB - (CONV_WIDTH - 1) + i + T, :]
                for i in range(CONV_WIDTH))
        return y * jax.nn.sigmoid(y)

    def l2n(x):
        return x * lax.rsqrt(jnp.sum(x * x, axis=-1, keepdims=True) + L2NORM_EPS)

    units = [(hh, slice(c * C, (c + 1) * C)) for hh in range(HS) for c in range(n_chunks)]
    qs, ks_, vs_, betas, g_rows = [], [], [], [], []
    for hh in range(HS):
        hc = slice(hh * Dh, (hh + 1) * Dh)
        head = hb * HS + hh
        q_all = l2n(conv_silu(q_ref, qp_ref, 0, hc)) * Dh ** -0.5
        k_all = l2n(conv_silu(k_ref, kp_ref, 1, hc))
        v_all = conv_silu(v_ref, vp_ref, 2, hc)
        a_col = jnp.sum(jnp.where(lane == SMALL_DN_A + head, ab, 0.0), axis=-1, keepdims=True)
        b_col = jnp.sum(jnp.where(lane == SMALL_DN_A + DN_HEADS + head, ab, 0.0), axis=-1, keepdims=True)
        beta_all = jax.nn.sigmoid(b_col)
        g_all = -jnp.exp(alog_ref[hh, 0:1, :]) * jax.nn.softplus(a_col + dtb_ref[hh, 0:1, :])
        for c in range(n_chunks):
            rs = slice(c * C, (c + 1) * C)
            qs.append(q_all[rs]); ks_.append(k_all[rs]); vs_.append(v_all[rs])
            betas.append(beta_all[rs]); g_rows.append(g_all[rs])

    ri = lax.broadcasted_iota(jnp.int32, (C, C), 0)
    ci = lax.broadcasted_iota(jnp.int32, (C, C), 1)
    tril, strict = ri >= ci, ri > ci
    ltri = tril.astype(bf16)
    eye = (ri == ci).astype(f32)
    gcs = []
    for g_rows_u in g_rows:
        gh, gm, gl = _split3(g_rows_u)
        gcs.append(jnp.dot(ltri, gh, preferred_element_type=f32) + jnp.dot(ltri, gm, preferred_element_type=f32)
                   + jnp.dot(ltri, gl, preferred_element_type=f32))
    decays = [jnp.exp(jnp.where(tril, gc[:, :C] - gc[:, :C].T, -jnp.inf)) for gc in gcs]
    kbs = [k.astype(bf16) for k in ks_]
    kbetas = [k * b for k, b in zip(ks_, betas)]
    ms = [jnp.where(strict, lax.dot_general(kbeta.astype(bf16), kb, nt, preferred_element_type=f32) * d, 0.0)
          for kbeta, kb, d in zip(kbetas, kbs, decays)]
    attns = [jnp.where(tril, lax.dot_general(q.astype(bf16), kb, nt, preferred_element_type=f32) * d, 0.0).astype(bf16)
             for q, kb, d in zip(qs, kbs, decays)]
    t_invs, pws = [eye - m for m in ms], ms
    for _ in range(5):
        pws = [_dot3(pw, pw) for pw in pws]
        t_invs = [t + _dot3(t, pw) for t, pw in zip(t_invs, pws)]
    egcs = [jnp.exp(gc) for gc in gcs]
    tbs = [t.astype(bf16) for t in t_invs]
    us = [jnp.dot(tb, (v * b).astype(bf16), preferred_element_type=f32) for tb, v, b in zip(tbs, vs_, betas)]
    ws = [jnp.dot(tb, (kbeta * egc).astype(bf16), preferred_element_type=f32).astype(bf16)
          for tb, kbeta, egc in zip(tbs, kbetas, egcs)]
    states = [st_ref[hh] for hh in range(HS)]
    for c in range(n_chunks):
        for hh in range(HS):
            u, rs = hh * n_chunks + c, slice(c * C, (c + 1) * C)
            g_last = gcs[u][C - 1:C, :]
            q_dec = (qs[u] * egcs[u]).astype(bf16)
            k_dec = (ks_[u] * jnp.exp(g_last - gcs[u])).astype(bf16)
            sb = states[hh].astype(bf16)
            v_new = us[u] - jnp.dot(ws[u], sb, preferred_element_type=f32)
            vnb = v_new.astype(bf16)
            o = jnp.dot(q_dec, sb, preferred_element_type=f32) + jnp.dot(attns[u], vnb, preferred_element_type=f32)
            states[hh] = states[hh] * jnp.exp(g_last) + jnp.dot(k_dec.T, vnb, preferred_element_type=f32)
            o = o * lax.rsqrt(jnp.mean(o * o, axis=-1, keepdims=True) + NORM_EPS) * ng_ref[...]
            zt = z_ref[0, rs, hh * Dh:(hh + 1) * Dh]
            o_ref[0, rs, hh * Dh:(hh + 1) * Dh] = (o * (zt * jax.nn.sigmoid(zt))).astype(o_ref.dtype)
    for hh in range(HS):
        st_ref[hh] = states[hh]


def deltanet_mixer_pallas(proj3, conv_w, a_log, dt_bias, norm_g, B, S, T=256, HS=4):
    H, Dh = DN_HEADS, HEAD_DIM
    W = HS * Dh
    assert S % T == 0 and T % DN_CHUNK == 0 and H % HS == 0
    blk = lambda j: pl.BlockSpec((1, T, W), lambda b, h, t: (b, t, (OFF_DN + j * D_GRP) // W + h))
    prev = lambda j: pl.BlockSpec((1, SUB, W), lambda b, h, t: (b, jnp.maximum(t * (T // SUB) - 1, 0),
                                                               (OFF_DN + j * D_GRP) // W + h))
    per_head = pl.BlockSpec((HS, SUB, Dh), lambda b, h, t: (h, 0, 0))
    bcast = lambda x: jnp.broadcast_to(x.astype(f32)[:, None, None], (H, SUB, Dh))
    return pl.pallas_call(
        functools.partial(_deltanet_body, T=T, HS=HS),
        grid=(B, H // HS, S // T),
        in_specs=[blk(0), blk(1), blk(2), blk(3), prev(0), prev(1), prev(2),
                  pl.BlockSpec((1, T, LANES), lambda b, h, t: (b, t, OFF_SMALL // LANES)),
                  pl.BlockSpec((CONV_WIDTH, 3, W), lambda b, h, t: (0, 0, h)),
                  per_head, per_head,
                  pl.BlockSpec((1, Dh), lambda b, h, t: (0, 0))],
        out_specs=pl.BlockSpec((1, T, W), lambda b, h, t: (b, t, h)),
        out_shape=jax.ShapeDtypeStruct((B, S, D_GRP), bf16),
        scratch_shapes=[pltpu.VMEM((HS, Dh, Dh), f32)],
        compiler_params=pltpu.CompilerParams(dimension_semantics=("parallel", "parallel", "arbitrary"),
                                             vmem_limit_bytes=VMEM_LIMIT),
        name="deltanet",
    )(proj3, proj3, proj3, proj3, proj3, proj3, proj3, proj3, conv_w.reshape(CONV_WIDTH, 3, D_GRP),
      bcast(a_log), bcast(dt_bias), norm_g.reshape(1, Dh))


def _w_in_pieces():
    n0, r0, d0, w0 = 0, NSA_COLS, NSA_COLS + RET_COLS, NSA_COLS + RET_COLS + DN_COLS
    n_gate = 3 * NSA_HEADS
    return [
        (OFF_RW, w0, 3 * D_GRP),
        (OFF_RET, r0, 4 * D_GRP),
        (OFF_DN, d0, 3 * D_GRP),
        (OFF_DN + 3 * D_GRP, d0 + 3 * D_GRP + 2 * DN_HEADS, D_GRP),
        (OFF_NSA_Q, n0, D_GRP + 6 * NSA_KV_W),
        (OFF_LORA, w0 + 3 * D_GRP, RW_LORA),
        (OFF_SMALL, n0 + D_GRP + 6 * NSA_KV_W, n_gate),
        (OFF_SMALL + SMALL_DN_A, d0 + 3 * D_GRP, 2 * DN_HEADS),
    ]


def _pack_w_in_body(w_ref, o_ref):
    cols = o_ref.shape[2]
    for off, width in ((OFF_LORA + RW_LORA, RW_LORA_PAD - RW_LORA), (OFF_SMALL, LANES)):
        o_ref[0, off:off + width, :] = jnp.zeros((width, cols), o_ref.dtype)
    for dst, src, width in _w_in_pieces():
        o_ref[0, dst:dst + width, :] = w_ref[0, src:src + width, :].astype(o_ref.dtype)


def pack_w_in_t(w_in, tc=256):
    L, K = w_in.shape[0], w_in.shape[1]
    w_t = jnp.swapaxes(w_in, 1, 2)
    return pl.pallas_call(
        _pack_w_in_body,
        grid=(L, K // tc),
        in_specs=[pl.BlockSpec((1, P_TOTAL, tc), lambda l, i: (l, 0, i))],
        out_specs=pl.BlockSpec((1, P_PAD, tc), lambda l, i: (l, 0, i)),
        out_shape=jax.ShapeDtypeStruct((L, P_PAD, K), bf16),
        compiler_params=pltpu.CompilerParams(dimension_semantics=("parallel", "parallel"),
                                             vmem_limit_bytes=VMEM_LIMIT),
        name="pack_w_in",
    )(w_t)


def hybrid_mixer(h, hn, B, S, layer, w_in_packed, w_out, ck_w1, ck_w2, ck_pe, cv_w1, cv_w2, cv_pe, dn_conv, dn_a_log,
                 dn_dt_bias, dn_norm, rw_mu, rw_w0, rw_w2, rw_a0, rw_a2, rw_g2, rw_kk, rw_ka, rw_rk, rw_ln_g, rw_ln_b):
    M = B * S
    proj = matmul_nt(hn, w_in_packed, layer, name="in_proj")
    p3 = proj.reshape(B, S, P_PAD)
    o_nsa = nsa_mixer_pallas(proj, ck_w1, ck_w2, ck_pe, cv_w1, cv_w2, cv_pe, B, S)
    o_ret = retention_mixer_pallas(p3, B, S)
    o_dn = deltanet_mixer_pallas(p3, dn_conv, dn_a_log, dn_dt_bias, dn_norm, B, S)
    o_rw = rwkv7_mixer(proj, OFF_RW // (3 * D_GRP), OFF_LORA // RW_LORA_PAD, rw_mu, rw_w0, rw_w2, rw_a0, rw_a2,
                       rw_g2, rw_kk, rw_ka, rw_rk, rw_ln_g, rw_ln_b, B, S)
    parts = [o.reshape(M, D_GRP) for o in (o_nsa, o_ret, o_dn, o_rw)]
    return out_proj_residual(parts, w_out.astype(bf16), h)


def kernel(x, p, norm_mix, w_in, w_out, nsa_ck_w1, nsa_ck_w2, nsa_ck_pe, nsa_cv_w1, nsa_cv_w2, nsa_cv_pe, dn_conv, dn_a_log, dn_dt_bias, dn_norm, rw_mu, rw_w0, rw_w2, rw_a0, rw_a2, rw_g2, rw_kk, rw_ka, rw_rk, rw_ln_g, rw_ln_b, norm_moe, moe_router_grp, moe_router_grp_b, moe_router_exp, moe_router_exp_b, moe_w_gate, moe_w_up, moe_w_down, norm_ple, ple_w, ple_gate, norm_final):
    B, S, D = x.shape
    M = B * S
    h = x.reshape(M, D)
    w_in_packed = pack_w_in_t(w_in)
    for i in range(DEPTH):
        hn = rms_norm_rows(h, norm_mix[i], bf16)
        h = hybrid_mixer(h, hn, B, S, i, w_in_packed, w_out[i], nsa_ck_w1[i], nsa_ck_w2[i], nsa_ck_pe[i], nsa_cv_w1[i],
                         nsa_cv_w2[i], nsa_cv_pe[i], dn_conv[i], dn_a_log[i], dn_dt_bias[i], dn_norm[i],
                         rw_mu[i], rw_w0[i], rw_w2[i], rw_a0[i], rw_a2[i], rw_g2[i], rw_kk[i], rw_ka[i],
                         rw_rk[i], rw_ln_g[i], rw_ln_b[i])
        hn = rms_norm_rows(h, norm_moe[i], bf16)
        h = h + hier_moe(hn, moe_router_grp[i], moe_router_grp_b[i], moe_router_exp[i], moe_router_exp_b[i],
                         moe_w_gate, moe_w_up, moe_w_down, i)
        hn = rms_norm_rows(h, norm_ple[i], bf16)
        h = ple_residual(hn, ple_gate[i].astype(bf16), p[i].reshape(M, PLE_DIM), ple_w[i].astype(bf16), h)
    return rms_norm_rows(h, norm_final, f32).reshape(B, S, D)
```

```python
import functools

import jax
import jax.numpy as jnp
import numpy as np
from jax import lax
from jax.experimental import pallas as pl
from jax.experimental.pallas import tpu as pltpu

D_MODEL = 4096
DEPTH = 2
f32 = jnp.float32
bf16 = jnp.bfloat16
D_MIX = D_MODEL
D_GRP = D_MIX // 4
HEAD_DIM = 128
NORM_EPS = 1e-6
L2NORM_EPS = 1e-6
NSA_HEADS = D_GRP // HEAD_DIM
NSA_KV_HEADS = 2
NSA_GROUP = NSA_HEADS // NSA_KV_HEADS
NSA_KV_W = NSA_KV_HEADS * HEAD_DIM
CMP_LEN = 32
CMP_STRIDE = 16
SEL_LEN = 64
N_SEL = 16
N_LOCAL_FORCED = 2
FORCED_SCORE = 1e4
WINDOW = 512
ROPE_THETA = 500000.0
ROPE_DIMS = HEAD_DIM // 4
RET_HEADS = D_GRP // HEAD_DIM
RET_CHUNK = 128
RET_THETA = 10000.0
DN_HEADS = D_GRP // HEAD_DIM
DN_CHUNK = 64
CONV_WIDTH = 4
RW_HEAD_DIM = 64
RW_DECAY_LORA = 64
RW_AAA_LORA = 64
RW_GATE_LORA = 160
RW_LN_EPS = 64e-5
N_GROUPS = 4
EXPERTS_PER_GROUP = 8
N_EXPERTS = N_GROUPS * EXPERTS_PER_GROUP
TOP_K = 2
D_EXPERT = 768
ROW_BLOCK = 256
PLE_DIM = 256
NSA_COLS = D_GRP + 6 * NSA_KV_W + 3 * NSA_HEADS
RET_COLS = 4 * D_GRP
DN_COLS = 4 * D_GRP + 2 * DN_HEADS
RW_COLS = 3 * D_GRP + RW_DECAY_LORA + RW_AAA_LORA + RW_GATE_LORA
P_TOTAL = NSA_COLS + RET_COLS + DN_COLS + RW_COLS

RW_LORA = RW_DECAY_LORA + RW_AAA_LORA + RW_GATE_LORA
RW_LORA_PAD = 384
RW_GROUPS_PER_BODY = 4
LANES = 128
SUB = 8

OFF_RW = 0
OFF_RET = 3 * D_GRP
OFF_DN = OFF_RET + 4 * D_GRP
OFF_NSA_Q = OFF_DN + 4 * D_GRP
OFF_NSA_KV = OFF_NSA_Q + D_GRP
OFF_LORA = OFF_NSA_KV + 6 * NSA_KV_W
OFF_SMALL = OFF_LORA + RW_LORA_PAD
SMALL_DN_A = 32
P_PAD = OFF_SMALL + LANES
assert P_PAD == 14336 and OFF_LORA % RW_LORA_PAD == 0

V7X_VMEM_BYTES = 64 * 1024 * 1024
VMEM_LIMIT = V7X_VMEM_BYTES * 7 // 8


def _split2(x):
    h = x.astype(bf16)
    return h, (x - h.astype(f32)).astype(bf16)


def _split3(x):
    h = x.astype(bf16)
    r1 = x - h.astype(f32)
    m = r1.astype(bf16)
    return h, m, (r1 - m.astype(f32)).astype(bf16)


def _rwkv_prep_body(x_ref, xprev_ref, l_ref, lprev_ref, mu_ref, mul_ref, w0_ref, a0_ref, kkw_ref, ka_ref,
                    w2_ref, a2_ref, g2_ref, e_ref, r_ref, w_ref, k_ref, v_ref, kk_ref, b_ref, g_ref, *, tm, S):
    i = pl.program_id(0)
    first = (i * tm) % S == 0

    def token_mix(x, prev8, mu):
        rows = lax.broadcasted_iota(jnp.int32, x.shape, 0)
        prev_row = jnp.where(first, 0.0, prev8[SUB - 1:SUB, :])
        x_prev = jnp.where(rows == 0, prev_row, pltpu.roll(x, 1, axis=0))
        return x + (x_prev - x) * mu

    x = token_mix(x_ref[...], xprev_ref[...], mu_ref[...])
    lora = token_mix(l_ref[...], lprev_ref[...], mul_ref[...])
    r = x[:, 0:D_GRP]
    k = x[:, D_GRP:2 * D_GRP]
    v = x[:, 2 * D_GRP:3 * D_GRP]
    dw = jnp.dot(jnp.tanh(lora).astype(bf16), w2_ref[...], preferred_element_type=f32)
    da = jnp.dot(lora.astype(bf16), a2_ref[...], preferred_element_type=f32)
    g = jnp.dot(jax.nn.sigmoid(lora).astype(bf16), g2_ref[...], preferred_element_type=f32)
    w = -jax.nn.softplus(-(w0_ref[...] + dw)) - 0.5
    decay = jnp.exp(-jnp.exp(w))
    a = jax.nn.sigmoid(a0_ref[...] + da)
    kq = k * kkw_ref[...]
    sq_h, sq_l = _split2(kq * kq)
    ss = (jnp.dot(sq_h, e_ref[...], preferred_element_type=f32)
          + jnp.dot(sq_l, e_ref[...], preferred_element_type=f32))
    kk = kq * lax.rsqrt(ss + L2NORM_EPS)
    r_ref[...] = r
    w_ref[...] = decay
    k_ref[...] = k * (1.0 + (a - 1.0) * ka_ref[...])
    v_ref[...] = v
    kk_ref[...] = kk
    b_ref[...] = kk * a
    g_ref[...] = g


def rwkv_prep(proj, rkv_blk, lora_blk, mu, w0, w2, a0, a2, g2, k_k, k_a, S, tm=256):
    M = proj.shape[0]
    assert M % tm == 0 and S % tm == 0
    W3 = 3 * D_GRP
    mu_x = mu[:W3].reshape(1, W3)
    mu_l = jnp.pad(mu[W3:], (0, RW_LORA_PAD - RW_LORA)).reshape(1, RW_LORA_PAD)
    w2p = jnp.zeros((RW_LORA_PAD, D_GRP), f32).at[0:RW_DECAY_LORA].set(w2).astype(bf16)
    a2p = jnp.zeros((RW_LORA_PAD, D_GRP), f32).at[RW_DECAY_LORA:RW_DECAY_LORA + RW_AAA_LORA].set(a2).astype(bf16)
    g2p = jnp.zeros((RW_LORA_PAD, D_GRP), f32).at[RW_DECAY_LORA + RW_AAA_LORA:RW_LORA].set(g2).astype(bf16)
    hid = np.arange(D_GRP) // RW_HEAD_DIM
    e = jnp.asarray(hid[:, None] == hid[None, :], bf16)
    row = lambda t: t.reshape(1, D_GRP)
    vec = pl.BlockSpec((1, D_GRP), lambda i: (0, 0))
    wspec = pl.BlockSpec((RW_LORA_PAD, D_GRP), lambda i: (0, 0))
    ospec = pl.BlockSpec((tm, D_GRP), lambda i: (i, 0))
    prev_row_blk = lambda i: jnp.maximum(i * (tm // SUB) - 1, 0)
    return pl.pallas_call(
        functools.partial(_rwkv_prep_body, tm=tm, S=S),
        grid=(M // tm,),
        in_specs=[pl.BlockSpec((tm, W3), lambda i: (i, rkv_blk)),
                  pl.BlockSpec((SUB, W3), lambda i: (prev_row_blk(i), rkv_blk)),
                  pl.BlockSpec((tm, RW_LORA_PAD), lambda i: (i, lora_blk)),
                  pl.BlockSpec((SUB, RW_LORA_PAD), lambda i: (prev_row_blk(i), lora_blk)),
                  pl.BlockSpec((1, W3), lambda i: (0, 0)),
                  pl.BlockSpec((1, RW_LORA_PAD), lambda i: (0, 0)),
                  vec, vec, vec, vec, wspec, wspec, wspec,
                  pl.BlockSpec((D_GRP, D_GRP), lambda i: (0, 0))],
        out_specs=[ospec] * 7,
        out_shape=[jax.ShapeDtypeStruct((M, D_GRP), f32)] * 7,
        compiler_params=pltpu.CompilerParams(dimension_semantics=("parallel",), vmem_limit_bytes=VMEM_LIMIT),
        name="rwkv_prep",
    )(proj, proj, proj, proj, mu_x, mu_l, row(w0), row(a0), row(k_k), row(k_a), w2p, a2p, g2p, e)


def _rwkv_scan_body(r_ref, w_ref, k_ref, v_ref, kk_ref, b_ref, g_ref, lng_ref, lnb_ref, rk_ref, o_ref, s_ref,
                    vc_ref, y_ref, *, G, T):
    tb = pl.program_id(2)

    @pl.when(tb == 0)
    def _():
        s_ref[...] = jnp.zeros_like(s_ref)

    Dh = RW_HEAD_DIM
    lo = lax.broadcasted_iota(jnp.int32, (Dh, LANES), 1) < Dh
    lo8 = lax.broadcasted_iota(jnp.int32, (SUB, LANES), 1) < Dh
    sub_iota = lax.broadcasted_iota(jnp.int32, (4 * SUB, LANES), 0)
    row_iota = lax.broadcasted_iota(jnp.int32, (SUB, LANES), 0)

    def seg_sum(x, m):
        s_lo = jnp.sum(jnp.where(m, x, 0.0), axis=-1, keepdims=True)
        s_hi = jnp.sum(jnp.where(m, 0.0, x), axis=-1, keepdims=True)
        return jnp.where(m, s_lo, s_hi)

    cols = [slice(p * LANES, (p + 1) * LANES) for p in range(G)]
    n_groups = T // SUB

    k_iota = lax.broadcasted_iota(jnp.int32, (8 * SUB, LANES), 0)
    k_head = k_iota // (4 * SUB) == lax.broadcasted_iota(jnp.int32, (8 * SUB, LANES), 1) // Dh
    k_piece = (k_iota // SUB) % 4

    def v_pieces(gi):
        t0 = pl.multiple_of(gi * SUB, SUB)
        vts = []
        for p in range(G):
            vh, vm, vl = _split3(v_ref[0, pl.ds(t0, SUB), cols[p]])
            vp = jnp.concatenate([vh, vm, vl, jnp.zeros_like(vh)], axis=0)
            vts.append(jnp.concatenate([vp[:, :Dh], vp[:, Dh:]], axis=0).T)
        return vts

    def v_columns(vts, i, slot):
        sel = (k_head & (k_piece < 3) & (k_iota % SUB == i)).astype(bf16)
        for p in range(G):
            vc_ref[slot, i, p] = jnp.dot(vts[p], sel, preferred_element_type=f32)

    vts0 = v_pieces(0)
    for i in range(SUB):
        v_columns(vts0, i, 0)

    def run_group(gi, slot, states):
        t0 = pl.multiple_of(gi * SUB, SUB)
        vts_next = v_pieces(jnp.minimum(gi + 1, n_groups - 1))
        r8b = [r_ref[0, pl.ds(t0, SUB), cols[p]].astype(bf16) for p in range(G)]
        yacc = [jnp.zeros((SUB, LANES), f32) for _ in range(G)]
        tiles = [[ref[0, pl.ds(t0, SUB), cols[p]] for ref in (kk_ref, w_ref, b_ref, k_ref)] for p in range(G)]
        for i in range(SUB):
            for p in range(G):
                kk_r, w_r, b_r, k_r = [jnp.broadcast_to(t8[i:i + 1, :], (Dh, LANES)) for t8 in tiles[p]]
                S = states[p]
                sab = -seg_sum(S * kk_r, lo)
                states[p] = S * w_r + sab * b_r + vc_ref[slot, i, p] * k_r
            for p in range(G):
                Sb = states[p].astype(bf16)
                zero = jnp.zeros_like(Sb)
                mt = jnp.concatenate([jnp.where(lo, Sb, zero), jnp.where(lo, zero, Sb)], axis=0)
                y8 = lax.dot_general(r8b[p], mt, (((1,), (1,)), ((), ())), preferred_element_type=f32)
                yacc[p] = jnp.where(row_iota == i, y8, yacc[p])
            v_columns(vts_next, i, 1 - slot)
        for p in range(G):
            y_ref[pl.ds(t0, SUB), cols[p]] = yacc[p]
        return states

    def group_batch(gj, carry):
        states = [s_ref[p] for p in range(G)]
        for u in range(RW_GROUPS_PER_BODY):
            states = run_group(RW_GROUPS_PER_BODY * gj + u, u % 2, states)
        for p in range(G):
            s_ref[p] = states[p]
        return carry

    assert n_groups % RW_GROUPS_PER_BODY == 0 and RW_GROUPS_PER_BODY % 2 == 0
    lax.fori_loop(0, n_groups // RW_GROUPS_PER_BODY, group_batch, 0)

    lo_t = lax.broadcasted_iota(jnp.int32, (T, LANES), 1) < Dh
    for p in range(G):
        y = y_ref[:, cols[p]]
        mean = seg_sum(y, lo_t) * (1.0 / Dh)
        d = y - mean
        var = seg_sum(d * d, lo_t) * (1.0 / Dh)
        yn = d * lax.rsqrt(var + RW_LN_EPS) * lng_ref[:, cols[p]] + lnb_ref[:, cols[p]]
        bonus = seg_sum(r_ref[0, :, cols[p]] * k_ref[0, :, cols[p]] * rk_ref[:, cols[p]], lo_t) * v_ref[0, :, cols[p]]
        o_ref[0, :, cols[p]] = (yn + bonus) * g_ref[0, :, cols[p]]


def rwkv_scan(r, w, k, v, kk, b, g, ln_g, ln_b, r_k, G=8, T=512):
    B, S, D = r.shape
    n_pairs = D // LANES
    assert n_pairs % G == 0 and S % T == 0
    spec = pl.BlockSpec((1, T, G * LANES), lambda bi, pi, ti: (bi, ti, pi))
    vec = pl.BlockSpec((1, G * LANES), lambda bi, pi, ti: (0, pi))
    row = lambda t: t.reshape(1, D)
    return pl.pallas_call(
        functools.partial(_rwkv_scan_body, G=G, T=T),
        grid=(B, n_pairs // G, S // T),
        in_specs=[spec] * 7 + [vec] * 3,
        out_specs=spec,
        out_shape=jax.ShapeDtypeStruct((B, S, D), f32),
        scratch_shapes=[pltpu.VMEM((G, RW_HEAD_DIM, LANES), f32),
                        pltpu.VMEM((2, SUB, G, RW_HEAD_DIM, LANES), f32),
                        pltpu.VMEM((T, G * LANES), f32)],
        compiler_params=pltpu.CompilerParams(dimension_semantics=("parallel", "parallel", "arbitrary"),
                                             vmem_limit_bytes=VMEM_LIMIT),
        name="rwkv_scan",
    )(r, w, k, v, kk, b, g, row(ln_g), row(ln_b), row(r_k))


def rwkv7_mixer(proj, rkv_blk, lora_blk, mu, w0, w2, a0, a2, g2, k_k, k_a, r_k, ln_g, ln_b, B, S):
    outs = rwkv_prep(proj, rkv_blk, lora_blk, mu, w0, w2, a0, a2, g2, k_k, k_a, S)
    r, w, k, v, kk, b, g = [t.reshape(B, S, D_GRP) for t in outs]
    return rwkv_scan(r, w, k, v, kk, b, g, ln_g, ln_b, r_k.reshape(-1))


def _mm_nt_body(a_ref, bt_ref, o_ref, acc_ref):
    k = pl.program_id(2)

    @pl.when(k == 0)
    def _():
        acc_ref[...] = jnp.zeros_like(acc_ref)

    acc_ref[...] += lax.dot_general(a_ref[...], bt_ref[...], (((1,), (1,)), ((), ())), preferred_element_type=f32)

    @pl.when(k == pl.num_programs(2) - 1)
    def _():
        o_ref[...] = acc_ref[...].astype(o_ref.dtype)


def matmul_nt(a, bt, layer, out_dtype=f32, tm=1024, tn=1024, tk=2048, name="matmul_nt"):
    M, K = a.shape
    _, N, _ = bt.shape
    assert M % tm == 0 and N % tn == 0 and K % tk == 0, (M, N, K, tm, tn, tk)
    return pl.pallas_call(
        _mm_nt_body,
        grid=(M // tm, N // tn, K // tk),
        in_specs=[pl.BlockSpec((tm, tk), lambda i, j, k: (i, k)),
                  pl.BlockSpec((None, tn, tk), lambda i, j, k: (layer, j, k))],
        out_specs=pl.BlockSpec((tm, tn), lambda i, j, k: (i, j)),
        out_shape=jax.ShapeDtypeStruct((M, N), out_dtype),
        scratch_shapes=[pltpu.VMEM((tm, tn), f32)],
        compiler_params=pltpu.CompilerParams(
            dimension_semantics=("parallel", "parallel", "arbitrary"), vmem_limit_bytes=VMEM_LIMIT),
        name=name,
    )(a, bt)


def _rms_norm_body(x_ref, g_ref, o_ref):
    x = x_ref[...]
    y = x * lax.rsqrt(jnp.mean(x * x, axis=-1, keepdims=True) + NORM_EPS)
    o_ref[...] = (y * g_ref[...]).astype(o_ref.dtype)


def rms_norm_rows(x, g, out_dtype, tm=256):
    M, D = x.shape
    return pl.pallas_call(
        _rms_norm_body,
        grid=(M // tm,),
        in_specs=[pl.BlockSpec((tm, D), lambda i: (i, 0)), pl.BlockSpec((1, D), lambda i: (0, 0))],
        out_specs=pl.BlockSpec((tm, D), lambda i: (i, 0)),
        out_shape=jax.ShapeDtypeStruct((M, D), out_dtype),
        compiler_params=pltpu.CompilerParams(dimension_semantics=("parallel",), vmem_limit_bytes=VMEM_LIMIT),
        name="rms_norm",
    )(x, g.reshape(1, D).astype(f32))


def _add_rms_norm_body(h_ref, ya_ref, yb_ref, g_ref, ho_ref, o_ref):
    x = h_ref[...] + (ya_ref[...] + yb_ref[...])
    ho_ref[...] = x
    y = x * lax.rsqrt(jnp.mean(x * x, axis=-1, keepdims=True) + NORM_EPS)
    o_ref[...] = (y * g_ref[...]).astype(o_ref.dtype)


def add_rms_norm_rows(h, ya, yb, g, out_dtype, tm=256):
    M, D = h.shape
    row = pl.BlockSpec((tm, D), lambda i: (i, 0))
    return pl.pallas_call(
        _add_rms_norm_body,
        grid=(M // tm,),
        in_specs=[row, row, row, pl.BlockSpec((1, D), lambda i: (0, 0))],
        out_specs=[row, row],
        out_shape=[jax.ShapeDtypeStruct((M, D), f32), jax.ShapeDtypeStruct((M, D), out_dtype)],
        compiler_params=pltpu.CompilerParams(dimension_semantics=("parallel",), vmem_limit_bytes=VMEM_LIMIT),
        name="add_rms_norm",
    )(h, ya, yb, g.reshape(1, D).astype(f32))


def _out_proj_body(a0_ref, a1_ref, a2_ref, a3_ref, b_ref, h_ref, o_ref, acc_ref):
    k = pl.program_id(2)

    @pl.when(k == 0)
    def _():
        acc_ref[...] = h_ref[...]

    for j, a_ref in enumerate((a0_ref, a1_ref, a2_ref, a3_ref)):
        @pl.when(k == j)
        def _(a_ref=a_ref):
            acc_ref[...] += jnp.dot(a_ref[...].astype(bf16), b_ref[...], preferred_element_type=f32)

    @pl.when(k == pl.num_programs(2) - 1)
    def _():
        o_ref[...] = acc_ref[...]


def out_proj_residual(parts, w, h, tm=1024, tn=1024):
    M, D = h.shape
    aspec = pl.BlockSpec((tm, D_GRP), lambda i, j, k: (i, 0))
    return pl.pallas_call(
        _out_proj_body,
        grid=(M // tm, D // tn, len(parts)),
        in_specs=[aspec] * 4 + [pl.BlockSpec((D_GRP, tn), lambda i, j, k: (k, j)),
                                pl.BlockSpec((tm, tn), lambda i, j, k: (i, j))],
        out_specs=pl.BlockSpec((tm, tn), lambda i, j, k: (i, j)),
        out_shape=jax.ShapeDtypeStruct((M, D), f32),
        scratch_shapes=[pltpu.VMEM((tm, tn), f32)],
        compiler_params=pltpu.CompilerParams(
            dimension_semantics=("parallel", "parallel", "arbitrary"), vmem_limit_bytes=VMEM_LIMIT),
        name="out_proj",
    )(*parts, w, h)


def _ple_body(a_ref, b_ref, p_ref, pw_ref, h_ref, o_ref, acc_ref):
    k = pl.program_id(2)

    @pl.when(k == 0)
    def _():
        acc_ref[...] = jnp.zeros_like(acc_ref)

    acc_ref[...] += jnp.dot(a_ref[...], b_ref[...], preferred_element_type=f32)

    @pl.when(k == pl.num_programs(2) - 1)
    def _():
        emb = jnp.dot(p_ref[...].astype(bf16), pw_ref[...], preferred_element_type=f32)
        o_ref[...] = h_ref[...] + emb * jax.nn.sigmoid(acc_ref[...])


def ple_residual(hn, gate_w, p, ple_w, h, tm=1024, tn=1024, tk=2048):
    M, D = h.shape
    return pl.pallas_call(
        _ple_body,
        grid=(M // tm, D // tn, D // tk),
        in_specs=[pl.BlockSpec((tm, tk), lambda i, j, k: (i, k)),
                  pl.BlockSpec((tk, tn), lambda i, j, k: (k, j)),
                  pl.BlockSpec((tm, PLE_DIM), lambda i, j, k: (i, 0)),
                  pl.BlockSpec((PLE_DIM, tn), lambda i, j, k: (0, j)),
                  pl.BlockSpec((tm, tn), lambda i, j, k: (i, j))],
        out_specs=pl.BlockSpec((tm, tn), lambda i, j, k: (i, j)),
        out_shape=jax.ShapeDtypeStruct((M, D), f32),
        scratch_shapes=[pltpu.VMEM((tm, tn), f32)],
        compiler_params=pltpu.CompilerParams(
            dimension_semantics=("parallel", "parallel", "arbitrary"), vmem_limit_bytes=VMEM_LIMIT),
        name="ple",
    )(hn, gate_w, p, ple_w, h)


def _moe_router_body(x_ref, w_ref, b_ref, id_ref, wt_ref):
    logits = jnp.dot(x_ref[...].astype(bf16), w_ref[...], preferred_element_type=f32) + b_ref[...]
    lane = lax.broadcasted_iota(jnp.int32, logits.shape, 1).astype(f32)
    ninf = -jnp.inf

    def top1(vals):
        m = jnp.max(vals, axis=-1, keepdims=True)
        idx = jnp.min(jnp.where(vals == m, lane, float(LANES)), axis=-1, keepdims=True)
        return m, idx

    gl = jnp.where(lane < N_GROUPS, logits, ninf)
    g_max, g_idx = top1(gl)
    g_w = 1.0 / jnp.sum(jnp.exp(gl - g_max), axis=-1, keepdims=True)
    lo = N_GROUPS + g_idx * EXPERTS_PER_GROUP
    el = jnp.where((lane >= lo) & (lane < lo + EXPERTS_PER_GROUP), logits, ninf)
    e_max, _ = top1(el)
    pe = jnp.exp(el - e_max)
    p = pe / jnp.sum(pe, axis=-1, keepdims=True)
    p = jnp.where(el == ninf, ninf, p)
    p1, i1 = top1(p)
    p2, i2 = top1(jnp.where(lane == i1, ninf, p))
    scale = g_w / (p1 + p2)
    id_ref[...] = jnp.where(lane == 0, i1 - N_GROUPS, jnp.where(lane == 1, i2 - N_GROUPS, 0.0)).astype(jnp.int32)
    wt_ref[...] = jnp.where(lane == 0, p1 * scale, jnp.where(lane == 1, p2 * scale, 0.0))


def moe_router(xb, router_grp, router_grp_b, router_exp, router_exp_b, tm=512):
    M, D = xb.shape
    n_log = N_GROUPS + N_EXPERTS
    w = jnp.pad(jnp.concatenate([router_grp, router_exp], axis=1), ((0, 0), (0, LANES - n_log))).astype(bf16)
    b = jnp.pad(jnp.concatenate([router_grp_b, router_exp_b]), (0, LANES - n_log)).reshape(1, LANES).astype(f32)
    ids, wts = pl.pallas_call(
        _moe_router_body,
        grid=(M // tm,),
        in_specs=[pl.BlockSpec((tm, D), lambda i: (i, 0)), pl.BlockSpec((D, LANES), lambda i: (0, 0)),
                  pl.BlockSpec((1, LANES), lambda i: (0, 0))],
        out_specs=[pl.BlockSpec((tm, LANES), lambda i: (i, 0))] * 2,
        out_shape=[jax.ShapeDtypeStruct((M, LANES), jnp.int32), jax.ShapeDtypeStruct((M, LANES), f32)],
        compiler_params=pltpu.CompilerParams(dimension_semantics=("parallel",), vmem_limit_bytes=VMEM_LIMIT),
        name="moe_router",
    )(xb, w, b)
    return ids[:, :TOP_K], wts[:, :TOP_K]


def _moe_up_body(be_ref, nu_ref, x_ref, wg_ref, wu_ref, h_ref, *, tf):
    i = pl.program_id(0)

    @pl.when(i < nu_ref[0])
    def _():
        x = x_ref[...].astype(bf16)
        for f in range(D_EXPERT // tf):
            c = slice(f * tf, (f + 1) * tf)
            hg = jnp.dot(x, wg_ref[0, 0, :, c].astype(bf16), preferred_element_type=f32)
            hu = jnp.dot(x, wu_ref[0, 0, :, c].astype(bf16), preferred_element_type=f32)
            h_ref[:, c] = (hg * jax.nn.sigmoid(hg) * hu).astype(bf16)

    @pl.when(i >= nu_ref[0])
    def _():
        h_ref[...] = jnp.zeros_like(h_ref)


def _moe_down_body(be_ref, nu_ref, h_ref, wd_ref, wt_ref, y_ref):
    i = pl.program_id(0)

    @pl.when(i < nu_ref[0])
    def _():
        y_ref[...] = jnp.dot(h_ref[...], wd_ref[0, 0].astype(bf16), preferred_element_type=f32) * wt_ref[...]

    @pl.when(i >= nu_ref[0])
    def _():
        y_ref[...] = jnp.zeros_like(y_ref)


def moe_experts(x_pad, w_pad, blk_expert, n_used, wg, wu, wd, layer, tf=256):
    A_pad, D = x_pad.shape
    n_blk = A_pad // ROW_BLOCK
    wmap = lambda i, be, nu: (layer, be[i], 0, 0)
    wspec = pl.BlockSpec((1, 1, D, D_EXPERT), wmap, pipeline_mode=pl.Buffered(1))
    wspec2 = pl.BlockSpec((1, 1, D, D_EXPERT), wmap)
    h = pl.pallas_call(
        functools.partial(_moe_up_body, tf=tf),
        grid_spec=pltpu.PrefetchScalarGridSpec(
            num_scalar_prefetch=2,
            grid=(n_blk,),
            in_specs=[pl.BlockSpec((ROW_BLOCK, D), lambda i, be, nu: (i, 0)), wspec2, wspec],
            out_specs=pl.BlockSpec((ROW_BLOCK, D_EXPERT), lambda i, be, nu: (i, 0)),
        ),
        out_shape=jax.ShapeDtypeStruct((A_pad, D_EXPERT), bf16),
        compiler_params=pltpu.CompilerParams(dimension_semantics=("arbitrary",), vmem_limit_bytes=VMEM_LIMIT),
        name="moe_up",
    )(blk_expert, n_used, x_pad, wg, wu)
    return pl.pallas_call(
        _moe_down_body,
        grid_spec=pltpu.PrefetchScalarGridSpec(
            num_scalar_prefetch=2,
            grid=(n_blk,),
            in_specs=[pl.BlockSpec((ROW_BLOCK, D_EXPERT), lambda i, be, nu: (i, 0)),
                      pl.BlockSpec((1, 1, D_EXPERT, D), lambda i, be, nu: (layer, be[i], 0, 0)),
                      pl.BlockSpec((ROW_BLOCK, 1), lambda i, be, nu: (i, 0))],
            out_specs=pl.BlockSpec((ROW_BLOCK, D), lambda i, be, nu: (i, 0)),
        ),
        out_shape=jax.ShapeDtypeStruct((A_pad, D), f32),
        compiler_params=pltpu.CompilerParams(dimension_semantics=("arbitrary",), vmem_limit_bytes=VMEM_LIMIT),
        name="moe_down",
    )(blk_expert, n_used, h, wd, w_pad.reshape(A_pad, 1))


def hier_moe(xb, router_grp, router_grp_b, router_exp, router_exp_b, w_gate, w_up, w_down, layer):
    M, D = xb.shape
    expert_ids, weights = moe_router(xb, router_grp, router_grp_b, router_exp, router_exp_b)
    A = M * TOP_K
    e_flat = expert_ids.reshape(A)
    onehot = (e_flat[:, None] == jnp.arange(N_EXPERTS)[None, :]).astype(jnp.int32)
    rank = jnp.take_along_axis(jnp.cumsum(onehot, axis=0), e_flat[:, None], axis=1)[:, 0] - 1
    counts = jnp.sum(onehot, axis=0)
    padded = (counts + ROW_BLOCK - 1) // ROW_BLOCK * ROW_BLOCK
    end_pad = jnp.cumsum(padded)
    dest = (end_pad - padded)[e_flat] + rank
    n_blk = -(-A // ROW_BLOCK) + N_EXPERTS
    A_pad = n_blk * ROW_BLOCK
    tok_pad = (jnp.arange(A_pad, dtype=jnp.int32) % M).at[dest].set(jnp.arange(A, dtype=jnp.int32) // TOP_K)
    w_pad = jnp.zeros((A_pad,), f32).at[dest].set(weights.reshape(A))
    blk_expert = jnp.minimum(jnp.searchsorted(end_pad, jnp.arange(n_blk) * ROW_BLOCK, side='right'),
                             N_EXPERTS - 1).astype(jnp.int32)
    n_used = (end_pad[-1:] // ROW_BLOCK).astype(jnp.int32)
    x_pad = xb[tok_pad]
    y_pad = moe_experts(x_pad, w_pad, blk_expert, n_used, w_gate, w_up, w_down, layer)
    rows = dest.reshape(M, TOP_K)
    return y_pad[rows[:, 0]], y_pad[rows[:, 1]]


def _rope_tables(S, theta, rot_dims, width):
    half = rot_dims // 2
    freqs = theta ** (-np.arange(half, dtype=np.float64) / half)
    ang = np.arange(S, dtype=np.float64)[:, None] * freqs[None, :]
    cos = np.ones((S, width), np.float64)
    sin_lo = np.zeros((S, width), np.float64)
    sin_hi = np.zeros((S, width), np.float64)
    cos[:, :half] = np.cos(ang)
    cos[:, half:rot_dims] = np.cos(ang)
    sin_lo[:, :half] = -np.sin(ang)
    sin_hi[:, half:rot_dims] = np.sin(ang)
    return [jnp.asarray(t, f32) for t in (cos, sin_lo, sin_hi)]


def _rope128(x, cos, sin_lo, sin_hi, half):
    return x * cos + pltpu.roll(x, LANES - half, axis=1) * sin_lo + pltpu.roll(x, half, axis=1) * sin_hi


def _nsa_prep_body(q_ref, kv_ref, cos_ref, sl_ref, sh_ref, qo_ref, kvo_ref, co_ref):
    cos, sl, sh = cos_ref[...], sl_ref[...], sh_ref[...]
    half = ROPE_DIMS // 2
    scale = HEAD_DIM ** -0.5
    for h in range(NSA_HEADS):
        c = slice(h * LANES, (h + 1) * LANES)
        qo_ref[:, c] = (_rope128(q_ref[:, c], cos, sl, sh, half) * scale).astype(bf16)
    for j in range(6):
        for g in range(NSA_KV_HEADS):
            c = slice((j * NSA_KV_HEADS + g) * LANES, (j * NSA_KV_HEADS + g + 1) * LANES)
            x = kv_ref[:, c]
            if j % 2 == 0:
                x = _rope128(x, cos, sl, sh, half)
            if j < 2:
                co_ref[:, c] = x.astype(bf16)
            else:
                kvo_ref[:, slice(c.start - 4 * LANES, c.stop - 4 * LANES)] = x.astype(bf16)


def nsa_prep(proj, S, tm=512):
    M = proj.shape[0]
    assert M % tm == 0 and S % tm == 0
    tabs = _rope_tables(S, ROPE_THETA, ROPE_DIMS, LANES)
    tspec = pl.BlockSpec((tm, LANES), lambda i: (i % (S // tm), 0))
    return pl.pallas_call(
        _nsa_prep_body,
        grid=(M // tm,),
        in_specs=[pl.BlockSpec((tm, D_GRP), lambda i: (i, OFF_NSA_Q // D_GRP)),
                  pl.BlockSpec((tm, 6 * NSA_KV_W), lambda i: (i, OFF_NSA_KV // (6 * NSA_KV_W))),
                  tspec, tspec, tspec],
        out_specs=[pl.BlockSpec((tm, D_GRP), lambda i: (i, 0)),
                   pl.BlockSpec((tm, 4 * NSA_KV_W), lambda i: (i, 0)),
                   pl.BlockSpec((tm, 2 * NSA_KV_W), lambda i: (i, 0))],
        out_shape=[jax.ShapeDtypeStruct((M, D_GRP), bf16), jax.ShapeDtypeStruct((M, 4 * NSA_KV_W), bf16),
                   jax.ShapeDtypeStruct((M, 2 * NSA_KV_W), bf16)],
        compiler_params=pltpu.CompilerParams(dimension_semantics=("parallel",), vmem_limit_bytes=VMEM_LIMIT),
        name="nsa_prep",
    )(proj, proj, *tabs)


def _nsa_compress_body(r_ref, w1_ref, w2_ref, pe_ref, o_ref):
    r = r_ref[0, 0, 0]
    half_w = CMP_STRIDE * HEAD_DIM
    a = jnp.dot(r, w1_ref[0, :half_w, :], preferred_element_type=f32)
    b = jnp.dot(r, w1_ref[0, half_w:, :], preferred_element_type=f32)
    n = r.shape[0]
    pe = jnp.dot(jnp.broadcast_to(pe_ref[0], (SUB, pe_ref.shape[-1])), w1_ref[0], preferred_element_type=f32)[0:1]
    pre = a + pltpu.roll(b, n - 1, axis=0) + pe
    hid = pre * jax.nn.sigmoid(pre)
    o_ref[0, 0, 0] = jnp.dot(hid.astype(bf16), w2_ref[0], preferred_element_type=f32).astype(bf16)


def nsa_compress(c_rot, w1s, w2s, pes, B, S):
    G, Dh = NSA_KV_HEADS, HEAD_DIM
    nc = S // CMP_STRIDE
    r = c_rot.reshape(B, nc, CMP_STRIDE, 2, G, Dh).transpose(3, 0, 4, 1, 2, 5).reshape(2, B, G, nc, CMP_STRIDE * Dh)
    w1 = jnp.stack(w1s).astype(bf16)
    w2 = jnp.stack(w2s).astype(bf16)
    pe = jnp.stack(pes).reshape(2, 1, CMP_LEN * Dh).astype(bf16)
    return pl.pallas_call(
        _nsa_compress_body,
        grid=(2, B, G),
        in_specs=[pl.BlockSpec((1, 1, 1, nc, CMP_STRIDE * Dh), lambda j, b, g: (j, b, g, 0, 0)),
                  pl.BlockSpec((1, CMP_LEN * Dh, Dh), lambda j, b, g: (j, 0, 0)),
                  pl.BlockSpec((1, Dh, Dh), lambda j, b, g: (j, 0, 0)),
                  pl.BlockSpec((1, 1, CMP_LEN * Dh), lambda j, b, g: (j, 0, 0))],
        out_specs=pl.BlockSpec((1, 1, 1, nc, Dh), lambda j, b, g: (j, b, g, 0, 0)),
        out_shape=jax.ShapeDtypeStruct((2, B, G, nc, Dh), bf16),
        compiler_params=pltpu.CompilerParams(dimension_semantics=("parallel", "parallel", "parallel"),
                                             vmem_limit_bytes=VMEM_LIMIT),
        name="nsa_compress",
    )(r, w1, w2, pe)


def _softmax_rows(s, mask):
    s = jnp.where(mask, s, -jnp.inf)
    m = jnp.max(s, axis=-1, keepdims=True)
    m = jnp.where(m == -jnp.inf, 0.0, m)
    e = jnp.exp(s - m)
    return e, jnp.maximum(jnp.sum(e, axis=-1, keepdims=True), jnp.finfo(f32).tiny)


def _nsa_attn_body(q_ref, cmp_ref, ks_ref, vs_ref, kw_ref, vw_ref, gate_ref, ovt_ref, o_ref, *, tq, tk, S):
    g = pl.program_id(1)
    qi = pl.program_id(2)
    R, Dh = NSA_GROUP, HEAD_DIM
    rows = R * tq
    n_blk = S // SEL_LEN
    n_cmp = S // CMP_STRIDE - 1
    q0 = qi * tq
    q4 = jnp.concatenate([q_ref[0, :, r * Dh:(r + 1) * Dh] for r in range(R)], axis=0)
    nt = (((1,), (1,)), ((), ()))

    def qpos(shape):
        return q0 + lax.broadcasted_iota(jnp.int32, shape, 0) % tq

    k_cmp, v_cmp = cmp_ref[0, 0, 0], cmp_ref[1, 0, 0]
    ncp = k_cmp.shape[0]
    s = lax.dot_general(q4, k_cmp, nt, preferred_element_type=f32)
    n_idx = lax.broadcasted_iota(jnp.int32, (rows, ncp), 1)
    mask = (n_idx * CMP_STRIDE + (CMP_LEN - 1) <= qpos((rows, ncp))) & (n_idx < n_cmp)
    e, den = _softmax_rows(s, mask)
    p = e / den
    o_cmp = jnp.dot(p.astype(bf16), v_cmp, preferred_element_type=f32)

    psum = p[0:tq]
    for r in range(1, R):
        psum = psum + p[r * tq:(r + 1) * tq]
    ph, plo = _split2(psum)
    imp = (lax.dot_general(ovt_ref[...], ph, nt, preferred_element_type=f32)
           + lax.dot_general(ovt_ref[...], plo, nt, preferred_element_type=f32))
    blk = lax.broadcasted_iota(jnp.int32, (n_blk, tq), 0)
    cur = (q0 + lax.broadcasted_iota(jnp.int32, (n_blk, tq), 1)) // SEL_LEN
    visible = blk <= cur
    forced = (blk == 0) | (visible & (blk > cur - N_LOCAL_FORCED))
    imp = jnp.where(forced, FORCED_SCORE, jnp.where(visible, imp, -1.0))
    sel_t = jnp.zeros((n_blk, tq), f32)
    blk_f = blk.astype(f32)
    for _ in range(min(N_SEL, n_blk)):
        m = jnp.max(imp, axis=0, keepdims=True)
        first = jnp.min(jnp.where(imp == m, blk_f, float(n_blk)), axis=0, keepdims=True)
        hit = blk_f == first
        sel_t = jnp.where(hit, 1.0, sel_t)
        imp = jnp.where(hit, -jnp.inf, imp)
    sel_q = jnp.concatenate([sel_t.T, jnp.ones((tq, LANES - n_blk), f32)], axis=1).astype(bf16)

    blocks_per_tile = tk // SEL_LEN

    big = float(2 ** 30)
    kb = sel_q.shape[1]

    def sel_step(kt, carry, causal):
        m_run, l_run, acc = carry
        k0 = pl.multiple_of(kt * tk, tk)
        kt_k = ks_ref[0, pl.ds(k0, tk), :]
        kt_v = vs_ref[0, pl.ds(k0, tk), :]
        s = lax.dot_general(q4, kt_k, nt, preferred_element_type=f32)
        row = lax.broadcasted_iota(jnp.int32, (kb, tk), 0)
        in_blk = row == kt * blocks_per_tile + lax.broadcasted_iota(jnp.int32, (kb, tk), 1) // SEL_LEN
        expand = jnp.where(in_blk, big, jnp.where(row == n_blk, -big, 0.0)).astype(bf16)
        bias = jnp.dot(sel_q, expand, preferred_element_type=f32)
        s = (s.reshape(R, tq, tk) + bias[None]).reshape(rows, tk)
        if causal:
            kpos = k0 + lax.broadcasted_iota(jnp.int32, (rows, tk), 1)
            s = jnp.where(kpos <= qpos((rows, tk)), s, -big)
        m_new = jnp.maximum(m_run, jnp.max(s, axis=-1, keepdims=True))
        alpha = jnp.exp(m_run - m_new)
        e = jnp.exp(s - m_new)
        l_new = alpha * l_run + jnp.sum(e, axis=-1, keepdims=True)
        acc = alpha * acc + jnp.dot(e.astype(bf16), kt_v, preferred_element_type=f32)
        return m_new, l_new, acc

    last_kt = (q0 + tq - 1) // tk
    init = (jnp.full((rows, 1), -jnp.inf, f32), jnp.zeros((rows, 1), f32), jnp.zeros((rows, Dh), f32))
    carry = lax.fori_loop(0, last_kt, functools.partial(sel_step, causal=False), init)
    _, l_sel, acc_sel = sel_step(last_kt, carry, causal=True)
    o_sel = acc_sel / l_sel

    span = WINDOW + tq
    w0 = pl.multiple_of(jnp.maximum(q0 - WINDOW, 0), tq)
    kwin = kw_ref[0, pl.ds(w0, span), :]
    vwin = vw_ref[0, pl.ds(w0, span), :]
    s = lax.dot_general(q4, kwin, nt, preferred_element_type=f32)
    kpos = w0 + lax.broadcasted_iota(jnp.int32, (rows, span), 1)
    qp = qpos((rows, span))
    e, den = _softmax_rows(s, (kpos <= qp) & (kpos > qp - WINDOW))
    o_win = jnp.dot(e.astype(bf16), vwin, preferred_element_type=f32) / den

    gate = jax.nn.sigmoid(gate_ref[0])
    for r in range(R):
        c0 = (g * R + r) * 3
        lane = lax.broadcasted_iota(jnp.int32, (tq, LANES), 1)
        gsel = lambda c: jnp.sum(jnp.where(lane == c0 + c, gate, 0.0), axis=-1, keepdims=True)
        rs = slice(r * tq, (r + 1) * tq)
        o_ref[0, :, r * Dh:(r + 1) * Dh] = (gsel(0) * o_cmp[rs] + gsel(1) * o_sel[rs]
                                            + gsel(2) * o_win[rs]).astype(o_ref.dtype)


def nsa_attention(q_rot, kv_rot, cmp_kv, proj3, B, S, tq=256, tk=512):
    G, R, Dh = NSA_KV_HEADS, NSA_GROUP, HEAD_DIM
    n_blk = S // SEL_LEN
    assert S % tk == 0 and tk % tq == 0 and S >= WINDOW + tq and n_blk % SUB == 0 and n_blk < LANES
    nc = S // CMP_STRIDE
    c0 = np.arange(nc - 1)[None, :] * CMP_STRIDE
    s0 = np.arange(n_blk)[:, None] * SEL_LEN
    ovt = np.zeros((n_blk, nc), np.float32)
    ovt[:, :nc - 1] = np.clip(np.minimum(c0 + CMP_LEN, s0 + SEL_LEN) - np.maximum(c0, s0), 0, None) / CMP_LEN
    kvspec = lambda j: pl.BlockSpec((1, S, Dh), lambda b, g, i: (b, 0, j * G + g))
    return pl.pallas_call(
        functools.partial(_nsa_attn_body, tq=tq, tk=tk, S=S),
        grid=(B, G, S // tq),
        in_specs=[pl.BlockSpec((1, tq, R * Dh), lambda b, g, i: (b, i, g)),
                  pl.BlockSpec((2, 1, 1, nc, Dh), lambda b, g, i: (0, b, g, 0, 0)),
                  kvspec(0), kvspec(1), kvspec(2), kvspec(3),
                  pl.BlockSpec((1, tq, LANES), lambda b, g, i: (b, i, OFF_SMALL // LANES)),
                  pl.BlockSpec((n_blk, nc), lambda b, g, i: (0, 0))],
        out_specs=pl.BlockSpec((1, tq, R * Dh), lambda b, g, i: (b, i, g)),
        out_shape=jax.ShapeDtypeStruct((B, S, D_GRP), bf16),
        compiler_params=pltpu.CompilerParams(dimension_semantics=("parallel", "parallel", "arbitrary"),
                                             vmem_limit_bytes=VMEM_LIMIT),
        name="nsa_attention",
    )(q_rot, cmp_kv, kv_rot, kv_rot, kv_rot, kv_rot, proj3, jnp.asarray(ovt, bf16))


def nsa_mixer_pallas(proj, ck_w1, ck_w2, ck_pe, cv_w1, cv_w2, cv_pe, B, S):
    q_rot, kv_rot, c_rot = nsa_prep(proj, S)
    cmp_kv = nsa_compress(c_rot, (ck_w1, cv_w1), (ck_w2, cv_w2), (ck_pe, cv_pe), B, S)
    return nsa_attention(q_rot.reshape(B, S, D_GRP), kv_rot.reshape(B, S, 4 * NSA_KV_W), cmp_kv,
                         proj.reshape(B, S, P_PAD), B, S)


def _retention_body(q_ref, k_ref, v_ref, g_ref, cos_ref, sin_ref, dec_ref, xi_ref, zeta_ref, gc_ref, o_ref, st_ref,
                    *, T, HS):
    @pl.when(pl.program_id(2) == 0)
    def _():
        st_ref[...] = jnp.zeros_like(st_ref)

    C, Dh = RET_CHUNK, HEAD_DIM
    nt = (((1,), (1,)), ((), ()))
    pre = {}
    for hh in range(HS):
        hc = slice(hh * Dh, (hh + 1) * Dh)
        for c in range(T // C):
            rs = slice(c * C, (c + 1) * C)
            cos, sin = cos_ref[rs, :], sin_ref[rs, :]
            rope = lambda x: x * cos + pltpu.roll(x, Dh // 2, axis=1) * sin
            q = rope(q_ref[0, rs, hc])
            k = rope(k_ref[0, rs, hc]) * Dh ** -0.5
            qb, vb = q.astype(bf16), v_ref[0, rs, hc].astype(bf16)
            inner = lax.dot_general(qb, k.astype(bf16), nt, preferred_element_type=f32) * dec_ref[hh]
            o_in = jnp.dot(inner.astype(bf16), vb, preferred_element_type=f32)
            kv = jnp.dot((k * zeta_ref[hh]).astype(bf16).T, vb, preferred_element_type=f32)
            pre[hh, c] = (qb, o_in, kv)
    states = [st_ref[hh] for hh in range(HS)]
    for c in range(T // C):
        rs = slice(c * C, (c + 1) * C)
        for hh in range(HS):
            hc = slice(hh * Dh, (hh + 1) * Dh)
            qb, o_in, kv = pre[hh, c]
            o = o_in + jnp.dot(qb, states[hh].astype(bf16), preferred_element_type=f32) * xi_ref[hh]
            states[hh] = states[hh] * gc_ref[hh, 0:1, :] + kv
            mu = jnp.mean(o, axis=-1, keepdims=True)
            d = o - mu
            o = d * lax.rsqrt(jnp.mean(d * d, axis=-1, keepdims=True) + NORM_EPS)
            gt = g_ref[0, rs, hc]
            o_ref[0, rs, hc] = (gt * jax.nn.sigmoid(gt) * o).astype(o_ref.dtype)
    for hh in range(HS):
        st_ref[hh] = states[hh]


def retention_mixer_pallas(proj3, B, S, T=512, HS=2):
    H, Dh, C = RET_HEADS, HEAD_DIM, RET_CHUNK
    W = HS * Dh
    assert S % T == 0 and T % C == 0 and H % HS == 0
    freqs = RET_THETA ** (-np.linspace(0.0, 1.0, Dh // 2))
    ang = np.arange(S, dtype=np.float64)[:, None] * freqs[None, :]
    cos = jnp.asarray(np.concatenate([np.cos(ang), np.cos(ang)], 1), f32)
    sin = jnp.asarray(np.concatenate([-np.sin(ang), np.sin(ang)], 1), f32)
    log_gamma = np.log1p(-(2.0 ** (-5.0 - np.arange(H, dtype=np.float64))))
    n = np.arange(C, dtype=np.float64)
    diff = n[:, None] - n[None, :]
    decay_in = np.where(diff >= 0, np.exp(log_gamma[:, None, None] * np.maximum(diff, 0.0)), 0.0)
    xi = np.exp(log_gamma[:, None] * (n + 1.0))[:, :, None] * np.ones((1, 1, Dh))
    zeta = np.exp(log_gamma[:, None] * (C - 1.0 - n))[:, :, None] * np.ones((1, 1, Dh))
    gamma_c = np.exp(log_gamma * C)[:, None, None] * np.ones((1, SUB, Dh))
    blk = lambda j: pl.BlockSpec((1, T, W), lambda b, h, t: (b, t, (OFF_RET + j * D_GRP) // W + h))
    tab = pl.BlockSpec((T, Dh), lambda b, h, t: (t, 0))
    per_head = lambda r: pl.BlockSpec((HS, r, Dh), lambda b, h, t: (h, 0, 0))
    return pl.pallas_call(
        functools.partial(_retention_body, T=T, HS=HS),
        grid=(B, H // HS, S // T),
        in_specs=[blk(0), blk(1), blk(2), blk(3), tab, tab, per_head(C), per_head(C), per_head(C), per_head(SUB)],
        out_specs=pl.BlockSpec((1, T, W), lambda b, h, t: (b, t, h)),
        out_shape=jax.ShapeDtypeStruct((B, S, D_GRP), bf16),
        scratch_shapes=[pltpu.VMEM((HS, Dh, Dh), f32)],
        compiler_params=pltpu.CompilerParams(dimension_semantics=("parallel", "parallel", "arbitrary"),
                                             vmem_limit_bytes=VMEM_LIMIT),
        name="retention",
    )(proj3, proj3, proj3, proj3, cos, sin, jnp.asarray(decay_in, f32), jnp.asarray(xi, f32),
      jnp.asarray(zeta, f32), jnp.asarray(gamma_c, f32))


def _dot3(a, b):
    ah, al = _split2(a)
    bh, bl = _split2(b)
    return (jnp.dot(ah, bh, preferred_element_type=f32) + jnp.dot(ah, bl, preferred_element_type=f32)
            + jnp.dot(al, bh, preferred_element_type=f32))


def _deltanet_body(q_ref, k_ref, v_ref, z_ref, qp_ref, kp_ref, vp_ref, ab_ref, cw_ref, alog_ref, dtb_ref, ng_ref,
                   o_ref, st_ref, *, T, HS):
    hb = pl.program_id(1)
    first = pl.program_id(2) == 0

    @pl.when(first)
    def _():
        st_ref[...] = jnp.zeros_like(st_ref)

    C, Dh = DN_CHUNK, HEAD_DIM
    nt = (((1,), (1,)), ((), ()))
    n_chunks = T // C
    ab = ab_ref[0]
    lane = lax.broadcasted_iota(jnp.int32, ab.shape, 1)

    def conv_silu(x_ref, prev_ref, j, hc):
        prev = jnp.where(first, 0.0, prev_ref[0, :, hc])
        xcat = jnp.concatenate([prev, x_ref[0, :, hc]], axis=0)
        w = cw_ref[:, j, hc]
        y = sum(w[i:i + 1, :] * xcat[SUB - (CONV_WIDTH - 1) + i:SUB - (CONV_WIDTH - 1) + i + T, :]
                for i in range(CONV_WIDTH))
        return y * jax.nn.sigmoid(y)

    def l2n(x):
        return x * lax.rsqrt(jnp.sum(x * x, axis=-1, keepdims=True) + L2NORM_EPS)

    units = [(hh, slice(c * C, (c + 1) * C)) for hh in range(HS) for c in range(n_chunks)]
    qs, ks_, vs_, betas, g_rows = [], [], [], [], []
    for hh in range(HS):
        hc = slice(hh * Dh, (hh + 1) * Dh)
        head = hb * HS + hh
        q_all = l2n(conv_silu(q_ref, qp_ref, 0, hc)) * Dh ** -0.5
        k_all = l2n(conv_silu(k_ref, kp_ref, 1, hc))
        v_all = conv_silu(v_ref, vp_ref, 2, hc)
        a_col = jnp.sum(jnp.where(lane == SMALL_DN_A + head, ab, 0.0), axis=-1, keepdims=True)
        b_col = jnp.sum(jnp.where(lane == SMALL_DN_A + DN_HEADS + head, ab, 0.0), axis=-1, keepdims=True)
        beta_all = jax.nn.sigmoid(b_col)
        g_all = -jnp.exp(alog_ref[hh, 0:1, :]) * jax.nn.softplus(a_col + dtb_ref[hh, 0:1, :])
        for c in range(n_chunks):
            rs = slice(c * C, (c + 1) * C)
            qs.append(q_all[rs]); ks_.append(k_all[rs]); vs_.append(v_all[rs])
            betas.append(beta_all[rs]); g_rows.append(g_all[rs])

    ri = lax.broadcasted_iota(jnp.int32, (C, C), 0)
    ci = lax.broadcasted_iota(jnp.int32, (C, C), 1)
    tril, strict = ri >= ci, ri > ci
    ltri = tril.astype(bf16)
    eye = (ri == ci).astype(f32)
    gcs = []
    for g_rows_u in g_rows:
        gh, gm, gl = _split3(g_rows_u)
        gcs.append(jnp.dot(ltri, gh, preferred_element_type=f32) + jnp.dot(ltri, gm, preferred_element_type=f32)
                   + jnp.dot(ltri, gl, preferred_element_type=f32))
    decays = [jnp.exp(jnp.where(tril, gc[:, :C] - gc[:, :C].T, -jnp.inf)) for gc in gcs]
    kbs = [k.astype(bf16) for k in ks_]
    kbetas = [k * b for k, b in zip(ks_, betas)]
    ms = [jnp.where(strict, lax.dot_general(kbeta.astype(bf16), kb, nt, preferred_element_type=f32) * d, 0.0)
          for kbeta, kb, d in zip(kbetas, kbs, decays)]
    attns = [jnp.where(tril, lax.dot_general(q.astype(bf16), kb, nt, preferred_element_type=f32) * d, 0.0).astype(bf16)
             for q, kb, d in zip(qs, kbs, decays)]
    t_invs, pws = [eye - m for m in ms], ms
    for _ in range(5):
        pws = [_dot3(pw, pw) for pw in pws]
        t_invs = [t + _dot3(t, pw) for t, pw in zip(t_invs, pws)]
    egcs = [jnp.exp(gc) for gc in gcs]
    tbs = [t.astype(bf16) for t in t_invs]
    us = [jnp.dot(tb, (v * b).astype(bf16), preferred_element_type=f32) for tb, v, b in zip(tbs, vs_, betas)]
    ws = [jnp.dot(tb, (kbeta * egc).astype(bf16), preferred_element_type=f32).astype(bf16)
          for tb, kbeta, egc in zip(tbs, kbetas, egcs)]
    states = [st_ref[hh] for hh in range(HS)]
    for c in range(n_chunks):
        for hh in range(HS):
            u, rs = hh * n_chunks + c, slice(c * C, (c + 1) * C)
            g_last = gcs[u][C - 1:C, :]
            q_dec = (qs[u] * egcs[u]).astype(bf16)
            k_dec = (ks_[u] * jnp.exp(g_last - gcs[u])).astype(bf16)
            sb = states[hh].astype(bf16)
            v_new = us[u] - jnp.dot(ws[u], sb, preferred_element_type=f32)
            vnb = v_new.astype(bf16)
            o = jnp.dot(q_dec, sb, preferred_element_type=f32) + jnp.dot(attns[u], vnb, preferred_element_type=f32)
            states[hh] = states[hh] * jnp.exp(g_last) + jnp.dot(k_dec.T, vnb, preferred_element_type=f32)
            o = o * lax.rsqrt(jnp.mean(o * o, axis=-1, keepdims=True) + NORM_EPS) * ng_ref[...]
            zt = z_ref[0, rs, hh * Dh:(hh + 1) * Dh]
            o_ref[0, rs, hh * Dh:(hh + 1) * Dh] = (o * (zt * jax.nn.sigmoid(zt))).astype(o_ref.dtype)
    for hh in range(HS):
        st_ref[hh] = states[hh]


def deltanet_mixer_pallas(proj3, conv_w, a_log, dt_bias, norm_g, B, S, T=256, HS=4):
    H, Dh = DN_HEADS, HEAD_DIM
    W = HS * Dh
    assert S % T == 0 and T % DN_CHUNK == 0 and H % HS == 0
    blk = lambda j: pl.BlockSpec((1, T, W), lambda b, h, t: (b, t, (OFF_DN + j * D_GRP) // W + h))
    prev = lambda j: pl.BlockSpec((1, SUB, W), lambda b, h, t: (b, jnp.maximum(t * (T // SUB) - 1, 0),
                                                               (OFF_DN + j * D_GRP) // W + h))
    per_head = pl.BlockSpec((HS, SUB, Dh), lambda b, h, t: (h, 0, 0))
    bcast = lambda x: jnp.broadcast_to(x.astype(f32)[:, None, None], (H, SUB, Dh))
    return pl.pallas_call(
        functools.partial(_deltanet_body, T=T, HS=HS),
        grid=(B, H // HS, S // T),
        in_specs=[blk(0), blk(1), blk(2), blk(3), prev(0), prev(1), prev(2),
                  pl.BlockSpec((1, T, LANES), lambda b, h, t: (b, t, OFF_SMALL // LANES)),
                  pl.BlockSpec((CONV_WIDTH, 3, W), lambda b, h, t: (0, 0, h)),
                  per_head, per_head,
                  pl.BlockSpec((1, Dh), lambda b, h, t: (0, 0))],
        out_specs=pl.BlockSpec((1, T, W), lambda b, h, t: (b, t, h)),
        out_shape=jax.ShapeDtypeStruct((B, S, D_GRP), bf16),
        scratch_shapes=[pltpu.VMEM((HS, Dh, Dh), f32)],
        compiler_params=pltpu.CompilerParams(dimension_semantics=("parallel", "parallel", "arbitrary"),
                                             vmem_limit_bytes=VMEM_LIMIT),
        name="deltanet",
    )(proj3, proj3, proj3, proj3, proj3, proj3, proj3, proj3, conv_w.reshape(CONV_WIDTH, 3, D_GRP),
      bcast(a_log), bcast(dt_bias), norm_g.reshape(1, Dh))


def _w_in_pieces():
    n0, r0, d0, w0 = 0, NSA_COLS, NSA_COLS + RET_COLS, NSA_COLS + RET_COLS + DN_COLS
    n_gate = 3 * NSA_HEADS
    return [
        (OFF_RW, w0, 3 * D_GRP),
        (OFF_RET, r0, 4 * D_GRP),
        (OFF_DN, d0, 3 * D_GRP),
        (OFF_DN + 3 * D_GRP, d0 + 3 * D_GRP + 2 * DN_HEADS, D_GRP),
        (OFF_NSA_Q, n0, D_GRP + 6 * NSA_KV_W),
        (OFF_LORA, w0 + 3 * D_GRP, RW_LORA),
        (OFF_SMALL, n0 + D_GRP + 6 * NSA_KV_W, n_gate),
        (OFF_SMALL + SMALL_DN_A, d0 + 3 * D_GRP, 2 * DN_HEADS),
    ]


def _pack_w_in_body(w_ref, o_ref):
    cols = o_ref.shape[2]
    for off, width in ((OFF_LORA + RW_LORA, RW_LORA_PAD - RW_LORA), (OFF_SMALL, LANES)):
        o_ref[0, off:off + width, :] = jnp.zeros((width, cols), o_ref.dtype)
    for dst, src, width in _w_in_pieces():
        o_ref[0, dst:dst + width, :] = w_ref[0, src:src + width, :].astype(o_ref.dtype)


def pack_w_in_t(w_in, tc=256):
    L, K = w_in.shape[0], w_in.shape[1]
    w_t = jnp.swapaxes(w_in, 1, 2)
    return pl.pallas_call(
        _pack_w_in_body,
        grid=(L, K // tc),
        in_specs=[pl.BlockSpec((1, P_TOTAL, tc), lambda l, i: (l, 0, i))],
        out_specs=pl.BlockSpec((1, P_PAD, tc), lambda l, i: (l, 0, i)),
        out_shape=jax.ShapeDtypeStruct((L, P_PAD, K), bf16),
        compiler_params=pltpu.CompilerParams(dimension_semantics=("parallel", "parallel"),
                                             vmem_limit_bytes=VMEM_LIMIT),
        name="pack_w_in",
    )(w_t)


def hybrid_mixer(h, hn, B, S, layer, w_in_packed, w_out, ck_w1, ck_w2, ck_pe, cv_w1, cv_w2, cv_pe, dn_conv, dn_a_log,
                 dn_dt_bias, dn_norm, rw_mu, rw_w0, rw_w2, rw_a0, rw_a2, rw_g2, rw_kk, rw_ka, rw_rk, rw_ln_g, rw_ln_b):
    M = B * S
    proj = matmul_nt(hn, w_in_packed, layer, name="in_proj")
    p3 = proj.reshape(B, S, P_PAD)
    o_nsa = nsa_mixer_pallas(proj, ck_w1, ck_w2, ck_pe, cv_w1, cv_w2, cv_pe, B, S)
    o_ret = retention_mixer_pallas(p3, B, S)
    o_dn = deltanet_mixer_pallas(p3, dn_conv, dn_a_log, dn_dt_bias, dn_norm, B, S)
    o_rw = rwkv7_mixer(proj, OFF_RW // (3 * D_GRP), OFF_LORA // RW_LORA_PAD, rw_mu, rw_w0, rw_w2, rw_a0, rw_a2,
                       rw_g2, rw_kk, rw_ka, rw_rk, rw_ln_g, rw_ln_b, B, S)
    parts = [o.reshape(M, D_GRP) for o in (o_nsa, o_ret, o_dn, o_rw)]
    return out_proj_residual(parts, w_out.astype(bf16), h)


def kernel(x, p, norm_mix, w_in, w_out, nsa_ck_w1, nsa_ck_w2, nsa_ck_pe, nsa_cv_w1, nsa_cv_w2, nsa_cv_pe, dn_conv, dn_a_log, dn_dt_bias, dn_norm, rw_mu, rw_w0, rw_w2, rw_a0, rw_a2, rw_g2, rw_kk, rw_ka, rw_rk, rw_ln_g, rw_ln_b, norm_moe, moe_router_grp, moe_router_grp_b, moe_router_exp, moe_router_exp_b, moe_w_gate, moe_w_up, moe_w_down, norm_ple, ple_w, ple_gate, norm_final):
    B, S, D = x.shape
    M = B * S
    h = x.reshape(M, D)
    w_in_packed = pack_w_in_t(w_in)
    for i in range(DEPTH):
        hn = rms_norm_rows(h, norm_mix[i], bf16)
        h = hybrid_mixer(h, hn, B, S, i, w_in_packed, w_out[i], nsa_ck_w1[i], nsa_ck_w2[i], nsa_ck_pe[i], nsa_cv_w1[i],
                         nsa_cv_w2[i], nsa_cv_pe[i], dn_conv[i], dn_a_log[i], dn_dt_bias[i], dn_norm[i],
                         rw_mu[i], rw_w0[i], rw_w2[i], rw_a0[i], rw_a2[i], rw_g2[i], rw_kk[i], rw_ka[i],
                         rw_rk[i], rw_ln_g[i], rw_ln_b[i])
        hn = rms_norm_rows(h, norm_moe[i], bf16)
        ya, yb = hier_moe(hn, moe_router_grp[i], moe_router_grp_b[i], moe_router_exp[i], moe_router_exp_b[i],
                          moe_w_gate, moe_w_up, moe_w_down, i)
        h, hn = add_rms_norm_rows(h, ya, yb, norm_ple[i], bf16)
        h = ple_residual(hn, ple_gate[i].astype(bf16), p[i].reshape(M, PLE_DIM), ple_w[i].astype(bf16), h)
    return rms_norm_rows(h, norm_final, f32).reshape(B, S, D)
```

```python
import functools

import jax
import jax.numpy as jnp
import numpy as np
from jax import lax
from jax.experimental import pallas as pl
from jax.experimental.pallas import tpu as pltpu

D_MODEL = 4096
DEPTH = 2
f32 = jnp.float32
bf16 = jnp.bfloat16
D_MIX = D_MODEL
D_GRP = D_MIX // 4
HEAD_DIM = 128
NORM_EPS = 1e-6
L2NORM_EPS = 1e-6
NSA_HEADS = D_GRP // HEAD_DIM
NSA_KV_HEADS = 2
NSA_GROUP = NSA_HEADS // NSA_KV_HEADS
NSA_KV_W = NSA_KV_HEADS * HEAD_DIM
CMP_LEN = 32
CMP_STRIDE = 16
SEL_LEN = 64
N_SEL = 16
N_LOCAL_FORCED = 2
FORCED_SCORE = 1e4
WINDOW = 512
ROPE_THETA = 500000.0
ROPE_DIMS = HEAD_DIM // 4
RET_HEADS = D_GRP // HEAD_DIM
RET_CHUNK = 128
RET_THETA = 10000.0
DN_HEADS = D_GRP // HEAD_DIM
DN_CHUNK = 64
CONV_WIDTH = 4
RW_HEAD_DIM = 64
RW_DECAY_LORA = 64
RW_AAA_LORA = 64
RW_GATE_LORA = 160
RW_LN_EPS = 64e-5
N_GROUPS = 4
EXPERTS_PER_GROUP = 8
N_EXPERTS = N_GROUPS * EXPERTS_PER_GROUP
TOP_K = 2
D_EXPERT = 768
ROW_BLOCK = 256
PLE_DIM = 256
NSA_COLS = D_GRP + 6 * NSA_KV_W + 3 * NSA_HEADS
RET_COLS = 4 * D_GRP
DN_COLS = 4 * D_GRP + 2 * DN_HEADS
RW_COLS = 3 * D_GRP + RW_DECAY_LORA + RW_AAA_LORA + RW_GATE_LORA
P_TOTAL = NSA_COLS + RET_COLS + DN_COLS + RW_COLS

RW_LORA = RW_DECAY_LORA + RW_AAA_LORA + RW_GATE_LORA
RW_LORA_PAD = 384
RW_GROUPS_PER_BODY = 4
LANES = 128
SUB = 8

OFF_RW = 0
OFF_RET = 3 * D_GRP
OFF_DN = OFF_RET + 4 * D_GRP
OFF_NSA_Q = OFF_DN + 4 * D_GRP
OFF_NSA_KV = OFF_NSA_Q + D_GRP
OFF_LORA = OFF_NSA_KV + 6 * NSA_KV_W
OFF_SMALL = OFF_LORA + RW_LORA_PAD
SMALL_DN_A = 32
P_PAD = OFF_SMALL + LANES
assert P_PAD == 14336 and OFF_LORA % RW_LORA_PAD == 0

V7X_VMEM_BYTES = 64 * 1024 * 1024
VMEM_LIMIT = V7X_VMEM_BYTES * 7 // 8


def _split2(x):
    h = x.astype(bf16)
    return h, (x - h.astype(f32)).astype(bf16)


def _split3(x):
    h = x.astype(bf16)
    r1 = x - h.astype(f32)
    m = r1.astype(bf16)
    return h, m, (r1 - m.astype(f32)).astype(bf16)


def _rwkv_prep_body(x_ref, xprev_ref, l_ref, lprev_ref, mu_ref, mul_ref, w0_ref, a0_ref, kkw_ref, ka_ref,
                    w2_ref, a2_ref, g2_ref, e_ref, r_ref, w_ref, k_ref, v_ref, kk_ref, b_ref, g_ref, *, tm, S):
    i = pl.program_id(0)
    first = (i * tm) % S == 0

    def token_mix(x, prev8, mu):
        rows = lax.broadcasted_iota(jnp.int32, x.shape, 0)
        prev_row = jnp.where(first, 0.0, prev8[SUB - 1:SUB, :])
        x_prev = jnp.where(rows == 0, prev_row, pltpu.roll(x, 1, axis=0))
        return x + (x_prev - x) * mu

    x = token_mix(x_ref[...], xprev_ref[...], mu_ref[...])
    lora = token_mix(l_ref[...], lprev_ref[...], mul_ref[...])
    r = x[:, 0:D_GRP]
    k = x[:, D_GRP:2 * D_GRP]
    v = x[:, 2 * D_GRP:3 * D_GRP]
    dw = jnp.dot(jnp.tanh(lora).astype(bf16), w2_ref[...], preferred_element_type=f32)
    da = jnp.dot(lora.astype(bf16), a2_ref[...], preferred_element_type=f32)
    g = jnp.dot(jax.nn.sigmoid(lora).astype(bf16), g2_ref[...], preferred_element_type=f32)
    w = -jax.nn.softplus(-(w0_ref[...] + dw)) - 0.5
    decay = jnp.exp(-jnp.exp(w))
    a = jax.nn.sigmoid(a0_ref[...] + da)
    kq = k * kkw_ref[...]
    sq_h, sq_l = _split2(kq * kq)
    ss = (jnp.dot(sq_h, e_ref[...], preferred_element_type=f32)
          + jnp.dot(sq_l, e_ref[...], preferred_element_type=f32))
    kk = kq * lax.rsqrt(ss + L2NORM_EPS)
    r_ref[...] = r
    w_ref[...] = decay
    k_ref[...] = k * (1.0 + (a - 1.0) * ka_ref[...])
    v_ref[...] = v
    kk_ref[...] = kk
    b_ref[...] = kk * a
    g_ref[...] = g


def rwkv_prep(proj, rkv_blk, lora_blk, mu, w0, w2, a0, a2, g2, k_k, k_a, S, tm=256):
    M = proj.shape[0]
    assert M % tm == 0 and S % tm == 0
    W3 = 3 * D_GRP
    mu_x = mu[:W3].reshape(1, W3)
    mu_l = jnp.pad(mu[W3:], (0, RW_LORA_PAD - RW_LORA)).reshape(1, RW_LORA_PAD)
    w2p = jnp.zeros((RW_LORA_PAD, D_GRP), f32).at[0:RW_DECAY_LORA].set(w2).astype(bf16)
    a2p = jnp.zeros((RW_LORA_PAD, D_GRP), f32).at[RW_DECAY_LORA:RW_DECAY_LORA + RW_AAA_LORA].set(a2).astype(bf16)
    g2p = jnp.zeros((RW_LORA_PAD, D_GRP), f32).at[RW_DECAY_LORA + RW_AAA_LORA:RW_LORA].set(g2).astype(bf16)
    hid = np.arange(D_GRP) // RW_HEAD_DIM
    e = jnp.asarray(hid[:, None] == hid[None, :], bf16)
    row = lambda t: t.reshape(1, D_GRP)
    vec = pl.BlockSpec((1, D_GRP), lambda i: (0, 0))
    wspec = pl.BlockSpec((RW_LORA_PAD, D_GRP), lambda i: (0, 0))
    ospec = pl.BlockSpec((tm, D_GRP), lambda i: (i, 0))
    prev_row_blk = lambda i: jnp.maximum(i * (tm // SUB) - 1, 0)
    return pl.pallas_call(
        functools.partial(_rwkv_prep_body, tm=tm, S=S),
        grid=(M // tm,),
        in_specs=[pl.BlockSpec((tm, W3), lambda i: (i, rkv_blk)),
                  pl.BlockSpec((SUB, W3), lambda i: (prev_row_blk(i), rkv_blk)),
                  pl.BlockSpec((tm, RW_LORA_PAD), lambda i: (i, lora_blk)),
                  pl.BlockSpec((SUB, RW_LORA_PAD), lambda i: (prev_row_blk(i), lora_blk)),
                  pl.BlockSpec((1, W3), lambda i: (0, 0)),
                  pl.BlockSpec((1, RW_LORA_PAD), lambda i: (0, 0)),
                  vec, vec, vec, vec, wspec, wspec, wspec,
                  pl.BlockSpec((D_GRP, D_GRP), lambda i: (0, 0))],
        out_specs=[ospec] * 7,
        out_shape=[jax.ShapeDtypeStruct((M, D_GRP), f32)] * 7,
        compiler_params=pltpu.CompilerParams(dimension_semantics=("parallel",), vmem_limit_bytes=VMEM_LIMIT),
        name="rwkv_prep",
    )(proj, proj, proj, proj, mu_x, mu_l, row(w0), row(a0), row(k_k), row(k_a), w2p, a2p, g2p, e)


def _rwkv_scan_body(r_ref, w_ref, k_ref, v_ref, kk_ref, b_ref, g_ref, lng_ref, lnb_ref, rk_ref, o_ref, s_ref,
                    vc_ref, y_ref, *, G, T):
    tb = pl.program_id(2)

    @pl.when(tb == 0)
    def _():
        s_ref[...] = jnp.zeros_like(s_ref)

    Dh = RW_HEAD_DIM
    lo = lax.broadcasted_iota(jnp.int32, (Dh, LANES), 1) < Dh
    lo8 = lax.broadcasted_iota(jnp.int32, (SUB, LANES), 1) < Dh
    sub_iota = lax.broadcasted_iota(jnp.int32, (4 * SUB, LANES), 0)
    row_iota = lax.broadcasted_iota(jnp.int32, (SUB, LANES), 0)

    def seg_sum(x, m):
        s_lo = jnp.sum(jnp.where(m, x, 0.0), axis=-1, keepdims=True)
        s_hi = jnp.sum(jnp.where(m, 0.0, x), axis=-1, keepdims=True)
        return jnp.where(m, s_lo, s_hi)

    cols = [slice(p * LANES, (p + 1) * LANES) for p in range(G)]
    n_groups = T // SUB

    k_iota = lax.broadcasted_iota(jnp.int32, (8 * SUB, LANES), 0)
    k_head = k_iota // (4 * SUB) == lax.broadcasted_iota(jnp.int32, (8 * SUB, LANES), 1) // Dh
    k_piece = (k_iota // SUB) % 4

    def v_pieces(gi):
        t0 = pl.multiple_of(gi * SUB, SUB)
        vts = []
        for p in range(G):
            vh, vm, vl = _split3(v_ref[0, pl.ds(t0, SUB), cols[p]])
            vp = jnp.concatenate([vh, vm, vl, jnp.zeros_like(vh)], axis=0)
            vts.append(jnp.concatenate([vp[:, :Dh], vp[:, Dh:]], axis=0).T)
        return vts

    def v_columns(vts, i, slot):
        sel = (k_head & (k_piece < 3) & (k_iota % SUB == i)).astype(bf16)
        for p in range(G):
            vc_ref[slot, i, p] = jnp.dot(vts[p], sel, preferred_element_type=f32)

    vts0 = v_pieces(0)
    for i in range(SUB):
        v_columns(vts0, i, 0)

    def run_group(gi, slot, states):
        t0 = pl.multiple_of(gi * SUB, SUB)
        vts_next = v_pieces(jnp.minimum(gi + 1, n_groups - 1))
        r8b = [r_ref[0, pl.ds(t0, SUB), cols[p]].astype(bf16) for p in range(G)]
        yacc = [jnp.zeros((SUB, LANES), f32) for _ in range(G)]
        tiles = [[ref[0, pl.ds(t0, SUB), cols[p]] for ref in (kk_ref, w_ref, b_ref, k_ref)] for p in range(G)]
        for i in range(SUB):
            for p in range(G):
                kk_r, w_r, b_r, k_r = [jnp.broadcast_to(t8[i:i + 1, :], (Dh, LANES)) for t8 in tiles[p]]
                S = states[p]
                sab = -seg_sum(S * kk_r, lo)
                states[p] = S * w_r + sab * b_r + vc_ref[slot, i, p] * k_r
            for p in range(G):
                Sb = states[p].astype(bf16)
                zero = jnp.zeros_like(Sb)
                mt = jnp.concatenate([jnp.where(lo, Sb, zero), jnp.where(lo, zero, Sb)], axis=0)
                y8 = lax.dot_general(r8b[p], mt, (((1,), (1,)), ((), ())), preferred_element_type=f32)
                yacc[p] = jnp.where(row_iota == i, y8, yacc[p])
            v_columns(vts_next, i, 1 - slot)
        for p in range(G):
            y_ref[pl.ds(t0, SUB), cols[p]] = yacc[p]
        return states

    def group_batch(gj, carry):
        states = [s_ref[p] for p in range(G)]
        for u in range(RW_GROUPS_PER_BODY):
            states = run_group(RW_GROUPS_PER_BODY * gj + u, u % 2, states)
        for p in range(G):
            s_ref[p] = states[p]
        return carry

    assert n_groups % RW_GROUPS_PER_BODY == 0 and RW_GROUPS_PER_BODY % 2 == 0
    lax.fori_loop(0, n_groups // RW_GROUPS_PER_BODY, group_batch, 0)

    lo_t = lax.broadcasted_iota(jnp.int32, (T, LANES), 1) < Dh
    for p in range(G):
        y = y_ref[:, cols[p]]
        mean = seg_sum(y, lo_t) * (1.0 / Dh)
        d = y - mean
        var = seg_sum(d * d, lo_t) * (1.0 / Dh)
        yn = d * lax.rsqrt(var + RW_LN_EPS) * lng_ref[:, cols[p]] + lnb_ref[:, cols[p]]
        bonus = seg_sum(r_ref[0, :, cols[p]] * k_ref[0, :, cols[p]] * rk_ref[:, cols[p]], lo_t) * v_ref[0, :, cols[p]]
        o_ref[0, :, cols[p]] = (yn + bonus) * g_ref[0, :, cols[p]]


def rwkv_scan(r, w, k, v, kk, b, g, ln_g, ln_b, r_k, G=8, T=512):
    B, S, D = r.shape
    n_pairs = D // LANES
    assert n_pairs % G == 0 and S % T == 0
    spec = pl.BlockSpec((1, T, G * LANES), lambda bi, pi, ti: (bi, ti, pi))
    vec = pl.BlockSpec((1, G * LANES), lambda bi, pi, ti: (0, pi))
    row = lambda t: t.reshape(1, D)
    return pl.pallas_call(
        functools.partial(_rwkv_scan_body, G=G, T=T),
        grid=(B, n_pairs // G, S // T),
        in_specs=[spec] * 7 + [vec] * 3,
        out_specs=spec,
        out_shape=jax.ShapeDtypeStruct((B, S, D), f32),
        scratch_shapes=[pltpu.VMEM((G, RW_HEAD_DIM, LANES), f32),
                        pltpu.VMEM((2, SUB, G, RW_HEAD_DIM, LANES), f32),
                        pltpu.VMEM((T, G * LANES), f32)],
        compiler_params=pltpu.CompilerParams(dimension_semantics=("parallel", "parallel", "arbitrary"),
                                             vmem_limit_bytes=VMEM_LIMIT),
        name="rwkv_scan",
    )(r, w, k, v, kk, b, g, row(ln_g), row(ln_b), row(r_k))


def rwkv7_mixer(proj, rkv_blk, lora_blk, mu, w0, w2, a0, a2, g2, k_k, k_a, r_k, ln_g, ln_b, B, S):
    outs = rwkv_prep(proj, rkv_blk, lora_blk, mu, w0, w2, a0, a2, g2, k_k, k_a, S)
    r, w, k, v, kk, b, g = [t.reshape(B, S, D_GRP) for t in outs]
    return rwkv_scan(r, w, k, v, kk, b, g, ln_g, ln_b, r_k.reshape(-1))


def _mm_nt_body(a_ref, bt_ref, o_ref, acc_ref):
    k = pl.program_id(2)

    @pl.when(k == 0)
    def _():
        acc_ref[...] = jnp.zeros_like(acc_ref)

    acc_ref[...] += lax.dot_general(a_ref[...], bt_ref[...], (((1,), (1,)), ((), ())), preferred_element_type=f32)

    @pl.when(k == pl.num_programs(2) - 1)
    def _():
        o_ref[...] = acc_ref[...].astype(o_ref.dtype)


def matmul_nt(a, bt, layer, out_dtype=f32, tm=1024, tn=1024, tk=2048, name="matmul_nt"):
    M, K = a.shape
    _, N, _ = bt.shape
    assert M % tm == 0 and N % tn == 0 and K % tk == 0, (M, N, K, tm, tn, tk)
    return pl.pallas_call(
        _mm_nt_body,
        grid=(M // tm, N // tn, K // tk),
        in_specs=[pl.BlockSpec((tm, tk), lambda i, j, k: (i, k)),
                  pl.BlockSpec((None, tn, tk), lambda i, j, k: (layer, j, k))],
        out_specs=pl.BlockSpec((tm, tn), lambda i, j, k: (i, j)),
        out_shape=jax.ShapeDtypeStruct((M, N), out_dtype),
        scratch_shapes=[pltpu.VMEM((tm, tn), f32)],
        compiler_params=pltpu.CompilerParams(
            dimension_semantics=("parallel", "parallel", "arbitrary"), vmem_limit_bytes=VMEM_LIMIT),
        name=name,
    )(a, bt)


def _rms_norm_body(x_ref, g_ref, o_ref):
    x = x_ref[...]
    y = x * lax.rsqrt(jnp.mean(x * x, axis=-1, keepdims=True) + NORM_EPS)
    o_ref[...] = (y * g_ref[...]).astype(o_ref.dtype)


def rms_norm_rows(x, g, out_dtype, tm=256):
    M, D = x.shape
    return pl.pallas_call(
        _rms_norm_body,
        grid=(M // tm,),
        in_specs=[pl.BlockSpec((tm, D), lambda i: (i, 0)), pl.BlockSpec((1, D), lambda i: (0, 0))],
        out_specs=pl.BlockSpec((tm, D), lambda i: (i, 0)),
        out_shape=jax.ShapeDtypeStruct((M, D), out_dtype),
        compiler_params=pltpu.CompilerParams(dimension_semantics=("parallel",), vmem_limit_bytes=VMEM_LIMIT),
        name="rms_norm",
    )(x, g.reshape(1, D).astype(f32))


def _add_rms_norm_body(h_ref, ya_ref, yb_ref, g_ref, ho_ref, o_ref):
    x = h_ref[...] + (ya_ref[...] + yb_ref[...])
    ho_ref[...] = x
    y = x * lax.rsqrt(jnp.mean(x * x, axis=-1, keepdims=True) + NORM_EPS)
    o_ref[...] = (y * g_ref[...]).astype(o_ref.dtype)


def add_rms_norm_rows(h, ya, yb, g, out_dtype, tm=256):
    M, D = h.shape
    row = pl.BlockSpec((tm, D), lambda i: (i, 0))
    return pl.pallas_call(
        _add_rms_norm_body,
        grid=(M // tm,),
        in_specs=[row, row, row, pl.BlockSpec((1, D), lambda i: (0, 0))],
        out_specs=[row, row],
        out_shape=[jax.ShapeDtypeStruct((M, D), f32), jax.ShapeDtypeStruct((M, D), out_dtype)],
        compiler_params=pltpu.CompilerParams(dimension_semantics=("parallel",), vmem_limit_bytes=VMEM_LIMIT),
        name="add_rms_norm",
    )(h, ya, yb, g.reshape(1, D).astype(f32))


def _out_proj_body(a0_ref, a1_ref, a2_ref, a3_ref, b_ref, h_ref, o_ref, acc_ref):
    k = pl.program_id(2)

    @pl.when(k == 0)
    def _():
        acc_ref[...] = h_ref[...]

    for j, a_ref in enumerate((a0_ref, a1_ref, a2_ref, a3_ref)):
        @pl.when(k == j)
        def _(a_ref=a_ref):
            acc_ref[...] += jnp.dot(a_ref[...].astype(bf16), b_ref[...].astype(bf16), preferred_element_type=f32)

    @pl.when(k == pl.num_programs(2) - 1)
    def _():
        o_ref[...] = acc_ref[...]


def out_proj_residual(parts, w, layer, h, tm=1024, tn=1024):
    M, D = h.shape
    aspec = pl.BlockSpec((tm, D_GRP), lambda i, j, k: (i, 0))
    return pl.pallas_call(
        _out_proj_body,
        grid=(M // tm, D // tn, len(parts)),
        in_specs=[aspec] * 4 + [pl.BlockSpec((None, D_GRP, tn), lambda i, j, k: (layer, k, j)),
                                pl.BlockSpec((tm, tn), lambda i, j, k: (i, j))],
        out_specs=pl.BlockSpec((tm, tn), lambda i, j, k: (i, j)),
        out_shape=jax.ShapeDtypeStruct((M, D), f32),
        scratch_shapes=[pltpu.VMEM((tm, tn), f32)],
        compiler_params=pltpu.CompilerParams(
            dimension_semantics=("parallel", "parallel", "arbitrary"), vmem_limit_bytes=VMEM_LIMIT),
        name="out_proj",
    )(*parts, w, h)


def _ple_body(a_ref, b_ref, p_ref, pw_ref, h_ref, o_ref, acc_ref):
    k = pl.program_id(2)

    @pl.when(k == 0)
    def _():
        acc_ref[...] = jnp.zeros_like(acc_ref)

    acc_ref[...] += jnp.dot(a_ref[...], b_ref[...], preferred_element_type=f32)

    @pl.when(k == pl.num_programs(2) - 1)
    def _():
        emb = jnp.dot(p_ref[...].astype(bf16), pw_ref[...], preferred_element_type=f32)
        o_ref[...] = h_ref[...] + emb * jax.nn.sigmoid(acc_ref[...])


def ple_residual(hn, gate_w, p, ple_w, h, tm=1024, tn=1024, tk=2048):
    M, D = h.shape
    return pl.pallas_call(
        _ple_body,
        grid=(M // tm, D // tn, D // tk),
        in_specs=[pl.BlockSpec((tm, tk), lambda i, j, k: (i, k)),
                  pl.BlockSpec((tk, tn), lambda i, j, k: (k, j)),
                  pl.BlockSpec((tm, PLE_DIM), lambda i, j, k: (i, 0)),
                  pl.BlockSpec((PLE_DIM, tn), lambda i, j, k: (0, j)),
                  pl.BlockSpec((tm, tn), lambda i, j, k: (i, j))],
        out_specs=pl.BlockSpec((tm, tn), lambda i, j, k: (i, j)),
        out_shape=jax.ShapeDtypeStruct((M, D), f32),
        scratch_shapes=[pltpu.VMEM((tm, tn), f32)],
        compiler_params=pltpu.CompilerParams(
            dimension_semantics=("parallel", "parallel", "arbitrary"), vmem_limit_bytes=VMEM_LIMIT),
        name="ple",
    )(hn, gate_w, p, ple_w, h)


def _moe_router_body(x_ref, w_ref, b_ref, id_ref, wt_ref):
    logits = jnp.dot(x_ref[...].astype(bf16), w_ref[...], preferred_element_type=f32) + b_ref[...]
    lane = lax.broadcasted_iota(jnp.int32, logits.shape, 1).astype(f32)
    ninf = -jnp.inf

    def top1(vals):
        m = jnp.max(vals, axis=-1, keepdims=True)
        idx = jnp.min(jnp.where(vals == m, lane, float(LANES)), axis=-1, keepdims=True)
        return m, idx

    gl = jnp.where(lane < N_GROUPS, logits, ninf)
    g_max, g_idx = top1(gl)
    g_w = 1.0 / jnp.sum(jnp.exp(gl - g_max), axis=-1, keepdims=True)
    lo = N_GROUPS + g_idx * EXPERTS_PER_GROUP
    el = jnp.where((lane >= lo) & (lane < lo + EXPERTS_PER_GROUP), logits, ninf)
    e_max, _ = top1(el)
    pe = jnp.exp(el - e_max)
    p = pe / jnp.sum(pe, axis=-1, keepdims=True)
    p = jnp.where(el == ninf, ninf, p)
    p1, i1 = top1(p)
    p2, i2 = top1(jnp.where(lane == i1, ninf, p))
    scale = g_w / (p1 + p2)
    id_ref[...] = jnp.where(lane == 0, i1 - N_GROUPS, jnp.where(lane == 1, i2 - N_GROUPS, 0.0)).astype(jnp.int32)
    wt_ref[...] = jnp.where(lane == 0, p1 * scale, jnp.where(lane == 1, p2 * scale, 0.0))


def moe_router(xb, router_grp, router_grp_b, router_exp, router_exp_b, tm=512):
    M, D = xb.shape
    n_log = N_GROUPS + N_EXPERTS
    w = jnp.pad(jnp.concatenate([router_grp, router_exp], axis=1), ((0, 0), (0, LANES - n_log))).astype(bf16)
    b = jnp.pad(jnp.concatenate([router_grp_b, router_exp_b]), (0, LANES - n_log)).reshape(1, LANES).astype(f32)
    ids, wts = pl.pallas_call(
        _moe_router_body,
        grid=(M // tm,),
        in_specs=[pl.BlockSpec((tm, D), lambda i: (i, 0)), pl.BlockSpec((D, LANES), lambda i: (0, 0)),
                  pl.BlockSpec((1, LANES), lambda i: (0, 0))],
        out_specs=[pl.BlockSpec((tm, LANES), lambda i: (i, 0))] * 2,
        out_shape=[jax.ShapeDtypeStruct((M, LANES), jnp.int32), jax.ShapeDtypeStruct((M, LANES), f32)],
        compiler_params=pltpu.CompilerParams(dimension_semantics=("parallel",), vmem_limit_bytes=VMEM_LIMIT),
        name="moe_router",
    )(xb, w, b)
    return ids[:, :TOP_K], wts[:, :TOP_K]


def _moe_up_body(be_ref, nu_ref, x_ref, wg_ref, wu_ref, h_ref, *, tf):
    i = pl.program_id(0)

    @pl.when(i < nu_ref[0])
    def _():
        x = x_ref[...].astype(bf16)
        for f in range(D_EXPERT // tf):
            c = slice(f * tf, (f + 1) * tf)
            hg = jnp.dot(x, wg_ref[0, 0, :, c].astype(bf16), preferred_element_type=f32)
            hu = jnp.dot(x, wu_ref[0, 0, :, c].astype(bf16), preferred_element_type=f32)
            h_ref[:, c] = (hg * jax.nn.sigmoid(hg) * hu).astype(bf16)

    @pl.when(i >= nu_ref[0])
    def _():
        h_ref[...] = jnp.zeros_like(h_ref)


def _moe_down_body(be_ref, nu_ref, h_ref, wd_ref, wt_ref, y_ref):
    i = pl.program_id(0)

    @pl.when(i < nu_ref[0])
    def _():
        y_ref[...] = jnp.dot(h_ref[...], wd_ref[0, 0].astype(bf16), preferred_element_type=f32) * wt_ref[...]

    @pl.when(i >= nu_ref[0])
    def _():
        y_ref[...] = jnp.zeros_like(y_ref)


def moe_experts(x_pad, w_pad, blk_expert, n_used, wg, wu, wd, layer, tf=256):
    A_pad, D = x_pad.shape
    n_blk = A_pad // ROW_BLOCK
    wmap = lambda i, be, nu: (layer, be[i], 0, 0)
    wspec = pl.BlockSpec((1, 1, D, D_EXPERT), wmap, pipeline_mode=pl.Buffered(1))
    wspec2 = pl.BlockSpec((1, 1, D, D_EXPERT), wmap)
    h = pl.pallas_call(
        functools.partial(_moe_up_body, tf=tf),
        grid_spec=pltpu.PrefetchScalarGridSpec(
            num_scalar_prefetch=2,
            grid=(n_blk,),
            in_specs=[pl.BlockSpec((ROW_BLOCK, D), lambda i, be, nu: (i, 0)), wspec2, wspec],
            out_specs=pl.BlockSpec((ROW_BLOCK, D_EXPERT), lambda i, be, nu: (i, 0)),
        ),
        out_shape=jax.ShapeDtypeStruct((A_pad, D_EXPERT), bf16),
        compiler_params=pltpu.CompilerParams(dimension_semantics=("arbitrary",), vmem_limit_bytes=VMEM_LIMIT),
        name="moe_up",
    )(blk_expert, n_used, x_pad, wg, wu)
    return pl.pallas_call(
        _moe_down_body,
        grid_spec=pltpu.PrefetchScalarGridSpec(
            num_scalar_prefetch=2,
            grid=(n_blk,),
            in_specs=[pl.BlockSpec((ROW_BLOCK, D_EXPERT), lambda i, be, nu: (i, 0)),
                      pl.BlockSpec((1, 1, D_EXPERT, D), lambda i, be, nu: (layer, be[i], 0, 0)),
                      pl.BlockSpec((ROW_BLOCK, 1), lambda i, be, nu: (i, 0))],
            out_specs=pl.BlockSpec((ROW_BLOCK, D), lambda i, be, nu: (i, 0)),
        ),
        out_shape=jax.ShapeDtypeStruct((A_pad, D), f32),
        compiler_params=pltpu.CompilerParams(dimension_semantics=("arbitrary",), vmem_limit_bytes=VMEM_LIMIT),
        name="moe_down",
    )(blk_expert, n_used, h, wd, w_pad.reshape(A_pad, 1))


def hier_moe(xb, router_grp, router_grp_b, router_exp, router_exp_b, w_gate, w_up, w_down, layer):
    M, D = xb.shape
    expert_ids, weights = moe_router(xb, router_grp, router_grp_b, router_exp, router_exp_b)
    A = M * TOP_K
    e_flat = expert_ids.reshape(A)
    onehot = (e_flat[:, None] == jnp.arange(N_EXPERTS)[None, :]).astype(jnp.int32)
    rank = jnp.take_along_axis(jnp.cumsum(onehot, axis=0), e_flat[:, None], axis=1)[:, 0] - 1
    counts = jnp.sum(onehot, axis=0)
    padded = (counts + ROW_BLOCK - 1) // ROW_BLOCK * ROW_BLOCK
    end_pad = jnp.cumsum(padded)
    dest = (end_pad - padded)[e_flat] + rank
    n_blk = -(-A // ROW_BLOCK) + N_EXPERTS
    A_pad = n_blk * ROW_BLOCK
    tok_pad = (jnp.arange(A_pad, dtype=jnp.int32) % M).at[dest].set(jnp.arange(A, dtype=jnp.int32) // TOP_K)
    w_pad = jnp.zeros((A_pad,), f32).at[dest].set(weights.reshape(A))
    blk_expert = jnp.minimum(jnp.searchsorted(end_pad, jnp.arange(n_blk) * ROW_BLOCK, side='right'),
                             N_EXPERTS - 1).astype(jnp.int32)
    n_used = (end_pad[-1:] // ROW_BLOCK).astype(jnp.int32)
    x_pad = xb[tok_pad]
    y_pad = moe_experts(x_pad, w_pad, blk_expert, n_used, w_gate, w_up, w_down, layer)
    rows = dest.reshape(M, TOP_K)
    return y_pad[rows[:, 0]], y_pad[rows[:, 1]]


def _rope_tables(S, theta, rot_dims, width):
    half = rot_dims // 2
    freqs = theta ** (-np.arange(half, dtype=np.float64) / half)
    ang = np.arange(S, dtype=np.float64)[:, None] * freqs[None, :]
    cos = np.ones((S, width), np.float64)
    sin_lo = np.zeros((S, width), np.float64)
    sin_hi = np.zeros((S, width), np.float64)
    cos[:, :half] = np.cos(ang)
    cos[:, half:rot_dims] = np.cos(ang)
    sin_lo[:, :half] = -np.sin(ang)
    sin_hi[:, half:rot_dims] = np.sin(ang)
    return [jnp.asarray(t, f32) for t in (cos, sin_lo, sin_hi)]


def _rope128(x, cos, sin_lo, sin_hi, half):
    return x * cos + pltpu.roll(x, LANES - half, axis=1) * sin_lo + pltpu.roll(x, half, axis=1) * sin_hi


def _nsa_prep_body(q_ref, kv_ref, cos_ref, sl_ref, sh_ref, qo_ref, kvo_ref, co_ref):
    cos, sl, sh = cos_ref[...], sl_ref[...], sh_ref[...]
    half = ROPE_DIMS // 2
    scale = HEAD_DIM ** -0.5
    for h in range(NSA_HEADS):
        c = slice(h * LANES, (h + 1) * LANES)
        qo_ref[:, c] = (_rope128(q_ref[:, c], cos, sl, sh, half) * scale).astype(bf16)
    for j in range(6):
        for g in range(NSA_KV_HEADS):
            c = slice((j * NSA_KV_HEADS + g) * LANES, (j * NSA_KV_HEADS + g + 1) * LANES)
            x = kv_ref[:, c]
            if j % 2 == 0:
                x = _rope128(x, cos, sl, sh, half)
            if j < 2:
                co_ref[:, c] = x.astype(bf16)
            else:
                kvo_ref[:, slice(c.start - 4 * LANES, c.stop - 4 * LANES)] = x.astype(bf16)


def nsa_prep(proj, S, tm=512):
    M = proj.shape[0]
    assert M % tm == 0 and S % tm == 0
    tabs = _rope_tables(S, ROPE_THETA, ROPE_DIMS, LANES)
    tspec = pl.BlockSpec((tm, LANES), lambda i: (i % (S // tm), 0))
    return pl.pallas_call(
        _nsa_prep_body,
        grid=(M // tm,),
        in_specs=[pl.BlockSpec((tm, D_GRP), lambda i: (i, OFF_NSA_Q // D_GRP)),
                  pl.BlockSpec((tm, 6 * NSA_KV_W), lambda i: (i, OFF_NSA_KV // (6 * NSA_KV_W))),
                  tspec, tspec, tspec],
        out_specs=[pl.BlockSpec((tm, D_GRP), lambda i: (i, 0)),
                   pl.BlockSpec((tm, 4 * NSA_KV_W), lambda i: (i, 0)),
                   pl.BlockSpec((tm, 2 * NSA_KV_W), lambda i: (i, 0))],
        out_shape=[jax.ShapeDtypeStruct((M, D_GRP), bf16), jax.ShapeDtypeStruct((M, 4 * NSA_KV_W), bf16),
                   jax.ShapeDtypeStruct((M, 2 * NSA_KV_W), bf16)],
        compiler_params=pltpu.CompilerParams(dimension_semantics=("parallel",), vmem_limit_bytes=VMEM_LIMIT),
        name="nsa_prep",
    )(proj, proj, *tabs)


def _nsa_compress_body(r_ref, w1_ref, w2_ref, pe_ref, o_ref):
    r = r_ref[0, 0, 0]
    half_w = CMP_STRIDE * HEAD_DIM
    a = jnp.dot(r, w1_ref[0, :half_w, :], preferred_element_type=f32)
    b = jnp.dot(r, w1_ref[0, half_w:, :], preferred_element_type=f32)
    n = r.shape[0]
    pe = jnp.dot(jnp.broadcast_to(pe_ref[0], (SUB, pe_ref.shape[-1])), w1_ref[0], preferred_element_type=f32)[0:1]
    pre = a + pltpu.roll(b, n - 1, axis=0) + pe
    hid = pre * jax.nn.sigmoid(pre)
    o_ref[0, 0, 0] = jnp.dot(hid.astype(bf16), w2_ref[0], preferred_element_type=f32).astype(bf16)


def nsa_compress(c_rot, w1s, w2s, pes, B, S):
    G, Dh = NSA_KV_HEADS, HEAD_DIM
    nc = S // CMP_STRIDE
    r = c_rot.reshape(B, nc, CMP_STRIDE, 2, G, Dh).transpose(3, 0, 4, 1, 2, 5).reshape(2, B, G, nc, CMP_STRIDE * Dh)
    w1 = jnp.stack(w1s).astype(bf16)
    w2 = jnp.stack(w2s).astype(bf16)
    pe = jnp.stack(pes).reshape(2, 1, CMP_LEN * Dh).astype(bf16)
    return pl.pallas_call(
        _nsa_compress_body,
        grid=(2, B, G),
        in_specs=[pl.BlockSpec((1, 1, 1, nc, CMP_STRIDE * Dh), lambda j, b, g: (j, b, g, 0, 0)),
                  pl.BlockSpec((1, CMP_LEN * Dh, Dh), lambda j, b, g: (j, 0, 0)),
                  pl.BlockSpec((1, Dh, Dh), lambda j, b, g: (j, 0, 0)),
                  pl.BlockSpec((1, 1, CMP_LEN * Dh), lambda j, b, g: (j, 0, 0))],
        out_specs=pl.BlockSpec((1, 1, 1, nc, Dh), lambda j, b, g: (j, b, g, 0, 0)),
        out_shape=jax.ShapeDtypeStruct((2, B, G, nc, Dh), bf16),
        compiler_params=pltpu.CompilerParams(dimension_semantics=("parallel", "parallel", "parallel"),
                                             vmem_limit_bytes=VMEM_LIMIT),
        name="nsa_compress",
    )(r, w1, w2, pe)


def _softmax_rows(s, mask):
    s = jnp.where(mask, s, -jnp.inf)
    m = jnp.max(s, axis=-1, keepdims=True)
    m = jnp.where(m == -jnp.inf, 0.0, m)
    e = jnp.exp(s - m)
    return e, jnp.maximum(jnp.sum(e, axis=-1, keepdims=True), jnp.finfo(f32).tiny)


def _nsa_attn_body(q_ref, cmp_ref, ks_ref, vs_ref, kw_ref, vw_ref, gate_ref, ovt_ref, o_ref, *, tq, tk, S):
    g = pl.program_id(1)
    qi = pl.program_id(2)
    R, Dh = NSA_GROUP, HEAD_DIM
    rows = R * tq
    n_blk = S // SEL_LEN
    n_cmp = S // CMP_STRIDE - 1
    q0 = qi * tq
    q4 = jnp.concatenate([q_ref[0, :, r * Dh:(r + 1) * Dh] for r in range(R)], axis=0)
    nt = (((1,), (1,)), ((), ()))

    def qpos(shape):
        return q0 + lax.broadcasted_iota(jnp.int32, shape, 0) % tq

    k_cmp, v_cmp = cmp_ref[0, 0, 0], cmp_ref[1, 0, 0]
    ncp = k_cmp.shape[0]
    s = lax.dot_general(q4, k_cmp, nt, preferred_element_type=f32)
    n_idx = lax.broadcasted_iota(jnp.int32, (rows, ncp), 1)
    mask = (n_idx * CMP_STRIDE + (CMP_LEN - 1) <= qpos((rows, ncp))) & (n_idx < n_cmp)
    e, den = _softmax_rows(s, mask)
    p = e / den
    o_cmp = jnp.dot(p.astype(bf16), v_cmp, preferred_element_type=f32)

    psum = p[0:tq]
    for r in range(1, R):
        psum = psum + p[r * tq:(r + 1) * tq]
    ph, plo = _split2(psum)
    imp = (lax.dot_general(ovt_ref[...], ph, nt, preferred_element_type=f32)
           + lax.dot_general(ovt_ref[...], plo, nt, preferred_element_type=f32))
    blk = lax.broadcasted_iota(jnp.int32, (n_blk, tq), 0)
    cur = (q0 + lax.broadcasted_iota(jnp.int32, (n_blk, tq), 1)) // SEL_LEN
    visible = blk <= cur
    forced = (blk == 0) | (visible & (blk > cur - N_LOCAL_FORCED))
    imp = jnp.where(forced, FORCED_SCORE, jnp.where(visible, imp, -1.0))
    sel_t = jnp.zeros((n_blk, tq), f32)
    blk_f = blk.astype(f32)
    for _ in range(min(N_SEL, n_blk)):
        m = jnp.max(imp, axis=0, keepdims=True)
        first = jnp.min(jnp.where(imp == m, blk_f, float(n_blk)), axis=0, keepdims=True)
        hit = blk_f == first
        sel_t = jnp.where(hit, 1.0, sel_t)
        imp = jnp.where(hit, -jnp.inf, imp)
    sel_q = jnp.concatenate([sel_t.T, jnp.ones((tq, LANES - n_blk), f32)], axis=1).astype(bf16)

    blocks_per_tile = tk // SEL_LEN

    big = float(2 ** 30)
    kb = sel_q.shape[1]

    def sel_step(kt, carry, causal):
        m_run, l_run, acc = carry
        k0 = pl.multiple_of(kt * tk, tk)
        kt_k = ks_ref[0, pl.ds(k0, tk), :]
        kt_v = vs_ref[0, pl.ds(k0, tk), :]
        s = lax.dot_general(q4, kt_k, nt, preferred_element_type=f32)
        row = lax.broadcasted_iota(jnp.int32, (kb, tk), 0)
        in_blk = row == kt * blocks_per_tile + lax.broadcasted_iota(jnp.int32, (kb, tk), 1) // SEL_LEN
        expand = jnp.where(in_blk, big, jnp.where(row == n_blk, -big, 0.0)).astype(bf16)
        bias = jnp.dot(sel_q, expand, preferred_element_type=f32)
        s = (s.reshape(R, tq, tk) + bias[None]).reshape(rows, tk)
        if causal:
            kpos = k0 + lax.broadcasted_iota(jnp.int32, (rows, tk), 1)
            s = jnp.where(kpos <= qpos((rows, tk)), s, -big)
        m_new = jnp.maximum(m_run, jnp.max(s, axis=-1, keepdims=True))
        alpha = jnp.exp(m_run - m_new)
        e = jnp.exp(s - m_new)
        l_new = alpha * l_run + jnp.sum(e, axis=-1, keepdims=True)
        acc = alpha * acc + jnp.dot(e.astype(bf16), kt_v, preferred_element_type=f32)
        return m_new, l_new, acc

    last_kt = (q0 + tq - 1) // tk
    init = (jnp.full((rows, 1), -jnp.inf, f32), jnp.zeros((rows, 1), f32), jnp.zeros((rows, Dh), f32))
    carry = lax.fori_loop(0, last_kt, functools.partial(sel_step, causal=False), init)
    _, l_sel, acc_sel = sel_step(last_kt, carry, causal=True)
    o_sel = acc_sel / l_sel

    span = WINDOW + tq
    w0 = pl.multiple_of(jnp.maximum(q0 - WINDOW, 0), tq)
    kwin = kw_ref[0, pl.ds(w0, span), :]
    vwin = vw_ref[0, pl.ds(w0, span), :]
    s = lax.dot_general(q4, kwin, nt, preferred_element_type=f32)
    kpos = w0 + lax.broadcasted_iota(jnp.int32, (rows, span), 1)
    qp = qpos((rows, span))
    e, den = _softmax_rows(s, (kpos <= qp) & (kpos > qp - WINDOW))
    o_win = jnp.dot(e.astype(bf16), vwin, preferred_element_type=f32) / den

    gate = jax.nn.sigmoid(gate_ref[0])
    for r in range(R):
        c0 = (g * R + r) * 3
        lane = lax.broadcasted_iota(jnp.int32, (tq, LANES), 1)
        gsel = lambda c: jnp.sum(jnp.where(lane == c0 + c, gate, 0.0), axis=-1, keepdims=True)
        rs = slice(r * tq, (r + 1) * tq)
        o_ref[0, :, r * Dh:(r + 1) * Dh] = (gsel(0) * o_cmp[rs] + gsel(1) * o_sel[rs]
                                            + gsel(2) * o_win[rs]).astype(o_ref.dtype)


def nsa_attention(q_rot, kv_rot, cmp_kv, proj3, B, S, tq=256, tk=512):
    G, R, Dh = NSA_KV_HEADS, NSA_GROUP, HEAD_DIM
    n_blk = S // SEL_LEN
    assert S % tk == 0 and tk % tq == 0 and S >= WINDOW + tq and n_blk % SUB == 0 and n_blk < LANES
    nc = S // CMP_STRIDE
    c0 = np.arange(nc - 1)[None, :] * CMP_STRIDE
    s0 = np.arange(n_blk)[:, None] * SEL_LEN
    ovt = np.zeros((n_blk, nc), np.float32)
    ovt[:, :nc - 1] = np.clip(np.minimum(c0 + CMP_LEN, s0 + SEL_LEN) - np.maximum(c0, s0), 0, None) / CMP_LEN
    kvspec = lambda j: pl.BlockSpec((1, S, Dh), lambda b, g, i: (b, 0, j * G + g))
    return pl.pallas_call(
        functools.partial(_nsa_attn_body, tq=tq, tk=tk, S=S),
        grid=(B, G, S // tq),
        in_specs=[pl.BlockSpec((1, tq, R * Dh), lambda b, g, i: (b, i, g)),
                  pl.BlockSpec((2, 1, 1, nc, Dh), lambda b, g, i: (0, b, g, 0, 0)),
                  kvspec(0), kvspec(1), kvspec(2), kvspec(3),
                  pl.BlockSpec((1, tq, LANES), lambda b, g, i: (b, i, OFF_SMALL // LANES)),
                  pl.BlockSpec((n_blk, nc), lambda b, g, i: (0, 0))],
        out_specs=pl.BlockSpec((1, tq, R * Dh), lambda b, g, i: (b, i, g)),
        out_shape=jax.ShapeDtypeStruct((B, S, D_GRP), bf16),
        compiler_params=pltpu.CompilerParams(dimension_semantics=("parallel", "parallel", "arbitrary"),
                                             vmem_limit_bytes=VMEM_LIMIT),
        name="nsa_attention",
    )(q_rot, cmp_kv, kv_rot, kv_rot, kv_rot, kv_rot, proj3, jnp.asarray(ovt, bf16))


def nsa_mixer_pallas(proj, ck_w1, ck_w2, ck_pe, cv_w1, cv_w2, cv_pe, B, S):
    q_rot, kv_rot, c_rot = nsa_prep(proj, S)
    cmp_kv = nsa_compress(c_rot, (ck_w1, cv_w1), (ck_w2, cv_w2), (ck_pe, cv_pe), B, S)
    return nsa_attention(q_rot.reshape(B, S, D_GRP), kv_rot.reshape(B, S, 4 * NSA_KV_W), cmp_kv,
                         proj.reshape(B, S, P_PAD), B, S)


def _retention_body(q_ref, k_ref, v_ref, g_ref, cos_ref, sin_ref, dec_ref, xi_ref, zeta_ref, gc_ref, o_ref, st_ref,
                    *, T, HS):
    @pl.when(pl.program_id(2) == 0)
    def _():
        st_ref[...] = jnp.zeros_like(st_ref)

    C, Dh = RET_CHUNK, HEAD_DIM
    nt = (((1,), (1,)), ((), ()))
    pre = {}
    for hh in range(HS):
        hc = slice(hh * Dh, (hh + 1) * Dh)
        for c in range(T // C):
            rs = slice(c * C, (c + 1) * C)
            cos, sin = cos_ref[rs, :], sin_ref[rs, :]
            rope = lambda x: x * cos + pltpu.roll(x, Dh // 2, axis=1) * sin
            q = rope(q_ref[0, rs, hc])
            k = rope(k_ref[0, rs, hc]) * Dh ** -0.5
            qb, vb = q.astype(bf16), v_ref[0, rs, hc].astype(bf16)
            inner = lax.dot_general(qb, k.astype(bf16), nt, preferred_element_type=f32) * dec_ref[hh]
            o_in = jnp.dot(inner.astype(bf16), vb, preferred_element_type=f32)
            kv = jnp.dot((k * zeta_ref[hh]).astype(bf16).T, vb, preferred_element_type=f32)
            pre[hh, c] = (qb, o_in, kv)
    states = [st_ref[hh] for hh in range(HS)]
    for c in range(T // C):
        rs = slice(c * C, (c + 1) * C)
        for hh in range(HS):
            hc = slice(hh * Dh, (hh + 1) * Dh)
            qb, o_in, kv = pre[hh, c]
            o = o_in + jnp.dot(qb, states[hh].astype(bf16), preferred_element_type=f32) * xi_ref[hh]
            states[hh] = states[hh] * gc_ref[hh, 0:1, :] + kv
            mu = jnp.mean(o, axis=-1, keepdims=True)
            d = o - mu
            o = d * lax.rsqrt(jnp.mean(d * d, axis=-1, keepdims=True) + NORM_EPS)
            gt = g_ref[0, rs, hc]
            o_ref[0, rs, hc] = (gt * jax.nn.sigmoid(gt) * o).astype(o_ref.dtype)
    for hh in range(HS):
        st_ref[hh] = states[hh]


def retention_mixer_pallas(proj3, B, S, T=512, HS=2):
    H, Dh, C = RET_HEADS, HEAD_DIM, RET_CHUNK
    W = HS * Dh
    assert S % T == 0 and T % C == 0 and H % HS == 0
    freqs = RET_THETA ** (-np.linspace(0.0, 1.0, Dh // 2))
    ang = np.arange(S, dtype=np.float64)[:, None] * freqs[None, :]
    cos = jnp.asarray(np.concatenate([np.cos(ang), np.cos(ang)], 1), f32)
    sin = jnp.asarray(np.concatenate([-np.sin(ang), np.sin(ang)], 1), f32)
    log_gamma = np.log1p(-(2.0 ** (-5.0 - np.arange(H, dtype=np.float64))))
    n = np.arange(C, dtype=np.float64)
    diff = n[:, None] - n[None, :]
    decay_in = np.where(diff >= 0, np.exp(log_gamma[:, None, None] * np.maximum(diff, 0.0)), 0.0)
    xi = np.exp(log_gamma[:, None] * (n + 1.0))[:, :, None] * np.ones((1, 1, Dh))
    zeta = np.exp(log_gamma[:, None] * (C - 1.0 - n))[:, :, None] * np.ones((1, 1, Dh))
    gamma_c = np.exp(log_gamma * C)[:, None, None] * np.ones((1, SUB, Dh))
    blk = lambda j: pl.BlockSpec((1, T, W), lambda b, h, t: (b, t, (OFF_RET + j * D_GRP) // W + h))
    tab = pl.BlockSpec((T, Dh), lambda b, h, t: (t, 0))
    per_head = lambda r: pl.BlockSpec((HS, r, Dh), lambda b, h, t: (h, 0, 0))
    return pl.pallas_call(
        functools.partial(_retention_body, T=T, HS=HS),
        grid=(B, H // HS, S // T),
        in_specs=[blk(0), blk(1), blk(2), blk(3), tab, tab, per_head(C), per_head(C), per_head(C), per_head(SUB)],
        out_specs=pl.BlockSpec((1, T, W), lambda b, h, t: (b, t, h)),
        out_shape=jax.ShapeDtypeStruct((B, S, D_GRP), bf16),
        scratch_shapes=[pltpu.VMEM((HS, Dh, Dh), f32)],
        compiler_params=pltpu.CompilerParams(dimension_semantics=("parallel", "parallel", "arbitrary"),
                                             vmem_limit_bytes=VMEM_LIMIT),
        name="retention",
    )(proj3, proj3, proj3, proj3, cos, sin, jnp.asarray(decay_in, f32), jnp.asarray(xi, f32),
      jnp.asarray(zeta, f32), jnp.asarray(gamma_c, f32))


def _dot3(a, b):
    ah, al = _split2(a)
    bh, bl = _split2(b)
    return (jnp.dot(ah, bh, preferred_element_type=f32) + jnp.dot(ah, bl, preferred_element_type=f32)
            + jnp.dot(al, bh, preferred_element_type=f32))


def _deltanet_body(q_ref, k_ref, v_ref, z_ref, qp_ref, kp_ref, vp_ref, ab_ref, cw_ref, alog_ref, dtb_ref, ng_ref,
                   o_ref, st_ref, *, T, HS):
    hb = pl.program_id(1)
    first = pl.program_id(2) == 0

    @pl.when(first)
    def _():
        st_ref[...] = jnp.zeros_like(st_ref)

    C, Dh = DN_CHUNK, HEAD_DIM
    nt = (((1,), (1,)), ((), ()))
    n_chunks = T // C
    ab = ab_ref[0]
    lane = lax.broadcasted_iota(jnp.int32, ab.shape, 1)

    def conv_silu(x_ref, prev_ref, j, hc):
        prev = jnp.where(first, 0.0, prev_ref[0, :, hc])
        xcat = jnp.concatenate([prev, x_ref[0, :, hc]], axis=0)
        w = cw_ref[:, j, hc]
        y = sum(w[i:i + 1, :] * xcat[SUB - (CONV_WIDTH - 1) + i:SUB - (CONV_WIDTH - 1) + i + T, :]
                for i in range(CONV_WIDTH))
        return y * jax.nn.sigmoid(y)

    def l2n(x):
        return x * lax.rsqrt(jnp.sum(x * x, axis=-1, keepdims=True) + L2NORM_EPS)

    units = [(hh, slice(c * C, (c + 1) * C)) for hh in range(HS) for c in range(n_chunks)]
    qs, ks_, vs_, betas, g_rows = [], [], [], [], []
    for hh in range(HS):
        hc = slice(hh * Dh, (hh + 1) * Dh)
        head = hb * HS + hh
        q_all = l2n(conv_silu(q_ref, qp_ref, 0, hc)) * Dh ** -0.5
        k_all = l2n(conv_silu(k_ref, kp_ref, 1, hc))
        v_all = conv_silu(v_ref, vp_ref, 2, hc)
        a_col = jnp.sum(jnp.where(lane == SMALL_DN_A + head, ab, 0.0), axis=-1, keepdims=True)
        b_col = jnp.sum(jnp.where(lane == SMALL_DN_A + DN_HEADS + head, ab, 0.0), axis=-1, keepdims=True)
        beta_all = jax.nn.sigmoid(b_col)
        g_all = -jnp.exp(alog_ref[hh, 0:1, :]) * jax.nn.softplus(a_col + dtb_ref[hh, 0:1, :])
        for c in range(n_chunks):
            rs = slice(c * C, (c + 1) * C)
            qs.append(q_all[rs]); ks_.append(k_all[rs]); vs_.append(v_all[rs])
            betas.append(beta_all[rs]); g_rows.append(g_all[rs])

    ri = lax.broadcasted_iota(jnp.int32, (C, C), 0)
    ci = lax.broadcasted_iota(jnp.int32, (C, C), 1)
    tril, strict = ri >= ci, ri > ci
    ltri = tril.astype(bf16)
    eye = (ri == ci).astype(f32)
    gcs = []
    for g_rows_u in g_rows:
        gh, gm, gl = _split3(g_rows_u)
        gcs.append(jnp.dot(ltri, gh, preferred_element_type=f32) + jnp.dot(ltri, gm, preferred_element_type=f32)
                   + jnp.dot(ltri, gl, preferred_element_type=f32))
    decays = [jnp.exp(jnp.where(tril, gc[:, :C] - gc[:, :C].T, -jnp.inf)) for gc in gcs]
    kbs = [k.astype(bf16) for k in ks_]
    kbetas = [k * b for k, b in zip(ks_, betas)]
    ms = [jnp.where(strict, lax.dot_general(kbeta.astype(bf16), kb, nt, preferred_element_type=f32) * d, 0.0)
          for kbeta, kb, d in zip(kbetas, kbs, decays)]
    attns = [jnp.where(tril, lax.dot_general(q.astype(bf16), kb, nt, preferred_element_type=f32) * d, 0.0).astype(bf16)
             for q, kb, d in zip(qs, kbs, decays)]
    t_invs, pws = [eye - m for m in ms], ms
    for _ in range(5):
        pws = [_dot3(pw, pw) for pw in pws]
        t_invs = [t + _dot3(t, pw) for t, pw in zip(t_invs, pws)]
    egcs = [jnp.exp(gc) for gc in gcs]
    tbs = [t.astype(bf16) for t in t_invs]
    us = [jnp.dot(tb, (v * b).astype(bf16), preferred_element_type=f32) for tb, v, b in zip(tbs, vs_, betas)]
    ws = [jnp.dot(tb, (kbeta * egc).astype(bf16), preferred_element_type=f32).astype(bf16)
          for tb, kbeta, egc in zip(tbs, kbetas, egcs)]
    states = [st_ref[hh] for hh in range(HS)]
    for c in range(n_chunks):
        for hh in range(HS):
            u, rs = hh * n_chunks + c, slice(c * C, (c + 1) * C)
            g_last = gcs[u][C - 1:C, :]
            q_dec = (qs[u] * egcs[u]).astype(bf16)
            k_dec = (ks_[u] * jnp.exp(g_last - gcs[u])).astype(bf16)
            sb = states[hh].astype(bf16)
            v_new = us[u] - jnp.dot(ws[u], sb, preferred_element_type=f32)
            vnb = v_new.astype(bf16)
            o = jnp.dot(q_dec, sb, preferred_element_type=f32) + jnp.dot(attns[u], vnb, preferred_element_type=f32)
            states[hh] = states[hh] * jnp.exp(g_last) + jnp.dot(k_dec.T, vnb, preferred_element_type=f32)
            o = o * lax.rsqrt(jnp.mean(o * o, axis=-1, keepdims=True) + NORM_EPS) * ng_ref[...]
            zt = z_ref[0, rs, hh * Dh:(hh + 1) * Dh]
            o_ref[0, rs, hh * Dh:(hh + 1) * Dh] = (o * (zt * jax.nn.sigmoid(zt))).astype(o_ref.dtype)
    for hh in range(HS):
        st_ref[hh] = states[hh]


def deltanet_mixer_pallas(proj3, conv_w, a_log, dt_bias, norm_g, B, S, T=256, HS=4):
    H, Dh = DN_HEADS, HEAD_DIM
    W = HS * Dh
    assert S % T == 0 and T % DN_CHUNK == 0 and H % HS == 0
    blk = lambda j: pl.BlockSpec((1, T, W), lambda b, h, t: (b, t, (OFF_DN + j * D_GRP) // W + h))
    prev = lambda j: pl.BlockSpec((1, SUB, W), lambda b, h, t: (b, jnp.maximum(t * (T // SUB) - 1, 0),
                                                               (OFF_DN + j * D_GRP) // W + h))
    per_head = pl.BlockSpec((HS, SUB, Dh), lambda b, h, t: (h, 0, 0))
    bcast = lambda x: jnp.broadcast_to(x.astype(f32)[:, None, None], (H, SUB, Dh))
    return pl.pallas_call(
        functools.partial(_deltanet_body, T=T, HS=HS),
        grid=(B, H // HS, S // T),
        in_specs=[blk(0), blk(1), blk(2), blk(3), prev(0), prev(1), prev(2),
                  pl.BlockSpec((1, T, LANES), lambda b, h, t: (b, t, OFF_SMALL // LANES)),
                  pl.BlockSpec((CONV_WIDTH, 3, W), lambda b, h, t: (0, 0, h)),
                  per_head, per_head,
                  pl.BlockSpec((1, Dh), lambda b, h, t: (0, 0))],
        out_specs=pl.BlockSpec((1, T, W), lambda b, h, t: (b, t, h)),
        out_shape=jax.ShapeDtypeStruct((B, S, D_GRP), bf16),
        scratch_shapes=[pltpu.VMEM((HS, Dh, Dh), f32)],
        compiler_params=pltpu.CompilerParams(dimension_semantics=("parallel", "parallel", "arbitrary"),
                                             vmem_limit_bytes=VMEM_LIMIT),
        name="deltanet",
    )(proj3, proj3, proj3, proj3, proj3, proj3, proj3, proj3, conv_w.reshape(CONV_WIDTH, 3, D_GRP),
      bcast(a_log), bcast(dt_bias), norm_g.reshape(1, Dh))


def _w_in_pieces():
    n0, r0, d0, w0 = 0, NSA_COLS, NSA_COLS + RET_COLS, NSA_COLS + RET_COLS + DN_COLS
    n_gate = 3 * NSA_HEADS
    return [
        (OFF_RW, w0, 3 * D_GRP),
        (OFF_RET, r0, 4 * D_GRP),
        (OFF_DN, d0, 3 * D_GRP),
        (OFF_DN + 3 * D_GRP, d0 + 3 * D_GRP + 2 * DN_HEADS, D_GRP),
        (OFF_NSA_Q, n0, D_GRP + 6 * NSA_KV_W),
        (OFF_LORA, w0 + 3 * D_GRP, RW_LORA),
        (OFF_SMALL, n0 + D_GRP + 6 * NSA_KV_W, n_gate),
        (OFF_SMALL + SMALL_DN_A, d0 + 3 * D_GRP, 2 * DN_HEADS),
    ]


def _pack_w_in_body(w_ref, o_ref):
    cols = o_ref.shape[2]
    for off, width in ((OFF_LORA + RW_LORA, RW_LORA_PAD - RW_LORA), (OFF_SMALL, LANES)):
        o_ref[0, off:off + width, :] = jnp.zeros((width, cols), o_ref.dtype)
    for dst, src, width in _w_in_pieces():
        o_ref[0, dst:dst + width, :] = w_ref[0, src:src + width, :].astype(o_ref.dtype)


def pack_w_in_t(w_in, tc=256):
    L, K = w_in.shape[0], w_in.shape[1]
    w_t = jnp.swapaxes(w_in, 1, 2)
    return pl.pallas_call(
        _pack_w_in_body,
        grid=(L, K // tc),
        in_specs=[pl.BlockSpec((1, P_TOTAL, tc), lambda l, i: (l, 0, i))],
        out_specs=pl.BlockSpec((1, P_PAD, tc), lambda l, i: (l, 0, i)),
        out_shape=jax.ShapeDtypeStruct((L, P_PAD, K), bf16),
        compiler_params=pltpu.CompilerParams(dimension_semantics=("parallel", "parallel"),
                                             vmem_limit_bytes=VMEM_LIMIT),
        name="pack_w_in",
    )(w_t)


def hybrid_mixer(h, hn, B, S, layer, w_in_packed, w_out, ck_w1, ck_w2, ck_pe, cv_w1, cv_w2, cv_pe, dn_conv, dn_a_log,
                 dn_dt_bias, dn_norm, rw_mu, rw_w0, rw_w2, rw_a0, rw_a2, rw_g2, rw_kk, rw_ka, rw_rk, rw_ln_g, rw_ln_b):
    M = B * S
    proj = matmul_nt(hn, w_in_packed, layer, name="in_proj")
    p3 = proj.reshape(B, S, P_PAD)
    o_nsa = nsa_mixer_pallas(proj, ck_w1, ck_w2, ck_pe, cv_w1, cv_w2, cv_pe, B, S)
    o_ret = retention_mixer_pallas(p3, B, S)
    o_dn = deltanet_mixer_pallas(p3, dn_conv, dn_a_log, dn_dt_bias, dn_norm, B, S)
    o_rw = rwkv7_mixer(proj, OFF_RW // (3 * D_GRP), OFF_LORA // RW_LORA_PAD, rw_mu, rw_w0, rw_w2, rw_a0, rw_a2,
                       rw_g2, rw_kk, rw_ka, rw_rk, rw_ln_g, rw_ln_b, B, S)
    parts = [o.reshape(M, D_GRP) for o in (o_nsa, o_ret, o_dn, o_rw)]
    return out_proj_residual(parts, w_out, layer, h)


def kernel(x, p, norm_mix, w_in, w_out, nsa_ck_w1, nsa_ck_w2, nsa_ck_pe, nsa_cv_w1, nsa_cv_w2, nsa_cv_pe, dn_conv, dn_a_log, dn_dt_bias, dn_norm, rw_mu, rw_w0, rw_w2, rw_a0, rw_a2, rw_g2, rw_kk, rw_ka, rw_rk, rw_ln_g, rw_ln_b, norm_moe, moe_router_grp, moe_router_grp_b, moe_router_exp, moe_router_exp_b, moe_w_gate, moe_w_up, moe_w_down, norm_ple, ple_w, ple_gate, norm_final):
    B, S, D = x.shape
    M = B * S
    h = x.reshape(M, D)
    w_in_packed = pack_w_in_t(w_in)
    for i in range(DEPTH):
        hn = rms_norm_rows(h, norm_mix[i], bf16)
        h = hybrid_mixer(h, hn, B, S, i, w_in_packed, w_out, nsa_ck_w1[i], nsa_ck_w2[i], nsa_ck_pe[i], nsa_cv_w1[i],
                         nsa_cv_w2[i], nsa_cv_pe[i], dn_conv[i], dn_a_log[i], dn_dt_bias[i], dn_norm[i],
                         rw_mu[i], rw_w0[i], rw_w2[i], rw_a0[i], rw_a2[i], rw_g2[i], rw_kk[i], rw_ka[i],
                         rw_rk[i], rw_ln_g[i], rw_ln_b[i])
        hn = rms_norm_rows(h, norm_moe[i], bf16)
        ya, yb = hier_moe(hn, moe_router_grp[i], moe_router_grp_b[i], moe_router_exp[i], moe_router_exp_b[i],
                          moe_w_gate, moe_w_up, moe_w_down, i)
        h, hn = add_rms_norm_rows(h, ya, yb, norm_ple[i], bf16)
        h = ple_residual(hn, ple_gate[i].astype(bf16), p[i].reshape(M, PLE_DIM), ple_w[i].astype(bf16), h)
    return rms_norm_rows(h, norm_final, f32).reshape(B, S, D)
```

```python
import functools

import jax
import jax.numpy as jnp
import numpy as np
from jax import lax
from jax.experimental import pallas as pl
from jax.experimental.pallas import tpu as pltpu

D_MODEL = 4096
DEPTH = 2
f32 = jnp.float32
bf16 = jnp.bfloat16
D_MIX = D_MODEL
D_GRP = D_MIX // 4
HEAD_DIM = 128
NORM_EPS = 1e-6
L2NORM_EPS = 1e-6
NSA_HEADS = D_GRP // HEAD_DIM
NSA_KV_HEADS = 2
NSA_GROUP = NSA_HEADS // NSA_KV_HEADS
NSA_KV_W = NSA_KV_HEADS * HEAD_DIM
CMP_LEN = 32
CMP_STRIDE = 16
SEL_LEN = 64
N_SEL = 16
N_LOCAL_FORCED = 2
FORCED_SCORE = 1e4
WINDOW = 512
ROPE_THETA = 500000.0
ROPE_DIMS = HEAD_DIM // 4
RET_HEADS = D_GRP // HEAD_DIM
RET_CHUNK = 128
RET_THETA = 10000.0
DN_HEADS = D_GRP // HEAD_DIM
DN_CHUNK = 64
CONV_WIDTH = 4
RW_HEAD_DIM = 64
RW_DECAY_LORA = 64
RW_AAA_LORA = 64
RW_GATE_LORA = 160
RW_LN_EPS = 64e-5
N_GROUPS = 4
EXPERTS_PER_GROUP = 8
N_EXPERTS = N_GROUPS * EXPERTS_PER_GROUP
TOP_K = 2
D_EXPERT = 768
ROW_BLOCK = 256
PLE_DIM = 256
NSA_COLS = D_GRP + 6 * NSA_KV_W + 3 * NSA_HEADS
RET_COLS = 4 * D_GRP
DN_COLS = 4 * D_GRP + 2 * DN_HEADS
RW_COLS = 3 * D_GRP + RW_DECAY_LORA + RW_AAA_LORA + RW_GATE_LORA
P_TOTAL = NSA_COLS + RET_COLS + DN_COLS + RW_COLS

RW_LORA = RW_DECAY_LORA + RW_AAA_LORA + RW_GATE_LORA
RW_LORA_PAD = 384
RW_GROUPS_PER_BODY = 4
LANES = 128
SUB = 8

OFF_RW = 0
OFF_RET = 3 * D_GRP
OFF_DN = OFF_RET + 4 * D_GRP
OFF_NSA_Q = OFF_DN + 4 * D_GRP
OFF_NSA_KV = OFF_NSA_Q + D_GRP
OFF_LORA = OFF_NSA_KV + 6 * NSA_KV_W
OFF_SMALL = OFF_LORA + RW_LORA_PAD
SMALL_DN_A = 32
P_PAD = OFF_SMALL + LANES
assert P_PAD == 14336 and OFF_LORA % RW_LORA_PAD == 0

V7X_VMEM_BYTES = 64 * 1024 * 1024
VMEM_LIMIT = V7X_VMEM_BYTES * 7 // 8


def _split2(x):
    h = x.astype(bf16)
    return h, (x - h.astype(f32)).astype(bf16)


def _split3(x):
    h = x.astype(bf16)
    r1 = x - h.astype(f32)
    m = r1.astype(bf16)
    return h, m, (r1 - m.astype(f32)).astype(bf16)


def _rwkv_prep_body(x_ref, xprev_ref, l_ref, lprev_ref, mu_ref, mul_ref, w0_ref, a0_ref, kkw_ref, ka_ref,
                    w2_ref, a2_ref, g2_ref, e_ref, r_ref, w_ref, k_ref, v_ref, kk_ref, b_ref, g_ref, *, tm, S):
    i = pl.program_id(0)
    first = (i * tm) % S == 0

    def token_mix(x, prev8, mu):
        rows = lax.broadcasted_iota(jnp.int32, x.shape, 0)
        prev_row = jnp.where(first, 0.0, prev8[SUB - 1:SUB, :])
        x_prev = jnp.where(rows == 0, prev_row, pltpu.roll(x, 1, axis=0))
        return x + (x_prev - x) * mu

    x = token_mix(x_ref[...], xprev_ref[...], mu_ref[...])
    lora = token_mix(l_ref[...], lprev_ref[...], mul_ref[...])
    r = x[:, 0:D_GRP]
    k = x[:, D_GRP:2 * D_GRP]
    v = x[:, 2 * D_GRP:3 * D_GRP]
    dw = jnp.dot(jnp.tanh(lora).astype(bf16), w2_ref[...], preferred_element_type=f32)
    da = jnp.dot(lora.astype(bf16), a2_ref[...], preferred_element_type=f32)
    g = jnp.dot(jax.nn.sigmoid(lora).astype(bf16), g2_ref[...], preferred_element_type=f32)
    w = -jax.nn.softplus(-(w0_ref[...] + dw)) - 0.5
    decay = jnp.exp(-jnp.exp(w))
    a = jax.nn.sigmoid(a0_ref[...] + da)
    kq = k * kkw_ref[...]
    sq_h, sq_l = _split2(kq * kq)
    ss = (jnp.dot(sq_h, e_ref[...], preferred_element_type=f32)
          + jnp.dot(sq_l, e_ref[...], preferred_element_type=f32))
    kk = kq * lax.rsqrt(ss + L2NORM_EPS)
    r_ref[...] = r
    w_ref[...] = decay
    k_ref[...] = k * (1.0 + (a - 1.0) * ka_ref[...])
    v_ref[...] = v
    kk_ref[...] = kk
    b_ref[...] = kk * a
    g_ref[...] = g


def rwkv_prep(proj, rkv_blk, lora_blk, mu, w0, w2, a0, a2, g2, k_k, k_a, S, tm=256):
    M = proj.shape[0]
    assert M % tm == 0 and S % tm == 0
    W3 = 3 * D_GRP
    mu_x = mu[:W3].reshape(1, W3)
    mu_l = jnp.pad(mu[W3:], (0, RW_LORA_PAD - RW_LORA)).reshape(1, RW_LORA_PAD)
    w2p = jnp.zeros((RW_LORA_PAD, D_GRP), f32).at[0:RW_DECAY_LORA].set(w2).astype(bf16)
    a2p = jnp.zeros((RW_LORA_PAD, D_GRP), f32).at[RW_DECAY_LORA:RW_DECAY_LORA + RW_AAA_LORA].set(a2).astype(bf16)
    g2p = jnp.zeros((RW_LORA_PAD, D_GRP), f32).at[RW_DECAY_LORA + RW_AAA_LORA:RW_LORA].set(g2).astype(bf16)
    hid = np.arange(D_GRP) // RW_HEAD_DIM
    e = jnp.asarray(hid[:, None] == hid[None, :], bf16)
    row = lambda t: t.reshape(1, D_GRP)
    vec = pl.BlockSpec((1, D_GRP), lambda i: (0, 0))
    wspec = pl.BlockSpec((RW_LORA_PAD, D_GRP), lambda i: (0, 0))
    ospec = pl.BlockSpec((tm, D_GRP), lambda i: (i, 0))
    prev_row_blk = lambda i: jnp.maximum(i * (tm // SUB) - 1, 0)
    return pl.pallas_call(
        functools.partial(_rwkv_prep_body, tm=tm, S=S),
        grid=(M // tm,),
        in_specs=[pl.BlockSpec((tm, W3), lambda i: (i, rkv_blk)),
                  pl.BlockSpec((SUB, W3), lambda i: (prev_row_blk(i), rkv_blk)),
                  pl.BlockSpec((tm, RW_LORA_PAD), lambda i: (i, lora_blk)),
                  pl.BlockSpec((SUB, RW_LORA_PAD), lambda i: (prev_row_blk(i), lora_blk)),
                  pl.BlockSpec((1, W3), lambda i: (0, 0)),
                  pl.BlockSpec((1, RW_LORA_PAD), lambda i: (0, 0)),
                  vec, vec, vec, vec, wspec, wspec, wspec,
                  pl.BlockSpec((D_GRP, D_GRP), lambda i: (0, 0))],
        out_specs=[ospec] * 7,
        out_shape=[jax.ShapeDtypeStruct((M, D_GRP), f32)] * 7,
        compiler_params=pltpu.CompilerParams(dimension_semantics=("parallel",), vmem_limit_bytes=VMEM_LIMIT),
        name="rwkv_prep",
    )(proj, proj, proj, proj, mu_x, mu_l, row(w0), row(a0), row(k_k), row(k_a), w2p, a2p, g2p, e)


def _rwkv_scan_body(r_ref, w_ref, k_ref, v_ref, kk_ref, b_ref, g_ref, lng_ref, lnb_ref, rk_ref, o_ref, s_ref,
                    vc_ref, y_ref, *, G, T):
    tb = pl.program_id(2)

    @pl.when(tb == 0)
    def _():
        s_ref[...] = jnp.zeros_like(s_ref)

    Dh = RW_HEAD_DIM
    lo = lax.broadcasted_iota(jnp.int32, (Dh, LANES), 1) < Dh
    lo8 = lax.broadcasted_iota(jnp.int32, (SUB, LANES), 1) < Dh
    sub_iota = lax.broadcasted_iota(jnp.int32, (4 * SUB, LANES), 0)
    row_iota = lax.broadcasted_iota(jnp.int32, (SUB, LANES), 0)

    def seg_sum(x, m):
        s_lo = jnp.sum(jnp.where(m, x, 0.0), axis=-1, keepdims=True)
        s_hi = jnp.sum(jnp.where(m, 0.0, x), axis=-1, keepdims=True)
        return jnp.where(m, s_lo, s_hi)

    cols = [slice(p * LANES, (p + 1) * LANES) for p in range(G)]
    n_groups = T // SUB

    k_iota = lax.broadcasted_iota(jnp.int32, (8 * SUB, LANES), 0)
    k_head = k_iota // (4 * SUB) == lax.broadcasted_iota(jnp.int32, (8 * SUB, LANES), 1) // Dh
    k_piece = (k_iota // SUB) % 4

    def v_pieces(gi):
        t0 = pl.multiple_of(gi * SUB, SUB)
        vts = []
        for p in range(G):
            vh, vm, vl = _split3(v_ref[0, pl.ds(t0, SUB), cols[p]])
            vp = jnp.concatenate([vh, vm, vl, jnp.zeros_like(vh)], axis=0)
            vts.append(jnp.concatenate([vp[:, :Dh], vp[:, Dh:]], axis=0).T)
        return vts

    def v_columns(vts, i, slot):
        sel = (k_head & (k_piece < 3) & (k_iota % SUB == i)).astype(bf16)
        for p in range(G):
            vc_ref[slot, i, p] = jnp.dot(vts[p], sel, preferred_element_type=f32)

    vts0 = v_pieces(0)
    for i in range(SUB):
        v_columns(vts0, i, 0)

    def run_group(gi, slot, states):
        t0 = pl.multiple_of(gi * SUB, SUB)
        vts_next = v_pieces(jnp.minimum(gi + 1, n_groups - 1))
        r8b = [r_ref[0, pl.ds(t0, SUB), cols[p]].astype(bf16) for p in range(G)]
        yacc = [jnp.zeros((SUB, LANES), f32) for _ in range(G)]
        tiles = [[ref[0, pl.ds(t0, SUB), cols[p]] for ref in (kk_ref, w_ref, b_ref, k_ref)] for p in range(G)]
        for i in range(SUB):
            for p in range(G):
                kk_r, w_r, b_r, k_r = [jnp.broadcast_to(t8[i:i + 1, :], (Dh, LANES)) for t8 in tiles[p]]
                S = states[p]
                sab = -seg_sum(S * kk_r, lo)
                states[p] = S * w_r + sab * b_r + vc_ref[slot, i, p] * k_r
            for p in range(G):
                Sb = states[p].astype(bf16)
                zero = jnp.zeros_like(Sb)
                mt = jnp.concatenate([jnp.where(lo, Sb, zero), jnp.where(lo, zero, Sb)], axis=0)
                y8 = lax.dot_general(r8b[p], mt, (((1,), (1,)), ((), ())), preferred_element_type=f32)
                yacc[p] = jnp.where(row_iota == i, y8, yacc[p])
            v_columns(vts_next, i, 1 - slot)
        for p in range(G):
            y_ref[pl.ds(t0, SUB), cols[p]] = yacc[p]
        return states

    def group_batch(gj, carry):
        states = [s_ref[p] for p in range(G)]
        for u in range(RW_GROUPS_PER_BODY):
            states = run_group(RW_GROUPS_PER_BODY * gj + u, u % 2, states)
        for p in range(G):
            s_ref[p] = states[p]
        return carry

    assert n_groups % RW_GROUPS_PER_BODY == 0 and RW_GROUPS_PER_BODY % 2 == 0
    lax.fori_loop(0, n_groups // RW_GROUPS_PER_BODY, group_batch, 0)

    lo_t = lax.broadcasted_iota(jnp.int32, (T, LANES), 1) < Dh
    for p in range(G):
        y = y_ref[:, cols[p]]
        mean = seg_sum(y, lo_t) * (1.0 / Dh)
        d = y - mean
        var = seg_sum(d * d, lo_t) * (1.0 / Dh)
        yn = d * lax.rsqrt(var + RW_LN_EPS) * lng_ref[:, cols[p]] + lnb_ref[:, cols[p]]
        bonus = seg_sum(r_ref[0, :, cols[p]] * k_ref[0, :, cols[p]] * rk_ref[:, cols[p]], lo_t) * v_ref[0, :, cols[p]]
        o_ref[0, :, cols[p]] = (yn + bonus) * g_ref[0, :, cols[p]]


def rwkv_scan(r, w, k, v, kk, b, g, ln_g, ln_b, r_k, G=8, T=512):
    B, S, D = r.shape
    n_pairs = D // LANES
    assert n_pairs % G == 0 and S % T == 0
    spec = pl.BlockSpec((1, T, G * LANES), lambda bi, pi, ti: (bi, ti, pi))
    vec = pl.BlockSpec((1, G * LANES), lambda bi, pi, ti: (0, pi))
    row = lambda t: t.reshape(1, D)
    return pl.pallas_call(
        functools.partial(_rwkv_scan_body, G=G, T=T),
        grid=(B, n_pairs // G, S // T),
        in_specs=[spec] * 7 + [vec] * 3,
        out_specs=spec,
        out_shape=jax.ShapeDtypeStruct((B, S, D), f32),
        scratch_shapes=[pltpu.VMEM((G, RW_HEAD_DIM, LANES), f32),
                        pltpu.VMEM((2, SUB, G, RW_HEAD_DIM, LANES), f32),
                        pltpu.VMEM((T, G * LANES), f32)],
        compiler_params=pltpu.CompilerParams(dimension_semantics=("parallel", "parallel", "arbitrary"),
                                             vmem_limit_bytes=VMEM_LIMIT),
        name="rwkv_scan",
    )(r, w, k, v, kk, b, g, row(ln_g), row(ln_b), row(r_k))


def rwkv7_mixer(proj, rkv_blk, lora_blk, mu, w0, w2, a0, a2, g2, k_k, k_a, r_k, ln_g, ln_b, B, S):
    outs = rwkv_prep(proj, rkv_blk, lora_blk, mu, w0, w2, a0, a2, g2, k_k, k_a, S)
    r, w, k, v, kk, b, g = [t.reshape(B, S, D_GRP) for t in outs]
    return rwkv_scan(r, w, k, v, kk, b, g, ln_g, ln_b, r_k.reshape(-1))


def _mm_nt_body(a_ref, bt_ref, o_ref, acc_ref):
    k = pl.program_id(2)

    @pl.when(k == 0)
    def _():
        acc_ref[...] = jnp.zeros_like(acc_ref)

    acc_ref[...] += lax.dot_general(a_ref[...], bt_ref[...], (((1,), (1,)), ((), ())), preferred_element_type=f32)

    @pl.when(k == pl.num_programs(2) - 1)
    def _():
        o_ref[...] = acc_ref[...].astype(o_ref.dtype)


def matmul_nt(a, bt, layer, out_dtype=f32, tm=1024, tn=1024, tk=2048, name="matmul_nt"):
    M, K = a.shape
    _, N, _ = bt.shape
    assert M % tm == 0 and N % tn == 0 and K % tk == 0, (M, N, K, tm, tn, tk)
    return pl.pallas_call(
        _mm_nt_body,
        grid=(M // tm, N // tn, K // tk),
        in_specs=[pl.BlockSpec((tm, tk), lambda i, j, k: (i, k)),
                  pl.BlockSpec((None, tn, tk), lambda i, j, k: (layer, j, k))],
        out_specs=pl.BlockSpec((tm, tn), lambda i, j, k: (i, j)),
        out_shape=jax.ShapeDtypeStruct((M, N), out_dtype),
        scratch_shapes=[pltpu.VMEM((tm, tn), f32)],
        compiler_params=pltpu.CompilerParams(
            dimension_semantics=("parallel", "parallel", "arbitrary"), vmem_limit_bytes=VMEM_LIMIT),
        name=name,
    )(a, bt)


def _rms_norm_body(x_ref, g_ref, o_ref):
    x = x_ref[...]
    y = x * lax.rsqrt(jnp.mean(x * x, axis=-1, keepdims=True) + NORM_EPS)
    o_ref[...] = (y * g_ref[...]).astype(o_ref.dtype)


def rms_norm_rows(x, g, out_dtype, tm=512):
    M, D = x.shape
    return pl.pallas_call(
        _rms_norm_body,
        grid=(M // tm,),
        in_specs=[pl.BlockSpec((tm, D), lambda i: (i, 0)), pl.BlockSpec((1, D), lambda i: (0, 0))],
        out_specs=pl.BlockSpec((tm, D), lambda i: (i, 0)),
        out_shape=jax.ShapeDtypeStruct((M, D), out_dtype),
        compiler_params=pltpu.CompilerParams(dimension_semantics=("parallel",), vmem_limit_bytes=VMEM_LIMIT),
        name="rms_norm",
    )(x, g.reshape(1, D).astype(f32))


def _add_rms_norm_body(h_ref, ya_ref, yb_ref, g_ref, ho_ref, o_ref):
    x = h_ref[...] + (ya_ref[...] + yb_ref[...])
    ho_ref[...] = x
    y = x * lax.rsqrt(jnp.mean(x * x, axis=-1, keepdims=True) + NORM_EPS)
    o_ref[...] = (y * g_ref[...]).astype(o_ref.dtype)


def add_rms_norm_rows(h, ya, yb, g, out_dtype, tm=256):
    M, D = h.shape
    row = pl.BlockSpec((tm, D), lambda i: (i, 0))
    return pl.pallas_call(
        _add_rms_norm_body,
        grid=(M // tm,),
        in_specs=[row, row, row, pl.BlockSpec((1, D), lambda i: (0, 0))],
        out_specs=[row, row],
        out_shape=[jax.ShapeDtypeStruct((M, D), f32), jax.ShapeDtypeStruct((M, D), out_dtype)],
        compiler_params=pltpu.CompilerParams(dimension_semantics=("parallel",), vmem_limit_bytes=VMEM_LIMIT),
        name="add_rms_norm",
    )(h, ya, yb, g.reshape(1, D).astype(f32))


def _out_proj_body(a0_ref, a1_ref, a2_ref, a3_ref, b_ref, h_ref, o_ref, acc_ref):
    k = pl.program_id(2)

    @pl.when(k == 0)
    def _():
        acc_ref[...] = h_ref[...]

    for j, a_ref in enumerate((a0_ref, a1_ref, a2_ref, a3_ref)):
        @pl.when(k == j)
        def _(a_ref=a_ref):
            acc_ref[...] += jnp.dot(a_ref[...].astype(bf16), b_ref[...], preferred_element_type=f32)

    @pl.when(k == pl.num_programs(2) - 1)
    def _():
        o_ref[...] = acc_ref[...]


def out_proj_residual(parts, w, h, tm=1024, tn=1024):
    M, D = h.shape
    aspec = pl.BlockSpec((tm, D_GRP), lambda i, j, k: (i, 0))
    return pl.pallas_call(
        _out_proj_body,
        grid=(M // tm, D // tn, len(parts)),
        in_specs=[aspec] * 4 + [pl.BlockSpec((D_GRP, tn), lambda i, j, k: (k, j)),
                                pl.BlockSpec((tm, tn), lambda i, j, k: (i, j))],
        out_specs=pl.BlockSpec((tm, tn), lambda i, j, k: (i, j)),
        out_shape=jax.ShapeDtypeStruct((M, D), f32),
        scratch_shapes=[pltpu.VMEM((tm, tn), f32)],
        compiler_params=pltpu.CompilerParams(
            dimension_semantics=("parallel", "parallel", "arbitrary"), vmem_limit_bytes=VMEM_LIMIT),
        name="out_proj",
    )(*parts, w, h)


def _ple_body(a_ref, b_ref, p_ref, pw_ref, h_ref, o_ref, acc_ref):
    k = pl.program_id(2)

    @pl.when(k == 0)
    def _():
        acc_ref[...] = jnp.zeros_like(acc_ref)

    acc_ref[...] += jnp.dot(a_ref[...], b_ref[...], preferred_element_type=f32)

    @pl.when(k == pl.num_programs(2) - 1)
    def _():
        emb = jnp.dot(p_ref[...].astype(bf16), pw_ref[...], preferred_element_type=f32)
        o_ref[...] = h_ref[...] + emb * jax.nn.sigmoid(acc_ref[...])


def ple_residual(hn, gate_w, p, ple_w, h, tm=1024, tn=1024, tk=2048):
    M, D = h.shape
    return pl.pallas_call(
        _ple_body,
        grid=(M // tm, D // tn, D // tk),
        in_specs=[pl.BlockSpec((tm, tk), lambda i, j, k: (i, k)),
                  pl.BlockSpec((tk, tn), lambda i, j, k: (k, j)),
                  pl.BlockSpec((tm, PLE_DIM), lambda i, j, k: (i, 0)),
                  pl.BlockSpec((PLE_DIM, tn), lambda i, j, k: (0, j)),
                  pl.BlockSpec((tm, tn), lambda i, j, k: (i, j))],
        out_specs=pl.BlockSpec((tm, tn), lambda i, j, k: (i, j)),
        out_shape=jax.ShapeDtypeStruct((M, D), f32),
        scratch_shapes=[pltpu.VMEM((tm, tn), f32)],
        compiler_params=pltpu.CompilerParams(
            dimension_semantics=("parallel", "parallel", "arbitrary"), vmem_limit_bytes=VMEM_LIMIT),
        name="ple",
    )(hn, gate_w, p, ple_w, h)


def _moe_router_body(x_ref, w_ref, b_ref, id_ref, wt_ref):
    logits = jnp.dot(x_ref[...].astype(bf16), w_ref[...], preferred_element_type=f32) + b_ref[...]
    lane = lax.broadcasted_iota(jnp.int32, logits.shape, 1).astype(f32)
    ninf = -jnp.inf

    def top1(vals):
        m = jnp.max(vals, axis=-1, keepdims=True)
        idx = jnp.min(jnp.where(vals == m, lane, float(LANES)), axis=-1, keepdims=True)
        return m, idx

    gl = jnp.where(lane < N_GROUPS, logits, ninf)
    g_max, g_idx = top1(gl)
    g_w = 1.0 / jnp.sum(jnp.exp(gl - g_max), axis=-1, keepdims=True)
    lo = N_GROUPS + g_idx * EXPERTS_PER_GROUP
    el = jnp.where((lane >= lo) & (lane < lo + EXPERTS_PER_GROUP), logits, ninf)
    e_max, _ = top1(el)
    pe = jnp.exp(el - e_max)
    p = pe / jnp.sum(pe, axis=-1, keepdims=True)
    p = jnp.where(el == ninf, ninf, p)
    p1, i1 = top1(p)
    p2, i2 = top1(jnp.where(lane == i1, ninf, p))
    scale = g_w / (p1 + p2)
    id_ref[...] = jnp.where(lane == 0, i1 - N_GROUPS, jnp.where(lane == 1, i2 - N_GROUPS, 0.0)).astype(jnp.int32)
    wt_ref[...] = jnp.where(lane == 0, p1 * scale, jnp.where(lane == 1, p2 * scale, 0.0))


def moe_router(xb, router_grp, router_grp_b, router_exp, router_exp_b, tm=512):
    M, D = xb.shape
    n_log = N_GROUPS + N_EXPERTS
    w = jnp.pad(jnp.concatenate([router_grp, router_exp], axis=1), ((0, 0), (0, LANES - n_log))).astype(bf16)
    b = jnp.pad(jnp.concatenate([router_grp_b, router_exp_b]), (0, LANES - n_log)).reshape(1, LANES).astype(f32)
    ids, wts = pl.pallas_call(
        _moe_router_body,
        grid=(M // tm,),
        in_specs=[pl.BlockSpec((tm, D), lambda i: (i, 0)), pl.BlockSpec((D, LANES), lambda i: (0, 0)),
                  pl.BlockSpec((1, LANES), lambda i: (0, 0))],
        out_specs=[pl.BlockSpec((tm, LANES), lambda i: (i, 0))] * 2,
        out_shape=[jax.ShapeDtypeStruct((M, LANES), jnp.int32), jax.ShapeDtypeStruct((M, LANES), f32)],
        compiler_params=pltpu.CompilerParams(dimension_semantics=("parallel",), vmem_limit_bytes=VMEM_LIMIT),
        name="moe_router",
    )(xb, w, b)
    return ids[:, :TOP_K], wts[:, :TOP_K]


def _moe_up_body(be_ref, nu_ref, x_ref, wg_ref, wu_ref, h_ref, *, tf):
    i = pl.program_id(0)

    @pl.when(i < nu_ref[0])
    def _():
        x = x_ref[...].astype(bf16)
        for f in range(D_EXPERT // tf):
            c = slice(f * tf, (f + 1) * tf)
            hg = jnp.dot(x, wg_ref[0, 0, :, c].astype(bf16), preferred_element_type=f32)
            hu = jnp.dot(x, wu_ref[0, 0, :, c].astype(bf16), preferred_element_type=f32)
            h_ref[:, c] = (hg * jax.nn.sigmoid(hg) * hu).astype(bf16)

    @pl.when(i >= nu_ref[0])
    def _():
        h_ref[...] = jnp.zeros_like(h_ref)


def _moe_down_body(be_ref, nu_ref, h_ref, wd_ref, wt_ref, y_ref):
    i = pl.program_id(0)

    @pl.when(i < nu_ref[0])
    def _():
        y_ref[...] = jnp.dot(h_ref[...], wd_ref[0, 0].astype(bf16), preferred_element_type=f32) * wt_ref[...]

    @pl.when(i >= nu_ref[0])
    def _():
        y_ref[...] = jnp.zeros_like(y_ref)


def moe_experts(x_pad, w_pad, blk_expert, n_used, wg, wu, wd, layer, tf=256):
    A_pad, D = x_pad.shape
    n_blk = A_pad // ROW_BLOCK
    wmap = lambda i, be, nu: (layer, be[i], 0, 0)
    wspec = pl.BlockSpec((1, 1, D, D_EXPERT), wmap, pipeline_mode=pl.Buffered(1))
    wspec2 = pl.BlockSpec((1, 1, D, D_EXPERT), wmap)
    h = pl.pallas_call(
        functools.partial(_moe_up_body, tf=tf),
        grid_spec=pltpu.PrefetchScalarGridSpec(
            num_scalar_prefetch=2,
            grid=(n_blk,),
            in_specs=[pl.BlockSpec((ROW_BLOCK, D), lambda i, be, nu: (i, 0)), wspec2, wspec],
            out_specs=pl.BlockSpec((ROW_BLOCK, D_EXPERT), lambda i, be, nu: (i, 0)),
        ),
        out_shape=jax.ShapeDtypeStruct((A_pad, D_EXPERT), bf16),
        compiler_params=pltpu.CompilerParams(dimension_semantics=("arbitrary",), vmem_limit_bytes=VMEM_LIMIT),
        name="moe_up",
    )(blk_expert, n_used, x_pad, wg, wu)
    return pl.pallas_call(
        _moe_down_body,
        grid_spec=pltpu.PrefetchScalarGridSpec(
            num_scalar_prefetch=2,
            grid=(n_blk,),
            in_specs=[pl.BlockSpec((ROW_BLOCK, D_EXPERT), lambda i, be, nu: (i, 0)),
                      pl.BlockSpec((1, 1, D_EXPERT, D), lambda i, be, nu: (layer, be[i], 0, 0)),
                      pl.BlockSpec((ROW_BLOCK, 1), lambda i, be, nu: (i, 0))],
            out_specs=pl.BlockSpec((ROW_BLOCK, D), lambda i, be, nu: (i, 0)),
        ),
        out_shape=jax.ShapeDtypeStruct((A_pad, D), f32),
        compiler_params=pltpu.CompilerParams(dimension_semantics=("arbitrary",), vmem_limit_bytes=VMEM_LIMIT),
        name="moe_down",
    )(blk_expert, n_used, h, wd, w_pad.reshape(A_pad, 1))


def hier_moe(xb, router_grp, router_grp_b, router_exp, router_exp_b, w_gate, w_up, w_down, layer):
    M, D = xb.shape
    expert_ids, weights = moe_router(xb, router_grp, router_grp_b, router_exp, router_exp_b)
    A = M * TOP_K
    e_flat = expert_ids.reshape(A)
    onehot = (e_flat[:, None] == jnp.arange(N_EXPERTS)[None, :]).astype(jnp.int32)
    rank = jnp.take_along_axis(jnp.cumsum(onehot, axis=0), e_flat[:, None], axis=1)[:, 0] - 1
    counts = jnp.sum(onehot, axis=0)
    padded = (counts + ROW_BLOCK - 1) // ROW_BLOCK * ROW_BLOCK
    end_pad = jnp.cumsum(padded)
    dest = (end_pad - padded)[e_flat] + rank
    n_blk = -(-A // ROW_BLOCK) + N_EXPERTS
    A_pad = n_blk * ROW_BLOCK
    tok_pad = (jnp.arange(A_pad, dtype=jnp.int32) % M).at[dest].set(jnp.arange(A, dtype=jnp.int32) // TOP_K)
    w_pad = jnp.zeros((A_pad,), f32).at[dest].set(weights.reshape(A))
    blk_expert = jnp.minimum(jnp.searchsorted(end_pad, jnp.arange(n_blk) * ROW_BLOCK, side='right'),
                             N_EXPERTS - 1).astype(jnp.int32)
    n_used = (end_pad[-1:] // ROW_BLOCK).astype(jnp.int32)
    x_pad = xb[tok_pad]
    y_pad = moe_experts(x_pad, w_pad, blk_expert, n_used, w_gate, w_up, w_down, layer)
    rows = dest.reshape(M, TOP_K)
    return y_pad[rows[:, 0]], y_pad[rows[:, 1]]


def _rope_tables(S, theta, rot_dims, width):
    half = rot_dims // 2
    freqs = theta ** (-np.arange(half, dtype=np.float64) / half)
    ang = np.arange(S, dtype=np.float64)[:, None] * freqs[None, :]
    cos = np.ones((S, width), np.float64)
    sin_lo = np.zeros((S, width), np.float64)
    sin_hi = np.zeros((S, width), np.float64)
    cos[:, :half] = np.cos(ang)
    cos[:, half:rot_dims] = np.cos(ang)
    sin_lo[:, :half] = -np.sin(ang)
    sin_hi[:, half:rot_dims] = np.sin(ang)
    return [jnp.asarray(t, f32) for t in (cos, sin_lo, sin_hi)]


def _rope128(x, cos, sin_lo, sin_hi, half):
    return x * cos + pltpu.roll(x, LANES - half, axis=1) * sin_lo + pltpu.roll(x, half, axis=1) * sin_hi


def _nsa_prep_body(q_ref, kv_ref, cos_ref, sl_ref, sh_ref, qo_ref, kvo_ref, co_ref):
    cos, sl, sh = cos_ref[...], sl_ref[...], sh_ref[...]
    half = ROPE_DIMS // 2
    scale = HEAD_DIM ** -0.5
    for h in range(NSA_HEADS):
        c = slice(h * LANES, (h + 1) * LANES)
        qo_ref[:, c] = (_rope128(q_ref[:, c], cos, sl, sh, half) * scale).astype(bf16)
    for j in range(6):
        for g in range(NSA_KV_HEADS):
            c = slice((j * NSA_KV_HEADS + g) * LANES, (j * NSA_KV_HEADS + g + 1) * LANES)
            x = kv_ref[:, c]
            if j % 2 == 0:
                x = _rope128(x, cos, sl, sh, half)
            if j < 2:
                co_ref[:, c] = x.astype(bf16)
            else:
                kvo_ref[:, slice(c.start - 4 * LANES, c.stop - 4 * LANES)] = x.astype(bf16)


def nsa_prep(proj, S, tm=512):
    M = proj.shape[0]
    assert M % tm == 0 and S % tm == 0
    tabs = _rope_tables(S, ROPE_THETA, ROPE_DIMS, LANES)
    tspec = pl.BlockSpec((tm, LANES), lambda i: (i % (S // tm), 0))
    return pl.pallas_call(
        _nsa_prep_body,
        grid=(M // tm,),
        in_specs=[pl.BlockSpec((tm, D_GRP), lambda i: (i, OFF_NSA_Q // D_GRP)),
                  pl.BlockSpec((tm, 6 * NSA_KV_W), lambda i: (i, OFF_NSA_KV // (6 * NSA_KV_W))),
                  tspec, tspec, tspec],
        out_specs=[pl.BlockSpec((tm, D_GRP), lambda i: (i, 0)),
                   pl.BlockSpec((tm, 4 * NSA_KV_W), lambda i: (i, 0)),
                   pl.BlockSpec((tm, 2 * NSA_KV_W), lambda i: (i, 0))],
        out_shape=[jax.ShapeDtypeStruct((M, D_GRP), bf16), jax.ShapeDtypeStruct((M, 4 * NSA_KV_W), bf16),
                   jax.ShapeDtypeStruct((M, 2 * NSA_KV_W), bf16)],
        compiler_params=pltpu.CompilerParams(dimension_semantics=("parallel",), vmem_limit_bytes=VMEM_LIMIT),
        name="nsa_prep",
    )(proj, proj, *tabs)


def _nsa_compress_body(r_ref, w1_ref, w2_ref, pe_ref, o_ref):
    r = r_ref[0, 0, 0]
    half_w = CMP_STRIDE * HEAD_DIM
    a = jnp.dot(r, w1_ref[0, :half_w, :], preferred_element_type=f32)
    b = jnp.dot(r, w1_ref[0, half_w:, :], preferred_element_type=f32)
    n = r.shape[0]
    pe = jnp.dot(jnp.broadcast_to(pe_ref[0], (SUB, pe_ref.shape[-1])), w1_ref[0], preferred_element_type=f32)[0:1]
    pre = a + pltpu.roll(b, n - 1, axis=0) + pe
    hid = pre * jax.nn.sigmoid(pre)
    o_ref[0, 0, 0] = jnp.dot(hid.astype(bf16), w2_ref[0], preferred_element_type=f32).astype(bf16)


def nsa_compress(c_rot, w1s, w2s, pes, B, S):
    G, Dh = NSA_KV_HEADS, HEAD_DIM
    nc = S // CMP_STRIDE
    r = c_rot.reshape(B, nc, CMP_STRIDE, 2, G, Dh).transpose(3, 0, 4, 1, 2, 5).reshape(2, B, G, nc, CMP_STRIDE * Dh)
    w1 = jnp.stack(w1s).astype(bf16)
    w2 = jnp.stack(w2s).astype(bf16)
    pe = jnp.stack(pes).reshape(2, 1, CMP_LEN * Dh).astype(bf16)
    return pl.pallas_call(
        _nsa_compress_body,
        grid=(2, B, G),
        in_specs=[pl.BlockSpec((1, 1, 1, nc, CMP_STRIDE * Dh), lambda j, b, g: (j, b, g, 0, 0)),
                  pl.BlockSpec((1, CMP_LEN * Dh, Dh), lambda j, b, g: (j, 0, 0)),
                  pl.BlockSpec((1, Dh, Dh), lambda j, b, g: (j, 0, 0)),
                  pl.BlockSpec((1, 1, CMP_LEN * Dh), lambda j, b, g: (j, 0, 0))],
        out_specs=pl.BlockSpec((1, 1, 1, nc, Dh), lambda j, b, g: (j, b, g, 0, 0)),
        out_shape=jax.ShapeDtypeStruct((2, B, G, nc, Dh), bf16),
        compiler_params=pltpu.CompilerParams(dimension_semantics=("parallel", "parallel", "parallel"),
                                             vmem_limit_bytes=VMEM_LIMIT),
        name="nsa_compress",
    )(r, w1, w2, pe)


def _softmax_rows(s, mask):
    s = jnp.where(mask, s, -jnp.inf)
    m = jnp.max(s, axis=-1, keepdims=True)
    m = jnp.where(m == -jnp.inf, 0.0, m)
    e = jnp.exp(s - m)
    return e, jnp.maximum(jnp.sum(e, axis=-1, keepdims=True), jnp.finfo(f32).tiny)


def _nsa_attn_body(q_ref, cmp_ref, ks_ref, vs_ref, kw_ref, vw_ref, gate_ref, ovt_ref, o_ref, *, tq, tk, S):
    g = pl.program_id(1)
    qi = pl.program_id(2)
    R, Dh = NSA_GROUP, HEAD_DIM
    rows = R * tq
    n_blk = S // SEL_LEN
    n_cmp = S // CMP_STRIDE - 1
    q0 = qi * tq
    q4 = jnp.concatenate([q_ref[0, :, r * Dh:(r + 1) * Dh] for r in range(R)], axis=0)
    nt = (((1,), (1,)), ((), ()))

    def qpos(shape):
        return q0 + lax.broadcasted_iota(jnp.int32, shape, 0) % tq

    k_cmp, v_cmp = cmp_ref[0, 0, 0], cmp_ref[1, 0, 0]
    ncp = k_cmp.shape[0]
    s = lax.dot_general(q4, k_cmp, nt, preferred_element_type=f32)
    n_idx = lax.broadcasted_iota(jnp.int32, (rows, ncp), 1)
    mask = (n_idx * CMP_STRIDE + (CMP_LEN - 1) <= qpos((rows, ncp))) & (n_idx < n_cmp)
    e, den = _softmax_rows(s, mask)
    p = e / den
    o_cmp = jnp.dot(p.astype(bf16), v_cmp, preferred_element_type=f32)

    psum = p[0:tq]
    for r in range(1, R):
        psum = psum + p[r * tq:(r + 1) * tq]
    ph, plo = _split2(psum)
    imp = (lax.dot_general(ovt_ref[...], ph, nt, preferred_element_type=f32)
           + lax.dot_general(ovt_ref[...], plo, nt, preferred_element_type=f32))
    blk = lax.broadcasted_iota(jnp.int32, (n_blk, tq), 0)
    cur = (q0 + lax.broadcasted_iota(jnp.int32, (n_blk, tq), 1)) // SEL_LEN
    visible = blk <= cur
    forced = (blk == 0) | (visible & (blk > cur - N_LOCAL_FORCED))
    imp = jnp.where(forced, FORCED_SCORE, jnp.where(visible, imp, -1.0))
    sel_t = jnp.zeros((n_blk, tq), f32)
    blk_f = blk.astype(f32)
    for _ in range(min(N_SEL, n_blk)):
        m = jnp.max(imp, axis=0, keepdims=True)
        first = jnp.min(jnp.where(imp == m, blk_f, float(n_blk)), axis=0, keepdims=True)
        hit = blk_f == first
        sel_t = jnp.where(hit, 1.0, sel_t)
        imp = jnp.where(hit, -jnp.inf, imp)
    sel_q = jnp.concatenate([sel_t.T, jnp.ones((tq, LANES - n_blk), f32)], axis=1).astype(bf16)

    blocks_per_tile = tk // SEL_LEN

    big = float(2 ** 30)
    kb = sel_q.shape[1]

    def sel_step(kt, carry, causal):
        m_run, l_run, acc = carry
        k0 = pl.multiple_of(kt * tk, tk)
        kt_k = ks_ref[0, pl.ds(k0, tk), :]
        kt_v = vs_ref[0, pl.ds(k0, tk), :]
        s = lax.dot_general(q4, kt_k, nt, preferred_element_type=f32)
        row = lax.broadcasted_iota(jnp.int32, (kb, tk), 0)
        in_blk = row == kt * blocks_per_tile + lax.broadcasted_iota(jnp.int32, (kb, tk), 1) // SEL_LEN
        expand = jnp.where(in_blk, big, jnp.where(row == n_blk, -big, 0.0)).astype(bf16)
        bias = jnp.dot(sel_q, expand, preferred_element_type=f32)
        s = (s.reshape(R, tq, tk) + bias[None]).reshape(rows, tk)
        if causal:
            kpos = k0 + lax.broadcasted_iota(jnp.int32, (rows, tk), 1)
            s = jnp.where(kpos <= qpos((rows, tk)), s, -big)
        m_new = jnp.maximum(m_run, jnp.max(s, axis=-1, keepdims=True))
        alpha = jnp.exp(m_run - m_new)
        e = jnp.exp(s - m_new)
        l_new = alpha * l_run + jnp.sum(e, axis=-1, keepdims=True)
        acc = alpha * acc + jnp.dot(e.astype(bf16), kt_v, preferred_element_type=f32)
        return m_new, l_new, acc

    last_kt = (q0 + tq - 1) // tk
    init = (jnp.full((rows, 1), -jnp.inf, f32), jnp.zeros((rows, 1), f32), jnp.zeros((rows, Dh), f32))
    carry = lax.fori_loop(0, last_kt, functools.partial(sel_step, causal=False), init)
    _, l_sel, acc_sel = sel_step(last_kt, carry, causal=True)
    o_sel = acc_sel / l_sel

    span = WINDOW + tq
    w0 = pl.multiple_of(jnp.maximum(q0 - WINDOW, 0), tq)
    kwin = kw_ref[0, pl.ds(w0, span), :]
    vwin = vw_ref[0, pl.ds(w0, span), :]
    s = lax.dot_general(q4, kwin, nt, preferred_element_type=f32)
    kpos = w0 + lax.broadcasted_iota(jnp.int32, (rows, span), 1)
    qp = qpos((rows, span))
    e, den = _softmax_rows(s, (kpos <= qp) & (kpos > qp - WINDOW))
    o_win = jnp.dot(e.astype(bf16), vwin, preferred_element_type=f32) / den

    gate = jax.nn.sigmoid(gate_ref[0])
    for r in range(R):
        c0 = (g * R + r) * 3
        lane = lax.broadcasted_iota(jnp.int32, (tq, LANES), 1)
        gsel = lambda c: jnp.sum(jnp.where(lane == c0 + c, gate, 0.0), axis=-1, keepdims=True)
        rs = slice(r * tq, (r + 1) * tq)
        o_ref[0, :, r * Dh:(r + 1) * Dh] = (gsel(0) * o_cmp[rs] + gsel(1) * o_sel[rs]
                                            + gsel(2) * o_win[rs]).astype(o_ref.dtype)


def nsa_attention(q_rot, kv_rot, cmp_kv, proj3, B, S, tq=512, tk=512):
    G, R, Dh = NSA_KV_HEADS, NSA_GROUP, HEAD_DIM
    n_blk = S // SEL_LEN
    assert S % tk == 0 and tk % tq == 0 and S >= WINDOW + tq and n_blk % SUB == 0 and n_blk < LANES
    nc = S // CMP_STRIDE
    c0 = np.arange(nc - 1)[None, :] * CMP_STRIDE
    s0 = np.arange(n_blk)[:, None] * SEL_LEN
    ovt = np.zeros((n_blk, nc), np.float32)
    ovt[:, :nc - 1] = np.clip(np.minimum(c0 + CMP_LEN, s0 + SEL_LEN) - np.maximum(c0, s0), 0, None) / CMP_LEN
    kvspec = lambda j: pl.BlockSpec((1, S, Dh), lambda b, g, i: (b, 0, j * G + g))
    return pl.pallas_call(
        functools.partial(_nsa_attn_body, tq=tq, tk=tk, S=S),
        grid=(B, G, S // tq),
        in_specs=[pl.BlockSpec((1, tq, R * Dh), lambda b, g, i: (b, i, g)),
                  pl.BlockSpec((2, 1, 1, nc, Dh), lambda b, g, i: (0, b, g, 0, 0)),
                  kvspec(0), kvspec(1), kvspec(2), kvspec(3),
                  pl.BlockSpec((1, tq, LANES), lambda b, g, i: (b, i, OFF_SMALL // LANES)),
                  pl.BlockSpec((n_blk, nc), lambda b, g, i: (0, 0))],
        out_specs=pl.BlockSpec((1, tq, R * Dh), lambda b, g, i: (b, i, g)),
        out_shape=jax.ShapeDtypeStruct((B, S, D_GRP), bf16),
        compiler_params=pltpu.CompilerParams(dimension_semantics=("parallel", "parallel", "arbitrary"),
                                             vmem_limit_bytes=VMEM_LIMIT),
        name="nsa_attention",
    )(q_rot, cmp_kv, kv_rot, kv_rot, kv_rot, kv_rot, proj3, jnp.asarray(ovt, bf16))


def nsa_mixer_pallas(proj, ck_w1, ck_w2, ck_pe, cv_w1, cv_w2, cv_pe, B, S):
    q_rot, kv_rot, c_rot = nsa_prep(proj, S)
    cmp_kv = nsa_compress(c_rot, (ck_w1, cv_w1), (ck_w2, cv_w2), (ck_pe, cv_pe), B, S)
    return nsa_attention(q_rot.reshape(B, S, D_GRP), kv_rot.reshape(B, S, 4 * NSA_KV_W), cmp_kv,
                         proj.reshape(B, S, P_PAD), B, S)


def _retention_body(q_ref, k_ref, v_ref, g_ref, cos_ref, sin_ref, dec_ref, xi_ref, zeta_ref, gc_ref, o_ref, st_ref,
                    *, T, HS):
    @pl.when(pl.program_id(2) == 0)
    def _():
        st_ref[...] = jnp.zeros_like(st_ref)

    C, Dh = RET_CHUNK, HEAD_DIM
    nt = (((1,), (1,)), ((), ()))
    pre = {}
    for hh in range(HS):
        hc = slice(hh * Dh, (hh + 1) * Dh)
        for c in range(T // C):
            rs = slice(c * C, (c + 1) * C)
            cos, sin = cos_ref[rs, :], sin_ref[rs, :]
            rope = lambda x: x * cos + pltpu.roll(x, Dh // 2, axis=1) * sin
            q = rope(q_ref[0, rs, hc])
            k = rope(k_ref[0, rs, hc]) * Dh ** -0.5
            qb, vb = q.astype(bf16), v_ref[0, rs, hc].astype(bf16)
            inner = lax.dot_general(qb, k.astype(bf16), nt, preferred_element_type=f32) * dec_ref[hh]
            o_in = jnp.dot(inner.astype(bf16), vb, preferred_element_type=f32)
            kv = jnp.dot((k * zeta_ref[hh]).astype(bf16).T, vb, preferred_element_type=f32)
            pre[hh, c] = (qb, o_in, kv)
    states = [st_ref[hh] for hh in range(HS)]
    for c in range(T // C):
        rs = slice(c * C, (c + 1) * C)
        for hh in range(HS):
            hc = slice(hh * Dh, (hh + 1) * Dh)
            qb, o_in, kv = pre[hh, c]
            o = o_in + jnp.dot(qb, states[hh].astype(bf16), preferred_element_type=f32) * xi_ref[hh]
            states[hh] = states[hh] * gc_ref[hh, 0:1, :] + kv
            mu = jnp.mean(o, axis=-1, keepdims=True)
            d = o - mu
            o = d * lax.rsqrt(jnp.mean(d * d, axis=-1, keepdims=True) + NORM_EPS)
            gt = g_ref[0, rs, hc]
            o_ref[0, rs, hc] = (gt * jax.nn.sigmoid(gt) * o).astype(o_ref.dtype)
    for hh in range(HS):
        st_ref[hh] = states[hh]


def retention_mixer_pallas(proj3, B, S, T=512, HS=2):
    H, Dh, C = RET_HEADS, HEAD_DIM, RET_CHUNK
    W = HS * Dh
    assert S % T == 0 and T % C == 0 and H % HS == 0
    freqs = RET_THETA ** (-np.linspace(0.0, 1.0, Dh // 2))
    ang = np.arange(S, dtype=np.float64)[:, None] * freqs[None, :]
    cos = jnp.asarray(np.concatenate([np.cos(ang), np.cos(ang)], 1), f32)
    sin = jnp.asarray(np.concatenate([-np.sin(ang), np.sin(ang)], 1), f32)
    log_gamma = np.log1p(-(2.0 ** (-5.0 - np.arange(H, dtype=np.float64))))
    n = np.arange(C, dtype=np.float64)
    diff = n[:, None] - n[None, :]
    decay_in = np.where(diff >= 0, np.exp(log_gamma[:, None, None] * np.maximum(diff, 0.0)), 0.0)
    xi = np.exp(log_gamma[:, None] * (n + 1.0))[:, :, None] * np.ones((1, 1, Dh))
    zeta = np.exp(log_gamma[:, None] * (C - 1.0 - n))[:, :, None] * np.ones((1, 1, Dh))
    gamma_c = np.exp(log_gamma * C)[:, None, None] * np.ones((1, SUB, Dh))
    blk = lambda j: pl.BlockSpec((1, T, W), lambda b, h, t: (b, t, (OFF_RET + j * D_GRP) // W + h))
    tab = pl.BlockSpec((T, Dh), lambda b, h, t: (t, 0))
    per_head = lambda r: pl.BlockSpec((HS, r, Dh), lambda b, h, t: (h, 0, 0))
    return pl.pallas_call(
        functools.partial(_retention_body, T=T, HS=HS),
        grid=(B, H // HS, S // T),
        in_specs=[blk(0), blk(1), blk(2), blk(3), tab, tab, per_head(C), per_head(C), per_head(C), per_head(SUB)],
        out_specs=pl.BlockSpec((1, T, W), lambda b, h, t: (b, t, h)),
        out_shape=jax.ShapeDtypeStruct((B, S, D_GRP), bf16),
        scratch_shapes=[pltpu.VMEM((HS, Dh, Dh), f32)],
        compiler_params=pltpu.CompilerParams(dimension_semantics=("parallel", "parallel", "arbitrary"),
                                             vmem_limit_bytes=VMEM_LIMIT),
        name="retention",
    )(proj3, proj3, proj3, proj3, cos, sin, jnp.asarray(decay_in, f32), jnp.asarray(xi, f32),
      jnp.asarray(zeta, f32), jnp.asarray(gamma_c, f32))


def _dot3(a, b):
    ah, al = _split2(a)
    bh, bl = _split2(b)
    return (jnp.dot(ah, bh, preferred_element_type=f32) + jnp.dot(ah, bl, preferred_element_type=f32)
            + jnp.dot(al, bh, preferred_element_type=f32))


def _deltanet_body(q_ref, k_ref, v_ref, z_ref, qp_ref, kp_ref, vp_ref, ab_ref, cw_ref, alog_ref, dtb_ref, ng_ref,
                   o_ref, st_ref, *, T, HS):
    hb = pl.program_id(1)
    first = pl.program_id(2) == 0

    @pl.when(first)
    def _():
        st_ref[...] = jnp.zeros_like(st_ref)

    C, Dh = DN_CHUNK, HEAD_DIM
    nt = (((1,), (1,)), ((), ()))
    n_chunks = T // C
    ab = ab_ref[0]
    lane = lax.broadcasted_iota(jnp.int32, ab.shape, 1)

    def conv_silu(x_ref, prev_ref, j, hc):
        prev = jnp.where(first, 0.0, prev_ref[0, :, hc])
        xcat = jnp.concatenate([prev, x_ref[0, :, hc]], axis=0)
        w = cw_ref[:, j, hc]
        y = sum(w[i:i + 1, :] * xcat[SUB - (CONV_WIDTH - 1) + i:SUB - (CONV_WIDTH - 1) + i + T, :]
                for i in range(CONV_WIDTH))
        return y * jax.nn.sigmoid(y)

    def l2n(x):
        return x * lax.rsqrt(jnp.sum(x * x, axis=-1, keepdims=True) + L2NORM_EPS)

    units = [(hh, slice(c * C, (c + 1) * C)) for hh in range(HS) for c in range(n_chunks)]
    qs, ks_, vs_, betas, g_rows = [], [], [], [], []
    for hh in range(HS):
        hc = slice(hh * Dh, (hh + 1) * Dh)
        head = hb * HS + hh
        q_all = l2n(conv_silu(q_ref, qp_ref, 0, hc)) * Dh ** -0.5
        k_all = l2n(conv_silu(k_ref, kp_ref, 1, hc))
        v_all = conv_silu(v_ref, vp_ref, 2, hc)
        a_col = jnp.sum(jnp.where(lane == SMALL_DN_A + head, ab, 0.0), axis=-1, keepdims=True)
        b_col = jnp.sum(jnp.where(lane == SMALL_DN_A + DN_HEADS + head, ab, 0.0), axis=-1, keepdims=True)
        beta_all = jax.nn.sigmoid(b_col)
        g_all = -jnp.exp(alog_ref[hh, 0:1, :]) * jax.nn.softplus(a_col + dtb_ref[hh, 0:1, :])
        for c in range(n_chunks):
            rs = slice(c * C, (c + 1) * C)
            qs.append(q_all[rs]); ks_.append(k_all[rs]); vs_.append(v_all[rs])
            betas.append(beta_all[rs]); g_rows.append(g_all[rs])

    ri = lax.broadcasted_iota(jnp.int32, (C, C), 0)
    ci = lax.broadcasted_iota(jnp.int32, (C, C), 1)
    tril, strict = ri >= ci, ri > ci
    ltri = tril.astype(bf16)
    eye = (ri == ci).astype(f32)
    gcs = []
    for g_rows_u in g_rows:
        gh, gm, gl = _split3(g_rows_u)
        gcs.append(jnp.dot(ltri, gh, preferred_element_type=f32) + jnp.dot(ltri, gm, preferred_element_type=f32)
                   + jnp.dot(ltri, gl, preferred_element_type=f32))
    decays = [jnp.exp(jnp.where(tril, gc[:, :C] - gc[:, :C].T, -jnp.inf)) for gc in gcs]
    kbs = [k.astype(bf16) for k in ks_]
    kbetas = [k * b for k, b in zip(ks_, betas)]
    ms = [jnp.where(strict, lax.dot_general(kbeta.astype(bf16), kb, nt, preferred_element_type=f32) * d, 0.0)
          for kbeta, kb, d in zip(kbetas, kbs, decays)]
    attns = [jnp.where(tril, lax.dot_general(q.astype(bf16), kb, nt, preferred_element_type=f32) * d, 0.0).astype(bf16)
             for q, kb, d in zip(qs, kbs, decays)]
    t_invs, pws = [eye - m for m in ms], ms
    for _ in range(5):
        pws = [_dot3(pw, pw) for pw in pws]
        t_invs = [t + _dot3(t, pw) for t, pw in zip(t_invs, pws)]
    egcs = [jnp.exp(gc) for gc in gcs]
    tbs = [t.astype(bf16) for t in t_invs]
    us = [jnp.dot(tb, (v * b).astype(bf16), preferred_element_type=f32) for tb, v, b in zip(tbs, vs_, betas)]
    ws = [jnp.dot(tb, (kbeta * egc).astype(bf16), preferred_element_type=f32).astype(bf16)
          for tb, kbeta, egc in zip(tbs, kbetas, egcs)]
    states = [st_ref[hh] for hh in range(HS)]
    for c in range(n_chunks):
        for hh in range(HS):
            u, rs = hh * n_chunks + c, slice(c * C, (c + 1) * C)
            g_last = gcs[u][C - 1:C, :]
            q_dec = (qs[u] * egcs[u]).astype(bf16)
            k_dec = (ks_[u] * jnp.exp(g_last - gcs[u])).astype(bf16)
            sb = states[hh].astype(bf16)
            v_new = us[u] - jnp.dot(ws[u], sb, preferred_element_type=f32)
            vnb = v_new.astype(bf16)
            o = jnp.dot(q_dec, sb, preferred_element_type=f32) + jnp.dot(attns[u], vnb, preferred_element_type=f32)
            states[hh] = states[hh] * jnp.exp(g_last) + jnp.dot(k_dec.T, vnb, preferred_element_type=f32)
            o = o * lax.rsqrt(jnp.mean(o * o, axis=-1, keepdims=True) + NORM_EPS) * ng_ref[...]
            zt = z_ref[0, rs, hh * Dh:(hh + 1) * Dh]
            o_ref[0, rs, hh * Dh:(hh + 1) * Dh] = (o * (zt * jax.nn.sigmoid(zt))).astype(o_ref.dtype)
    for hh in range(HS):
        st_ref[hh] = states[hh]


def deltanet_mixer_pallas(proj3, conv_w, a_log, dt_bias, norm_g, B, S, T=256, HS=4):
    H, Dh = DN_HEADS, HEAD_DIM
    W = HS * Dh
    assert S % T == 0 and T % DN_CHUNK == 0 and H % HS == 0
    blk = lambda j: pl.BlockSpec((1, T, W), lambda b, h, t: (b, t, (OFF_DN + j * D_GRP) // W + h))
    prev = lambda j: pl.BlockSpec((1, SUB, W), lambda b, h, t: (b, jnp.maximum(t * (T // SUB) - 1, 0),
                                                               (OFF_DN + j * D_GRP) // W + h))
    per_head = pl.BlockSpec((HS, SUB, Dh), lambda b, h, t: (h, 0, 0))
    bcast = lambda x: jnp.broadcast_to(x.astype(f32)[:, None, None], (H, SUB, Dh))
    return pl.pallas_call(
        functools.partial(_deltanet_body, T=T, HS=HS),
        grid=(B, H // HS, S // T),
        in_specs=[blk(0), blk(1), blk(2), blk(3), prev(0), prev(1), prev(2),
                  pl.BlockSpec((1, T, LANES), lambda b, h, t: (b, t, OFF_SMALL // LANES)),
                  pl.BlockSpec((CONV_WIDTH, 3, W), lambda b, h, t: (0, 0, h)),
                  per_head, per_head,
                  pl.BlockSpec((1, Dh), lambda b, h, t: (0, 0))],
        out_specs=pl.BlockSpec((1, T, W), lambda b, h, t: (b, t, h)),
        out_shape=jax.ShapeDtypeStruct((B, S, D_GRP), bf16),
        scratch_shapes=[pltpu.VMEM((HS, Dh, Dh), f32)],
        compiler_params=pltpu.CompilerParams(dimension_semantics=("parallel", "parallel", "arbitrary"),
                                             vmem_limit_bytes=VMEM_LIMIT),
        name="deltanet",
    )(proj3, proj3, proj3, proj3, proj3, proj3, proj3, proj3, conv_w.reshape(CONV_WIDTH, 3, D_GRP),
      bcast(a_log), bcast(dt_bias), norm_g.reshape(1, Dh))


def _w_in_pieces():
    n0, r0, d0, w0 = 0, NSA_COLS, NSA_COLS + RET_COLS, NSA_COLS + RET_COLS + DN_COLS
    n_gate = 3 * NSA_HEADS
    return [
        (OFF_RW, w0, 3 * D_GRP),
        (OFF_RET, r0, 4 * D_GRP),
        (OFF_DN, d0, 3 * D_GRP),
        (OFF_DN + 3 * D_GRP, d0 + 3 * D_GRP + 2 * DN_HEADS, D_GRP),
        (OFF_NSA_Q, n0, D_GRP + 6 * NSA_KV_W),
        (OFF_LORA, w0 + 3 * D_GRP, RW_LORA),
        (OFF_SMALL, n0 + D_GRP + 6 * NSA_KV_W, n_gate),
        (OFF_SMALL + SMALL_DN_A, d0 + 3 * D_GRP, 2 * DN_HEADS),
    ]


def _pack_w_in_body(w_ref, o_ref):
    cols = o_ref.shape[2]
    for off, width in ((OFF_LORA + RW_LORA, RW_LORA_PAD - RW_LORA), (OFF_SMALL, LANES)):
        o_ref[0, off:off + width, :] = jnp.zeros((width, cols), o_ref.dtype)
    for dst, src, width in _w_in_pieces():
        o_ref[0, dst:dst + width, :] = w_ref[0, src:src + width, :].astype(o_ref.dtype)


def pack_w_in_t(w_in, tc=256):
    L, K = w_in.shape[0], w_in.shape[1]
    w_t = jnp.swapaxes(w_in, 1, 2)
    return pl.pallas_call(
        _pack_w_in_body,
        grid=(L, K // tc),
        in_specs=[pl.BlockSpec((1, P_TOTAL, tc), lambda l, i: (l, 0, i))],
        out_specs=pl.BlockSpec((1, P_PAD, tc), lambda l, i: (l, 0, i)),
        out_shape=jax.ShapeDtypeStruct((L, P_PAD, K), bf16),
        compiler_params=pltpu.CompilerParams(dimension_semantics=("parallel", "parallel"),
                                             vmem_limit_bytes=VMEM_LIMIT),
        name="pack_w_in",
    )(w_t)


def hybrid_mixer(h, hn, B, S, layer, w_in_packed, w_out, ck_w1, ck_w2, ck_pe, cv_w1, cv_w2, cv_pe, dn_conv, dn_a_log,
                 dn_dt_bias, dn_norm, rw_mu, rw_w0, rw_w2, rw_a0, rw_a2, rw_g2, rw_kk, rw_ka, rw_rk, rw_ln_g, rw_ln_b):
    M = B * S
    proj = matmul_nt(hn, w_in_packed, layer, name="in_proj")
    p3 = proj.reshape(B, S, P_PAD)
    o_nsa = nsa_mixer_pallas(proj, ck_w1, ck_w2, ck_pe, cv_w1, cv_w2, cv_pe, B, S)
    o_ret = retention_mixer_pallas(p3, B, S)
    o_dn = deltanet_mixer_pallas(p3, dn_conv, dn_a_log, dn_dt_bias, dn_norm, B, S)
    o_rw = rwkv7_mixer(proj, OFF_RW // (3 * D_GRP), OFF_LORA // RW_LORA_PAD, rw_mu, rw_w0, rw_w2, rw_a0, rw_a2,
                       rw_g2, rw_kk, rw_ka, rw_rk, rw_ln_g, rw_ln_b, B, S)
    parts = [o.reshape(M, D_GRP) for o in (o_nsa, o_ret, o_dn, o_rw)]
    return out_proj_residual(parts, w_out.astype(bf16), h)


def kernel(x, p, norm_mix, w_in, w_out, nsa_ck_w1, nsa_ck_w2, nsa_ck_pe, nsa_cv_w1, nsa_cv_w2, nsa_cv_pe, dn_conv, dn_a_log, dn_dt_bias, dn_norm, rw_mu, rw_w0, rw_w2, rw_a0, rw_a2, rw_g2, rw_kk, rw_ka, rw_rk, rw_ln_g, rw_ln_b, norm_moe, moe_router_grp, moe_router_grp_b, moe_router_exp, moe_router_exp_b, moe_w_gate, moe_w_up, moe_w_down, norm_ple, ple_w, ple_gate, norm_final):
    B, S, D = x.shape
    M = B * S
    h = x.reshape(M, D)
    w_in_packed = pack_w_in_t(w_in)
    for i in range(DEPTH):
        hn = rms_norm_rows(h, norm_mix[i], bf16)
        h = hybrid_mixer(h, hn, B, S, i, w_in_packed, w_out[i], nsa_ck_w1[i], nsa_ck_w2[i], nsa_ck_pe[i], nsa_cv_w1[i],
                         nsa_cv_w2[i], nsa_cv_pe[i], dn_conv[i], dn_a_log[i], dn_dt_bias[i], dn_norm[i],
                         rw_mu[i], rw_w0[i], rw_w2[i], rw_a0[i], rw_a2[i], rw_g2[i], rw_kk[i], rw_ka[i],
                         rw_rk[i], rw_ln_g[i], rw_ln_b[i])
        hn = rms_norm_rows(h, norm_moe[i], bf16)
        ya, yb = hier_moe(hn, moe_router_grp[i], moe_router_grp_b[i], moe_router_exp[i], moe_router_exp_b[i],
                          moe_w_gate, moe_w_up, moe_w_down, i)
        h, hn = add_rms_norm_rows(h, ya, yb, norm_ple[i], bf16)
        h = ple_residual(hn, ple_gate[i].astype(bf16), p[i].reshape(M, PLE_DIM), ple_w[i].astype(bf16), h)
    return rms_norm_rows(h, norm_final, f32).reshape(B, S, D)
```
